```python
import math
import jax, jax.numpy as jnp
from jax import lax
import numpy as np

D_MODEL = 1024
BATCH = 8
SEQ = 4096
DEPTH = 1

D_MIX = D_MODEL
MLA_HEADS = 8
MLA_NOPE = 64
MLA_ROPE = 32
MLA_V = 64
MLA_Q_RANK = 256
MLA_KV_RANK = 128
MLA_WIDTH = MLA_HEADS * MLA_V
HG_HEADS = 4
HG_EXPAND = 128
HG_WIDTH = D_MIX - MLA_WIDTH
HG_HEAD_V = HG_WIDTH // HG_HEADS
HG_FDIM = HG_HEADS * HG_EXPAND
HG_CHUNK = 64
Q_BLOCK = 128
ROPE_THETA = 10000.0
EPS = 1e-6
IN_SPLITS = (MLA_Q_RANK, MLA_KV_RANK, MLA_ROPE, MLA_WIDTH,
             HG_FDIM, HG_FDIM, HG_WIDTH, HG_WIDTH)
D_IN = MLA_Q_RANK + MLA_KV_RANK + MLA_ROPE + MLA_WIDTH + 2 * HG_FDIM + 2 * HG_WIDTH

kernel_name = "hybrid_mla_hgrn2_parallel_heads"


def rms_norm(x, g):
    xf = x.astype(jnp.float32)
    y = xf * lax.rsqrt(jnp.mean(xf * xf, axis=-1, keepdims=True) + EPS)
    return (y * g.astype(jnp.float32)).astype(x.dtype)


def rope(x, positions):
    half = x.shape[-1] // 2
    inv = ROPE_THETA ** (-jnp.arange(half, dtype=jnp.float32) / half)
    ang = positions.astype(jnp.float32)[..., None] * inv
    ang = ang.reshape(ang.shape[:2] + (1,) * (x.ndim - 3) + (half,))
    cos, sin = jnp.cos(ang), jnp.sin(ang)
    xf = x.astype(jnp.float32)
    x1, x2 = xf[..., :half], xf[..., half:]
    out = jnp.concatenate([x1 * cos - x2 * sin, x2 * cos + x1 * sin], axis=-1)
    return out.astype(x.dtype)


def mla_group(q_lat, kv_lat, k_rope, positions, q_a_norm_g, w_q_b, kv_a_norm_g, w_kv_b):
    B, S, _ = q_lat.shape
    q = rms_norm(q_lat, q_a_norm_g) @ w_q_b
    q = q.reshape(B, S, MLA_HEADS, MLA_NOPE + MLA_ROPE)
    q_nope, q_pe = q[..., :MLA_NOPE], rope(q[..., MLA_NOPE:], positions)
    kv = (rms_norm(kv_lat, kv_a_norm_g) @ w_kv_b).reshape(B, S, MLA_HEADS, MLA_NOPE + MLA_V)
    k_nope, v = kv[..., :MLA_NOPE], kv[..., MLA_NOPE:]
    k_pe = rope(k_rope, positions)
    scale = 1.0 / math.sqrt(MLA_NOPE + MLA_ROPE)

    nb = S // Q_BLOCK
    qn_b = q_nope.reshape(B, nb, Q_BLOCK, MLA_HEADS, MLA_NOPE).transpose(1, 0, 2, 3, 4)
    qp_b = q_pe.reshape(B, nb, Q_BLOCK, MLA_HEADS, MLA_ROPE).transpose(1, 0, 2, 3, 4)
    kpos = jnp.arange(S)

    def block(args):
        qn, qp, bi = args
        s = (jnp.einsum('bqhd,bkhd->bhqk', qn, k_nope)
             + jnp.einsum('bqhr,bkr->bhqk', qp, k_pe)).astype(jnp.float32) * scale
        qpos = bi * Q_BLOCK + jnp.arange(Q_BLOCK)
        mask = kpos[None, :] <= qpos[:, None]
        s = jnp.where(mask, s, -jnp.inf)
        p = jax.nn.softmax(s, axis=-1).astype(v.dtype)
        return jnp.einsum('bhqk,bkhd->bqhd', p, v)

    out = lax.map(block, (qn_b, qp_b, jnp.arange(nb)))
    return out.transpose(1, 0, 2, 3, 4).reshape(B, S, MLA_WIDTH)


def hgrn2_group(q_in, f_in, i_in, lb, norm_g):
    B, S, _ = q_in.shape
    nc = S // HG_CHUNK
    lbf = lb.astype(jnp.float32)
    f = lbf + (1.0 - lbf) * jax.nn.sigmoid(f_in.astype(jnp.float32))
    log_f = jnp.log(f)
    k = 1.0 - f
    q = jax.nn.silu(q_in.astype(jnp.float32))
    v = i_in.astype(jnp.float32)

    def chunks(t, d):
        return t.reshape(B, nc, HG_CHUNK, HG_HEADS, d).transpose(1, 0, 3, 2, 4)

    qc, kc, gc = chunks(q, HG_EXPAND), chunks(k, HG_EXPAND), chunks(log_f, HG_EXPAND)
    vc = chunks(v, HG_HEAD_V)
    causal = jnp.tril(jnp.ones((HG_CHUNK, HG_CHUNK), dtype=bool))

    def step(state, inp):
        qb, kb, vb, gb = inp
        G = jnp.cumsum(gb, axis=-2)
        diff = G[..., :, None, :] - G[..., None, :, :]
        decay = jnp.exp(jnp.where(causal[:, :, None], diff, -jnp.inf))
        A = jnp.einsum('bhtk,bhtsk,bhsk->bhts', qb, decay, kb)
        o = (jnp.einsum('bhts,bhsv->bhtv', A, vb)
             + jnp.einsum('bhtk,bhkv->bhtv', qb * jnp.exp(G), state))
        G_end = G[..., -1:, :]
        new_state = (state * jnp.exp(G_end)[..., 0, :, None]
                     + jnp.einsum('bhsk,bhsv->bhkv', kb * jnp.exp(G_end - G), vb))
        return new_state, o

    s0 = jnp.zeros((B, HG_HEADS, HG_EXPAND, HG_HEAD_V), jnp.float32)
    _, o = lax.scan(step, s0, (qc, kc, vc, gc))
    o = o.transpose(1, 0, 3, 2, 4).reshape(B, S, HG_HEADS, HG_HEAD_V)
    o = o * lax.rsqrt(jnp.mean(o * o, axis=-1, keepdims=True) + EPS)
    o = o * norm_g.astype(jnp.float32).reshape(HG_HEADS, HG_HEAD_V)
    return o.reshape(B, S, HG_WIDTH).astype(q_in.dtype)


def _fwd_setup_inputs(seed: int = 0) -> dict:
    key = jax.random.key(seed)
    ks = jax.random.split(key, 12)
    nrm = jax.random.normal
    x = nrm(ks[0], (BATCH, SEQ, D_MODEL), jnp.float32)
    start = jax.random.randint(ks[1], (BATCH, 1), 0, 1024, dtype=jnp.int32)
    positions = (start + jnp.arange(SEQ, dtype=jnp.int32)[None, :]).astype(jnp.int32)
    ln_g = 1.0 + 0.02 * nrm(ks[2], (DEPTH, D_MODEL), jnp.float32)
    w_in = nrm(ks[3], (DEPTH, D_MODEL, D_IN), jnp.float32) * D_MODEL ** -0.5
    q_a_norm_g = 1.0 + 0.02 * nrm(ks[4], (DEPTH, MLA_Q_RANK), jnp.float32)
    w_q_b = nrm(ks[5], (DEPTH, MLA_Q_RANK, MLA_HEADS * (MLA_NOPE + MLA_ROPE)), jnp.float32) * MLA_Q_RANK ** -0.5
    kv_a_norm_g = 1.0 + 0.02 * nrm(ks[6], (DEPTH, MLA_KV_RANK), jnp.float32)
    w_kv_b = nrm(ks[7], (DEPTH, MLA_KV_RANK, MLA_HEADS * (MLA_NOPE + MLA_V)), jnp.float32) * MLA_KV_RANK ** -0.5
    hg_lower_bounds = nrm(ks[8], (DEPTH + 1, HG_FDIM), jnp.float32)
    hg_norm_g = 1.0 + 0.02 * nrm(ks[9], (DEPTH, HG_WIDTH), jnp.float32)
    w_out = nrm(ks[10], (DEPTH, D_MIX, D_MODEL), jnp.float32) * D_MIX ** -0.5
    final_norm_g = 1.0 + 0.02 * nrm(ks[11], (D_MODEL,), jnp.float32)
    return {"x": x, "positions": positions, "ln_g": ln_g, "w_in": w_in,
            "q_a_norm_g": q_a_norm_g, "w_q_b": w_q_b, "kv_a_norm_g": kv_a_norm_g,
            "w_kv_b": w_kv_b, "hg_lower_bounds": hg_lower_bounds, "hg_norm_g": hg_norm_g,
            "w_out": w_out, "final_norm_g": final_norm_g}


def _fwd_reference(x, positions, ln_g, w_in, q_a_norm_g, w_q_b, kv_a_norm_g, w_kv_b,
              hg_lower_bounds, hg_norm_g, w_out, final_norm_g):
    lb_all = jnp.cumsum(jax.nn.softmax(hg_lower_bounds.astype(jnp.float32), axis=0), axis=0)
    split_pts = [int(p) for p in np.cumsum(IN_SPLITS)[:-1]]
    for l in range(DEPTH):
        h = rms_norm(x, ln_g[l])
        proj = h @ w_in[l]
        q_lat, kv_lat, k_rope, g_mla, hq, hf, hi, g_hg = jnp.split(proj, split_pts, axis=-1)
        y_mla = mla_group(q_lat, kv_lat, k_rope, positions,
                          q_a_norm_g[l], w_q_b[l], kv_a_norm_g[l], w_kv_b[l]) * jax.nn.silu(g_mla)
        y_hg = hgrn2_group(hq, hf, hi, lb_all[l], hg_norm_g[l]) * jax.nn.silu(g_hg)
        y = jnp.concatenate([y_mla, y_hg], axis=-1) @ w_out[l]
        x = x + y.astype(x.dtype)
    return rms_norm(x, final_norm_g)


import jax as _jax
import jax.numpy as _jnp

TWIN_FORMAT = 'train_step'
FWD_PARAMS = ['x', 'positions', 'ln_g', 'w_in', 'q_a_norm_g', 'w_q_b', 'kv_a_norm_g', 'w_kv_b', 'hg_lower_bounds', 'hg_norm_g', 'w_out', 'final_norm_g']
TWIN_WEIGHTS = ['ln_g', 'w_in', 'q_a_norm_g', 'w_q_b', 'kv_a_norm_g', 'w_kv_b', 'hg_lower_bounds', 'hg_norm_g', 'w_out', 'final_norm_g']
TWIN_DIFF_INPUT = 'x'
TWIN_INPUTS = ['x', 'positions', 'ln_g', 'w_in', 'q_a_norm_g', 'w_q_b', 'kv_a_norm_g', 'w_kv_b', 'hg_lower_bounds', 'hg_norm_g', 'w_out', 'final_norm_g', 'loss_target', 'm_ln_g', 'm_w_in', 'm_q_a_norm_g', 'm_w_q_b', 'm_kv_a_norm_g', 'm_w_kv_b', 'm_hg_lower_bounds', 'm_hg_norm_g', 'm_w_out', 'm_final_norm_g', 'v_ln_g', 'v_w_in', 'v_q_a_norm_g', 'v_w_q_b', 'v_kv_a_norm_g', 'v_w_kv_b', 'v_hg_lower_bounds', 'v_hg_norm_g', 'v_w_out', 'v_final_norm_g']
TWIN_OUTPUTS = ['loss', 'grad_x', 'grad_ln_g', 'grad_w_in', 'grad_q_a_norm_g', 'grad_w_q_b', 'grad_kv_a_norm_g', 'grad_w_kv_b', 'grad_hg_lower_bounds', 'grad_hg_norm_g', 'grad_w_out', 'grad_final_norm_g', 'delta_ln_g', 'delta_w_in', 'delta_q_a_norm_g', 'delta_w_q_b', 'delta_kv_a_norm_g', 'delta_w_kv_b', 'delta_hg_lower_bounds', 'delta_hg_norm_g', 'delta_w_out', 'delta_final_norm_g', 'new_m_ln_g', 'new_m_w_in', 'new_m_q_a_norm_g', 'new_m_w_q_b', 'new_m_kv_a_norm_g', 'new_m_w_kv_b', 'new_m_hg_lower_bounds', 'new_m_hg_norm_g', 'new_m_w_out', 'new_m_final_norm_g', 'new_v_ln_g', 'new_v_w_in', 'new_v_q_a_norm_g', 'new_v_w_q_b', 'new_v_kv_a_norm_g', 'new_v_w_kv_b', 'new_v_hg_lower_bounds', 'new_v_hg_norm_g', 'new_v_w_out', 'new_v_final_norm_g']
TWIN_LEAF_KINDS = {'loss': 'loss', 'grad_x': 'grad_x', 'grad_ln_g': 'grad_w', 'grad_w_in': 'grad_w', 'grad_q_a_norm_g': 'grad_w', 'grad_w_q_b': 'grad_w', 'grad_kv_a_norm_g': 'grad_w', 'grad_w_kv_b': 'grad_w', 'grad_hg_lower_bounds': 'grad_w', 'grad_hg_norm_g': 'grad_w', 'grad_w_out': 'grad_w', 'grad_final_norm_g': 'grad_w', 'delta_ln_g': 'delta_w', 'delta_w_in': 'delta_w', 'delta_q_a_norm_g': 'delta_w', 'delta_w_q_b': 'delta_w', 'delta_kv_a_norm_g': 'delta_w', 'delta_w_kv_b': 'delta_w', 'delta_hg_lower_bounds': 'delta_w', 'delta_hg_norm_g': 'delta_w', 'delta_w_out': 'delta_w', 'delta_final_norm_g': 'delta_w', 'new_m_ln_g': 'new_m', 'new_m_w_in': 'new_m', 'new_m_q_a_norm_g': 'new_m', 'new_m_w_q_b': 'new_m', 'new_m_kv_a_norm_g': 'new_m', 'new_m_w_kv_b': 'new_m', 'new_m_hg_lower_bounds': 'new_m', 'new_m_hg_norm_g': 'new_m', 'new_m_w_out': 'new_m', 'new_m_final_norm_g': 'new_m', 'new_v_ln_g': 'new_v', 'new_v_w_in': 'new_v', 'new_v_q_a_norm_g': 'new_v', 'new_v_w_q_b': 'new_v', 'new_v_kv_a_norm_g': 'new_v', 'new_v_w_kv_b': 'new_v', 'new_v_hg_lower_bounds': 'new_v', 'new_v_hg_norm_g': 'new_v', 'new_v_w_out': 'new_v', 'new_v_final_norm_g': 'new_v'}


def _forward(args):
    return _fwd_reference(*[args[k] for k in FWD_PARAMS])


def _output_shape():
    out = _jax.eval_shape(lambda: _forward(_fwd_setup_inputs(0)))
    return out.shape, out.dtype

N_MICROBATCH = 1
ADAM_LR = 0.001
ADAM_B1 = 0.9
ADAM_B2 = 0.999
ADAM_EPS = 1e-08
ADAM_WD = 0.01
ADAM_STEP = 10
PER_EXAMPLE_BATCH_AXIS = {'x': 0, 'positions': 0, 'loss_target': 0}
SHARED_INPUTS = []
_WEIGHT_DTYPES = {'ln_g': _jnp.float32, 'w_in': _jnp.float32, 'q_a_norm_g': _jnp.float32, 'w_q_b': _jnp.float32, 'kv_a_norm_g': _jnp.float32, 'w_kv_b': _jnp.float32, 'hg_lower_bounds': _jnp.float32, 'hg_norm_g': _jnp.float32, 'w_out': _jnp.float32, 'final_norm_g': _jnp.float32}
MOMENT_SCALE = {'ln_g': 1.010980e-01, 'w_in': 6.080300e-02, 'q_a_norm_g': 2.663520e-02, 'w_q_b': 1.516717e-02, 'kv_a_norm_g': 5.361978e-02, 'w_kv_b': 1.870641e-02, 'hg_lower_bounds': 8.075856e-03, 'hg_norm_g': 1.031174e-01, 'w_out': 7.042087e-02, 'final_norm_g': 3.198969e+01}


def _to_microbatches(a, axis):
    t = _jnp.moveaxis(a, axis, 0)
    t = t.reshape((N_MICROBATCH, t.shape[0] // N_MICROBATCH) + t.shape[1:])
    return _jnp.moveaxis(t, 1, axis + 1)


def setup_inputs(seed: int = 0) -> dict:
    inp = _fwd_setup_inputs(seed)
    key = _jax.random.fold_in(_jax.random.key(seed), 7919)
    shape, _ = _output_shape()
    out = dict(inp)
    out["loss_target"] = _jax.random.normal(_jax.random.fold_in(key, 0), shape, _jnp.float32)
    for i, name in enumerate(TWIN_WEIGHTS):
        w = inp[name].astype(_jnp.float32)
        if MOMENT_SCALE is None:
            s = _jnp.sqrt(_jnp.mean(_jnp.square(w)) + 1e-30)
        else:
            s = MOMENT_SCALE[name]
        km, kv = _jax.random.split(_jax.random.fold_in(key, i + 1))
        out[name] = w
        out["m_" + name] = s * _jax.random.normal(km, w.shape, _jnp.float32)
        out["v_" + name] = (s * s) * _jax.random.uniform(kv, w.shape, _jnp.float32, 0.5, 1.5)
    if N_MICROBATCH > 1:
        for name, axis in PER_EXAMPLE_BATCH_AXIS.items():
            out[name] = _to_microbatches(out[name], axis)
    return {'x': out['x'], 'positions': out['positions'], 'ln_g': out['ln_g'], 'w_in': out['w_in'], 'q_a_norm_g': out['q_a_norm_g'], 'w_q_b': out['w_q_b'], 'kv_a_norm_g': out['kv_a_norm_g'], 'w_kv_b': out['w_kv_b'], 'hg_lower_bounds': out['hg_lower_bounds'], 'hg_norm_g': out['hg_norm_g'], 'w_out': out['w_out'], 'final_norm_g': out['final_norm_g'], 'loss_target': out['loss_target'], 'm_ln_g': out['m_ln_g'], 'm_w_in': out['m_w_in'], 'm_q_a_norm_g': out['m_q_a_norm_g'], 'm_w_q_b': out['m_w_q_b'], 'm_kv_a_norm_g': out['m_kv_a_norm_g'], 'm_w_kv_b': out['m_w_kv_b'], 'm_hg_lower_bounds': out['m_hg_lower_bounds'], 'm_hg_norm_g': out['m_hg_norm_g'], 'm_w_out': out['m_w_out'], 'm_final_norm_g': out['m_final_norm_g'], 'v_ln_g': out['v_ln_g'], 'v_w_in': out['v_w_in'], 'v_q_a_norm_g': out['v_q_a_norm_g'], 'v_w_q_b': out['v_w_q_b'], 'v_kv_a_norm_g': out['v_kv_a_norm_g'], 'v_w_kv_b': out['v_w_kv_b'], 'v_hg_lower_bounds': out['v_hg_lower_bounds'], 'v_hg_norm_g': out['v_hg_norm_g'], 'v_w_out': out['v_w_out'], 'v_final_norm_g': out['v_final_norm_g']}


def _loss(weights, diff, rest, loss_target):
    with _jax.named_scope("forward"):
        args = {**rest, TWIN_DIFF_INPUT: diff, **{k: w.astype(_WEIGHT_DTYPES[k]) for k, w in weights.items()}}
        y = _forward(args)
    with _jax.named_scope("loss_head"):
        err = _jnp.square(y.astype(_jnp.float32) - loss_target)
        return 0.5 * _jnp.sum(_jnp.mean(err, axis=-1)) if err.ndim else 0.5 * err


def _adamw(w, g, m, v):
    m = ADAM_B1 * m + (1.0 - ADAM_B1) * g
    v = ADAM_B2 * v + (1.0 - ADAM_B2) * _jnp.square(g)
    m_hat = m / (1.0 - ADAM_B1 ** ADAM_STEP)
    v_hat = v / (1.0 - ADAM_B2 ** ADAM_STEP)
    delta = -ADAM_LR * (m_hat / (_jnp.sqrt(v_hat) + ADAM_EPS) + ADAM_WD * w)
    return delta, m, v


def reference(x, positions, ln_g, w_in, q_a_norm_g, w_q_b, kv_a_norm_g, w_kv_b, hg_lower_bounds, hg_norm_g, w_out, final_norm_g, loss_target, m_ln_g, m_w_in, m_q_a_norm_g, m_w_q_b, m_kv_a_norm_g, m_w_kv_b, m_hg_lower_bounds, m_hg_norm_g, m_w_out, m_final_norm_g, v_ln_g, v_w_in, v_q_a_norm_g, v_w_q_b, v_kv_a_norm_g, v_w_kv_b, v_hg_lower_bounds, v_hg_norm_g, v_w_out, v_final_norm_g):
    given = dict(x=x, positions=positions, ln_g=ln_g, w_in=w_in, q_a_norm_g=q_a_norm_g, w_q_b=w_q_b, kv_a_norm_g=kv_a_norm_g, w_kv_b=w_kv_b, hg_lower_bounds=hg_lower_bounds, hg_norm_g=hg_norm_g, w_out=w_out, final_norm_g=final_norm_g, loss_target=loss_target, m_ln_g=m_ln_g, m_w_in=m_w_in, m_q_a_norm_g=m_q_a_norm_g, m_w_q_b=m_w_q_b, m_kv_a_norm_g=m_kv_a_norm_g, m_w_kv_b=m_w_kv_b, m_hg_lower_bounds=m_hg_lower_bounds, m_hg_norm_g=m_hg_norm_g, m_w_out=m_w_out, m_final_norm_g=m_final_norm_g, v_ln_g=v_ln_g, v_w_in=v_w_in, v_q_a_norm_g=v_q_a_norm_g, v_w_q_b=v_w_q_b, v_kv_a_norm_g=v_kv_a_norm_g, v_w_kv_b=v_w_kv_b, v_hg_lower_bounds=v_hg_lower_bounds, v_hg_norm_g=v_hg_norm_g, v_w_out=v_w_out, v_final_norm_g=v_final_norm_g)
    weights = {n: given[n] for n in TWIN_WEIGHTS}
    shared = {n: given[n] for n in SHARED_INPUTS}
    per_example = {n: given[n] for n in ['x', 'positions']}
    grad_fn = _jax.value_and_grad(_loss, argnums=(0, 1))

    def one_microbatch(ex, loss_target):
        ex = dict(ex)
        diff = ex.pop(TWIN_DIFF_INPUT)
        return grad_fn(weights, diff, {**shared, **ex}, loss_target)

    if N_MICROBATCH == 1:
        loss, (grad_w, grad_x) = one_microbatch(per_example, given["loss_target"])
    else:
        def body(carry, xs):
            loss_sum, grad_sum = carry
            l_k, (gw_k, gx_k) = one_microbatch(xs[0], xs[1])
            with _jax.named_scope("update"):
                return (loss_sum + l_k, _jax.tree.map(_jnp.add, grad_sum, gw_k)), gx_k

        init = (_jnp.zeros((), _jnp.float32), _jax.tree.map(_jnp.zeros_like, weights))
        (loss, grad_w), grad_x = _jax.lax.scan(body, init, (per_example, given["loss_target"]))
    with _jax.named_scope("update"):
        delta_w, new_m, new_v = {}, {}, {}
        for n in TWIN_WEIGHTS:
            delta_w[n], new_m[n], new_v[n] = _adamw(weights[n], grad_w[n], given["m_" + n], given["v_" + n])
    return (loss, grad_x, *[grad_w[n] for n in TWIN_WEIGHTS], *[delta_w[n] for n in TWIN_WEIGHTS],
            *[new_m[n] for n in TWIN_WEIGHTS], *[new_v[n] for n in TWIN_WEIGHTS])
```

```python
import math

import numpy as np
import jax
import jax.numpy as jnp
from jax import lax
from jax.experimental import pallas as pl
from jax.experimental.pallas import tpu as pltpu

F32 = jnp.float32
BF16 = jnp.bfloat16

D_MODEL = 1024
N_HEADS = 8
NOPE = 64
ROPE = 32
HALF_ROPE = ROPE // 2
V_DIM = 64
Q_RANK = 256
KV_RANK = 128
MLA_WIDTH = N_HEADS * V_DIM
HG_HEADS = 4
HG_DIM = 128
HG_WIDTH = HG_HEADS * HG_DIM
CHUNK = 64
SUB = 16
D_IN = 2976
D_PERM = 3072
ROPE_THETA = 10000.0
EPS = 1e-6
N_DEV = 8
LANES = 128
HEAD_LANES = 128

P_GM, P_HQ, P_HF, P_HI, P_GH, P_QL, P_KVL, P_KR = 0, 512, 1024, 1536, 2048, 2560, 2816, 2944
ROPE_LO = NOPE
SCALE = 1.0 / math.sqrt(NOPE + ROPE)

ADAM_LR = 0.001
ADAM_B1 = 0.9
ADAM_B2 = 0.999
ADAM_EPS = 1e-08
ADAM_WD = 0.01
ADAM_STEP = 10

VMEM_LIMIT = 56 * 1024 * 1024
MESH = pl.DeviceIdType.MESH

NT = (((1,), (1,)), ((), ()))
TN = (((0,), (0,)), ((), ()))


def _params(n_grid=0, **kw):
    sem = ("arbitrary",) * n_grid if n_grid else None
    return pltpu.CompilerParams(dimension_semantics=sem, vmem_limit_bytes=VMEM_LIMIT, **kw)


def _dot(a, b):
    return jnp.dot(a, b, preferred_element_type=F32)


def _dot_nt(a, b):
    return lax.dot_general(a, b, NT, preferred_element_type=F32)


def _dot_tn(a, b):
    return lax.dot_general(a, b, TN, preferred_element_type=F32)


def _sigmoid(x):
    return 1.0 / (1.0 + jnp.exp(-x))


def _rope_fwd(x, c, s1, s2):
    return x * c + pltpu.roll(x, LANES - HALF_ROPE, 1) * s1 + pltpu.roll(x, HALF_ROPE, 1) * s2


def _rope_bwd(dy, c, s1, s2):
    return dy * c - pltpu.roll(dy, LANES - HALF_ROPE, 1) * s1 - pltpu.roll(dy, HALF_ROPE, 1) * s2


def _full(shape):
    n = len(shape)
    return pl.BlockSpec(shape, lambda *_: (0,) * n)


def _rope_tables(pos_f, tm):
    T = pos_f.shape[0]
    inv = (np.float32(ROPE_THETA) ** (-np.arange(HALF_ROPE, dtype=np.float32) / np.float32(HALF_ROPE))).astype(np.float32)
    inv1 = np.zeros((1, LANES), np.float32)
    inv2 = np.zeros((1, LANES), np.float32)
    inv1[0, ROPE_LO:ROPE_LO + HALF_ROPE] = inv
    inv2[0, ROPE_LO + HALF_ROPE:ROPE_LO + ROPE] = inv

    def body(pos_ref, inv1_ref, inv2_ref, c_ref, s1_ref, s2_ref):
        pos = pos_ref[...]
        a1 = pos * inv1_ref[...]
        a2 = pos * inv2_ref[...]
        c_ref[...] = jnp.cos(a1 + a2)
        s1_ref[...] = -jnp.sin(a1)
        s2_ref[...] = jnp.sin(a2)

    tab = jax.ShapeDtypeStruct((T, LANES), F32)
    return pl.pallas_call(
        body, name="rope_tables", grid=(T // tm,),
        in_specs=[pl.BlockSpec((tm, 1), lambda i: (i, 0)), _full((1, LANES)), _full((1, LANES))],
        out_specs=[pl.BlockSpec((tm, LANES), lambda i: (i, 0))] * 3,
        out_shape=[tab, tab, tab], compiler_params=_params(1),
    )(pos_f, jnp.asarray(inv1), jnp.asarray(inv2))


def _all_gather_weights(shards):
    n = len(shards)

    def body(*refs):
        ins, outs = refs[:n], refs[n:2 * n]
        send_sems, recv_sems = refs[2 * n], refs[2 * n + 1]
        x, y, c = lax.axis_index("x"), lax.axis_index("y"), lax.axis_index("c")
        me, sibling = (x, y, c), (x, y, 1 - c)
        chips = [(1 - x, y), (x, 1 - y), (1 - x, 1 - y)]

        def idx(d):
            return 4 * d[0] + 2 * d[1] + d[2]

        def copy(a, k, block, to):
            rows = outs[a].at[idx(block)]
            return pltpu.make_async_remote_copy(src_ref=rows, dst_ref=rows, send_sem=send_sems.at[a, k],
                                                recv_sem=recv_sems.at[a, k], device_id=to, device_id_type=MESH)

        for a in range(n):
            outs[a][idx(me)] = ins[a][...].astype(BF16)
        first = []
        for a in range(n):
            first.append(copy(a, 0, me, sibling))
            first += [copy(a, 1 + j, me, (*chip, c)) for j, chip in enumerate(chips)]
        for cp in first:
            cp.start()
        passed = []
        for j, chip in enumerate(chips):
            for a in range(n):
                copy(a, 1 + j, (*chip, c), me).wait_recv()
                cp = copy(a, 4 + j, (*chip, c), sibling)
                cp.start()
                passed.append(cp)
        for a in range(n):
            copy(a, 0, sibling, me).wait_recv()
            for j, chip in enumerate(chips):
                copy(a, 4 + j, (*chip, 1 - c), me).wait_recv()
        for cp in first + passed:
            cp.wait_send()

    vm = pl.BlockSpec(memory_space=pltpu.VMEM)
    return pl.pallas_call(
        body, name="all_gather_weights",
        in_specs=[vm] * n, out_specs=[vm] * n,
        out_shape=[jax.ShapeDtypeStruct((N_DEV,) + s.shape, BF16) for s in shards],
        scratch_shapes=[pltpu.SemaphoreType.DMA((n, 7)), pltpu.SemaphoreType.DMA((n, 7))],
        compiler_params=pltpu.CompilerParams(vmem_limit_bytes=VMEM_LIMIT),
    )(*shards)


def _fwd_in(x, ln_g, w_in_p, q_g, w_q_p, kv_g, w_kv_p, c_t, s1_t, s2_t, tm):
    T = x.shape[0]

    def body(x_ref, lng_ref, win_ref, qg_ref, wq_ref, kvg_ref, wkv_ref, c_ref, s1_ref, s2_ref,
             proj_ref, h_ref, qn_ref, kvn_ref, q_ref, k_ref, v_ref):
        xv = x_ref[...]
        r = lax.rsqrt(jnp.mean(xv * xv, axis=-1, keepdims=True) + EPS)
        h = (xv * r * lng_ref[...]).astype(BF16)
        h_ref[...] = h
        proj = _dot(h, win_ref[...])
        proj_ref[...] = proj
        c, s1, s2 = c_ref[...], s1_ref[...], s2_ref[...]

        ql = proj[:, P_QL:P_QL + Q_RANK]
        rq = lax.rsqrt(jnp.mean(ql * ql, axis=-1, keepdims=True) + EPS)
        qn = (ql * rq * qg_ref[...]).astype(BF16)
        qn_ref[...] = qn
        q = _dot(qn, wq_ref[...])
        for hd in range(N_HEADS):
            sl = slice(hd * HEAD_LANES, (hd + 1) * HEAD_LANES)
            q_ref[:, sl] = _rope_fwd(q[:, sl], c, s1, s2).astype(BF16)

        kvl = proj[:, P_KVL:P_KVL + KV_RANK]
        rk = lax.rsqrt(jnp.mean(kvl * kvl, axis=-1, keepdims=True) + EPS)
        kvn = (kvl * rk * kvg_ref[...]).astype(BF16)
        kvn_ref[...] = kvn
        kv = _dot(kvn, wkv_ref[...])
        kpe = _rope_fwd(proj[:, P_KR:P_KR + LANES], c, s1, s2)
        for hd in range(N_HEADS):
            sl = slice(hd * HEAD_LANES, (hd + 1) * HEAD_LANES)
            k_ref[:, sl] = (kv[:, sl] + kpe).astype(BF16)
        v_ref[...] = kv[:, N_HEADS * HEAD_LANES:].astype(BF16)

    def row(w):
        return pl.BlockSpec((tm, w), lambda i: (i, 0))

    outs = [(D_PERM, F32), (D_MODEL, BF16), (Q_RANK, BF16), (KV_RANK, BF16),
            (N_HEADS * HEAD_LANES, BF16), (N_HEADS * HEAD_LANES, BF16), (MLA_WIDTH, BF16)]
    return pl.pallas_call(
        body, name="fwd_in", grid=(T // tm,),
        in_specs=[row(D_MODEL), _full((1, D_MODEL)), _full((D_MODEL, D_PERM)), _full((1, Q_RANK)),
                  _full((Q_RANK, N_HEADS * HEAD_LANES)), _full((1, KV_RANK)),
                  _full((KV_RANK, N_HEADS * HEAD_LANES + MLA_WIDTH)), row(LANES), row(LANES), row(LANES)],
        out_specs=[row(w) for w, _ in outs],
        out_shape=[jax.ShapeDtypeStruct((T, w), dt) for w, dt in outs],
        compiler_params=_params(1),
    )(x, ln_g, w_in_p, q_g, w_q_p, kv_g, w_kv_p, c_t, s1_t, s2_t)


def _head_mask(tq):
    return lax.broadcasted_iota(jnp.int32, (tq, LANES), 1) < V_DIM


def _attn_fwd(q, k, v, tq):
    T = q.shape[0]
    nq = T // tq

    def body(q_ref, k_ref, v_ref, o_ref, lse_ref):
        i = pl.program_id(1)
        row = i * tq + lax.broadcasted_iota(jnp.int32, (tq, tq), 0)
        col0 = lax.broadcasted_iota(jnp.int32, (tq, tq), 1)
        res = []
        for hh in range(2):
            sl = slice(hh * HEAD_LANES, (hh + 1) * HEAD_LANES)
            qh = q_ref[:, sl]

            def step(j, carry, sl=sl, qh=qh):
                m, l, acc = carry
                start = pl.multiple_of(j * tq, tq)
                kj = k_ref[pl.ds(start, tq), sl]
                s = _dot_nt(qh, kj) * SCALE
                s = jnp.where(col0 + j * tq <= row, s, -jnp.inf)
                m_new = jnp.maximum(m, jnp.max(s, axis=-1, keepdims=True))
                alpha = jnp.exp(m - m_new)
                p = jnp.exp(s - m_new)
                l = alpha * l + jnp.sum(p, axis=-1, keepdims=True)
                acc = alpha * acc + _dot(p.astype(BF16), v_ref[pl.ds(start, tq), :])
                return m_new, l, acc

            m0 = jnp.full((tq, 1), -1e30, F32)
            m, l, acc = lax.fori_loop(0, i + 1, step, (m0, jnp.zeros((tq, 1), F32), jnp.zeros((tq, LANES), F32)))
            res.append(acc / l)
            lse_ref[hh] = m + jnp.log(l)
        o_ref[...] = jnp.where(_head_mask(tq), res[0], res[1])

    return pl.pallas_call(
        body, name="attn_fwd", grid=(N_HEADS // 2, nq),
        in_specs=[pl.BlockSpec((tq, 2 * HEAD_LANES), lambda p, i: (i, p)),
                  pl.BlockSpec((T, 2 * HEAD_LANES), lambda p, i: (0, p)),
                  pl.BlockSpec((T, LANES), lambda p, i: (0, p))],
        out_specs=[pl.BlockSpec((tq, LANES), lambda p, i: (i, p)),
                   pl.BlockSpec((2, tq, 1), lambda p, i: (p, i, 0))],
        out_shape=[jax.ShapeDtypeStruct((T, MLA_WIDTH), F32), jax.ShapeDtypeStruct((N_HEADS, T, 1), F32)],
        compiler_params=_params(2),
    )(q, k, v)


def _attn_bwd(q, k, v, o, do, lse, tq):
    T = q.shape[0]
    nq = T // tq

    def body(q_ref, k_ref, v_ref, o_ref, do_ref, lse_ref, dq_ref, dk_ref, dv_ref):
        j = pl.program_id(1)

        @pl.when(j == 0)
        def _():
            dq_ref[...] = jnp.zeros_like(dq_ref)

        col = j * tq + lax.broadcasted_iota(jnp.int32, (tq, tq), 1)
        row0 = lax.broadcasted_iota(jnp.int32, (tq, tq), 0)
        mask0 = _head_mask(tq)
        vj = v_ref[...]
        dv_acc = jnp.zeros((tq, LANES), F32)
        for hh in range(2):
            sl = slice(hh * HEAD_LANES, (hh + 1) * HEAD_LANES)
            kh = k_ref[:, sl]
            hmask = mask0 if hh == 0 else jnp.logical_not(mask0)

            def step(i, carry, sl=sl, kh=kh, hmask=hmask, hh=hh):
                dk_acc, dv_acc = carry
                start = pl.multiple_of(i * tq, tq)
                rows = pl.ds(start, tq)
                qi = q_ref[rows, sl]
                dom = jnp.where(hmask, do_ref[rows, :], 0.0)
                dsum = jnp.sum(dom * o_ref[rows, :], axis=-1, keepdims=True)
                s = _dot_nt(qi, kh) * SCALE
                s = jnp.where(col <= row0 + i * tq, s, -jnp.inf)
                p = jnp.exp(s - lse_ref[hh, rows, :])
                dob = dom.astype(BF16)
                dv_acc = dv_acc + _dot_tn(p.astype(BF16), dob)
                dp = _dot_nt(dob, vj)
                ds = (p * (dp - dsum) * SCALE).astype(BF16)
                dk_acc = dk_acc + _dot_tn(ds, qi)
                dq_ref[rows, sl] += _dot(ds, kh)
                return dk_acc, dv_acc

            dk_acc, dv_acc = lax.fori_loop(j, nq, step, (jnp.zeros((tq, HEAD_LANES), F32), dv_acc))
            dk_ref[:, sl] = dk_acc
        dv_ref[...] = dv_acc

    return pl.pallas_call(
        body, name="attn_bwd", grid=(N_HEADS // 2, nq),
        in_specs=[pl.BlockSpec((T, 2 * HEAD_LANES), lambda p, j: (0, p)),
                  pl.BlockSpec((tq, 2 * HEAD_LANES), lambda p, j: (j, p)),
                  pl.BlockSpec((tq, LANES), lambda p, j: (j, p)),
                  pl.BlockSpec((T, LANES), lambda p, j: (0, p)),
                  pl.BlockSpec((T, LANES), lambda p, j: (0, p)),
                  pl.BlockSpec((2, T, 1), lambda p, j: (p, 0, 0))],
        out_specs=[pl.BlockSpec((T, 2 * HEAD_LANES), lambda p, j: (0, p)),
                   pl.BlockSpec((tq, 2 * HEAD_LANES), lambda p, j: (j, p)),
                   pl.BlockSpec((tq, LANES), lambda p, j: (j, p))],
        out_shape=[jax.ShapeDtypeStruct((T, N_HEADS * HEAD_LANES), F32),
                   jax.ShapeDtypeStruct((T, N_HEADS * HEAD_LANES), F32),
                   jax.ShapeDtypeStruct((T, MLA_WIDTH), F32)],
        compiler_params=_params(2),
    )(q, k, v, o, do, lse)


def _lower_bound(lbp):
    a, b = lbp[0:1, :], lbp[1:2, :]
    mx = jnp.maximum(a, b)
    ea, eb = jnp.exp(a - mx), jnp.exp(b - mx)
    return ea / (ea + eb)


def _tri(lower):
    r = lax.broadcasted_iota(jnp.int32, (CHUNK, CHUNK), 0)
    c = lax.broadcasted_iota(jnp.int32, (CHUNK, CHUNK), 1)
    return (c <= r) if lower else (c >= r)


def _hg_gates(hq, hf, lb):
    sq = _sigmoid(hq)
    sf = _sigmoid(hf)
    f = lb + (1.0 - lb) * sf
    g = jnp.log(f)
    gcum = jnp.dot(_tri(True).astype(F32), g, precision=lax.Precision.HIGHEST, preferred_element_type=F32)
    return sq, sf, f, hq * sq, 1.0 - f, gcum


def _hg_blocks(q, kk, gcum):
    rowi = lax.broadcasted_iota(jnp.int32, (CHUNK, HG_DIM), 0)
    out = []
    for blk in range(CHUNK // SUB):
        lo, hi = blk * SUB, (blk + 1) * SUB
        gb = gcum[lo - 1:lo, :] if blk else jnp.zeros((1, HG_DIM), F32)
        eq = jnp.exp(gcum[lo:hi, :] - gb)
        ek = jnp.exp(jnp.where(rowi < hi, gb - gcum, 0.0))
        out.append((eq, ek, (q[lo:hi, :] * eq).astype(BF16), (kk * ek).astype(BF16)))
    return out


def _hg_scores(blocks):
    a = jnp.concatenate([_dot_nt(qb, kb) for _, _, qb, kb in blocks], axis=0)
    return jnp.where(_tri(True), a, 0.0)


def _hgrn_fwd(proj, lbp):
    T = proj.shape[0]
    nc = T // CHUNK

    def body(hq_ref, hf_ref, hi_ref, lbp_ref, o_ref, st_ref, state):
        @pl.when(pl.program_id(0) == 0)
        def _():
            state[...] = jnp.zeros_like(state)

        lb_all = _lower_bound(lbp_ref[...])
        for hd in range(HG_HEADS):
            sl = slice(hd * HG_DIM, (hd + 1) * HG_DIM)
            _, _, _, q, kk, gcum = _hg_gates(hq_ref[:, sl], hf_ref[:, sl], lb_all[:, sl])
            vb = hi_ref[:, sl].astype(BF16)
            st = state[hd]
            st_ref[0, hd] = st
            a = _hg_scores(_hg_blocks(q, kk, gcum))
            gend = gcum[CHUNK - 1:CHUNK, :]
            o = _dot(a.astype(BF16), vb) + _dot_nt((q * jnp.exp(gcum)).astype(BF16), st.astype(BF16))
            o_ref[:, sl] = o
            kge = (kk * jnp.exp(gend - gcum)).astype(BF16)
            state[hd] = st * jnp.exp(gend) + _dot_tn(vb, kge)

    def col(cb):
        return pl.BlockSpec((CHUNK, HG_WIDTH), lambda i: (i, cb))

    return pl.pallas_call(
        body, name="hgrn_fwd", grid=(nc,),
        in_specs=[col(P_HQ // HG_WIDTH), col(P_HF // HG_WIDTH), col(P_HI // HG_WIDTH), _full((2, HG_WIDTH))],
        out_specs=[pl.BlockSpec((CHUNK, HG_WIDTH), lambda i: (i, 0)),
                   pl.BlockSpec((1, HG_HEADS, HG_DIM, HG_DIM), lambda i: (i, 0, 0, 0))],
        out_shape=[jax.ShapeDtypeStruct((T, HG_WIDTH), F32),
                   jax.ShapeDtypeStruct((nc, HG_HEADS, HG_DIM, HG_DIM), F32)],
        scratch_shapes=[pltpu.VMEM((HG_HEADS, HG_DIM, HG_DIM), F32)],
        compiler_params=_params(1),
    )(proj, proj, proj, lbp)


def _hgrn_bwd(proj, lbp, do_hg, states):
    T = proj.shape[0]
    nc = T // CHUNK

    def body(hq_ref, hf_ref, hi_ref, lbp_ref, do_ref, st_ref, dhq_ref, dhf_ref, dhi_ref, dlb_ref, dstate):
        @pl.when(pl.program_id(0) == 0)
        def _():
            dstate[...] = jnp.zeros_like(dstate)
            dlb_ref[...] = jnp.zeros_like(dlb_ref)

        lb_all = _lower_bound(lbp_ref[...])
        for hd in range(HG_HEADS):
            sl = slice(hd * HG_DIM, (hd + 1) * HG_DIM)
            hq, lb = hq_ref[:, sl], lb_all[:, sl]
            sq, sf, f, q, kk, gcum = _hg_gates(hq, hf_ref[:, sl], lb)
            vb = hi_ref[:, sl].astype(BF16)
            dob = do_ref[:, sl].astype(BF16)
            st = st_ref[0, hd]
            dst = dstate[hd]
            dstb = dst.astype(BF16)
            blocks = _hg_blocks(q, kk, gcum)
            a = _hg_scores(blocks)
            gend = gcum[CHUNK - 1:CHUNK, :]
            eg, egend, ekend = jnp.exp(gcum), jnp.exp(gend), jnp.exp(gend - gcum)
            qg, kge = q * eg, kk * ekend
            kgeb = kge.astype(BF16)

            dv = _dot_tn(a.astype(BF16), dob) + _dot_nt(kgeb, dstb)
            da = jnp.where(_tri(True), _dot_nt(dob, vb), 0.0).astype(BF16)
            dqg = _dot(dob, st.astype(BF16))
            dkge = _dot(vb, dstb)
            dgend = jnp.sum(st * dst, axis=0, keepdims=True) * egend + jnp.sum(dkge * kge, axis=0, keepdims=True)
            dstate[hd] = _dot_tn(dob, qg.astype(BF16)) + dst * egend

            dq_a, dg_q = [], []
            dk_a, dg_k = jnp.zeros((CHUNK, HG_DIM), F32), jnp.zeros((CHUNK, HG_DIM), F32)
            for blk, (eq, ek, qb, kb) in enumerate(blocks):
                da_blk = da[blk * SUB:(blk + 1) * SUB, :]
                dq_blk, dk_blk = _dot(da_blk, kb), _dot_tn(da_blk, qb)
                dq_a.append(dq_blk * eq)
                dk_a = dk_a + dk_blk * ek
                dg_q.append(qb.astype(F32) * dq_blk)
                dg_k = dg_k + kb.astype(F32) * dk_blk
            dq_a = jnp.concatenate(dq_a, axis=0)

            dq = dq_a + dqg * eg
            dk = dk_a + dkge * ekend
            dgc = jnp.concatenate(dg_q, axis=0) - dg_k + dqg * qg - dkge * kge
            last = lax.broadcasted_iota(jnp.int32, (CHUNK, HG_DIM), 0) == CHUNK - 1
            dgc = dgc + jnp.where(last, dgend, 0.0)
            dg = jnp.dot(_tri(False).astype(F32), dgc, precision=lax.Precision.HIGHEST, preferred_element_type=F32)
            df = dg / f - dk
            dhf_ref[:, sl] = df * (1.0 - lb) * sf * (1.0 - sf)
            dlb_ref[:, sl] += jnp.sum(df * (1.0 - sf), axis=0, keepdims=True)
            dhq_ref[:, sl] = dq * (sq * (1.0 + hq * (1.0 - sq)))
            dhi_ref[:, sl] = dv

    def col(cb):
        return pl.BlockSpec((CHUNK, HG_WIDTH), lambda i: (nc - 1 - i, cb))

    grad = jax.ShapeDtypeStruct((T, HG_WIDTH), F32)
    return pl.pallas_call(
        body, name="hgrn_bwd", grid=(nc,),
        in_specs=[col(P_HQ // HG_WIDTH), col(P_HF // HG_WIDTH), col(P_HI // HG_WIDTH), _full((2, HG_WIDTH)),
                  col(0), pl.BlockSpec((1, HG_HEADS, HG_DIM, HG_DIM), lambda i: (nc - 1 - i, 0, 0, 0))],
        out_specs=[col(0), col(0), col(0), _full((1, HG_WIDTH))],
        out_shape=[grad, grad, grad, jax.ShapeDtypeStruct((1, HG_WIDTH), F32)],
        scratch_shapes=[pltpu.VMEM((HG_HEADS, HG_DIM, HG_DIM), F32)],
        compiler_params=_params(1),
    )(proj, proj, proj, lbp, do_hg, states)


def _top(x, tgt, o_mla, o_hg, proj, w_out, hg_norm_g, final_g, tm):
    T = x.shape[0]

    def body(x_ref, tgt_ref, om_ref, oh_ref, gm_ref, gh_ref, wout_ref, hgn_ref, fng_ref,
             dx2_ref, dx2b_ref, ycat_ref, dom_ref, dgm_ref, doh_ref, dgh_ref, loss_ref, dfng_ref, dhgn_ref):
        @pl.when(pl.program_id(0) == 0)
        def _():
            loss_ref[...] = jnp.zeros_like(loss_ref)
            dfng_ref[...] = jnp.zeros_like(dfng_ref)
            dhgn_ref[...] = jnp.zeros_like(dhgn_ref)

        gm, om = gm_ref[...], om_ref[...]
        sgm = _sigmoid(gm)
        silu_m = gm * sgm
        gh, oh, gam = gh_ref[...], oh_ref[...], hgn_ref[...]
        sgh = _sigmoid(gh)
        silu_h = gh * sgh
        rr, nn = [], []
        for hd in range(HG_HEADS):
            oh_h = oh[:, hd * HG_DIM:(hd + 1) * HG_DIM]
            r_h = lax.rsqrt(jnp.mean(oh_h * oh_h, axis=-1, keepdims=True) + EPS)
            rr.append(r_h)
            nn.append(oh_h * r_h)
        n = jnp.concatenate(nn, axis=1)
        ng = n * gam
        ycat_ref[:, :MLA_WIDTH] = (om * silu_m).astype(BF16)
        ycat_ref[:, MLA_WIDTH:] = (ng * silu_h).astype(BF16)
        wout = wout_ref[...]
        x2 = x_ref[...] + _dot(ycat_ref[...], wout)
        r = lax.rsqrt(jnp.mean(x2 * x2, axis=-1, keepdims=True) + EPS)
        xh = x2 * r
        fng = fng_ref[...]
        err = xh * fng - tgt_ref[...]
        loss_ref[...] += 0.5 * jnp.sum(jnp.mean(err * err, axis=-1, keepdims=True), axis=0, keepdims=True)
        dout = err * (1.0 / D_MODEL)
        dfng_ref[...] += jnp.sum(dout * xh, axis=0, keepdims=True)
        dxh = dout * fng
        dx2 = r * (dxh - xh * jnp.mean(dxh * xh, axis=-1, keepdims=True))
        dx2_ref[...] = dx2
        dx2b = dx2.astype(BF16)
        dx2b_ref[...] = dx2b
        dycat = _dot_nt(dx2b, wout)
        dym, dyh = dycat[:, :MLA_WIDTH], dycat[:, MLA_WIDTH:]
        dom_ref[...] = dym * silu_m
        dgm_ref[...] = dym * om * (sgm * (1.0 + gm * (1.0 - sgm)))
        dgh_ref[...] = dyh * ng * (sgh * (1.0 + gh * (1.0 - sgh)))
        dng = dyh * silu_h
        dhgn_ref[...] += jnp.sum(dng * n, axis=0, keepdims=True)
        dn = dng * gam
        for hd in range(HG_HEADS):
            sl = slice(hd * HG_DIM, (hd + 1) * HG_DIM)
            dn_h, n_h = dn[:, sl], nn[hd]
            doh_ref[:, sl] = rr[hd] * (dn_h - n_h * jnp.mean(dn_h * n_h, axis=-1, keepdims=True))

    def row(w, cb=0):
        return pl.BlockSpec((tm, w), lambda i: (i, cb))

    outs = [(D_MODEL, F32), (D_MODEL, BF16), (D_MODEL, BF16), (MLA_WIDTH, F32), (MLA_WIDTH, F32),
            (HG_WIDTH, F32), (HG_WIDTH, F32)]
    small = [(1, LANES), (1, D_MODEL), (1, HG_WIDTH)]
    return pl.pallas_call(
        body, name="top", grid=(T // tm,),
        in_specs=[row(D_MODEL), row(D_MODEL), row(MLA_WIDTH), row(HG_WIDTH),
                  row(MLA_WIDTH, P_GM // MLA_WIDTH), row(HG_WIDTH, P_GH // HG_WIDTH),
                  _full((D_MODEL, D_MODEL)), _full((1, HG_WIDTH)), _full((1, D_MODEL))],
        out_specs=[row(w) for w, _ in outs] + [_full(s) for s in small],
        out_shape=[jax.ShapeDtypeStruct((T, w), dt) for w, dt in outs] + [jax.ShapeDtypeStruct(s, F32) for s in small],
        compiler_params=_params(1),
    )(x, tgt, o_mla, o_hg, proj, proj, w_out, hg_norm_g, final_g)


def _bot(x, dx2, proj, dq, dk, dv, dgm, dhq, dhf, dhi, dgh, c_t, s1_t, s2_t, w_in_p, w_q_p, w_kv_p, ln_g, q_g, kv_g, tm):
    T = x.shape[0]
    lat_w = D_PERM - P_QL

    def body(x_ref, dx2_ref, lat_ref, dq_ref, dk_ref, dv_ref, dgm_ref, dhq_ref, dhf_ref, dhi_ref, dgh_ref,
             c_ref, s1_ref, s2_ref, win_ref, wq_ref, wkv_ref, lng_ref, qg_ref, kvg_ref,
             dx_ref, dproj_ref, dqpre_ref, dkv_ref, dlng_ref, dqg_ref, dkvg_ref):
        @pl.when(pl.program_id(0) == 0)
        def _():
            dlng_ref[...] = jnp.zeros_like(dlng_ref)
            dqg_ref[...] = jnp.zeros_like(dqg_ref)
            dkvg_ref[...] = jnp.zeros_like(dkvg_ref)

        c, s1, s2 = c_ref[...], s1_ref[...], s2_ref[...]
        dkpe = jnp.zeros((tm, LANES), F32)
        for hd in range(N_HEADS):
            sl = slice(hd * HEAD_LANES, (hd + 1) * HEAD_LANES)
            dqpre_ref[:, sl] = _rope_bwd(dq_ref[:, sl], c, s1, s2).astype(BF16)
            dk_h = dk_ref[:, sl]
            dkpe = dkpe + dk_h
            dkv_ref[:, sl] = dk_h.astype(BF16)
        dkv_ref[:, N_HEADS * HEAD_LANES:] = dv_ref[...].astype(BF16)
        lane = lax.broadcasted_iota(jnp.int32, (tm, LANES), 1)
        rope_lanes = jnp.logical_and(lane >= ROPE_LO, lane < ROPE_LO + ROPE)
        dkr = jnp.where(rope_lanes, _rope_bwd(dkpe, c, s1, s2), 0.0)

        def norm_bwd(v, g, dy):
            r = lax.rsqrt(jnp.mean(v * v, axis=-1, keepdims=True) + EPS)
            vh = v * r
            dvh = dy * g
            return jnp.sum(dy * vh, axis=0, keepdims=True), r * (dvh - vh * jnp.mean(dvh * vh, axis=-1, keepdims=True))

        dqn = _dot_nt(dqpre_ref[...], wq_ref[...])
        dg_q, dql = norm_bwd(lat_ref[:, :Q_RANK], qg_ref[...], dqn)
        dqg_ref[...] += dg_q
        dkn = _dot_nt(dkv_ref[...], wkv_ref[...])
        dg_kv, dkvl = norm_bwd(lat_ref[:, Q_RANK:Q_RANK + KV_RANK], kvg_ref[...], dkn)
        dkvg_ref[...] += dg_kv

        dproj_ref[:, P_GM:P_GM + MLA_WIDTH] = dgm_ref[...].astype(BF16)
        dproj_ref[:, P_HQ:P_HQ + HG_WIDTH] = dhq_ref[...].astype(BF16)
        dproj_ref[:, P_HF:P_HF + HG_WIDTH] = dhf_ref[...].astype(BF16)
        dproj_ref[:, P_HI:P_HI + HG_WIDTH] = dhi_ref[...].astype(BF16)
        dproj_ref[:, P_GH:P_GH + HG_WIDTH] = dgh_ref[...].astype(BF16)
        dproj_ref[:, P_QL:P_QL + Q_RANK] = dql.astype(BF16)
        dproj_ref[:, P_KVL:P_KVL + KV_RANK] = dkvl.astype(BF16)
        dproj_ref[:, P_KR:P_KR + LANES] = dkr.astype(BF16)
        dh = _dot_nt(dproj_ref[...], win_ref[...])
        dg_ln, dxn = norm_bwd(x_ref[...], lng_ref[...], dh)
        dlng_ref[...] += dg_ln
        dx_ref[...] = dx2_ref[...] + dxn

    def row(w, cb=0):
        return pl.BlockSpec((tm, w), lambda i: (i, cb))

    hl = N_HEADS * HEAD_LANES
    outs = [(D_MODEL, F32), (D_PERM, BF16), (hl, BF16), (hl + MLA_WIDTH, BF16)]
    small = [(1, D_MODEL), (1, Q_RANK), (1, KV_RANK)]
    return pl.pallas_call(
        body, name="bot", grid=(T // tm,),
        in_specs=[row(D_MODEL), row(D_MODEL), row(lat_w, P_QL // lat_w), row(hl), row(hl), row(MLA_WIDTH),
                  row(MLA_WIDTH), row(HG_WIDTH), row(HG_WIDTH), row(HG_WIDTH), row(HG_WIDTH),
                  row(LANES), row(LANES), row(LANES),
                  _full((D_MODEL, D_PERM)), _full((Q_RANK, hl)), _full((KV_RANK, hl + MLA_WIDTH)),
                  _full((1, D_MODEL)), _full((1, Q_RANK)), _full((1, KV_RANK))],
        out_specs=[row(w) for w, _ in outs] + [_full(s) for s in small],
        out_shape=[jax.ShapeDtypeStruct((T, w), dt) for w, dt in outs] + [jax.ShapeDtypeStruct(s, F32) for s in small],
        compiler_params=_params(1),
    )(x, dx2, proj, dq, dk, dv, dgm, dhq, dhf, dhi, dgh, c_t, s1_t, s2_t, w_in_p, w_q_p, w_kv_p, ln_g, q_g, kv_g)


def _matmul_tn(a, b, bn, bt, name):
    T, M = a.shape
    N = b.shape[1]

    def body(a_ref, b_ref, o_ref):
        @pl.when(pl.program_id(1) == 0)
        def _():
            o_ref[...] = jnp.zeros_like(o_ref)

        o_ref[...] += _dot_tn(a_ref[...], b_ref[...])

    return pl.pallas_call(
        body, name=name, grid=(N // bn, T // bt),
        in_specs=[pl.BlockSpec((bt, M), lambda n, t: (t, 0)), pl.BlockSpec((bt, bn), lambda n, t: (t, n))],
        out_specs=pl.BlockSpec((M, bn), lambda n, t: (0, n)),
        out_shape=jax.ShapeDtypeStruct((M, N), F32),
        compiler_params=_params(2),
    )(a, b)


def _reduce_scatter(slabs, small):
    n = len(slabs)

    def body(*refs):
        ins, small_ref = refs[:n], refs[n]
        outs, small_out = refs[n + 1:2 * n + 1], refs[2 * n + 1]
        land = refs[2 * n + 2:3 * n + 2]
        small_land = refs[3 * n + 2]
        send_sems, recv_sems, local_sems = refs[3 * n + 3:3 * n + 6]
        x, y, c = lax.axis_index("x"), lax.axis_index("y"), lax.axis_index("c")
        me = 4 * x + 2 * y + c

        def peer(k):
            px = 1 - x if k & 4 else x
            py = 1 - y if k & 2 else y
            pc = 1 - c if k & 1 else c
            return (px, py, pc), 4 * px + 2 * py + pc

        def copy(a, k, receiving):
            dev, pidx = peer(k)
            src = small_ref if a == n else ins[a].at[pidx]
            dst_all = small_land if a == n else land[a]
            dst = dst_all.at[pidx if receiving else me]
            return pltpu.make_async_remote_copy(src_ref=src, dst_ref=dst, send_sem=send_sems.at[a, k - 1],
                                                recv_sem=recv_sems.at[a, k - 1], device_id=dev, device_id_type=MESH)

        own = [pltpu.make_async_copy(ins[a].at[me], land[a].at[me], local_sems.at[a]) for a in range(n)]
        for cp in own:
            cp.start()
        small_land[me] = small_ref[...]
        sends = [copy(a, k, False) for k in range(1, N_DEV) for a in range(n + 1)]
        for cp in sends:
            cp.start()
        for cp in own:
            cp.wait()
        for k in range(1, N_DEV):
            for a in range(n + 1):
                copy(a, k, True).wait_recv()
        for a in range(n):
            acc = land[a][0]
            for d in range(1, N_DEV):
                acc = acc + land[a][d]
            outs[a][...] = acc
        acc = small_land[0]
        for d in range(1, N_DEV):
            acc = acc + small_land[d]
        small_out[...] = acc
        for cp in sends:
            cp.wait_send()

    vm = pl.BlockSpec(memory_space=pltpu.VMEM)
    hbm = pl.BlockSpec(memory_space=pl.ANY)
    return pl.pallas_call(
        body, name="reduce_scatter_grads",
        in_specs=[hbm] * n + [vm], out_specs=[vm] * (n + 1),
        out_shape=[jax.ShapeDtypeStruct(s.shape[1:], F32) for s in slabs] + [jax.ShapeDtypeStruct(small.shape, F32)],
        scratch_shapes=[pltpu.VMEM(s.shape, F32) for s in slabs] + [pltpu.VMEM((N_DEV,) + small.shape, F32)]
        + [pltpu.SemaphoreType.DMA((n + 1, N_DEV - 1)), pltpu.SemaphoreType.DMA((n + 1, N_DEV - 1)),
           pltpu.SemaphoreType.DMA((n,))],
        compiler_params=pltpu.CompilerParams(vmem_limit_bytes=VMEM_LIMIT),
    )(*slabs, small)


def _adamw_math(w, g, m, v):
    m = ADAM_B1 * m + (1.0 - ADAM_B1) * g
    v = ADAM_B2 * v + (1.0 - ADAM_B2) * (g * g)
    m_hat = m / (1.0 - ADAM_B1 ** ADAM_STEP)
    v_hat = v / (1.0 - ADAM_B2 ** ADAM_STEP)
    delta = -ADAM_LR * (m_hat / (jnp.sqrt(v_hat) + ADAM_EPS) + ADAM_WD * w)
    return delta, m, v


SMALL_W = 512


def _adamw(big, small_w, small_g):
    nb, ns = len(big), len(small_w)

    def body(*refs):
        k = 0
        big_in = [refs[4 * i:4 * i + 4] for i in range(nb)]
        k = 4 * nb
        small_in = [refs[k + 3 * i:k + 3 * i + 3] for i in range(ns)]
        k += 3 * ns
        sg_ref = refs[k]
        k += 1
        big_out = [refs[k + 3 * i:k + 3 * i + 3] for i in range(nb)]
        k += 3 * nb
        small_out = [refs[k + 4 * i:k + 4 * i + 4] for i in range(ns)]

        for (w, g, m, v), (od, om, ov) in zip(big_in, big_out):
            od[...], om[...], ov[...] = _adamw_math(w[...], g[...], m[...], v[...])

        sg = sg_ref[...]
        lbp = small_in[2][0][...]
        lb = _lower_bound(lbp)
        t = sg[4:5, :] * lb * (1.0 - lb)
        grads = [jnp.concatenate([sg[0:1, :], sg[1:2, :]], axis=1),
                 jnp.concatenate([sg[2:3, :], sg[3:4, :]], axis=1),
                 jnp.concatenate([t, -t], axis=0),
                 sg[6:7, :], sg[7:8, 0:Q_RANK], sg[7:8, Q_RANK:Q_RANK + KV_RANK]]
        for (w, m, v), g, (og, od, om, ov) in zip(small_in, grads, small_out):
            og[...] = g
            od[...], om[...], ov[...] = _adamw_math(w[...], g, m[...], v[...])

    ins = [a for grp in big for a in grp] + [a for grp in small_w for a in grp] + [small_g]
    out_shape = ([jax.ShapeDtypeStruct(grp[0].shape, F32) for grp in big for _ in range(3)]
                 + [jax.ShapeDtypeStruct(grp[0].shape, F32) for grp in small_w for _ in range(4)])
    vm = pl.BlockSpec(memory_space=pltpu.VMEM)
    res = pl.pallas_call(
        body, name="adamw", in_specs=[vm] * len(ins), out_specs=[vm] * len(out_shape), out_shape=out_shape,
        compiler_params=pltpu.CompilerParams(vmem_limit_bytes=VMEM_LIMIT),
    )(*ins)
    big_res = [res[3 * i:3 * i + 3] for i in range(nb)]
    small_res = [res[3 * nb + 4 * i:3 * nb + 4 * i + 4] for i in range(ns)]
    return big_res, small_res


def _perm_weights(g_in, g_q, g_kv, g_out):
    w = g_in.transpose(1, 0, 2).reshape(D_MODEL, D_IN)
    z = lambda n: jnp.zeros((D_MODEL, n), BF16)
    w_in_p = jnp.concatenate([w[:, 416:], w[:, :384], z(64), w[:, 384:416], z(32)], axis=1)
    wq = g_q.transpose(1, 0, 2)
    w_q_p = jnp.pad(wq, ((0, 0), (0, 0), (0, HEAD_LANES - NOPE - ROPE))).reshape(Q_RANK, N_HEADS * HEAD_LANES)
    wkv = g_kv.transpose(1, 0, 2)
    wk = jnp.pad(wkv[:, :, :NOPE], ((0, 0), (0, 0), (0, HEAD_LANES - NOPE))).reshape(KV_RANK, N_HEADS * HEAD_LANES)
    wv = wkv[:, :, NOPE:].reshape(KV_RANK, MLA_WIDTH)
    return w_in_p, w_q_p, jnp.concatenate([wk, wv], axis=1), g_out.reshape(D_MODEL, D_MODEL)


def _grad_slabs(dw_in_p, dw_q_p, dw_kv_p, dw_out):
    dw_in = jnp.concatenate([dw_in_p[:, P_QL:P_KR], dw_in_p[:, P_KR + ROPE_LO:P_KR + ROPE_LO + ROPE], dw_in_p[:, :P_QL]], axis=1)
    s_in = dw_in.reshape(D_MODEL, N_DEV, D_IN // N_DEV).transpose(1, 0, 2)
    s_q = dw_q_p.reshape(Q_RANK, N_HEADS, HEAD_LANES)[:, :, :NOPE + ROPE].transpose(1, 0, 2)
    hl = N_HEADS * HEAD_LANES
    dk = dw_kv_p[:, :hl].reshape(KV_RANK, N_HEADS, HEAD_LANES)[:, :, :NOPE]
    dv = dw_kv_p[:, hl:].reshape(KV_RANK, N_HEADS, V_DIM)
    s_kv = jnp.concatenate([dk, dv], axis=2).transpose(1, 0, 2)
    return s_in, s_q, s_kv, dw_out.reshape(N_DEV, D_MODEL // N_DEV, D_MODEL)


def kernel(x, positions, ln_g, w_in, q_a_norm_g, w_q_b, kv_a_norm_g, w_kv_b, hg_lower_bounds, hg_norm_g, w_out, final_norm_g, loss_target, m_ln_g, m_w_in, m_q_a_norm_g, m_w_q_b, m_kv_a_norm_g, m_w_kv_b, m_hg_lower_bounds, m_hg_norm_g, m_w_out, m_final_norm_g, v_ln_g, v_w_in, v_q_a_norm_g, v_w_q_b, v_kv_a_norm_g, v_w_kv_b, v_hg_lower_bounds, v_hg_norm_g, v_w_out, v_final_norm_g):
    T = x.shape[1]
    tm = min(256, T)
    tq = min(256, T)
    xs, tgt = x[0], loss_target[0]
    pos_f = positions[0].astype(F32).reshape(T, 1)
    fng = final_norm_g.reshape(1, D_MODEL)

    gathered = _all_gather_weights([w_in[0], w_q_b[0], w_kv_b[0], w_out[0]])
    w_in_p, w_q_p, w_kv_p, w_out_b = _perm_weights(*gathered)

    c_t, s1_t, s2_t = _rope_tables(pos_f, tm)
    proj, h, qn, kvn, q, k, v = _fwd_in(xs, ln_g, w_in_p, q_a_norm_g, w_q_p, kv_a_norm_g, w_kv_p, c_t, s1_t, s2_t, tm)
    o_mla, lse = _attn_fwd(q, k, v, tq)
    o_hg, states = _hgrn_fwd(proj, hg_lower_bounds)
    dx2, dx2b, ycat, d_om, d_gm, d_oh, d_gh, loss_p, d_fng, d_hgn = _top(
        xs, tgt, o_mla, o_hg, proj, w_out_b, hg_norm_g, fng, tm)
    dq, dk, dv = _attn_bwd(q, k, v, o_mla, d_om, lse, tq)
    d_hq, d_hf, d_hi, d_lb = _hgrn_bwd(proj, hg_lower_bounds, d_oh, states)
    dx, dproj, dq_pre, dkv, d_lng, d_qg, d_kvg = _bot(
        xs, dx2, proj, dq, dk, dv, d_gm, d_hq, d_hf, d_hi, d_gh, c_t, s1_t, s2_t, w_in_p, w_q_p, w_kv_p,
        ln_g, q_a_norm_g, kv_a_norm_g, tm)
    bt = min(512, T)
    dw_in_p = _matmul_tn(h, dproj, 512, bt, "dw_in")
    dw_out = _matmul_tn(ycat, dx2b, 512, bt, "dw_out")
    dw_q_p = _matmul_tn(qn, dq_pre, 512, bt, "dw_q_b")
    dw_kv_p = _matmul_tn(kvn, dkv, 512, bt, "dw_kv_b")

    zrow = jnp.zeros((1, SMALL_W), F32)
    small = jnp.concatenate([
        d_lng.reshape(2, SMALL_W), d_fng.reshape(2, SMALL_W), d_lb, zrow, d_hgn,
        jnp.concatenate([d_qg, d_kvg, jnp.zeros((1, SMALL_W - Q_RANK - KV_RANK), F32)], axis=1)], axis=0)
    g_in, g_q, g_kv, g_out, small_sum = _reduce_scatter(list(_grad_slabs(dw_in_p, dw_q_p, dw_kv_p, dw_out)), small)

    big = [(w_in[0], g_in, m_w_in[0], v_w_in[0]), (w_q_b[0], g_q, m_w_q_b[0], v_w_q_b[0]),
           (w_kv_b[0], g_kv, m_w_kv_b[0], v_w_kv_b[0]), (w_out[0], g_out, m_w_out[0], v_w_out[0])]
    small_w = [(ln_g, m_ln_g, v_ln_g),
               (fng, m_final_norm_g.reshape(1, D_MODEL), v_final_norm_g.reshape(1, D_MODEL)),
               (hg_lower_bounds, m_hg_lower_bounds, v_hg_lower_bounds), (hg_norm_g, m_hg_norm_g, v_hg_norm_g),
               (q_a_norm_g, m_q_a_norm_g, v_q_a_norm_g), (kv_a_norm_g, m_kv_a_norm_g, v_kv_a_norm_g)]
    big_res, small_res = _adamw(big, small_w, small_sum)

    loss = lax.psum(loss_p[0, 0], ("x", "y", "c"))
    (r_in, r_q, r_kv, r_out) = big_res
    (s_ln, s_fn, s_lb, s_hgn, s_qg, s_kvg) = small_res
    flat = lambda t: t.reshape(D_MODEL)
    lead = lambda t: t[None]
    grads = [s_ln[0], lead(g_in), s_qg[0], lead(g_q), s_kvg[0], lead(g_kv), s_lb[0], s_hgn[0], lead(g_out), flat(s_fn[0])]

    def pick(i):
        return [s_ln[i + 1], lead(r_in[i]), s_qg[i + 1], lead(r_q[i]), s_kvg[i + 1], lead(r_kv[i]), s_lb[i + 1],
                s_hgn[i + 1], lead(r_out[i]), flat(s_fn[i + 1])]

    return (loss, dx[None], *grads, *pick(0), *pick(1), *pick(2))
```

```python
import math

import numpy as np
import jax
import jax.numpy as jnp
from jax import lax
from jax.experimental import pallas as pl
from jax.experimental.pallas import tpu as pltpu

F32 = jnp.float32
BF16 = jnp.bfloat16

D_MODEL = 1024
N_HEADS = 8
NOPE = 64
ROPE = 32
HALF_ROPE = ROPE // 2
V_DIM = 64
Q_RANK = 256
KV_RANK = 128
MLA_WIDTH = N_HEADS * V_DIM
HG_HEADS = 4
HG_DIM = 128
HG_WIDTH = HG_HEADS * HG_DIM
CHUNK = 64
SUB = 16
D_IN = 2976
D_PERM = 3072
ROPE_THETA = 10000.0
EPS = 1e-6
N_DEV = 8
LANES = 128
HEAD_LANES = 128

P_GM, P_HQ, P_HF, P_HI, P_GH, P_QL, P_KVL, P_KR = 0, 512, 1024, 1536, 2048, 2560, 2816, 2944
ROPE_LO = NOPE
SCALE = 1.0 / math.sqrt(NOPE + ROPE)

ADAM_LR = 0.001
ADAM_B1 = 0.9
ADAM_B2 = 0.999
ADAM_EPS = 1e-08
ADAM_WD = 0.01
ADAM_STEP = 10

VMEM_LIMIT = 56 * 1024 * 1024
MESH = pl.DeviceIdType.MESH

NT = (((1,), (1,)), ((), ()))
TN = (((0,), (0,)), ((), ()))


def _params(n_grid=0, **kw):
    sem = ("arbitrary",) * n_grid if n_grid else None
    return pltpu.CompilerParams(dimension_semantics=sem, vmem_limit_bytes=VMEM_LIMIT, **kw)


def _dot(a, b):
    return jnp.dot(a, b, preferred_element_type=F32)


def _dot_nt(a, b):
    return lax.dot_general(a, b, NT, preferred_element_type=F32)


def _dot_tn(a, b):
    return lax.dot_general(a, b, TN, preferred_element_type=F32)


def _sigmoid(x):
    return 1.0 / (1.0 + jnp.exp(-x))


def _rope_fwd(x, c, s1, s2):
    return x * c + pltpu.roll(x, LANES - HALF_ROPE, 1) * s1 + pltpu.roll(x, HALF_ROPE, 1) * s2


def _rope_bwd(dy, c, s1, s2):
    return dy * c - pltpu.roll(dy, LANES - HALF_ROPE, 1) * s1 - pltpu.roll(dy, HALF_ROPE, 1) * s2


def _full(shape):
    n = len(shape)
    return pl.BlockSpec(shape, lambda *_: (0,) * n)


def _rope_tables(pos_f, tm):
    T = pos_f.shape[0]
    inv = (np.float32(ROPE_THETA) ** (-np.arange(HALF_ROPE, dtype=np.float32) / np.float32(HALF_ROPE))).astype(np.float32)
    inv1 = np.zeros((1, LANES), np.float32)
    inv2 = np.zeros((1, LANES), np.float32)
    inv1[0, ROPE_LO:ROPE_LO + HALF_ROPE] = inv
    inv2[0, ROPE_LO + HALF_ROPE:ROPE_LO + ROPE] = inv

    def body(pos_ref, inv1_ref, inv2_ref, c_ref, s1_ref, s2_ref):
        pos = pos_ref[...]
        a1 = pos * inv1_ref[...]
        a2 = pos * inv2_ref[...]
        c_ref[...] = jnp.cos(a1 + a2)
        s1_ref[...] = -jnp.sin(a1)
        s2_ref[...] = jnp.sin(a2)

    tab = jax.ShapeDtypeStruct((T, LANES), F32)
    return pl.pallas_call(
        body, name="rope_tables", grid=(T // tm,),
        in_specs=[pl.BlockSpec((tm, 1), lambda i: (i, 0)), _full((1, LANES)), _full((1, LANES))],
        out_specs=[pl.BlockSpec((tm, LANES), lambda i: (i, 0))] * 3,
        out_shape=[tab, tab, tab], compiler_params=_params(1),
    )(pos_f, jnp.asarray(inv1), jnp.asarray(inv2))


def _all_gather_weights(shards):
    n = len(shards)

    def body(*refs):
        ins, outs = refs[:n], refs[n:2 * n]
        send_sems, recv_sems = refs[2 * n], refs[2 * n + 1]
        x, y, c = lax.axis_index("x"), lax.axis_index("y"), lax.axis_index("c")
        me, sibling = (x, y, c), (x, y, 1 - c)
        chips = [(1 - x, y), (x, 1 - y), (1 - x, 1 - y)]

        def idx(d):
            return 4 * d[0] + 2 * d[1] + d[2]

        def copy(a, k, block, to):
            rows = outs[a].at[idx(block)]
            return pltpu.make_async_remote_copy(src_ref=rows, dst_ref=rows, send_sem=send_sems.at[a, k],
                                                recv_sem=recv_sems.at[a, k], device_id=to, device_id_type=MESH)

        for a in range(n):
            outs[a][idx(me)] = ins[a][...].astype(BF16)
        first = []
        for a in range(n):
            first.append(copy(a, 0, me, sibling))
            first += [copy(a, 1 + j, me, (*chip, c)) for j, chip in enumerate(chips)]
        for cp in first:
            cp.start()
        passed = []
        for j, chip in enumerate(chips):
            for a in range(n):
                copy(a, 1 + j, (*chip, c), me).wait_recv()
                cp = copy(a, 4 + j, (*chip, c), sibling)
                cp.start()
                passed.append(cp)
        for a in range(n):
            copy(a, 0, sibling, me).wait_recv()
            for j, chip in enumerate(chips):
                copy(a, 4 + j, (*chip, 1 - c), me).wait_recv()
        for cp in first + passed:
            cp.wait_send()

    vm = pl.BlockSpec(memory_space=pltpu.VMEM)
    return pl.pallas_call(
        body, name="all_gather_weights",
        in_specs=[vm] * n, out_specs=[vm] * n,
        out_shape=[jax.ShapeDtypeStruct((N_DEV,) + s.shape, BF16) for s in shards],
        scratch_shapes=[pltpu.SemaphoreType.DMA((n, 7)), pltpu.SemaphoreType.DMA((n, 7))],
        compiler_params=pltpu.CompilerParams(vmem_limit_bytes=VMEM_LIMIT),
    )(*shards)


def _fwd_in(x, ln_g, w_in_p, q_g, w_q_p, kv_g, w_kv_p, c_t, s1_t, s2_t, tm):
    T = x.shape[0]

    def body(x_ref, lng_ref, win_ref, qg_ref, wq_ref, kvg_ref, wkv_ref, c_ref, s1_ref, s2_ref,
             proj_ref, h_ref, qn_ref, kvn_ref, q_ref, k_ref, v_ref):
        xv = x_ref[...]
        r = lax.rsqrt(jnp.mean(xv * xv, axis=-1, keepdims=True) + EPS)
        h = (xv * r * lng_ref[...]).astype(BF16)
        h_ref[...] = h
        proj = _dot(h, win_ref[...])
        proj_ref[...] = proj
        c, s1, s2 = c_ref[...], s1_ref[...], s2_ref[...]

        ql = proj[:, P_QL:P_QL + Q_RANK]
        rq = lax.rsqrt(jnp.mean(ql * ql, axis=-1, keepdims=True) + EPS)
        qn = (ql * rq * qg_ref[...]).astype(BF16)
        qn_ref[...] = qn
        q = _dot(qn, wq_ref[...])
        for hd in range(N_HEADS):
            sl = slice(hd * HEAD_LANES, (hd + 1) * HEAD_LANES)
            q_ref[:, sl] = _rope_fwd(q[:, sl], c, s1, s2).astype(BF16)

        kvl = proj[:, P_KVL:P_KVL + KV_RANK]
        rk = lax.rsqrt(jnp.mean(kvl * kvl, axis=-1, keepdims=True) + EPS)
        kvn = (kvl * rk * kvg_ref[...]).astype(BF16)
        kvn_ref[...] = kvn
        kv = _dot(kvn, wkv_ref[...])
        kpe = _rope_fwd(proj[:, P_KR:P_KR + LANES], c, s1, s2)
        for hd in range(N_HEADS):
            sl = slice(hd * HEAD_LANES, (hd + 1) * HEAD_LANES)
            k_ref[:, sl] = (kv[:, sl] + kpe).astype(BF16)
        v_ref[...] = kv[:, N_HEADS * HEAD_LANES:].astype(BF16)

    def row(w):
        return pl.BlockSpec((tm, w), lambda i: (i, 0))

    outs = [(D_PERM, F32), (D_MODEL, BF16), (Q_RANK, BF16), (KV_RANK, BF16),
            (N_HEADS * HEAD_LANES, BF16), (N_HEADS * HEAD_LANES, BF16), (MLA_WIDTH, BF16)]
    return pl.pallas_call(
        body, name="fwd_in", grid=(T // tm,),
        in_specs=[row(D_MODEL), _full((1, D_MODEL)), _full((D_MODEL, D_PERM)), _full((1, Q_RANK)),
                  _full((Q_RANK, N_HEADS * HEAD_LANES)), _full((1, KV_RANK)),
                  _full((KV_RANK, N_HEADS * HEAD_LANES + MLA_WIDTH)), row(LANES), row(LANES), row(LANES)],
        out_specs=[row(w) for w, _ in outs],
        out_shape=[jax.ShapeDtypeStruct((T, w), dt) for w, dt in outs],
        compiler_params=_params(1),
    )(x, ln_g, w_in_p, q_g, w_q_p, kv_g, w_kv_p, c_t, s1_t, s2_t)


LOG2E = 1.4426950408889634
SCALE2 = SCALE * LOG2E


def _causal(tq):
    r = lax.broadcasted_iota(jnp.int32, (tq, tq), 0)
    c = lax.broadcasted_iota(jnp.int32, (tq, tq), 1)
    return r <= c


def _attn_fwd(q, k, v_t, tq):
    T = q.shape[0]
    nq = T // tq

    def body(q_ref, k_ref, vt_ref, o_ref, lse_ref, sa_ref, sb_ref):
        i = pl.program_id(1)
        qh = [q_ref[:, hh * HEAD_LANES:(hh + 1) * HEAD_LANES] for hh in range(2)]

        def scores(j, s_ref):
            kj = k_ref[pl.ds(pl.multiple_of(j * tq, tq), tq), :]
            for hh in range(2):
                s_ref[hh] = _dot_nt(kj[:, hh * HEAD_LANES:(hh + 1) * HEAD_LANES], qh[hh])

        def tile(j, stats, s_ref, masked):
            vt = vt_ref[j]
            out = []
            for hh in range(2):
                m, l, acc = stats[3 * hh:3 * hh + 3]
                s = s_ref[hh] * SCALE2
                if masked:
                    s = jnp.where(_causal(tq), s, -jnp.inf)
                m_new = jnp.maximum(m, jnp.max(s, axis=0, keepdims=True))
                alpha = jnp.exp2(m - m_new)
                p = jnp.exp2(s - m_new)
                out += [m_new, alpha * l + jnp.sum(p, axis=0, keepdims=True), alpha * acc + _dot(vt, p.astype(BF16))]
            return tuple(out)

        def pair(r, stats):
            scores(2 * r + 1, sb_ref)
            stats = tile(2 * r, stats, sa_ref, False)
            scores(2 * r + 2, sa_ref)
            return tile(2 * r + 1, stats, sb_ref, False)

        def last_even(stats):
            return tile(i, stats, sa_ref, True)

        def last_odd(stats):
            scores(i, sb_ref)
            return tile(i, tile(i - 1, stats, sa_ref, False), sb_ref, True)

        init = (jnp.full((1, tq), -1e30, F32), jnp.zeros((1, tq), F32), jnp.zeros((LANES, tq), F32)) * 2
        scores(0, sa_ref)
        stats = lax.fori_loop(0, i // 2, pair, init)
        m0, l0, a0, m1, l1, a1 = lax.cond(i % 2 == 0, last_even, last_odd, stats)
        first = lax.broadcasted_iota(jnp.int32, (LANES, tq), 0) < V_DIM
        o_ref[...] = jnp.where(first, a0 / l0, a1 / l1).T
        lse_ref[0, 0] = m0 + jnp.log2(l0)
        lse_ref[1, 0] = m1 + jnp.log2(l1)

    return pl.pallas_call(
        body, name="attn_fwd", grid=(N_HEADS // 2, nq),
        in_specs=[pl.BlockSpec((tq, 2 * HEAD_LANES), lambda p, i: (i, p)),
                  pl.BlockSpec((T, 2 * HEAD_LANES), lambda p, i: (0, p)),
                  pl.BlockSpec((nq, LANES, tq), lambda p, i: (0, p, 0))],
        out_specs=[pl.BlockSpec((tq, LANES), lambda p, i: (i, p)),
                   pl.BlockSpec((2, 1, 1, tq), lambda p, i: (p, i, 0, 0))],
        out_shape=[jax.ShapeDtypeStruct((T, MLA_WIDTH), F32), jax.ShapeDtypeStruct((N_HEADS, nq, 1, tq), F32)],
        scratch_shapes=[pltpu.VMEM((2, tq, tq), F32), pltpu.VMEM((2, tq, tq), F32)],
        compiler_params=_params(2),
    )(q, k, v_t)


def _attn_bwd(q, k, v, q_t, k_t, do_m, do_t, lse, dsum, tq):
    T = q.shape[0]
    nq = T // tq

    def body(q_ref, k_ref, v_ref, qt_ref, kt_ref, do_ref, dot_ref, lse_ref, dsum_ref, dqt_ref, dk_ref, dv_ref,
             ba_ref, bb_ref, dkt_ref, dvt_ref):
        j = pl.program_id(1)

        @pl.when(j == 0)
        def _():
            dqt_ref[...] = jnp.zeros_like(dqt_ref)

        dkt_ref[...] = jnp.zeros_like(dkt_ref)
        dvt_ref[...] = jnp.zeros_like(dvt_ref)
        heads = [slice(hh * HEAD_LANES, (hh + 1) * HEAD_LANES) for hh in range(2)]

        def rows_of(t):
            i = nq - 1 - t
            return i, pl.ds(pl.multiple_of(i * tq, tq), tq)

        def products(t, buf):
            _, rows = rows_of(t)
            for hh, sl in enumerate(heads):
                buf[hh] = _dot_nt(k_ref[:, sl], q_ref[rows, sl])
                buf[2 + hh] = _dot_nt(v_ref[...], do_ref[rows, sl])

        def tile(t, buf, masked):
            i, rows = rows_of(t)
            dv_new = None
            for hh, sl in enumerate(heads):
                s = buf[hh] * SCALE2
                if masked:
                    s = jnp.where(_causal(tq), s, -jnp.inf)
                p = jnp.exp2(s - lse_ref[hh, i])
                ds = (p * (buf[2 + hh] - dsum_ref[hh, i]) * SCALE).astype(BF16)
                dv_h = _dot_nt(dot_ref[i, sl, :], p.astype(BF16))
                dv_new = dv_h if dv_new is None else dv_new + dv_h
                dkt_ref[sl, :] += _dot_nt(qt_ref[i, sl, :], ds)
                dqt_ref[i, sl, :] += _dot(kt_ref[0, sl, :], ds)
            dvt_ref[...] += dv_new

        n_plain = nq - 1 - j
        products(0, ba_ref)

        def pair(r, carry):
            products(2 * r + 1, bb_ref)
            tile(2 * r, ba_ref, False)
            products(2 * r + 2, ba_ref)
            tile(2 * r + 1, bb_ref, False)
            return carry

        lax.fori_loop(0, n_plain // 2, pair, 0)

        @pl.when(n_plain % 2 == 0)
        def _():
            tile(n_plain, ba_ref, True)

        @pl.when(n_plain % 2 == 1)
        def _():
            products(n_plain, bb_ref)
            tile(n_plain - 1, ba_ref, False)
            tile(n_plain, bb_ref, True)

        dk_ref[...] = dkt_ref[...].T
        dv_ref[...] = dvt_ref[...].T

    stat = pl.BlockSpec((2, nq, 1, tq), lambda p, j: (p, 0, 0, 0))
    blocks_t = pl.BlockSpec((nq, 2 * HEAD_LANES, tq), lambda p, j: (0, p, 0))
    return pl.pallas_call(
        body, name="attn_bwd", grid=(N_HEADS // 2, nq),
        in_specs=[pl.BlockSpec((T, 2 * HEAD_LANES), lambda p, j: (0, p)),
                  pl.BlockSpec((tq, 2 * HEAD_LANES), lambda p, j: (j, p)),
                  pl.BlockSpec((tq, LANES), lambda p, j: (j, p)),
                  blocks_t,
                  pl.BlockSpec((1, 2 * HEAD_LANES, tq), lambda p, j: (j, p, 0)),
                  pl.BlockSpec((T, 2 * HEAD_LANES), lambda p, j: (0, p)), blocks_t, stat, stat],
        out_specs=[blocks_t,
                   pl.BlockSpec((tq, 2 * HEAD_LANES), lambda p, j: (j, p)),
                   pl.BlockSpec((tq, LANES), lambda p, j: (j, p))],
        out_shape=[jax.ShapeDtypeStruct((nq, N_HEADS * HEAD_LANES, tq), F32),
                   jax.ShapeDtypeStruct((T, N_HEADS * HEAD_LANES), F32),
                   jax.ShapeDtypeStruct((T, MLA_WIDTH), F32)],
        scratch_shapes=[pltpu.VMEM((4, tq, tq), F32), pltpu.VMEM((4, tq, tq), F32),
                        pltpu.VMEM((2 * HEAD_LANES, tq), F32), pltpu.VMEM((LANES, tq), F32)],
        compiler_params=_params(2),
    )(q, k, v, q_t, k_t, do_m, do_t, lse, dsum)


def _lower_bound(lbp):
    a, b = lbp[0:1, :], lbp[1:2, :]
    mx = jnp.maximum(a, b)
    ea, eb = jnp.exp(a - mx), jnp.exp(b - mx)
    return ea / (ea + eb)


def _tri(lower):
    r = lax.broadcasted_iota(jnp.int32, (CHUNK, CHUNK), 0)
    c = lax.broadcasted_iota(jnp.int32, (CHUNK, CHUNK), 1)
    return (c <= r) if lower else (c >= r)


def _hg_gates(hq, hf, lb):
    sq = _sigmoid(hq)
    sf = _sigmoid(hf)
    f = lb + (1.0 - lb) * sf
    g = jnp.log(f)
    gcum = jnp.dot(_tri(True).astype(F32), g, precision=lax.Precision.HIGHEST, preferred_element_type=F32)
    return sq, sf, f, hq * sq, 1.0 - f, gcum


def _hg_blocks(q, kk, gcum):
    rowi = lax.broadcasted_iota(jnp.int32, (CHUNK, HG_DIM), 0)
    out = []
    for blk in range(CHUNK // SUB):
        lo, hi = blk * SUB, (blk + 1) * SUB
        gb = gcum[lo - 1:lo, :] if blk else jnp.zeros((1, HG_DIM), F32)
        eq = jnp.exp(gcum[lo:hi, :] - gb)
        ek = jnp.exp(jnp.where(rowi < hi, gb - gcum, 0.0))
        out.append((eq, ek, (q[lo:hi, :] * eq).astype(BF16), (kk * ek).astype(BF16)))
    return out


def _hg_scores(blocks):
    a = jnp.concatenate([_dot_nt(qb, kb) for _, _, qb, kb in blocks], axis=0)
    return jnp.where(_tri(True), a, 0.0)


def _hgrn_fwd(proj, lbp):
    T = proj.shape[0]
    nc = T // CHUNK

    def body(hq_ref, hf_ref, hi_ref, lbp_ref, o_ref, st_ref, state):
        @pl.when(pl.program_id(0) == 0)
        def _():
            state[...] = jnp.zeros_like(state)

        lb_all = _lower_bound(lbp_ref[...])
        for hd in range(HG_HEADS):
            sl = slice(hd * HG_DIM, (hd + 1) * HG_DIM)
            _, _, _, q, kk, gcum = _hg_gates(hq_ref[:, sl], hf_ref[:, sl], lb_all[:, sl])
            vb = hi_ref[:, sl].astype(BF16)
            st = state[hd]
            st_ref[0, hd] = st
            a = _hg_scores(_hg_blocks(q, kk, gcum))
            gend = gcum[CHUNK - 1:CHUNK, :]
            o = _dot(a.astype(BF16), vb) + _dot_nt((q * jnp.exp(gcum)).astype(BF16), st.astype(BF16))
            o_ref[:, sl] = o
            kge = (kk * jnp.exp(gend - gcum)).astype(BF16)
            state[hd] = st * jnp.exp(gend) + _dot_tn(vb, kge)

    def col(cb):
        return pl.BlockSpec((CHUNK, HG_WIDTH), lambda i: (i, cb))

    return pl.pallas_call(
        body, name="hgrn_fwd", grid=(nc,),
        in_specs=[col(P_HQ // HG_WIDTH), col(P_HF // HG_WIDTH), col(P_HI // HG_WIDTH), _full((2, HG_WIDTH))],
        out_specs=[pl.BlockSpec((CHUNK, HG_WIDTH), lambda i: (i, 0)),
                   pl.BlockSpec((1, HG_HEADS, HG_DIM, HG_DIM), lambda i: (i, 0, 0, 0))],
        out_shape=[jax.ShapeDtypeStruct((T, HG_WIDTH), F32),
                   jax.ShapeDtypeStruct((nc, HG_HEADS, HG_DIM, HG_DIM), F32)],
        scratch_shapes=[pltpu.VMEM((HG_HEADS, HG_DIM, HG_DIM), F32)],
        compiler_params=_params(1),
    )(proj, proj, proj, lbp)


def _hgrn_bwd(proj, lbp, do_hg, states):
    T = proj.shape[0]
    nc = T // CHUNK

    def body(hq_ref, hf_ref, hi_ref, lbp_ref, do_ref, st_ref, dhq_ref, dhf_ref, dhi_ref, dlb_ref, dstate):
        @pl.when(pl.program_id(0) == 0)
        def _():
            dstate[...] = jnp.zeros_like(dstate)
            dlb_ref[...] = jnp.zeros_like(dlb_ref)

        lb_all = _lower_bound(lbp_ref[...])
        for hd in range(HG_HEADS):
            sl = slice(hd * HG_DIM, (hd + 1) * HG_DIM)
            hq, lb = hq_ref[:, sl], lb_all[:, sl]
            sq, sf, f, q, kk, gcum = _hg_gates(hq, hf_ref[:, sl], lb)
            vb = hi_ref[:, sl].astype(BF16)
            dob = do_ref[:, sl].astype(BF16)
            st = st_ref[0, hd]
            dst = dstate[hd]
            dstb = dst.astype(BF16)
            blocks = _hg_blocks(q, kk, gcum)
            a = _hg_scores(blocks)
            gend = gcum[CHUNK - 1:CHUNK, :]
            eg, egend, ekend = jnp.exp(gcum), jnp.exp(gend), jnp.exp(gend - gcum)
            qg, kge = q * eg, kk * ekend
            kgeb = kge.astype(BF16)

            dv = _dot_tn(a.astype(BF16), dob) + _dot_nt(kgeb, dstb)
            da = jnp.where(_tri(True), _dot_nt(dob, vb), 0.0).astype(BF16)
            dqg = _dot(dob, st.astype(BF16))
            dkge = _dot(vb, dstb)
            dgend = jnp.sum(st * dst, axis=0, keepdims=True) * egend + jnp.sum(dkge * kge, axis=0, keepdims=True)
            dstate[hd] = _dot_tn(dob, qg.astype(BF16)) + dst * egend

            dq_a, dg_q = [], []
            dk_a, dg_k = jnp.zeros((CHUNK, HG_DIM), F32), jnp.zeros((CHUNK, HG_DIM), F32)
            for blk, (eq, ek, qb, kb) in enumerate(blocks):
                da_blk = da[blk * SUB:(blk + 1) * SUB, :]
                dq_blk, dk_blk = _dot(da_blk, kb), _dot_tn(da_blk, qb)
                dq_a.append(dq_blk * eq)
                dk_a = dk_a + dk_blk * ek
                dg_q.append(qb.astype(F32) * dq_blk)
                dg_k = dg_k + kb.astype(F32) * dk_blk
            dq_a = jnp.concatenate(dq_a, axis=0)

            dq = dq_a + dqg * eg
            dk = dk_a + dkge * ekend
            dgc = jnp.concatenate(dg_q, axis=0) - dg_k + dqg * qg - dkge * kge
            last = lax.broadcasted_iota(jnp.int32, (CHUNK, HG_DIM), 0) == CHUNK - 1
            dgc = dgc + jnp.where(last, dgend, 0.0)
            dg = jnp.dot(_tri(False).astype(F32), dgc, precision=lax.Precision.HIGHEST, preferred_element_type=F32)
            df = dg / f - dk
            dhf_ref[:, sl] = df * (1.0 - lb) * sf * (1.0 - sf)
            dlb_ref[:, sl] += jnp.sum(df * (1.0 - sf), axis=0, keepdims=True)
            dhq_ref[:, sl] = dq * (sq * (1.0 + hq * (1.0 - sq)))
            dhi_ref[:, sl] = dv

    def col(cb):
        return pl.BlockSpec((CHUNK, HG_WIDTH), lambda i: (nc - 1 - i, cb))

    grad = jax.ShapeDtypeStruct((T, HG_WIDTH), F32)
    return pl.pallas_call(
        body, name="hgrn_bwd", grid=(nc,),
        in_specs=[col(P_HQ // HG_WIDTH), col(P_HF // HG_WIDTH), col(P_HI // HG_WIDTH), _full((2, HG_WIDTH)),
                  col(0), pl.BlockSpec((1, HG_HEADS, HG_DIM, HG_DIM), lambda i: (nc - 1 - i, 0, 0, 0))],
        out_specs=[col(0), col(0), col(0), _full((1, HG_WIDTH))],
        out_shape=[grad, grad, grad, jax.ShapeDtypeStruct((1, HG_WIDTH), F32)],
        scratch_shapes=[pltpu.VMEM((HG_HEADS, HG_DIM, HG_DIM), F32)],
        compiler_params=_params(1),
    )(proj, proj, proj, lbp, do_hg, states)


def _top(x, tgt, o_mla, o_hg, proj, w_out, hg_norm_g, final_g, tm):
    T = x.shape[0]

    def body(x_ref, tgt_ref, om_ref, oh_ref, gm_ref, gh_ref, wout_ref, hgn_ref, fng_ref,
             dx2_ref, dx2b_ref, ycat_ref, dom_ref, dsum_ref, dgm_ref, doh_ref, dgh_ref, loss_ref, dfng_ref, dhgn_ref):
        @pl.when(pl.program_id(0) == 0)
        def _():
            loss_ref[...] = jnp.zeros_like(loss_ref)
            dfng_ref[...] = jnp.zeros_like(dfng_ref)
            dhgn_ref[...] = jnp.zeros_like(dhgn_ref)

        gm, om = gm_ref[...], om_ref[...]
        sgm = _sigmoid(gm)
        silu_m = gm * sgm
        gh, oh, gam = gh_ref[...], oh_ref[...], hgn_ref[...]
        sgh = _sigmoid(gh)
        silu_h = gh * sgh
        rr, nn = [], []
        for hd in range(HG_HEADS):
            oh_h = oh[:, hd * HG_DIM:(hd + 1) * HG_DIM]
            r_h = lax.rsqrt(jnp.mean(oh_h * oh_h, axis=-1, keepdims=True) + EPS)
            rr.append(r_h)
            nn.append(oh_h * r_h)
        n = jnp.concatenate(nn, axis=1)
        ng = n * gam
        ycat_ref[:, :MLA_WIDTH] = (om * silu_m).astype(BF16)
        ycat_ref[:, MLA_WIDTH:] = (ng * silu_h).astype(BF16)
        wout = wout_ref[...]
        x2 = x_ref[...] + _dot(ycat_ref[...], wout)
        r = lax.rsqrt(jnp.mean(x2 * x2, axis=-1, keepdims=True) + EPS)
        xh = x2 * r
        fng = fng_ref[...]
        err = xh * fng - tgt_ref[...]
        loss_ref[...] += 0.5 * jnp.sum(jnp.mean(err * err, axis=-1, keepdims=True), axis=0, keepdims=True)
        dout = err * (1.0 / D_MODEL)
        dfng_ref[...] += jnp.sum(dout * xh, axis=0, keepdims=True)
        dxh = dout * fng
        dx2 = r * (dxh - xh * jnp.mean(dxh * xh, axis=-1, keepdims=True))
        dx2_ref[...] = dx2
        dx2b = dx2.astype(BF16)
        dx2b_ref[...] = dx2b
        dycat = _dot_nt(dx2b, wout)
        dym, dyh = dycat[:, :MLA_WIDTH], dycat[:, MLA_WIDTH:]
        dom = dym * silu_m
        first = lax.broadcasted_iota(jnp.int32, (tm, LANES), 1) < V_DIM
        for pp in range(N_HEADS // 2):
            pair = dom[:, pp * LANES:(pp + 1) * LANES]
            dom_ref[:, 2 * pp * HEAD_LANES:(2 * pp + 1) * HEAD_LANES] = jnp.where(first, pair, 0.0).astype(BF16)
            dom_ref[:, (2 * pp + 1) * HEAD_LANES:(2 * pp + 2) * HEAD_LANES] = jnp.where(first, 0.0, pair).astype(BF16)
        head_of = lax.broadcasted_iota(jnp.int32, (MLA_WIDTH, LANES), 0) // V_DIM
        pick = (head_of == lax.broadcasted_iota(jnp.int32, (MLA_WIDTH, LANES), 1)).astype(F32)
        dsum_ref[...] = jnp.dot(dom * om, pick, precision=lax.Precision.HIGHEST, preferred_element_type=F32)
        dgm_ref[...] = dym * om * (sgm * (1.0 + gm * (1.0 - sgm)))
        dgh_ref[...] = dyh * ng * (sgh * (1.0 + gh * (1.0 - sgh)))
        dng = dyh * silu_h
        dhgn_ref[...] += jnp.sum(dng * n, axis=0, keepdims=True)
        dn = dng * gam
        for hd in range(HG_HEADS):
            sl = slice(hd * HG_DIM, (hd + 1) * HG_DIM)
            dn_h, n_h = dn[:, sl], nn[hd]
            doh_ref[:, sl] = rr[hd] * (dn_h - n_h * jnp.mean(dn_h * n_h, axis=-1, keepdims=True))

    def row(w, cb=0):
        return pl.BlockSpec((tm, w), lambda i: (i, cb))

    outs = [(D_MODEL, F32), (D_MODEL, BF16), (D_MODEL, BF16), (N_HEADS * HEAD_LANES, BF16), (LANES, F32),
            (MLA_WIDTH, F32), (HG_WIDTH, F32), (HG_WIDTH, F32)]
    small = [(1, LANES), (1, D_MODEL), (1, HG_WIDTH)]
    return pl.pallas_call(
        body, name="top", grid=(T // tm,),
        in_specs=[row(D_MODEL), row(D_MODEL), row(MLA_WIDTH), row(HG_WIDTH),
                  row(MLA_WIDTH, P_GM // MLA_WIDTH), row(HG_WIDTH, P_GH // HG_WIDTH),
                  _full((D_MODEL, D_MODEL)), _full((1, HG_WIDTH)), _full((1, D_MODEL))],
        out_specs=[row(w) for w, _ in outs] + [_full(s) for s in small],
        out_shape=[jax.ShapeDtypeStruct((T, w), dt) for w, dt in outs] + [jax.ShapeDtypeStruct(s, F32) for s in small],
        compiler_params=_params(1),
    )(x, tgt, o_mla, o_hg, proj, proj, w_out, hg_norm_g, final_g)


def _bot(x, dx2, proj, dq, dk, dv, dgm, dhq, dhf, dhi, dgh, c_t, s1_t, s2_t, w_in_p, w_q_p, w_kv_p, ln_g, q_g, kv_g, tm):
    T = x.shape[0]
    lat_w = D_PERM - P_QL

    def body(x_ref, dx2_ref, lat_ref, dq_ref, dk_ref, dv_ref, dgm_ref, dhq_ref, dhf_ref, dhi_ref, dgh_ref,
             c_ref, s1_ref, s2_ref, win_ref, wq_ref, wkv_ref, lng_ref, qg_ref, kvg_ref,
             dx_ref, dproj_ref, dqpre_ref, dkv_ref, dlng_ref, dqg_ref, dkvg_ref):
        @pl.when(pl.program_id(0) == 0)
        def _():
            dlng_ref[...] = jnp.zeros_like(dlng_ref)
            dqg_ref[...] = jnp.zeros_like(dqg_ref)
            dkvg_ref[...] = jnp.zeros_like(dkvg_ref)

        c, s1, s2 = c_ref[...], s1_ref[...], s2_ref[...]
        dkpe = jnp.zeros((tm, LANES), F32)
        for hd in range(N_HEADS):
            sl = slice(hd * HEAD_LANES, (hd + 1) * HEAD_LANES)
            dqpre_ref[:, sl] = _rope_bwd(dq_ref[:, sl], c, s1, s2).astype(BF16)
            dk_h = dk_ref[:, sl]
            dkpe = dkpe + dk_h
            dkv_ref[:, sl] = dk_h.astype(BF16)
        dkv_ref[:, N_HEADS * HEAD_LANES:] = dv_ref[...].astype(BF16)
        lane = lax.broadcasted_iota(jnp.int32, (tm, LANES), 1)
        rope_lanes = jnp.logical_and(lane >= ROPE_LO, lane < ROPE_LO + ROPE)
        dkr = jnp.where(rope_lanes, _rope_bwd(dkpe, c, s1, s2), 0.0)

        def norm_bwd(v, g, dy):
            r = lax.rsqrt(jnp.mean(v * v, axis=-1, keepdims=True) + EPS)
            vh = v * r
            dvh = dy * g
            return jnp.sum(dy * vh, axis=0, keepdims=True), r * (dvh - vh * jnp.mean(dvh * vh, axis=-1, keepdims=True))

        dqn = _dot_nt(dqpre_ref[...], wq_ref[...])
        dg_q, dql = norm_bwd(lat_ref[:, :Q_RANK], qg_ref[...], dqn)
        dqg_ref[...] += dg_q
        dkn = _dot_nt(dkv_ref[...], wkv_ref[...])
        dg_kv, dkvl = norm_bwd(lat_ref[:, Q_RANK:Q_RANK + KV_RANK], kvg_ref[...], dkn)
        dkvg_ref[...] += dg_kv

        dproj_ref[:, P_GM:P_GM + MLA_WIDTH] = dgm_ref[...].astype(BF16)
        dproj_ref[:, P_HQ:P_HQ + HG_WIDTH] = dhq_ref[...].astype(BF16)
        dproj_ref[:, P_HF:P_HF + HG_WIDTH] = dhf_ref[...].astype(BF16)
        dproj_ref[:, P_HI:P_HI + HG_WIDTH] = dhi_ref[...].astype(BF16)
        dproj_ref[:, P_GH:P_GH + HG_WIDTH] = dgh_ref[...].astype(BF16)
        dproj_ref[:, P_QL:P_QL + Q_RANK] = dql.astype(BF16)
        dproj_ref[:, P_KVL:P_KVL + KV_RANK] = dkvl.astype(BF16)
        dproj_ref[:, P_KR:P_KR + LANES] = dkr.astype(BF16)
        dh = _dot_nt(dproj_ref[...], win_ref[...])
        dg_ln, dxn = norm_bwd(x_ref[...], lng_ref[...], dh)
        dlng_ref[...] += dg_ln
        dx_ref[...] = dx2_ref[...] + dxn

    def row(w, cb=0):
        return pl.BlockSpec((tm, w), lambda i: (i, cb))

    hl = N_HEADS * HEAD_LANES
    outs = [(D_MODEL, F32), (D_PERM, BF16), (hl, BF16), (hl + MLA_WIDTH, BF16)]
    small = [(1, D_MODEL), (1, Q_RANK), (1, KV_RANK)]
    return pl.pallas_call(
        body, name="bot", grid=(T // tm,),
        in_specs=[row(D_MODEL), row(D_MODEL), row(lat_w, P_QL // lat_w), row(hl), row(hl), row(MLA_WIDTH),
                  row(MLA_WIDTH), row(HG_WIDTH), row(HG_WIDTH), row(HG_WIDTH), row(HG_WIDTH),
                  row(LANES), row(LANES), row(LANES),
                  _full((D_MODEL, D_PERM)), _full((Q_RANK, hl)), _full((KV_RANK, hl + MLA_WIDTH)),
                  _full((1, D_MODEL)), _full((1, Q_RANK)), _full((1, KV_RANK))],
        out_specs=[row(w) for w, _ in outs] + [_full(s) for s in small],
        out_shape=[jax.ShapeDtypeStruct((T, w), dt) for w, dt in outs] + [jax.ShapeDtypeStruct(s, F32) for s in small],
        compiler_params=_params(1),
    )(x, dx2, proj, dq, dk, dv, dgm, dhq, dhf, dhi, dgh, c_t, s1_t, s2_t, w_in_p, w_q_p, w_kv_p, ln_g, q_g, kv_g)


def _matmul_tn(a, b, bn, bt, name):
    T, M = a.shape
    N = b.shape[1]

    def body(a_ref, b_ref, o_ref):
        @pl.when(pl.program_id(1) == 0)
        def _():
            o_ref[...] = jnp.zeros_like(o_ref)

        o_ref[...] += _dot_tn(a_ref[...], b_ref[...])

    return pl.pallas_call(
        body, name=name, grid=(N // bn, T // bt),
        in_specs=[pl.BlockSpec((bt, M), lambda n, t: (t, 0)), pl.BlockSpec((bt, bn), lambda n, t: (t, n))],
        out_specs=pl.BlockSpec((M, bn), lambda n, t: (0, n)),
        out_shape=jax.ShapeDtypeStruct((M, N), F32),
        compiler_params=_params(2),
    )(a, b)


def _reduce_scatter(slabs, small):
    n = len(slabs)

    def body(*refs):
        ins, small_ref = refs[:n], refs[n]
        outs, small_out = refs[n + 1:2 * n + 1], refs[2 * n + 1]
        land = refs[2 * n + 2:3 * n + 2]
        small_land = refs[3 * n + 2]
        send_sems, recv_sems, local_sems = refs[3 * n + 3:3 * n + 6]
        x, y, c = lax.axis_index("x"), lax.axis_index("y"), lax.axis_index("c")
        me = 4 * x + 2 * y + c

        def peer(k):
            px = 1 - x if k & 4 else x
            py = 1 - y if k & 2 else y
            pc = 1 - c if k & 1 else c
            return (px, py, pc), 4 * px + 2 * py + pc

        def copy(a, k, receiving):
            dev, pidx = peer(k)
            src = small_ref if a == n else ins[a].at[pidx]
            dst_all = small_land if a == n else land[a]
            dst = dst_all.at[pidx if receiving else me]
            return pltpu.make_async_remote_copy(src_ref=src, dst_ref=dst, send_sem=send_sems.at[a, k - 1],
                                                recv_sem=recv_sems.at[a, k - 1], device_id=dev, device_id_type=MESH)

        own = [pltpu.make_async_copy(ins[a].at[me], land[a].at[me], local_sems.at[a]) for a in range(n)]
        for cp in own:
            cp.start()
        small_land[me] = small_ref[...]
        sends = [copy(a, k, False) for k in range(1, N_DEV) for a in range(n + 1)]
        for cp in sends:
            cp.start()
        for cp in own:
            cp.wait()
        for k in range(1, N_DEV):
            for a in range(n + 1):
                copy(a, k, True).wait_recv()
        for a in range(n):
            acc = land[a][0]
            for d in range(1, N_DEV):
                acc = acc + land[a][d]
            outs[a][...] = acc
        acc = small_land[0]
        for d in range(1, N_DEV):
            acc = acc + small_land[d]
        small_out[...] = acc
        for cp in sends:
            cp.wait_send()

    vm = pl.BlockSpec(memory_space=pltpu.VMEM)
    hbm = pl.BlockSpec(memory_space=pl.ANY)
    return pl.pallas_call(
        body, name="reduce_scatter_grads",
        in_specs=[hbm] * n + [vm], out_specs=[vm] * (n + 1),
        out_shape=[jax.ShapeDtypeStruct(s.shape[1:], F32) for s in slabs] + [jax.ShapeDtypeStruct(small.shape, F32)],
        scratch_shapes=[pltpu.VMEM(s.shape, F32) for s in slabs] + [pltpu.VMEM((N_DEV,) + small.shape, F32)]
        + [pltpu.SemaphoreType.DMA((n + 1, N_DEV - 1)), pltpu.SemaphoreType.DMA((n + 1, N_DEV - 1)),
           pltpu.SemaphoreType.DMA((n,))],
        compiler_params=pltpu.CompilerParams(vmem_limit_bytes=VMEM_LIMIT),
    )(*slabs, small)


def _adamw_math(w, g, m, v):
    m = ADAM_B1 * m + (1.0 - ADAM_B1) * g
    v = ADAM_B2 * v + (1.0 - ADAM_B2) * (g * g)
    m_hat = m / (1.0 - ADAM_B1 ** ADAM_STEP)
    v_hat = v / (1.0 - ADAM_B2 ** ADAM_STEP)
    delta = -ADAM_LR * (m_hat / (jnp.sqrt(v_hat) + ADAM_EPS) + ADAM_WD * w)
    return delta, m, v


SMALL_W = 512


def _adamw(big, small_w, small_g):
    nb, ns = len(big), len(small_w)

    def body(*refs):
        k = 0
        big_in = [refs[4 * i:4 * i + 4] for i in range(nb)]
        k = 4 * nb
        small_in = [refs[k + 3 * i:k + 3 * i + 3] for i in range(ns)]
        k += 3 * ns
        sg_ref = refs[k]
        k += 1
        big_out = [refs[k + 3 * i:k + 3 * i + 3] for i in range(nb)]
        k += 3 * nb
        small_out = [refs[k + 4 * i:k + 4 * i + 4] for i in range(ns)]

        for (w, g, m, v), (od, om, ov) in zip(big_in, big_out):
            od[...], om[...], ov[...] = _adamw_math(w[...], g[...], m[...], v[...])

        sg = sg_ref[...]
        lbp = small_in[2][0][...]
        lb = _lower_bound(lbp)
        t = sg[4:5, :] * lb * (1.0 - lb)
        grads = [jnp.concatenate([sg[0:1, :], sg[1:2, :]], axis=1),
                 jnp.concatenate([sg[2:3, :], sg[3:4, :]], axis=1),
                 jnp.concatenate([t, -t], axis=0),
                 sg[6:7, :], sg[7:8, 0:Q_RANK], sg[7:8, Q_RANK:Q_RANK + KV_RANK]]
        for (w, m, v), g, (og, od, om, ov) in zip(small_in, grads, small_out):
            og[...] = g
            od[...], om[...], ov[...] = _adamw_math(w[...], g, m[...], v[...])

    ins = [a for grp in big for a in grp] + [a for grp in small_w for a in grp] + [small_g]
    out_shape = ([jax.ShapeDtypeStruct(grp[0].shape, F32) for grp in big for _ in range(3)]
                 + [jax.ShapeDtypeStruct(grp[0].shape, F32) for grp in small_w for _ in range(4)])
    vm = pl.BlockSpec(memory_space=pltpu.VMEM)
    res = pl.pallas_call(
        body, name="adamw", in_specs=[vm] * len(ins), out_specs=[vm] * len(out_shape), out_shape=out_shape,
        compiler_params=pltpu.CompilerParams(vmem_limit_bytes=VMEM_LIMIT),
    )(*ins)
    big_res = [res[3 * i:3 * i + 3] for i in range(nb)]
    small_res = [res[3 * nb + 4 * i:3 * nb + 4 * i + 4] for i in range(ns)]
    return big_res, small_res


def _perm_weights(g_in, g_q, g_kv, g_out):
    w = g_in.transpose(1, 0, 2).reshape(D_MODEL, D_IN)
    z = lambda n: jnp.zeros((D_MODEL, n), BF16)
    w_in_p = jnp.concatenate([w[:, 416:], w[:, :384], z(64), w[:, 384:416], z(32)], axis=1)
    wq = g_q.transpose(1, 0, 2)
    w_q_p = jnp.pad(wq, ((0, 0), (0, 0), (0, HEAD_LANES - NOPE - ROPE))).reshape(Q_RANK, N_HEADS * HEAD_LANES)
    wkv = g_kv.transpose(1, 0, 2)
    wk = jnp.pad(wkv[:, :, :NOPE], ((0, 0), (0, 0), (0, HEAD_LANES - NOPE))).reshape(KV_RANK, N_HEADS * HEAD_LANES)
    wv = wkv[:, :, NOPE:].reshape(KV_RANK, MLA_WIDTH)
    return w_in_p, w_q_p, jnp.concatenate([wk, wv], axis=1), g_out.reshape(D_MODEL, D_MODEL)


def _grad_slabs(dw_in_p, dw_q_p, dw_kv_p, dw_out):
    dw_in = jnp.concatenate([dw_in_p[:, P_QL:P_KR], dw_in_p[:, P_KR + ROPE_LO:P_KR + ROPE_LO + ROPE], dw_in_p[:, :P_QL]], axis=1)
    s_in = dw_in.reshape(D_MODEL, N_DEV, D_IN // N_DEV).transpose(1, 0, 2)
    s_q = dw_q_p.reshape(Q_RANK, N_HEADS, HEAD_LANES)[:, :, :NOPE + ROPE].transpose(1, 0, 2)
    hl = N_HEADS * HEAD_LANES
    dk = dw_kv_p[:, :hl].reshape(KV_RANK, N_HEADS, HEAD_LANES)[:, :, :NOPE]
    dv = dw_kv_p[:, hl:].reshape(KV_RANK, N_HEADS, V_DIM)
    s_kv = jnp.concatenate([dk, dv], axis=2).transpose(1, 0, 2)
    return s_in, s_q, s_kv, dw_out.reshape(N_DEV, D_MODEL // N_DEV, D_MODEL)


def _block_sizes(T):
    return min(256, T), min(256, T), min(512, T)


def kernel(x, positions, ln_g, w_in, q_a_norm_g, w_q_b, kv_a_norm_g, w_kv_b, hg_lower_bounds, hg_norm_g, w_out, final_norm_g, loss_target, m_ln_g, m_w_in, m_q_a_norm_g, m_w_q_b, m_kv_a_norm_g, m_w_kv_b, m_hg_lower_bounds, m_hg_norm_g, m_w_out, m_final_norm_g, v_ln_g, v_w_in, v_q_a_norm_g, v_w_q_b, v_kv_a_norm_g, v_w_kv_b, v_hg_lower_bounds, v_hg_norm_g, v_w_out, v_final_norm_g):
    T = x.shape[1]
    tm, tq, bt = _block_sizes(T)
    nq = T // tq
    xs, tgt = x[0], loss_target[0]
    pos_f = positions[0].astype(F32).reshape(T, 1)
    fng = final_norm_g.reshape(1, D_MODEL)

    gathered = _all_gather_weights([w_in[0], w_q_b[0], w_kv_b[0], w_out[0]])
    w_in_p, w_q_p, w_kv_p, w_out_b = _perm_weights(*gathered)

    c_t, s1_t, s2_t = _rope_tables(pos_f, tm)
    proj, h, qn, kvn, q, k, v = _fwd_in(xs, ln_g, w_in_p, q_a_norm_g, w_q_p, kv_a_norm_g, w_kv_p, c_t, s1_t, s2_t, tm)
    v_t = v.reshape(nq, tq, MLA_WIDTH).transpose(0, 2, 1)
    k_t = k.reshape(nq, tq, N_HEADS * HEAD_LANES).transpose(0, 2, 1)
    o_mla, lse = _attn_fwd(q, k, v_t, tq)
    o_hg, states = _hgrn_fwd(proj, hg_lower_bounds)
    dx2, dx2b, ycat, d_om, dsum, d_gm, d_oh, d_gh, loss_p, d_fng, d_hgn = _top(
        xs, tgt, o_mla, o_hg, proj, w_out_b, hg_norm_g, fng, tm)
    dsum = dsum[:, :N_HEADS].T.reshape(N_HEADS, nq, 1, tq)
    hl = N_HEADS * HEAD_LANES
    q_t = q.reshape(nq, tq, hl).transpose(0, 2, 1)
    do_t = d_om.reshape(nq, tq, hl).transpose(0, 2, 1)
    dq_t, dk, dv = _attn_bwd(q, k, v, q_t, k_t, d_om, do_t, lse, dsum, tq)
    dq = dq_t.transpose(0, 2, 1).reshape(T, N_HEADS * HEAD_LANES)
    d_hq, d_hf, d_hi, d_lb = _hgrn_bwd(proj, hg_lower_bounds, d_oh, states)
    dx, dproj, dq_pre, dkv, d_lng, d_qg, d_kvg = _bot(
        xs, dx2, proj, dq, dk, dv, d_gm, d_hq, d_hf, d_hi, d_gh, c_t, s1_t, s2_t, w_in_p, w_q_p, w_kv_p,
        ln_g, q_a_norm_g, kv_a_norm_g, tm)
    dw_in_p = _matmul_tn(h, dproj, 512, bt, "dw_in")
    dw_out = _matmul_tn(ycat, dx2b, 512, bt, "dw_out")
    dw_q_p = _matmul_tn(qn, dq_pre, 512, bt, "dw_q_b")
    dw_kv_p = _matmul_tn(kvn, dkv, 512, bt, "dw_kv_b")

    zrow = jnp.zeros((1, SMALL_W), F32)
    small = jnp.concatenate([
        d_lng.reshape(2, SMALL_W), d_fng.reshape(2, SMALL_W), d_lb, zrow, d_hgn,
        jnp.concatenate([d_qg, d_kvg, jnp.zeros((1, SMALL_W - Q_RANK - KV_RANK), F32)], axis=1)], axis=0)
    g_in, g_q, g_kv, g_out, small_sum = _reduce_scatter(list(_grad_slabs(dw_in_p, dw_q_p, dw_kv_p, dw_out)), small)

    big = [(w_in[0], g_in, m_w_in[0], v_w_in[0]), (w_q_b[0], g_q, m_w_q_b[0], v_w_q_b[0]),
           (w_kv_b[0], g_kv, m_w_kv_b[0], v_w_kv_b[0]), (w_out[0], g_out, m_w_out[0], v_w_out[0])]
    small_w = [(ln_g, m_ln_g, v_ln_g),
               (fng, m_final_norm_g.reshape(1, D_MODEL), v_final_norm_g.reshape(1, D_MODEL)),
               (hg_lower_bounds, m_hg_lower_bounds, v_hg_lower_bounds), (hg_norm_g, m_hg_norm_g, v_hg_norm_g),
               (q_a_norm_g, m_q_a_norm_g, v_q_a_norm_g), (kv_a_norm_g, m_kv_a_norm_g, v_kv_a_norm_g)]
    big_res, small_res = _adamw(big, small_w, small_sum)

    loss = lax.psum(loss_p[0, 0], ("x", "y", "c"))
    (r_in, r_q, r_kv, r_out) = big_res
    (s_ln, s_fn, s_lb, s_hgn, s_qg, s_kvg) = small_res
    flat = lambda t: t.reshape(D_MODEL)
    lead = lambda t: t[None]
    grads = [s_ln[0], lead(g_in), s_qg[0], lead(g_q), s_kvg[0], lead(g_kv), s_lb[0], s_hgn[0], lead(g_out), flat(s_fn[0])]

    def pick(i):
        return [s_ln[i + 1], lead(r_in[i]), s_qg[i + 1], lead(r_q[i]), s_kvg[i + 1], lead(r_kv[i]), s_lb[i + 1],
                s_hgn[i + 1], lead(r_out[i]), flat(s_fn[i + 1])]

    return (loss, dx[None], *grads, *pick(0), *pick(1), *pick(2))
```

```python
import math

import numpy as np
import jax
import jax.numpy as jnp
from jax import lax
from jax.experimental import pallas as pl
from jax.experimental.pallas import tpu as pltpu

F32 = jnp.float32
BF16 = jnp.bfloat16

D_MODEL = 1024
N_HEADS = 8
NOPE = 64
ROPE = 32
HALF_ROPE = ROPE // 2
V_DIM = 64
Q_RANK = 256
KV_RANK = 128
MLA_WIDTH = N_HEADS * V_DIM
HG_HEADS = 4
HG_DIM = 128
HG_WIDTH = HG_HEADS * HG_DIM
CHUNK = 64
SUB = 16
D_IN = 2976
D_PERM = 3072
ROPE_THETA = 10000.0
EPS = 1e-6
N_DEV = 8
LANES = 128
HEAD_LANES = 128

P_GM, P_HQ, P_HF, P_HI, P_GH, P_QL, P_KVL, P_KR = 0, 512, 1024, 1536, 2048, 2560, 2816, 2944
ROPE_LO = NOPE
SCALE = 1.0 / math.sqrt(NOPE + ROPE)

ADAM_LR = 0.001
ADAM_B1 = 0.9
ADAM_B2 = 0.999
ADAM_EPS = 1e-08
ADAM_WD = 0.01
ADAM_STEP = 10

VMEM_LIMIT = 56 * 1024 * 1024
MESH = pl.DeviceIdType.MESH

NT = (((1,), (1,)), ((), ()))
TN = (((0,), (0,)), ((), ()))


def _params(n_grid=0, **kw):
    sem = ("arbitrary",) * n_grid if n_grid else None
    return pltpu.CompilerParams(dimension_semantics=sem, vmem_limit_bytes=VMEM_LIMIT, **kw)


def _dot(a, b):
    return jnp.dot(a, b, preferred_element_type=F32)


def _dot_nt(a, b):
    return lax.dot_general(a, b, NT, preferred_element_type=F32)


def _dot_tn(a, b):
    return lax.dot_general(a, b, TN, preferred_element_type=F32)


def _sigmoid(x):
    return 1.0 / (1.0 + jnp.exp(-x))


def _rope_fwd(x, c, s1, s2):
    return x * c + pltpu.roll(x, LANES - HALF_ROPE, 1) * s1 + pltpu.roll(x, HALF_ROPE, 1) * s2


def _rope_bwd(dy, c, s1, s2):
    return dy * c - pltpu.roll(dy, LANES - HALF_ROPE, 1) * s1 - pltpu.roll(dy, HALF_ROPE, 1) * s2


def _full(shape):
    n = len(shape)
    return pl.BlockSpec(shape, lambda *_: (0,) * n)


def _rope_tables(pos_f, tm):
    T = pos_f.shape[0]
    inv = (np.float32(ROPE_THETA) ** (-np.arange(HALF_ROPE, dtype=np.float32) / np.float32(HALF_ROPE))).astype(np.float32)
    inv1 = np.zeros((1, LANES), np.float32)
    inv2 = np.zeros((1, LANES), np.float32)
    inv1[0, ROPE_LO:ROPE_LO + HALF_ROPE] = inv
    inv2[0, ROPE_LO + HALF_ROPE:ROPE_LO + ROPE] = inv

    def body(pos_ref, inv1_ref, inv2_ref, c_ref, s1_ref, s2_ref):
        pos = pos_ref[...]
        a1 = pos * inv1_ref[...]
        a2 = pos * inv2_ref[...]
        c_ref[...] = jnp.cos(a1 + a2)
        s1_ref[...] = -jnp.sin(a1)
        s2_ref[...] = jnp.sin(a2)

    tab = jax.ShapeDtypeStruct((T, LANES), F32)
    return pl.pallas_call(
        body, name="rope_tables", grid=(T // tm,),
        in_specs=[pl.BlockSpec((tm, 1), lambda i: (i, 0)), _full((1, LANES)), _full((1, LANES))],
        out_specs=[pl.BlockSpec((tm, LANES), lambda i: (i, 0))] * 3,
        out_shape=[tab, tab, tab], compiler_params=_params(1),
    )(pos_f, jnp.asarray(inv1), jnp.asarray(inv2))


def _all_gather_weights(shards):
    n = len(shards)

    def body(*refs):
        ins, outs = refs[:n], refs[n:2 * n]
        send_sems, recv_sems = refs[2 * n], refs[2 * n + 1]
        x, y, c = lax.axis_index("x"), lax.axis_index("y"), lax.axis_index("c")
        me, sibling = (x, y, c), (x, y, 1 - c)
        chips = [(1 - x, y), (x, 1 - y), (1 - x, 1 - y)]

        def idx(d):
            return 4 * d[0] + 2 * d[1] + d[2]

        def copy(a, k, block, to):
            rows = outs[a].at[idx(block)]
            return pltpu.make_async_remote_copy(src_ref=rows, dst_ref=rows, send_sem=send_sems.at[a, k],
                                                recv_sem=recv_sems.at[a, k], device_id=to, device_id_type=MESH)

        for a in range(n):
            outs[a][idx(me)] = ins[a][...].astype(BF16)
        first = []
        for a in range(n):
            first.append(copy(a, 0, me, sibling))
            first += [copy(a, 1 + j, me, (*chip, c)) for j, chip in enumerate(chips)]
        for cp in first:
            cp.start()
        passed = []
        for j, chip in enumerate(chips):
            for a in range(n):
                copy(a, 1 + j, (*chip, c), me).wait_recv()
                cp = copy(a, 4 + j, (*chip, c), sibling)
                cp.start()
                passed.append(cp)
        for a in range(n):
            copy(a, 0, sibling, me).wait_recv()
            for j, chip in enumerate(chips):
                copy(a, 4 + j, (*chip, 1 - c), me).wait_recv()
        for cp in first + passed:
            cp.wait_send()

    vm = pl.BlockSpec(memory_space=pltpu.VMEM)
    return pl.pallas_call(
        body, name="all_gather_weights",
        in_specs=[vm] * n, out_specs=[vm] * n,
        out_shape=[jax.ShapeDtypeStruct((N_DEV,) + s.shape, BF16) for s in shards],
        scratch_shapes=[pltpu.SemaphoreType.DMA((n, 7)), pltpu.SemaphoreType.DMA((n, 7))],
        compiler_params=pltpu.CompilerParams(vmem_limit_bytes=VMEM_LIMIT),
    )(*shards)


def _fwd_in(x, ln_g, w_in_p, q_g, w_q_p, kv_g, w_kv_p, c_t, s1_t, s2_t, tm):
    T = x.shape[0]

    def body(x_ref, lng_ref, win_ref, qg_ref, wq_ref, kvg_ref, wkv_ref, c_ref, s1_ref, s2_ref,
             proj_ref, h_ref, qn_ref, kvn_ref, q_ref, k_ref, v_ref):
        xv = x_ref[...]
        r = lax.rsqrt(jnp.mean(xv * xv, axis=-1, keepdims=True) + EPS)
        h = (xv * r * lng_ref[...]).astype(BF16)
        h_ref[...] = h
        proj = _dot(h, win_ref[...])
        proj_ref[...] = proj
        c, s1, s2 = c_ref[...], s1_ref[...], s2_ref[...]

        ql = proj[:, P_QL:P_QL + Q_RANK]
        rq = lax.rsqrt(jnp.mean(ql * ql, axis=-1, keepdims=True) + EPS)
        qn = (ql * rq * qg_ref[...]).astype(BF16)
        qn_ref[...] = qn
        q = _dot(qn, wq_ref[...])
        for hd in range(N_HEADS):
            sl = slice(hd * HEAD_LANES, (hd + 1) * HEAD_LANES)
            q_ref[:, sl] = _rope_fwd(q[:, sl], c, s1, s2).astype(BF16)

        kvl = proj[:, P_KVL:P_KVL + KV_RANK]
        rk = lax.rsqrt(jnp.mean(kvl * kvl, axis=-1, keepdims=True) + EPS)
        kvn = (kvl * rk * kvg_ref[...]).astype(BF16)
        kvn_ref[...] = kvn
        kv = _dot(kvn, wkv_ref[...])
        kpe = _rope_fwd(proj[:, P_KR:P_KR + LANES], c, s1, s2)
        for hd in range(N_HEADS):
            sl = slice(hd * HEAD_LANES, (hd + 1) * HEAD_LANES)
            k_ref[:, sl] = (kv[:, sl] + kpe).astype(BF16)
        v_ref[...] = kv[:, N_HEADS * HEAD_LANES:].astype(BF16)

    def row(w):
        return pl.BlockSpec((tm, w), lambda i: (i, 0))

    outs = [(D_PERM, F32), (D_MODEL, BF16), (Q_RANK, BF16), (KV_RANK, BF16),
            (N_HEADS * HEAD_LANES, BF16), (N_HEADS * HEAD_LANES, BF16), (MLA_WIDTH, BF16)]
    return pl.pallas_call(
        body, name="fwd_in", grid=(T // tm,),
        in_specs=[row(D_MODEL), _full((1, D_MODEL)), _full((D_MODEL, D_PERM)), _full((1, Q_RANK)),
                  _full((Q_RANK, N_HEADS * HEAD_LANES)), _full((1, KV_RANK)),
                  _full((KV_RANK, N_HEADS * HEAD_LANES + MLA_WIDTH)), row(LANES), row(LANES), row(LANES)],
        out_specs=[row(w) for w, _ in outs],
        out_shape=[jax.ShapeDtypeStruct((T, w), dt) for w, dt in outs],
        compiler_params=_params(1),
    )(x, ln_g, w_in_p, q_g, w_q_p, kv_g, w_kv_p, c_t, s1_t, s2_t)


LOG2E = 1.4426950408889634
SCALE2 = SCALE * LOG2E


def _causal(tq):
    r = lax.broadcasted_iota(jnp.int32, (tq, tq), 0)
    c = lax.broadcasted_iota(jnp.int32, (tq, tq), 1)
    return r <= c


def _attn_fwd(q, k, v_t, tq):
    T = q.shape[0]
    nq = T // tq

    def body(q_ref, k_ref, vt_ref, o_ref, lse_ref, sa_ref, sb_ref):
        i = pl.program_id(1)
        qh = [q_ref[:, hh * HEAD_LANES:(hh + 1) * HEAD_LANES] for hh in range(2)]

        def scores(j, s_ref):
            kj = k_ref[pl.ds(pl.multiple_of(j * tq, tq), tq), :]
            for hh in range(2):
                s_ref[hh] = _dot_nt(kj[:, hh * HEAD_LANES:(hh + 1) * HEAD_LANES], qh[hh])

        def tile(j, stats, s_ref, masked):
            vt = vt_ref[j]
            out = []
            for hh in range(2):
                m, l, acc = stats[3 * hh:3 * hh + 3]
                s = s_ref[hh] * SCALE2
                if masked:
                    s = jnp.where(_causal(tq), s, -jnp.inf)
                m_new = jnp.maximum(m, jnp.max(s, axis=0, keepdims=True))
                alpha = jnp.exp2(m - m_new)
                p = jnp.exp2(s - m_new)
                out += [m_new, alpha * l + jnp.sum(p, axis=0, keepdims=True), alpha * acc + _dot(vt, p.astype(BF16))]
            return tuple(out)

        def pair(r, stats):
            scores(2 * r + 1, sb_ref)
            stats = tile(2 * r, stats, sa_ref, False)
            scores(2 * r + 2, sa_ref)
            return tile(2 * r + 1, stats, sb_ref, False)

        def last_even(stats):
            return tile(i, stats, sa_ref, True)

        def last_odd(stats):
            scores(i, sb_ref)
            return tile(i, tile(i - 1, stats, sa_ref, False), sb_ref, True)

        init = (jnp.full((1, tq), -1e30, F32), jnp.zeros((1, tq), F32), jnp.zeros((LANES, tq), F32)) * 2
        scores(0, sa_ref)
        stats = lax.fori_loop(0, i // 2, pair, init)
        m0, l0, a0, m1, l1, a1 = lax.cond(i % 2 == 0, last_even, last_odd, stats)
        first = lax.broadcasted_iota(jnp.int32, (LANES, tq), 0) < V_DIM
        o_ref[...] = jnp.where(first, a0 / l0, a1 / l1).T
        lse_ref[0, 0] = m0 + jnp.log2(l0)
        lse_ref[1, 0] = m1 + jnp.log2(l1)

    return pl.pallas_call(
        body, name="attn_fwd", grid=(N_HEADS // 2, nq),
        in_specs=[pl.BlockSpec((tq, 2 * HEAD_LANES), lambda p, i: (i, p)),
                  pl.BlockSpec((T, 2 * HEAD_LANES), lambda p, i: (0, p)),
                  pl.BlockSpec((nq, LANES, tq), lambda p, i: (0, p, 0))],
        out_specs=[pl.BlockSpec((tq, LANES), lambda p, i: (i, p)),
                   pl.BlockSpec((2, 1, 1, tq), lambda p, i: (p, i, 0, 0))],
        out_shape=[jax.ShapeDtypeStruct((T, MLA_WIDTH), F32), jax.ShapeDtypeStruct((N_HEADS, nq, 1, tq), F32)],
        scratch_shapes=[pltpu.VMEM((2, tq, tq), F32), pltpu.VMEM((2, tq, tq), F32)],
        compiler_params=_params(2),
    )(q, k, v_t)


def _attn_bwd(q, k, v, q_t, k_t, do_m, do_t, lse, dsum, tq):
    T = q.shape[0]
    nq = T // tq

    def body(q_ref, k_ref, v_ref, qt_ref, kt_ref, do_ref, dot_ref, lse_ref, dsum_ref, dqt_ref, dk_ref, dv_ref,
             ba_ref, bb_ref, dkt_ref, dvt_ref):
        j = pl.program_id(1)

        @pl.when(j == 0)
        def _():
            dqt_ref[...] = jnp.zeros_like(dqt_ref)

        dkt_ref[...] = jnp.zeros_like(dkt_ref)
        dvt_ref[...] = jnp.zeros_like(dvt_ref)
        heads = [slice(hh * HEAD_LANES, (hh + 1) * HEAD_LANES) for hh in range(2)]

        def rows_of(t):
            i = nq - 1 - t
            return i, pl.ds(pl.multiple_of(i * tq, tq), tq)

        def products(t, buf):
            _, rows = rows_of(t)
            for hh, sl in enumerate(heads):
                buf[hh] = _dot_nt(k_ref[:, sl], q_ref[rows, sl])
                buf[2 + hh] = _dot_nt(v_ref[...], do_ref[rows, sl])

        def tile(t, buf, masked):
            i, rows = rows_of(t)
            dv_new = None
            for hh, sl in enumerate(heads):
                s = buf[hh] * SCALE2
                if masked:
                    s = jnp.where(_causal(tq), s, -jnp.inf)
                p = jnp.exp2(s - lse_ref[hh, i])
                ds = (p * (buf[2 + hh] - dsum_ref[hh, i]) * SCALE).astype(BF16)
                dv_h = _dot_nt(dot_ref[i, sl, :], p.astype(BF16))
                dv_new = dv_h if dv_new is None else dv_new + dv_h
                dkt_ref[sl, :] += _dot_nt(qt_ref[i, sl, :], ds)
                dqt_ref[i, sl, :] += _dot(kt_ref[0, sl, :], ds)
            dvt_ref[...] += dv_new

        n_plain = nq - 1 - j
        products(0, ba_ref)

        def pair(r, carry):
            products(2 * r + 1, bb_ref)
            tile(2 * r, ba_ref, False)
            products(2 * r + 2, ba_ref)
            tile(2 * r + 1, bb_ref, False)
            return carry

        lax.fori_loop(0, n_plain // 2, pair, 0)

        @pl.when(n_plain % 2 == 0)
        def _():
            tile(n_plain, ba_ref, True)

        @pl.when(n_plain % 2 == 1)
        def _():
            products(n_plain, bb_ref)
            tile(n_plain - 1, ba_ref, False)
            tile(n_plain, bb_ref, True)

        dk_ref[...] = dkt_ref[...].T
        dv_ref[...] = dvt_ref[...].T

    stat = pl.BlockSpec((2, nq, 1, tq), lambda p, j: (p, 0, 0, 0))
    blocks_t = pl.BlockSpec((nq, 2 * HEAD_LANES, tq), lambda p, j: (0, p, 0))
    return pl.pallas_call(
        body, name="attn_bwd", grid=(N_HEADS // 2, nq),
        in_specs=[pl.BlockSpec((T, 2 * HEAD_LANES), lambda p, j: (0, p)),
                  pl.BlockSpec((tq, 2 * HEAD_LANES), lambda p, j: (j, p)),
                  pl.BlockSpec((tq, LANES), lambda p, j: (j, p)),
                  blocks_t,
                  pl.BlockSpec((1, 2 * HEAD_LANES, tq), lambda p, j: (j, p, 0)),
                  pl.BlockSpec((T, 2 * HEAD_LANES), lambda p, j: (0, p)), blocks_t, stat, stat],
        out_specs=[blocks_t,
                   pl.BlockSpec((tq, 2 * HEAD_LANES), lambda p, j: (j, p)),
                   pl.BlockSpec((tq, LANES), lambda p, j: (j, p))],
        out_shape=[jax.ShapeDtypeStruct((nq, N_HEADS * HEAD_LANES, tq), F32),
                   jax.ShapeDtypeStruct((T, N_HEADS * HEAD_LANES), F32),
                   jax.ShapeDtypeStruct((T, MLA_WIDTH), F32)],
        scratch_shapes=[pltpu.VMEM((4, tq, tq), F32), pltpu.VMEM((4, tq, tq), F32),
                        pltpu.VMEM((2 * HEAD_LANES, tq), F32), pltpu.VMEM((LANES, tq), F32)],
        compiler_params=_params(2),
    )(q, k, v, q_t, k_t, do_m, do_t, lse, dsum)


def _lower_bound(lbp):
    a, b = lbp[0:1, :], lbp[1:2, :]
    mx = jnp.maximum(a, b)
    ea, eb = jnp.exp(a - mx), jnp.exp(b - mx)
    return ea / (ea + eb)


def _tri(lower):
    r = lax.broadcasted_iota(jnp.int32, (CHUNK, CHUNK), 0)
    c = lax.broadcasted_iota(jnp.int32, (CHUNK, CHUNK), 1)
    return (c <= r) if lower else (c >= r)


def _hg_gates(hq, hf, lb):
    sq = _sigmoid(hq)
    sf = _sigmoid(hf)
    f = lb + (1.0 - lb) * sf
    g = jnp.log(f)
    gcum = jnp.dot(_tri(True).astype(F32), g, precision=lax.Precision.HIGHEST, preferred_element_type=F32)
    return sq, sf, f, hq * sq, 1.0 - f, gcum


def _hg_blocks(q, kk, gcum):
    rowi = lax.broadcasted_iota(jnp.int32, (CHUNK, HG_DIM), 0)
    out = []
    for blk in range(CHUNK // SUB):
        lo, hi = blk * SUB, (blk + 1) * SUB
        gb = gcum[lo - 1:lo, :] if blk else jnp.zeros((1, HG_DIM), F32)
        eq = jnp.exp(gcum[lo:hi, :] - gb)
        ek = jnp.exp(jnp.where(rowi < hi, gb - gcum, 0.0))
        out.append((eq, ek, (q[lo:hi, :] * eq).astype(BF16), (kk * ek).astype(BF16)))
    return out


def _hg_scores(blocks):
    a = jnp.concatenate([_dot_nt(qb, kb) for _, _, qb, kb in blocks], axis=0)
    return jnp.where(_tri(True), a, 0.0)


def _hgrn_fwd(proj, lbp):
    T = proj.shape[0]
    nc = T // CHUNK

    def body(hq_ref, hf_ref, hi_ref, lbp_ref, o_ref, st_ref, state):
        @pl.when(pl.program_id(0) == 0)
        def _():
            state[...] = jnp.zeros_like(state)

        lb_all = _lower_bound(lbp_ref[...])
        for hd in range(HG_HEADS):
            sl = slice(hd * HG_DIM, (hd + 1) * HG_DIM)
            _, _, _, q, kk, gcum = _hg_gates(hq_ref[:, sl], hf_ref[:, sl], lb_all[:, sl])
            vb = hi_ref[:, sl].astype(BF16)
            st = state[hd]
            st_ref[0, hd] = st
            a = _hg_scores(_hg_blocks(q, kk, gcum))
            gend = gcum[CHUNK - 1:CHUNK, :]
            o = _dot(a.astype(BF16), vb) + _dot_nt((q * jnp.exp(gcum)).astype(BF16), st.astype(BF16))
            o_ref[:, sl] = o
            kge = (kk * jnp.exp(gend - gcum)).astype(BF16)
            state[hd] = st * jnp.exp(gend) + _dot_tn(vb, kge)

    def col(cb):
        return pl.BlockSpec((CHUNK, HG_WIDTH), lambda i: (i, cb))

    return pl.pallas_call(
        body, name="hgrn_fwd", grid=(nc,),
        in_specs=[col(P_HQ // HG_WIDTH), col(P_HF // HG_WIDTH), col(P_HI // HG_WIDTH), _full((2, HG_WIDTH))],
        out_specs=[pl.BlockSpec((CHUNK, HG_WIDTH), lambda i: (i, 0)),
                   pl.BlockSpec((1, HG_HEADS, HG_DIM, HG_DIM), lambda i: (i, 0, 0, 0))],
        out_shape=[jax.ShapeDtypeStruct((T, HG_WIDTH), F32),
                   jax.ShapeDtypeStruct((nc, HG_HEADS, HG_DIM, HG_DIM), F32)],
        scratch_shapes=[pltpu.VMEM((HG_HEADS, HG_DIM, HG_DIM), F32)],
        compiler_params=_params(1),
    )(proj, proj, proj, lbp)


def _hgrn_bwd(proj, lbp, do_hg, states):
    T = proj.shape[0]
    nc = T // CHUNK

    def body(hq_ref, hf_ref, hi_ref, lbp_ref, do_ref, st_ref, dhq_ref, dhf_ref, dhi_ref, dlb_ref, dstate):
        @pl.when(pl.program_id(0) == 0)
        def _():
            dstate[...] = jnp.zeros_like(dstate)
            dlb_ref[...] = jnp.zeros_like(dlb_ref)

        lb_all = _lower_bound(lbp_ref[...])
        for hd in range(HG_HEADS):
            sl = slice(hd * HG_DIM, (hd + 1) * HG_DIM)
            hq, lb = hq_ref[:, sl], lb_all[:, sl]
            sq, sf, f, q, kk, gcum = _hg_gates(hq, hf_ref[:, sl], lb)
            vb = hi_ref[:, sl].astype(BF16)
            dob = do_ref[:, sl].astype(BF16)
            st = st_ref[0, hd]
            dst = dstate[hd]
            dstb = dst.astype(BF16)
            blocks = _hg_blocks(q, kk, gcum)
            a = _hg_scores(blocks)
            gend = gcum[CHUNK - 1:CHUNK, :]
            eg, egend, ekend = jnp.exp(gcum), jnp.exp(gend), jnp.exp(gend - gcum)
            qg, kge = q * eg, kk * ekend
            kgeb = kge.astype(BF16)

            dv = _dot_tn(a.astype(BF16), dob) + _dot_nt(kgeb, dstb)
            da = jnp.where(_tri(True), _dot_nt(dob, vb), 0.0).astype(BF16)
            dqg = _dot(dob, st.astype(BF16))
            dkge = _dot(vb, dstb)
            dgend = jnp.sum(st * dst, axis=0, keepdims=True) * egend + jnp.sum(dkge * kge, axis=0, keepdims=True)
            dstate[hd] = _dot_tn(dob, qg.astype(BF16)) + dst * egend

            dq_a, dg_q = [], []
            dk_a, dg_k = jnp.zeros((CHUNK, HG_DIM), F32), jnp.zeros((CHUNK, HG_DIM), F32)
            for blk, (eq, ek, qb, kb) in enumerate(blocks):
                da_blk = da[blk * SUB:(blk + 1) * SUB, :]
                dq_blk, dk_blk = _dot(da_blk, kb), _dot_tn(da_blk, qb)
                dq_a.append(dq_blk * eq)
                dk_a = dk_a + dk_blk * ek
                dg_q.append(qb.astype(F32) * dq_blk)
                dg_k = dg_k + kb.astype(F32) * dk_blk
            dq_a = jnp.concatenate(dq_a, axis=0)

            dq = dq_a + dqg * eg
            dk = dk_a + dkge * ekend
            dgc = jnp.concatenate(dg_q, axis=0) - dg_k + dqg * qg - dkge * kge
            last = lax.broadcasted_iota(jnp.int32, (CHUNK, HG_DIM), 0) == CHUNK - 1
            dgc = dgc + jnp.where(last, dgend, 0.0)
            dg = jnp.dot(_tri(False).astype(F32), dgc, precision=lax.Precision.HIGHEST, preferred_element_type=F32)
            df = dg / f - dk
            dhf_ref[:, sl] = df * (1.0 - lb) * sf * (1.0 - sf)
            dlb_ref[:, sl] += jnp.sum(df * (1.0 - sf), axis=0, keepdims=True)
            dhq_ref[:, sl] = dq * (sq * (1.0 + hq * (1.0 - sq)))
            dhi_ref[:, sl] = dv

    def col(cb):
        return pl.BlockSpec((CHUNK, HG_WIDTH), lambda i: (nc - 1 - i, cb))

    grad = jax.ShapeDtypeStruct((T, HG_WIDTH), F32)
    return pl.pallas_call(
        body, name="hgrn_bwd", grid=(nc,),
        in_specs=[col(P_HQ // HG_WIDTH), col(P_HF // HG_WIDTH), col(P_HI // HG_WIDTH), _full((2, HG_WIDTH)),
                  col(0), pl.BlockSpec((1, HG_HEADS, HG_DIM, HG_DIM), lambda i: (nc - 1 - i, 0, 0, 0))],
        out_specs=[col(0), col(0), col(0), _full((1, HG_WIDTH))],
        out_shape=[grad, grad, grad, jax.ShapeDtypeStruct((1, HG_WIDTH), F32)],
        scratch_shapes=[pltpu.VMEM((HG_HEADS, HG_DIM, HG_DIM), F32)],
        compiler_params=_params(1),
    )(proj, proj, proj, lbp, do_hg, states)


def _top(x, tgt, o_mla, o_hg, proj, w_out, hg_norm_g, final_g, tm):
    T = x.shape[0]

    def body(x_ref, tgt_ref, om_ref, oh_ref, gm_ref, gh_ref, wout_ref, hgn_ref, fng_ref,
             dx2_ref, dx2b_ref, ycat_ref, dom_ref, dsum_ref, dgm_ref, doh_ref, dgh_ref, loss_ref, dfng_ref, dhgn_ref):
        @pl.when(pl.program_id(0) == 0)
        def _():
            loss_ref[...] = jnp.zeros_like(loss_ref)
            dfng_ref[...] = jnp.zeros_like(dfng_ref)
            dhgn_ref[...] = jnp.zeros_like(dhgn_ref)

        gm, om = gm_ref[...], om_ref[...]
        sgm = _sigmoid(gm)
        silu_m = gm * sgm
        gh, oh, gam = gh_ref[...], oh_ref[...], hgn_ref[...]
        sgh = _sigmoid(gh)
        silu_h = gh * sgh
        rr, nn = [], []
        for hd in range(HG_HEADS):
            oh_h = oh[:, hd * HG_DIM:(hd + 1) * HG_DIM]
            r_h = lax.rsqrt(jnp.mean(oh_h * oh_h, axis=-1, keepdims=True) + EPS)
            rr.append(r_h)
            nn.append(oh_h * r_h)
        n = jnp.concatenate(nn, axis=1)
        ng = n * gam
        ycat_ref[:, :MLA_WIDTH] = (om * silu_m).astype(BF16)
        ycat_ref[:, MLA_WIDTH:] = (ng * silu_h).astype(BF16)
        wout = wout_ref[...]
        x2 = x_ref[...] + _dot(ycat_ref[...], wout)
        r = lax.rsqrt(jnp.mean(x2 * x2, axis=-1, keepdims=True) + EPS)
        xh = x2 * r
        fng = fng_ref[...]
        err = xh * fng - tgt_ref[...]
        loss_ref[...] += 0.5 * jnp.sum(jnp.mean(err * err, axis=-1, keepdims=True), axis=0, keepdims=True)
        dout = err * (1.0 / D_MODEL)
        dfng_ref[...] += jnp.sum(dout * xh, axis=0, keepdims=True)
        dxh = dout * fng
        dx2 = r * (dxh - xh * jnp.mean(dxh * xh, axis=-1, keepdims=True))
        dx2_ref[...] = dx2
        dx2b = dx2.astype(BF16)
        dx2b_ref[...] = dx2b
        dycat = _dot_nt(dx2b, wout)
        dym, dyh = dycat[:, :MLA_WIDTH], dycat[:, MLA_WIDTH:]
        dom = dym * silu_m
        first = lax.broadcasted_iota(jnp.int32, (tm, LANES), 1) < V_DIM
        for pp in range(N_HEADS // 2):
            pair = dom[:, pp * LANES:(pp + 1) * LANES]
            dom_ref[:, 2 * pp * HEAD_LANES:(2 * pp + 1) * HEAD_LANES] = jnp.where(first, pair, 0.0).astype(BF16)
            dom_ref[:, (2 * pp + 1) * HEAD_LANES:(2 * pp + 2) * HEAD_LANES] = jnp.where(first, 0.0, pair).astype(BF16)
        head_of = lax.broadcasted_iota(jnp.int32, (MLA_WIDTH, LANES), 0) // V_DIM
        pick = (head_of == lax.broadcasted_iota(jnp.int32, (MLA_WIDTH, LANES), 1)).astype(F32)
        dsum_ref[...] = jnp.dot(dom * om, pick, precision=lax.Precision.HIGHEST, preferred_element_type=F32)
        dgm_ref[...] = dym * om * (sgm * (1.0 + gm * (1.0 - sgm)))
        dgh_ref[...] = dyh * ng * (sgh * (1.0 + gh * (1.0 - sgh)))
        dng = dyh * silu_h
        dhgn_ref[...] += jnp.sum(dng * n, axis=0, keepdims=True)
        dn = dng * gam
        for hd in range(HG_HEADS):
            sl = slice(hd * HG_DIM, (hd + 1) * HG_DIM)
            dn_h, n_h = dn[:, sl], nn[hd]
            doh_ref[:, sl] = rr[hd] * (dn_h - n_h * jnp.mean(dn_h * n_h, axis=-1, keepdims=True))

    def row(w, cb=0):
        return pl.BlockSpec((tm, w), lambda i: (i, cb))

    outs = [(D_MODEL, F32), (D_MODEL, BF16), (D_MODEL, BF16), (N_HEADS * HEAD_LANES, BF16), (LANES, F32),
            (MLA_WIDTH, F32), (HG_WIDTH, F32), (HG_WIDTH, F32)]
    small = [(1, LANES), (1, D_MODEL), (1, HG_WIDTH)]
    return pl.pallas_call(
        body, name="top", grid=(T // tm,),
        in_specs=[row(D_MODEL), row(D_MODEL), row(MLA_WIDTH), row(HG_WIDTH),
                  row(MLA_WIDTH, P_GM // MLA_WIDTH), row(HG_WIDTH, P_GH // HG_WIDTH),
                  _full((D_MODEL, D_MODEL)), _full((1, HG_WIDTH)), _full((1, D_MODEL))],
        out_specs=[row(w) for w, _ in outs] + [_full(s) for s in small],
        out_shape=[jax.ShapeDtypeStruct((T, w), dt) for w, dt in outs] + [jax.ShapeDtypeStruct(s, F32) for s in small],
        compiler_params=_params(1),
    )(x, tgt, o_mla, o_hg, proj, proj, w_out, hg_norm_g, final_g)


def _bot(x, dx2, proj, dq, dk, dv, dgm, dhq, dhf, dhi, dgh, c_t, s1_t, s2_t, w_in_p, w_q_p, w_kv_p, ln_g, q_g, kv_g, tm):
    T = x.shape[0]
    lat_w = D_PERM - P_QL

    def body(x_ref, dx2_ref, lat_ref, dq_ref, dk_ref, dv_ref, dgm_ref, dhq_ref, dhf_ref, dhi_ref, dgh_ref,
             c_ref, s1_ref, s2_ref, win_ref, wq_ref, wkv_ref, lng_ref, qg_ref, kvg_ref,
             dx_ref, dproj_ref, dqpre_ref, dkv_ref, dlng_ref, dqg_ref, dkvg_ref):
        @pl.when(pl.program_id(0) == 0)
        def _():
            dlng_ref[...] = jnp.zeros_like(dlng_ref)
            dqg_ref[...] = jnp.zeros_like(dqg_ref)
            dkvg_ref[...] = jnp.zeros_like(dkvg_ref)

        c, s1, s2 = c_ref[...], s1_ref[...], s2_ref[...]
        dkpe = jnp.zeros((tm, LANES), F32)
        for hd in range(N_HEADS):
            sl = slice(hd * HEAD_LANES, (hd + 1) * HEAD_LANES)
            dqpre_ref[:, sl] = _rope_bwd(dq_ref[:, sl], c, s1, s2).astype(BF16)
            dk_h = dk_ref[:, sl]
            dkpe = dkpe + dk_h
            dkv_ref[:, sl] = dk_h.astype(BF16)
        dkv_ref[:, N_HEADS * HEAD_LANES:] = dv_ref[...].astype(BF16)
        lane = lax.broadcasted_iota(jnp.int32, (tm, LANES), 1)
        rope_lanes = jnp.logical_and(lane >= ROPE_LO, lane < ROPE_LO + ROPE)
        dkr = jnp.where(rope_lanes, _rope_bwd(dkpe, c, s1, s2), 0.0)

        def norm_bwd(v, g, dy):
            r = lax.rsqrt(jnp.mean(v * v, axis=-1, keepdims=True) + EPS)
            vh = v * r
            dvh = dy * g
            return jnp.sum(dy * vh, axis=0, keepdims=True), r * (dvh - vh * jnp.mean(dvh * vh, axis=-1, keepdims=True))

        dqn = _dot_nt(dqpre_ref[...], wq_ref[...])
        dg_q, dql = norm_bwd(lat_ref[:, :Q_RANK], qg_ref[...], dqn)
        dqg_ref[...] += dg_q
        dkn = _dot_nt(dkv_ref[...], wkv_ref[...])
        dg_kv, dkvl = norm_bwd(lat_ref[:, Q_RANK:Q_RANK + KV_RANK], kvg_ref[...], dkn)
        dkvg_ref[...] += dg_kv

        dproj_ref[:, P_GM:P_GM + MLA_WIDTH] = dgm_ref[...].astype(BF16)
        dproj_ref[:, P_HQ:P_HQ + HG_WIDTH] = dhq_ref[...].astype(BF16)
        dproj_ref[:, P_HF:P_HF + HG_WIDTH] = dhf_ref[...].astype(BF16)
        dproj_ref[:, P_HI:P_HI + HG_WIDTH] = dhi_ref[...].astype(BF16)
        dproj_ref[:, P_GH:P_GH + HG_WIDTH] = dgh_ref[...].astype(BF16)
        dproj_ref[:, P_QL:P_QL + Q_RANK] = dql.astype(BF16)
        dproj_ref[:, P_KVL:P_KVL + KV_RANK] = dkvl.astype(BF16)
        dproj_ref[:, P_KR:P_KR + LANES] = dkr.astype(BF16)
        dh = _dot_nt(dproj_ref[...], win_ref[...])
        dg_ln, dxn = norm_bwd(x_ref[...], lng_ref[...], dh)
        dlng_ref[...] += dg_ln
        dx_ref[...] = dx2_ref[...] + dxn

    def row(w, cb=0):
        return pl.BlockSpec((tm, w), lambda i: (i, cb))

    hl = N_HEADS * HEAD_LANES
    outs = [(D_MODEL, F32), (D_PERM, BF16), (hl, BF16), (hl + MLA_WIDTH, BF16)]
    small = [(1, D_MODEL), (1, Q_RANK), (1, KV_RANK)]
    return pl.pallas_call(
        body, name="bot", grid=(T // tm,),
        in_specs=[row(D_MODEL), row(D_MODEL), row(lat_w, P_QL // lat_w), row(hl), row(hl), row(MLA_WIDTH),
                  row(MLA_WIDTH), row(HG_WIDTH), row(HG_WIDTH), row(HG_WIDTH), row(HG_WIDTH),
                  row(LANES), row(LANES), row(LANES),
                  _full((D_MODEL, D_PERM)), _full((Q_RANK, hl)), _full((KV_RANK, hl + MLA_WIDTH)),
                  _full((1, D_MODEL)), _full((1, Q_RANK)), _full((1, KV_RANK))],
        out_specs=[row(w) for w, _ in outs] + [_full(s) for s in small],
        out_shape=[jax.ShapeDtypeStruct((T, w), dt) for w, dt in outs] + [jax.ShapeDtypeStruct(s, F32) for s in small],
        compiler_params=_params(1),
    )(x, dx2, proj, dq, dk, dv, dgm, dhq, dhf, dhi, dgh, c_t, s1_t, s2_t, w_in_p, w_q_p, w_kv_p, ln_g, q_g, kv_g)


def _matmul_tn(a, b, bn, bt, name):
    T, M = a.shape
    N = b.shape[1]

    def body(a_ref, b_ref, o_ref):
        @pl.when(pl.program_id(1) == 0)
        def _():
            o_ref[...] = jnp.zeros_like(o_ref)

        o_ref[...] += _dot_tn(a_ref[...], b_ref[...])

    return pl.pallas_call(
        body, name=name, grid=(N // bn, T // bt),
        in_specs=[pl.BlockSpec((bt, M), lambda n, t: (t, 0)), pl.BlockSpec((bt, bn), lambda n, t: (t, n))],
        out_specs=pl.BlockSpec((M, bn), lambda n, t: (0, n)),
        out_shape=jax.ShapeDtypeStruct((M, N), F32),
        compiler_params=_params(2),
    )(a, b)


RS_ROWS = 256


def _reduce_scatter(slabs, small):
    n = len(slabs)
    units = [(a, r0, min(s.shape[1], RS_ROWS)) for a, s in enumerate(slabs) for r0 in range(0, s.shape[1], RS_ROWS)]
    nu = len(units)

    def body(*refs):
        ins, small_ref = refs[:n], refs[n]
        outs, small_out = refs[n + 1:2 * n + 1], refs[2 * n + 1]
        own, sib_land, ici_out, ici_land = (refs[(2 + g) * n + 2:(3 + g) * n + 2] for g in range(4))
        small_land = refs[6 * n + 2]
        loc_sems, d2d_send, d2d_recv, ici_send, ici_recv, sm_send, sm_recv = refs[6 * n + 3:6 * n + 10]
        x, y, c = lax.axis_index("x"), lax.axis_index("y"), lax.axis_index("c")
        me = 4 * x + 2 * y + c

        def chip(k):
            return (1 - x if k & 2 else x, 1 - y if k & 1 else y)

        def block(k, core):
            px, py = chip(k)
            return 4 * px + 2 * py + core

        def part(u):
            a, r0, nr = units[u]
            return a, pl.ds(r0, nr)

        def local(u, k):
            a, rows = part(u)
            return pltpu.make_async_copy(ins[a].at[block(k, c), rows, :], own[a].at[k, rows, :], loc_sems.at[u, k])

        def to_sibling(u, k):
            a, rows = part(u)
            return pltpu.make_async_remote_copy(
                src_ref=ins[a].at[block(k, 1 - c), rows, :], dst_ref=sib_land[a].at[k, rows, :],
                send_sem=d2d_send.at[u, k], recv_sem=d2d_recv.at[u, k], device_id=(x, y, 1 - c), device_id_type=MESH)

        def to_chip(u, k):
            a, rows = part(u)
            return pltpu.make_async_remote_copy(
                src_ref=ici_out[a].at[k - 1, rows, :], dst_ref=ici_land[a].at[k - 1, rows, :],
                send_sem=ici_send.at[u, k - 1], recv_sem=ici_recv.at[u, k - 1], device_id=(*chip(k), c),
                device_id_type=MESH)

        def small_copy(k, receiving):
            px, py = chip(k >> 1)
            pc = 1 - c if k & 1 else c
            slot = 4 * px + 2 * py + pc if receiving else me
            return pltpu.make_async_remote_copy(
                src_ref=small_ref, dst_ref=small_land.at[slot], send_sem=sm_send.at[k - 1], recv_sem=sm_recv.at[k - 1],
                device_id=(px, py, pc), device_id_type=MESH)

        for u in range(nu):
            for k in range(4):
                local(u, k).start()
        for u in range(nu):
            for k in range(4):
                to_sibling(u, k).start()
        small_land[me] = small_ref[...]
        for k in range(1, N_DEV):
            small_copy(k, False).start()
        for u in range(nu):
            a, rows = part(u)
            for k in range(4):
                local(u, k).wait()
                to_sibling(u, k).wait_recv()
            for k in range(1, 4):
                ici_out[a][k - 1, rows, :] = (own[a][k, rows, :] + sib_land[a][k, rows, :]).astype(BF16)
                to_chip(u, k).start()
        for u in range(nu):
            a, rows = part(u)
            acc = own[a][0, rows, :] + sib_land[a][0, rows, :]
            for k in range(1, 4):
                to_chip(u, k).wait_recv()
                acc = acc + ici_land[a][k - 1, rows, :].astype(F32)
            outs[a][rows, :] = acc
        for k in range(1, N_DEV):
            small_copy(k, True).wait_recv()
        acc = small_land[0]
        for d in range(1, N_DEV):
            acc = acc + small_land[d]
        small_out[...] = acc
        for u in range(nu):
            for k in range(4):
                to_sibling(u, k).wait_send()
            for k in range(1, 4):
                to_chip(u, k).wait_send()
        for k in range(1, N_DEV):
            small_copy(k, False).wait_send()

    vm = pl.BlockSpec(memory_space=pltpu.VMEM)
    hbm = pl.BlockSpec(memory_space=pl.ANY)
    dma = pltpu.SemaphoreType.DMA
    return pl.pallas_call(
        body, name="reduce_scatter_grads",
        in_specs=[hbm] * n + [vm], out_specs=[vm] * (n + 1),
        out_shape=[jax.ShapeDtypeStruct(s.shape[1:], F32) for s in slabs] + [jax.ShapeDtypeStruct(small.shape, F32)],
        scratch_shapes=[pltpu.VMEM((4,) + s.shape[1:], F32) for s in slabs] * 2
        + [pltpu.VMEM((3,) + s.shape[1:], BF16) for s in slabs] * 2
        + [pltpu.VMEM((N_DEV,) + small.shape, F32)]
        + [dma((nu, 4)), dma((nu, 4)), dma((nu, 4)), dma((nu, 3)), dma((nu, 3)), dma((N_DEV - 1,)), dma((N_DEV - 1,))],
        compiler_params=pltpu.CompilerParams(vmem_limit_bytes=VMEM_LIMIT),
    )(*slabs, small)


def _adamw_math(w, g, m, v):
    m = ADAM_B1 * m + (1.0 - ADAM_B1) * g
    v = ADAM_B2 * v + (1.0 - ADAM_B2) * (g * g)
    m_hat = m / (1.0 - ADAM_B1 ** ADAM_STEP)
    v_hat = v / (1.0 - ADAM_B2 ** ADAM_STEP)
    delta = -ADAM_LR * (m_hat / (jnp.sqrt(v_hat) + ADAM_EPS) + ADAM_WD * w)
    return delta, m, v


SMALL_W = 512


def _adamw(big, small_w, small_g):
    nb, ns = len(big), len(small_w)

    def body(*refs):
        k = 0
        big_in = [refs[4 * i:4 * i + 4] for i in range(nb)]
        k = 4 * nb
        small_in = [refs[k + 3 * i:k + 3 * i + 3] for i in range(ns)]
        k += 3 * ns
        sg_ref = refs[k]
        k += 1
        big_out = [refs[k + 3 * i:k + 3 * i + 3] for i in range(nb)]
        k += 3 * nb
        small_out = [refs[k + 4 * i:k + 4 * i + 4] for i in range(ns)]

        for (w, g, m, v), (od, om, ov) in zip(big_in, big_out):
            od[...], om[...], ov[...] = _adamw_math(w[...], g[...], m[...], v[...])

        sg = sg_ref[...]
        lbp = small_in[2][0][...]
        lb = _lower_bound(lbp)
        t = sg[4:5, :] * lb * (1.0 - lb)
        grads = [jnp.concatenate([sg[0:1, :], sg[1:2, :]], axis=1),
                 jnp.concatenate([sg[2:3, :], sg[3:4, :]], axis=1),
                 jnp.concatenate([t, -t], axis=0),
                 sg[6:7, :], sg[7:8, 0:Q_RANK], sg[7:8, Q_RANK:Q_RANK + KV_RANK]]
        for (w, m, v), g, (og, od, om, ov) in zip(small_in, grads, small_out):
            og[...] = g
            od[...], om[...], ov[...] = _adamw_math(w[...], g, m[...], v[...])

    ins = [a for grp in big for a in grp] + [a for grp in small_w for a in grp] + [small_g]
    out_shape = ([jax.ShapeDtypeStruct(grp[0].shape, F32) for grp in big for _ in range(3)]
                 + [jax.ShapeDtypeStruct(grp[0].shape, F32) for grp in small_w for _ in range(4)])
    vm = pl.BlockSpec(memory_space=pltpu.VMEM)
    res = pl.pallas_call(
        body, name="adamw", in_specs=[vm] * len(ins), out_specs=[vm] * len(out_shape), out_shape=out_shape,
        compiler_params=pltpu.CompilerParams(vmem_limit_bytes=VMEM_LIMIT),
    )(*ins)
    big_res = [res[3 * i:3 * i + 3] for i in range(nb)]
    small_res = [res[3 * nb + 4 * i:3 * nb + 4 * i + 4] for i in range(ns)]
    return big_res, small_res


def _perm_weights(g_in, g_q, g_kv, g_out):
    w = g_in.transpose(1, 0, 2).reshape(D_MODEL, D_IN)
    z = lambda n: jnp.zeros((D_MODEL, n), BF16)
    w_in_p = jnp.concatenate([w[:, 416:], w[:, :384], z(64), w[:, 384:416], z(32)], axis=1)
    wq = g_q.transpose(1, 0, 2)
    w_q_p = jnp.pad(wq, ((0, 0), (0, 0), (0, HEAD_LANES - NOPE - ROPE))).reshape(Q_RANK, N_HEADS * HEAD_LANES)
    wkv = g_kv.transpose(1, 0, 2)
    wk = jnp.pad(wkv[:, :, :NOPE], ((0, 0), (0, 0), (0, HEAD_LANES - NOPE))).reshape(KV_RANK, N_HEADS * HEAD_LANES)
    wv = wkv[:, :, NOPE:].reshape(KV_RANK, MLA_WIDTH)
    return w_in_p, w_q_p, jnp.concatenate([wk, wv], axis=1), g_out.reshape(D_MODEL, D_MODEL)


def _grad_slabs(dw_in_p, dw_q_p, dw_kv_p, dw_out):
    dw_in = jnp.concatenate([dw_in_p[:, P_QL:P_KR], dw_in_p[:, P_KR + ROPE_LO:P_KR + ROPE_LO + ROPE], dw_in_p[:, :P_QL]], axis=1)
    s_in = dw_in.reshape(D_MODEL, N_DEV, D_IN // N_DEV).transpose(1, 0, 2)
    s_q = dw_q_p.reshape(Q_RANK, N_HEADS, HEAD_LANES)[:, :, :NOPE + ROPE].transpose(1, 0, 2)
    hl = N_HEADS * HEAD_LANES
    dk = dw_kv_p[:, :hl].reshape(KV_RANK, N_HEADS, HEAD_LANES)[:, :, :NOPE]
    dv = dw_kv_p[:, hl:].reshape(KV_RANK, N_HEADS, V_DIM)
    s_kv = jnp.concatenate([dk, dv], axis=2).transpose(1, 0, 2)
    return s_in, s_q, s_kv, dw_out.reshape(N_DEV, D_MODEL // N_DEV, D_MODEL)


def _block_sizes(T):
    return min(256, T), min(256, T), min(512, T)


def kernel(x, positions, ln_g, w_in, q_a_norm_g, w_q_b, kv_a_norm_g, w_kv_b, hg_lower_bounds, hg_norm_g, w_out, final_norm_g, loss_target, m_ln_g, m_w_in, m_q_a_norm_g, m_w_q_b, m_kv_a_norm_g, m_w_kv_b, m_hg_lower_bounds, m_hg_norm_g, m_w_out, m_final_norm_g, v_ln_g, v_w_in, v_q_a_norm_g, v_w_q_b, v_kv_a_norm_g, v_w_kv_b, v_hg_lower_bounds, v_hg_norm_g, v_w_out, v_final_norm_g):
    T = x.shape[1]
    tm, tq, bt = _block_sizes(T)
    nq = T // tq
    xs, tgt = x[0], loss_target[0]
    pos_f = positions[0].astype(F32).reshape(T, 1)
    fng = final_norm_g.reshape(1, D_MODEL)

    gathered = _all_gather_weights([w_in[0], w_q_b[0], w_kv_b[0], w_out[0]])
    w_in_p, w_q_p, w_kv_p, w_out_b = _perm_weights(*gathered)

    c_t, s1_t, s2_t = _rope_tables(pos_f, tm)
    proj, h, qn, kvn, q, k, v = _fwd_in(xs, ln_g, w_in_p, q_a_norm_g, w_q_p, kv_a_norm_g, w_kv_p, c_t, s1_t, s2_t, tm)
    v_t = v.reshape(nq, tq, MLA_WIDTH).transpose(0, 2, 1)
    k_t = k.reshape(nq, tq, N_HEADS * HEAD_LANES).transpose(0, 2, 1)
    o_mla, lse = _attn_fwd(q, k, v_t, tq)
    o_hg, states = _hgrn_fwd(proj, hg_lower_bounds)
    dx2, dx2b, ycat, d_om, dsum, d_gm, d_oh, d_gh, loss_p, d_fng, d_hgn = _top(
        xs, tgt, o_mla, o_hg, proj, w_out_b, hg_norm_g, fng, tm)
    dsum = dsum[:, :N_HEADS].T.reshape(N_HEADS, nq, 1, tq)
    hl = N_HEADS * HEAD_LANES
    q_t = q.reshape(nq, tq, hl).transpose(0, 2, 1)
    do_t = d_om.reshape(nq, tq, hl).transpose(0, 2, 1)
    dq_t, dk, dv = _attn_bwd(q, k, v, q_t, k_t, d_om, do_t, lse, dsum, tq)
    dq = dq_t.transpose(0, 2, 1).reshape(T, N_HEADS * HEAD_LANES)
    d_hq, d_hf, d_hi, d_lb = _hgrn_bwd(proj, hg_lower_bounds, d_oh, states)
    dx, dproj, dq_pre, dkv, d_lng, d_qg, d_kvg = _bot(
        xs, dx2, proj, dq, dk, dv, d_gm, d_hq, d_hf, d_hi, d_gh, c_t, s1_t, s2_t, w_in_p, w_q_p, w_kv_p,
        ln_g, q_a_norm_g, kv_a_norm_g, tm)
    dw_in_p = _matmul_tn(h, dproj, 512, bt, "dw_in")
    dw_out = _matmul_tn(ycat, dx2b, 512, bt, "dw_out")
    dw_q_p = _matmul_tn(qn, dq_pre, 512, bt, "dw_q_b")
    dw_kv_p = _matmul_tn(kvn, dkv, 512, bt, "dw_kv_b")

    zrow = jnp.zeros((1, SMALL_W), F32)
    small = jnp.concatenate([
        d_lng.reshape(2, SMALL_W), d_fng.reshape(2, SMALL_W), d_lb, zrow, d_hgn,
        jnp.concatenate([d_qg, d_kvg, jnp.zeros((1, SMALL_W - Q_RANK - KV_RANK), F32)], axis=1)], axis=0)
    g_in, g_q, g_kv, g_out, small_sum = _reduce_scatter(list(_grad_slabs(dw_in_p, dw_q_p, dw_kv_p, dw_out)), small)

    big = [(w_in[0], g_in, m_w_in[0], v_w_in[0]), (w_q_b[0], g_q, m_w_q_b[0], v_w_q_b[0]),
           (w_kv_b[0], g_kv, m_w_kv_b[0], v_w_kv_b[0]), (w_out[0], g_out, m_w_out[0], v_w_out[0])]
    small_w = [(ln_g, m_ln_g, v_ln_g),
               (fng, m_final_norm_g.reshape(1, D_MODEL), v_final_norm_g.reshape(1, D_MODEL)),
               (hg_lower_bounds, m_hg_lower_bounds, v_hg_lower_bounds), (hg_norm_g, m_hg_norm_g, v_hg_norm_g),
               (q_a_norm_g, m_q_a_norm_g, v_q_a_norm_g), (kv_a_norm_g, m_kv_a_norm_g, v_kv_a_norm_g)]
    big_res, small_res = _adamw(big, small_w, small_sum)

    loss = lax.psum(loss_p[0, 0], ("x", "y", "c"))
    (r_in, r_q, r_kv, r_out) = big_res
    (s_ln, s_fn, s_lb, s_hgn, s_qg, s_kvg) = small_res
    flat = lambda t: t.reshape(D_MODEL)
    lead = lambda t: t[None]
    grads = [s_ln[0], lead(g_in), s_qg[0], lead(g_q), s_kvg[0], lead(g_kv), s_lb[0], s_hgn[0], lead(g_out), flat(s_fn[0])]

    def pick(i):
        return [s_ln[i + 1], lead(r_in[i]), s_qg[i + 1], lead(r_q[i]), s_kvg[i + 1], lead(r_kv[i]), s_lb[i + 1],
                s_hgn[i + 1], lead(r_out[i]), flat(s_fn[i + 1])]

    return (loss, dx[None], *grads, *pick(0), *pick(1), *pick(2))
```

```python
import math

import numpy as np
import jax
import jax.numpy as jnp
from jax import lax
from jax.experimental import pallas as pl
from jax.experimental.pallas import tpu as pltpu

F32 = jnp.float32
BF16 = jnp.bfloat16

D_MODEL = 1024
N_HEADS = 8
NOPE = 64
ROPE = 32
HALF_ROPE = ROPE // 2
V_DIM = 64
Q_RANK = 256
KV_RANK = 128
MLA_WIDTH = N_HEADS * V_DIM
HG_HEADS = 4
HG_DIM = 128
HG_WIDTH = HG_HEADS * HG_DIM
CHUNK = 64
SUB = 16
D_IN = 2976
D_PERM = 3072
ROPE_THETA = 10000.0
EPS = 1e-6
N_DEV = 8
LANES = 128
HEAD_LANES = 128

P_GM, P_HQ, P_HF, P_HI, P_GH, P_QL, P_KVL, P_KR = 0, 512, 1024, 1536, 2048, 2560, 2816, 2944
ROPE_LO = NOPE
SCALE = 1.0 / math.sqrt(NOPE + ROPE)

ADAM_LR = 0.001
ADAM_B1 = 0.9
ADAM_B2 = 0.999
ADAM_EPS = 1e-08
ADAM_WD = 0.01
ADAM_STEP = 10

VMEM_LIMIT = 56 * 1024 * 1024
MESH = pl.DeviceIdType.MESH

NT = (((1,), (1,)), ((), ()))
TN = (((0,), (0,)), ((), ()))


def _params(n_grid=0, **kw):
    sem = ("arbitrary",) * n_grid if n_grid else None
    return pltpu.CompilerParams(dimension_semantics=sem, vmem_limit_bytes=VMEM_LIMIT, **kw)


def _dot(a, b):
    return jnp.dot(a, b, preferred_element_type=F32)


def _dot_nt(a, b):
    return lax.dot_general(a, b, NT, preferred_element_type=F32)


def _dot_tn(a, b):
    return lax.dot_general(a, b, TN, preferred_element_type=F32)


def _sigmoid(x):
    return 1.0 / (1.0 + jnp.exp(-x))


def _rope_fwd(x, c, s1, s2):
    return x * c + pltpu.roll(x, LANES - HALF_ROPE, 1) * s1 + pltpu.roll(x, HALF_ROPE, 1) * s2


def _rope_bwd(dy, c, s1, s2):
    return dy * c - pltpu.roll(dy, LANES - HALF_ROPE, 1) * s1 - pltpu.roll(dy, HALF_ROPE, 1) * s2


def _full(shape):
    n = len(shape)
    return pl.BlockSpec(shape, lambda *_: (0,) * n)


def _rope_tables(pos_f, tm):
    T = pos_f.shape[1]
    inv = (np.float32(ROPE_THETA) ** (-np.arange(HALF_ROPE, dtype=np.float32) / np.float32(HALF_ROPE))).astype(np.float32)
    place = np.zeros((3, HALF_ROPE, LANES), np.float32)
    for i in range(HALF_ROPE):
        place[0, i, ROPE_LO + i] = place[0, i, ROPE_LO + HALF_ROPE + i] = 1.0
        place[1, i, ROPE_LO + i] = -1.0
        place[2, i, ROPE_LO + HALF_ROPE + i] = 1.0
    base = np.ones((1, LANES), np.float32)
    base[0, ROPE_LO:ROPE_LO + ROPE] = 0.0

    def body(pos_ref, inv_ref, place_ref, base_ref, c_ref, s1_ref, s2_ref):
        ang = inv_ref[...] * pos_ref[...]
        cos, sin = jnp.cos(ang), jnp.sin(ang)

        def put(v, k):
            return lax.dot_general(v, place_ref[k], TN, precision=lax.Precision.HIGHEST, preferred_element_type=F32)

        c_ref[...] = put(cos, 0) + base_ref[...]
        s1_ref[...] = put(sin, 1)
        s2_ref[...] = put(sin, 2)

    tab = jax.ShapeDtypeStruct((T, LANES), F32)
    return pl.pallas_call(
        body, name="rope_tables", grid=(T // tm,),
        in_specs=[pl.BlockSpec((1, tm), lambda i: (0, i)), _full((HALF_ROPE, 1)), _full((3, HALF_ROPE, LANES)),
                  _full((1, LANES))],
        out_specs=[pl.BlockSpec((tm, LANES), lambda i: (i, 0))] * 3,
        out_shape=[tab, tab, tab], compiler_params=_params(1),
    )(pos_f, jnp.asarray(inv.reshape(HALF_ROPE, 1)), jnp.asarray(place), jnp.asarray(base))


def _all_gather_weights(shards):
    n = len(shards)

    def body(*refs):
        ins, outs = refs[:n], refs[n:2 * n]
        send_sems, recv_sems = refs[2 * n], refs[2 * n + 1]
        x, y, c = lax.axis_index("x"), lax.axis_index("y"), lax.axis_index("c")
        me, sibling = (x, y, c), (x, y, 1 - c)
        chips = [(1 - x, y), (x, 1 - y), (1 - x, 1 - y)]

        def idx(d):
            return 4 * d[0] + 2 * d[1] + d[2]

        def copy(a, k, block, to):
            rows = outs[a].at[idx(block)]
            return pltpu.make_async_remote_copy(src_ref=rows, dst_ref=rows, send_sem=send_sems.at[a, k],
                                                recv_sem=recv_sems.at[a, k], device_id=to, device_id_type=MESH)

        for a in range(n):
            outs[a][idx(me)] = ins[a][...].astype(BF16)
        first = []
        for a in range(n):
            first.append(copy(a, 0, me, sibling))
            first += [copy(a, 1 + j, me, (*chip, c)) for j, chip in enumerate(chips)]
        for cp in first:
            cp.start()
        passed = []
        for j, chip in enumerate(chips):
            for a in range(n):
                copy(a, 1 + j, (*chip, c), me).wait_recv()
                cp = copy(a, 4 + j, (*chip, c), sibling)
                cp.start()
                passed.append(cp)
        for a in range(n):
            copy(a, 0, sibling, me).wait_recv()
            for j, chip in enumerate(chips):
                copy(a, 4 + j, (*chip, 1 - c), me).wait_recv()
        for cp in first + passed:
            cp.wait_send()

    vm = pl.BlockSpec(memory_space=pltpu.VMEM)
    return pl.pallas_call(
        body, name="all_gather_weights",
        in_specs=[vm] * n, out_specs=[vm] * n,
        out_shape=[jax.ShapeDtypeStruct((N_DEV,) + s.shape, BF16) for s in shards],
        scratch_shapes=[pltpu.SemaphoreType.DMA((n, 7)), pltpu.SemaphoreType.DMA((n, 7))],
        compiler_params=pltpu.CompilerParams(vmem_limit_bytes=VMEM_LIMIT),
    )(*shards)


def _fwd_in(x, ln_g, w_in_p, q_g, w_q_p, kv_g, w_kv_p, c_t, s1_t, s2_t, tm):
    T = x.shape[0]

    def body(x_ref, lng_ref, win_ref, qg_ref, wq_ref, kvg_ref, wkv_ref, c_ref, s1_ref, s2_ref,
             proj_ref, h_ref, qn_ref, kvn_ref, q_ref, k_ref, v_ref):
        xv = x_ref[...]
        r = lax.rsqrt(jnp.mean(xv * xv, axis=-1, keepdims=True) + EPS)
        h = (xv * r * lng_ref[...]).astype(BF16)
        h_ref[...] = h
        proj = _dot(h, win_ref[...])
        proj_ref[...] = proj
        c, s1, s2 = c_ref[...], s1_ref[...], s2_ref[...]

        ql = proj[:, P_QL:P_QL + Q_RANK]
        rq = lax.rsqrt(jnp.mean(ql * ql, axis=-1, keepdims=True) + EPS)
        qn = (ql * rq * qg_ref[...]).astype(BF16)
        qn_ref[...] = qn
        q = _dot(qn, wq_ref[...])
        for hd in range(N_HEADS):
            sl = slice(hd * HEAD_LANES, (hd + 1) * HEAD_LANES)
            q_ref[:, sl] = _rope_fwd(q[:, sl], c, s1, s2).astype(BF16)

        kvl = proj[:, P_KVL:P_KVL + KV_RANK]
        rk = lax.rsqrt(jnp.mean(kvl * kvl, axis=-1, keepdims=True) + EPS)
        kvn = (kvl * rk * kvg_ref[...]).astype(BF16)
        kvn_ref[...] = kvn
        kv = _dot(kvn, wkv_ref[...])
        kpe = _rope_fwd(proj[:, P_KR:P_KR + LANES], c, s1, s2)
        for hd in range(N_HEADS):
            sl = slice(hd * HEAD_LANES, (hd + 1) * HEAD_LANES)
            k_ref[:, sl] = (kv[:, sl] + kpe).astype(BF16)
        v_ref[...] = kv[:, N_HEADS * HEAD_LANES:].astype(BF16)

    def row(w):
        return pl.BlockSpec((tm, w), lambda i: (i, 0))

    outs = [(D_PERM, F32), (D_MODEL, BF16), (Q_RANK, BF16), (KV_RANK, BF16),
            (N_HEADS * HEAD_LANES, BF16), (N_HEADS * HEAD_LANES, BF16), (MLA_WIDTH, BF16)]
    return pl.pallas_call(
        body, name="fwd_in", grid=(T // tm,),
        in_specs=[row(D_MODEL), _full((1, D_MODEL)), _full((D_MODEL, D_PERM)), _full((1, Q_RANK)),
                  _full((Q_RANK, N_HEADS * HEAD_LANES)), _full((1, KV_RANK)),
                  _full((KV_RANK, N_HEADS * HEAD_LANES + MLA_WIDTH)), row(LANES), row(LANES), row(LANES)],
        out_specs=[row(w) for w, _ in outs],
        out_shape=[jax.ShapeDtypeStruct((T, w), dt) for w, dt in outs],
        compiler_params=_params(1),
    )(x, ln_g, w_in_p, q_g, w_q_p, kv_g, w_kv_p, c_t, s1_t, s2_t)


LOG2E = 1.4426950408889634
SCALE2 = SCALE * LOG2E


def _causal(tq):
    r = lax.broadcasted_iota(jnp.int32, (tq, tq), 0)
    c = lax.broadcasted_iota(jnp.int32, (tq, tq), 1)
    return r <= c


def _attn_fwd(q, k, v_t, tq):
    T = q.shape[0]
    nq = T // tq

    def body(q_ref, k_ref, vt_ref, o_ref, lse_ref, sa_ref, sb_ref):
        i = pl.program_id(1)
        qh = [q_ref[:, hh * HEAD_LANES:(hh + 1) * HEAD_LANES] for hh in range(2)]

        def scores(j, s_ref):
            kj = k_ref[pl.ds(pl.multiple_of(j * tq, tq), tq), :]
            for hh in range(2):
                s_ref[hh] = _dot_nt(kj[:, hh * HEAD_LANES:(hh + 1) * HEAD_LANES], qh[hh])

        def tile(j, stats, s_ref, masked):
            vt = vt_ref[j]
            out = []
            for hh in range(2):
                m, l, acc = stats[3 * hh:3 * hh + 3]
                s = s_ref[hh] * SCALE2
                if masked:
                    s = jnp.where(_causal(tq), s, -jnp.inf)
                m_new = jnp.maximum(m, jnp.max(s, axis=0, keepdims=True))
                alpha = jnp.exp2(m - m_new)
                p = jnp.exp2(s - m_new)
                out += [m_new, alpha * l + jnp.sum(p, axis=0, keepdims=True), alpha * acc + _dot(vt, p.astype(BF16))]
            return tuple(out)

        def pair(r, stats):
            scores(2 * r + 1, sb_ref)
            stats = tile(2 * r, stats, sa_ref, False)
            scores(2 * r + 2, sa_ref)
            return tile(2 * r + 1, stats, sb_ref, False)

        def last_even(stats):
            return tile(i, stats, sa_ref, True)

        def last_odd(stats):
            scores(i, sb_ref)
            return tile(i, tile(i - 1, stats, sa_ref, False), sb_ref, True)

        init = (jnp.full((1, tq), -1e30, F32), jnp.zeros((1, tq), F32), jnp.zeros((LANES, tq), F32)) * 2
        scores(0, sa_ref)
        stats = lax.fori_loop(0, i // 2, pair, init)
        m0, l0, a0, m1, l1, a1 = lax.cond(i % 2 == 0, last_even, last_odd, stats)
        first = lax.broadcasted_iota(jnp.int32, (LANES, tq), 0) < V_DIM
        o_ref[...] = jnp.where(first, a0 / l0, a1 / l1).T
        lse_ref[0, 0] = m0 + jnp.log2(l0)
        lse_ref[1, 0] = m1 + jnp.log2(l1)

    return pl.pallas_call(
        body, name="attn_fwd", grid=(N_HEADS // 2, nq),
        in_specs=[pl.BlockSpec((tq, 2 * HEAD_LANES), lambda p, i: (i, p)),
                  pl.BlockSpec((T, 2 * HEAD_LANES), lambda p, i: (0, p)),
                  pl.BlockSpec((nq, LANES, tq), lambda p, i: (0, p, 0))],
        out_specs=[pl.BlockSpec((tq, LANES), lambda p, i: (i, p)),
                   pl.BlockSpec((2, 1, 1, tq), lambda p, i: (p, i, 0, 0))],
        out_shape=[jax.ShapeDtypeStruct((T, MLA_WIDTH), F32), jax.ShapeDtypeStruct((N_HEADS, nq, 1, tq), F32)],
        scratch_shapes=[pltpu.VMEM((2, tq, tq), F32), pltpu.VMEM((2, tq, tq), F32)],
        compiler_params=_params(2),
    )(q, k, v_t)


def _attn_bwd(q, k, v, q_t, k_t, do_m, do_t, lse, dsum, tq):
    T = q.shape[0]
    nq = T // tq

    def body(q_ref, k_ref, v_ref, qt_ref, kt_ref, do_ref, dot_ref, lse_ref, dsum_ref, dqt_ref, dk_ref, dv_ref,
             ba_ref, bb_ref, dkt_ref, dvt_ref):
        j = pl.program_id(1)

        @pl.when(j == 0)
        def _():
            dqt_ref[...] = jnp.zeros_like(dqt_ref)

        dkt_ref[...] = jnp.zeros_like(dkt_ref)
        dvt_ref[...] = jnp.zeros_like(dvt_ref)
        heads = [slice(hh * HEAD_LANES, (hh + 1) * HEAD_LANES) for hh in range(2)]

        def rows_of(t):
            i = nq - 1 - t
            return i, pl.ds(pl.multiple_of(i * tq, tq), tq)

        def products(t, buf):
            _, rows = rows_of(t)
            for hh, sl in enumerate(heads):
                buf[hh] = _dot_nt(k_ref[:, sl], q_ref[rows, sl])
                buf[2 + hh] = _dot_nt(v_ref[...], do_ref[rows, sl])

        def tile(t, buf, masked):
            i, rows = rows_of(t)
            dv_new = None
            for hh, sl in enumerate(heads):
                s = buf[hh] * SCALE2
                if masked:
                    s = jnp.where(_causal(tq), s, -jnp.inf)
                p = jnp.exp2(s - lse_ref[hh, i])
                ds = (p * (buf[2 + hh] - dsum_ref[hh, i]) * SCALE).astype(BF16)
                dv_h = _dot_nt(dot_ref[i, sl, :], p.astype(BF16))
                dv_new = dv_h if dv_new is None else dv_new + dv_h
                dkt_ref[sl, :] += _dot_nt(qt_ref[i, sl, :], ds)
                dqt_ref[i, sl, :] += _dot(kt_ref[0, sl, :], ds)
            dvt_ref[...] += dv_new

        n_plain = nq - 1 - j
        products(0, ba_ref)

        def pair(r, carry):
            products(2 * r + 1, bb_ref)
            tile(2 * r, ba_ref, False)
            products(2 * r + 2, ba_ref)
            tile(2 * r + 1, bb_ref, False)
            return carry

        lax.fori_loop(0, n_plain // 2, pair, 0)

        @pl.when(n_plain % 2 == 0)
        def _():
            tile(n_plain, ba_ref, True)

        @pl.when(n_plain % 2 == 1)
        def _():
            products(n_plain, bb_ref)
            tile(n_plain - 1, ba_ref, False)
            tile(n_plain, bb_ref, True)

        dk_ref[...] = dkt_ref[...].T
        dv_ref[...] = dvt_ref[...].T

    stat = pl.BlockSpec((2, nq, 1, tq), lambda p, j: (p, 0, 0, 0))
    blocks_t = pl.BlockSpec((nq, 2 * HEAD_LANES, tq), lambda p, j: (0, p, 0))
    return pl.pallas_call(
        body, name="attn_bwd", grid=(N_HEADS // 2, nq),
        in_specs=[pl.BlockSpec((T, 2 * HEAD_LANES), lambda p, j: (0, p)),
                  pl.BlockSpec((tq, 2 * HEAD_LANES), lambda p, j: (j, p)),
                  pl.BlockSpec((tq, LANES), lambda p, j: (j, p)),
                  blocks_t,
                  pl.BlockSpec((1, 2 * HEAD_LANES, tq), lambda p, j: (j, p, 0)),
                  pl.BlockSpec((T, 2 * HEAD_LANES), lambda p, j: (0, p)), blocks_t, stat, stat],
        out_specs=[blocks_t,
                   pl.BlockSpec((tq, 2 * HEAD_LANES), lambda p, j: (j, p)),
                   pl.BlockSpec((tq, LANES), lambda p, j: (j, p))],
        out_shape=[jax.ShapeDtypeStruct((nq, N_HEADS * HEAD_LANES, tq), F32),
                   jax.ShapeDtypeStruct((T, N_HEADS * HEAD_LANES), F32),
                   jax.ShapeDtypeStruct((T, MLA_WIDTH), F32)],
        scratch_shapes=[pltpu.VMEM((4, tq, tq), F32), pltpu.VMEM((4, tq, tq), F32),
                        pltpu.VMEM((2 * HEAD_LANES, tq), F32), pltpu.VMEM((LANES, tq), F32)],
        compiler_params=_params(2),
    )(q, k, v, q_t, k_t, do_m, do_t, lse, dsum)


def _lower_bound(lbp):
    a, b = lbp[0:1, :], lbp[1:2, :]
    mx = jnp.maximum(a, b)
    ea, eb = jnp.exp(a - mx), jnp.exp(b - mx)
    return ea / (ea + eb)


def _tri(lower):
    r = lax.broadcasted_iota(jnp.int32, (CHUNK, CHUNK), 0)
    c = lax.broadcasted_iota(jnp.int32, (CHUNK, CHUNK), 1)
    return (c <= r) if lower else (c >= r)


def _hg_gates(hq, hf, lb):
    sq = _sigmoid(hq)
    sf = _sigmoid(hf)
    f = lb + (1.0 - lb) * sf
    g = jnp.log(f)
    gcum = jnp.dot(_tri(True).astype(F32), g, precision=lax.Precision.HIGHEST, preferred_element_type=F32)
    return sq, sf, f, hq * sq, 1.0 - f, gcum


def _head(x, hd):
    return x[:, hd * HG_DIM:(hd + 1) * HG_DIM]


def _all_heads(fn):
    return jnp.concatenate([fn(hd) for hd in range(HG_HEADS)], axis=1)


def _hg_blocks(q, kk, gcum):
    rowi = lax.broadcasted_iota(jnp.int32, gcum.shape, 0)
    out = []
    for blk in range(CHUNK // SUB):
        lo, hi = blk * SUB, (blk + 1) * SUB
        gb = gcum[lo - 1:lo, :] if blk else jnp.zeros_like(gcum[0:1, :])
        eq = jnp.exp(gcum[lo:hi, :] - gb)
        ek = jnp.exp(jnp.where(rowi < hi, gb - gcum, 0.0))
        out.append((eq, ek, (q[lo:hi, :] * eq).astype(BF16), (kk * ek).astype(BF16)))
    return out


def _hg_scores(blocks):
    out = []
    for hd in range(HG_HEADS):
        a = jnp.concatenate([_dot_nt(_head(qb, hd), _head(kb, hd)) for _, _, qb, kb in blocks], axis=0)
        out.append(jnp.where(_tri(True), a, 0.0))
    return out


def _hgrn_fwd(proj, lbp):
    T = proj.shape[0]
    nc = T // CHUNK

    def body(hq_ref, hf_ref, hi_ref, lbp_ref, o_ref, st_ref, state):
        @pl.when(pl.program_id(0) == 0)
        def _():
            state[...] = jnp.zeros_like(state)

        _, _, _, q, kk, gcum = _hg_gates(hq_ref[...], hf_ref[...], _lower_bound(lbp_ref[...]))
        vb = hi_ref[...].astype(BF16)
        a = _hg_scores(_hg_blocks(q, kk, gcum))
        gend = gcum[CHUNK - 1:CHUNK, :]
        qgb = (q * jnp.exp(gcum)).astype(BF16)
        kgeb = (kk * jnp.exp(gend - gcum)).astype(BF16)
        egend = jnp.exp(gend)
        for hd in range(HG_HEADS):
            st = state[hd]
            st_ref[0, hd] = st
            o_ref[:, hd * HG_DIM:(hd + 1) * HG_DIM] = (
                _dot(a[hd].astype(BF16), _head(vb, hd)) + _dot_nt(_head(qgb, hd), st.astype(BF16)))
            state[hd] = st * _head(egend, hd) + _dot_tn(_head(vb, hd), _head(kgeb, hd))

    def col(cb):
        return pl.BlockSpec((CHUNK, HG_WIDTH), lambda i: (i, cb))

    return pl.pallas_call(
        body, name="hgrn_fwd", grid=(nc,),
        in_specs=[col(P_HQ // HG_WIDTH), col(P_HF // HG_WIDTH), col(P_HI // HG_WIDTH), _full((2, HG_WIDTH))],
        out_specs=[pl.BlockSpec((CHUNK, HG_WIDTH), lambda i: (i, 0)),
                   pl.BlockSpec((1, HG_HEADS, HG_DIM, HG_DIM), lambda i: (i, 0, 0, 0))],
        out_shape=[jax.ShapeDtypeStruct((T, HG_WIDTH), F32),
                   jax.ShapeDtypeStruct((nc, HG_HEADS, HG_DIM, HG_DIM), F32)],
        scratch_shapes=[pltpu.VMEM((HG_HEADS, HG_DIM, HG_DIM), F32)],
        compiler_params=_params(1),
    )(proj, proj, proj, lbp)


def _hgrn_bwd(proj, lbp, do_hg, states):
    T = proj.shape[0]
    nc = T // CHUNK

    def body(hq_ref, hf_ref, hi_ref, lbp_ref, do_ref, st_ref, dhq_ref, dhf_ref, dhi_ref, dlb_ref, dstate):
        @pl.when(pl.program_id(0) == 0)
        def _():
            dstate[...] = jnp.zeros_like(dstate)
            dlb_ref[...] = jnp.zeros_like(dlb_ref)

        hq, lb = hq_ref[...], _lower_bound(lbp_ref[...])
        sq, sf, f, q, kk, gcum = _hg_gates(hq, hf_ref[...], lb)
        vb = hi_ref[...].astype(BF16)
        dob = do_ref[...].astype(BF16)
        blocks = _hg_blocks(q, kk, gcum)
        a = _hg_scores(blocks)
        gend = gcum[CHUNK - 1:CHUNK, :]
        eg, egend, ekend = jnp.exp(gcum), jnp.exp(gend), jnp.exp(gend - gcum)
        qg, kge = q * eg, kk * ekend
        qgb, kgeb = qg.astype(BF16), kge.astype(BF16)

        dv, dqg, dkge, st_dst, dq_blk, dk_blk = [], [], [], [], [], []
        for hd in range(HG_HEADS):
            st = st_ref[0, hd]
            dst = dstate[hd]
            dstb = dst.astype(BF16)
            do_h, v_h = _head(dob, hd), _head(vb, hd)
            dv.append(_dot_tn(a[hd].astype(BF16), do_h) + _dot_nt(_head(kgeb, hd), dstb))
            da = jnp.where(_tri(True), _dot_nt(do_h, v_h), 0.0).astype(BF16)
            dqg.append(_dot(do_h, st.astype(BF16)))
            dkge.append(_dot(v_h, dstb))
            st_dst.append(jnp.sum(st * dst, axis=0, keepdims=True))
            dstate[hd] = _dot_tn(do_h, _head(qgb, hd)) + dst * _head(egend, hd)
            dq_blk.append([_dot(da[b * SUB:(b + 1) * SUB, :], _head(kb, hd)) for b, (_, _, _, kb) in enumerate(blocks)])
            dk_blk.append([_dot_tn(da[b * SUB:(b + 1) * SUB, :], _head(qb, hd)) for b, (_, _, qb, _) in enumerate(blocks)])
        dv, dqg, dkge, st_dst = (jnp.concatenate(t, axis=1) for t in (dv, dqg, dkge, st_dst))

        dq_a, dg_q = [], []
        dk_a, dg_k = jnp.zeros_like(gcum), jnp.zeros_like(gcum)
        for b, (eq, ek, qb, kb) in enumerate(blocks):
            dq_b = _all_heads(lambda hd: dq_blk[hd][b])
            dk_b = _all_heads(lambda hd: dk_blk[hd][b])
            dq_a.append(dq_b * eq)
            dk_a = dk_a + dk_b * ek
            dg_q.append(qb.astype(F32) * dq_b)
            dg_k = dg_k + kb.astype(F32) * dk_b
        dq_a = jnp.concatenate(dq_a, axis=0)

        dgend = st_dst * egend + jnp.sum(dkge * kge, axis=0, keepdims=True)
        dq = dq_a + dqg * eg
        dk = dk_a + dkge * ekend
        dgc = jnp.concatenate(dg_q, axis=0) - dg_k + dqg * qg - dkge * kge
        last = lax.broadcasted_iota(jnp.int32, gcum.shape, 0) == CHUNK - 1
        dgc = dgc + jnp.where(last, dgend, 0.0)
        dg = jnp.dot(_tri(False).astype(F32), dgc, precision=lax.Precision.HIGHEST, preferred_element_type=F32)
        df = dg / f - dk
        dhf_ref[...] = df * (1.0 - lb) * sf * (1.0 - sf)
        dlb_ref[...] += jnp.sum(df * (1.0 - sf), axis=0, keepdims=True)
        dhq_ref[...] = dq * (sq * (1.0 + hq * (1.0 - sq)))
        dhi_ref[...] = dv

    def col(cb):
        return pl.BlockSpec((CHUNK, HG_WIDTH), lambda i: (nc - 1 - i, cb))

    grad = jax.ShapeDtypeStruct((T, HG_WIDTH), F32)
    return pl.pallas_call(
        body, name="hgrn_bwd", grid=(nc,),
        in_specs=[col(P_HQ // HG_WIDTH), col(P_HF // HG_WIDTH), col(P_HI // HG_WIDTH), _full((2, HG_WIDTH)),
                  col(0), pl.BlockSpec((1, HG_HEADS, HG_DIM, HG_DIM), lambda i: (nc - 1 - i, 0, 0, 0))],
        out_specs=[col(0), col(0), col(0), _full((1, HG_WIDTH))],
        out_shape=[grad, grad, grad, jax.ShapeDtypeStruct((1, HG_WIDTH), F32)],
        scratch_shapes=[pltpu.VMEM((HG_HEADS, HG_DIM, HG_DIM), F32)],
        compiler_params=_params(1),
    )(proj, proj, proj, lbp, do_hg, states)


def _top(x, tgt, o_mla, o_hg, proj, w_out, hg_norm_g, final_g, tm):
    T = x.shape[0]

    def body(x_ref, tgt_ref, om_ref, oh_ref, gm_ref, gh_ref, wout_ref, hgn_ref, fng_ref,
             dx2_ref, dx2b_ref, ycat_ref, dom_ref, dsum_ref, dgm_ref, doh_ref, dgh_ref, loss_ref, dfng_ref, dhgn_ref):
        @pl.when(pl.program_id(0) == 0)
        def _():
            loss_ref[...] = jnp.zeros_like(loss_ref)
            dfng_ref[...] = jnp.zeros_like(dfng_ref)
            dhgn_ref[...] = jnp.zeros_like(dhgn_ref)

        gm, om = gm_ref[...], om_ref[...]
        sgm = _sigmoid(gm)
        silu_m = gm * sgm
        gh, oh, gam = gh_ref[...], oh_ref[...], hgn_ref[...]
        sgh = _sigmoid(gh)
        silu_h = gh * sgh
        rr, nn = [], []
        for hd in range(HG_HEADS):
            oh_h = oh[:, hd * HG_DIM:(hd + 1) * HG_DIM]
            r_h = lax.rsqrt(jnp.mean(oh_h * oh_h, axis=-1, keepdims=True) + EPS)
            rr.append(r_h)
            nn.append(oh_h * r_h)
        n = jnp.concatenate(nn, axis=1)
        ng = n * gam
        ycat_ref[:, :MLA_WIDTH] = (om * silu_m).astype(BF16)
        ycat_ref[:, MLA_WIDTH:] = (ng * silu_h).astype(BF16)
        wout = wout_ref[...]
        x2 = x_ref[...] + _dot(ycat_ref[...], wout)
        r = lax.rsqrt(jnp.mean(x2 * x2, axis=-1, keepdims=True) + EPS)
        xh = x2 * r
        fng = fng_ref[...]
        err = xh * fng - tgt_ref[...]
        loss_ref[...] += 0.5 * jnp.sum(jnp.mean(err * err, axis=-1, keepdims=True), axis=0, keepdims=True)
        dout = err * (1.0 / D_MODEL)
        dfng_ref[...] += jnp.sum(dout * xh, axis=0, keepdims=True)
        dxh = dout * fng
        dx2 = r * (dxh - xh * jnp.mean(dxh * xh, axis=-1, keepdims=True))
        dx2_ref[...] = dx2
        dx2b = dx2.astype(BF16)
        dx2b_ref[...] = dx2b
        dycat = _dot_nt(dx2b, wout)
        dym, dyh = dycat[:, :MLA_WIDTH], dycat[:, MLA_WIDTH:]
        dom = dym * silu_m
        first = lax.broadcasted_iota(jnp.int32, (tm, LANES), 1) < V_DIM
        for pp in range(N_HEADS // 2):
            pair = dom[:, pp * LANES:(pp + 1) * LANES]
            dom_ref[:, 2 * pp * HEAD_LANES:(2 * pp + 1) * HEAD_LANES] = jnp.where(first, pair, 0.0).astype(BF16)
            dom_ref[:, (2 * pp + 1) * HEAD_LANES:(2 * pp + 2) * HEAD_LANES] = jnp.where(first, 0.0, pair).astype(BF16)
        head_of = lax.broadcasted_iota(jnp.int32, (MLA_WIDTH, LANES), 0) // V_DIM
        pick = (head_of == lax.broadcasted_iota(jnp.int32, (MLA_WIDTH, LANES), 1)).astype(F32)
        dsum_ref[...] = jnp.dot(dom * om, pick, precision=lax.Precision.HIGHEST, preferred_element_type=F32)
        dgm_ref[...] = dym * om * (sgm * (1.0 + gm * (1.0 - sgm)))
        dgh_ref[...] = dyh * ng * (sgh * (1.0 + gh * (1.0 - sgh)))
        dng = dyh * silu_h
        dhgn_ref[...] += jnp.sum(dng * n, axis=0, keepdims=True)
        dn = dng * gam
        for hd in range(HG_HEADS):
            sl = slice(hd * HG_DIM, (hd + 1) * HG_DIM)
            dn_h, n_h = dn[:, sl], nn[hd]
            doh_ref[:, sl] = rr[hd] * (dn_h - n_h * jnp.mean(dn_h * n_h, axis=-1, keepdims=True))

    def row(w, cb=0):
        return pl.BlockSpec((tm, w), lambda i: (i, cb))

    outs = [(D_MODEL, F32), (D_MODEL, BF16), (D_MODEL, BF16), (N_HEADS * HEAD_LANES, BF16), (LANES, F32),
            (MLA_WIDTH, F32), (HG_WIDTH, F32), (HG_WIDTH, F32)]
    small = [(1, LANES), (1, D_MODEL), (1, HG_WIDTH)]
    return pl.pallas_call(
        body, name="top", grid=(T // tm,),
        in_specs=[row(D_MODEL), row(D_MODEL), row(MLA_WIDTH), row(HG_WIDTH),
                  row(MLA_WIDTH, P_GM // MLA_WIDTH), row(HG_WIDTH, P_GH // HG_WIDTH),
                  _full((D_MODEL, D_MODEL)), _full((1, HG_WIDTH)), _full((1, D_MODEL))],
        out_specs=[row(w) for w, _ in outs] + [_full(s) for s in small],
        out_shape=[jax.ShapeDtypeStruct((T, w), dt) for w, dt in outs] + [jax.ShapeDtypeStruct(s, F32) for s in small],
        compiler_params=_params(1),
    )(x, tgt, o_mla, o_hg, proj, proj, w_out, hg_norm_g, final_g)


def _bot(x, dx2, proj, dq, dk, dv, dgm, dhq, dhf, dhi, dgh, c_t, s1_t, s2_t, w_in_p, w_q_p, w_kv_p, ln_g, q_g, kv_g, tm):
    T = x.shape[0]
    lat_w = D_PERM - P_QL

    def body(x_ref, dx2_ref, lat_ref, dq_ref, dk_ref, dv_ref, dgm_ref, dhq_ref, dhf_ref, dhi_ref, dgh_ref,
             c_ref, s1_ref, s2_ref, win_ref, wq_ref, wkv_ref, lng_ref, qg_ref, kvg_ref,
             dx_ref, dproj_ref, dqpre_ref, dkv_ref, dlng_ref, dqg_ref, dkvg_ref):
        @pl.when(pl.program_id(0) == 0)
        def _():
            dlng_ref[...] = jnp.zeros_like(dlng_ref)
            dqg_ref[...] = jnp.zeros_like(dqg_ref)
            dkvg_ref[...] = jnp.zeros_like(dkvg_ref)

        c, s1, s2 = c_ref[...], s1_ref[...], s2_ref[...]
        dkpe = jnp.zeros((tm, LANES), F32)
        for hd in range(N_HEADS):
            sl = slice(hd * HEAD_LANES, (hd + 1) * HEAD_LANES)
            dqpre_ref[:, sl] = _rope_bwd(dq_ref[:, sl], c, s1, s2).astype(BF16)
            dk_h = dk_ref[:, sl]
            dkpe = dkpe + dk_h
            dkv_ref[:, sl] = dk_h.astype(BF16)
        dkv_ref[:, N_HEADS * HEAD_LANES:] = dv_ref[...].astype(BF16)
        lane = lax.broadcasted_iota(jnp.int32, (tm, LANES), 1)
        rope_lanes = jnp.logical_and(lane >= ROPE_LO, lane < ROPE_LO + ROPE)
        dkr = jnp.where(rope_lanes, _rope_bwd(dkpe, c, s1, s2), 0.0)

        def norm_bwd(v, g, dy):
            r = lax.rsqrt(jnp.mean(v * v, axis=-1, keepdims=True) + EPS)
            vh = v * r
            dvh = dy * g
            return jnp.sum(dy * vh, axis=0, keepdims=True), r * (dvh - vh * jnp.mean(dvh * vh, axis=-1, keepdims=True))

        dqn = _dot_nt(dqpre_ref[...], wq_ref[...])
        dg_q, dql = norm_bwd(lat_ref[:, :Q_RANK], qg_ref[...], dqn)
        dqg_ref[...] += dg_q
        dkn = _dot_nt(dkv_ref[...], wkv_ref[...])
        dg_kv, dkvl = norm_bwd(lat_ref[:, Q_RANK:Q_RANK + KV_RANK], kvg_ref[...], dkn)
        dkvg_ref[...] += dg_kv

        dproj_ref[:, P_GM:P_GM + MLA_WIDTH] = dgm_ref[...].astype(BF16)
        dproj_ref[:, P_HQ:P_HQ + HG_WIDTH] = dhq_ref[...].astype(BF16)
        dproj_ref[:, P_HF:P_HF + HG_WIDTH] = dhf_ref[...].astype(BF16)
        dproj_ref[:, P_HI:P_HI + HG_WIDTH] = dhi_ref[...].astype(BF16)
        dproj_ref[:, P_GH:P_GH + HG_WIDTH] = dgh_ref[...].astype(BF16)
        dproj_ref[:, P_QL:P_QL + Q_RANK] = dql.astype(BF16)
        dproj_ref[:, P_KVL:P_KVL + KV_RANK] = dkvl.astype(BF16)
        dproj_ref[:, P_KR:P_KR + LANES] = dkr.astype(BF16)
        dh = _dot_nt(dproj_ref[...], win_ref[...])
        dg_ln, dxn = norm_bwd(x_ref[...], lng_ref[...], dh)
        dlng_ref[...] += dg_ln
        dx_ref[...] = dx2_ref[...] + dxn

    def row(w, cb=0):
        return pl.BlockSpec((tm, w), lambda i: (i, cb))

    hl = N_HEADS * HEAD_LANES
    outs = [(D_MODEL, F32), (D_PERM, BF16), (hl, BF16), (hl + MLA_WIDTH, BF16)]
    small = [(1, D_MODEL), (1, Q_RANK), (1, KV_RANK)]
    return pl.pallas_call(
        body, name="bot", grid=(T // tm,),
        in_specs=[row(D_MODEL), row(D_MODEL), row(lat_w, P_QL // lat_w), row(hl), row(hl), row(MLA_WIDTH),
                  row(MLA_WIDTH), row(HG_WIDTH), row(HG_WIDTH), row(HG_WIDTH), row(HG_WIDTH),
                  row(LANES), row(LANES), row(LANES),
                  _full((D_MODEL, D_PERM)), _full((Q_RANK, hl)), _full((KV_RANK, hl + MLA_WIDTH)),
                  _full((1, D_MODEL)), _full((1, Q_RANK)), _full((1, KV_RANK))],
        out_specs=[row(w) for w, _ in outs] + [_full(s) for s in small],
        out_shape=[jax.ShapeDtypeStruct((T, w), dt) for w, dt in outs] + [jax.ShapeDtypeStruct(s, F32) for s in small],
        compiler_params=_params(1),
    )(x, dx2, proj, dq, dk, dv, dgm, dhq, dhf, dhi, dgh, c_t, s1_t, s2_t, w_in_p, w_q_p, w_kv_p, ln_g, q_g, kv_g)


def _matmul_tn(a, b, bn, bt, name):
    T, M = a.shape
    N = b.shape[1]

    def body(a_ref, b_ref, o_ref):
        @pl.when(pl.program_id(1) == 0)
        def _():
            o_ref[...] = jnp.zeros_like(o_ref)

        o_ref[...] += _dot_tn(a_ref[...], b_ref[...])

    return pl.pallas_call(
        body, name=name, grid=(N // bn, T // bt),
        in_specs=[pl.BlockSpec((bt, M), lambda n, t: (t, 0)), pl.BlockSpec((bt, bn), lambda n, t: (t, n))],
        out_specs=pl.BlockSpec((M, bn), lambda n, t: (0, n)),
        out_shape=jax.ShapeDtypeStruct((M, N), F32),
        compiler_params=_params(2),
    )(a, b)


RS_ROWS = 256


def _reduce_scatter(slabs, small):
    n = len(slabs)
    units = [(a, r0, min(s.shape[1], RS_ROWS)) for a, s in enumerate(slabs) for r0 in range(0, s.shape[1], RS_ROWS)]
    nu = len(units)

    def body(*refs):
        ins, small_ref = refs[:n], refs[n]
        outs, small_out = refs[n + 1:2 * n + 1], refs[2 * n + 1]
        own, sib_land, ici_out, ici_land = (refs[(2 + g) * n + 2:(3 + g) * n + 2] for g in range(4))
        small_land = refs[6 * n + 2]
        loc_sems, d2d_send, d2d_recv, ici_send, ici_recv, sm_send, sm_recv = refs[6 * n + 3:6 * n + 10]
        x, y, c = lax.axis_index("x"), lax.axis_index("y"), lax.axis_index("c")
        me = 4 * x + 2 * y + c

        def chip(k):
            return (1 - x if k & 2 else x, 1 - y if k & 1 else y)

        def block(k, core):
            px, py = chip(k)
            return 4 * px + 2 * py + core

        def part(u):
            a, r0, nr = units[u]
            return a, pl.ds(r0, nr)

        def local(u, k):
            a, rows = part(u)
            return pltpu.make_async_copy(ins[a].at[block(k, c), rows, :], own[a].at[k, rows, :], loc_sems.at[u, k])

        def to_sibling(u, k):
            a, rows = part(u)
            return pltpu.make_async_remote_copy(
                src_ref=ins[a].at[block(k, 1 - c), rows, :], dst_ref=sib_land[a].at[k, rows, :],
                send_sem=d2d_send.at[u, k], recv_sem=d2d_recv.at[u, k], device_id=(x, y, 1 - c), device_id_type=MESH)

        def to_chip(u, k):
            a, rows = part(u)
            return pltpu.make_async_remote_copy(
                src_ref=ici_out[a].at[k - 1, rows, :], dst_ref=ici_land[a].at[k - 1, rows, :],
                send_sem=ici_send.at[u, k - 1], recv_sem=ici_recv.at[u, k - 1], device_id=(*chip(k), c),
                device_id_type=MESH)

        def small_copy(k, receiving):
            px, py = chip(k >> 1)
            pc = 1 - c if k & 1 else c
            slot = 4 * px + 2 * py + pc if receiving else me
            return pltpu.make_async_remote_copy(
                src_ref=small_ref, dst_ref=small_land.at[slot], send_sem=sm_send.at[k - 1], recv_sem=sm_recv.at[k - 1],
                device_id=(px, py, pc), device_id_type=MESH)

        for u in range(nu):
            for k in range(4):
                local(u, k).start()
        for u in range(nu):
            for k in range(4):
                to_sibling(u, k).start()
        small_land[me] = small_ref[...]
        for k in range(1, N_DEV):
            small_copy(k, False).start()
        for u in range(nu):
            a, rows = part(u)
            for k in range(4):
                local(u, k).wait()
                to_sibling(u, k).wait_recv()
            for k in range(1, 4):
                ici_out[a][k - 1, rows, :] = (own[a][k, rows, :] + sib_land[a][k, rows, :]).astype(BF16)
                to_chip(u, k).start()
        for u in range(nu):
            a, rows = part(u)
            acc = own[a][0, rows, :] + sib_land[a][0, rows, :]
            for k in range(1, 4):
                to_chip(u, k).wait_recv()
                acc = acc + ici_land[a][k - 1, rows, :].astype(F32)
            outs[a][rows, :] = acc
        for k in range(1, N_DEV):
            small_copy(k, True).wait_recv()
        acc = small_land[0]
        for d in range(1, N_DEV):
            acc = acc + small_land[d]
        small_out[...] = acc
        for u in range(nu):
            for k in range(4):
                to_sibling(u, k).wait_send()
            for k in range(1, 4):
                to_chip(u, k).wait_send()
        for k in range(1, N_DEV):
            small_copy(k, False).wait_send()

    vm = pl.BlockSpec(memory_space=pltpu.VMEM)
    hbm = pl.BlockSpec(memory_space=pl.ANY)
    dma = pltpu.SemaphoreType.DMA
    return pl.pallas_call(
        body, name="reduce_scatter_grads",
        in_specs=[hbm] * n + [vm], out_specs=[vm] * (n + 1),
        out_shape=[jax.ShapeDtypeStruct(s.shape[1:], F32) for s in slabs] + [jax.ShapeDtypeStruct(small.shape, F32)],
        scratch_shapes=[pltpu.VMEM((4,) + s.shape[1:], F32) for s in slabs] * 2
        + [pltpu.VMEM((3,) + s.shape[1:], BF16) for s in slabs] * 2
        + [pltpu.VMEM((N_DEV,) + small.shape, F32)]
        + [dma((nu, 4)), dma((nu, 4)), dma((nu, 4)), dma((nu, 3)), dma((nu, 3)), dma((N_DEV - 1,)), dma((N_DEV - 1,))],
        compiler_params=pltpu.CompilerParams(vmem_limit_bytes=VMEM_LIMIT),
    )(*slabs, small)


def _adamw_math(w, g, m, v):
    m = ADAM_B1 * m + (1.0 - ADAM_B1) * g
    v = ADAM_B2 * v + (1.0 - ADAM_B2) * (g * g)
    m_hat = m / (1.0 - ADAM_B1 ** ADAM_STEP)
    v_hat = v / (1.0 - ADAM_B2 ** ADAM_STEP)
    delta = -ADAM_LR * (m_hat / (jnp.sqrt(v_hat) + ADAM_EPS) + ADAM_WD * w)
    return delta, m, v


SMALL_W = 512


def _adamw(big, small_w, small_g):
    nb, ns = len(big), len(small_w)

    def body(*refs):
        k = 0
        big_in = [refs[4 * i:4 * i + 4] for i in range(nb)]
        k = 4 * nb
        small_in = [refs[k + 3 * i:k + 3 * i + 3] for i in range(ns)]
        k += 3 * ns
        sg_ref = refs[k]
        k += 1
        big_out = [refs[k + 3 * i:k + 3 * i + 3] for i in range(nb)]
        k += 3 * nb
        small_out = [refs[k + 4 * i:k + 4 * i + 4] for i in range(ns)]

        for (w, g, m, v), (od, om, ov) in zip(big_in, big_out):
            od[...], om[...], ov[...] = _adamw_math(w[...], g[...], m[...], v[...])

        sg = sg_ref[...]
        lbp = small_in[2][0][...]
        lb = _lower_bound(lbp)
        t = sg[4:5, :] * lb * (1.0 - lb)
        grads = [jnp.concatenate([sg[0:1, :], sg[1:2, :]], axis=1),
                 jnp.concatenate([sg[2:3, :], sg[3:4, :]], axis=1),
                 jnp.concatenate([t, -t], axis=0),
                 sg[6:7, :], sg[7:8, 0:Q_RANK], sg[7:8, Q_RANK:Q_RANK + KV_RANK]]
        for (w, m, v), g, (og, od, om, ov) in zip(small_in, grads, small_out):
            og[...] = g
            od[...], om[...], ov[...] = _adamw_math(w[...], g, m[...], v[...])

    ins = [a for grp in big for a in grp] + [a for grp in small_w for a in grp] + [small_g]
    out_shape = ([jax.ShapeDtypeStruct(grp[0].shape, F32) for grp in big for _ in range(3)]
                 + [jax.ShapeDtypeStruct(grp[0].shape, F32) for grp in small_w for _ in range(4)])
    vm = pl.BlockSpec(memory_space=pltpu.VMEM)
    res = pl.pallas_call(
        body, name="adamw", in_specs=[vm] * len(ins), out_specs=[vm] * len(out_shape), out_shape=out_shape,
        compiler_params=pltpu.CompilerParams(vmem_limit_bytes=VMEM_LIMIT),
    )(*ins)
    big_res = [res[3 * i:3 * i + 3] for i in range(nb)]
    small_res = [res[3 * nb + 4 * i:3 * nb + 4 * i + 4] for i in range(ns)]
    return big_res, small_res


def _perm_weights(g_in, g_q, g_kv, g_out):
    w = g_in.transpose(1, 0, 2).reshape(D_MODEL, D_IN)
    z = lambda n: jnp.zeros((D_MODEL, n), BF16)
    w_in_p = jnp.concatenate([w[:, 416:], w[:, :384], z(64), w[:, 384:416], z(32)], axis=1)
    wq = g_q.transpose(1, 0, 2)
    w_q_p = jnp.pad(wq, ((0, 0), (0, 0), (0, HEAD_LANES - NOPE - ROPE))).reshape(Q_RANK, N_HEADS * HEAD_LANES)
    wkv = g_kv.transpose(1, 0, 2)
    wk = jnp.pad(wkv[:, :, :NOPE], ((0, 0), (0, 0), (0, HEAD_LANES - NOPE))).reshape(KV_RANK, N_HEADS * HEAD_LANES)
    wv = wkv[:, :, NOPE:].reshape(KV_RANK, MLA_WIDTH)
    return w_in_p, w_q_p, jnp.concatenate([wk, wv], axis=1), g_out.reshape(D_MODEL, D_MODEL)


def _grad_slabs(dw_in_p, dw_q_p, dw_kv_p, dw_out):
    dw_in = jnp.concatenate([dw_in_p[:, P_QL:P_KR], dw_in_p[:, P_KR + ROPE_LO:P_KR + ROPE_LO + ROPE], dw_in_p[:, :P_QL]], axis=1)
    s_in = dw_in.reshape(D_MODEL, N_DEV, D_IN // N_DEV).transpose(1, 0, 2)
    s_q = dw_q_p.reshape(Q_RANK, N_HEADS, HEAD_LANES)[:, :, :NOPE + ROPE].transpose(1, 0, 2)
    hl = N_HEADS * HEAD_LANES
    dk = dw_kv_p[:, :hl].reshape(KV_RANK, N_HEADS, HEAD_LANES)[:, :, :NOPE]
    dv = dw_kv_p[:, hl:].reshape(KV_RANK, N_HEADS, V_DIM)
    s_kv = jnp.concatenate([dk, dv], axis=2).transpose(1, 0, 2)
    return s_in, s_q, s_kv, dw_out.reshape(N_DEV, D_MODEL // N_DEV, D_MODEL)


def _block_sizes(T):
    return min(256, T), min(256, T), min(512, T)


def kernel(x, positions, ln_g, w_in, q_a_norm_g, w_q_b, kv_a_norm_g, w_kv_b, hg_lower_bounds, hg_norm_g, w_out, final_norm_g, loss_target, m_ln_g, m_w_in, m_q_a_norm_g, m_w_q_b, m_kv_a_norm_g, m_w_kv_b, m_hg_lower_bounds, m_hg_norm_g, m_w_out, m_final_norm_g, v_ln_g, v_w_in, v_q_a_norm_g, v_w_q_b, v_kv_a_norm_g, v_w_kv_b, v_hg_lower_bounds, v_hg_norm_g, v_w_out, v_final_norm_g):
    T = x.shape[1]
    tm, tq, bt = _block_sizes(T)
    nq = T // tq
    xs, tgt = x[0], loss_target[0]
    pos_f = positions.astype(F32)
    fng = final_norm_g.reshape(1, D_MODEL)

    gathered = _all_gather_weights([w_in[0], w_q_b[0], w_kv_b[0], w_out[0]])
    w_in_p, w_q_p, w_kv_p, w_out_b = _perm_weights(*gathered)

    c_t, s1_t, s2_t = _rope_tables(pos_f, bt)
    proj, h, qn, kvn, q, k, v = _fwd_in(xs, ln_g, w_in_p, q_a_norm_g, w_q_p, kv_a_norm_g, w_kv_p, c_t, s1_t, s2_t, tm)
    v_t = v.reshape(nq, tq, MLA_WIDTH).transpose(0, 2, 1)
    k_t = k.reshape(nq, tq, N_HEADS * HEAD_LANES).transpose(0, 2, 1)
    o_mla, lse = _attn_fwd(q, k, v_t, tq)
    o_hg, states = _hgrn_fwd(proj, hg_lower_bounds)
    dx2, dx2b, ycat, d_om, dsum, d_gm, d_oh, d_gh, loss_p, d_fng, d_hgn = _top(
        xs, tgt, o_mla, o_hg, proj, w_out_b, hg_norm_g, fng, tm)
    dsum = dsum[:, :N_HEADS].T.reshape(N_HEADS, nq, 1, tq)
    hl = N_HEADS * HEAD_LANES
    q_t = q.reshape(nq, tq, hl).transpose(0, 2, 1)
    do_t = d_om.reshape(nq, tq, hl).transpose(0, 2, 1)
    dq_t, dk, dv = _attn_bwd(q, k, v, q_t, k_t, d_om, do_t, lse, dsum, tq)
    dq = dq_t.transpose(0, 2, 1).reshape(T, N_HEADS * HEAD_LANES)
    d_hq, d_hf, d_hi, d_lb = _hgrn_bwd(proj, hg_lower_bounds, d_oh, states)
    dx, dproj, dq_pre, dkv, d_lng, d_qg, d_kvg = _bot(
        xs, dx2, proj, dq, dk, dv, d_gm, d_hq, d_hf, d_hi, d_gh, c_t, s1_t, s2_t, w_in_p, w_q_p, w_kv_p,
        ln_g, q_a_norm_g, kv_a_norm_g, tm)
    dw_in_p = _matmul_tn(h, dproj, 512, bt, "dw_in")
    dw_out = _matmul_tn(ycat, dx2b, 512, bt, "dw_out")
    dw_q_p = _matmul_tn(qn, dq_pre, 512, bt, "dw_q_b")
    dw_kv_p = _matmul_tn(kvn, dkv, 512, bt, "dw_kv_b")

    zrow = jnp.zeros((1, SMALL_W), F32)
    small = jnp.concatenate([
        d_lng.reshape(2, SMALL_W), d_fng.reshape(2, SMALL_W), d_lb, zrow, d_hgn,
        jnp.concatenate([d_qg, d_kvg, jnp.zeros((1, SMALL_W - Q_RANK - KV_RANK), F32)], axis=1)], axis=0)
    g_in, g_q, g_kv, g_out, small_sum = _reduce_scatter(list(_grad_slabs(dw_in_p, dw_q_p, dw_kv_p, dw_out)), small)

    big = [(w_in[0], g_in, m_w_in[0], v_w_in[0]), (w_q_b[0], g_q, m_w_q_b[0], v_w_q_b[0]),
           (w_kv_b[0], g_kv, m_w_kv_b[0], v_w_kv_b[0]), (w_out[0], g_out, m_w_out[0], v_w_out[0])]
    small_w = [(ln_g, m_ln_g, v_ln_g),
               (fng, m_final_norm_g.reshape(1, D_MODEL), v_final_norm_g.reshape(1, D_MODEL)),
               (hg_lower_bounds, m_hg_lower_bounds, v_hg_lower_bounds), (hg_norm_g, m_hg_norm_g, v_hg_norm_g),
               (q_a_norm_g, m_q_a_norm_g, v_q_a_norm_g), (kv_a_norm_g, m_kv_a_norm_g, v_kv_a_norm_g)]
    big_res, small_res = _adamw(big, small_w, small_sum)

    loss = lax.psum(loss_p[0, 0], ("x", "y", "c"))
    (r_in, r_q, r_kv, r_out) = big_res
    (s_ln, s_fn, s_lb, s_hgn, s_qg, s_kvg) = small_res
    flat = lambda t: t.reshape(D_MODEL)
    lead = lambda t: t[None]
    grads = [s_ln[0], lead(g_in), s_qg[0], lead(g_q), s_kvg[0], lead(g_kv), s_lb[0], s_hgn[0], lead(g_out), flat(s_fn[0])]

    def pick(i):
        return [s_ln[i + 1], lead(r_in[i]), s_qg[i + 1], lead(r_q[i]), s_kvg[i + 1], lead(r_kv[i]), s_lb[i + 1],
                s_hgn[i + 1], lead(r_out[i]), flat(s_fn[i + 1])]

    return (loss, dx[None], *grads, *pick(0), *pick(1), *pick(2))
```

```python
import math

import numpy as np
import jax
import jax.numpy as jnp
from jax import lax
from jax.experimental import pallas as pl
from jax.experimental.pallas import tpu as pltpu

F32 = jnp.float32
BF16 = jnp.bfloat16

D_MODEL = 1024
N_HEADS = 8
NOPE = 64
ROPE = 32
HALF_ROPE = ROPE // 2
V_DIM = 64
Q_RANK = 256
KV_RANK = 128
MLA_WIDTH = N_HEADS * V_DIM
HG_HEADS = 4
HG_DIM = 128
HG_WIDTH = HG_HEADS * HG_DIM
CHUNK = 64
SUB = 16
D_IN = 2976
D_PERM = 3072
ROPE_THETA = 10000.0
EPS = 1e-6
N_DEV = 8
LANES = 128
HEAD_LANES = 128

P_GM, P_HQ, P_HF, P_HI, P_GH, P_QL, P_KVL, P_KR = 0, 512, 1024, 1536, 2048, 2560, 2816, 2944
ROPE_LO = NOPE
SCALE = 1.0 / math.sqrt(NOPE + ROPE)

ADAM_LR = 0.001
ADAM_B1 = 0.9
ADAM_B2 = 0.999
ADAM_EPS = 1e-08
ADAM_WD = 0.01
ADAM_STEP = 10

VMEM_LIMIT = 56 * 1024 * 1024
MESH = pl.DeviceIdType.MESH

NT = (((1,), (1,)), ((), ()))
TN = (((0,), (0,)), ((), ()))


def _params(n_grid=0, **kw):
    sem = ("arbitrary",) * n_grid if n_grid else None
    return pltpu.CompilerParams(dimension_semantics=sem, vmem_limit_bytes=VMEM_LIMIT, **kw)


def _dot(a, b):
    return jnp.dot(a, b, preferred_element_type=F32)


def _dot_nt(a, b):
    return lax.dot_general(a, b, NT, preferred_element_type=F32)


def _dot_tn(a, b):
    return lax.dot_general(a, b, TN, preferred_element_type=F32)


def _sigmoid(x):
    return 1.0 / (1.0 + jnp.exp(-x))


def _rope_fwd(x, c, s1, s2):
    return x * c + pltpu.roll(x, LANES - HALF_ROPE, 1) * s1 + pltpu.roll(x, HALF_ROPE, 1) * s2


def _rope_bwd(dy, c, s1, s2):
    return dy * c - pltpu.roll(dy, LANES - HALF_ROPE, 1) * s1 - pltpu.roll(dy, HALF_ROPE, 1) * s2


def _full(shape):
    n = len(shape)
    return pl.BlockSpec(shape, lambda *_: (0,) * n)


def _rope_tables(pos_f, tm):
    T = pos_f.shape[1]
    inv = (np.float32(ROPE_THETA) ** (-np.arange(HALF_ROPE, dtype=np.float32) / np.float32(HALF_ROPE))).astype(np.float32)
    place = np.zeros((3, HALF_ROPE, LANES), np.float32)
    for i in range(HALF_ROPE):
        place[0, i, ROPE_LO + i] = place[0, i, ROPE_LO + HALF_ROPE + i] = 1.0
        place[1, i, ROPE_LO + i] = -1.0
        place[2, i, ROPE_LO + HALF_ROPE + i] = 1.0
    base = np.ones((1, LANES), np.float32)
    base[0, ROPE_LO:ROPE_LO + ROPE] = 0.0

    def body(pos_ref, inv_ref, place_ref, base_ref, c_ref, s1_ref, s2_ref):
        ang = inv_ref[...] * pos_ref[...]
        cos, sin = jnp.cos(ang), jnp.sin(ang)

        def put(v, k):
            return lax.dot_general(v, place_ref[k], TN, precision=lax.Precision.HIGHEST, preferred_element_type=F32)

        c_ref[...] = put(cos, 0) + base_ref[...]
        s1_ref[...] = put(sin, 1)
        s2_ref[...] = put(sin, 2)

    tab = jax.ShapeDtypeStruct((T, LANES), F32)
    return pl.pallas_call(
        body, name="rope_tables", grid=(T // tm,),
        in_specs=[pl.BlockSpec((1, tm), lambda i: (0, i)), _full((HALF_ROPE, 1)), _full((3, HALF_ROPE, LANES)),
                  _full((1, LANES))],
        out_specs=[pl.BlockSpec((tm, LANES), lambda i: (i, 0))] * 3,
        out_shape=[tab, tab, tab], compiler_params=_params(1),
    )(pos_f, jnp.asarray(inv.reshape(HALF_ROPE, 1)), jnp.asarray(place), jnp.asarray(base))


def _all_gather_weights(shards):
    n = len(shards)

    def body(*refs):
        ins, outs = refs[:n], refs[n:2 * n]
        send_sems, recv_sems = refs[2 * n], refs[2 * n + 1]
        x, y, c = lax.axis_index("x"), lax.axis_index("y"), lax.axis_index("c")
        me, sibling = (x, y, c), (x, y, 1 - c)
        chips = [(1 - x, y), (x, 1 - y), (1 - x, 1 - y)]

        def idx(d):
            return 4 * d[0] + 2 * d[1] + d[2]

        def copy(a, k, block, to):
            rows = outs[a].at[idx(block)]
            return pltpu.make_async_remote_copy(src_ref=rows, dst_ref=rows, send_sem=send_sems.at[a, k],
                                                recv_sem=recv_sems.at[a, k], device_id=to, device_id_type=MESH)

        for a in range(n):
            outs[a][idx(me)] = ins[a][...].astype(BF16)
        first = []
        for a in range(n):
            first.append(copy(a, 0, me, sibling))
            first += [copy(a, 1 + j, me, (*chip, c)) for j, chip in enumerate(chips)]
        for cp in first:
            cp.start()
        passed = []
        for j, chip in enumerate(chips):
            for a in range(n):
                copy(a, 1 + j, (*chip, c), me).wait_recv()
                cp = copy(a, 4 + j, (*chip, c), sibling)
                cp.start()
                passed.append(cp)
        for a in range(n):
            copy(a, 0, sibling, me).wait_recv()
            for j, chip in enumerate(chips):
                copy(a, 4 + j, (*chip, 1 - c), me).wait_recv()
        for cp in first + passed:
            cp.wait_send()

    vm = pl.BlockSpec(memory_space=pltpu.VMEM)
    return pl.pallas_call(
        body, name="all_gather_weights",
        in_specs=[vm] * n, out_specs=[vm] * n,
        out_shape=[jax.ShapeDtypeStruct((N_DEV,) + s.shape, BF16) for s in shards],
        scratch_shapes=[pltpu.SemaphoreType.DMA((n, 7)), pltpu.SemaphoreType.DMA((n, 7))],
        compiler_params=pltpu.CompilerParams(vmem_limit_bytes=VMEM_LIMIT),
    )(*shards)


def _fwd_in(x, ln_g, w_in_p, q_g, w_q_p, kv_g, w_kv_p, c_t, s1_t, s2_t, tm):
    T = x.shape[0]

    def body(x_ref, lng_ref, win_ref, qg_ref, wq_ref, kvg_ref, wkv_ref, c_ref, s1_ref, s2_ref,
             proj_ref, h_ref, qn_ref, kvn_ref, q_ref, k_ref, v_ref):
        xv = x_ref[...]
        r = lax.rsqrt(jnp.mean(xv * xv, axis=-1, keepdims=True) + EPS)
        h = (xv * r * lng_ref[...]).astype(BF16)
        h_ref[...] = h
        proj = _dot(h, win_ref[...])
        proj_ref[...] = proj
        c, s1, s2 = c_ref[...], s1_ref[...], s2_ref[...]

        ql = proj[:, P_QL:P_QL + Q_RANK]
        rq = lax.rsqrt(jnp.mean(ql * ql, axis=-1, keepdims=True) + EPS)
        qn = (ql * rq * qg_ref[...]).astype(BF16)
        qn_ref[...] = qn
        q = _dot(qn, wq_ref[...])
        for hd in range(N_HEADS):
            sl = slice(hd * HEAD_LANES, (hd + 1) * HEAD_LANES)
            q_ref[:, sl] = _rope_fwd(q[:, sl], c, s1, s2).astype(BF16)

        kvl = proj[:, P_KVL:P_KVL + KV_RANK]
        rk = lax.rsqrt(jnp.mean(kvl * kvl, axis=-1, keepdims=True) + EPS)
        kvn = (kvl * rk * kvg_ref[...]).astype(BF16)
        kvn_ref[...] = kvn
        kv = _dot(kvn, wkv_ref[...])
        kpe = _rope_fwd(proj[:, P_KR:P_KR + LANES], c, s1, s2)
        for hd in range(N_HEADS):
            sl = slice(hd * HEAD_LANES, (hd + 1) * HEAD_LANES)
            k_ref[:, sl] = (kv[:, sl] + kpe).astype(BF16)
        v_ref[...] = kv[:, N_HEADS * HEAD_LANES:].astype(BF16)

    def row(w):
        return pl.BlockSpec((tm, w), lambda i: (i, 0))

    outs = [(D_PERM, F32), (D_MODEL, BF16), (Q_RANK, BF16), (KV_RANK, BF16),
            (N_HEADS * HEAD_LANES, BF16), (N_HEADS * HEAD_LANES, BF16), (MLA_WIDTH, BF16)]
    return pl.pallas_call(
        body, name="fwd_in", grid=(T // tm,),
        in_specs=[row(D_MODEL), _full((1, D_MODEL)), _full((D_MODEL, D_PERM)), _full((1, Q_RANK)),
                  _full((Q_RANK, N_HEADS * HEAD_LANES)), _full((1, KV_RANK)),
                  _full((KV_RANK, N_HEADS * HEAD_LANES + MLA_WIDTH)), row(LANES), row(LANES), row(LANES)],
        out_specs=[row(w) for w, _ in outs],
        out_shape=[jax.ShapeDtypeStruct((T, w), dt) for w, dt in outs],
        compiler_params=_params(1),
    )(x, ln_g, w_in_p, q_g, w_q_p, kv_g, w_kv_p, c_t, s1_t, s2_t)


LOG2E = 1.4426950408889634
SCALE2 = SCALE * LOG2E


def _causal(tq):
    r = lax.broadcasted_iota(jnp.int32, (tq, tq), 0)
    c = lax.broadcasted_iota(jnp.int32, (tq, tq), 1)
    return r <= c


def _attn_fwd(q, k, v_t, tq):
    T = q.shape[0]
    nq = T // tq

    def body(q_ref, k_ref, vt_ref, o_ref, lse_ref, sa_ref, sb_ref):
        i = pl.program_id(1)
        qh = [q_ref[:, hh * HEAD_LANES:(hh + 1) * HEAD_LANES] for hh in range(2)]

        def scores(j, s_ref):
            kj = k_ref[pl.ds(pl.multiple_of(j * tq, tq), tq), :]
            for hh in range(2):
                s_ref[hh] = _dot_nt(kj[:, hh * HEAD_LANES:(hh + 1) * HEAD_LANES], qh[hh])

        def tile(j, stats, s_ref, masked):
            vt = vt_ref[j]
            out = []
            for hh in range(2):
                m, l, acc = stats[3 * hh:3 * hh + 3]
                s = s_ref[hh] * SCALE2
                if masked:
                    s = jnp.where(_causal(tq), s, -jnp.inf)
                m_new = jnp.maximum(m, jnp.max(s, axis=0, keepdims=True))
                alpha = jnp.exp2(m - m_new)
                p = jnp.exp2(s - m_new)
                out += [m_new, alpha * l + jnp.sum(p, axis=0, keepdims=True), alpha * acc + _dot(vt, p.astype(BF16))]
            return tuple(out)

        def pair(r, stats):
            scores(2 * r + 1, sb_ref)
            stats = tile(2 * r, stats, sa_ref, False)
            scores(2 * r + 2, sa_ref)
            return tile(2 * r + 1, stats, sb_ref, False)

        def last_even(stats):
            return tile(i, stats, sa_ref, True)

        def last_odd(stats):
            scores(i, sb_ref)
            return tile(i, tile(i - 1, stats, sa_ref, False), sb_ref, True)

        init = (jnp.full((1, tq), -1e30, F32), jnp.zeros((1, tq), F32), jnp.zeros((LANES, tq), F32)) * 2
        scores(0, sa_ref)
        stats = lax.fori_loop(0, i // 2, pair, init)
        m0, l0, a0, m1, l1, a1 = lax.cond(i % 2 == 0, last_even, last_odd, stats)
        first = lax.broadcasted_iota(jnp.int32, (LANES, tq), 0) < V_DIM
        o_ref[...] = jnp.where(first, a0 / l0, a1 / l1).T
        lse_ref[0, 0] = m0 + jnp.log2(l0)
        lse_ref[1, 0] = m1 + jnp.log2(l1)

    return pl.pallas_call(
        body, name="attn_fwd", grid=(N_HEADS // 2, nq),
        in_specs=[pl.BlockSpec((tq, 2 * HEAD_LANES), lambda p, i: (i, p)),
                  pl.BlockSpec((T, 2 * HEAD_LANES), lambda p, i: (0, p)),
                  pl.BlockSpec((nq, LANES, tq), lambda p, i: (0, p, 0))],
        out_specs=[pl.BlockSpec((tq, LANES), lambda p, i: (i, p)),
                   pl.BlockSpec((2, 1, 1, tq), lambda p, i: (p, i, 0, 0))],
        out_shape=[jax.ShapeDtypeStruct((T, MLA_WIDTH), F32), jax.ShapeDtypeStruct((N_HEADS, nq, 1, tq), F32)],
        scratch_shapes=[pltpu.VMEM((2, tq, tq), F32), pltpu.VMEM((2, tq, tq), F32)],
        compiler_params=_params(2),
    )(q, k, v_t)


def _attn_bwd(q, k, v, q_t, k_t, do_m, do_t, lse, dsum, tq):
    T = q.shape[0]
    nq = T // tq

    def body(q_ref, k_ref, v_ref, qt_ref, kt_ref, do_ref, dot_ref, lse_ref, dsum_ref, dqt_ref, dk_ref, dv_ref,
             ba_ref, bb_ref, dkt_ref, dvt_ref):
        j = pl.program_id(1)

        @pl.when(j == 0)
        def _():
            dqt_ref[...] = jnp.zeros_like(dqt_ref)

        dkt_ref[...] = jnp.zeros_like(dkt_ref)
        dvt_ref[...] = jnp.zeros_like(dvt_ref)
        heads = [slice(hh * HEAD_LANES, (hh + 1) * HEAD_LANES) for hh in range(2)]

        def rows_of(t):
            i = nq - 1 - t
            return i, pl.ds(pl.multiple_of(i * tq, tq), tq)

        def products(t, buf):
            _, rows = rows_of(t)
            for hh, sl in enumerate(heads):
                buf[hh] = _dot_nt(k_ref[:, sl], q_ref[rows, sl])
                buf[2 + hh] = _dot_nt(v_ref[...], do_ref[rows, sl])

        def tile(t, buf, masked):
            i, rows = rows_of(t)
            dv_new = None
            for hh, sl in enumerate(heads):
                s = buf[hh] * SCALE2
                if masked:
                    s = jnp.where(_causal(tq), s, -jnp.inf)
                p = jnp.exp2(s - lse_ref[hh, i])
                ds = (p * (buf[2 + hh] - dsum_ref[hh, i]) * SCALE).astype(BF16)
                dv_h = _dot_nt(dot_ref[i, sl, :], p.astype(BF16))
                dv_new = dv_h if dv_new is None else dv_new + dv_h
                dkt_ref[sl, :] += _dot_nt(qt_ref[i, sl, :], ds)
                dqt_ref[i, sl, :] += _dot(kt_ref[0, sl, :], ds)
            dvt_ref[...] += dv_new

        n_plain = nq - 1 - j
        products(0, ba_ref)

        def pair(r, carry):
            products(2 * r + 1, bb_ref)
            tile(2 * r, ba_ref, False)
            products(2 * r + 2, ba_ref)
            tile(2 * r + 1, bb_ref, False)
            return carry

        lax.fori_loop(0, n_plain // 2, pair, 0)

        @pl.when(n_plain % 2 == 0)
        def _():
            tile(n_plain, ba_ref, True)

        @pl.when(n_plain % 2 == 1)
        def _():
            products(n_plain, bb_ref)
            tile(n_plain - 1, ba_ref, False)
            tile(n_plain, bb_ref, True)

        dk_ref[...] = dkt_ref[...].T
        dv_ref[...] = dvt_ref[...].T

    stat = pl.BlockSpec((2, nq, 1, tq), lambda p, j: (p, 0, 0, 0))
    blocks_t = pl.BlockSpec((nq, 2 * HEAD_LANES, tq), lambda p, j: (0, p, 0))
    return pl.pallas_call(
        body, name="attn_bwd", grid=(N_HEADS // 2, nq),
        in_specs=[pl.BlockSpec((T, 2 * HEAD_LANES), lambda p, j: (0, p)),
                  pl.BlockSpec((tq, 2 * HEAD_LANES), lambda p, j: (j, p)),
                  pl.BlockSpec((tq, LANES), lambda p, j: (j, p)),
                  blocks_t,
                  pl.BlockSpec((1, 2 * HEAD_LANES, tq), lambda p, j: (j, p, 0)),
                  pl.BlockSpec((T, 2 * HEAD_LANES), lambda p, j: (0, p)), blocks_t, stat, stat],
        out_specs=[blocks_t,
                   pl.BlockSpec((tq, 2 * HEAD_LANES), lambda p, j: (j, p)),
                   pl.BlockSpec((tq, LANES), lambda p, j: (j, p))],
        out_shape=[jax.ShapeDtypeStruct((nq, N_HEADS * HEAD_LANES, tq), F32),
                   jax.ShapeDtypeStruct((T, N_HEADS * HEAD_LANES), F32),
                   jax.ShapeDtypeStruct((T, MLA_WIDTH), F32)],
        scratch_shapes=[pltpu.VMEM((4, tq, tq), F32), pltpu.VMEM((4, tq, tq), F32),
                        pltpu.VMEM((2 * HEAD_LANES, tq), F32), pltpu.VMEM((LANES, tq), F32)],
        compiler_params=_params(2),
    )(q, k, v, q_t, k_t, do_m, do_t, lse, dsum)


MASKED = -1e30


def _causal_bias(bias_ref, tq):
    bias_ref[0] = jnp.zeros((tq, tq), F32)
    bias_ref[1] = jnp.where(_causal(tq), 0.0, MASKED)


def _tile_tables(nq, by_query):
    if by_query:
        pairs = [(j, i) for i in range(nq) for j in range(i + 1)]
    else:
        pairs = [(j, i) for j in range(nq) for i in range(nq - 1, j - 1, -1)]
    pairs.append(pairs[-1])
    jj, ii = np.array(pairs, np.int32).T
    return jnp.asarray(jj), jnp.asarray(ii), len(pairs) - 1


def _walk_tiles(n, products, tile, flush, buf_a, buf_b):
    products(0, buf_a)

    def two(r, carry):
        products(2 * r + 1, buf_b)
        tile(2 * r, buf_a)
        products(2 * r + 2, buf_a)
        tile(2 * r + 1, buf_b)
        flush(2 * r)
        flush(2 * r + 1)
        return carry

    lax.fori_loop(0, n // 2, two, 0)
    if n % 2:
        tile(n - 1, buf_a)
        flush(n - 1)


def _attn_fwd_flat(k, q_t, v_t, tq):
    T = k.shape[0]
    nq = T // tq
    jj, ii, n = _tile_tables(nq, True)
    heads = [slice(hh * HEAD_LANES, (hh + 1) * HEAD_LANES) for hh in range(2)]

    def body(jj_ref, ii_ref, k_ref, qt_ref, vt_ref, o_ref, lse_ref, sa_ref, sb_ref, m_ref, l_ref, acc_ref, bias_ref):
        def reset(st):
            m_ref[st] = jnp.full(m_ref.shape[1:], MASKED, F32)
            l_ref[st] = jnp.zeros(l_ref.shape[1:], F32)
            acc_ref[st] = jnp.zeros(acc_ref.shape[1:], F32)

        _causal_bias(bias_ref, tq)
        reset(0)
        reset(1)

        def products(t, buf):
            j, i = jj_ref[t], ii_ref[t]
            kj = k_ref[pl.ds(pl.multiple_of(j * tq, tq), tq), :]
            for hh, sl in enumerate(heads):
                buf[hh] = _dot(kj[:, sl], qt_ref[i, sl, :])

        def tile(t, buf):
            j, i = jj_ref[t], ii_ref[t]
            vt = vt_ref[j]
            bias = bias_ref.at[(j == i).astype(jnp.int32)]
            st = i % 2
            for hh in range(2):
                s = buf[hh] * SCALE2 + bias[...]
                m = m_ref[st, hh]
                m_new = jnp.maximum(m, jnp.max(s, axis=0, keepdims=True))
                alpha = jnp.exp2(m - m_new)
                p = jnp.exp2(s - m_new)
                m_ref[st, hh] = m_new
                l_ref[st, hh] = alpha * l_ref[st, hh] + jnp.sum(p, axis=0, keepdims=True)
                acc_ref[st, hh] = alpha * acc_ref[st, hh] + _dot(vt, p.astype(BF16))

        def flush(t):
            j, i = jj_ref[t], ii_ref[t]

            @pl.when(j == i)
            def _():
                st = i % 2
                first = lax.broadcasted_iota(jnp.int32, (LANES, tq), 0) < V_DIM
                out = jnp.where(first, acc_ref[st, 0] / l_ref[st, 0], acc_ref[st, 1] / l_ref[st, 1])
                o_ref[pl.ds(pl.multiple_of(i * tq, tq), tq), :] = out.T
                for hh in range(2):
                    lse_ref[hh, i] = m_ref[st, hh] + jnp.log2(l_ref[st, hh])
                reset(st)

        _walk_tiles(n, products, tile, flush, sa_ref, sb_ref)

    smem = pl.BlockSpec(memory_space=pltpu.SMEM)
    return pl.pallas_call(
        body, name="attn_fwd", grid=(N_HEADS // 2,),
        in_specs=[smem, smem,
                  pl.BlockSpec((T, 2 * HEAD_LANES), lambda p: (0, p)),
                  pl.BlockSpec((nq, 2 * HEAD_LANES, tq), lambda p: (0, p, 0)),
                  pl.BlockSpec((nq, LANES, tq), lambda p: (0, p, 0))],
        out_specs=[pl.BlockSpec((T, LANES), lambda p: (0, p)),
                   pl.BlockSpec((2, nq, 1, tq), lambda p: (p, 0, 0, 0))],
        out_shape=[jax.ShapeDtypeStruct((T, MLA_WIDTH), F32), jax.ShapeDtypeStruct((N_HEADS, nq, 1, tq), F32)],
        scratch_shapes=[pltpu.VMEM((2, tq, tq), F32), pltpu.VMEM((2, tq, tq), F32),
                        pltpu.VMEM((2, 2, 1, tq), F32), pltpu.VMEM((2, 2, 1, tq), F32),
                        pltpu.VMEM((2, 2, LANES, tq), F32), pltpu.VMEM((2, tq, tq), F32)],
        compiler_params=_params(1),
    )(jj, ii, k, q_t, v_t)


def _attn_bwd_flat(k, v, q_t, k_t, do_t, lse, dsum, tq):
    T = k.shape[0]
    nq = T // tq
    jj, ii, n = _tile_tables(nq, False)
    heads = [slice(hh * HEAD_LANES, (hh + 1) * HEAD_LANES) for hh in range(2)]

    def body(jj_ref, ii_ref, k_ref, v_ref, qt_ref, kt_ref, dot_ref, lse_ref, dsum_ref, dqt_ref, dk_ref, dv_ref,
             ba_ref, bb_ref, dkt_ref, dvt_ref, bias_ref):
        _causal_bias(bias_ref, tq)
        dqt_ref[...] = jnp.zeros_like(dqt_ref)
        dkt_ref[...] = jnp.zeros_like(dkt_ref)
        dvt_ref[...] = jnp.zeros_like(dvt_ref)

        def products(t, buf):
            j, i = jj_ref[t], ii_ref[t]
            rows = pl.ds(pl.multiple_of(j * tq, tq), tq)
            for hh, sl in enumerate(heads):
                buf[hh] = _dot(k_ref[rows, sl], qt_ref[i, sl, :])
                buf[2 + hh] = _dot(v_ref[rows, :], dot_ref[i, sl, :])

        def tile(t, buf):
            j, i = jj_ref[t], ii_ref[t]
            bias = bias_ref.at[(j == i).astype(jnp.int32)]
            st = j % 2
            dv_new = None
            for hh, sl in enumerate(heads):
                p = jnp.exp2(buf[hh] * SCALE2 + bias[...] - lse_ref[hh, i])
                ds = (p * (buf[2 + hh] - dsum_ref[hh, i]) * SCALE).astype(BF16)
                dv_h = _dot_nt(dot_ref[i, sl, :], p.astype(BF16))
                dv_new = dv_h if dv_new is None else dv_new + dv_h
                dkt_ref[st, sl, :] += _dot_nt(qt_ref[i, sl, :], ds)
                dqt_ref[i, sl, :] += _dot(kt_ref[j, sl, :], ds)
            dvt_ref[st] += dv_new

        def flush(t):
            j, i = jj_ref[t], ii_ref[t]

            @pl.when(j == i)
            def _():
                st = j % 2
                rows = pl.ds(pl.multiple_of(j * tq, tq), tq)
                dk_ref[rows, :] = dkt_ref[st].T
                dv_ref[rows, :] = dvt_ref[st].T
                dkt_ref[st] = jnp.zeros(dkt_ref.shape[1:], F32)
                dvt_ref[st] = jnp.zeros(dvt_ref.shape[1:], F32)

        _walk_tiles(n, products, tile, flush, ba_ref, bb_ref)

    smem = pl.BlockSpec(memory_space=pltpu.SMEM)
    stat = pl.BlockSpec((2, nq, 1, tq), lambda p: (p, 0, 0, 0))
    blocks_t = pl.BlockSpec((nq, 2 * HEAD_LANES, tq), lambda p: (0, p, 0))
    return pl.pallas_call(
        body, name="attn_bwd", grid=(N_HEADS // 2,),
        in_specs=[smem, smem,
                  pl.BlockSpec((T, 2 * HEAD_LANES), lambda p: (0, p)),
                  pl.BlockSpec((T, LANES), lambda p: (0, p)),
                  blocks_t, blocks_t, blocks_t, stat, stat],
        out_specs=[blocks_t,
                   pl.BlockSpec((T, 2 * HEAD_LANES), lambda p: (0, p)),
                   pl.BlockSpec((T, LANES), lambda p: (0, p))],
        out_shape=[jax.ShapeDtypeStruct((nq, N_HEADS * HEAD_LANES, tq), F32),
                   jax.ShapeDtypeStruct((T, N_HEADS * HEAD_LANES), F32),
                   jax.ShapeDtypeStruct((T, MLA_WIDTH), F32)],
        scratch_shapes=[pltpu.VMEM((4, tq, tq), F32), pltpu.VMEM((4, tq, tq), F32),
                        pltpu.VMEM((2, 2 * HEAD_LANES, tq), F32), pltpu.VMEM((2, LANES, tq), F32),
                        pltpu.VMEM((2, tq, tq), F32)],
        compiler_params=_params(1),
    )(jj, ii, k, v, q_t, k_t, do_t, lse, dsum)


def _lower_bound(lbp):
    a, b = lbp[0:1, :], lbp[1:2, :]
    mx = jnp.maximum(a, b)
    ea, eb = jnp.exp(a - mx), jnp.exp(b - mx)
    return ea / (ea + eb)


def _tri(lower):
    r = lax.broadcasted_iota(jnp.int32, (CHUNK, CHUNK), 0)
    c = lax.broadcasted_iota(jnp.int32, (CHUNK, CHUNK), 1)
    return (c <= r) if lower else (c >= r)


def _hg_gates(hq, hf, lb):
    sq = _sigmoid(hq)
    sf = _sigmoid(hf)
    f = lb + (1.0 - lb) * sf
    g = jnp.log(f)
    gcum = jnp.dot(_tri(True).astype(F32), g, precision=lax.Precision.HIGHEST, preferred_element_type=F32)
    return sq, sf, f, hq * sq, 1.0 - f, gcum


def _head(x, hd):
    return x[:, hd * HG_DIM:(hd + 1) * HG_DIM]


def _all_heads(fn):
    return jnp.concatenate([fn(hd) for hd in range(HG_HEADS)], axis=1)


def _hg_blocks(q, kk, gcum):
    rowi = lax.broadcasted_iota(jnp.int32, gcum.shape, 0)
    out = []
    for blk in range(CHUNK // SUB):
        lo, hi = blk * SUB, (blk + 1) * SUB
        gb = gcum[lo - 1:lo, :] if blk else jnp.zeros_like(gcum[0:1, :])
        eq = jnp.exp(gcum[lo:hi, :] - gb)
        ek = jnp.exp(jnp.where(rowi < hi, gb - gcum, 0.0))
        out.append((eq, ek, (q[lo:hi, :] * eq).astype(BF16), (kk * ek).astype(BF16)))
    return out


def _hg_scores(blocks):
    out = []
    for hd in range(HG_HEADS):
        a = jnp.concatenate([_dot_nt(_head(qb, hd), _head(kb, hd)) for _, _, qb, kb in blocks], axis=0)
        out.append(jnp.where(_tri(True), a, 0.0))
    return out


def _hgrn_fwd(proj, lbp):
    T = proj.shape[0]
    nc = T // CHUNK

    def body(hq_ref, hf_ref, hi_ref, lbp_ref, o_ref, st_ref, state):
        @pl.when(pl.program_id(0) == 0)
        def _():
            state[...] = jnp.zeros_like(state)

        _, _, _, q, kk, gcum = _hg_gates(hq_ref[...], hf_ref[...], _lower_bound(lbp_ref[...]))
        vb = hi_ref[...].astype(BF16)
        a = _hg_scores(_hg_blocks(q, kk, gcum))
        gend = gcum[CHUNK - 1:CHUNK, :]
        qgb = (q * jnp.exp(gcum)).astype(BF16)
        kgeb = (kk * jnp.exp(gend - gcum)).astype(BF16)
        egend = jnp.exp(gend)
        for hd in range(HG_HEADS):
            st = state[hd]
            st_ref[0, hd] = st
            o_ref[:, hd * HG_DIM:(hd + 1) * HG_DIM] = (
                _dot(a[hd].astype(BF16), _head(vb, hd)) + _dot_nt(_head(qgb, hd), st.astype(BF16)))
            state[hd] = st * _head(egend, hd) + _dot_tn(_head(vb, hd), _head(kgeb, hd))

    def col(cb):
        return pl.BlockSpec((CHUNK, HG_WIDTH), lambda i: (i, cb))

    return pl.pallas_call(
        body, name="hgrn_fwd", grid=(nc,),
        in_specs=[col(P_HQ // HG_WIDTH), col(P_HF // HG_WIDTH), col(P_HI // HG_WIDTH), _full((2, HG_WIDTH))],
        out_specs=[pl.BlockSpec((CHUNK, HG_WIDTH), lambda i: (i, 0)),
                   pl.BlockSpec((1, HG_HEADS, HG_DIM, HG_DIM), lambda i: (i, 0, 0, 0))],
        out_shape=[jax.ShapeDtypeStruct((T, HG_WIDTH), F32),
                   jax.ShapeDtypeStruct((nc, HG_HEADS, HG_DIM, HG_DIM), F32)],
        scratch_shapes=[pltpu.VMEM((HG_HEADS, HG_DIM, HG_DIM), F32)],
        compiler_params=_params(1),
    )(proj, proj, proj, lbp)


def _hgrn_bwd(proj, lbp, do_hg, states):
    T = proj.shape[0]
    nc = T // CHUNK

    def body(hq_ref, hf_ref, hi_ref, lbp_ref, do_ref, st_ref, dhq_ref, dhf_ref, dhi_ref, dlb_ref, dstate):
        @pl.when(pl.program_id(0) == 0)
        def _():
            dstate[...] = jnp.zeros_like(dstate)
            dlb_ref[...] = jnp.zeros_like(dlb_ref)

        hq, lb = hq_ref[...], _lower_bound(lbp_ref[...])
        sq, sf, f, q, kk, gcum = _hg_gates(hq, hf_ref[...], lb)
        vb = hi_ref[...].astype(BF16)
        dob = do_ref[...].astype(BF16)
        blocks = _hg_blocks(q, kk, gcum)
        a = _hg_scores(blocks)
        gend = gcum[CHUNK - 1:CHUNK, :]
        eg, egend, ekend = jnp.exp(gcum), jnp.exp(gend), jnp.exp(gend - gcum)
        qg, kge = q * eg, kk * ekend
        qgb, kgeb = qg.astype(BF16), kge.astype(BF16)

        dv, dqg, dkge, st_dst, dq_blk, dk_blk = [], [], [], [], [], []
        for hd in range(HG_HEADS):
            st = st_ref[0, hd]
            dst = dstate[hd]
            dstb = dst.astype(BF16)
            do_h, v_h = _head(dob, hd), _head(vb, hd)
            dv.append(_dot_tn(a[hd].astype(BF16), do_h) + _dot_nt(_head(kgeb, hd), dstb))
            da = jnp.where(_tri(True), _dot_nt(do_h, v_h), 0.0).astype(BF16)
            dqg.append(_dot(do_h, st.astype(BF16)))
            dkge.append(_dot(v_h, dstb))
            st_dst.append(jnp.sum(st * dst, axis=0, keepdims=True))
            dstate[hd] = _dot_tn(do_h, _head(qgb, hd)) + dst * _head(egend, hd)
            dq_blk.append([_dot(da[b * SUB:(b + 1) * SUB, :], _head(kb, hd)) for b, (_, _, _, kb) in enumerate(blocks)])
            dk_blk.append([_dot_tn(da[b * SUB:(b + 1) * SUB, :], _head(qb, hd)) for b, (_, _, qb, _) in enumerate(blocks)])
        dv, dqg, dkge, st_dst = (jnp.concatenate(t, axis=1) for t in (dv, dqg, dkge, st_dst))

        dq_a, dg_q = [], []
        dk_a, dg_k = jnp.zeros_like(gcum), jnp.zeros_like(gcum)
        for b, (eq, ek, qb, kb) in enumerate(blocks):
            dq_b = _all_heads(lambda hd: dq_blk[hd][b])
            dk_b = _all_heads(lambda hd: dk_blk[hd][b])
            dq_a.append(dq_b * eq)
            dk_a = dk_a + dk_b * ek
            dg_q.append(qb.astype(F32) * dq_b)
            dg_k = dg_k + kb.astype(F32) * dk_b
        dq_a = jnp.concatenate(dq_a, axis=0)

        dgend = st_dst * egend + jnp.sum(dkge * kge, axis=0, keepdims=True)
        dq = dq_a + dqg * eg
        dk = dk_a + dkge * ekend
        dgc = jnp.concatenate(dg_q, axis=0) - dg_k + dqg * qg - dkge * kge
        last = lax.broadcasted_iota(jnp.int32, gcum.shape, 0) == CHUNK - 1
        dgc = dgc + jnp.where(last, dgend, 0.0)
        dg = jnp.dot(_tri(False).astype(F32), dgc, precision=lax.Precision.HIGHEST, preferred_element_type=F32)
        df = dg / f - dk
        dhf_ref[...] = df * (1.0 - lb) * sf * (1.0 - sf)
        dlb_ref[...] += jnp.sum(df * (1.0 - sf), axis=0, keepdims=True)
        dhq_ref[...] = dq * (sq * (1.0 + hq * (1.0 - sq)))
        dhi_ref[...] = dv

    def col(cb):
        return pl.BlockSpec((CHUNK, HG_WIDTH), lambda i: (nc - 1 - i, cb))

    grad = jax.ShapeDtypeStruct((T, HG_WIDTH), F32)
    return pl.pallas_call(
        body, name="hgrn_bwd", grid=(nc,),
        in_specs=[col(P_HQ // HG_WIDTH), col(P_HF // HG_WIDTH), col(P_HI // HG_WIDTH), _full((2, HG_WIDTH)),
                  col(0), pl.BlockSpec((1, HG_HEADS, HG_DIM, HG_DIM), lambda i: (nc - 1 - i, 0, 0, 0))],
        out_specs=[col(0), col(0), col(0), _full((1, HG_WIDTH))],
        out_shape=[grad, grad, grad, jax.ShapeDtypeStruct((1, HG_WIDTH), F32)],
        scratch_shapes=[pltpu.VMEM((HG_HEADS, HG_DIM, HG_DIM), F32)],
        compiler_params=_params(1),
    )(proj, proj, proj, lbp, do_hg, states)


def _top(x, tgt, o_mla, o_hg, proj, w_out, hg_norm_g, final_g, tm):
    T = x.shape[0]

    def body(x_ref, tgt_ref, om_ref, oh_ref, gm_ref, gh_ref, wout_ref, hgn_ref, fng_ref,
             dx2_ref, dx2b_ref, ycat_ref, dom_ref, dsum_ref, dgm_ref, doh_ref, dgh_ref, loss_ref, dfng_ref, dhgn_ref):
        @pl.when(pl.program_id(0) == 0)
        def _():
            loss_ref[...] = jnp.zeros_like(loss_ref)
            dfng_ref[...] = jnp.zeros_like(dfng_ref)
            dhgn_ref[...] = jnp.zeros_like(dhgn_ref)

        gm, om = gm_ref[...], om_ref[...]
        sgm = _sigmoid(gm)
        silu_m = gm * sgm
        gh, oh, gam = gh_ref[...], oh_ref[...], hgn_ref[...]
        sgh = _sigmoid(gh)
        silu_h = gh * sgh
        rr, nn = [], []
        for hd in range(HG_HEADS):
            oh_h = oh[:, hd * HG_DIM:(hd + 1) * HG_DIM]
            r_h = lax.rsqrt(jnp.mean(oh_h * oh_h, axis=-1, keepdims=True) + EPS)
            rr.append(r_h)
            nn.append(oh_h * r_h)
        n = jnp.concatenate(nn, axis=1)
        ng = n * gam
        ycat_ref[:, :MLA_WIDTH] = (om * silu_m).astype(BF16)
        ycat_ref[:, MLA_WIDTH:] = (ng * silu_h).astype(BF16)
        wout = wout_ref[...]
        x2 = x_ref[...] + _dot(ycat_ref[...], wout)
        r = lax.rsqrt(jnp.mean(x2 * x2, axis=-1, keepdims=True) + EPS)
        xh = x2 * r
        fng = fng_ref[...]
        err = xh * fng - tgt_ref[...]
        loss_ref[...] += 0.5 * jnp.sum(jnp.mean(err * err, axis=-1, keepdims=True), axis=0, keepdims=True)
        dout = err * (1.0 / D_MODEL)
        dfng_ref[...] += jnp.sum(dout * xh, axis=0, keepdims=True)
        dxh = dout * fng
        dx2 = r * (dxh - xh * jnp.mean(dxh * xh, axis=-1, keepdims=True))
        dx2_ref[...] = dx2
        dx2b = dx2.astype(BF16)
        dx2b_ref[...] = dx2b
        dycat = _dot_nt(dx2b, wout)
        dym, dyh = dycat[:, :MLA_WIDTH], dycat[:, MLA_WIDTH:]
        dom = dym * silu_m
        first = lax.broadcasted_iota(jnp.int32, (tm, LANES), 1) < V_DIM
        for pp in range(N_HEADS // 2):
            pair = dom[:, pp * LANES:(pp + 1) * LANES]
            dom_ref[:, 2 * pp * HEAD_LANES:(2 * pp + 1) * HEAD_LANES] = jnp.where(first, pair, 0.0).astype(BF16)
            dom_ref[:, (2 * pp + 1) * HEAD_LANES:(2 * pp + 2) * HEAD_LANES] = jnp.where(first, 0.0, pair).astype(BF16)
        head_of = lax.broadcasted_iota(jnp.int32, (MLA_WIDTH, LANES), 0) // V_DIM
        pick = (head_of == lax.broadcasted_iota(jnp.int32, (MLA_WIDTH, LANES), 1)).astype(F32)
        dsum_ref[...] = jnp.dot(dom * om, pick, precision=lax.Precision.HIGHEST, preferred_element_type=F32)
        dgm_ref[...] = dym * om * (sgm * (1.0 + gm * (1.0 - sgm)))
        dgh_ref[...] = dyh * ng * (sgh * (1.0 + gh * (1.0 - sgh)))
        dng = dyh * silu_h
        dhgn_ref[...] += jnp.sum(dng * n, axis=0, keepdims=True)
        dn = dng * gam
        for hd in range(HG_HEADS):
            sl = slice(hd * HG_DIM, (hd + 1) * HG_DIM)
            dn_h, n_h = dn[:, sl], nn[hd]
            doh_ref[:, sl] = rr[hd] * (dn_h - n_h * jnp.mean(dn_h * n_h, axis=-1, keepdims=True))

    def row(w, cb=0):
        return pl.BlockSpec((tm, w), lambda i: (i, cb))

    outs = [(D_MODEL, F32), (D_MODEL, BF16), (D_MODEL, BF16), (N_HEADS * HEAD_LANES, BF16), (LANES, F32),
            (MLA_WIDTH, F32), (HG_WIDTH, F32), (HG_WIDTH, F32)]
    small = [(1, LANES), (1, D_MODEL), (1, HG_WIDTH)]
    return pl.pallas_call(
        body, name="top", grid=(T // tm,),
        in_specs=[row(D_MODEL), row(D_MODEL), row(MLA_WIDTH), row(HG_WIDTH),
                  row(MLA_WIDTH, P_GM // MLA_WIDTH), row(HG_WIDTH, P_GH // HG_WIDTH),
                  _full((D_MODEL, D_MODEL)), _full((1, HG_WIDTH)), _full((1, D_MODEL))],
        out_specs=[row(w) for w, _ in outs] + [_full(s) for s in small],
        out_shape=[jax.ShapeDtypeStruct((T, w), dt) for w, dt in outs] + [jax.ShapeDtypeStruct(s, F32) for s in small],
        compiler_params=_params(1),
    )(x, tgt, o_mla, o_hg, proj, proj, w_out, hg_norm_g, final_g)


def _bot(x, dx2, proj, dq, dk, dv, dgm, dhq, dhf, dhi, dgh, c_t, s1_t, s2_t, w_in_p, w_q_p, w_kv_p, ln_g, q_g, kv_g, tm):
    T = x.shape[0]
    lat_w = D_PERM - P_QL

    def body(x_ref, dx2_ref, lat_ref, dq_ref, dk_ref, dv_ref, dgm_ref, dhq_ref, dhf_ref, dhi_ref, dgh_ref,
             c_ref, s1_ref, s2_ref, win_ref, wq_ref, wkv_ref, lng_ref, qg_ref, kvg_ref,
             dx_ref, dproj_ref, dqpre_ref, dkv_ref, dlng_ref, dqg_ref, dkvg_ref):
        @pl.when(pl.program_id(0) == 0)
        def _():
            dlng_ref[...] = jnp.zeros_like(dlng_ref)
            dqg_ref[...] = jnp.zeros_like(dqg_ref)
            dkvg_ref[...] = jnp.zeros_like(dkvg_ref)

        c, s1, s2 = c_ref[...], s1_ref[...], s2_ref[...]
        dkpe = jnp.zeros((tm, LANES), F32)
        for hd in range(N_HEADS):
            sl = slice(hd * HEAD_LANES, (hd + 1) * HEAD_LANES)
            dqpre_ref[:, sl] = _rope_bwd(dq_ref[:, sl], c, s1, s2).astype(BF16)
            dk_h = dk_ref[:, sl]
            dkpe = dkpe + dk_h
            dkv_ref[:, sl] = dk_h.astype(BF16)
        dkv_ref[:, N_HEADS * HEAD_LANES:] = dv_ref[...].astype(BF16)
        lane = lax.broadcasted_iota(jnp.int32, (tm, LANES), 1)
        rope_lanes = jnp.logical_and(lane >= ROPE_LO, lane < ROPE_LO + ROPE)
        dkr = jnp.where(rope_lanes, _rope_bwd(dkpe, c, s1, s2), 0.0)

        def norm_bwd(v, g, dy):
            r = lax.rsqrt(jnp.mean(v * v, axis=-1, keepdims=True) + EPS)
            vh = v * r
            dvh = dy * g
            return jnp.sum(dy * vh, axis=0, keepdims=True), r * (dvh - vh * jnp.mean(dvh * vh, axis=-1, keepdims=True))

        dqn = _dot_nt(dqpre_ref[...], wq_ref[...])
        dg_q, dql = norm_bwd(lat_ref[:, :Q_RANK], qg_ref[...], dqn)
        dqg_ref[...] += dg_q
        dkn = _dot_nt(dkv_ref[...], wkv_ref[...])
        dg_kv, dkvl = norm_bwd(lat_ref[:, Q_RANK:Q_RANK + KV_RANK], kvg_ref[...], dkn)
        dkvg_ref[...] += dg_kv

        dproj_ref[:, P_GM:P_GM + MLA_WIDTH] = dgm_ref[...].astype(BF16)
        dproj_ref[:, P_HQ:P_HQ + HG_WIDTH] = dhq_ref[...].astype(BF16)
        dproj_ref[:, P_HF:P_HF + HG_WIDTH] = dhf_ref[...].astype(BF16)
        dproj_ref[:, P_HI:P_HI + HG_WIDTH] = dhi_ref[...].astype(BF16)
        dproj_ref[:, P_GH:P_GH + HG_WIDTH] = dgh_ref[...].astype(BF16)
        dproj_ref[:, P_QL:P_QL + Q_RANK] = dql.astype(BF16)
        dproj_ref[:, P_KVL:P_KVL + KV_RANK] = dkvl.astype(BF16)
        dproj_ref[:, P_KR:P_KR + LANES] = dkr.astype(BF16)
        dh = _dot_nt(dproj_ref[...], win_ref[...])
        dg_ln, dxn = norm_bwd(x_ref[...], lng_ref[...], dh)
        dlng_ref[...] += dg_ln
        dx_ref[...] = dx2_ref[...] + dxn

    def row(w, cb=0):
        return pl.BlockSpec((tm, w), lambda i: (i, cb))

    hl = N_HEADS * HEAD_LANES
    outs = [(D_MODEL, F32), (D_PERM, BF16), (hl, BF16), (hl + MLA_WIDTH, BF16)]
    small = [(1, D_MODEL), (1, Q_RANK), (1, KV_RANK)]
    return pl.pallas_call(
        body, name="bot", grid=(T // tm,),
        in_specs=[row(D_MODEL), row(D_MODEL), row(lat_w, P_QL // lat_w), row(hl), row(hl), row(MLA_WIDTH),
                  row(MLA_WIDTH), row(HG_WIDTH), row(HG_WIDTH), row(HG_WIDTH), row(HG_WIDTH),
                  row(LANES), row(LANES), row(LANES),
                  _full((D_MODEL, D_PERM)), _full((Q_RANK, hl)), _full((KV_RANK, hl + MLA_WIDTH)),
                  _full((1, D_MODEL)), _full((1, Q_RANK)), _full((1, KV_RANK))],
        out_specs=[row(w) for w, _ in outs] + [_full(s) for s in small],
        out_shape=[jax.ShapeDtypeStruct((T, w), dt) for w, dt in outs] + [jax.ShapeDtypeStruct(s, F32) for s in small],
        compiler_params=_params(1),
    )(x, dx2, proj, dq, dk, dv, dgm, dhq, dhf, dhi, dgh, c_t, s1_t, s2_t, w_in_p, w_q_p, w_kv_p, ln_g, q_g, kv_g)


def _matmul_tn(a, b, bn, bt, name):
    T, M = a.shape
    N = b.shape[1]

    def body(a_ref, b_ref, o_ref):
        @pl.when(pl.program_id(1) == 0)
        def _():
            o_ref[...] = jnp.zeros_like(o_ref)

        o_ref[...] += _dot_tn(a_ref[...], b_ref[...])

    return pl.pallas_call(
        body, name=name, grid=(N // bn, T // bt),
        in_specs=[pl.BlockSpec((bt, M), lambda n, t: (t, 0)), pl.BlockSpec((bt, bn), lambda n, t: (t, n))],
        out_specs=pl.BlockSpec((M, bn), lambda n, t: (0, n)),
        out_shape=jax.ShapeDtypeStruct((M, N), F32),
        compiler_params=_params(2),
    )(a, b)


RS_ROWS = 256


def _reduce_scatter(slabs, small):
    n = len(slabs)
    units = [(a, r0, min(s.shape[1], RS_ROWS)) for a, s in enumerate(slabs) for r0 in range(0, s.shape[1], RS_ROWS)]
    nu = len(units)

    def body(*refs):
        ins, small_ref = refs[:n], refs[n]
        outs, small_out = refs[n + 1:2 * n + 1], refs[2 * n + 1]
        own, sib_land, ici_out, ici_land = (refs[(2 + g) * n + 2:(3 + g) * n + 2] for g in range(4))
        small_land = refs[6 * n + 2]
        loc_sems, d2d_send, d2d_recv, ici_send, ici_recv, sm_send, sm_recv = refs[6 * n + 3:6 * n + 10]
        x, y, c = lax.axis_index("x"), lax.axis_index("y"), lax.axis_index("c")
        me = 4 * x + 2 * y + c

        def chip(k):
            return (1 - x if k & 2 else x, 1 - y if k & 1 else y)

        def block(k, core):
            px, py = chip(k)
            return 4 * px + 2 * py + core

        def part(u):
            a, r0, nr = units[u]
            return a, pl.ds(r0, nr)

        def local(u, k):
            a, rows = part(u)
            return pltpu.make_async_copy(ins[a].at[block(k, c), rows, :], own[a].at[k, rows, :], loc_sems.at[u, k])

        def to_sibling(u, k):
            a, rows = part(u)
            return pltpu.make_async_remote_copy(
                src_ref=ins[a].at[block(k, 1 - c), rows, :], dst_ref=sib_land[a].at[k, rows, :],
                send_sem=d2d_send.at[u, k], recv_sem=d2d_recv.at[u, k], device_id=(x, y, 1 - c), device_id_type=MESH)

        def to_chip(u, k):
            a, rows = part(u)
            return pltpu.make_async_remote_copy(
                src_ref=ici_out[a].at[k - 1, rows, :], dst_ref=ici_land[a].at[k - 1, rows, :],
                send_sem=ici_send.at[u, k - 1], recv_sem=ici_recv.at[u, k - 1], device_id=(*chip(k), c),
                device_id_type=MESH)

        def small_copy(k, receiving):
            px, py = chip(k >> 1)
            pc = 1 - c if k & 1 else c
            slot = 4 * px + 2 * py + pc if receiving else me
            return pltpu.make_async_remote_copy(
                src_ref=small_ref, dst_ref=small_land.at[slot], send_sem=sm_send.at[k - 1], recv_sem=sm_recv.at[k - 1],
                device_id=(px, py, pc), device_id_type=MESH)

        for u in range(nu):
            for k in range(4):
                local(u, k).start()
        for u in range(nu):
            for k in range(4):
                to_sibling(u, k).start()
        small_land[me] = small_ref[...]
        for k in range(1, N_DEV):
            small_copy(k, False).start()
        for u in range(nu):
            a, rows = part(u)
            for k in range(4):
                local(u, k).wait()
                to_sibling(u, k).wait_recv()
            for k in range(1, 4):
                ici_out[a][k - 1, rows, :] = (own[a][k, rows, :] + sib_land[a][k, rows, :]).astype(BF16)
                to_chip(u, k).start()
        for u in range(nu):
            a, rows = part(u)
            acc = own[a][0, rows, :] + sib_land[a][0, rows, :]
            for k in range(1, 4):
                to_chip(u, k).wait_recv()
                acc = acc + ici_land[a][k - 1, rows, :].astype(F32)
            outs[a][rows, :] = acc
        for k in range(1, N_DEV):
            small_copy(k, True).wait_recv()
        acc = small_land[0]
        for d in range(1, N_DEV):
            acc = acc + small_land[d]
        small_out[...] = acc
        for u in range(nu):
            for k in range(4):
                to_sibling(u, k).wait_send()
            for k in range(1, 4):
                to_chip(u, k).wait_send()
        for k in range(1, N_DEV):
            small_copy(k, False).wait_send()

    vm = pl.BlockSpec(memory_space=pltpu.VMEM)
    hbm = pl.BlockSpec(memory_space=pl.ANY)
    dma = pltpu.SemaphoreType.DMA
    return pl.pallas_call(
        body, name="reduce_scatter_grads",
        in_specs=[hbm] * n + [vm], out_specs=[vm] * (n + 1),
        out_shape=[jax.ShapeDtypeStruct(s.shape[1:], F32) for s in slabs] + [jax.ShapeDtypeStruct(small.shape, F32)],
        scratch_shapes=[pltpu.VMEM((4,) + s.shape[1:], F32) for s in slabs] * 2
        + [pltpu.VMEM((3,) + s.shape[1:], BF16) for s in slabs] * 2
        + [pltpu.VMEM((N_DEV,) + small.shape, F32)]
        + [dma((nu, 4)), dma((nu, 4)), dma((nu, 4)), dma((nu, 3)), dma((nu, 3)), dma((N_DEV - 1,)), dma((N_DEV - 1,))],
        compiler_params=pltpu.CompilerParams(vmem_limit_bytes=VMEM_LIMIT),
    )(*slabs, small)


def _adamw_math(w, g, m, v):
    m = ADAM_B1 * m + (1.0 - ADAM_B1) * g
    v = ADAM_B2 * v + (1.0 - ADAM_B2) * (g * g)
    m_hat = m / (1.0 - ADAM_B1 ** ADAM_STEP)
    v_hat = v / (1.0 - ADAM_B2 ** ADAM_STEP)
    delta = -ADAM_LR * (m_hat / (jnp.sqrt(v_hat) + ADAM_EPS) + ADAM_WD * w)
    return delta, m, v


SMALL_W = 512


def _adamw(big, small_w, small_g):
    nb, ns = len(big), len(small_w)

    def body(*refs):
        k = 0
        big_in = [refs[4 * i:4 * i + 4] for i in range(nb)]
        k = 4 * nb
        small_in = [refs[k + 3 * i:k + 3 * i + 3] for i in range(ns)]
        k += 3 * ns
        sg_ref = refs[k]
        k += 1
        big_out = [refs[k + 3 * i:k + 3 * i + 3] for i in range(nb)]
        k += 3 * nb
        small_out = [refs[k + 4 * i:k + 4 * i + 4] for i in range(ns)]

        for (w, g, m, v), (od, om, ov) in zip(big_in, big_out):
            od[...], om[...], ov[...] = _adamw_math(w[...], g[...], m[...], v[...])

        sg = sg_ref[...]
        lbp = small_in[2][0][...]
        lb = _lower_bound(lbp)
        t = sg[4:5, :] * lb * (1.0 - lb)
        grads = [jnp.concatenate([sg[0:1, :], sg[1:2, :]], axis=1),
                 jnp.concatenate([sg[2:3, :], sg[3:4, :]], axis=1),
                 jnp.concatenate([t, -t], axis=0),
                 sg[6:7, :], sg[7:8, 0:Q_RANK], sg[7:8, Q_RANK:Q_RANK + KV_RANK]]
        for (w, m, v), g, (og, od, om, ov) in zip(small_in, grads, small_out):
            og[...] = g
            od[...], om[...], ov[...] = _adamw_math(w[...], g, m[...], v[...])

    ins = [a for grp in big for a in grp] + [a for grp in small_w for a in grp] + [small_g]
    out_shape = ([jax.ShapeDtypeStruct(grp[0].shape, F32) for grp in big for _ in range(3)]
                 + [jax.ShapeDtypeStruct(grp[0].shape, F32) for grp in small_w for _ in range(4)])
    vm = pl.BlockSpec(memory_space=pltpu.VMEM)
    res = pl.pallas_call(
        body, name="adamw", in_specs=[vm] * len(ins), out_specs=[vm] * len(out_shape), out_shape=out_shape,
        compiler_params=pltpu.CompilerParams(vmem_limit_bytes=VMEM_LIMIT),
    )(*ins)
    big_res = [res[3 * i:3 * i + 3] for i in range(nb)]
    small_res = [res[3 * nb + 4 * i:3 * nb + 4 * i + 4] for i in range(ns)]
    return big_res, small_res


def _perm_weights(g_in, g_q, g_kv, g_out):
    w = g_in.transpose(1, 0, 2).reshape(D_MODEL, D_IN)
    z = lambda n: jnp.zeros((D_MODEL, n), BF16)
    w_in_p = jnp.concatenate([w[:, 416:], w[:, :384], z(64), w[:, 384:416], z(32)], axis=1)
    wq = g_q.transpose(1, 0, 2)
    w_q_p = jnp.pad(wq, ((0, 0), (0, 0), (0, HEAD_LANES - NOPE - ROPE))).reshape(Q_RANK, N_HEADS * HEAD_LANES)
    wkv = g_kv.transpose(1, 0, 2)
    wk = jnp.pad(wkv[:, :, :NOPE], ((0, 0), (0, 0), (0, HEAD_LANES - NOPE))).reshape(KV_RANK, N_HEADS * HEAD_LANES)
    wv = wkv[:, :, NOPE:].reshape(KV_RANK, MLA_WIDTH)
    return w_in_p, w_q_p, jnp.concatenate([wk, wv], axis=1), g_out.reshape(D_MODEL, D_MODEL)


def _grad_slabs(dw_in_p, dw_q_p, dw_kv_p, dw_out):
    dw_in = jnp.concatenate([dw_in_p[:, P_QL:P_KR], dw_in_p[:, P_KR + ROPE_LO:P_KR + ROPE_LO + ROPE], dw_in_p[:, :P_QL]], axis=1)
    s_in = dw_in.reshape(D_MODEL, N_DEV, D_IN // N_DEV).transpose(1, 0, 2)
    s_q = dw_q_p.reshape(Q_RANK, N_HEADS, HEAD_LANES)[:, :, :NOPE + ROPE].transpose(1, 0, 2)
    hl = N_HEADS * HEAD_LANES
    dk = dw_kv_p[:, :hl].reshape(KV_RANK, N_HEADS, HEAD_LANES)[:, :, :NOPE]
    dv = dw_kv_p[:, hl:].reshape(KV_RANK, N_HEADS, V_DIM)
    s_kv = jnp.concatenate([dk, dv], axis=2).transpose(1, 0, 2)
    return s_in, s_q, s_kv, dw_out.reshape(N_DEV, D_MODEL // N_DEV, D_MODEL)


def _block_sizes(T):
    return min(256, T), min(256, T), min(512, T)


def kernel(x, positions, ln_g, w_in, q_a_norm_g, w_q_b, kv_a_norm_g, w_kv_b, hg_lower_bounds, hg_norm_g, w_out, final_norm_g, loss_target, m_ln_g, m_w_in, m_q_a_norm_g, m_w_q_b, m_kv_a_norm_g, m_w_kv_b, m_hg_lower_bounds, m_hg_norm_g, m_w_out, m_final_norm_g, v_ln_g, v_w_in, v_q_a_norm_g, v_w_q_b, v_kv_a_norm_g, v_w_kv_b, v_hg_lower_bounds, v_hg_norm_g, v_w_out, v_final_norm_g):
    T = x.shape[1]
    tm, tq, bt = _block_sizes(T)
    nq = T // tq
    xs, tgt = x[0], loss_target[0]
    pos_f = positions.astype(F32)
    fng = final_norm_g.reshape(1, D_MODEL)

    gathered = _all_gather_weights([w_in[0], w_q_b[0], w_kv_b[0], w_out[0]])
    w_in_p, w_q_p, w_kv_p, w_out_b = _perm_weights(*gathered)

    c_t, s1_t, s2_t = _rope_tables(pos_f, bt)
    proj, h, qn, kvn, q, k, v = _fwd_in(xs, ln_g, w_in_p, q_a_norm_g, w_q_p, kv_a_norm_g, w_kv_p, c_t, s1_t, s2_t, tm)
    hl = N_HEADS * HEAD_LANES
    v_t = v.reshape(nq, tq, MLA_WIDTH).transpose(0, 2, 1)
    k_t = k.reshape(nq, tq, hl).transpose(0, 2, 1)
    q_t = q.reshape(nq, tq, hl).transpose(0, 2, 1)
    o_mla, lse = _attn_fwd_flat(k, q_t, v_t, tq)
    o_hg, states = _hgrn_fwd(proj, hg_lower_bounds)
    dx2, dx2b, ycat, d_om, dsum, d_gm, d_oh, d_gh, loss_p, d_fng, d_hgn = _top(
        xs, tgt, o_mla, o_hg, proj, w_out_b, hg_norm_g, fng, tm)
    dsum = dsum[:, :N_HEADS].T.reshape(N_HEADS, nq, 1, tq)
    do_t = d_om.reshape(nq, tq, hl).transpose(0, 2, 1)
    dq_t, dk, dv = _attn_bwd_flat(k, v, q_t, k_t, do_t, lse, dsum, tq)
    dq = dq_t.transpose(0, 2, 1).reshape(T, N_HEADS * HEAD_LANES)
    d_hq, d_hf, d_hi, d_lb = _hgrn_bwd(proj, hg_lower_bounds, d_oh, states)
    dx, dproj, dq_pre, dkv, d_lng, d_qg, d_kvg = _bot(
        xs, dx2, proj, dq, dk, dv, d_gm, d_hq, d_hf, d_hi, d_gh, c_t, s1_t, s2_t, w_in_p, w_q_p, w_kv_p,
        ln_g, q_a_norm_g, kv_a_norm_g, tm)
    dw_in_p = _matmul_tn(h, dproj, 512, bt, "dw_in")
    dw_out = _matmul_tn(ycat, dx2b, 512, bt, "dw_out")
    dw_q_p = _matmul_tn(qn, dq_pre, 512, bt, "dw_q_b")
    dw_kv_p = _matmul_tn(kvn, dkv, 512, bt, "dw_kv_b")

    zrow = jnp.zeros((1, SMALL_W), F32)
    small = jnp.concatenate([
        d_lng.reshape(2, SMALL_W), d_fng.reshape(2, SMALL_W), d_lb, zrow, d_hgn,
        jnp.concatenate([d_qg, d_kvg, jnp.zeros((1, SMALL_W - Q_RANK - KV_RANK), F32)], axis=1)], axis=0)
    g_in, g_q, g_kv, g_out, small_sum = _reduce_scatter(list(_grad_slabs(dw_in_p, dw_q_p, dw_kv_p, dw_out)), small)

    big = [(w_in[0], g_in, m_w_in[0], v_w_in[0]), (w_q_b[0], g_q, m_w_q_b[0], v_w_q_b[0]),
           (w_kv_b[0], g_kv, m_w_kv_b[0], v_w_kv_b[0]), (w_out[0], g_out, m_w_out[0], v_w_out[0])]
    small_w = [(ln_g, m_ln_g, v_ln_g),
               (fng, m_final_norm_g.reshape(1, D_MODEL), v_final_norm_g.reshape(1, D_MODEL)),
               (hg_lower_bounds, m_hg_lower_bounds, v_hg_lower_bounds), (hg_norm_g, m_hg_norm_g, v_hg_norm_g),
               (q_a_norm_g, m_q_a_norm_g, v_q_a_norm_g), (kv_a_norm_g, m_kv_a_norm_g, v_kv_a_norm_g)]
    big_res, small_res = _adamw(big, small_w, small_sum)

    loss = lax.psum(loss_p[0, 0], ("x", "y", "c"))
    (r_in, r_q, r_kv, r_out) = big_res
    (s_ln, s_fn, s_lb, s_hgn, s_qg, s_kvg) = small_res
    flat = lambda t: t.reshape(D_MODEL)
    lead = lambda t: t[None]
    grads = [s_ln[0], lead(g_in), s_qg[0], lead(g_q), s_kvg[0], lead(g_kv), s_lb[0], s_hgn[0], lead(g_out), flat(s_fn[0])]

    def pick(i):
        return [s_ln[i + 1], lead(r_in[i]), s_qg[i + 1], lead(r_q[i]), s_kvg[i + 1], lead(r_kv[i]), s_lb[i + 1],
                s_hgn[i + 1], lead(r_out[i]), flat(s_fn[i + 1])]

    return (loss, dx[None], *grads, *pick(0), *pick(1), *pick(2))
```

```python
import math

import numpy as np
import jax
import jax.numpy as jnp
from jax import lax
from jax.experimental import pallas as pl
from jax.experimental.pallas import tpu as pltpu

F32 = jnp.float32
BF16 = jnp.bfloat16

D_MODEL = 1024
N_HEADS = 8
NOPE = 64
ROPE = 32
HALF_ROPE = ROPE // 2
V_DIM = 64
Q_RANK = 256
KV_RANK = 128
MLA_WIDTH = N_HEADS * V_DIM
HG_HEADS = 4
HG_DIM = 128
HG_WIDTH = HG_HEADS * HG_DIM
CHUNK = 64
SUB = 16
D_IN = 2976
D_PERM = 3072
ROPE_THETA = 10000.0
EPS = 1e-6
N_DEV = 8
LANES = 128
HEAD_LANES = 128

P_GM, P_HQ, P_HF, P_HI, P_GH, P_QL, P_KVL, P_KR = 0, 512, 1024, 1536, 2048, 2560, 2816, 2944
ROPE_LO = NOPE
SCALE = 1.0 / math.sqrt(NOPE + ROPE)

ADAM_LR = 0.001
ADAM_B1 = 0.9
ADAM_B2 = 0.999
ADAM_EPS = 1e-08
ADAM_WD = 0.01
ADAM_STEP = 10

VMEM_LIMIT = 56 * 1024 * 1024
MESH = pl.DeviceIdType.MESH

NT = (((1,), (1,)), ((), ()))
TN = (((0,), (0,)), ((), ()))


def _params(n_grid=0, **kw):
    sem = ("arbitrary",) * n_grid if n_grid else None
    return pltpu.CompilerParams(dimension_semantics=sem, vmem_limit_bytes=VMEM_LIMIT, **kw)


def _dot(a, b):
    return jnp.dot(a, b, preferred_element_type=F32)


def _dot_nt(a, b):
    return lax.dot_general(a, b, NT, preferred_element_type=F32)


def _dot_tn(a, b):
    return lax.dot_general(a, b, TN, preferred_element_type=F32)


def _sigmoid(x):
    return 1.0 / (1.0 + jnp.exp(-x))


def _rope_fwd(x, c, s1, s2):
    return x * c + pltpu.roll(x, LANES - HALF_ROPE, 1) * s1 + pltpu.roll(x, HALF_ROPE, 1) * s2


def _rope_bwd(dy, c, s1, s2):
    return dy * c - pltpu.roll(dy, LANES - HALF_ROPE, 1) * s1 - pltpu.roll(dy, HALF_ROPE, 1) * s2


def _full(shape):
    n = len(shape)
    return pl.BlockSpec(shape, lambda *_: (0,) * n)


def _rope_tables(pos_f, tm):
    T = pos_f.shape[1]
    inv = (np.float32(ROPE_THETA) ** (-np.arange(HALF_ROPE, dtype=np.float32) / np.float32(HALF_ROPE))).astype(np.float32)
    place = np.zeros((3, HALF_ROPE, LANES), np.float32)
    for i in range(HALF_ROPE):
        place[0, i, ROPE_LO + i] = place[0, i, ROPE_LO + HALF_ROPE + i] = 1.0
        place[1, i, ROPE_LO + i] = -1.0
        place[2, i, ROPE_LO + HALF_ROPE + i] = 1.0
    base = np.ones((1, LANES), np.float32)
    base[0, ROPE_LO:ROPE_LO + ROPE] = 0.0

    def body(pos_ref, inv_ref, place_ref, base_ref, c_ref, s1_ref, s2_ref):
        ang = inv_ref[...] * pos_ref[...]
        cos, sin = jnp.cos(ang), jnp.sin(ang)

        def put(v, k):
            return lax.dot_general(v, place_ref[k], TN, precision=lax.Precision.HIGHEST, preferred_element_type=F32)

        c_ref[...] = put(cos, 0) + base_ref[...]
        s1_ref[...] = put(sin, 1)
        s2_ref[...] = put(sin, 2)

    tab = jax.ShapeDtypeStruct((T, LANES), F32)
    return pl.pallas_call(
        body, name="rope_tables", grid=(T // tm,),
        in_specs=[pl.BlockSpec((1, tm), lambda i: (0, i)), _full((HALF_ROPE, 1)), _full((3, HALF_ROPE, LANES)),
                  _full((1, LANES))],
        out_specs=[pl.BlockSpec((tm, LANES), lambda i: (i, 0))] * 3,
        out_shape=[tab, tab, tab], compiler_params=_params(1),
    )(pos_f, jnp.asarray(inv.reshape(HALF_ROPE, 1)), jnp.asarray(place), jnp.asarray(base))


def _all_gather_weights(shards):
    n = len(shards)

    def body(*refs):
        ins, outs = refs[:n], refs[n:2 * n]
        send_sems, recv_sems = refs[2 * n], refs[2 * n + 1]
        x, y, c = lax.axis_index("x"), lax.axis_index("y"), lax.axis_index("c")
        me, sibling = (x, y, c), (x, y, 1 - c)
        chips = [(1 - x, y), (x, 1 - y), (1 - x, 1 - y)]

        def idx(d):
            return 4 * d[0] + 2 * d[1] + d[2]

        def copy(a, k, block, to):
            rows = outs[a].at[idx(block)]
            return pltpu.make_async_remote_copy(src_ref=rows, dst_ref=rows, send_sem=send_sems.at[a, k],
                                                recv_sem=recv_sems.at[a, k], device_id=to, device_id_type=MESH)

        for a in range(n):
            outs[a][idx(me)] = ins[a][...].astype(BF16)
        first = []
        for a in range(n):
            first.append(copy(a, 0, me, sibling))
            first += [copy(a, 1 + j, me, (*chip, c)) for j, chip in enumerate(chips)]
        for cp in first:
            cp.start()
        passed = []
        for j, chip in enumerate(chips):
            for a in range(n):
                copy(a, 1 + j, (*chip, c), me).wait_recv()
                cp = copy(a, 4 + j, (*chip, c), sibling)
                cp.start()
                passed.append(cp)
        for a in range(n):
            copy(a, 0, sibling, me).wait_recv()
            for j, chip in enumerate(chips):
                copy(a, 4 + j, (*chip, 1 - c), me).wait_recv()
        for cp in first + passed:
            cp.wait_send()

    vm = pl.BlockSpec(memory_space=pltpu.VMEM)
    return pl.pallas_call(
        body, name="all_gather_weights",
        in_specs=[vm] * n, out_specs=[vm] * n,
        out_shape=[jax.ShapeDtypeStruct((N_DEV,) + s.shape, BF16) for s in shards],
        scratch_shapes=[pltpu.SemaphoreType.DMA((n, 7)), pltpu.SemaphoreType.DMA((n, 7))],
        compiler_params=pltpu.CompilerParams(vmem_limit_bytes=VMEM_LIMIT),
    )(*shards)


def _fwd_in(x, ln_g, w_in_p, q_g, w_q_p, kv_g, w_kv_p, c_t, s1_t, s2_t, tm):
    T = x.shape[0]

    def body(x_ref, lng_ref, win_ref, qg_ref, wq_ref, kvg_ref, wkv_ref, c_ref, s1_ref, s2_ref,
             proj_ref, h_ref, qn_ref, kvn_ref, q_ref, k_ref, v_ref):
        xv = x_ref[...]
        r = lax.rsqrt(jnp.mean(xv * xv, axis=-1, keepdims=True) + EPS)
        h = (xv * r * lng_ref[...]).astype(BF16)
        h_ref[...] = h
        proj = _dot(h, win_ref[...])
        proj_ref[...] = proj
        c, s1, s2 = c_ref[...], s1_ref[...], s2_ref[...]

        ql = proj[:, P_QL:P_QL + Q_RANK]
        rq = lax.rsqrt(jnp.mean(ql * ql, axis=-1, keepdims=True) + EPS)
        qn = (ql * rq * qg_ref[...]).astype(BF16)
        qn_ref[...] = qn
        q = _dot(qn, wq_ref[...])
        for hd in range(N_HEADS):
            sl = slice(hd * HEAD_LANES, (hd + 1) * HEAD_LANES)
            q_ref[:, sl] = _rope_fwd(q[:, sl], c, s1, s2).astype(BF16)

        kvl = proj[:, P_KVL:P_KVL + KV_RANK]
        rk = lax.rsqrt(jnp.mean(kvl * kvl, axis=-1, keepdims=True) + EPS)
        kvn = (kvl * rk * kvg_ref[...]).astype(BF16)
        kvn_ref[...] = kvn
        kv = _dot(kvn, wkv_ref[...])
        kpe = _rope_fwd(proj[:, P_KR:P_KR + LANES], c, s1, s2)
        for hd in range(N_HEADS):
            sl = slice(hd * HEAD_LANES, (hd + 1) * HEAD_LANES)
            k_ref[:, sl] = (kv[:, sl] + kpe).astype(BF16)
        v_ref[...] = kv[:, N_HEADS * HEAD_LANES:].astype(BF16)

    def row(w):
        return pl.BlockSpec((tm, w), lambda i: (i, 0))

    outs = [(D_PERM, F32), (D_MODEL, BF16), (Q_RANK, BF16), (KV_RANK, BF16),
            (N_HEADS * HEAD_LANES, BF16), (N_HEADS * HEAD_LANES, BF16), (MLA_WIDTH, BF16)]
    return pl.pallas_call(
        body, name="fwd_in", grid=(T // tm,),
        in_specs=[row(D_MODEL), _full((1, D_MODEL)), _full((D_MODEL, D_PERM)), _full((1, Q_RANK)),
                  _full((Q_RANK, N_HEADS * HEAD_LANES)), _full((1, KV_RANK)),
                  _full((KV_RANK, N_HEADS * HEAD_LANES + MLA_WIDTH)), row(LANES), row(LANES), row(LANES)],
        out_specs=[row(w) for w, _ in outs],
        out_shape=[jax.ShapeDtypeStruct((T, w), dt) for w, dt in outs],
        compiler_params=_params(1),
    )(x, ln_g, w_in_p, q_g, w_q_p, kv_g, w_kv_p, c_t, s1_t, s2_t)


LOG2E = 1.4426950408889634
SCALE2 = SCALE * LOG2E


def _causal(tq):
    r = lax.broadcasted_iota(jnp.int32, (tq, tq), 0)
    c = lax.broadcasted_iota(jnp.int32, (tq, tq), 1)
    return r <= c


MASKED = -1e30


def _causal_bias(bias_ref, tq):
    bias_ref[0] = jnp.zeros((tq, tq), F32)
    bias_ref[1] = jnp.where(_causal(tq), 0.0, MASKED)


def _tile_tables(nq, by_query):
    if by_query:
        pairs = [(j, i) for i in range(nq) for j in range(i + 1)]
    else:
        pairs = [(j, i) for j in range(nq) for i in range(nq - 1, j - 1, -1)]
    pairs.append(pairs[-1])
    jj, ii = np.array(pairs, np.int32).T
    return jnp.asarray(jj), jnp.asarray(ii), len(pairs) - 1


def _walk_tiles(n, products, tile, flush, buf_a, buf_b):
    products(0, buf_a)

    def two(r, carry):
        products(2 * r + 1, buf_b)
        tile(2 * r, buf_a)
        products(2 * r + 2, buf_a)
        tile(2 * r + 1, buf_b)
        flush(2 * r)
        flush(2 * r + 1)
        return carry

    lax.fori_loop(0, n // 2, two, 0)
    if n % 2:
        tile(n - 1, buf_a)
        flush(n - 1)


def _attn_fwd_flat(k, q_t, v_t, tq):
    T = k.shape[0]
    nq = T // tq
    jj, ii, n = _tile_tables(nq, True)
    heads = [slice(hh * HEAD_LANES, (hh + 1) * HEAD_LANES) for hh in range(2)]

    def body(jj_ref, ii_ref, k_ref, qt_ref, vt_ref, o_ref, lse_ref, sa_ref, sb_ref, m_ref, l_ref, acc_ref, bias_ref):
        def reset(st):
            m_ref[st] = jnp.full(m_ref.shape[1:], MASKED, F32)
            l_ref[st] = jnp.zeros(l_ref.shape[1:], F32)
            acc_ref[st] = jnp.zeros(acc_ref.shape[1:], F32)

        _causal_bias(bias_ref, tq)
        reset(0)
        reset(1)

        def products(t, buf):
            j, i = jj_ref[t], ii_ref[t]
            kj = k_ref[pl.ds(pl.multiple_of(j * tq, tq), tq), :]
            for hh, sl in enumerate(heads):
                buf[hh] = _dot(kj[:, sl], qt_ref[i, sl, :])

        def tile(t, buf):
            j, i = jj_ref[t], ii_ref[t]
            vt = vt_ref[j]
            bias = bias_ref.at[(j == i).astype(jnp.int32)]
            st = i % 2
            for hh in range(2):
                s = buf[hh] * SCALE2 + bias[...]
                m = m_ref[st, hh]
                m_new = jnp.maximum(m, jnp.max(s, axis=0, keepdims=True))
                alpha = jnp.exp2(m - m_new)
                p = jnp.exp2(s - m_new)
                m_ref[st, hh] = m_new
                l_ref[st, hh] = alpha * l_ref[st, hh] + jnp.sum(p, axis=0, keepdims=True)
                acc_ref[st, hh] = alpha * acc_ref[st, hh] + _dot(vt, p.astype(BF16))

        def flush(t):
            j, i = jj_ref[t], ii_ref[t]

            @pl.when(j == i)
            def _():
                st = i % 2
                first = lax.broadcasted_iota(jnp.int32, (LANES, tq), 0) < V_DIM
                out = jnp.where(first, acc_ref[st, 0] / l_ref[st, 0], acc_ref[st, 1] / l_ref[st, 1])
                o_ref[pl.ds(pl.multiple_of(i * tq, tq), tq), :] = out.T
                for hh in range(2):
                    lse_ref[hh, i] = m_ref[st, hh] + jnp.log2(l_ref[st, hh])
                reset(st)

        _walk_tiles(n, products, tile, flush, sa_ref, sb_ref)

    smem = pl.BlockSpec(memory_space=pltpu.SMEM)
    return pl.pallas_call(
        body, name="attn_fwd", grid=(N_HEADS // 2,),
        in_specs=[smem, smem,
                  pl.BlockSpec((T, 2 * HEAD_LANES), lambda p: (0, p)),
                  pl.BlockSpec((nq, 2 * HEAD_LANES, tq), lambda p: (0, p, 0)),
                  pl.BlockSpec((nq, LANES, tq), lambda p: (0, p, 0))],
        out_specs=[pl.BlockSpec((T, LANES), lambda p: (0, p)),
                   pl.BlockSpec((2, nq, 1, tq), lambda p: (p, 0, 0, 0))],
        out_shape=[jax.ShapeDtypeStruct((T, MLA_WIDTH), F32), jax.ShapeDtypeStruct((N_HEADS, nq, 1, tq), F32)],
        scratch_shapes=[pltpu.VMEM((2, tq, tq), F32), pltpu.VMEM((2, tq, tq), F32),
                        pltpu.VMEM((2, 2, 1, tq), F32), pltpu.VMEM((2, 2, 1, tq), F32),
                        pltpu.VMEM((2, 2, LANES, tq), F32), pltpu.VMEM((2, tq, tq), F32)],
        compiler_params=_params(1),
    )(jj, ii, k, q_t, v_t)


def _attn_bwd_flat(k, v, q_t, k_t, do_t, lse, dsum, tq):
    T = k.shape[0]
    nq = T // tq
    jj, ii, n = _tile_tables(nq, False)
    heads = [slice(hh * HEAD_LANES, (hh + 1) * HEAD_LANES) for hh in range(2)]

    def body(jj_ref, ii_ref, k_ref, v_ref, qt_ref, kt_ref, dot_ref, lse_ref, dsum_ref, dqt_ref, dk_ref, dv_ref,
             ba_ref, bb_ref, dkt_ref, dvt_ref, bias_ref):
        _causal_bias(bias_ref, tq)
        dqt_ref[...] = jnp.zeros_like(dqt_ref)
        dkt_ref[...] = jnp.zeros_like(dkt_ref)
        dvt_ref[...] = jnp.zeros_like(dvt_ref)

        def products(t, buf):
            j, i = jj_ref[t], ii_ref[t]
            rows = pl.ds(pl.multiple_of(j * tq, tq), tq)
            for hh, sl in enumerate(heads):
                buf[hh] = _dot(k_ref[rows, sl], qt_ref[i, sl, :])
                buf[2 + hh] = _dot(v_ref[rows, :], dot_ref[i, sl, :])

        def tile(t, buf):
            j, i = jj_ref[t], ii_ref[t]
            bias = bias_ref.at[(j == i).astype(jnp.int32)]
            st = j % 2
            dv_new = None
            for hh, sl in enumerate(heads):
                p = jnp.exp2(buf[hh] * SCALE2 + bias[...] - lse_ref[hh, i])
                ds = (p * (buf[2 + hh] - dsum_ref[hh, i]) * SCALE).astype(BF16)
                dv_h = _dot_nt(dot_ref[i, sl, :], p.astype(BF16))
                dv_new = dv_h if dv_new is None else dv_new + dv_h
                dkt_ref[st, sl, :] += _dot_nt(qt_ref[i, sl, :], ds)
                dqt_ref[i, sl, :] += _dot(kt_ref[j, sl, :], ds)
            dvt_ref[st] += dv_new

        def flush(t):
            j, i = jj_ref[t], ii_ref[t]

            @pl.when(j == i)
            def _():
                st = j % 2
                rows = pl.ds(pl.multiple_of(j * tq, tq), tq)
                dk_ref[rows, :] = dkt_ref[st].T
                dv_ref[rows, :] = dvt_ref[st].T
                dkt_ref[st] = jnp.zeros(dkt_ref.shape[1:], F32)
                dvt_ref[st] = jnp.zeros(dvt_ref.shape[1:], F32)

        _walk_tiles(n, products, tile, flush, ba_ref, bb_ref)

    smem = pl.BlockSpec(memory_space=pltpu.SMEM)
    stat = pl.BlockSpec((2, nq, 1, tq), lambda p: (p, 0, 0, 0))
    blocks_t = pl.BlockSpec((nq, 2 * HEAD_LANES, tq), lambda p: (0, p, 0))
    return pl.pallas_call(
        body, name="attn_bwd", grid=(N_HEADS // 2,),
        in_specs=[smem, smem,
                  pl.BlockSpec((T, 2 * HEAD_LANES), lambda p: (0, p)),
                  pl.BlockSpec((T, LANES), lambda p: (0, p)),
                  blocks_t, blocks_t, blocks_t, stat, stat],
        out_specs=[blocks_t,
                   pl.BlockSpec((T, 2 * HEAD_LANES), lambda p: (0, p)),
                   pl.BlockSpec((T, LANES), lambda p: (0, p))],
        out_shape=[jax.ShapeDtypeStruct((nq, N_HEADS * HEAD_LANES, tq), F32),
                   jax.ShapeDtypeStruct((T, N_HEADS * HEAD_LANES), F32),
                   jax.ShapeDtypeStruct((T, MLA_WIDTH), F32)],
        scratch_shapes=[pltpu.VMEM((4, tq, tq), F32), pltpu.VMEM((4, tq, tq), F32),
                        pltpu.VMEM((2, 2 * HEAD_LANES, tq), F32), pltpu.VMEM((2, LANES, tq), F32),
                        pltpu.VMEM((2, tq, tq), F32)],
        compiler_params=_params(1),
    )(jj, ii, k, v, q_t, k_t, do_t, lse, dsum)


def _lower_bound(lbp):
    a, b = lbp[0:1, :], lbp[1:2, :]
    mx = jnp.maximum(a, b)
    ea, eb = jnp.exp(a - mx), jnp.exp(b - mx)
    return ea / (ea + eb)


def _tri(lower):
    r = lax.broadcasted_iota(jnp.int32, (CHUNK, CHUNK), 0)
    c = lax.broadcasted_iota(jnp.int32, (CHUNK, CHUNK), 1)
    return (c <= r) if lower else (c >= r)


def _hg_gates(hq, hf, lb):
    sq = _sigmoid(hq)
    sf = _sigmoid(hf)
    f = lb + (1.0 - lb) * sf
    g = jnp.log(f)
    gcum = jnp.dot(_tri(True).astype(F32), g, precision=lax.Precision.HIGHEST, preferred_element_type=F32)
    return sq, sf, f, hq * sq, 1.0 - f, gcum


def _head(x, hd):
    return x[:, hd * HG_DIM:(hd + 1) * HG_DIM]


def _all_heads(fn):
    return jnp.concatenate([fn(hd) for hd in range(HG_HEADS)], axis=1)


def _hg_blocks(q, kk, gcum):
    rowi = lax.broadcasted_iota(jnp.int32, gcum.shape, 0)
    out = []
    for blk in range(CHUNK // SUB):
        lo, hi = blk * SUB, (blk + 1) * SUB
        gb = gcum[lo - 1:lo, :] if blk else jnp.zeros_like(gcum[0:1, :])
        eq = jnp.exp(gcum[lo:hi, :] - gb)
        ek = jnp.exp(jnp.where(rowi < hi, gb - gcum, 0.0))
        out.append((eq, ek, (q[lo:hi, :] * eq).astype(BF16), (kk * ek).astype(BF16)))
    return out


def _hg_scores(blocks):
    out = []
    for hd in range(HG_HEADS):
        a = jnp.concatenate([_dot_nt(_head(qb, hd), _head(kb, hd)) for _, _, qb, kb in blocks], axis=0)
        out.append(jnp.where(_tri(True), a, 0.0))
    return out


HG_STEP_CHUNKS = 4


def _hgrn_fwd(proj, lbp):
    T = proj.shape[0]
    nc = T // CHUNK
    ns = min(HG_STEP_CHUNKS, nc)
    rows = ns * CHUNK

    def body(hq_ref, hf_ref, hi_ref, lbp_ref, o_ref, st_ref, state):
        @pl.when(pl.program_id(0) == 0)
        def _():
            state[...] = jnp.zeros_like(state)

        lb = _lower_bound(lbp_ref[...])
        work = []
        for c in range(ns):
            r = slice(c * CHUNK, (c + 1) * CHUNK)
            _, _, _, q, kk, gcum = _hg_gates(hq_ref[r, :], hf_ref[r, :], lb)
            vb = hi_ref[r, :].astype(BF16)
            a = _hg_scores(_hg_blocks(q, kk, gcum))
            gend = gcum[CHUNK - 1:CHUNK, :]
            qgb = (q * jnp.exp(gcum)).astype(BF16)
            kgeb = (kk * jnp.exp(gend - gcum)).astype(BF16)
            intra = [_dot(a[hd].astype(BF16), _head(vb, hd)) for hd in range(HG_HEADS)]
            update = [_dot_tn(_head(vb, hd), _head(kgeb, hd)) for hd in range(HG_HEADS)]
            work.append((qgb, jnp.exp(gend), intra, update))
        for hd in range(HG_HEADS):
            st = state[hd]
            for c, (qgb, egend, intra, update) in enumerate(work):
                st_ref[c, hd] = st
                o_ref[c * CHUNK:(c + 1) * CHUNK, hd * HG_DIM:(hd + 1) * HG_DIM] = (
                    intra[hd] + _dot_nt(_head(qgb, hd), st.astype(BF16)))
                st = st * _head(egend, hd) + update[hd]
            state[hd] = st

    def col(cb):
        return pl.BlockSpec((rows, HG_WIDTH), lambda i: (i, cb))

    return pl.pallas_call(
        body, name="hgrn_fwd", grid=(nc // ns,),
        in_specs=[col(P_HQ // HG_WIDTH), col(P_HF // HG_WIDTH), col(P_HI // HG_WIDTH), _full((2, HG_WIDTH))],
        out_specs=[pl.BlockSpec((rows, HG_WIDTH), lambda i: (i, 0)),
                   pl.BlockSpec((ns, HG_HEADS, HG_DIM, HG_DIM), lambda i: (i, 0, 0, 0))],
        out_shape=[jax.ShapeDtypeStruct((T, HG_WIDTH), F32),
                   jax.ShapeDtypeStruct((nc, HG_HEADS, HG_DIM, HG_DIM), F32)],
        scratch_shapes=[pltpu.VMEM((HG_HEADS, HG_DIM, HG_DIM), F32)],
        compiler_params=_params(1),
    )(proj, proj, proj, lbp)


def _hgrn_bwd(proj, lbp, do_hg, states):
    T = proj.shape[0]
    nc = T // CHUNK
    ns = min(HG_STEP_CHUNKS, nc)
    rows = ns * CHUNK
    steps = nc // ns

    def body(hq_ref, hf_ref, hi_ref, lbp_ref, do_ref, st_ref, dhq_ref, dhf_ref, dhi_ref, dlb_ref, dstate):
        @pl.when(pl.program_id(0) == 0)
        def _():
            dstate[...] = jnp.zeros_like(dstate)
            dlb_ref[...] = jnp.zeros_like(dlb_ref)

        lb = _lower_bound(lbp_ref[...])

        dst_all = [dstate[hd] for hd in range(HG_HEADS)]
        dlb = jnp.zeros_like(lb)
        last = lax.broadcasted_iota(jnp.int32, (CHUNK, HG_WIDTH), 0) == CHUNK - 1
        for c in reversed(range(ns)):
            r = slice(c * CHUNK, (c + 1) * CHUNK)
            hq = hq_ref[r, :]
            sq, sf, f, q, kk, gcum = _hg_gates(hq, hf_ref[r, :], lb)
            vb = hi_ref[r, :].astype(BF16)
            dob = do_ref[r, :].astype(BF16)
            blocks = _hg_blocks(q, kk, gcum)
            a = _hg_scores(blocks)
            gend = gcum[CHUNK - 1:CHUNK, :]
            eg, egend, ekend = jnp.exp(gcum), jnp.exp(gend), jnp.exp(gend - gcum)
            qg, kge = q * eg, kk * ekend
            qgb, kgeb = qg.astype(BF16), kge.astype(BF16)

            dv, dqg, dkge, st_dst, dq_blk, dk_blk = [], [], [], [], [], []
            for hd in range(HG_HEADS):
                st = st_ref[c, hd]
                dst = dst_all[hd]
                dstb = dst.astype(BF16)
                do_h, v_h = _head(dob, hd), _head(vb, hd)
                dv.append(_dot_tn(a[hd].astype(BF16), do_h) + _dot_nt(_head(kgeb, hd), dstb))
                da = jnp.where(_tri(True), _dot_nt(do_h, v_h), 0.0).astype(BF16)
                dqg.append(_dot(do_h, st.astype(BF16)))
                dkge.append(_dot(v_h, dstb))
                st_dst.append(jnp.sum(st * dst, axis=0, keepdims=True))
                dst_all[hd] = _dot_tn(do_h, _head(qgb, hd)) + dst * _head(egend, hd)
                dq_blk.append([_dot(da[b * SUB:(b + 1) * SUB, :], _head(kb, hd)) for b, (_, _, _, kb) in enumerate(blocks)])
                dk_blk.append([_dot_tn(da[b * SUB:(b + 1) * SUB, :], _head(qb, hd)) for b, (_, _, qb, _) in enumerate(blocks)])
            dv, dqg, dkge, st_dst = (jnp.concatenate(t, axis=1) for t in (dv, dqg, dkge, st_dst))

            dq_a, dg_q = [], []
            dk_a, dg_k = jnp.zeros_like(gcum), jnp.zeros_like(gcum)
            for b, (eq, ek, qb, kb) in enumerate(blocks):
                dq_b = _all_heads(lambda hd: dq_blk[hd][b])
                dk_b = _all_heads(lambda hd: dk_blk[hd][b])
                dq_a.append(dq_b * eq)
                dk_a = dk_a + dk_b * ek
                dg_q.append(qb.astype(F32) * dq_b)
                dg_k = dg_k + kb.astype(F32) * dk_b
            dq_a = jnp.concatenate(dq_a, axis=0)

            dgend = st_dst * egend + jnp.sum(dkge * kge, axis=0, keepdims=True)
            dq = dq_a + dqg * eg
            dk = dk_a + dkge * ekend
            dgc = jnp.concatenate(dg_q, axis=0) - dg_k + dqg * qg - dkge * kge + jnp.where(last, dgend, 0.0)
            dg = jnp.dot(_tri(False).astype(F32), dgc, precision=lax.Precision.HIGHEST, preferred_element_type=F32)
            df = dg / f - dk
            dhf_ref[r, :] = df * (1.0 - lb) * sf * (1.0 - sf)
            dlb = dlb + jnp.sum(df * (1.0 - sf), axis=0, keepdims=True)
            dhq_ref[r, :] = dq * (sq * (1.0 + hq * (1.0 - sq)))
            dhi_ref[r, :] = dv
        for hd in range(HG_HEADS):
            dstate[hd] = dst_all[hd]
        dlb_ref[...] += dlb

    def col(cb):
        return pl.BlockSpec((rows, HG_WIDTH), lambda i: (steps - 1 - i, cb))

    grad = jax.ShapeDtypeStruct((T, HG_WIDTH), F32)
    return pl.pallas_call(
        body, name="hgrn_bwd", grid=(steps,),
        in_specs=[col(P_HQ // HG_WIDTH), col(P_HF // HG_WIDTH), col(P_HI // HG_WIDTH), _full((2, HG_WIDTH)),
                  col(0), pl.BlockSpec((ns, HG_HEADS, HG_DIM, HG_DIM), lambda i: (steps - 1 - i, 0, 0, 0))],
        out_specs=[col(0), col(0), col(0), _full((1, HG_WIDTH))],
        out_shape=[grad, grad, grad, jax.ShapeDtypeStruct((1, HG_WIDTH), F32)],
        scratch_shapes=[pltpu.VMEM((HG_HEADS, HG_DIM, HG_DIM), F32)],
        compiler_params=_params(1),
    )(proj, proj, proj, lbp, do_hg, states)


def _top(x, tgt, o_mla, o_hg, proj, w_out, hg_norm_g, final_g, tm):
    T = x.shape[0]

    def body(x_ref, tgt_ref, om_ref, oh_ref, gm_ref, gh_ref, wout_ref, hgn_ref, fng_ref,
             dx2_ref, dx2b_ref, ycat_ref, dom_ref, dsum_ref, dgm_ref, doh_ref, dgh_ref, loss_ref, dfng_ref, dhgn_ref):
        @pl.when(pl.program_id(0) == 0)
        def _():
            loss_ref[...] = jnp.zeros_like(loss_ref)
            dfng_ref[...] = jnp.zeros_like(dfng_ref)
            dhgn_ref[...] = jnp.zeros_like(dhgn_ref)

        gm, om = gm_ref[...], om_ref[...]
        sgm = _sigmoid(gm)
        silu_m = gm * sgm
        gh, oh, gam = gh_ref[...], oh_ref[...], hgn_ref[...]
        sgh = _sigmoid(gh)
        silu_h = gh * sgh
        rr, nn = [], []
        for hd in range(HG_HEADS):
            oh_h = oh[:, hd * HG_DIM:(hd + 1) * HG_DIM]
            r_h = lax.rsqrt(jnp.mean(oh_h * oh_h, axis=-1, keepdims=True) + EPS)
            rr.append(r_h)
            nn.append(oh_h * r_h)
        n = jnp.concatenate(nn, axis=1)
        ng = n * gam
        ycat_ref[:, :MLA_WIDTH] = (om * silu_m).astype(BF16)
        ycat_ref[:, MLA_WIDTH:] = (ng * silu_h).astype(BF16)
        wout = wout_ref[...]
        x2 = x_ref[...] + _dot(ycat_ref[...], wout)
        r = lax.rsqrt(jnp.mean(x2 * x2, axis=-1, keepdims=True) + EPS)
        xh = x2 * r
        fng = fng_ref[...]
        err = xh * fng - tgt_ref[...]
        loss_ref[...] += 0.5 * jnp.sum(jnp.mean(err * err, axis=-1, keepdims=True), axis=0, keepdims=True)
        dout = err * (1.0 / D_MODEL)
        dfng_ref[...] += jnp.sum(dout * xh, axis=0, keepdims=True)
        dxh = dout * fng
        dx2 = r * (dxh - xh * jnp.mean(dxh * xh, axis=-1, keepdims=True))
        dx2_ref[...] = dx2
        dx2b = dx2.astype(BF16)
        dx2b_ref[...] = dx2b
        dycat = _dot_nt(dx2b, wout)
        dym, dyh = dycat[:, :MLA_WIDTH], dycat[:, MLA_WIDTH:]
        dom = dym * silu_m
        first = lax.broadcasted_iota(jnp.int32, (tm, LANES), 1) < V_DIM
        for pp in range(N_HEADS // 2):
            pair = dom[:, pp * LANES:(pp + 1) * LANES]
            dom_ref[:, 2 * pp * HEAD_LANES:(2 * pp + 1) * HEAD_LANES] = jnp.where(first, pair, 0.0).astype(BF16)
            dom_ref[:, (2 * pp + 1) * HEAD_LANES:(2 * pp + 2) * HEAD_LANES] = jnp.where(first, 0.0, pair).astype(BF16)
        head_of = lax.broadcasted_iota(jnp.int32, (MLA_WIDTH, LANES), 0) // V_DIM
        pick = (head_of == lax.broadcasted_iota(jnp.int32, (MLA_WIDTH, LANES), 1)).astype(F32)
        dsum_ref[...] = jnp.dot(dom * om, pick, precision=lax.Precision.HIGHEST, preferred_element_type=F32)
        dgm_ref[...] = dym * om * (sgm * (1.0 + gm * (1.0 - sgm)))
        dgh_ref[...] = dyh * ng * (sgh * (1.0 + gh * (1.0 - sgh)))
        dng = dyh * silu_h
        dhgn_ref[...] += jnp.sum(dng * n, axis=0, keepdims=True)
        dn = dng * gam
        for hd in range(HG_HEADS):
            sl = slice(hd * HG_DIM, (hd + 1) * HG_DIM)
            dn_h, n_h = dn[:, sl], nn[hd]
            doh_ref[:, sl] = rr[hd] * (dn_h - n_h * jnp.mean(dn_h * n_h, axis=-1, keepdims=True))

    def row(w, cb=0):
        return pl.BlockSpec((tm, w), lambda i: (i, cb))

    outs = [(D_MODEL, F32), (D_MODEL, BF16), (D_MODEL, BF16), (N_HEADS * HEAD_LANES, BF16), (LANES, F32),
            (MLA_WIDTH, F32), (HG_WIDTH, F32), (HG_WIDTH, F32)]
    small = [(1, LANES), (1, D_MODEL), (1, HG_WIDTH)]
    return pl.pallas_call(
        body, name="top", grid=(T // tm,),
        in_specs=[row(D_MODEL), row(D_MODEL), row(MLA_WIDTH), row(HG_WIDTH),
                  row(MLA_WIDTH, P_GM // MLA_WIDTH), row(HG_WIDTH, P_GH // HG_WIDTH),
                  _full((D_MODEL, D_MODEL)), _full((1, HG_WIDTH)), _full((1, D_MODEL))],
        out_specs=[row(w) for w, _ in outs] + [_full(s) for s in small],
        out_shape=[jax.ShapeDtypeStruct((T, w), dt) for w, dt in outs] + [jax.ShapeDtypeStruct(s, F32) for s in small],
        compiler_params=_params(1),
    )(x, tgt, o_mla, o_hg, proj, proj, w_out, hg_norm_g, final_g)


def _bot(x, dx2, proj, qn, kvn, dq, dk, dv, dgm, dhq, dhf, dhi, dgh, c_t, s1_t, s2_t, w_in_p, w_q_p, w_kv_p, ln_g, q_g, kv_g, tm):
    T = x.shape[0]
    lat_w = D_PERM - P_QL

    def body(x_ref, dx2_ref, lat_ref, qn_ref, kvn_ref, dq_ref, dk_ref, dv_ref, dgm_ref, dhq_ref, dhf_ref, dhi_ref,
             dgh_ref, c_ref, s1_ref, s2_ref, win_ref, wq_ref, wkv_ref, lng_ref, qg_ref, kvg_ref,
             dx_ref, dproj_ref, dlng_ref, dqg_ref, dkvg_ref, dwq_ref, dwkv_ref, dqpre_ref, dkv_ref):
        @pl.when(pl.program_id(0) == 0)
        def _():
            for ref in (dlng_ref, dqg_ref, dkvg_ref, dwq_ref, dwkv_ref):
                ref[...] = jnp.zeros_like(ref)

        c, s1, s2 = c_ref[...], s1_ref[...], s2_ref[...]
        dkpe = jnp.zeros((tm, LANES), F32)
        for hd in range(N_HEADS):
            sl = slice(hd * HEAD_LANES, (hd + 1) * HEAD_LANES)
            dqpre_ref[:, sl] = _rope_bwd(dq_ref[:, sl], c, s1, s2).astype(BF16)
            dk_h = dk_ref[:, sl]
            dkpe = dkpe + dk_h
            dkv_ref[:, sl] = dk_h.astype(BF16)
        dkv_ref[:, N_HEADS * HEAD_LANES:] = dv_ref[...].astype(BF16)
        lane = lax.broadcasted_iota(jnp.int32, (tm, LANES), 1)
        rope_lanes = jnp.logical_and(lane >= ROPE_LO, lane < ROPE_LO + ROPE)
        dkr = jnp.where(rope_lanes, _rope_bwd(dkpe, c, s1, s2), 0.0)

        def norm_bwd(v, g, dy):
            r = lax.rsqrt(jnp.mean(v * v, axis=-1, keepdims=True) + EPS)
            vh = v * r
            dvh = dy * g
            return jnp.sum(dy * vh, axis=0, keepdims=True), r * (dvh - vh * jnp.mean(dvh * vh, axis=-1, keepdims=True))

        dwq_ref[...] += _dot_tn(qn_ref[...], dqpre_ref[...])
        dwkv_ref[...] += _dot_tn(kvn_ref[...], dkv_ref[...])
        dqn = _dot_nt(dqpre_ref[...], wq_ref[...])
        dg_q, dql = norm_bwd(lat_ref[:, :Q_RANK], qg_ref[...], dqn)
        dqg_ref[...] += dg_q
        dkn = _dot_nt(dkv_ref[...], wkv_ref[...])
        dg_kv, dkvl = norm_bwd(lat_ref[:, Q_RANK:Q_RANK + KV_RANK], kvg_ref[...], dkn)
        dkvg_ref[...] += dg_kv

        dproj_ref[:, P_GM:P_GM + MLA_WIDTH] = dgm_ref[...].astype(BF16)
        dproj_ref[:, P_HQ:P_HQ + HG_WIDTH] = dhq_ref[...].astype(BF16)
        dproj_ref[:, P_HF:P_HF + HG_WIDTH] = dhf_ref[...].astype(BF16)
        dproj_ref[:, P_HI:P_HI + HG_WIDTH] = dhi_ref[...].astype(BF16)
        dproj_ref[:, P_GH:P_GH + HG_WIDTH] = dgh_ref[...].astype(BF16)
        dproj_ref[:, P_QL:P_QL + Q_RANK] = dql.astype(BF16)
        dproj_ref[:, P_KVL:P_KVL + KV_RANK] = dkvl.astype(BF16)
        dproj_ref[:, P_KR:P_KR + LANES] = dkr.astype(BF16)
        dh = _dot_nt(dproj_ref[...], win_ref[...])
        dg_ln, dxn = norm_bwd(x_ref[...], lng_ref[...], dh)
        dlng_ref[...] += dg_ln
        dx_ref[...] = dx2_ref[...] + dxn

    def row(w, cb=0):
        return pl.BlockSpec((tm, w), lambda i: (i, cb))

    hl = N_HEADS * HEAD_LANES
    outs = [(D_MODEL, F32), (D_PERM, BF16)]
    small = [(1, D_MODEL), (1, Q_RANK), (1, KV_RANK), (Q_RANK, hl), (KV_RANK, hl + MLA_WIDTH)]
    return pl.pallas_call(
        body, name="bot", grid=(T // tm,),
        in_specs=[row(D_MODEL), row(D_MODEL), row(lat_w, P_QL // lat_w), row(Q_RANK), row(KV_RANK),
                  row(hl), row(hl), row(MLA_WIDTH),
                  row(MLA_WIDTH), row(HG_WIDTH), row(HG_WIDTH), row(HG_WIDTH), row(HG_WIDTH),
                  row(LANES), row(LANES), row(LANES),
                  _full((D_MODEL, D_PERM)), _full((Q_RANK, hl)), _full((KV_RANK, hl + MLA_WIDTH)),
                  _full((1, D_MODEL)), _full((1, Q_RANK)), _full((1, KV_RANK))],
        out_specs=[row(w) for w, _ in outs] + [_full(s) for s in small],
        out_shape=[jax.ShapeDtypeStruct((T, w), dt) for w, dt in outs] + [jax.ShapeDtypeStruct(s, F32) for s in small],
        scratch_shapes=[pltpu.VMEM((tm, hl), BF16), pltpu.VMEM((tm, hl + MLA_WIDTH), BF16)],
        compiler_params=_params(1),
    )(x, dx2, proj, qn, kvn, dq, dk, dv, dgm, dhq, dhf, dhi, dgh, c_t, s1_t, s2_t, w_in_p, w_q_p, w_kv_p, ln_g, q_g,
      kv_g)


def _matmul_tn(a, b, bn, bt, name):
    T, M = a.shape
    N = b.shape[1]

    def body(a_ref, b_ref, o_ref):
        @pl.when(pl.program_id(1) == 0)
        def _():
            o_ref[...] = jnp.zeros_like(o_ref)

        o_ref[...] += _dot_tn(a_ref[...], b_ref[...])

    return pl.pallas_call(
        body, name=name, grid=(N // bn, T // bt),
        in_specs=[pl.BlockSpec((bt, M), lambda n, t: (t, 0)), pl.BlockSpec((bt, bn), lambda n, t: (t, n))],
        out_specs=pl.BlockSpec((M, bn), lambda n, t: (0, n)),
        out_shape=jax.ShapeDtypeStruct((M, N), F32),
        compiler_params=_params(2),
    )(a, b)


RS_ROWS = 256


def _reduce_scatter(slabs, small):
    n = len(slabs)
    units = [(a, r0, min(s.shape[1], RS_ROWS)) for a, s in enumerate(slabs) for r0 in range(0, s.shape[1], RS_ROWS)]
    nu = len(units)

    def body(*refs):
        ins, small_ref = refs[:n], refs[n]
        outs, small_out = refs[n + 1:2 * n + 1], refs[2 * n + 1]
        own, sib_land, ici_out, ici_land = (refs[(2 + g) * n + 2:(3 + g) * n + 2] for g in range(4))
        small_land = refs[6 * n + 2]
        loc_sems, d2d_send, d2d_recv, ici_send, ici_recv, sm_send, sm_recv = refs[6 * n + 3:6 * n + 10]
        x, y, c = lax.axis_index("x"), lax.axis_index("y"), lax.axis_index("c")
        me = 4 * x + 2 * y + c

        def chip(k):
            return (1 - x if k & 2 else x, 1 - y if k & 1 else y)

        def block(k, core):
            px, py = chip(k)
            return 4 * px + 2 * py + core

        def part(u):
            a, r0, nr = units[u]
            return a, pl.ds(r0, nr)

        def local(u, k):
            a, rows = part(u)
            return pltpu.make_async_copy(ins[a].at[block(k, c), rows, :], own[a].at[k, rows, :], loc_sems.at[u, k])

        def to_sibling(u, k):
            a, rows = part(u)
            return pltpu.make_async_remote_copy(
                src_ref=ins[a].at[block(k, 1 - c), rows, :], dst_ref=sib_land[a].at[k, rows, :],
                send_sem=d2d_send.at[u, k], recv_sem=d2d_recv.at[u, k], device_id=(x, y, 1 - c), device_id_type=MESH)

        def to_chip(u, k):
            a, rows = part(u)
            return pltpu.make_async_remote_copy(
                src_ref=ici_out[a].at[k - 1, rows, :], dst_ref=ici_land[a].at[k - 1, rows, :],
                send_sem=ici_send.at[u, k - 1], recv_sem=ici_recv.at[u, k - 1], device_id=(*chip(k), c),
                device_id_type=MESH)

        def small_copy(k, receiving):
            px, py = chip(k >> 1)
            pc = 1 - c if k & 1 else c
            slot = 4 * px + 2 * py + pc if receiving else me
            return pltpu.make_async_remote_copy(
                src_ref=small_ref, dst_ref=small_land.at[slot], send_sem=sm_send.at[k - 1], recv_sem=sm_recv.at[k - 1],
                device_id=(px, py, pc), device_id_type=MESH)

        for u in range(nu):
            for k in range(4):
                local(u, k).start()
        for u in range(nu):
            for k in range(4):
                to_sibling(u, k).start()
        small_land[me] = small_ref[...]
        for k in range(1, N_DEV):
            small_copy(k, False).start()
        for u in range(nu):
            a, rows = part(u)
            for k in range(4):
                local(u, k).wait()
                to_sibling(u, k).wait_recv()
            for k in range(1, 4):
                ici_out[a][k - 1, rows, :] = (own[a][k, rows, :] + sib_land[a][k, rows, :]).astype(BF16)
                to_chip(u, k).start()
        for u in range(nu):
            a, rows = part(u)
            acc = own[a][0, rows, :] + sib_land[a][0, rows, :]
            for k in range(1, 4):
                to_chip(u, k).wait_recv()
                acc = acc + ici_land[a][k - 1, rows, :].astype(F32)
            outs[a][rows, :] = acc
        for k in range(1, N_DEV):
            small_copy(k, True).wait_recv()
        acc = small_land[0]
        for d in range(1, N_DEV):
            acc = acc + small_land[d]
        small_out[...] = acc
        for u in range(nu):
            for k in range(4):
                to_sibling(u, k).wait_send()
            for k in range(1, 4):
                to_chip(u, k).wait_send()
        for k in range(1, N_DEV):
            small_copy(k, False).wait_send()

    vm = pl.BlockSpec(memory_space=pltpu.VMEM)
    hbm = pl.BlockSpec(memory_space=pl.ANY)
    dma = pltpu.SemaphoreType.DMA
    return pl.pallas_call(
        body, name="reduce_scatter_grads",
        in_specs=[hbm] * n + [vm], out_specs=[vm] * (n + 1),
        out_shape=[jax.ShapeDtypeStruct(s.shape[1:], F32) for s in slabs] + [jax.ShapeDtypeStruct(small.shape, F32)],
        scratch_shapes=[pltpu.VMEM((4,) + s.shape[1:], F32) for s in slabs] * 2
        + [pltpu.VMEM((3,) + s.shape[1:], BF16) for s in slabs] * 2
        + [pltpu.VMEM((N_DEV,) + small.shape, F32)]
        + [dma((nu, 4)), dma((nu, 4)), dma((nu, 4)), dma((nu, 3)), dma((nu, 3)), dma((N_DEV - 1,)), dma((N_DEV - 1,))],
        compiler_params=pltpu.CompilerParams(vmem_limit_bytes=VMEM_LIMIT),
    )(*slabs, small)


def _adamw_math(w, g, m, v):
    m = ADAM_B1 * m + (1.0 - ADAM_B1) * g
    v = ADAM_B2 * v + (1.0 - ADAM_B2) * (g * g)
    m_hat = m / (1.0 - ADAM_B1 ** ADAM_STEP)
    v_hat = v / (1.0 - ADAM_B2 ** ADAM_STEP)
    delta = -ADAM_LR * (m_hat / (jnp.sqrt(v_hat) + ADAM_EPS) + ADAM_WD * w)
    return delta, m, v


SMALL_W = 512


def _adamw(big, small_w, small_g):
    nb, ns = len(big), len(small_w)

    def body(*refs):
        k = 0
        big_in = [refs[4 * i:4 * i + 4] for i in range(nb)]
        k = 4 * nb
        small_in = [refs[k + 3 * i:k + 3 * i + 3] for i in range(ns)]
        k += 3 * ns
        sg_ref = refs[k]
        k += 1
        big_out = [refs[k + 3 * i:k + 3 * i + 3] for i in range(nb)]
        k += 3 * nb
        small_out = [refs[k + 4 * i:k + 4 * i + 4] for i in range(ns)]

        for (w, g, m, v), (od, om, ov) in zip(big_in, big_out):
            od[...], om[...], ov[...] = _adamw_math(w[...], g[...], m[...], v[...])

        sg = sg_ref[...]
        lbp = small_in[2][0][...]
        lb = _lower_bound(lbp)
        t = sg[4:5, :] * lb * (1.0 - lb)
        grads = [jnp.concatenate([sg[0:1, :], sg[1:2, :]], axis=1),
                 jnp.concatenate([sg[2:3, :], sg[3:4, :]], axis=1),
                 jnp.concatenate([t, -t], axis=0),
                 sg[6:7, :], sg[7:8, 0:Q_RANK], sg[7:8, Q_RANK:Q_RANK + KV_RANK]]
        for (w, m, v), g, (og, od, om, ov) in zip(small_in, grads, small_out):
            og[...] = g
            od[...], om[...], ov[...] = _adamw_math(w[...], g, m[...], v[...])

    ins = [a for grp in big for a in grp] + [a for grp in small_w for a in grp] + [small_g]
    out_shape = ([jax.ShapeDtypeStruct(grp[0].shape, F32) for grp in big for _ in range(3)]
                 + [jax.ShapeDtypeStruct(grp[0].shape, F32) for grp in small_w for _ in range(4)])
    vm = pl.BlockSpec(memory_space=pltpu.VMEM)
    res = pl.pallas_call(
        body, name="adamw", in_specs=[vm] * len(ins), out_specs=[vm] * len(out_shape), out_shape=out_shape,
        compiler_params=pltpu.CompilerParams(vmem_limit_bytes=VMEM_LIMIT),
    )(*ins)
    big_res = [res[3 * i:3 * i + 3] for i in range(nb)]
    small_res = [res[3 * nb + 4 * i:3 * nb + 4 * i + 4] for i in range(ns)]
    return big_res, small_res


def _perm_weights(g_in, g_q, g_kv, g_out):
    w = g_in.transpose(1, 0, 2).reshape(D_MODEL, D_IN)
    z = lambda n: jnp.zeros((D_MODEL, n), BF16)
    w_in_p = jnp.concatenate([w[:, 416:], w[:, :384], z(64), w[:, 384:416], z(32)], axis=1)
    wq = g_q.transpose(1, 0, 2)
    w_q_p = jnp.pad(wq, ((0, 0), (0, 0), (0, HEAD_LANES - NOPE - ROPE))).reshape(Q_RANK, N_HEADS * HEAD_LANES)
    wkv = g_kv.transpose(1, 0, 2)
    wk = jnp.pad(wkv[:, :, :NOPE], ((0, 0), (0, 0), (0, HEAD_LANES - NOPE))).reshape(KV_RANK, N_HEADS * HEAD_LANES)
    wv = wkv[:, :, NOPE:].reshape(KV_RANK, MLA_WIDTH)
    return w_in_p, w_q_p, jnp.concatenate([wk, wv], axis=1), g_out.reshape(D_MODEL, D_MODEL)


def _grad_slabs(dw_in_p, dw_q_p, dw_kv_p, dw_out):
    dw_in = jnp.concatenate([dw_in_p[:, P_QL:P_KR], dw_in_p[:, P_KR + ROPE_LO:P_KR + ROPE_LO + ROPE], dw_in_p[:, :P_QL]], axis=1)
    s_in = dw_in.reshape(D_MODEL, N_DEV, D_IN // N_DEV).transpose(1, 0, 2)
    s_q = dw_q_p.reshape(Q_RANK, N_HEADS, HEAD_LANES)[:, :, :NOPE + ROPE].transpose(1, 0, 2)
    hl = N_HEADS * HEAD_LANES
    dk = dw_kv_p[:, :hl].reshape(KV_RANK, N_HEADS, HEAD_LANES)[:, :, :NOPE]
    dv = dw_kv_p[:, hl:].reshape(KV_RANK, N_HEADS, V_DIM)
    s_kv = jnp.concatenate([dk, dv], axis=2).transpose(1, 0, 2)
    return s_in, s_q, s_kv, dw_out.reshape(N_DEV, D_MODEL // N_DEV, D_MODEL)


def _block_sizes(T):
    return min(256, T), min(256, T), min(512, T)


def kernel(x, positions, ln_g, w_in, q_a_norm_g, w_q_b, kv_a_norm_g, w_kv_b, hg_lower_bounds, hg_norm_g, w_out, final_norm_g, loss_target, m_ln_g, m_w_in, m_q_a_norm_g, m_w_q_b, m_kv_a_norm_g, m_w_kv_b, m_hg_lower_bounds, m_hg_norm_g, m_w_out, m_final_norm_g, v_ln_g, v_w_in, v_q_a_norm_g, v_w_q_b, v_kv_a_norm_g, v_w_kv_b, v_hg_lower_bounds, v_hg_norm_g, v_w_out, v_final_norm_g):
    T = x.shape[1]
    tm, tq, bt = _block_sizes(T)
    nq = T // tq
    xs, tgt = x[0], loss_target[0]
    pos_f = positions.astype(F32)
    fng = final_norm_g.reshape(1, D_MODEL)

    gathered = _all_gather_weights([w_in[0], w_q_b[0], w_kv_b[0], w_out[0]])
    w_in_p, w_q_p, w_kv_p, w_out_b = _perm_weights(*gathered)

    c_t, s1_t, s2_t = _rope_tables(pos_f, bt)
    proj, h, qn, kvn, q, k, v = _fwd_in(xs, ln_g, w_in_p, q_a_norm_g, w_q_p, kv_a_norm_g, w_kv_p, c_t, s1_t, s2_t, tm)
    hl = N_HEADS * HEAD_LANES
    v_t = v.reshape(nq, tq, MLA_WIDTH).transpose(0, 2, 1)
    k_t = k.reshape(nq, tq, hl).transpose(0, 2, 1)
    q_t = q.reshape(nq, tq, hl).transpose(0, 2, 1)
    o_mla, lse = _attn_fwd_flat(k, q_t, v_t, tq)
    o_hg, states = _hgrn_fwd(proj, hg_lower_bounds)
    dx2, dx2b, ycat, d_om, dsum, d_gm, d_oh, d_gh, loss_p, d_fng, d_hgn = _top(
        xs, tgt, o_mla, o_hg, proj, w_out_b, hg_norm_g, fng, tm)
    dsum = dsum[:, :N_HEADS].T.reshape(N_HEADS, nq, 1, tq)
    do_t = d_om.reshape(nq, tq, hl).transpose(0, 2, 1)
    dq_t, dk, dv = _attn_bwd_flat(k, v, q_t, k_t, do_t, lse, dsum, tq)
    dq = dq_t.transpose(0, 2, 1).reshape(T, N_HEADS * HEAD_LANES)
    d_hq, d_hf, d_hi, d_lb = _hgrn_bwd(proj, hg_lower_bounds, d_oh, states)
    dx, dproj, d_lng, d_qg, d_kvg, dw_q_p, dw_kv_p = _bot(
        xs, dx2, proj, qn, kvn, dq, dk, dv, d_gm, d_hq, d_hf, d_hi, d_gh, c_t, s1_t, s2_t, w_in_p, w_q_p, w_kv_p,
        ln_g, q_a_norm_g, kv_a_norm_g, tm)
    dw_in_p = _matmul_tn(h, dproj, 512, bt, "dw_in")
    dw_out = _matmul_tn(ycat, dx2b, 512, bt, "dw_out")

    zrow = jnp.zeros((1, SMALL_W), F32)
    small = jnp.concatenate([
        d_lng.reshape(2, SMALL_W), d_fng.reshape(2, SMALL_W), d_lb, zrow, d_hgn,
        jnp.concatenate([d_qg, d_kvg, jnp.zeros((1, SMALL_W - Q_RANK - KV_RANK), F32)], axis=1)], axis=0)
    g_in, g_q, g_kv, g_out, small_sum = _reduce_scatter(list(_grad_slabs(dw_in_p, dw_q_p, dw_kv_p, dw_out)), small)

    big = [(w_in[0], g_in, m_w_in[0], v_w_in[0]), (w_q_b[0], g_q, m_w_q_b[0], v_w_q_b[0]),
           (w_kv_b[0], g_kv, m_w_kv_b[0], v_w_kv_b[0]), (w_out[0], g_out, m_w_out[0], v_w_out[0])]
    small_w = [(ln_g, m_ln_g, v_ln_g),
               (fng, m_final_norm_g.reshape(1, D_MODEL), v_final_norm_g.reshape(1, D_MODEL)),
               (hg_lower_bounds, m_hg_lower_bounds, v_hg_lower_bounds), (hg_norm_g, m_hg_norm_g, v_hg_norm_g),
               (q_a_norm_g, m_q_a_norm_g, v_q_a_norm_g), (kv_a_norm_g, m_kv_a_norm_g, v_kv_a_norm_g)]
    big_res, small_res = _adamw(big, small_w, small_sum)

    loss = lax.psum(loss_p[0, 0], ("x", "y", "c"))
    (r_in, r_q, r_kv, r_out) = big_res
    (s_ln, s_fn, s_lb, s_hgn, s_qg, s_kvg) = small_res
    flat = lambda t: t.reshape(D_MODEL)
    lead = lambda t: t[None]
    grads = [s_ln[0], lead(g_in), s_qg[0], lead(g_q), s_kvg[0], lead(g_kv), s_lb[0], s_hgn[0], lead(g_out), flat(s_fn[0])]

    def pick(i):
        return [s_ln[i + 1], lead(r_in[i]), s_qg[i + 1], lead(r_q[i]), s_kvg[i + 1], lead(r_kv[i]), s_lb[i + 1],
                s_hgn[i + 1], lead(r_out[i]), flat(s_fn[i + 1])]

    return (loss, dx[None], *grads, *pick(0), *pick(1), *pick(2))
```

```python
import math

import numpy as np
import jax
import jax.numpy as jnp
from jax import lax
from jax.experimental import pallas as pl
from jax.experimental.pallas import tpu as pltpu

F32 = jnp.float32
BF16 = jnp.bfloat16

D_MODEL = 1024
N_HEADS = 8
NOPE = 64
ROPE = 32
HALF_ROPE = ROPE // 2
V_DIM = 64
Q_RANK = 256
KV_RANK = 128
MLA_WIDTH = N_HEADS * V_DIM
HG_HEADS = 4
HG_DIM = 128
HG_WIDTH = HG_HEADS * HG_DIM
CHUNK = 64
SUB = 16
D_IN = 2976
D_PERM = 3072
ROPE_THETA = 10000.0
EPS = 1e-6
N_DEV = 8
LANES = 128
HEAD_LANES = 128

P_GM, P_HQ, P_HF, P_HI, P_GH, P_QL, P_KVL, P_KR = 0, 512, 1024, 1536, 2048, 2560, 2816, 2944
ROPE_LO = NOPE
SCALE = 1.0 / math.sqrt(NOPE + ROPE)

ADAM_LR = 0.001
ADAM_B1 = 0.9
ADAM_B2 = 0.999
ADAM_EPS = 1e-08
ADAM_WD = 0.01
ADAM_STEP = 10

VMEM_LIMIT = 56 * 1024 * 1024
MESH = pl.DeviceIdType.MESH

NT = (((1,), (1,)), ((), ()))
TN = (((0,), (0,)), ((), ()))


def _params(n_grid=0, **kw):
    sem = ("arbitrary",) * n_grid if n_grid else None
    return pltpu.CompilerParams(dimension_semantics=sem, vmem_limit_bytes=VMEM_LIMIT, **kw)


def _dot(a, b):
    return jnp.dot(a, b, preferred_element_type=F32)


def _dot_nt(a, b):
    return lax.dot_general(a, b, NT, preferred_element_type=F32)


def _dot_tn(a, b):
    return lax.dot_general(a, b, TN, preferred_element_type=F32)


def _sigmoid(x):
    return 1.0 / (1.0 + jnp.exp(-x))


def _rope_fwd(x, c, s1, s2):
    return x * c + pltpu.roll(x, LANES - HALF_ROPE, 1) * s1 + pltpu.roll(x, HALF_ROPE, 1) * s2


def _rope_bwd(dy, c, s1, s2):
    return dy * c - pltpu.roll(dy, LANES - HALF_ROPE, 1) * s1 - pltpu.roll(dy, HALF_ROPE, 1) * s2


def _full(shape):
    n = len(shape)
    return pl.BlockSpec(shape, lambda *_: (0,) * n)


def _rope_tables(pos_f, tm):
    T = pos_f.shape[1]
    inv = (np.float32(ROPE_THETA) ** (-np.arange(HALF_ROPE, dtype=np.float32) / np.float32(HALF_ROPE))).astype(np.float32)
    place = np.zeros((3, HALF_ROPE, LANES), np.float32)
    for i in range(HALF_ROPE):
        place[0, i, ROPE_LO + i] = place[0, i, ROPE_LO + HALF_ROPE + i] = 1.0
        place[1, i, ROPE_LO + i] = -1.0
        place[2, i, ROPE_LO + HALF_ROPE + i] = 1.0
    base = np.ones((1, LANES), np.float32)
    base[0, ROPE_LO:ROPE_LO + ROPE] = 0.0

    def body(pos_ref, inv_ref, place_ref, base_ref, c_ref, s1_ref, s2_ref):
        ang = inv_ref[...] * pos_ref[...]
        cos, sin = jnp.cos(ang), jnp.sin(ang)

        def put(v, k):
            return lax.dot_general(v, place_ref[k], TN, precision=lax.Precision.HIGHEST, preferred_element_type=F32)

        c_ref[...] = put(cos, 0) + base_ref[...]
        s1_ref[...] = put(sin, 1)
        s2_ref[...] = put(sin, 2)

    tab = jax.ShapeDtypeStruct((T, LANES), F32)
    return pl.pallas_call(
        body, name="rope_tables", grid=(T // tm,),
        in_specs=[pl.BlockSpec((1, tm), lambda i: (0, i)), _full((HALF_ROPE, 1)), _full((3, HALF_ROPE, LANES)),
                  _full((1, LANES))],
        out_specs=[pl.BlockSpec((tm, LANES), lambda i: (i, 0))] * 3,
        out_shape=[tab, tab, tab], compiler_params=_params(1),
    )(pos_f, jnp.asarray(inv.reshape(HALF_ROPE, 1)), jnp.asarray(place), jnp.asarray(base))


def _all_gather_weights(shards):
    n = len(shards)

    def body(*refs):
        ins, outs = refs[:n], refs[n:2 * n]
        send_sems, recv_sems = refs[2 * n], refs[2 * n + 1]
        x, y, c = lax.axis_index("x"), lax.axis_index("y"), lax.axis_index("c")
        me, sibling = (x, y, c), (x, y, 1 - c)
        chips = [(1 - x, y), (x, 1 - y), (1 - x, 1 - y)]

        def idx(d):
            return 4 * d[0] + 2 * d[1] + d[2]

        def copy(a, k, block, to):
            rows = outs[a].at[idx(block)]
            return pltpu.make_async_remote_copy(src_ref=rows, dst_ref=rows, send_sem=send_sems.at[a, k],
                                                recv_sem=recv_sems.at[a, k], device_id=to, device_id_type=MESH)

        for a in range(n):
            outs[a][idx(me)] = ins[a][...].astype(BF16)
        first = []
        for a in range(n):
            first.append(copy(a, 0, me, sibling))
            first += [copy(a, 1 + j, me, (*chip, c)) for j, chip in enumerate(chips)]
        for cp in first:
            cp.start()
        passed = []
        for j, chip in enumerate(chips):
            for a in range(n):
                copy(a, 1 + j, (*chip, c), me).wait_recv()
                cp = copy(a, 4 + j, (*chip, c), sibling)
                cp.start()
                passed.append(cp)
        for a in range(n):
            copy(a, 0, sibling, me).wait_recv()
            for j, chip in enumerate(chips):
                copy(a, 4 + j, (*chip, 1 - c), me).wait_recv()
        for cp in first + passed:
            cp.wait_send()

    vm = pl.BlockSpec(memory_space=pltpu.VMEM)
    return pl.pallas_call(
        body, name="all_gather_weights",
        in_specs=[vm] * n, out_specs=[vm] * n,
        out_shape=[jax.ShapeDtypeStruct((N_DEV,) + s.shape, BF16) for s in shards],
        scratch_shapes=[pltpu.SemaphoreType.DMA((n, 7)), pltpu.SemaphoreType.DMA((n, 7))],
        compiler_params=pltpu.CompilerParams(vmem_limit_bytes=VMEM_LIMIT),
    )(*shards)


def _fwd_in(x, ln_g, w_in_p, q_g, w_q_p, kv_g, w_kv_p, c_t, s1_t, s2_t, tm):
    T = x.shape[0]

    def body(x_ref, lng_ref, win_ref, qg_ref, wq_ref, kvg_ref, wkv_ref, c_ref, s1_ref, s2_ref,
             proj_ref, h_ref, qn_ref, kvn_ref, q_ref, k_ref, v_ref):
        xv = x_ref[...]
        r = lax.rsqrt(jnp.mean(xv * xv, axis=-1, keepdims=True) + EPS)
        h = (xv * r * lng_ref[...]).astype(BF16)
        h_ref[...] = h
        proj = _dot(h, win_ref[...])
        proj_ref[...] = proj
        c, s1, s2 = c_ref[...], s1_ref[...], s2_ref[...]

        ql = proj[:, P_QL:P_QL + Q_RANK]
        rq = lax.rsqrt(jnp.mean(ql * ql, axis=-1, keepdims=True) + EPS)
        qn = (ql * rq * qg_ref[...]).astype(BF16)
        qn_ref[...] = qn
        q = _dot(qn, wq_ref[...])
        for hd in range(N_HEADS):
            sl = slice(hd * HEAD_LANES, (hd + 1) * HEAD_LANES)
            q_ref[:, sl] = _rope_fwd(q[:, sl], c, s1, s2).astype(BF16)

        kvl = proj[:, P_KVL:P_KVL + KV_RANK]
        rk = lax.rsqrt(jnp.mean(kvl * kvl, axis=-1, keepdims=True) + EPS)
        kvn = (kvl * rk * kvg_ref[...]).astype(BF16)
        kvn_ref[...] = kvn
        kv = _dot(kvn, wkv_ref[...])
        kpe = _rope_fwd(proj[:, P_KR:P_KR + LANES], c, s1, s2)
        for hd in range(N_HEADS):
            sl = slice(hd * HEAD_LANES, (hd + 1) * HEAD_LANES)
            k_ref[:, sl] = (kv[:, sl] + kpe).astype(BF16)
        v_ref[...] = kv[:, N_HEADS * HEAD_LANES:].astype(BF16)

    def row(w):
        return pl.BlockSpec((tm, w), lambda i: (i, 0))

    outs = [(D_PERM, F32), (D_MODEL, BF16), (Q_RANK, BF16), (KV_RANK, BF16),
            (N_HEADS * HEAD_LANES, BF16), (N_HEADS * HEAD_LANES, BF16), (MLA_WIDTH, BF16)]
    return pl.pallas_call(
        body, name="fwd_in", grid=(T // tm,),
        in_specs=[row(D_MODEL), _full((1, D_MODEL)), _full((D_MODEL, D_PERM)), _full((1, Q_RANK)),
                  _full((Q_RANK, N_HEADS * HEAD_LANES)), _full((1, KV_RANK)),
                  _full((KV_RANK, N_HEADS * HEAD_LANES + MLA_WIDTH)), row(LANES), row(LANES), row(LANES)],
        out_specs=[row(w) for w, _ in outs],
        out_shape=[jax.ShapeDtypeStruct((T, w), dt) for w, dt in outs],
        compiler_params=_params(1),
    )(x, ln_g, w_in_p, q_g, w_q_p, kv_g, w_kv_p, c_t, s1_t, s2_t)


LOG2E = 1.4426950408889634
SCALE2 = SCALE * LOG2E


def _causal(tq):
    r = lax.broadcasted_iota(jnp.int32, (tq, tq), 0)
    c = lax.broadcasted_iota(jnp.int32, (tq, tq), 1)
    return r <= c


MASKED = -1e30


def _causal_bias(bias_ref, tq):
    bias_ref[0] = jnp.zeros((tq, tq), F32)
    bias_ref[1] = jnp.where(_causal(tq), 0.0, MASKED)


def _tile_tables(nq, by_query):
    if by_query:
        pairs = [(j, i) for i in range(nq) for j in range(i + 1)]
    else:
        pairs = [(j, i) for j in range(nq) for i in range(nq - 1, j - 1, -1)]
    pairs.append(pairs[-1])
    jj, ii = np.array(pairs, np.int32).T
    return jnp.asarray(jj), jnp.asarray(ii), len(pairs) - 1


ATTN_TRIP = 4


def _walk_tiles(n, products, tile, flush, buf_a, buf_b):
    bufs = (buf_a, buf_b)
    products(0, buf_a)

    def trip(r, carry):
        for u in range(ATTN_TRIP):
            products(ATTN_TRIP * r + u + 1, bufs[(u + 1) % 2])
            tile(ATTN_TRIP * r + u, bufs[u % 2])
        for u in range(ATTN_TRIP):
            flush(ATTN_TRIP * r + u)
        return carry

    lax.fori_loop(0, n // ATTN_TRIP, trip, 0)
    rest = n - n % ATTN_TRIP
    for u in range(n % ATTN_TRIP):
        if rest + u + 1 < n:
            products(rest + u + 1, bufs[(u + 1) % 2])
        tile(rest + u, bufs[u % 2])
    for u in range(n % ATTN_TRIP):
        flush(rest + u)


def _attn_fwd_flat(k, q_t, v_t, tq):
    T = k.shape[0]
    nq = T // tq
    jj, ii, n = _tile_tables(nq, True)
    heads = [slice(hh * HEAD_LANES, (hh + 1) * HEAD_LANES) for hh in range(2)]

    def body(jj_ref, ii_ref, k_ref, qt_ref, vt_ref, o_ref, lse_ref, sa_ref, sb_ref, m_ref, l_ref, acc_ref, bias_ref):
        def reset(st):
            m_ref[st] = jnp.full(m_ref.shape[1:], MASKED, F32)
            l_ref[st] = jnp.zeros(l_ref.shape[1:], F32)
            acc_ref[st] = jnp.zeros(acc_ref.shape[1:], F32)

        _causal_bias(bias_ref, tq)
        for st in range(ATTN_TRIP):
            reset(st)

        def products(t, buf):
            j, i = jj_ref[t], ii_ref[t]
            kj = k_ref[pl.ds(pl.multiple_of(j * tq, tq), tq), :]
            for hh, sl in enumerate(heads):
                buf[hh] = _dot(kj[:, sl], qt_ref[i, sl, :])

        def tile(t, buf):
            j, i = jj_ref[t], ii_ref[t]
            vt = vt_ref[j]
            bias = bias_ref.at[(j == i).astype(jnp.int32)]
            st = i % ATTN_TRIP
            for hh in range(2):
                s = buf[hh] * SCALE2 + bias[...]
                m = m_ref[st, hh]
                m_new = jnp.maximum(m, jnp.max(s, axis=0, keepdims=True))
                alpha = jnp.exp2(m - m_new)
                p = jnp.exp2(s - m_new)
                m_ref[st, hh] = m_new
                l_ref[st, hh] = alpha * l_ref[st, hh] + jnp.sum(p, axis=0, keepdims=True)
                acc_ref[st, hh] = alpha * acc_ref[st, hh] + _dot(vt, p.astype(BF16))

        def flush(t):
            j, i = jj_ref[t], ii_ref[t]

            @pl.when(j == i)
            def _():
                st = i % ATTN_TRIP
                first = lax.broadcasted_iota(jnp.int32, (LANES, tq), 0) < V_DIM
                out = jnp.where(first, acc_ref[st, 0] / l_ref[st, 0], acc_ref[st, 1] / l_ref[st, 1])
                o_ref[pl.ds(pl.multiple_of(i * tq, tq), tq), :] = out.T
                for hh in range(2):
                    lse_ref[hh, i] = m_ref[st, hh] + jnp.log2(l_ref[st, hh])
                reset(st)

        _walk_tiles(n, products, tile, flush, sa_ref, sb_ref)

    smem = pl.BlockSpec(memory_space=pltpu.SMEM)
    return pl.pallas_call(
        body, name="attn_fwd", grid=(N_HEADS // 2,),
        in_specs=[smem, smem,
                  pl.BlockSpec((T, 2 * HEAD_LANES), lambda p: (0, p)),
                  pl.BlockSpec((nq, 2 * HEAD_LANES, tq), lambda p: (0, p, 0)),
                  pl.BlockSpec((nq, LANES, tq), lambda p: (0, p, 0))],
        out_specs=[pl.BlockSpec((T, LANES), lambda p: (0, p)),
                   pl.BlockSpec((2, nq, 1, tq), lambda p: (p, 0, 0, 0))],
        out_shape=[jax.ShapeDtypeStruct((T, MLA_WIDTH), F32), jax.ShapeDtypeStruct((N_HEADS, nq, 1, tq), F32)],
        scratch_shapes=[pltpu.VMEM((2, tq, tq), F32), pltpu.VMEM((2, tq, tq), F32),
                        pltpu.VMEM((ATTN_TRIP, 2, 1, tq), F32), pltpu.VMEM((ATTN_TRIP, 2, 1, tq), F32),
                        pltpu.VMEM((ATTN_TRIP, 2, LANES, tq), F32), pltpu.VMEM((2, tq, tq), F32)],
        compiler_params=_params(1),
    )(jj, ii, k, q_t, v_t)


def _attn_bwd_flat(k, v, q_t, k_t, do_t, lse, dsum, tq):
    T = k.shape[0]
    nq = T // tq
    jj, ii, n = _tile_tables(nq, False)
    heads = [slice(hh * HEAD_LANES, (hh + 1) * HEAD_LANES) for hh in range(2)]

    def body(jj_ref, ii_ref, k_ref, v_ref, qt_ref, kt_ref, dot_ref, lse_ref, dsum_ref, dqt_ref, dk_ref, dv_ref,
             ba_ref, bb_ref, dkt_ref, dvt_ref, bias_ref):
        _causal_bias(bias_ref, tq)
        dqt_ref[...] = jnp.zeros_like(dqt_ref)
        dkt_ref[...] = jnp.zeros_like(dkt_ref)
        dvt_ref[...] = jnp.zeros_like(dvt_ref)

        def products(t, buf):
            j, i = jj_ref[t], ii_ref[t]
            rows = pl.ds(pl.multiple_of(j * tq, tq), tq)
            for hh, sl in enumerate(heads):
                buf[hh] = _dot(k_ref[rows, sl], qt_ref[i, sl, :])
                buf[2 + hh] = _dot(v_ref[rows, :], dot_ref[i, sl, :])

        def tile(t, buf):
            j, i = jj_ref[t], ii_ref[t]
            bias = bias_ref.at[(j == i).astype(jnp.int32)]
            st = j % ATTN_TRIP
            dv_new = None
            for hh, sl in enumerate(heads):
                p = jnp.exp2(buf[hh] * SCALE2 + bias[...] - lse_ref[hh, i])
                ds = (p * (buf[2 + hh] - dsum_ref[hh, i]) * SCALE).astype(BF16)
                dv_h = _dot_nt(dot_ref[i, sl, :], p.astype(BF16))
                dv_new = dv_h if dv_new is None else dv_new + dv_h
                dkt_ref[st, sl, :] += _dot_nt(qt_ref[i, sl, :], ds)
                dqt_ref[i, sl, :] += _dot(kt_ref[j, sl, :], ds)
            dvt_ref[st] += dv_new

        def flush(t):
            j, i = jj_ref[t], ii_ref[t]

            @pl.when(j == i)
            def _():
                st = j % ATTN_TRIP
                rows = pl.ds(pl.multiple_of(j * tq, tq), tq)
                dk_ref[rows, :] = dkt_ref[st].T
                dv_ref[rows, :] = dvt_ref[st].T
                dkt_ref[st] = jnp.zeros(dkt_ref.shape[1:], F32)
                dvt_ref[st] = jnp.zeros(dvt_ref.shape[1:], F32)

        _walk_tiles(n, products, tile, flush, ba_ref, bb_ref)

    smem = pl.BlockSpec(memory_space=pltpu.SMEM)
    stat = pl.BlockSpec((2, nq, 1, tq), lambda p: (p, 0, 0, 0))
    blocks_t = pl.BlockSpec((nq, 2 * HEAD_LANES, tq), lambda p: (0, p, 0))
    return pl.pallas_call(
        body, name="attn_bwd", grid=(N_HEADS // 2,),
        in_specs=[smem, smem,
                  pl.BlockSpec((T, 2 * HEAD_LANES), lambda p: (0, p)),
                  pl.BlockSpec((T, LANES), lambda p: (0, p)),
                  blocks_t, blocks_t, blocks_t, stat, stat],
        out_specs=[blocks_t,
                   pl.BlockSpec((T, 2 * HEAD_LANES), lambda p: (0, p)),
                   pl.BlockSpec((T, LANES), lambda p: (0, p))],
        out_shape=[jax.ShapeDtypeStruct((nq, N_HEADS * HEAD_LANES, tq), F32),
                   jax.ShapeDtypeStruct((T, N_HEADS * HEAD_LANES), F32),
                   jax.ShapeDtypeStruct((T, MLA_WIDTH), F32)],
        scratch_shapes=[pltpu.VMEM((4, tq, tq), F32), pltpu.VMEM((4, tq, tq), F32),
                        pltpu.VMEM((ATTN_TRIP, 2 * HEAD_LANES, tq), F32), pltpu.VMEM((ATTN_TRIP, LANES, tq), F32),
                        pltpu.VMEM((2, tq, tq), F32)],
        compiler_params=_params(1),
    )(jj, ii, k, v, q_t, k_t, do_t, lse, dsum)


def _lower_bound(lbp):
    a, b = lbp[0:1, :], lbp[1:2, :]
    mx = jnp.maximum(a, b)
    ea, eb = jnp.exp(a - mx), jnp.exp(b - mx)
    return ea / (ea + eb)


def _tri(lower):
    r = lax.broadcasted_iota(jnp.int32, (CHUNK, CHUNK), 0)
    c = lax.broadcasted_iota(jnp.int32, (CHUNK, CHUNK), 1)
    return (c <= r) if lower else (c >= r)


def _hg_gates(hq, hf, lb):
    sq = _sigmoid(hq)
    sf = _sigmoid(hf)
    f = lb + (1.0 - lb) * sf
    g = jnp.log(f)
    gcum = jnp.dot(_tri(True).astype(F32), g, precision=lax.Precision.HIGHEST, preferred_element_type=F32)
    return sq, sf, f, hq * sq, 1.0 - f, gcum


def _head(x, hd):
    return x[:, hd * HG_DIM:(hd + 1) * HG_DIM]


def _all_heads(fn):
    return jnp.concatenate([fn(hd) for hd in range(HG_HEADS)], axis=1)


def _hg_blocks(q, kk, gcum):
    rowi = lax.broadcasted_iota(jnp.int32, gcum.shape, 0)
    out = []
    for blk in range(CHUNK // SUB):
        lo, hi = blk * SUB, (blk + 1) * SUB
        gb = gcum[lo - 1:lo, :] if blk else jnp.zeros_like(gcum[0:1, :])
        eq = jnp.exp(gcum[lo:hi, :] - gb)
        ek = jnp.exp(jnp.where(rowi < hi, gb - gcum, 0.0))
        out.append((eq, ek, (q[lo:hi, :] * eq).astype(BF16), (kk * ek).astype(BF16)))
    return out


def _hg_scores(blocks):
    out = []
    for hd in range(HG_HEADS):
        a = jnp.concatenate([_dot_nt(_head(qb, hd), _head(kb, hd)) for _, _, qb, kb in blocks], axis=0)
        out.append(jnp.where(_tri(True), a, 0.0))
    return out


HG_STEP_CHUNKS = 4


def _hgrn_fwd(proj, lbp):
    T = proj.shape[0]
    nc = T // CHUNK
    ns = min(HG_STEP_CHUNKS, nc)
    rows = ns * CHUNK

    def body(hq_ref, hf_ref, hi_ref, lbp_ref, o_ref, st_ref, state):
        @pl.when(pl.program_id(0) == 0)
        def _():
            state[...] = jnp.zeros_like(state)

        lb = _lower_bound(lbp_ref[...])
        work = []
        for c in range(ns):
            r = slice(c * CHUNK, (c + 1) * CHUNK)
            _, _, _, q, kk, gcum = _hg_gates(hq_ref[r, :], hf_ref[r, :], lb)
            vb = hi_ref[r, :].astype(BF16)
            a = _hg_scores(_hg_blocks(q, kk, gcum))
            gend = gcum[CHUNK - 1:CHUNK, :]
            qgb = (q * jnp.exp(gcum)).astype(BF16)
            kgeb = (kk * jnp.exp(gend - gcum)).astype(BF16)
            intra = [_dot(a[hd].astype(BF16), _head(vb, hd)) for hd in range(HG_HEADS)]
            update = [_dot_tn(_head(vb, hd), _head(kgeb, hd)) for hd in range(HG_HEADS)]
            work.append((qgb, jnp.exp(gend), intra, update))
        for hd in range(HG_HEADS):
            st = state[hd]
            for c, (qgb, egend, intra, update) in enumerate(work):
                st_ref[c, hd] = st
                o_ref[c * CHUNK:(c + 1) * CHUNK, hd * HG_DIM:(hd + 1) * HG_DIM] = (
                    intra[hd] + _dot_nt(_head(qgb, hd), st.astype(BF16)))
                st = st * _head(egend, hd) + update[hd]
            state[hd] = st

    def col(cb):
        return pl.BlockSpec((rows, HG_WIDTH), lambda i: (i, cb))

    return pl.pallas_call(
        body, name="hgrn_fwd", grid=(nc // ns,),
        in_specs=[col(P_HQ // HG_WIDTH), col(P_HF // HG_WIDTH), col(P_HI // HG_WIDTH), _full((2, HG_WIDTH))],
        out_specs=[pl.BlockSpec((rows, HG_WIDTH), lambda i: (i, 0)),
                   pl.BlockSpec((ns, HG_HEADS, HG_DIM, HG_DIM), lambda i: (i, 0, 0, 0))],
        out_shape=[jax.ShapeDtypeStruct((T, HG_WIDTH), F32),
                   jax.ShapeDtypeStruct((nc, HG_HEADS, HG_DIM, HG_DIM), F32)],
        scratch_shapes=[pltpu.VMEM((HG_HEADS, HG_DIM, HG_DIM), F32)],
        compiler_params=_params(1),
    )(proj, proj, proj, lbp)


def _hgrn_bwd(proj, lbp, do_hg, states):
    T = proj.shape[0]
    nc = T // CHUNK
    ns = min(HG_STEP_CHUNKS, nc)
    rows = ns * CHUNK
    steps = nc // ns

    def body(hq_ref, hf_ref, hi_ref, lbp_ref, do_ref, st_ref, dhq_ref, dhf_ref, dhi_ref, dlb_ref, dstate):
        @pl.when(pl.program_id(0) == 0)
        def _():
            dstate[...] = jnp.zeros_like(dstate)
            dlb_ref[...] = jnp.zeros_like(dlb_ref)

        lb = _lower_bound(lbp_ref[...])

        dst_all = [dstate[hd] for hd in range(HG_HEADS)]
        dlb = jnp.zeros_like(lb)
        last = lax.broadcasted_iota(jnp.int32, (CHUNK, HG_WIDTH), 0) == CHUNK - 1
        for c in reversed(range(ns)):
            r = slice(c * CHUNK, (c + 1) * CHUNK)
            hq = hq_ref[r, :]
            sq, sf, f, q, kk, gcum = _hg_gates(hq, hf_ref[r, :], lb)
            vb = hi_ref[r, :].astype(BF16)
            dob = do_ref[r, :].astype(BF16)
            blocks = _hg_blocks(q, kk, gcum)
            a = _hg_scores(blocks)
            gend = gcum[CHUNK - 1:CHUNK, :]
            eg, egend, ekend = jnp.exp(gcum), jnp.exp(gend), jnp.exp(gend - gcum)
            qg, kge = q * eg, kk * ekend
            qgb, kgeb = qg.astype(BF16), kge.astype(BF16)

            dv, dqg, dkge, st_dst, dq_blk, dk_blk = [], [], [], [], [], []
            for hd in range(HG_HEADS):
                st = st_ref[c, hd]
                dst = dst_all[hd]
                dstb = dst.astype(BF16)
                do_h, v_h = _head(dob, hd), _head(vb, hd)
                dv.append(_dot_tn(a[hd].astype(BF16), do_h) + _dot_nt(_head(kgeb, hd), dstb))
                da = jnp.where(_tri(True), _dot_nt(do_h, v_h), 0.0).astype(BF16)
                dqg.append(_dot(do_h, st.astype(BF16)))
                dkge.append(_dot(v_h, dstb))
                st_dst.append(jnp.sum(st * dst, axis=0, keepdims=True))
                dst_all[hd] = _dot_tn(do_h, _head(qgb, hd)) + dst * _head(egend, hd)
                dq_blk.append([_dot(da[b * SUB:(b + 1) * SUB, :], _head(kb, hd)) for b, (_, _, _, kb) in enumerate(blocks)])
                dk_blk.append([_dot_tn(da[b * SUB:(b + 1) * SUB, :], _head(qb, hd)) for b, (_, _, qb, _) in enumerate(blocks)])
            dv, dqg, dkge, st_dst = (jnp.concatenate(t, axis=1) for t in (dv, dqg, dkge, st_dst))

            dq_a, dg_q = [], []
            dk_a, dg_k = jnp.zeros_like(gcum), jnp.zeros_like(gcum)
            for b, (eq, ek, qb, kb) in enumerate(blocks):
                dq_b = _all_heads(lambda hd: dq_blk[hd][b])
                dk_b = _all_heads(lambda hd: dk_blk[hd][b])
                dq_a.append(dq_b * eq)
                dk_a = dk_a + dk_b * ek
                dg_q.append(qb.astype(F32) * dq_b)
                dg_k = dg_k + kb.astype(F32) * dk_b
            dq_a = jnp.concatenate(dq_a, axis=0)

            dgend = st_dst * egend + jnp.sum(dkge * kge, axis=0, keepdims=True)
            dq = dq_a + dqg * eg
            dk = dk_a + dkge * ekend
            dgc = jnp.concatenate(dg_q, axis=0) - dg_k + dqg * qg - dkge * kge + jnp.where(last, dgend, 0.0)
            dg = jnp.dot(_tri(False).astype(F32), dgc, precision=lax.Precision.HIGHEST, preferred_element_type=F32)
            df = dg / f - dk
            dhf_ref[r, :] = df * (1.0 - lb) * sf * (1.0 - sf)
            dlb = dlb + jnp.sum(df * (1.0 - sf), axis=0, keepdims=True)
            dhq_ref[r, :] = dq * (sq * (1.0 + hq * (1.0 - sq)))
            dhi_ref[r, :] = dv
        for hd in range(HG_HEADS):
            dstate[hd] = dst_all[hd]
        dlb_ref[...] += dlb

    def col(cb):
        return pl.BlockSpec((rows, HG_WIDTH), lambda i: (steps - 1 - i, cb))

    grad = jax.ShapeDtypeStruct((T, HG_WIDTH), F32)
    return pl.pallas_call(
        body, name="hgrn_bwd", grid=(steps,),
        in_specs=[col(P_HQ // HG_WIDTH), col(P_HF // HG_WIDTH), col(P_HI // HG_WIDTH), _full((2, HG_WIDTH)),
                  col(0), pl.BlockSpec((ns, HG_HEADS, HG_DIM, HG_DIM), lambda i: (steps - 1 - i, 0, 0, 0))],
        out_specs=[col(0), col(0), col(0), _full((1, HG_WIDTH))],
        out_shape=[grad, grad, grad, jax.ShapeDtypeStruct((1, HG_WIDTH), F32)],
        scratch_shapes=[pltpu.VMEM((HG_HEADS, HG_DIM, HG_DIM), F32)],
        compiler_params=_params(1),
    )(proj, proj, proj, lbp, do_hg, states)


def _top(x, tgt, o_mla, o_hg, proj, w_out, hg_norm_g, final_g, tm):
    T = x.shape[0]

    def body(x_ref, tgt_ref, om_ref, oh_ref, gm_ref, gh_ref, wout_ref, hgn_ref, fng_ref,
             dx2_ref, dom_ref, dsum_ref, dgm_ref, doh_ref, dgh_ref, loss_ref, dfng_ref, dhgn_ref, dwout_ref, ycat_ref):
        @pl.when(pl.program_id(0) == 0)
        def _():
            for ref in (loss_ref, dfng_ref, dhgn_ref, dwout_ref):
                ref[...] = jnp.zeros_like(ref)

        gm, om = gm_ref[...], om_ref[...]
        sgm = _sigmoid(gm)
        silu_m = gm * sgm
        gh, oh, gam = gh_ref[...], oh_ref[...], hgn_ref[...]
        sgh = _sigmoid(gh)
        silu_h = gh * sgh
        rr, nn = [], []
        for hd in range(HG_HEADS):
            oh_h = oh[:, hd * HG_DIM:(hd + 1) * HG_DIM]
            r_h = lax.rsqrt(jnp.mean(oh_h * oh_h, axis=-1, keepdims=True) + EPS)
            rr.append(r_h)
            nn.append(oh_h * r_h)
        n = jnp.concatenate(nn, axis=1)
        ng = n * gam
        ycat_ref[:, :MLA_WIDTH] = (om * silu_m).astype(BF16)
        ycat_ref[:, MLA_WIDTH:] = (ng * silu_h).astype(BF16)
        wout = wout_ref[...]
        x2 = x_ref[...] + _dot(ycat_ref[...], wout)
        r = lax.rsqrt(jnp.mean(x2 * x2, axis=-1, keepdims=True) + EPS)
        xh = x2 * r
        fng = fng_ref[...]
        err = xh * fng - tgt_ref[...]
        loss_ref[...] += 0.5 * jnp.sum(jnp.mean(err * err, axis=-1, keepdims=True), axis=0, keepdims=True)
        dout = err * (1.0 / D_MODEL)
        dfng_ref[...] += jnp.sum(dout * xh, axis=0, keepdims=True)
        dxh = dout * fng
        dx2 = r * (dxh - xh * jnp.mean(dxh * xh, axis=-1, keepdims=True))
        dx2_ref[...] = dx2
        dx2b = dx2.astype(BF16)
        dwout_ref[...] += _dot_tn(ycat_ref[...], dx2b)
        dycat = _dot_nt(dx2b, wout)
        dym, dyh = dycat[:, :MLA_WIDTH], dycat[:, MLA_WIDTH:]
        dom = dym * silu_m
        first = lax.broadcasted_iota(jnp.int32, (tm, LANES), 1) < V_DIM
        for pp in range(N_HEADS // 2):
            pair = dom[:, pp * LANES:(pp + 1) * LANES]
            dom_ref[:, 2 * pp * HEAD_LANES:(2 * pp + 1) * HEAD_LANES] = jnp.where(first, pair, 0.0).astype(BF16)
            dom_ref[:, (2 * pp + 1) * HEAD_LANES:(2 * pp + 2) * HEAD_LANES] = jnp.where(first, 0.0, pair).astype(BF16)
        head_of = lax.broadcasted_iota(jnp.int32, (MLA_WIDTH, LANES), 0) // V_DIM
        pick = (head_of == lax.broadcasted_iota(jnp.int32, (MLA_WIDTH, LANES), 1)).astype(F32)
        dsum_ref[...] = jnp.dot(dom * om, pick, precision=lax.Precision.HIGHEST, preferred_element_type=F32)
        dgm_ref[...] = dym * om * (sgm * (1.0 + gm * (1.0 - sgm)))
        dgh_ref[...] = dyh * ng * (sgh * (1.0 + gh * (1.0 - sgh)))
        dng = dyh * silu_h
        dhgn_ref[...] += jnp.sum(dng * n, axis=0, keepdims=True)
        dn = dng * gam
        for hd in range(HG_HEADS):
            sl = slice(hd * HG_DIM, (hd + 1) * HG_DIM)
            dn_h, n_h = dn[:, sl], nn[hd]
            doh_ref[:, sl] = rr[hd] * (dn_h - n_h * jnp.mean(dn_h * n_h, axis=-1, keepdims=True))

    def row(w, cb=0):
        return pl.BlockSpec((tm, w), lambda i: (i, cb))

    outs = [(D_MODEL, F32), (N_HEADS * HEAD_LANES, BF16), (LANES, F32), (MLA_WIDTH, F32), (HG_WIDTH, F32), (HG_WIDTH, F32)]
    small = [(1, SMALL_W), (1, D_MODEL), (1, HG_WIDTH), (D_MODEL, D_MODEL)]
    return pl.pallas_call(
        body, name="top", grid=(T // tm,),
        in_specs=[row(D_MODEL), row(D_MODEL), row(MLA_WIDTH), row(HG_WIDTH),
                  row(MLA_WIDTH, P_GM // MLA_WIDTH), row(HG_WIDTH, P_GH // HG_WIDTH),
                  _full((D_MODEL, D_MODEL)), _full((1, HG_WIDTH)), _full((1, D_MODEL))],
        out_specs=[row(w) for w, _ in outs] + [_full(s) for s in small],
        out_shape=[jax.ShapeDtypeStruct((T, w), dt) for w, dt in outs] + [jax.ShapeDtypeStruct(s, F32) for s in small],
        scratch_shapes=[pltpu.VMEM((tm, D_MODEL), BF16)],
        compiler_params=_params(1),
    )(x, tgt, o_mla, o_hg, proj, proj, w_out, hg_norm_g, final_g)


def _bot(x, dx2, proj, qn, kvn, dq, dk, dv, dgm, dhq, dhf, dhi, dgh, c_t, s1_t, s2_t, w_in_p, w_q_p, w_kv_p, ln_g, q_g, kv_g, tm):
    T = x.shape[0]
    lat_w = D_PERM - P_QL

    def body(x_ref, dx2_ref, lat_ref, qn_ref, kvn_ref, dq_ref, dk_ref, dv_ref, dgm_ref, dhq_ref, dhf_ref, dhi_ref,
             dgh_ref, c_ref, s1_ref, s2_ref, win_ref, wq_ref, wkv_ref, lng_ref, qg_ref, kvg_ref,
             dx_ref, dproj_ref, dlng_ref, dqg_ref, dkvg_ref, dwq_ref, dwkv_ref, dqpre_ref, dkv_ref):
        @pl.when(pl.program_id(0) == 0)
        def _():
            for ref in (dlng_ref, dqg_ref, dkvg_ref, dwq_ref, dwkv_ref):
                ref[...] = jnp.zeros_like(ref)

        c, s1, s2 = c_ref[...], s1_ref[...], s2_ref[...]
        dkpe = jnp.zeros((tm, LANES), F32)
        for hd in range(N_HEADS):
            sl = slice(hd * HEAD_LANES, (hd + 1) * HEAD_LANES)
            dqpre_ref[:, sl] = _rope_bwd(dq_ref[:, sl], c, s1, s2).astype(BF16)
            dk_h = dk_ref[:, sl]
            dkpe = dkpe + dk_h
            dkv_ref[:, sl] = dk_h.astype(BF16)
        dkv_ref[:, N_HEADS * HEAD_LANES:] = dv_ref[...].astype(BF16)
        lane = lax.broadcasted_iota(jnp.int32, (tm, LANES), 1)
        rope_lanes = jnp.logical_and(lane >= ROPE_LO, lane < ROPE_LO + ROPE)
        dkr = jnp.where(rope_lanes, _rope_bwd(dkpe, c, s1, s2), 0.0)

        def norm_bwd(v, g, dy):
            r = lax.rsqrt(jnp.mean(v * v, axis=-1, keepdims=True) + EPS)
            vh = v * r
            dvh = dy * g
            return jnp.sum(dy * vh, axis=0, keepdims=True), r * (dvh - vh * jnp.mean(dvh * vh, axis=-1, keepdims=True))

        dwq_ref[...] += _dot_tn(qn_ref[...], dqpre_ref[...])
        dwkv_ref[...] += _dot_tn(kvn_ref[...], dkv_ref[...])
        dqn = _dot_nt(dqpre_ref[...], wq_ref[...])
        dg_q, dql = norm_bwd(lat_ref[:, :Q_RANK], qg_ref[...], dqn)
        dqg_ref[...] += dg_q
        dkn = _dot_nt(dkv_ref[...], wkv_ref[...])
        dg_kv, dkvl = norm_bwd(lat_ref[:, Q_RANK:Q_RANK + KV_RANK], kvg_ref[...], dkn)
        dkvg_ref[...] += dg_kv

        dproj_ref[:, P_GM:P_GM + MLA_WIDTH] = dgm_ref[...].astype(BF16)
        dproj_ref[:, P_HQ:P_HQ + HG_WIDTH] = dhq_ref[...].astype(BF16)
        dproj_ref[:, P_HF:P_HF + HG_WIDTH] = dhf_ref[...].astype(BF16)
        dproj_ref[:, P_HI:P_HI + HG_WIDTH] = dhi_ref[...].astype(BF16)
        dproj_ref[:, P_GH:P_GH + HG_WIDTH] = dgh_ref[...].astype(BF16)
        dproj_ref[:, P_QL:P_QL + Q_RANK] = dql.astype(BF16)
        dproj_ref[:, P_KVL:P_KVL + KV_RANK] = dkvl.astype(BF16)
        dproj_ref[:, P_KR:P_KR + LANES] = dkr.astype(BF16)
        dh = _dot_nt(dproj_ref[...], win_ref[...])
        dg_ln, dxn = norm_bwd(x_ref[...], lng_ref[...], dh)
        dlng_ref[...] += dg_ln
        dx_ref[...] = dx2_ref[...] + dxn

    def row(w, cb=0):
        return pl.BlockSpec((tm, w), lambda i: (i, cb))

    hl = N_HEADS * HEAD_LANES
    outs = [(D_MODEL, F32), (D_PERM, BF16)]
    small = [(1, D_MODEL), (1, Q_RANK), (1, KV_RANK), (Q_RANK, hl), (KV_RANK, hl + MLA_WIDTH)]
    return pl.pallas_call(
        body, name="bot", grid=(T // tm,),
        in_specs=[row(D_MODEL), row(D_MODEL), row(lat_w, P_QL // lat_w), row(Q_RANK), row(KV_RANK),
                  row(hl), row(hl), row(MLA_WIDTH),
                  row(MLA_WIDTH), row(HG_WIDTH), row(HG_WIDTH), row(HG_WIDTH), row(HG_WIDTH),
                  row(LANES), row(LANES), row(LANES),
                  _full((D_MODEL, D_PERM)), _full((Q_RANK, hl)), _full((KV_RANK, hl + MLA_WIDTH)),
                  _full((1, D_MODEL)), _full((1, Q_RANK)), _full((1, KV_RANK))],
        out_specs=[row(w) for w, _ in outs] + [_full(s) for s in small],
        out_shape=[jax.ShapeDtypeStruct((T, w), dt) for w, dt in outs] + [jax.ShapeDtypeStruct(s, F32) for s in small],
        scratch_shapes=[pltpu.VMEM((tm, hl), BF16), pltpu.VMEM((tm, hl + MLA_WIDTH), BF16)],
        compiler_params=_params(1),
    )(x, dx2, proj, qn, kvn, dq, dk, dv, dgm, dhq, dhf, dhi, dgh, c_t, s1_t, s2_t, w_in_p, w_q_p, w_kv_p, ln_g, q_g,
      kv_g)


def _matmul_tn(a, b, bn, bt, name):
    T, M = a.shape
    N = b.shape[1]

    def body(a_ref, b_ref, o_ref):
        @pl.when(pl.program_id(1) == 0)
        def _():
            o_ref[...] = jnp.zeros_like(o_ref)

        o_ref[...] += _dot_tn(a_ref[...], b_ref[...])

    return pl.pallas_call(
        body, name=name, grid=(N // bn, T // bt),
        in_specs=[pl.BlockSpec((bt, M), lambda n, t: (t, 0)), pl.BlockSpec((bt, bn), lambda n, t: (t, n))],
        out_specs=pl.BlockSpec((M, bn), lambda n, t: (0, n)),
        out_shape=jax.ShapeDtypeStruct((M, N), F32),
        compiler_params=_params(2),
    )(a, b)


RS_ROWS = 256


def _reduce_scatter(slabs, small):
    n = len(slabs)
    units = [(a, r0, min(s.shape[1], RS_ROWS)) for a, s in enumerate(slabs) for r0 in range(0, s.shape[1], RS_ROWS)]
    nu = len(units)

    def body(*refs):
        ins, small_ref = refs[:n], refs[n]
        outs, small_out = refs[n + 1:2 * n + 1], refs[2 * n + 1]
        own, sib_land, ici_out, ici_land = (refs[(2 + g) * n + 2:(3 + g) * n + 2] for g in range(4))
        small_land = refs[6 * n + 2]
        loc_sems, d2d_send, d2d_recv, ici_send, ici_recv, sm_send, sm_recv = refs[6 * n + 3:6 * n + 10]
        x, y, c = lax.axis_index("x"), lax.axis_index("y"), lax.axis_index("c")
        me = 4 * x + 2 * y + c

        def chip(k):
            return (1 - x if k & 2 else x, 1 - y if k & 1 else y)

        def block(k, core):
            px, py = chip(k)
            return 4 * px + 2 * py + core

        def part(u):
            a, r0, nr = units[u]
            return a, pl.ds(r0, nr)

        def local(u, k):
            a, rows = part(u)
            return pltpu.make_async_copy(ins[a].at[block(k, c), rows, :], own[a].at[k, rows, :], loc_sems.at[u, k])

        def to_sibling(u, k):
            a, rows = part(u)
            return pltpu.make_async_remote_copy(
                src_ref=ins[a].at[block(k, 1 - c), rows, :], dst_ref=sib_land[a].at[k, rows, :],
                send_sem=d2d_send.at[u, k], recv_sem=d2d_recv.at[u, k], device_id=(x, y, 1 - c), device_id_type=MESH)

        def to_chip(u, k):
            a, rows = part(u)
            return pltpu.make_async_remote_copy(
                src_ref=ici_out[a].at[k - 1, rows, :], dst_ref=ici_land[a].at[k - 1, rows, :],
                send_sem=ici_send.at[u, k - 1], recv_sem=ici_recv.at[u, k - 1], device_id=(*chip(k), c),
                device_id_type=MESH)

        def small_copy(k, receiving):
            px, py = chip(k >> 1)
            pc = 1 - c if k & 1 else c
            slot = 4 * px + 2 * py + pc if receiving else me
            return pltpu.make_async_remote_copy(
                src_ref=small_ref, dst_ref=small_land.at[slot], send_sem=sm_send.at[k - 1], recv_sem=sm_recv.at[k - 1],
                device_id=(px, py, pc), device_id_type=MESH)

        for u in range(nu):
            for k in range(4):
                local(u, k).start()
        for u in range(nu):
            for k in range(4):
                to_sibling(u, k).start()
        small_land[me] = small_ref[...]
        for k in range(1, N_DEV):
            small_copy(k, False).start()
        for u in range(nu):
            a, rows = part(u)
            for k in range(4):
                local(u, k).wait()
                to_sibling(u, k).wait_recv()
            for k in range(1, 4):
                ici_out[a][k - 1, rows, :] = (own[a][k, rows, :] + sib_land[a][k, rows, :]).astype(BF16)
                to_chip(u, k).start()
        for u in range(nu):
            a, rows = part(u)
            acc = own[a][0, rows, :] + sib_land[a][0, rows, :]
            for k in range(1, 4):
                to_chip(u, k).wait_recv()
                acc = acc + ici_land[a][k - 1, rows, :].astype(F32)
            outs[a][rows, :] = acc
        for k in range(1, N_DEV):
            small_copy(k, True).wait_recv()
        acc = small_land[0]
        for d in range(1, N_DEV):
            acc = acc + small_land[d]
        small_out[...] = acc
        for u in range(nu):
            for k in range(4):
                to_sibling(u, k).wait_send()
            for k in range(1, 4):
                to_chip(u, k).wait_send()
        for k in range(1, N_DEV):
            small_copy(k, False).wait_send()

    vm = pl.BlockSpec(memory_space=pltpu.VMEM)
    hbm = pl.BlockSpec(memory_space=pl.ANY)
    dma = pltpu.SemaphoreType.DMA
    return pl.pallas_call(
        body, name="reduce_scatter_grads",
        in_specs=[hbm] * n + [vm], out_specs=[vm] * (n + 1),
        out_shape=[jax.ShapeDtypeStruct(s.shape[1:], F32) for s in slabs] + [jax.ShapeDtypeStruct(small.shape, F32)],
        scratch_shapes=[pltpu.VMEM((4,) + s.shape[1:], F32) for s in slabs] * 2
        + [pltpu.VMEM((3,) + s.shape[1:], BF16) for s in slabs] * 2
        + [pltpu.VMEM((N_DEV,) + small.shape, F32)]
        + [dma((nu, 4)), dma((nu, 4)), dma((nu, 4)), dma((nu, 3)), dma((nu, 3)), dma((N_DEV - 1,)), dma((N_DEV - 1,))],
        compiler_params=pltpu.CompilerParams(vmem_limit_bytes=VMEM_LIMIT),
    )(*slabs, small)


def _adamw_math(w, g, m, v):
    m = ADAM_B1 * m + (1.0 - ADAM_B1) * g
    v = ADAM_B2 * v + (1.0 - ADAM_B2) * (g * g)
    m_hat = m / (1.0 - ADAM_B1 ** ADAM_STEP)
    v_hat = v / (1.0 - ADAM_B2 ** ADAM_STEP)
    delta = -ADAM_LR * (m_hat / (jnp.sqrt(v_hat) + ADAM_EPS) + ADAM_WD * w)
    return delta, m, v


SMALL_W = 512


def _adamw(big, small_w, small_g):
    nb, ns = len(big), len(small_w)

    def body(*refs):
        k = 0
        big_in = [refs[4 * i:4 * i + 4] for i in range(nb)]
        k = 4 * nb
        small_in = [refs[k + 3 * i:k + 3 * i + 3] for i in range(ns)]
        k += 3 * ns
        sg_ref = refs[k]
        k += 1
        big_out = [refs[k + 3 * i:k + 3 * i + 3] for i in range(nb)]
        k += 3 * nb
        small_out = [refs[k + 4 * i:k + 4 * i + 4] for i in range(ns)]

        for (w, g, m, v), (od, om, ov) in zip(big_in, big_out):
            od[...], om[...], ov[...] = _adamw_math(w[...], g[...], m[...], v[...])

        sg = sg_ref[...]
        lbp = small_in[2][0][...]
        lb = _lower_bound(lbp)
        t = sg[4:5, :] * lb * (1.0 - lb)
        grads = [jnp.concatenate([sg[0:1, :], sg[1:2, :]], axis=1),
                 jnp.concatenate([sg[2:3, :], sg[3:4, :]], axis=1),
                 jnp.concatenate([t, -t], axis=0),
                 sg[6:7, :], sg[7:8, 0:Q_RANK], sg[7:8, Q_RANK:Q_RANK + KV_RANK]]
        for (w, m, v), g, (og, od, om, ov) in zip(small_in, grads, small_out):
            og[...] = g
            od[...], om[...], ov[...] = _adamw_math(w[...], g, m[...], v[...])

    ins = [a for grp in big for a in grp] + [a for grp in small_w for a in grp] + [small_g]
    out_shape = ([jax.ShapeDtypeStruct(grp[0].shape, F32) for grp in big for _ in range(3)]
                 + [jax.ShapeDtypeStruct(grp[0].shape, F32) for grp in small_w for _ in range(4)])
    vm = pl.BlockSpec(memory_space=pltpu.VMEM)
    res = pl.pallas_call(
        body, name="adamw", in_specs=[vm] * len(ins), out_specs=[vm] * len(out_shape), out_shape=out_shape,
        compiler_params=pltpu.CompilerParams(vmem_limit_bytes=VMEM_LIMIT),
    )(*ins)
    big_res = [res[3 * i:3 * i + 3] for i in range(nb)]
    small_res = [res[3 * nb + 4 * i:3 * nb + 4 * i + 4] for i in range(ns)]
    return big_res, small_res


def _perm_weights(g_in, g_q, g_kv, g_out):
    w = g_in.transpose(1, 0, 2).reshape(D_MODEL, D_IN)
    z = lambda n: jnp.zeros((D_MODEL, n), BF16)
    w_in_p = jnp.concatenate([w[:, 416:], w[:, :384], z(64), w[:, 384:416], z(32)], axis=1)
    wq = g_q.transpose(1, 0, 2)
    w_q_p = jnp.pad(wq, ((0, 0), (0, 0), (0, HEAD_LANES - NOPE - ROPE))).reshape(Q_RANK, N_HEADS * HEAD_LANES)
    wkv = g_kv.transpose(1, 0, 2)
    wk = jnp.pad(wkv[:, :, :NOPE], ((0, 0), (0, 0), (0, HEAD_LANES - NOPE))).reshape(KV_RANK, N_HEADS * HEAD_LANES)
    wv = wkv[:, :, NOPE:].reshape(KV_RANK, MLA_WIDTH)
    return w_in_p, w_q_p, jnp.concatenate([wk, wv], axis=1), g_out.reshape(D_MODEL, D_MODEL)


def _grad_slabs(dw_in_p, dw_q_p, dw_kv_p, dw_out):
    dw_in = jnp.concatenate([dw_in_p[:, P_QL:P_KR], dw_in_p[:, P_KR + ROPE_LO:P_KR + ROPE_LO + ROPE], dw_in_p[:, :P_QL]], axis=1)
    s_in = dw_in.reshape(D_MODEL, N_DEV, D_IN // N_DEV).transpose(1, 0, 2)
    s_q = dw_q_p.reshape(Q_RANK, N_HEADS, HEAD_LANES)[:, :, :NOPE + ROPE].transpose(1, 0, 2)
    hl = N_HEADS * HEAD_LANES
    dk = dw_kv_p[:, :hl].reshape(KV_RANK, N_HEADS, HEAD_LANES)[:, :, :NOPE]
    dv = dw_kv_p[:, hl:].reshape(KV_RANK, N_HEADS, V_DIM)
    s_kv = jnp.concatenate([dk, dv], axis=2).transpose(1, 0, 2)
    return s_in, s_q, s_kv, dw_out.reshape(N_DEV, D_MODEL // N_DEV, D_MODEL)


def _block_sizes(T):
    return min(256, T), min(256, T), min(512, T)


def kernel(x, positions, ln_g, w_in, q_a_norm_g, w_q_b, kv_a_norm_g, w_kv_b, hg_lower_bounds, hg_norm_g, w_out, final_norm_g, loss_target, m_ln_g, m_w_in, m_q_a_norm_g, m_w_q_b, m_kv_a_norm_g, m_w_kv_b, m_hg_lower_bounds, m_hg_norm_g, m_w_out, m_final_norm_g, v_ln_g, v_w_in, v_q_a_norm_g, v_w_q_b, v_kv_a_norm_g, v_w_kv_b, v_hg_lower_bounds, v_hg_norm_g, v_w_out, v_final_norm_g):
    T = x.shape[1]
    tm, tq, bt = _block_sizes(T)
    nq = T // tq
    xs, tgt = x[0], loss_target[0]
    pos_f = positions.astype(F32)
    fng = final_norm_g.reshape(1, D_MODEL)

    gathered = _all_gather_weights([w_in[0], w_q_b[0], w_kv_b[0], w_out[0]])
    w_in_p, w_q_p, w_kv_p, w_out_b = _perm_weights(*gathered)

    c_t, s1_t, s2_t = _rope_tables(pos_f, bt)
    proj, h, qn, kvn, q, k, v = _fwd_in(xs, ln_g, w_in_p, q_a_norm_g, w_q_p, kv_a_norm_g, w_kv_p, c_t, s1_t, s2_t, tm)
    hl = N_HEADS * HEAD_LANES
    v_t = v.reshape(nq, tq, MLA_WIDTH).transpose(0, 2, 1)
    k_t = k.reshape(nq, tq, hl).transpose(0, 2, 1)
    q_t = q.reshape(nq, tq, hl).transpose(0, 2, 1)
    o_mla, lse = _attn_fwd_flat(k, q_t, v_t, tq)
    o_hg, states = _hgrn_fwd(proj, hg_lower_bounds)
    dx2, d_om, dsum, d_gm, d_oh, d_gh, loss_p, d_fng, d_hgn, dw_out = _top(
        xs, tgt, o_mla, o_hg, proj, w_out_b, hg_norm_g, fng, tm)
    dsum = dsum[:, :N_HEADS].T.reshape(N_HEADS, nq, 1, tq)
    do_t = d_om.reshape(nq, tq, hl).transpose(0, 2, 1)
    dq_t, dk, dv = _attn_bwd_flat(k, v, q_t, k_t, do_t, lse, dsum, tq)
    dq = dq_t.transpose(0, 2, 1).reshape(T, N_HEADS * HEAD_LANES)
    d_hq, d_hf, d_hi, d_lb = _hgrn_bwd(proj, hg_lower_bounds, d_oh, states)
    dx, dproj, d_lng, d_qg, d_kvg, dw_q_p, dw_kv_p = _bot(
        xs, dx2, proj, qn, kvn, dq, dk, dv, d_gm, d_hq, d_hf, d_hi, d_gh, c_t, s1_t, s2_t, w_in_p, w_q_p, w_kv_p,
        ln_g, q_a_norm_g, kv_a_norm_g, tm)
    dw_in_p = _matmul_tn(h, dproj, 512, bt, "dw_in")

    small = jnp.concatenate([
        d_lng.reshape(2, SMALL_W), d_fng.reshape(2, SMALL_W), d_lb, loss_p, d_hgn,
        jnp.concatenate([d_qg, d_kvg, jnp.zeros((1, SMALL_W - Q_RANK - KV_RANK), F32)], axis=1)], axis=0)
    g_in, g_q, g_kv, g_out, small_sum = _reduce_scatter(list(_grad_slabs(dw_in_p, dw_q_p, dw_kv_p, dw_out)), small)

    big = [(w_in[0], g_in, m_w_in[0], v_w_in[0]), (w_q_b[0], g_q, m_w_q_b[0], v_w_q_b[0]),
           (w_kv_b[0], g_kv, m_w_kv_b[0], v_w_kv_b[0]), (w_out[0], g_out, m_w_out[0], v_w_out[0])]
    small_w = [(ln_g, m_ln_g, v_ln_g),
               (fng, m_final_norm_g.reshape(1, D_MODEL), v_final_norm_g.reshape(1, D_MODEL)),
               (hg_lower_bounds, m_hg_lower_bounds, v_hg_lower_bounds), (hg_norm_g, m_hg_norm_g, v_hg_norm_g),
               (q_a_norm_g, m_q_a_norm_g, v_q_a_norm_g), (kv_a_norm_g, m_kv_a_norm_g, v_kv_a_norm_g)]
    big_res, small_res = _adamw(big, small_w, small_sum)

    loss = small_sum[5, 0]
    (r_in, r_q, r_kv, r_out) = big_res
    (s_ln, s_fn, s_lb, s_hgn, s_qg, s_kvg) = small_res
    flat = lambda t: t.reshape(D_MODEL)
    lead = lambda t: t[None]
    grads = [s_ln[0], lead(g_in), s_qg[0], lead(g_q), s_kvg[0], lead(g_kv), s_lb[0], s_hgn[0], lead(g_out), flat(s_fn[0])]

    def pick(i):
        return [s_ln[i + 1], lead(r_in[i]), s_qg[i + 1], lead(r_q[i]), s_kvg[i + 1], lead(r_kv[i]), s_lb[i + 1],
                s_hgn[i + 1], lead(r_out[i]), flat(s_fn[i + 1])]

    return (loss, dx[None], *grads, *pick(0), *pick(1), *pick(2))
```

```python
import math

import numpy as np
import jax
import jax.numpy as jnp
from jax import lax
from jax.experimental import pallas as pl
from jax.experimental.pallas import tpu as pltpu

F32 = jnp.float32
BF16 = jnp.bfloat16

D_MODEL = 1024
N_HEADS = 8
NOPE = 64
ROPE = 32
HALF_ROPE = ROPE // 2
V_DIM = 64
Q_RANK = 256
KV_RANK = 128
MLA_WIDTH = N_HEADS * V_DIM
HG_HEADS = 4
HG_DIM = 128
HG_WIDTH = HG_HEADS * HG_DIM
CHUNK = 64
SUB = 16
D_IN = 2976
D_PERM = 3072
ROPE_THETA = 10000.0
EPS = 1e-6
N_DEV = 8
LANES = 128
HEAD_LANES = 128

P_GM, P_HQ, P_HF, P_HI, P_GH, P_QL, P_KVL, P_KR = 0, 512, 1024, 1536, 2048, 2560, 2816, 2944
ROPE_LO = NOPE
SCALE = 1.0 / math.sqrt(NOPE + ROPE)

ADAM_LR = 0.001
ADAM_B1 = 0.9
ADAM_B2 = 0.999
ADAM_EPS = 1e-08
ADAM_WD = 0.01
ADAM_STEP = 10

VMEM_LIMIT = 56 * 1024 * 1024
MESH = pl.DeviceIdType.MESH

NT = (((1,), (1,)), ((), ()))
TN = (((0,), (0,)), ((), ()))


def _params(n_grid=0, **kw):
    sem = ("arbitrary",) * n_grid if n_grid else None
    return pltpu.CompilerParams(dimension_semantics=sem, vmem_limit_bytes=VMEM_LIMIT, **kw)


def _dot(a, b):
    return jnp.dot(a, b, preferred_element_type=F32)


def _dot_nt(a, b):
    return lax.dot_general(a, b, NT, preferred_element_type=F32)


def _dot_tn(a, b):
    return lax.dot_general(a, b, TN, preferred_element_type=F32)


def _sigmoid(x):
    return 1.0 / (1.0 + jnp.exp(-x))


def _rope_fwd(x, c, s1, s2):
    return x * c + pltpu.roll(x, LANES - HALF_ROPE, 1) * s1 + pltpu.roll(x, HALF_ROPE, 1) * s2


def _rope_bwd(dy, c, s1, s2):
    return dy * c - pltpu.roll(dy, LANES - HALF_ROPE, 1) * s1 - pltpu.roll(dy, HALF_ROPE, 1) * s2


def _full(shape):
    n = len(shape)
    return pl.BlockSpec(shape, lambda *_: (0,) * n)


def _rope_tables(pos_f, tm):
    T = pos_f.shape[1]
    inv = (np.float32(ROPE_THETA) ** (-np.arange(HALF_ROPE, dtype=np.float32) / np.float32(HALF_ROPE))).astype(np.float32)
    place = np.zeros((3, HALF_ROPE, LANES), np.float32)
    for i in range(HALF_ROPE):
        place[0, i, ROPE_LO + i] = place[0, i, ROPE_LO + HALF_ROPE + i] = 1.0
        place[1, i, ROPE_LO + i] = -1.0
        place[2, i, ROPE_LO + HALF_ROPE + i] = 1.0
    base = np.ones((1, LANES), np.float32)
    base[0, ROPE_LO:ROPE_LO + ROPE] = 0.0

    def body(pos_ref, inv_ref, place_ref, base_ref, c_ref, s1_ref, s2_ref):
        ang = inv_ref[...] * pos_ref[...]
        cos, sin = jnp.cos(ang), jnp.sin(ang)

        def put(v, k):
            return lax.dot_general(v, place_ref[k], TN, precision=lax.Precision.HIGHEST, preferred_element_type=F32)

        c_ref[...] = put(cos, 0) + base_ref[...]
        s1_ref[...] = put(sin, 1)
        s2_ref[...] = put(sin, 2)

    tab = jax.ShapeDtypeStruct((T, LANES), F32)
    return pl.pallas_call(
        body, name="rope_tables", grid=(T // tm,),
        in_specs=[pl.BlockSpec((1, tm), lambda i: (0, i)), _full((HALF_ROPE, 1)), _full((3, HALF_ROPE, LANES)),
                  _full((1, LANES))],
        out_specs=[pl.BlockSpec((tm, LANES), lambda i: (i, 0))] * 3,
        out_shape=[tab, tab, tab], compiler_params=_params(1),
    )(pos_f, jnp.asarray(inv.reshape(HALF_ROPE, 1)), jnp.asarray(place), jnp.asarray(base))


def _all_gather_weights(shards):
    n = len(shards)

    def body(*refs):
        ins, outs = refs[:n], refs[n:2 * n]
        send_sems, recv_sems = refs[2 * n], refs[2 * n + 1]
        x, y, c = lax.axis_index("x"), lax.axis_index("y"), lax.axis_index("c")
        me, sibling = (x, y, c), (x, y, 1 - c)
        chips = [(1 - x, y), (x, 1 - y), (1 - x, 1 - y)]

        def idx(d):
            return 4 * d[0] + 2 * d[1] + d[2]

        def copy(a, k, block, to):
            rows = outs[a].at[idx(block)]
            return pltpu.make_async_remote_copy(src_ref=rows, dst_ref=rows, send_sem=send_sems.at[a, k],
                                                recv_sem=recv_sems.at[a, k], device_id=to, device_id_type=MESH)

        for a in range(n):
            outs[a][idx(me)] = ins[a][...].astype(BF16)
        first = []
        for a in range(n):
            first.append(copy(a, 0, me, sibling))
            first += [copy(a, 1 + j, me, (*chip, c)) for j, chip in enumerate(chips)]
        for cp in first:
            cp.start()
        passed = []
        for j, chip in enumerate(chips):
            for a in range(n):
                copy(a, 1 + j, (*chip, c), me).wait_recv()
                cp = copy(a, 4 + j, (*chip, c), sibling)
                cp.start()
                passed.append(cp)
        for a in range(n):
            copy(a, 0, sibling, me).wait_recv()
            for j, chip in enumerate(chips):
                copy(a, 4 + j, (*chip, 1 - c), me).wait_recv()
        for cp in first + passed:
            cp.wait_send()

    vm = pl.BlockSpec(memory_space=pltpu.VMEM)
    return pl.pallas_call(
        body, name="all_gather_weights",
        in_specs=[vm] * n, out_specs=[vm] * n,
        out_shape=[jax.ShapeDtypeStruct((N_DEV,) + s.shape, BF16) for s in shards],
        scratch_shapes=[pltpu.SemaphoreType.DMA((n, 7)), pltpu.SemaphoreType.DMA((n, 7))],
        compiler_params=pltpu.CompilerParams(vmem_limit_bytes=VMEM_LIMIT),
    )(*shards)


def _fwd_in(x, ln_g, w_in_p, q_g, w_q_p, kv_g, w_kv_p, c_t, s1_t, s2_t, tm):
    T = x.shape[0]

    def body(x_ref, lng_ref, win_ref, qg_ref, wq_ref, kvg_ref, wkv_ref, c_ref, s1_ref, s2_ref,
             proj_ref, h_ref, qn_ref, kvn_ref, q_ref, k_ref, v_ref):
        xv = x_ref[...]
        r = lax.rsqrt(jnp.mean(xv * xv, axis=-1, keepdims=True) + EPS)
        h = (xv * r * lng_ref[...]).astype(BF16)
        h_ref[...] = h
        proj = _dot(h, win_ref[...])
        proj_ref[...] = proj
        c, s1, s2 = c_ref[...], s1_ref[...], s2_ref[...]

        ql = proj[:, P_QL:P_QL + Q_RANK]
        rq = lax.rsqrt(jnp.mean(ql * ql, axis=-1, keepdims=True) + EPS)
        qn = (ql * rq * qg_ref[...]).astype(BF16)
        qn_ref[...] = qn
        q = _dot(qn, wq_ref[...])
        for hd in range(N_HEADS):
            sl = slice(hd * HEAD_LANES, (hd + 1) * HEAD_LANES)
            q_ref[:, sl] = _rope_fwd(q[:, sl], c, s1, s2).astype(BF16)

        kvl = proj[:, P_KVL:P_KVL + KV_RANK]
        rk = lax.rsqrt(jnp.mean(kvl * kvl, axis=-1, keepdims=True) + EPS)
        kvn = (kvl * rk * kvg_ref[...]).astype(BF16)
        kvn_ref[...] = kvn
        kv = _dot(kvn, wkv_ref[...])
        kpe = _rope_fwd(proj[:, P_KR:P_KR + LANES], c, s1, s2)
        for hd in range(N_HEADS):
            sl = slice(hd * HEAD_LANES, (hd + 1) * HEAD_LANES)
            k_ref[:, sl] = (kv[:, sl] + kpe).astype(BF16)
        v_ref[...] = kv[:, N_HEADS * HEAD_LANES:].astype(BF16)

    def row(w):
        return pl.BlockSpec((tm, w), lambda i: (i, 0))

    outs = [(D_PERM, F32), (D_MODEL, BF16), (Q_RANK, BF16), (KV_RANK, BF16),
            (N_HEADS * HEAD_LANES, BF16), (N_HEADS * HEAD_LANES, BF16), (MLA_WIDTH, BF16)]
    return pl.pallas_call(
        body, name="fwd_in", grid=(T // tm,),
        in_specs=[row(D_MODEL), _full((1, D_MODEL)), _full((D_MODEL, D_PERM)), _full((1, Q_RANK)),
                  _full((Q_RANK, N_HEADS * HEAD_LANES)), _full((1, KV_RANK)),
                  _full((KV_RANK, N_HEADS * HEAD_LANES + MLA_WIDTH)), row(LANES), row(LANES), row(LANES)],
        out_specs=[row(w) for w, _ in outs],
        out_shape=[jax.ShapeDtypeStruct((T, w), dt) for w, dt in outs],
        compiler_params=_params(1),
    )(x, ln_g, w_in_p, q_g, w_q_p, kv_g, w_kv_p, c_t, s1_t, s2_t)


LOG2E = 1.4426950408889634
SCALE2 = SCALE * LOG2E


def _causal(tq):
    r = lax.broadcasted_iota(jnp.int32, (tq, tq), 0)
    c = lax.broadcasted_iota(jnp.int32, (tq, tq), 1)
    return r <= c


MASKED = -1e30


def _causal_bias(bias_ref, tq):
    bias_ref[0] = jnp.zeros((tq, tq), F32)
    bias_ref[1] = jnp.where(_causal(tq), 0.0, MASKED)


def _tile_tables(nq, by_query):
    if by_query:
        pairs = [(j, i) for i in range(nq) for j in range(i + 1)]
    else:
        pairs = [(j, i) for j in range(nq) for i in range(nq - 1, j - 1, -1)]
    pairs.append(pairs[-1])
    jj, ii = np.array(pairs, np.int32).T
    return jnp.asarray(jj), jnp.asarray(ii), len(pairs) - 1


ATTN_TRIP = 8


def _walk_tiles(n, products, tile, flush, buf_a, buf_b):
    bufs = (buf_a, buf_b)
    products(0, buf_a)

    def trip(r, carry):
        for u in range(ATTN_TRIP):
            products(ATTN_TRIP * r + u + 1, bufs[(u + 1) % 2])
            tile(ATTN_TRIP * r + u, bufs[u % 2])
        for u in range(ATTN_TRIP):
            flush(ATTN_TRIP * r + u)
        return carry

    lax.fori_loop(0, n // ATTN_TRIP, trip, 0)
    rest = n - n % ATTN_TRIP
    for u in range(n % ATTN_TRIP):
        if rest + u + 1 < n:
            products(rest + u + 1, bufs[(u + 1) % 2])
        tile(rest + u, bufs[u % 2])
    for u in range(n % ATTN_TRIP):
        flush(rest + u)


def _attn_fwd_flat(k, q_t, v_t, tq):
    T = k.shape[0]
    nq = T // tq
    jj, ii, n = _tile_tables(nq, True)
    heads = [slice(hh * HEAD_LANES, (hh + 1) * HEAD_LANES) for hh in range(2)]

    def body(jj_ref, ii_ref, k_ref, qt_ref, vt_ref, o_ref, lse_ref, sa_ref, sb_ref, m_ref, l_ref, acc_ref, bias_ref):
        def reset(st):
            m_ref[st] = jnp.full(m_ref.shape[1:], MASKED, F32)
            l_ref[st] = jnp.zeros(l_ref.shape[1:], F32)
            acc_ref[st] = jnp.zeros(acc_ref.shape[1:], F32)

        _causal_bias(bias_ref, tq)
        for st in range(ATTN_TRIP):
            reset(st)

        def products(t, buf):
            j, i = jj_ref[t], ii_ref[t]
            kj = k_ref[pl.ds(pl.multiple_of(j * tq, tq), tq), :]
            for hh, sl in enumerate(heads):
                buf[hh] = _dot(kj[:, sl], qt_ref[i, sl, :])

        def tile(t, buf):
            j, i = jj_ref[t], ii_ref[t]
            vt = vt_ref[j]
            bias = bias_ref.at[(j == i).astype(jnp.int32)]
            st = i % ATTN_TRIP
            for hh in range(2):
                s = buf[hh] * SCALE2 + bias[...]
                m = m_ref[st, hh]
                m_new = jnp.maximum(m, jnp.max(s, axis=0, keepdims=True))
                alpha = jnp.exp2(m - m_new)
                p = jnp.exp2(s - m_new)
                m_ref[st, hh] = m_new
                l_ref[st, hh] = alpha * l_ref[st, hh] + jnp.sum(p, axis=0, keepdims=True)
                acc_ref[st, hh] = alpha * acc_ref[st, hh] + _dot(vt, p.astype(BF16))

        def flush(t):
            j, i = jj_ref[t], ii_ref[t]

            @pl.when(j == i)
            def _():
                st = i % ATTN_TRIP
                first = lax.broadcasted_iota(jnp.int32, (LANES, tq), 0) < V_DIM
                out = jnp.where(first, acc_ref[st, 0] / l_ref[st, 0], acc_ref[st, 1] / l_ref[st, 1])
                o_ref[pl.ds(pl.multiple_of(i * tq, tq), tq), :] = out.T
                for hh in range(2):
                    lse_ref[hh, i] = m_ref[st, hh] + jnp.log2(l_ref[st, hh])
                reset(st)

        _walk_tiles(n, products, tile, flush, sa_ref, sb_ref)

    smem = pl.BlockSpec(memory_space=pltpu.SMEM)
    return pl.pallas_call(
        body, name="attn_fwd", grid=(N_HEADS // 2,),
        in_specs=[smem, smem,
                  pl.BlockSpec((T, 2 * HEAD_LANES), lambda p: (0, p)),
                  pl.BlockSpec((nq, 2 * HEAD_LANES, tq), lambda p: (0, p, 0)),
                  pl.BlockSpec((nq, LANES, tq), lambda p: (0, p, 0))],
        out_specs=[pl.BlockSpec((T, LANES), lambda p: (0, p)),
                   pl.BlockSpec((2, nq, 1, tq), lambda p: (p, 0, 0, 0))],
        out_shape=[jax.ShapeDtypeStruct((T, MLA_WIDTH), F32), jax.ShapeDtypeStruct((N_HEADS, nq, 1, tq), F32)],
        scratch_shapes=[pltpu.VMEM((2, tq, tq), F32), pltpu.VMEM((2, tq, tq), F32),
                        pltpu.VMEM((ATTN_TRIP, 2, 1, tq), F32), pltpu.VMEM((ATTN_TRIP, 2, 1, tq), F32),
                        pltpu.VMEM((ATTN_TRIP, 2, LANES, tq), F32), pltpu.VMEM((2, tq, tq), F32)],
        compiler_params=_params(1),
    )(jj, ii, k, q_t, v_t)


def _attn_bwd_flat(k, v, q_t, k_t, do_t, lse, dsum, tq):
    T = k.shape[0]
    nq = T // tq
    jj, ii, n = _tile_tables(nq, False)
    heads = [slice(hh * HEAD_LANES, (hh + 1) * HEAD_LANES) for hh in range(2)]

    def body(jj_ref, ii_ref, k_ref, v_ref, qt_ref, kt_ref, dot_ref, lse_ref, dsum_ref, dqt_ref, dk_ref, dv_ref,
             ba_ref, bb_ref, dkt_ref, dvt_ref, bias_ref):
        _causal_bias(bias_ref, tq)
        dqt_ref[...] = jnp.zeros_like(dqt_ref)
        dkt_ref[...] = jnp.zeros_like(dkt_ref)
        dvt_ref[...] = jnp.zeros_like(dvt_ref)

        def products(t, buf):
            j, i = jj_ref[t], ii_ref[t]
            rows = pl.ds(pl.multiple_of(j * tq, tq), tq)
            for hh, sl in enumerate(heads):
                buf[hh] = _dot(k_ref[rows, sl], qt_ref[i, sl, :])
                buf[2 + hh] = _dot(v_ref[rows, :], dot_ref[i, sl, :])

        def tile(t, buf):
            j, i = jj_ref[t], ii_ref[t]
            bias = bias_ref.at[(j == i).astype(jnp.int32)]
            st = j % ATTN_TRIP
            dv_new = None
            for hh, sl in enumerate(heads):
                p = jnp.exp2(buf[hh] * SCALE2 + bias[...] - lse_ref[hh, i])
                ds = (p * (buf[2 + hh] - dsum_ref[hh, i]) * SCALE).astype(BF16)
                dv_h = _dot_nt(dot_ref[i, sl, :], p.astype(BF16))
                dv_new = dv_h if dv_new is None else dv_new + dv_h
                dkt_ref[st, sl, :] += _dot_nt(qt_ref[i, sl, :], ds)
                dqt_ref[i, sl, :] += _dot(kt_ref[j, sl, :], ds)
            dvt_ref[st] += dv_new

        def flush(t):
            j, i = jj_ref[t], ii_ref[t]

            @pl.when(j == i)
            def _():
                st = j % ATTN_TRIP
                rows = pl.ds(pl.multiple_of(j * tq, tq), tq)
                dk_ref[rows, :] = dkt_ref[st].T
                dv_ref[rows, :] = dvt_ref[st].T
                dkt_ref[st] = jnp.zeros(dkt_ref.shape[1:], F32)
                dvt_ref[st] = jnp.zeros(dvt_ref.shape[1:], F32)

        _walk_tiles(n, products, tile, flush, ba_ref, bb_ref)

    smem = pl.BlockSpec(memory_space=pltpu.SMEM)
    stat = pl.BlockSpec((2, nq, 1, tq), lambda p: (p, 0, 0, 0))
    blocks_t = pl.BlockSpec((nq, 2 * HEAD_LANES, tq), lambda p: (0, p, 0))
    return pl.pallas_call(
        body, name="attn_bwd", grid=(N_HEADS // 2,),
        in_specs=[smem, smem,
                  pl.BlockSpec((T, 2 * HEAD_LANES), lambda p: (0, p)),
                  pl.BlockSpec((T, LANES), lambda p: (0, p)),
                  blocks_t, blocks_t, blocks_t, stat, stat],
        out_specs=[blocks_t,
                   pl.BlockSpec((T, 2 * HEAD_LANES), lambda p: (0, p)),
                   pl.BlockSpec((T, LANES), lambda p: (0, p))],
        out_shape=[jax.ShapeDtypeStruct((nq, N_HEADS * HEAD_LANES, tq), F32),
                   jax.ShapeDtypeStruct((T, N_HEADS * HEAD_LANES), F32),
                   jax.ShapeDtypeStruct((T, MLA_WIDTH), F32)],
        scratch_shapes=[pltpu.VMEM((4, tq, tq), F32), pltpu.VMEM((4, tq, tq), F32),
                        pltpu.VMEM((ATTN_TRIP, 2 * HEAD_LANES, tq), F32), pltpu.VMEM((ATTN_TRIP, LANES, tq), F32),
                        pltpu.VMEM((2, tq, tq), F32)],
        compiler_params=_params(1),
    )(jj, ii, k, v, q_t, k_t, do_t, lse, dsum)


def _lower_bound(lbp):
    a, b = lbp[0:1, :], lbp[1:2, :]
    mx = jnp.maximum(a, b)
    ea, eb = jnp.exp(a - mx), jnp.exp(b - mx)
    return ea / (ea + eb)


def _tri(lower):
    r = lax.broadcasted_iota(jnp.int32, (CHUNK, CHUNK), 0)
    c = lax.broadcasted_iota(jnp.int32, (CHUNK, CHUNK), 1)
    return (c <= r) if lower else (c >= r)


def _hg_gates(hq, hf, lb):
    sq = _sigmoid(hq)
    sf = _sigmoid(hf)
    f = lb + (1.0 - lb) * sf
    g = jnp.log(f)
    gcum = jnp.dot(_tri(True).astype(F32), g, precision=lax.Precision.HIGHEST, preferred_element_type=F32)
    return sq, sf, f, hq * sq, 1.0 - f, gcum


def _head(x, hd):
    return x[:, hd * HG_DIM:(hd + 1) * HG_DIM]


def _all_heads(fn):
    return jnp.concatenate([fn(hd) for hd in range(HG_HEADS)], axis=1)


def _hg_blocks(q, kk, gcum):
    rowi = lax.broadcasted_iota(jnp.int32, gcum.shape, 0)
    out = []
    for blk in range(CHUNK // SUB):
        lo, hi = blk * SUB, (blk + 1) * SUB
        gb = gcum[lo - 1:lo, :] if blk else jnp.zeros_like(gcum[0:1, :])
        eq = jnp.exp(gcum[lo:hi, :] - gb)
        ek = jnp.exp(jnp.where(rowi < hi, gb - gcum, 0.0))
        out.append((eq, ek, (q[lo:hi, :] * eq).astype(BF16), (kk * ek).astype(BF16)))
    return out


def _hg_scores(blocks):
    out = []
    for hd in range(HG_HEADS):
        a = jnp.concatenate([_dot_nt(_head(qb, hd), _head(kb, hd)) for _, _, qb, kb in blocks], axis=0)
        out.append(jnp.where(_tri(True), a, 0.0))
    return out


HG_STEP_CHUNKS = 4


def _hgrn_fwd(proj, lbp):
    T = proj.shape[0]
    nc = T // CHUNK
    ns = min(HG_STEP_CHUNKS, nc)
    rows = ns * CHUNK

    def body(hq_ref, hf_ref, hi_ref, lbp_ref, o_ref, st_ref, state):
        @pl.when(pl.program_id(0) == 0)
        def _():
            state[...] = jnp.zeros_like(state)

        lb = _lower_bound(lbp_ref[...])
        work = []
        for c in range(ns):
            r = slice(c * CHUNK, (c + 1) * CHUNK)
            _, _, _, q, kk, gcum = _hg_gates(hq_ref[r, :], hf_ref[r, :], lb)
            vb = hi_ref[r, :].astype(BF16)
            a = _hg_scores(_hg_blocks(q, kk, gcum))
            gend = gcum[CHUNK - 1:CHUNK, :]
            qgb = (q * jnp.exp(gcum)).astype(BF16)
            kgeb = (kk * jnp.exp(gend - gcum)).astype(BF16)
            intra = [_dot(a[hd].astype(BF16), _head(vb, hd)) for hd in range(HG_HEADS)]
            update = [_dot_tn(_head(vb, hd), _head(kgeb, hd)) for hd in range(HG_HEADS)]
            work.append((qgb, jnp.exp(gend), intra, update))
        for hd in range(HG_HEADS):
            st = state[hd]
            for c, (qgb, egend, intra, update) in enumerate(work):
                st_ref[c, hd] = st
                o_ref[c * CHUNK:(c + 1) * CHUNK, hd * HG_DIM:(hd + 1) * HG_DIM] = (
                    intra[hd] + _dot_nt(_head(qgb, hd), st.astype(BF16)))
                st = st * _head(egend, hd) + update[hd]
            state[hd] = st

    def col(cb):
        return pl.BlockSpec((rows, HG_WIDTH), lambda i: (i, cb))

    return pl.pallas_call(
        body, name="hgrn_fwd", grid=(nc // ns,),
        in_specs=[col(P_HQ // HG_WIDTH), col(P_HF // HG_WIDTH), col(P_HI // HG_WIDTH), _full((2, HG_WIDTH))],
        out_specs=[pl.BlockSpec((rows, HG_WIDTH), lambda i: (i, 0)),
                   pl.BlockSpec((ns, HG_HEADS, HG_DIM, HG_DIM), lambda i: (i, 0, 0, 0))],
        out_shape=[jax.ShapeDtypeStruct((T, HG_WIDTH), F32),
                   jax.ShapeDtypeStruct((nc, HG_HEADS, HG_DIM, HG_DIM), F32)],
        scratch_shapes=[pltpu.VMEM((HG_HEADS, HG_DIM, HG_DIM), F32)],
        compiler_params=_params(1),
    )(proj, proj, proj, lbp)


def _hgrn_bwd(proj, lbp, do_hg, states):
    T = proj.shape[0]
    nc = T // CHUNK
    ns = min(HG_STEP_CHUNKS, nc)
    rows = ns * CHUNK
    steps = nc // ns

    def body(hq_ref, hf_ref, hi_ref, lbp_ref, do_ref, st_ref, dhq_ref, dhf_ref, dhi_ref, dlb_ref, dstate):
        @pl.when(pl.program_id(0) == 0)
        def _():
            dstate[...] = jnp.zeros_like(dstate)
            dlb_ref[...] = jnp.zeros_like(dlb_ref)

        lb = _lower_bound(lbp_ref[...])

        dst_all = [dstate[hd] for hd in range(HG_HEADS)]
        dlb = jnp.zeros_like(lb)
        last = lax.broadcasted_iota(jnp.int32, (CHUNK, HG_WIDTH), 0) == CHUNK - 1
        for c in reversed(range(ns)):
            r = slice(c * CHUNK, (c + 1) * CHUNK)
            hq = hq_ref[r, :]
            sq, sf, f, q, kk, gcum = _hg_gates(hq, hf_ref[r, :], lb)
            vb = hi_ref[r, :].astype(BF16)
            dob = do_ref[r, :].astype(BF16)
            blocks = _hg_blocks(q, kk, gcum)
            a = _hg_scores(blocks)
            gend = gcum[CHUNK - 1:CHUNK, :]
            eg, egend, ekend = jnp.exp(gcum), jnp.exp(gend), jnp.exp(gend - gcum)
            qg, kge = q * eg, kk * ekend
            qgb, kgeb = qg.astype(BF16), kge.astype(BF16)

            dv, dqg, dkge, st_dst, dq_blk, dk_blk = [], [], [], [], [], []
            for hd in range(HG_HEADS):
                st = st_ref[c, hd]
                dst = dst_all[hd]
                dstb = dst.astype(BF16)
                do_h, v_h = _head(dob, hd), _head(vb, hd)
                dv.append(_dot_tn(a[hd].astype(BF16), do_h) + _dot_nt(_head(kgeb, hd), dstb))
                da = jnp.where(_tri(True), _dot_nt(do_h, v_h), 0.0).astype(BF16)
                dqg.append(_dot(do_h, st.astype(BF16)))
                dkge.append(_dot(v_h, dstb))
                st_dst.append(jnp.sum(st * dst, axis=0, keepdims=True))
                dst_all[hd] = _dot_tn(do_h, _head(qgb, hd)) + dst * _head(egend, hd)
                dq_blk.append([_dot(da[b * SUB:(b + 1) * SUB, :], _head(kb, hd)) for b, (_, _, _, kb) in enumerate(blocks)])
                dk_blk.append([_dot_tn(da[b * SUB:(b + 1) * SUB, :], _head(qb, hd)) for b, (_, _, qb, _) in enumerate(blocks)])
            dv, dqg, dkge, st_dst = (jnp.concatenate(t, axis=1) for t in (dv, dqg, dkge, st_dst))

            dq_a, dg_q = [], []
            dk_a, dg_k = jnp.zeros_like(gcum), jnp.zeros_like(gcum)
            for b, (eq, ek, qb, kb) in enumerate(blocks):
                dq_b = _all_heads(lambda hd: dq_blk[hd][b])
                dk_b = _all_heads(lambda hd: dk_blk[hd][b])
                dq_a.append(dq_b * eq)
                dk_a = dk_a + dk_b * ek
                dg_q.append(qb.astype(F32) * dq_b)
                dg_k = dg_k + kb.astype(F32) * dk_b
            dq_a = jnp.concatenate(dq_a, axis=0)

            dgend = st_dst * egend + jnp.sum(dkge * kge, axis=0, keepdims=True)
            dq = dq_a + dqg * eg
            dk = dk_a + dkge * ekend
            dgc = jnp.concatenate(dg_q, axis=0) - dg_k + dqg * qg - dkge * kge + jnp.where(last, dgend, 0.0)
            dg = jnp.dot(_tri(False).astype(F32), dgc, precision=lax.Precision.HIGHEST, preferred_element_type=F32)
            df = dg / f - dk
            dhf_ref[r, :] = df * (1.0 - lb) * sf * (1.0 - sf)
            dlb = dlb + jnp.sum(df * (1.0 - sf), axis=0, keepdims=True)
            dhq_ref[r, :] = dq * (sq * (1.0 + hq * (1.0 - sq)))
            dhi_ref[r, :] = dv
        for hd in range(HG_HEADS):
            dstate[hd] = dst_all[hd]
        dlb_ref[...] += dlb

    def col(cb):
        return pl.BlockSpec((rows, HG_WIDTH), lambda i: (steps - 1 - i, cb))

    grad = jax.ShapeDtypeStruct((T, HG_WIDTH), F32)
    return pl.pallas_call(
        body, name="hgrn_bwd", grid=(steps,),
        in_specs=[col(P_HQ // HG_WIDTH), col(P_HF // HG_WIDTH), col(P_HI // HG_WIDTH), _full((2, HG_WIDTH)),
                  col(0), pl.BlockSpec((ns, HG_HEADS, HG_DIM, HG_DIM), lambda i: (steps - 1 - i, 0, 0, 0))],
        out_specs=[col(0), col(0), col(0), _full((1, HG_WIDTH))],
        out_shape=[grad, grad, grad, jax.ShapeDtypeStruct((1, HG_WIDTH), F32)],
        scratch_shapes=[pltpu.VMEM((HG_HEADS, HG_DIM, HG_DIM), F32)],
        compiler_params=_params(1),
    )(proj, proj, proj, lbp, do_hg, states)


def _top(x, tgt, o_mla, o_hg, proj, w_out, hg_norm_g, final_g, tm):
    T = x.shape[0]

    def body(x_ref, tgt_ref, om_ref, oh_ref, gm_ref, gh_ref, wout_ref, hgn_ref, fng_ref,
             dx2_ref, dom_ref, dsum_ref, dgm_ref, doh_ref, dgh_ref, loss_ref, dfng_ref, dhgn_ref, dwout_ref, ycat_ref):
        @pl.when(pl.program_id(0) == 0)
        def _():
            for ref in (loss_ref, dfng_ref, dhgn_ref, dwout_ref):
                ref[...] = jnp.zeros_like(ref)

        gm, om = gm_ref[...], om_ref[...]
        sgm = _sigmoid(gm)
        silu_m = gm * sgm
        gh, oh, gam = gh_ref[...], oh_ref[...], hgn_ref[...]
        sgh = _sigmoid(gh)
        silu_h = gh * sgh
        rr, nn = [], []
        for hd in range(HG_HEADS):
            oh_h = oh[:, hd * HG_DIM:(hd + 1) * HG_DIM]
            r_h = lax.rsqrt(jnp.mean(oh_h * oh_h, axis=-1, keepdims=True) + EPS)
            rr.append(r_h)
            nn.append(oh_h * r_h)
        n = jnp.concatenate(nn, axis=1)
        ng = n * gam
        ycat_ref[:, :MLA_WIDTH] = (om * silu_m).astype(BF16)
        ycat_ref[:, MLA_WIDTH:] = (ng * silu_h).astype(BF16)
        wout = wout_ref[...]
        x2 = x_ref[...] + _dot(ycat_ref[...], wout)
        r = lax.rsqrt(jnp.mean(x2 * x2, axis=-1, keepdims=True) + EPS)
        xh = x2 * r
        fng = fng_ref[...]
        err = xh * fng - tgt_ref[...]
        loss_ref[...] += 0.5 * jnp.sum(jnp.mean(err * err, axis=-1, keepdims=True), axis=0, keepdims=True)
        dout = err * (1.0 / D_MODEL)
        dfng_ref[...] += jnp.sum(dout * xh, axis=0, keepdims=True)
        dxh = dout * fng
        dx2 = r * (dxh - xh * jnp.mean(dxh * xh, axis=-1, keepdims=True))
        dx2_ref[...] = dx2
        dx2b = dx2.astype(BF16)
        dwout_ref[...] += _dot_tn(ycat_ref[...], dx2b)
        dycat = _dot_nt(dx2b, wout)
        dym, dyh = dycat[:, :MLA_WIDTH], dycat[:, MLA_WIDTH:]
        dom = dym * silu_m
        first = lax.broadcasted_iota(jnp.int32, (tm, LANES), 1) < V_DIM
        for pp in range(N_HEADS // 2):
            pair = dom[:, pp * LANES:(pp + 1) * LANES]
            dom_ref[:, 2 * pp * HEAD_LANES:(2 * pp + 1) * HEAD_LANES] = jnp.where(first, pair, 0.0).astype(BF16)
            dom_ref[:, (2 * pp + 1) * HEAD_LANES:(2 * pp + 2) * HEAD_LANES] = jnp.where(first, 0.0, pair).astype(BF16)
        head_of = lax.broadcasted_iota(jnp.int32, (MLA_WIDTH, LANES), 0) // V_DIM
        pick = (head_of == lax.broadcasted_iota(jnp.int32, (MLA_WIDTH, LANES), 1)).astype(F32)
        dsum_ref[...] = jnp.dot(dom * om, pick, precision=lax.Precision.HIGHEST, preferred_element_type=F32)
        dgm_ref[...] = dym * om * (sgm * (1.0 + gm * (1.0 - sgm)))
        dgh_ref[...] = dyh * ng * (sgh * (1.0 + gh * (1.0 - sgh)))
        dng = dyh * silu_h
        dhgn_ref[...] += jnp.sum(dng * n, axis=0, keepdims=True)
        dn = dng * gam
        for hd in range(HG_HEADS):
            sl = slice(hd * HG_DIM, (hd + 1) * HG_DIM)
            dn_h, n_h = dn[:, sl], nn[hd]
            doh_ref[:, sl] = rr[hd] * (dn_h - n_h * jnp.mean(dn_h * n_h, axis=-1, keepdims=True))

    def row(w, cb=0):
        return pl.BlockSpec((tm, w), lambda i: (i, cb))

    outs = [(D_MODEL, F32), (N_HEADS * HEAD_LANES, BF16), (LANES, F32), (MLA_WIDTH, F32), (HG_WIDTH, F32), (HG_WIDTH, F32)]
    small = [(1, SMALL_W), (1, D_MODEL), (1, HG_WIDTH), (D_MODEL, D_MODEL)]
    return pl.pallas_call(
        body, name="top", grid=(T // tm,),
        in_specs=[row(D_MODEL), row(D_MODEL), row(MLA_WIDTH), row(HG_WIDTH),
                  row(MLA_WIDTH, P_GM // MLA_WIDTH), row(HG_WIDTH, P_GH // HG_WIDTH),
                  _full((D_MODEL, D_MODEL)), _full((1, HG_WIDTH)), _full((1, D_MODEL))],
        out_specs=[row(w) for w, _ in outs] + [_full(s) for s in small],
        out_shape=[jax.ShapeDtypeStruct((T, w), dt) for w, dt in outs] + [jax.ShapeDtypeStruct(s, F32) for s in small],
        scratch_shapes=[pltpu.VMEM((tm, D_MODEL), BF16)],
        compiler_params=_params(1),
    )(x, tgt, o_mla, o_hg, proj, proj, w_out, hg_norm_g, final_g)


def _bot(x, dx2, proj, qn, kvn, dq, dk, dv, dgm, dhq, dhf, dhi, dgh, c_t, s1_t, s2_t, w_in_p, w_q_p, w_kv_p, ln_g, q_g, kv_g, tm):
    T = x.shape[0]
    lat_w = D_PERM - P_QL

    def body(x_ref, dx2_ref, lat_ref, qn_ref, kvn_ref, dq_ref, dk_ref, dv_ref, dgm_ref, dhq_ref, dhf_ref, dhi_ref,
             dgh_ref, c_ref, s1_ref, s2_ref, win_ref, wq_ref, wkv_ref, lng_ref, qg_ref, kvg_ref,
             dx_ref, dproj_ref, dlng_ref, dqg_ref, dkvg_ref, dwq_ref, dwkv_ref, dqpre_ref, dkv_ref):
        @pl.when(pl.program_id(0) == 0)
        def _():
            for ref in (dlng_ref, dqg_ref, dkvg_ref, dwq_ref, dwkv_ref):
                ref[...] = jnp.zeros_like(ref)

        c, s1, s2 = c_ref[...], s1_ref[...], s2_ref[...]
        dkpe = jnp.zeros((tm, LANES), F32)
        for hd in range(N_HEADS):
            sl = slice(hd * HEAD_LANES, (hd + 1) * HEAD_LANES)
            dqpre_ref[:, sl] = _rope_bwd(dq_ref[:, sl], c, s1, s2).astype(BF16)
            dk_h = dk_ref[:, sl]
            dkpe = dkpe + dk_h
            dkv_ref[:, sl] = dk_h.astype(BF16)
        dkv_ref[:, N_HEADS * HEAD_LANES:] = dv_ref[...].astype(BF16)
        lane = lax.broadcasted_iota(jnp.int32, (tm, LANES), 1)
        rope_lanes = jnp.logical_and(lane >= ROPE_LO, lane < ROPE_LO + ROPE)
        dkr = jnp.where(rope_lanes, _rope_bwd(dkpe, c, s1, s2), 0.0)

        def norm_bwd(v, g, dy):
            r = lax.rsqrt(jnp.mean(v * v, axis=-1, keepdims=True) + EPS)
            vh = v * r
            dvh = dy * g
            return jnp.sum(dy * vh, axis=0, keepdims=True), r * (dvh - vh * jnp.mean(dvh * vh, axis=-1, keepdims=True))

        dwq_ref[...] += _dot_tn(qn_ref[...], dqpre_ref[...])
        dwkv_ref[...] += _dot_tn(kvn_ref[...], dkv_ref[...])
        dqn = _dot_nt(dqpre_ref[...], wq_ref[...])
        dg_q, dql = norm_bwd(lat_ref[:, :Q_RANK], qg_ref[...], dqn)
        dqg_ref[...] += dg_q
        dkn = _dot_nt(dkv_ref[...], wkv_ref[...])
        dg_kv, dkvl = norm_bwd(lat_ref[:, Q_RANK:Q_RANK + KV_RANK], kvg_ref[...], dkn)
        dkvg_ref[...] += dg_kv

        dproj_ref[:, P_GM:P_GM + MLA_WIDTH] = dgm_ref[...].astype(BF16)
        dproj_ref[:, P_HQ:P_HQ + HG_WIDTH] = dhq_ref[...].astype(BF16)
        dproj_ref[:, P_HF:P_HF + HG_WIDTH] = dhf_ref[...].astype(BF16)
        dproj_ref[:, P_HI:P_HI + HG_WIDTH] = dhi_ref[...].astype(BF16)
        dproj_ref[:, P_GH:P_GH + HG_WIDTH] = dgh_ref[...].astype(BF16)
        dproj_ref[:, P_QL:P_QL + Q_RANK] = dql.astype(BF16)
        dproj_ref[:, P_KVL:P_KVL + KV_RANK] = dkvl.astype(BF16)
        dproj_ref[:, P_KR:P_KR + LANES] = dkr.astype(BF16)
        dh = _dot_nt(dproj_ref[...], win_ref[...])
        dg_ln, dxn = norm_bwd(x_ref[...], lng_ref[...], dh)
        dlng_ref[...] += dg_ln
        dx_ref[...] = dx2_ref[...] + dxn

    def row(w, cb=0):
        return pl.BlockSpec((tm, w), lambda i: (i, cb))

    hl = N_HEADS * HEAD_LANES
    outs = [(D_MODEL, F32), (D_PERM, BF16)]
    small = [(1, D_MODEL), (1, Q_RANK), (1, KV_RANK), (Q_RANK, hl), (KV_RANK, hl + MLA_WIDTH)]
    return pl.pallas_call(
        body, name="bot", grid=(T // tm,),
        in_specs=[row(D_MODEL), row(D_MODEL), row(lat_w, P_QL // lat_w), row(Q_RANK), row(KV_RANK),
                  row(hl), row(hl), row(MLA_WIDTH),
                  row(MLA_WIDTH), row(HG_WIDTH), row(HG_WIDTH), row(HG_WIDTH), row(HG_WIDTH),
                  row(LANES), row(LANES), row(LANES),
                  _full((D_MODEL, D_PERM)), _full((Q_RANK, hl)), _full((KV_RANK, hl + MLA_WIDTH)),
                  _full((1, D_MODEL)), _full((1, Q_RANK)), _full((1, KV_RANK))],
        out_specs=[row(w) for w, _ in outs] + [_full(s) for s in small],
        out_shape=[jax.ShapeDtypeStruct((T, w), dt) for w, dt in outs] + [jax.ShapeDtypeStruct(s, F32) for s in small],
        scratch_shapes=[pltpu.VMEM((tm, hl), BF16), pltpu.VMEM((tm, hl + MLA_WIDTH), BF16)],
        compiler_params=_params(1),
    )(x, dx2, proj, qn, kvn, dq, dk, dv, dgm, dhq, dhf, dhi, dgh, c_t, s1_t, s2_t, w_in_p, w_q_p, w_kv_p, ln_g, q_g,
      kv_g)


def _matmul_tn(a, b, bn, bt, name):
    T, M = a.shape
    N = b.shape[1]

    def body(a_ref, b_ref, o_ref):
        @pl.when(pl.program_id(1) == 0)
        def _():
            o_ref[...] = jnp.zeros_like(o_ref)

        o_ref[...] += _dot_tn(a_ref[...], b_ref[...])

    return pl.pallas_call(
        body, name=name, grid=(N // bn, T // bt),
        in_specs=[pl.BlockSpec((bt, M), lambda n, t: (t, 0)), pl.BlockSpec((bt, bn), lambda n, t: (t, n))],
        out_specs=pl.BlockSpec((M, bn), lambda n, t: (0, n)),
        out_shape=jax.ShapeDtypeStruct((M, N), F32),
        compiler_params=_params(2),
    )(a, b)


RS_ROWS = 256


def _reduce_scatter(slabs, small):
    n = len(slabs)
    units = [(a, r0, min(s.shape[1], RS_ROWS)) for a, s in enumerate(slabs) for r0 in range(0, s.shape[1], RS_ROWS)]
    nu = len(units)

    def body(*refs):
        ins, small_ref = refs[:n], refs[n]
        outs, small_out = refs[n + 1:2 * n + 1], refs[2 * n + 1]
        own, sib_land, ici_out, ici_land = (refs[(2 + g) * n + 2:(3 + g) * n + 2] for g in range(4))
        small_land = refs[6 * n + 2]
        loc_sems, d2d_send, d2d_recv, ici_send, ici_recv, sm_send, sm_recv = refs[6 * n + 3:6 * n + 10]
        x, y, c = lax.axis_index("x"), lax.axis_index("y"), lax.axis_index("c")
        me = 4 * x + 2 * y + c

        def chip(k):
            return (1 - x if k & 2 else x, 1 - y if k & 1 else y)

        def block(k, core):
            px, py = chip(k)
            return 4 * px + 2 * py + core

        def part(u):
            a, r0, nr = units[u]
            return a, pl.ds(r0, nr)

        def local(u, k):
            a, rows = part(u)
            return pltpu.make_async_copy(ins[a].at[block(k, c), rows, :], own[a].at[k, rows, :], loc_sems.at[u, k])

        def to_sibling(u, k):
            a, rows = part(u)
            return pltpu.make_async_remote_copy(
                src_ref=ins[a].at[block(k, 1 - c), rows, :], dst_ref=sib_land[a].at[k, rows, :],
                send_sem=d2d_send.at[u, k], recv_sem=d2d_recv.at[u, k], device_id=(x, y, 1 - c), device_id_type=MESH)

        def to_chip(u, k):
            a, rows = part(u)
            return pltpu.make_async_remote_copy(
                src_ref=ici_out[a].at[k - 1, rows, :], dst_ref=ici_land[a].at[k - 1, rows, :],
                send_sem=ici_send.at[u, k - 1], recv_sem=ici_recv.at[u, k - 1], device_id=(*chip(k), c),
                device_id_type=MESH)

        def small_copy(k, receiving):
            px, py = chip(k >> 1)
            pc = 1 - c if k & 1 else c
            slot = 4 * px + 2 * py + pc if receiving else me
            return pltpu.make_async_remote_copy(
                src_ref=small_ref, dst_ref=small_land.at[slot], send_sem=sm_send.at[k - 1], recv_sem=sm_recv.at[k - 1],
                device_id=(px, py, pc), device_id_type=MESH)

        for u in range(nu):
            for k in range(4):
                local(u, k).start()
        for u in range(nu):
            for k in range(4):
                to_sibling(u, k).start()
        small_land[me] = small_ref[...]
        for k in range(1, N_DEV):
            small_copy(k, False).start()
        for u in range(nu):
            a, rows = part(u)
            for k in range(4):
                local(u, k).wait()
                to_sibling(u, k).wait_recv()
            for k in range(1, 4):
                ici_out[a][k - 1, rows, :] = (own[a][k, rows, :] + sib_land[a][k, rows, :]).astype(BF16)
                to_chip(u, k).start()
        for u in range(nu):
            a, rows = part(u)
            acc = own[a][0, rows, :] + sib_land[a][0, rows, :]
            for k in range(1, 4):
                to_chip(u, k).wait_recv()
                acc = acc + ici_land[a][k - 1, rows, :].astype(F32)
            outs[a][rows, :] = acc
        for k in range(1, N_DEV):
            small_copy(k, True).wait_recv()
        acc = small_land[0]
        for d in range(1, N_DEV):
            acc = acc + small_land[d]
        small_out[...] = acc
        for u in range(nu):
            for k in range(4):
                to_sibling(u, k).wait_send()
            for k in range(1, 4):
                to_chip(u, k).wait_send()
        for k in range(1, N_DEV):
            small_copy(k, False).wait_send()

    vm = pl.BlockSpec(memory_space=pltpu.VMEM)
    hbm = pl.BlockSpec(memory_space=pl.ANY)
    dma = pltpu.SemaphoreType.DMA
    return pl.pallas_call(
        body, name="reduce_scatter_grads",
        in_specs=[hbm] * n + [vm], out_specs=[vm] * (n + 1),
        out_shape=[jax.ShapeDtypeStruct(s.shape[1:], F32) for s in slabs] + [jax.ShapeDtypeStruct(small.shape, F32)],
        scratch_shapes=[pltpu.VMEM((4,) + s.shape[1:], F32) for s in slabs] * 2
        + [pltpu.VMEM((3,) + s.shape[1:], BF16) for s in slabs] * 2
        + [pltpu.VMEM((N_DEV,) + small.shape, F32)]
        + [dma((nu, 4)), dma((nu, 4)), dma((nu, 4)), dma((nu, 3)), dma((nu, 3)), dma((N_DEV - 1,)), dma((N_DEV - 1,))],
        compiler_params=pltpu.CompilerParams(vmem_limit_bytes=VMEM_LIMIT),
    )(*slabs, small)


def _adamw_math(w, g, m, v):
    m = ADAM_B1 * m + (1.0 - ADAM_B1) * g
    v = ADAM_B2 * v + (1.0 - ADAM_B2) * (g * g)
    m_hat = m / (1.0 - ADAM_B1 ** ADAM_STEP)
    v_hat = v / (1.0 - ADAM_B2 ** ADAM_STEP)
    delta = -ADAM_LR * (m_hat / (jnp.sqrt(v_hat) + ADAM_EPS) + ADAM_WD * w)
    return delta, m, v


SMALL_W = 512


def _adamw(big, small_w, small_g):
    nb, ns = len(big), len(small_w)

    def body(*refs):
        k = 0
        big_in = [refs[4 * i:4 * i + 4] for i in range(nb)]
        k = 4 * nb
        small_in = [refs[k + 3 * i:k + 3 * i + 3] for i in range(ns)]
        k += 3 * ns
        sg_ref = refs[k]
        k += 1
        big_out = [refs[k + 3 * i:k + 3 * i + 3] for i in range(nb)]
        k += 3 * nb
        small_out = [refs[k + 4 * i:k + 4 * i + 4] for i in range(ns)]

        for (w, g, m, v), (od, om, ov) in zip(big_in, big_out):
            od[...], om[...], ov[...] = _adamw_math(w[...], g[...], m[...], v[...])

        sg = sg_ref[...]
        lbp = small_in[2][0][...]
        lb = _lower_bound(lbp)
        t = sg[4:5, :] * lb * (1.0 - lb)
        grads = [jnp.concatenate([sg[0:1, :], sg[1:2, :]], axis=1),
                 jnp.concatenate([sg[2:3, :], sg[3:4, :]], axis=1),
                 jnp.concatenate([t, -t], axis=0),
                 sg[6:7, :], sg[7:8, 0:Q_RANK], sg[7:8, Q_RANK:Q_RANK + KV_RANK]]
        for (w, m, v), g, (og, od, om, ov) in zip(small_in, grads, small_out):
            og[...] = g
            od[...], om[...], ov[...] = _adamw_math(w[...], g, m[...], v[...])

    ins = [a for grp in big for a in grp] + [a for grp in small_w for a in grp] + [small_g]
    out_shape = ([jax.ShapeDtypeStruct(grp[0].shape, F32) for grp in big for _ in range(3)]
                 + [jax.ShapeDtypeStruct(grp[0].shape, F32) for grp in small_w for _ in range(4)])
    vm = pl.BlockSpec(memory_space=pltpu.VMEM)
    res = pl.pallas_call(
        body, name="adamw", in_specs=[vm] * len(ins), out_specs=[vm] * len(out_shape), out_shape=out_shape,
        compiler_params=pltpu.CompilerParams(vmem_limit_bytes=VMEM_LIMIT),
    )(*ins)
    big_res = [res[3 * i:3 * i + 3] for i in range(nb)]
    small_res = [res[3 * nb + 4 * i:3 * nb + 4 * i + 4] for i in range(ns)]
    return big_res, small_res


def _perm_weights(g_in, g_q, g_kv, g_out):
    w = g_in.transpose(1, 0, 2).reshape(D_MODEL, D_IN)
    z = lambda n: jnp.zeros((D_MODEL, n), BF16)
    w_in_p = jnp.concatenate([w[:, 416:], w[:, :384], z(64), w[:, 384:416], z(32)], axis=1)
    wq = g_q.transpose(1, 0, 2)
    w_q_p = jnp.pad(wq, ((0, 0), (0, 0), (0, HEAD_LANES - NOPE - ROPE))).reshape(Q_RANK, N_HEADS * HEAD_LANES)
    wkv = g_kv.transpose(1, 0, 2)
    wk = jnp.pad(wkv[:, :, :NOPE], ((0, 0), (0, 0), (0, HEAD_LANES - NOPE))).reshape(KV_RANK, N_HEADS * HEAD_LANES)
    wv = wkv[:, :, NOPE:].reshape(KV_RANK, MLA_WIDTH)
    return w_in_p, w_q_p, jnp.concatenate([wk, wv], axis=1), g_out.reshape(D_MODEL, D_MODEL)


def _grad_slabs(dw_in_p, dw_q_p, dw_kv_p, dw_out):
    dw_in = jnp.concatenate([dw_in_p[:, P_QL:P_KR], dw_in_p[:, P_KR + ROPE_LO:P_KR + ROPE_LO + ROPE], dw_in_p[:, :P_QL]], axis=1)
    s_in = dw_in.reshape(D_MODEL, N_DEV, D_IN // N_DEV).transpose(1, 0, 2)
    s_q = dw_q_p.reshape(Q_RANK, N_HEADS, HEAD_LANES)[:, :, :NOPE + ROPE].transpose(1, 0, 2)
    hl = N_HEADS * HEAD_LANES
    dk = dw_kv_p[:, :hl].reshape(KV_RANK, N_HEADS, HEAD_LANES)[:, :, :NOPE]
    dv = dw_kv_p[:, hl:].reshape(KV_RANK, N_HEADS, V_DIM)
    s_kv = jnp.concatenate([dk, dv], axis=2).transpose(1, 0, 2)
    return s_in, s_q, s_kv, dw_out.reshape(N_DEV, D_MODEL // N_DEV, D_MODEL)


def _block_sizes(T):
    return min(256, T), min(256, T), min(512, T)


def kernel(x, positions, ln_g, w_in, q_a_norm_g, w_q_b, kv_a_norm_g, w_kv_b, hg_lower_bounds, hg_norm_g, w_out, final_norm_g, loss_target, m_ln_g, m_w_in, m_q_a_norm_g, m_w_q_b, m_kv_a_norm_g, m_w_kv_b, m_hg_lower_bounds, m_hg_norm_g, m_w_out, m_final_norm_g, v_ln_g, v_w_in, v_q_a_norm_g, v_w_q_b, v_kv_a_norm_g, v_w_kv_b, v_hg_lower_bounds, v_hg_norm_g, v_w_out, v_final_norm_g):
    T = x.shape[1]
    tm, tq, bt = _block_sizes(T)
    nq = T // tq
    xs, tgt = x[0], loss_target[0]
    pos_f = positions.astype(F32)
    fng = final_norm_g.reshape(1, D_MODEL)

    gathered = _all_gather_weights([w_in[0], w_q_b[0], w_kv_b[0], w_out[0]])
    w_in_p, w_q_p, w_kv_p, w_out_b = _perm_weights(*gathered)

    c_t, s1_t, s2_t = _rope_tables(pos_f, bt)
    proj, h, qn, kvn, q, k, v = _fwd_in(xs, ln_g, w_in_p, q_a_norm_g, w_q_p, kv_a_norm_g, w_kv_p, c_t, s1_t, s2_t, bt)
    hl = N_HEADS * HEAD_LANES
    v_t = v.reshape(nq, tq, MLA_WIDTH).transpose(0, 2, 1)
    k_t = k.reshape(nq, tq, hl).transpose(0, 2, 1)
    q_t = q.reshape(nq, tq, hl).transpose(0, 2, 1)
    o_mla, lse = _attn_fwd_flat(k, q_t, v_t, tq)
    o_hg, states = _hgrn_fwd(proj, hg_lower_bounds)
    dx2, d_om, dsum, d_gm, d_oh, d_gh, loss_p, d_fng, d_hgn, dw_out = _top(
        xs, tgt, o_mla, o_hg, proj, w_out_b, hg_norm_g, fng, tm)
    dsum = dsum[:, :N_HEADS].T.reshape(N_HEADS, nq, 1, tq)
    do_t = d_om.reshape(nq, tq, hl).transpose(0, 2, 1)
    dq_t, dk, dv = _attn_bwd_flat(k, v, q_t, k_t, do_t, lse, dsum, tq)
    dq = dq_t.transpose(0, 2, 1).reshape(T, N_HEADS * HEAD_LANES)
    d_hq, d_hf, d_hi, d_lb = _hgrn_bwd(proj, hg_lower_bounds, d_oh, states)
    dx, dproj, d_lng, d_qg, d_kvg, dw_q_p, dw_kv_p = _bot(
        xs, dx2, proj, qn, kvn, dq, dk, dv, d_gm, d_hq, d_hf, d_hi, d_gh, c_t, s1_t, s2_t, w_in_p, w_q_p, w_kv_p,
        ln_g, q_a_norm_g, kv_a_norm_g, tm)
    dw_in_p = _matmul_tn(h, dproj, 512, bt, "dw_in")

    small = jnp.concatenate([
        d_lng.reshape(2, SMALL_W), d_fng.reshape(2, SMALL_W), d_lb, loss_p, d_hgn,
        jnp.concatenate([d_qg, d_kvg, jnp.zeros((1, SMALL_W - Q_RANK - KV_RANK), F32)], axis=1)], axis=0)
    g_in, g_q, g_kv, g_out, small_sum = _reduce_scatter(list(_grad_slabs(dw_in_p, dw_q_p, dw_kv_p, dw_out)), small)

    big = [(w_in[0], g_in, m_w_in[0], v_w_in[0]), (w_q_b[0], g_q, m_w_q_b[0], v_w_q_b[0]),
           (w_kv_b[0], g_kv, m_w_kv_b[0], v_w_kv_b[0]), (w_out[0], g_out, m_w_out[0], v_w_out[0])]
    small_w = [(ln_g, m_ln_g, v_ln_g),
               (fng, m_final_norm_g.reshape(1, D_MODEL), v_final_norm_g.reshape(1, D_MODEL)),
               (hg_lower_bounds, m_hg_lower_bounds, v_hg_lower_bounds), (hg_norm_g, m_hg_norm_g, v_hg_norm_g),
               (q_a_norm_g, m_q_a_norm_g, v_q_a_norm_g), (kv_a_norm_g, m_kv_a_norm_g, v_kv_a_norm_g)]
    big_res, small_res = _adamw(big, small_w, small_sum)

    loss = small_sum[5, 0]
    (r_in, r_q, r_kv, r_out) = big_res
    (s_ln, s_fn, s_lb, s_hgn, s_qg, s_kvg) = small_res
    flat = lambda t: t.reshape(D_MODEL)
    lead = lambda t: t[None]
    grads = [s_ln[0], lead(g_in), s_qg[0], lead(g_q), s_kvg[0], lead(g_kv), s_lb[0], s_hgn[0], lead(g_out), flat(s_fn[0])]

    def pick(i):
        return [s_ln[i + 1], lead(r_in[i]), s_qg[i + 1], lead(r_q[i]), s_kvg[i + 1], lead(r_kv[i]), s_lb[i + 1],
                s_hgn[i + 1], lead(r_out[i]), flat(s_fn[i + 1])]

    return (loss, dx[None], *grads, *pick(0), *pick(1), *pick(2))
```

```python
import math

import numpy as np
import jax
import jax.numpy as jnp
from jax import lax
from jax.experimental import pallas as pl
from jax.experimental.pallas import tpu as pltpu

F32 = jnp.float32
BF16 = jnp.bfloat16

D_MODEL = 1024
N_HEADS = 8
NOPE = 64
ROPE = 32
HALF_ROPE = ROPE // 2
V_DIM = 64
Q_RANK = 256
KV_RANK = 128
MLA_WIDTH = N_HEADS * V_DIM
HG_HEADS = 4
HG_DIM = 128
HG_WIDTH = HG_HEADS * HG_DIM
CHUNK = 64
SUB = 16
D_IN = 2976
D_PERM = 3072
ROPE_THETA = 10000.0
EPS = 1e-6
N_DEV = 8
LANES = 128
HEAD_LANES = 128

P_GM, P_HQ, P_HF, P_HI, P_GH, P_QL, P_KVL, P_KR = 0, 512, 1024, 1536, 2048, 2560, 2816, 2944
R_QL, R_KVL, R_KR, R_MAIN = (0, 256), (256, 384), (384, 416), (416, 2976)
ROPE_LO = NOPE
SCALE = 1.0 / math.sqrt(NOPE + ROPE)

ADAM_LR = 0.001
ADAM_B1 = 0.9
ADAM_B2 = 0.999
ADAM_EPS = 1e-08
ADAM_WD = 0.01
ADAM_STEP = 10

VMEM_LIMIT = 56 * 1024 * 1024
MESH = pl.DeviceIdType.MESH

NT = (((1,), (1,)), ((), ()))
TN = (((0,), (0,)), ((), ()))


def _params(n_grid=0, **kw):
    sem = ("arbitrary",) * n_grid if n_grid else None
    return pltpu.CompilerParams(dimension_semantics=sem, vmem_limit_bytes=VMEM_LIMIT, **kw)


def _dot(a, b):
    return jnp.dot(a, b, preferred_element_type=F32)


def _dot_nt(a, b):
    return lax.dot_general(a, b, NT, preferred_element_type=F32)


def _dot_tn(a, b):
    return lax.dot_general(a, b, TN, preferred_element_type=F32)


def _sigmoid(x):
    return 1.0 / (1.0 + jnp.exp(-x))


def _rope_fwd(x, c, s1, s2):
    return x * c + pltpu.roll(x, LANES - HALF_ROPE, 1) * s1 + pltpu.roll(x, HALF_ROPE, 1) * s2


def _rope_bwd(dy, c, s1, s2):
    return dy * c - pltpu.roll(dy, LANES - HALF_ROPE, 1) * s1 - pltpu.roll(dy, HALF_ROPE, 1) * s2


def _in_proj_rows(wt_ref):
    kr = wt_ref[R_KR[0]:R_KR[1], :]
    pad = lambda n: jnp.zeros((n, D_MODEL), kr.dtype)
    return ((P_GM, wt_ref[R_MAIN[0]:R_MAIN[1], :]), (P_QL, wt_ref[R_QL[0]:R_QL[1], :]),
            (P_KVL, wt_ref[R_KVL[0]:R_KVL[1], :]),
            (P_KR, jnp.concatenate([pad(ROPE_LO), kr, pad(LANES - ROPE_LO - ROPE)], axis=0)))


def _full(shape):
    n = len(shape)
    return pl.BlockSpec(shape, lambda *_: (0,) * n)


def _rope_tables(pos_f, tm):
    T = pos_f.shape[1]
    inv = (np.float32(ROPE_THETA) ** (-np.arange(HALF_ROPE, dtype=np.float32) / np.float32(HALF_ROPE))).astype(np.float32)
    place = np.zeros((3, HALF_ROPE, LANES), np.float32)
    for i in range(HALF_ROPE):
        place[0, i, ROPE_LO + i] = place[0, i, ROPE_LO + HALF_ROPE + i] = 1.0
        place[1, i, ROPE_LO + i] = -1.0
        place[2, i, ROPE_LO + HALF_ROPE + i] = 1.0
    base = np.ones((1, LANES), np.float32)
    base[0, ROPE_LO:ROPE_LO + ROPE] = 0.0

    def body(pos_ref, inv_ref, place_ref, base_ref, c_ref, s1_ref, s2_ref):
        ang = inv_ref[...] * pos_ref[...]
        cos, sin = jnp.cos(ang), jnp.sin(ang)

        def put(v, k):
            return lax.dot_general(v, place_ref[k], TN, precision=lax.Precision.HIGHEST, preferred_element_type=F32)

        c_ref[...] = put(cos, 0) + base_ref[...]
        s1_ref[...] = put(sin, 1)
        s2_ref[...] = put(sin, 2)

    tab = jax.ShapeDtypeStruct((T, LANES), F32)
    return pl.pallas_call(
        body, name="rope_tables", grid=(T // tm,),
        in_specs=[pl.BlockSpec((1, tm), lambda i: (0, i)), _full((HALF_ROPE, 1)), _full((3, HALF_ROPE, LANES)),
                  _full((1, LANES))],
        out_specs=[pl.BlockSpec((tm, LANES), lambda i: (i, 0))] * 3,
        out_shape=[tab, tab, tab], compiler_params=_params(1),
    )(pos_f, jnp.asarray(inv.reshape(HALF_ROPE, 1)), jnp.asarray(place), jnp.asarray(base))


def _all_gather_weights(shards):
    n = len(shards)

    def body(*refs):
        ins, outs = refs[:n], refs[n:2 * n]
        send_sems, recv_sems = refs[2 * n], refs[2 * n + 1]
        x, y, c = lax.axis_index("x"), lax.axis_index("y"), lax.axis_index("c")
        me, sibling = (x, y, c), (x, y, 1 - c)
        chips = [(1 - x, y), (x, 1 - y), (1 - x, 1 - y)]

        def idx(d):
            return 4 * d[0] + 2 * d[1] + d[2]

        def copy(a, k, block, to):
            rows = outs[a].at[idx(block)]
            return pltpu.make_async_remote_copy(src_ref=rows, dst_ref=rows, send_sem=send_sems.at[a, k],
                                                recv_sem=recv_sems.at[a, k], device_id=to, device_id_type=MESH)

        for a in range(n):
            outs[a][idx(me)] = ins[a][...].astype(BF16)
        first = []
        for a in range(n):
            first.append(copy(a, 0, me, sibling))
            first += [copy(a, 1 + j, me, (*chip, c)) for j, chip in enumerate(chips)]
        for cp in first:
            cp.start()
        passed = []
        for j, chip in enumerate(chips):
            for a in range(n):
                copy(a, 1 + j, (*chip, c), me).wait_recv()
                cp = copy(a, 4 + j, (*chip, c), sibling)
                cp.start()
                passed.append(cp)
        for a in range(n):
            copy(a, 0, sibling, me).wait_recv()
            for j, chip in enumerate(chips):
                copy(a, 4 + j, (*chip, 1 - c), me).wait_recv()
        for cp in first + passed:
            cp.wait_send()

    vm = pl.BlockSpec(memory_space=pltpu.VMEM)
    return pl.pallas_call(
        body, name="all_gather_weights",
        in_specs=[vm] * n, out_specs=[vm] * n,
        out_shape=[jax.ShapeDtypeStruct((N_DEV,) + s.shape, BF16) for s in shards],
        scratch_shapes=[pltpu.SemaphoreType.DMA((n, 7)), pltpu.SemaphoreType.DMA((n, 7))],
        compiler_params=pltpu.CompilerParams(vmem_limit_bytes=VMEM_LIMIT),
    )(*shards)


def _fwd_in(x, ln_g, w_in_t, q_g, w_q_p, kv_g, w_kv_p, c_t, s1_t, s2_t, tm):
    T = x.shape[0]

    def body(x_ref, lng_ref, win_ref, qg_ref, wq_ref, kvg_ref, wkv_ref, c_ref, s1_ref, s2_ref,
             proj_ref, h_ref, qn_ref, kvn_ref, q_ref, k_ref, v_ref):
        xv = x_ref[...]
        r = lax.rsqrt(jnp.mean(xv * xv, axis=-1, keepdims=True) + EPS)
        h = (xv * r * lng_ref[...]).astype(BF16)
        h_ref[...] = h
        for col, rows in _in_proj_rows(win_ref):
            proj_ref[:, col:col + rows.shape[0]] = _dot_nt(h, rows)
        c, s1, s2 = c_ref[...], s1_ref[...], s2_ref[...]

        ql = proj_ref[:, P_QL:P_QL + Q_RANK]
        rq = lax.rsqrt(jnp.mean(ql * ql, axis=-1, keepdims=True) + EPS)
        qn = (ql * rq * qg_ref[...]).astype(BF16)
        qn_ref[...] = qn
        q = _dot(qn, wq_ref[...])
        for hd in range(N_HEADS):
            sl = slice(hd * HEAD_LANES, (hd + 1) * HEAD_LANES)
            q_ref[:, sl] = _rope_fwd(q[:, sl], c, s1, s2).astype(BF16)

        kvl = proj_ref[:, P_KVL:P_KVL + KV_RANK]
        rk = lax.rsqrt(jnp.mean(kvl * kvl, axis=-1, keepdims=True) + EPS)
        kvn = (kvl * rk * kvg_ref[...]).astype(BF16)
        kvn_ref[...] = kvn
        kv = _dot(kvn, wkv_ref[...])
        kpe = _rope_fwd(proj_ref[:, P_KR:P_KR + LANES], c, s1, s2)
        for hd in range(N_HEADS):
            sl = slice(hd * HEAD_LANES, (hd + 1) * HEAD_LANES)
            k_ref[:, sl] = (kv[:, sl] + kpe).astype(BF16)
        v_ref[...] = kv[:, N_HEADS * HEAD_LANES:].astype(BF16)

    def row(w):
        return pl.BlockSpec((tm, w), lambda i: (i, 0))

    outs = [(D_PERM, F32), (D_MODEL, BF16), (Q_RANK, BF16), (KV_RANK, BF16),
            (N_HEADS * HEAD_LANES, BF16), (N_HEADS * HEAD_LANES, BF16), (MLA_WIDTH, BF16)]
    return pl.pallas_call(
        body, name="fwd_in", grid=(T // tm,),
        in_specs=[row(D_MODEL), _full((1, D_MODEL)), _full((D_IN, D_MODEL)), _full((1, Q_RANK)),
                  _full((Q_RANK, N_HEADS * HEAD_LANES)), _full((1, KV_RANK)),
                  _full((KV_RANK, N_HEADS * HEAD_LANES + MLA_WIDTH)), row(LANES), row(LANES), row(LANES)],
        out_specs=[row(w) for w, _ in outs],
        out_shape=[jax.ShapeDtypeStruct((T, w), dt) for w, dt in outs],
        compiler_params=_params(1),
    )(x, ln_g, w_in_t, q_g, w_q_p, kv_g, w_kv_p, c_t, s1_t, s2_t)


LOG2E = 1.4426950408889634
SCALE2 = SCALE * LOG2E


def _causal(tq):
    r = lax.broadcasted_iota(jnp.int32, (tq, tq), 0)
    c = lax.broadcasted_iota(jnp.int32, (tq, tq), 1)
    return r <= c


MASKED = -1e30


def _causal_bias(bias_ref, tq):
    bias_ref[0] = jnp.zeros((tq, tq), F32)
    bias_ref[1] = jnp.where(_causal(tq), 0.0, MASKED)


def _tile_tables(nq, by_query):
    if by_query:
        pairs = [(j, i) for i in range(nq) for j in range(i + 1)]
    else:
        pairs = [(j, i) for j in range(nq) for i in range(nq - 1, j - 1, -1)]
    pairs.append(pairs[-1])
    jj, ii = np.array(pairs, np.int32).T
    return jnp.asarray(jj), jnp.asarray(ii), len(pairs) - 1


ATTN_TRIP = 8


def _walk_tiles(n, products, tile, flush, buf_a, buf_b):
    bufs = (buf_a, buf_b)
    products(0, buf_a)

    def trip(r, carry):
        for u in range(ATTN_TRIP):
            products(ATTN_TRIP * r + u + 1, bufs[(u + 1) % 2])
            tile(ATTN_TRIP * r + u, bufs[u % 2])
        for u in range(ATTN_TRIP):
            flush(ATTN_TRIP * r + u)
        return carry

    lax.fori_loop(0, n // ATTN_TRIP, trip, 0)
    rest = n - n % ATTN_TRIP
    for u in range(n % ATTN_TRIP):
        if rest + u + 1 < n:
            products(rest + u + 1, bufs[(u + 1) % 2])
        tile(rest + u, bufs[u % 2])
    for u in range(n % ATTN_TRIP):
        flush(rest + u)


def _attn_fwd_flat(k, q_t, v_t, tq):
    T = k.shape[0]
    nq = T // tq
    jj, ii, n = _tile_tables(nq, True)
    heads = [slice(hh * HEAD_LANES, (hh + 1) * HEAD_LANES) for hh in range(2)]

    def body(jj_ref, ii_ref, k_ref, qt_ref, vt_ref, o_ref, lse_ref, sa_ref, sb_ref, m_ref, l_ref, acc_ref, bias_ref):
        def reset(st):
            m_ref[st] = jnp.full(m_ref.shape[1:], MASKED, F32)
            l_ref[st] = jnp.zeros(l_ref.shape[1:], F32)
            acc_ref[st] = jnp.zeros(acc_ref.shape[1:], F32)

        _causal_bias(bias_ref, tq)
        for st in range(ATTN_TRIP):
            reset(st)

        def products(t, buf):
            j, i = jj_ref[t], ii_ref[t]
            kj = k_ref[pl.ds(pl.multiple_of(j * tq, tq), tq), :]
            for hh, sl in enumerate(heads):
                buf[hh] = _dot(kj[:, sl], qt_ref[i, sl, :])

        def tile(t, buf):
            j, i = jj_ref[t], ii_ref[t]
            vt = vt_ref[j]
            bias = bias_ref.at[(j == i).astype(jnp.int32)]
            st = i % ATTN_TRIP
            for hh in range(2):
                s = buf[hh] * SCALE2 + bias[...]
                m = m_ref[st, hh]
                m_new = jnp.maximum(m, jnp.max(s, axis=0, keepdims=True))
                alpha = jnp.exp2(m - m_new)
                p = jnp.exp2(s - m_new)
                m_ref[st, hh] = m_new
                l_ref[st, hh] = alpha * l_ref[st, hh] + jnp.sum(p, axis=0, keepdims=True)
                acc_ref[st, hh] = alpha * acc_ref[st, hh] + _dot(vt, p.astype(BF16))

        def flush(t):
            j, i = jj_ref[t], ii_ref[t]

            @pl.when(j == i)
            def _():
                st = i % ATTN_TRIP
                first = lax.broadcasted_iota(jnp.int32, (LANES, tq), 0) < V_DIM
                out = jnp.where(first, acc_ref[st, 0] / l_ref[st, 0], acc_ref[st, 1] / l_ref[st, 1])
                o_ref[pl.ds(pl.multiple_of(i * tq, tq), tq), :] = out.T
                for hh in range(2):
                    lse_ref[hh, i] = m_ref[st, hh] + jnp.log2(l_ref[st, hh])
                reset(st)

        _walk_tiles(n, products, tile, flush, sa_ref, sb_ref)

    smem = pl.BlockSpec(memory_space=pltpu.SMEM)
    return pl.pallas_call(
        body, name="attn_fwd", grid=(N_HEADS // 2,),
        in_specs=[smem, smem,
                  pl.BlockSpec((T, 2 * HEAD_LANES), lambda p: (0, p)),
                  pl.BlockSpec((nq, 2 * HEAD_LANES, tq), lambda p: (0, p, 0)),
                  pl.BlockSpec((nq, LANES, tq), lambda p: (0, p, 0))],
        out_specs=[pl.BlockSpec((T, LANES), lambda p: (0, p)),
                   pl.BlockSpec((2, nq, 1, tq), lambda p: (p, 0, 0, 0))],
        out_shape=[jax.ShapeDtypeStruct((T, MLA_WIDTH), F32), jax.ShapeDtypeStruct((N_HEADS, nq, 1, tq), F32)],
        scratch_shapes=[pltpu.VMEM((2, tq, tq), F32), pltpu.VMEM((2, tq, tq), F32),
                        pltpu.VMEM((ATTN_TRIP, 2, 1, tq), F32), pltpu.VMEM((ATTN_TRIP, 2, 1, tq), F32),
                        pltpu.VMEM((ATTN_TRIP, 2, LANES, tq), F32), pltpu.VMEM((2, tq, tq), F32)],
        compiler_params=_params(1),
    )(jj, ii, k, q_t, v_t)


def _attn_bwd_flat(k, v, q_t, k_t, do_t, lse, dsum, tq):
    T = k.shape[0]
    nq = T // tq
    jj, ii, n = _tile_tables(nq, False)
    heads = [slice(hh * HEAD_LANES, (hh + 1) * HEAD_LANES) for hh in range(2)]

    def body(jj_ref, ii_ref, k_ref, v_ref, qt_ref, kt_ref, dot_ref, lse_ref, dsum_ref, dqt_ref, dk_ref, dv_ref,
             ba_ref, bb_ref, dkt_ref, dvt_ref, bias_ref):
        _causal_bias(bias_ref, tq)
        dqt_ref[...] = jnp.zeros_like(dqt_ref)
        dkt_ref[...] = jnp.zeros_like(dkt_ref)
        dvt_ref[...] = jnp.zeros_like(dvt_ref)

        def products(t, buf):
            j, i = jj_ref[t], ii_ref[t]
            rows = pl.ds(pl.multiple_of(j * tq, tq), tq)
            for hh, sl in enumerate(heads):
                buf[hh] = _dot(k_ref[rows, sl], qt_ref[i, sl, :])
                buf[2 + hh] = _dot(v_ref[rows, :], dot_ref[i, sl, :])

        def tile(t, buf):
            j, i = jj_ref[t], ii_ref[t]
            bias = bias_ref.at[(j == i).astype(jnp.int32)]
            st = j % ATTN_TRIP
            dv_new = None
            for hh, sl in enumerate(heads):
                p = jnp.exp2(buf[hh] * SCALE2 + bias[...] - lse_ref[hh, i])
                ds = (p * (buf[2 + hh] - dsum_ref[hh, i]) * SCALE).astype(BF16)
                dv_h = _dot_nt(dot_ref[i, sl, :], p.astype(BF16))
                dv_new = dv_h if dv_new is None else dv_new + dv_h
                dkt_ref[st, sl, :] += _dot_nt(qt_ref[i, sl, :], ds)
                dqt_ref[i, sl, :] += _dot(kt_ref[j, sl, :], ds)
            dvt_ref[st] += dv_new

        def flush(t):
            j, i = jj_ref[t], ii_ref[t]

            @pl.when(j == i)
            def _():
                st = j % ATTN_TRIP
                rows = pl.ds(pl.multiple_of(j * tq, tq), tq)
                dk_ref[rows, :] = dkt_ref[st].T
                dv_ref[rows, :] = dvt_ref[st].T
                dkt_ref[st] = jnp.zeros(dkt_ref.shape[1:], F32)
                dvt_ref[st] = jnp.zeros(dvt_ref.shape[1:], F32)

        _walk_tiles(n, products, tile, flush, ba_ref, bb_ref)

    smem = pl.BlockSpec(memory_space=pltpu.SMEM)
    stat = pl.BlockSpec((2, nq, 1, tq), lambda p: (p, 0, 0, 0))
    blocks_t = pl.BlockSpec((nq, 2 * HEAD_LANES, tq), lambda p: (0, p, 0))
    return pl.pallas_call(
        body, name="attn_bwd", grid=(N_HEADS // 2,),
        in_specs=[smem, smem,
                  pl.BlockSpec((T, 2 * HEAD_LANES), lambda p: (0, p)),
                  pl.BlockSpec((T, LANES), lambda p: (0, p)),
                  blocks_t, blocks_t, blocks_t, stat, stat],
        out_specs=[blocks_t,
                   pl.BlockSpec((T, 2 * HEAD_LANES), lambda p: (0, p)),
                   pl.BlockSpec((T, LANES), lambda p: (0, p))],
        out_shape=[jax.ShapeDtypeStruct((nq, N_HEADS * HEAD_LANES, tq), F32),
                   jax.ShapeDtypeStruct((T, N_HEADS * HEAD_LANES), F32),
                   jax.ShapeDtypeStruct((T, MLA_WIDTH), F32)],
        scratch_shapes=[pltpu.VMEM((4, tq, tq), F32), pltpu.VMEM((4, tq, tq), F32),
                        pltpu.VMEM((ATTN_TRIP, 2 * HEAD_LANES, tq), F32), pltpu.VMEM((ATTN_TRIP, LANES, tq), F32),
                        pltpu.VMEM((2, tq, tq), F32)],
        compiler_params=_params(1),
    )(jj, ii, k, v, q_t, k_t, do_t, lse, dsum)


def _lower_bound(lbp):
    a, b = lbp[0:1, :], lbp[1:2, :]
    mx = jnp.maximum(a, b)
    ea, eb = jnp.exp(a - mx), jnp.exp(b - mx)
    return ea / (ea + eb)


def _tri(lower):
    r = lax.broadcasted_iota(jnp.int32, (CHUNK, CHUNK), 0)
    c = lax.broadcasted_iota(jnp.int32, (CHUNK, CHUNK), 1)
    return (c <= r) if lower else (c >= r)


def _hg_gates(hq, hf, lb):
    sq = _sigmoid(hq)
    sf = _sigmoid(hf)
    f = lb + (1.0 - lb) * sf
    g = jnp.log(f)
    gcum = jnp.dot(_tri(True).astype(F32), g, precision=lax.Precision.HIGHEST, preferred_element_type=F32)
    return sq, sf, f, hq * sq, 1.0 - f, gcum


def _head(x, hd):
    return x[:, hd * HG_DIM:(hd + 1) * HG_DIM]


def _all_heads(fn):
    return jnp.concatenate([fn(hd) for hd in range(HG_HEADS)], axis=1)


def _hg_blocks(q, kk, gcum):
    rowi = lax.broadcasted_iota(jnp.int32, gcum.shape, 0)
    out = []
    for blk in range(CHUNK // SUB):
        lo, hi = blk * SUB, (blk + 1) * SUB
        gb = gcum[lo - 1:lo, :] if blk else jnp.zeros_like(gcum[0:1, :])
        eq = jnp.exp(gcum[lo:hi, :] - gb)
        ek = jnp.exp(jnp.where(rowi < hi, gb - gcum, 0.0))
        out.append((eq, ek, (q[lo:hi, :] * eq).astype(BF16), (kk * ek).astype(BF16)))
    return out


def _hg_scores(blocks):
    out = []
    for hd in range(HG_HEADS):
        a = jnp.concatenate([_dot_nt(_head(qb, hd), _head(kb, hd)) for _, _, qb, kb in blocks], axis=0)
        out.append(jnp.where(_tri(True), a, 0.0))
    return out


HG_STEP_CHUNKS = 4


def _hgrn_fwd(proj, lbp):
    T = proj.shape[0]
    nc = T // CHUNK
    ns = min(HG_STEP_CHUNKS, nc)
    rows = ns * CHUNK

    def body(hq_ref, hf_ref, hi_ref, lbp_ref, o_ref, st_ref, state):
        @pl.when(pl.program_id(0) == 0)
        def _():
            state[...] = jnp.zeros_like(state)

        lb = _lower_bound(lbp_ref[...])
        work = []
        for c in range(ns):
            r = slice(c * CHUNK, (c + 1) * CHUNK)
            _, _, _, q, kk, gcum = _hg_gates(hq_ref[r, :], hf_ref[r, :], lb)
            vb = hi_ref[r, :].astype(BF16)
            a = _hg_scores(_hg_blocks(q, kk, gcum))
            gend = gcum[CHUNK - 1:CHUNK, :]
            qgb = (q * jnp.exp(gcum)).astype(BF16)
            kgeb = (kk * jnp.exp(gend - gcum)).astype(BF16)
            intra = [_dot(a[hd].astype(BF16), _head(vb, hd)) for hd in range(HG_HEADS)]
            update = [_dot_tn(_head(vb, hd), _head(kgeb, hd)) for hd in range(HG_HEADS)]
            work.append((qgb, jnp.exp(gend), intra, update))
        for hd in range(HG_HEADS):
            st = state[hd]
            for c, (qgb, egend, intra, update) in enumerate(work):
                st_ref[c, hd] = st
                o_ref[c * CHUNK:(c + 1) * CHUNK, hd * HG_DIM:(hd + 1) * HG_DIM] = (
                    intra[hd] + _dot_nt(_head(qgb, hd), st.astype(BF16)))
                st = st * _head(egend, hd) + update[hd]
            state[hd] = st

    def col(cb):
        return pl.BlockSpec((rows, HG_WIDTH), lambda i: (i, cb))

    return pl.pallas_call(
        body, name="hgrn_fwd", grid=(nc // ns,),
        in_specs=[col(P_HQ // HG_WIDTH), col(P_HF // HG_WIDTH), col(P_HI // HG_WIDTH), _full((2, HG_WIDTH))],
        out_specs=[pl.BlockSpec((rows, HG_WIDTH), lambda i: (i, 0)),
                   pl.BlockSpec((ns, HG_HEADS, HG_DIM, HG_DIM), lambda i: (i, 0, 0, 0))],
        out_shape=[jax.ShapeDtypeStruct((T, HG_WIDTH), F32),
                   jax.ShapeDtypeStruct((nc, HG_HEADS, HG_DIM, HG_DIM), F32)],
        scratch_shapes=[pltpu.VMEM((HG_HEADS, HG_DIM, HG_DIM), F32)],
        compiler_params=_params(1),
    )(proj, proj, proj, lbp)


def _hgrn_bwd(proj, lbp, do_hg, states):
    T = proj.shape[0]
    nc = T // CHUNK
    ns = min(HG_STEP_CHUNKS, nc)
    rows = ns * CHUNK
    steps = nc // ns

    def body(hq_ref, hf_ref, hi_ref, lbp_ref, do_ref, st_ref, dhq_ref, dhf_ref, dhi_ref, dlb_ref, dstate):
        @pl.when(pl.program_id(0) == 0)
        def _():
            dstate[...] = jnp.zeros_like(dstate)
            dlb_ref[...] = jnp.zeros_like(dlb_ref)

        lb = _lower_bound(lbp_ref[...])

        dst_all = [dstate[hd] for hd in range(HG_HEADS)]
        dlb = jnp.zeros_like(lb)
        last = lax.broadcasted_iota(jnp.int32, (CHUNK, HG_WIDTH), 0) == CHUNK - 1
        for c in reversed(range(ns)):
            r = slice(c * CHUNK, (c + 1) * CHUNK)
            hq = hq_ref[r, :]
            sq, sf, f, q, kk, gcum = _hg_gates(hq, hf_ref[r, :], lb)
            vb = hi_ref[r, :].astype(BF16)
            dob = do_ref[r, :].astype(BF16)
            blocks = _hg_blocks(q, kk, gcum)
            a = _hg_scores(blocks)
            gend = gcum[CHUNK - 1:CHUNK, :]
            eg, egend, ekend = jnp.exp(gcum), jnp.exp(gend), jnp.exp(gend - gcum)
            qg, kge = q * eg, kk * ekend
            qgb, kgeb = qg.astype(BF16), kge.astype(BF16)

            dv, dqg, dkge, st_dst, dq_blk, dk_blk = [], [], [], [], [], []
            for hd in range(HG_HEADS):
                st = st_ref[c, hd]
                dst = dst_all[hd]
                dstb = dst.astype(BF16)
                do_h, v_h = _head(dob, hd), _head(vb, hd)
                dv.append(_dot_tn(a[hd].astype(BF16), do_h) + _dot_nt(_head(kgeb, hd), dstb))
                da = jnp.where(_tri(True), _dot_nt(do_h, v_h), 0.0).astype(BF16)
                dqg.append(_dot(do_h, st.astype(BF16)))
                dkge.append(_dot(v_h, dstb))
                st_dst.append(jnp.sum(st * dst, axis=0, keepdims=True))
                dst_all[hd] = _dot_tn(do_h, _head(qgb, hd)) + dst * _head(egend, hd)
                dq_blk.append([_dot(da[b * SUB:(b + 1) * SUB, :], _head(kb, hd)) for b, (_, _, _, kb) in enumerate(blocks)])
                dk_blk.append([_dot_tn(da[b * SUB:(b + 1) * SUB, :], _head(qb, hd)) for b, (_, _, qb, _) in enumerate(blocks)])
            dv, dqg, dkge, st_dst = (jnp.concatenate(t, axis=1) for t in (dv, dqg, dkge, st_dst))

            dq_a, dg_q = [], []
            dk_a, dg_k = jnp.zeros_like(gcum), jnp.zeros_like(gcum)
            for b, (eq, ek, qb, kb) in enumerate(blocks):
                dq_b = _all_heads(lambda hd: dq_blk[hd][b])
                dk_b = _all_heads(lambda hd: dk_blk[hd][b])
                dq_a.append(dq_b * eq)
                dk_a = dk_a + dk_b * ek
                dg_q.append(qb.astype(F32) * dq_b)
                dg_k = dg_k + kb.astype(F32) * dk_b
            dq_a = jnp.concatenate(dq_a, axis=0)

            dgend = st_dst * egend + jnp.sum(dkge * kge, axis=0, keepdims=True)
            dq = dq_a + dqg * eg
            dk = dk_a + dkge * ekend
            dgc = jnp.concatenate(dg_q, axis=0) - dg_k + dqg * qg - dkge * kge + jnp.where(last, dgend, 0.0)
            dg = jnp.dot(_tri(False).astype(F32), dgc, precision=lax.Precision.HIGHEST, preferred_element_type=F32)
            df = dg / f - dk
            dhf_ref[r, :] = df * (1.0 - lb) * sf * (1.0 - sf)
            dlb = dlb + jnp.sum(df * (1.0 - sf), axis=0, keepdims=True)
            dhq_ref[r, :] = dq * (sq * (1.0 + hq * (1.0 - sq)))
            dhi_ref[r, :] = dv
        for hd in range(HG_HEADS):
            dstate[hd] = dst_all[hd]
        dlb_ref[...] += dlb

    def col(cb):
        return pl.BlockSpec((rows, HG_WIDTH), lambda i: (steps - 1 - i, cb))

    grad = jax.ShapeDtypeStruct((T, HG_WIDTH), F32)
    return pl.pallas_call(
        body, name="hgrn_bwd", grid=(steps,),
        in_specs=[col(P_HQ // HG_WIDTH), col(P_HF // HG_WIDTH), col(P_HI // HG_WIDTH), _full((2, HG_WIDTH)),
                  col(0), pl.BlockSpec((ns, HG_HEADS, HG_DIM, HG_DIM), lambda i: (steps - 1 - i, 0, 0, 0))],
        out_specs=[col(0), col(0), col(0), _full((1, HG_WIDTH))],
        out_shape=[grad, grad, grad, jax.ShapeDtypeStruct((1, HG_WIDTH), F32)],
        scratch_shapes=[pltpu.VMEM((HG_HEADS, HG_DIM, HG_DIM), F32)],
        compiler_params=_params(1),
    )(proj, proj, proj, lbp, do_hg, states)


def _top(x, tgt, o_mla, o_hg, proj, w_out, hg_norm_g, final_g, tm):
    T = x.shape[0]

    def body(x_ref, tgt_ref, om_ref, oh_ref, gm_ref, gh_ref, wout_ref, hgn_ref, fng_ref,
             dx2_ref, dom_ref, dsum_ref, dgm_ref, doh_ref, dgh_ref, loss_ref, dfng_ref, dhgn_ref, dwout_ref, ycat_ref):
        @pl.when(pl.program_id(0) == 0)
        def _():
            for ref in (loss_ref, dfng_ref, dhgn_ref, dwout_ref):
                ref[...] = jnp.zeros_like(ref)

        gm, om = gm_ref[...], om_ref[...]
        sgm = _sigmoid(gm)
        silu_m = gm * sgm
        gh, oh, gam = gh_ref[...], oh_ref[...], hgn_ref[...]
        sgh = _sigmoid(gh)
        silu_h = gh * sgh
        rr, nn = [], []
        for hd in range(HG_HEADS):
            oh_h = oh[:, hd * HG_DIM:(hd + 1) * HG_DIM]
            r_h = lax.rsqrt(jnp.mean(oh_h * oh_h, axis=-1, keepdims=True) + EPS)
            rr.append(r_h)
            nn.append(oh_h * r_h)
        n = jnp.concatenate(nn, axis=1)
        ng = n * gam
        ycat_ref[:, :MLA_WIDTH] = (om * silu_m).astype(BF16)
        ycat_ref[:, MLA_WIDTH:] = (ng * silu_h).astype(BF16)
        wout = wout_ref[...]
        x2 = x_ref[...] + _dot(ycat_ref[...], wout)
        r = lax.rsqrt(jnp.mean(x2 * x2, axis=-1, keepdims=True) + EPS)
        xh = x2 * r
        fng = fng_ref[...]
        err = xh * fng - tgt_ref[...]
        loss_ref[...] += 0.5 * jnp.sum(jnp.mean(err * err, axis=-1, keepdims=True), axis=0, keepdims=True)
        dout = err * (1.0 / D_MODEL)
        dfng_ref[...] += jnp.sum(dout * xh, axis=0, keepdims=True)
        dxh = dout * fng
        dx2 = r * (dxh - xh * jnp.mean(dxh * xh, axis=-1, keepdims=True))
        dx2_ref[...] = dx2
        dx2b = dx2.astype(BF16)
        dwout_ref[...] += _dot_tn(ycat_ref[...], dx2b)
        dycat = _dot_nt(dx2b, wout)
        dym, dyh = dycat[:, :MLA_WIDTH], dycat[:, MLA_WIDTH:]
        dom = dym * silu_m
        first = lax.broadcasted_iota(jnp.int32, (tm, LANES), 1) < V_DIM
        for pp in range(N_HEADS // 2):
            pair = dom[:, pp * LANES:(pp + 1) * LANES]
            dom_ref[:, 2 * pp * HEAD_LANES:(2 * pp + 1) * HEAD_LANES] = jnp.where(first, pair, 0.0).astype(BF16)
            dom_ref[:, (2 * pp + 1) * HEAD_LANES:(2 * pp + 2) * HEAD_LANES] = jnp.where(first, 0.0, pair).astype(BF16)
        head_of = lax.broadcasted_iota(jnp.int32, (MLA_WIDTH, LANES), 0) // V_DIM
        pick = (head_of == lax.broadcasted_iota(jnp.int32, (MLA_WIDTH, LANES), 1)).astype(F32)
        dsum_ref[...] = jnp.dot(dom * om, pick, precision=lax.Precision.HIGHEST, preferred_element_type=F32)
        dgm_ref[...] = dym * om * (sgm * (1.0 + gm * (1.0 - sgm)))
        dgh_ref[...] = dyh * ng * (sgh * (1.0 + gh * (1.0 - sgh)))
        dng = dyh * silu_h
        dhgn_ref[...] += jnp.sum(dng * n, axis=0, keepdims=True)
        dn = dng * gam
        for hd in range(HG_HEADS):
            sl = slice(hd * HG_DIM, (hd + 1) * HG_DIM)
            dn_h, n_h = dn[:, sl], nn[hd]
            doh_ref[:, sl] = rr[hd] * (dn_h - n_h * jnp.mean(dn_h * n_h, axis=-1, keepdims=True))

    def row(w, cb=0):
        return pl.BlockSpec((tm, w), lambda i: (i, cb))

    outs = [(D_MODEL, F32), (N_HEADS * HEAD_LANES, BF16), (LANES, F32), (MLA_WIDTH, F32), (HG_WIDTH, F32), (HG_WIDTH, F32)]
    small = [(1, SMALL_W), (1, D_MODEL), (1, HG_WIDTH), (D_MODEL, D_MODEL)]
    return pl.pallas_call(
        body, name="top", grid=(T // tm,),
        in_specs=[row(D_MODEL), row(D_MODEL), row(MLA_WIDTH), row(HG_WIDTH),
                  row(MLA_WIDTH, P_GM // MLA_WIDTH), row(HG_WIDTH, P_GH // HG_WIDTH),
                  _full((D_MODEL, D_MODEL)), _full((1, HG_WIDTH)), _full((1, D_MODEL))],
        out_specs=[row(w) for w, _ in outs] + [_full(s) for s in small],
        out_shape=[jax.ShapeDtypeStruct((T, w), dt) for w, dt in outs] + [jax.ShapeDtypeStruct(s, F32) for s in small],
        scratch_shapes=[pltpu.VMEM((tm, D_MODEL), BF16)],
        compiler_params=_params(1),
    )(x, tgt, o_mla, o_hg, proj, proj, w_out, hg_norm_g, final_g)


def _bot(x, dx2, proj, qn, kvn, dq, dk, dv, dgm, dhq, dhf, dhi, dgh, c_t, s1_t, s2_t, w_in_t, w_q_p, w_kv_p, ln_g, q_g, kv_g, tm):
    T = x.shape[0]
    lat_w = D_PERM - P_QL

    def body(x_ref, dx2_ref, lat_ref, qn_ref, kvn_ref, dq_ref, dk_ref, dv_ref, dgm_ref, dhq_ref, dhf_ref, dhi_ref,
             dgh_ref, c_ref, s1_ref, s2_ref, win_ref, wq_ref, wkv_ref, lng_ref, qg_ref, kvg_ref,
             dx_ref, dproj_ref, dlng_ref, dqg_ref, dkvg_ref, dwq_ref, dwkv_ref, dqpre_ref, dkv_ref):
        @pl.when(pl.program_id(0) == 0)
        def _():
            for ref in (dlng_ref, dqg_ref, dkvg_ref, dwq_ref, dwkv_ref):
                ref[...] = jnp.zeros_like(ref)

        c, s1, s2 = c_ref[...], s1_ref[...], s2_ref[...]
        dkpe = jnp.zeros((tm, LANES), F32)
        for hd in range(N_HEADS):
            sl = slice(hd * HEAD_LANES, (hd + 1) * HEAD_LANES)
            dqpre_ref[:, sl] = _rope_bwd(dq_ref[:, sl], c, s1, s2).astype(BF16)
            dk_h = dk_ref[:, sl]
            dkpe = dkpe + dk_h
            dkv_ref[:, sl] = dk_h.astype(BF16)
        dkv_ref[:, N_HEADS * HEAD_LANES:] = dv_ref[...].astype(BF16)
        lane = lax.broadcasted_iota(jnp.int32, (tm, LANES), 1)
        rope_lanes = jnp.logical_and(lane >= ROPE_LO, lane < ROPE_LO + ROPE)
        dkr = jnp.where(rope_lanes, _rope_bwd(dkpe, c, s1, s2), 0.0)

        def norm_bwd(v, g, dy):
            r = lax.rsqrt(jnp.mean(v * v, axis=-1, keepdims=True) + EPS)
            vh = v * r
            dvh = dy * g
            return jnp.sum(dy * vh, axis=0, keepdims=True), r * (dvh - vh * jnp.mean(dvh * vh, axis=-1, keepdims=True))

        dwq_ref[...] += _dot_tn(qn_ref[...], dqpre_ref[...])
        dwkv_ref[...] += _dot_tn(kvn_ref[...], dkv_ref[...])
        dqn = _dot_nt(dqpre_ref[...], wq_ref[...])
        dg_q, dql = norm_bwd(lat_ref[:, :Q_RANK], qg_ref[...], dqn)
        dqg_ref[...] += dg_q
        dkn = _dot_nt(dkv_ref[...], wkv_ref[...])
        dg_kv, dkvl = norm_bwd(lat_ref[:, Q_RANK:Q_RANK + KV_RANK], kvg_ref[...], dkn)
        dkvg_ref[...] += dg_kv

        dproj_ref[:, P_GM:P_GM + MLA_WIDTH] = dgm_ref[...].astype(BF16)
        dproj_ref[:, P_HQ:P_HQ + HG_WIDTH] = dhq_ref[...].astype(BF16)
        dproj_ref[:, P_HF:P_HF + HG_WIDTH] = dhf_ref[...].astype(BF16)
        dproj_ref[:, P_HI:P_HI + HG_WIDTH] = dhi_ref[...].astype(BF16)
        dproj_ref[:, P_GH:P_GH + HG_WIDTH] = dgh_ref[...].astype(BF16)
        dproj_ref[:, P_QL:P_QL + Q_RANK] = dql.astype(BF16)
        dproj_ref[:, P_KVL:P_KVL + KV_RANK] = dkvl.astype(BF16)
        dproj_ref[:, P_KR:P_KR + LANES] = dkr.astype(BF16)
        dh = sum(_dot(dproj_ref[:, col:col + rows.shape[0]], rows) for col, rows in _in_proj_rows(win_ref))
        dg_ln, dxn = norm_bwd(x_ref[...], lng_ref[...], dh)
        dlng_ref[...] += dg_ln
        dx_ref[...] = dx2_ref[...] + dxn

    def row(w, cb=0):
        return pl.BlockSpec((tm, w), lambda i: (i, cb))

    hl = N_HEADS * HEAD_LANES
    outs = [(D_MODEL, F32), (D_PERM, BF16)]
    small = [(1, D_MODEL), (1, Q_RANK), (1, KV_RANK), (Q_RANK, hl), (KV_RANK, hl + MLA_WIDTH)]
    return pl.pallas_call(
        body, name="bot", grid=(T // tm,),
        in_specs=[row(D_MODEL), row(D_MODEL), row(lat_w, P_QL // lat_w), row(Q_RANK), row(KV_RANK),
                  row(hl), row(hl), row(MLA_WIDTH),
                  row(MLA_WIDTH), row(HG_WIDTH), row(HG_WIDTH), row(HG_WIDTH), row(HG_WIDTH),
                  row(LANES), row(LANES), row(LANES),
                  _full((D_IN, D_MODEL)), _full((Q_RANK, hl)), _full((KV_RANK, hl + MLA_WIDTH)),
                  _full((1, D_MODEL)), _full((1, Q_RANK)), _full((1, KV_RANK))],
        out_specs=[row(w) for w, _ in outs] + [_full(s) for s in small],
        out_shape=[jax.ShapeDtypeStruct((T, w), dt) for w, dt in outs] + [jax.ShapeDtypeStruct(s, F32) for s in small],
        scratch_shapes=[pltpu.VMEM((tm, hl), BF16), pltpu.VMEM((tm, hl + MLA_WIDTH), BF16)],
        compiler_params=_params(1),
    )(x, dx2, proj, qn, kvn, dq, dk, dv, dgm, dhq, dhf, dhi, dgh, c_t, s1_t, s2_t, w_in_t, w_q_p, w_kv_p, ln_g, q_g,
      kv_g)


DW_ROWS = 512


def _dw_in_t(dproj, h, bt):
    T = h.shape[0]
    nb, nt = D_PERM // DW_ROWS, T // bt
    assert P_QL == (nb - 1) * DW_ROWS

    def body(a_ref, b_ref, o_ref, acc_ref, sem):
        blk, t = pl.program_id(0), pl.program_id(1)

        @pl.when(t == 0)
        def _():
            acc_ref[...] = jnp.zeros_like(acc_ref)

        acc_ref[...] += _dot_tn(a_ref[...], b_ref[...])

        def put(src, dst):
            cp = pltpu.make_async_copy(src, dst, sem)
            cp.start()
            cp.wait()

        @pl.when(jnp.logical_and(t == nt - 1, blk < nb - 1))
        def _():
            put(acc_ref, o_ref.at[pl.ds(pl.multiple_of(R_MAIN[0] + blk * DW_ROWS, 8), DW_ROWS), :])

        @pl.when(jnp.logical_and(t == nt - 1, blk == nb - 1))
        def _():
            lat = R_KVL[1] - R_QL[0]
            put(acc_ref.at[0:lat, :], o_ref.at[R_QL[0]:R_KVL[1], :])
            kr = P_KR - P_QL + ROPE_LO
            put(acc_ref.at[kr:kr + ROPE, :], o_ref.at[R_KR[0]:R_KR[1], :])

    return pl.pallas_call(
        body, name="dw_in", grid=(nb, nt),
        in_specs=[pl.BlockSpec((bt, DW_ROWS), lambda n, t: (t, n)), pl.BlockSpec((bt, D_MODEL), lambda n, t: (t, 0))],
        out_specs=pl.BlockSpec(memory_space=pl.ANY),
        out_shape=jax.ShapeDtypeStruct((D_IN, D_MODEL), F32),
        scratch_shapes=[pltpu.VMEM((DW_ROWS, D_MODEL), F32), pltpu.SemaphoreType.DMA],
        compiler_params=_params(2),
    )(dproj, h)


RS_ROWS = 256


def _reduce_scatter(slabs, small):
    n = len(slabs)
    units = []
    for a, s in enumerate(slabs):
        rows, cols = s.shape[1:]
        if rows % RS_ROWS == 0 or rows < RS_ROWS:
            units += [(a, (pl.ds(r0, min(rows, RS_ROWS)), slice(None))) for r0 in range(0, rows, RS_ROWS)]
        else:
            units += [(a, (slice(None), pl.ds(c0, RS_ROWS))) for c0 in range(0, cols, RS_ROWS)]
    nu = len(units)

    def body(*refs):
        ins, small_ref = refs[:n], refs[n]
        outs, small_out = refs[n + 1:2 * n + 1], refs[2 * n + 1]
        own, sib_land, ici_out, ici_land = (refs[(2 + g) * n + 2:(3 + g) * n + 2] for g in range(4))
        small_land = refs[6 * n + 2]
        loc_sems, d2d_send, d2d_recv, ici_send, ici_recv, sm_send, sm_recv = refs[6 * n + 3:6 * n + 10]
        x, y, c = lax.axis_index("x"), lax.axis_index("y"), lax.axis_index("c")
        me = 4 * x + 2 * y + c

        def chip(k):
            return (1 - x if k & 2 else x, 1 - y if k & 1 else y)

        def block(k, core):
            px, py = chip(k)
            return 4 * px + 2 * py + core

        def part(u):
            return units[u]

        def local(u, k):
            a, rows = part(u)
            return pltpu.make_async_copy(ins[a].at[(block(k, c),) + rows], own[a].at[(k,) + rows], loc_sems.at[u, k])

        def to_sibling(u, k):
            a, rows = part(u)
            return pltpu.make_async_remote_copy(
                src_ref=ins[a].at[(block(k, 1 - c),) + rows], dst_ref=sib_land[a].at[(k,) + rows],
                send_sem=d2d_send.at[u, k], recv_sem=d2d_recv.at[u, k], device_id=(x, y, 1 - c), device_id_type=MESH)

        def to_chip(u, k):
            a, rows = part(u)
            return pltpu.make_async_remote_copy(
                src_ref=ici_out[a].at[(k - 1,) + rows], dst_ref=ici_land[a].at[(k - 1,) + rows],
                send_sem=ici_send.at[u, k - 1], recv_sem=ici_recv.at[u, k - 1], device_id=(*chip(k), c),
                device_id_type=MESH)

        def small_copy(k, receiving):
            px, py = chip(k >> 1)
            pc = 1 - c if k & 1 else c
            slot = 4 * px + 2 * py + pc if receiving else me
            return pltpu.make_async_remote_copy(
                src_ref=small_ref, dst_ref=small_land.at[slot], send_sem=sm_send.at[k - 1], recv_sem=sm_recv.at[k - 1],
                device_id=(px, py, pc), device_id_type=MESH)

        for u in range(nu):
            for k in range(4):
                local(u, k).start()
        for u in range(nu):
            for k in range(4):
                to_sibling(u, k).start()
        small_land[me] = small_ref[...]
        for k in range(1, N_DEV):
            small_copy(k, False).start()
        for u in range(nu):
            a, rows = part(u)
            for k in range(4):
                local(u, k).wait()
                to_sibling(u, k).wait_recv()
            for k in range(1, 4):
                ici_out[a][(k - 1,) + rows] = (own[a][(k,) + rows] + sib_land[a][(k,) + rows]).astype(BF16)
                to_chip(u, k).start()
        for u in range(nu):
            a, rows = part(u)
            acc = own[a][(0,) + rows] + sib_land[a][(0,) + rows]
            for k in range(1, 4):
                to_chip(u, k).wait_recv()
                acc = acc + ici_land[a][(k - 1,) + rows].astype(F32)
            outs[a][rows] = acc
        for k in range(1, N_DEV):
            small_copy(k, True).wait_recv()
        acc = small_land[0]
        for d in range(1, N_DEV):
            acc = acc + small_land[d]
        small_out[...] = acc
        for u in range(nu):
            for k in range(4):
                to_sibling(u, k).wait_send()
            for k in range(1, 4):
                to_chip(u, k).wait_send()
        for k in range(1, N_DEV):
            small_copy(k, False).wait_send()

    vm = pl.BlockSpec(memory_space=pltpu.VMEM)
    hbm = pl.BlockSpec(memory_space=pl.ANY)
    dma = pltpu.SemaphoreType.DMA
    return pl.pallas_call(
        body, name="reduce_scatter_grads",
        in_specs=[hbm] * n + [vm], out_specs=[vm] * (n + 1),
        out_shape=[jax.ShapeDtypeStruct(s.shape[1:], F32) for s in slabs] + [jax.ShapeDtypeStruct(small.shape, F32)],
        scratch_shapes=[pltpu.VMEM((4,) + s.shape[1:], F32) for s in slabs] * 2
        + [pltpu.VMEM((3,) + s.shape[1:], BF16) for s in slabs] * 2
        + [pltpu.VMEM((N_DEV,) + small.shape, F32)]
        + [dma((nu, 4)), dma((nu, 4)), dma((nu, 4)), dma((nu, 3)), dma((nu, 3)), dma((N_DEV - 1,)), dma((N_DEV - 1,))],
        compiler_params=pltpu.CompilerParams(vmem_limit_bytes=VMEM_LIMIT),
    )(*slabs, small)


def _adamw_math(w, g, m, v):
    m = ADAM_B1 * m + (1.0 - ADAM_B1) * g
    v = ADAM_B2 * v + (1.0 - ADAM_B2) * (g * g)
    m_hat = m / (1.0 - ADAM_B1 ** ADAM_STEP)
    v_hat = v / (1.0 - ADAM_B2 ** ADAM_STEP)
    delta = -ADAM_LR * (m_hat / (jnp.sqrt(v_hat) + ADAM_EPS) + ADAM_WD * w)
    return delta, m, v


SMALL_W = 512


def _adamw(big, small_w, small_g):
    nb, ns = len(big), len(small_w)

    def body(*refs):
        k = 0
        big_in = [refs[4 * i:4 * i + 4] for i in range(nb)]
        k = 4 * nb
        small_in = [refs[k + 3 * i:k + 3 * i + 3] for i in range(ns)]
        k += 3 * ns
        sg_ref = refs[k]
        k += 1
        big_out = [refs[k + 3 * i:k + 3 * i + 3] for i in range(nb)]
        k += 3 * nb
        small_out = [refs[k + 4 * i:k + 4 * i + 4] for i in range(ns)]

        for (w, g, m, v), (od, om, ov) in zip(big_in, big_out):
            od[...], om[...], ov[...] = _adamw_math(w[...], g[...], m[...], v[...])

        sg = sg_ref[...]
        lbp = small_in[2][0][...]
        lb = _lower_bound(lbp)
        t = sg[4:5, :] * lb * (1.0 - lb)
        grads = [jnp.concatenate([sg[0:1, :], sg[1:2, :]], axis=1),
                 jnp.concatenate([sg[2:3, :], sg[3:4, :]], axis=1),
                 jnp.concatenate([t, -t], axis=0),
                 sg[6:7, :], sg[7:8, 0:Q_RANK], sg[7:8, Q_RANK:Q_RANK + KV_RANK]]
        for (w, m, v), g, (og, od, om, ov) in zip(small_in, grads, small_out):
            og[...] = g
            od[...], om[...], ov[...] = _adamw_math(w[...], g, m[...], v[...])

    ins = [a for grp in big for a in grp] + [a for grp in small_w for a in grp] + [small_g]
    out_shape = ([jax.ShapeDtypeStruct(grp[0].shape, F32) for grp in big for _ in range(3)]
                 + [jax.ShapeDtypeStruct(grp[0].shape, F32) for grp in small_w for _ in range(4)])
    vm = pl.BlockSpec(memory_space=pltpu.VMEM)
    res = pl.pallas_call(
        body, name="adamw", in_specs=[vm] * len(ins), out_specs=[vm] * len(out_shape), out_shape=out_shape,
        compiler_params=pltpu.CompilerParams(vmem_limit_bytes=VMEM_LIMIT),
    )(*ins)
    big_res = [res[3 * i:3 * i + 3] for i in range(nb)]
    small_res = [res[3 * nb + 4 * i:3 * nb + 4 * i + 4] for i in range(ns)]
    return big_res, small_res


def _perm_weights(g_in_t, g_q, g_kv, g_out):
    w_in_t = g_in_t.reshape(D_IN, D_MODEL)
    wq = g_q.transpose(1, 0, 2)
    w_q_p = jnp.pad(wq, ((0, 0), (0, 0), (0, HEAD_LANES - NOPE - ROPE))).reshape(Q_RANK, N_HEADS * HEAD_LANES)
    wkv = g_kv.transpose(1, 0, 2)
    wk = jnp.pad(wkv[:, :, :NOPE], ((0, 0), (0, 0), (0, HEAD_LANES - NOPE))).reshape(KV_RANK, N_HEADS * HEAD_LANES)
    wv = wkv[:, :, NOPE:].reshape(KV_RANK, MLA_WIDTH)
    return w_in_t, w_q_p, jnp.concatenate([wk, wv], axis=1), g_out.reshape(D_MODEL, D_MODEL)


def _grad_slabs(dw_in_t, dw_q_p, dw_kv_p, dw_out):
    s_in = dw_in_t.reshape(N_DEV, D_IN // N_DEV, D_MODEL)
    s_q = dw_q_p.reshape(Q_RANK, N_HEADS, HEAD_LANES)[:, :, :NOPE + ROPE].transpose(1, 0, 2)
    hl = N_HEADS * HEAD_LANES
    dk = dw_kv_p[:, :hl].reshape(KV_RANK, N_HEADS, HEAD_LANES)[:, :, :NOPE]
    dv = dw_kv_p[:, hl:].reshape(KV_RANK, N_HEADS, V_DIM)
    s_kv = jnp.concatenate([dk, dv], axis=2).transpose(1, 0, 2)
    return s_in, s_q, s_kv, dw_out.reshape(N_DEV, D_MODEL // N_DEV, D_MODEL)


def _block_sizes(T):
    return min(256, T), min(256, T), min(512, T)


def kernel(x, positions, ln_g, w_in, q_a_norm_g, w_q_b, kv_a_norm_g, w_kv_b, hg_lower_bounds, hg_norm_g, w_out, final_norm_g, loss_target, m_ln_g, m_w_in, m_q_a_norm_g, m_w_q_b, m_kv_a_norm_g, m_w_kv_b, m_hg_lower_bounds, m_hg_norm_g, m_w_out, m_final_norm_g, v_ln_g, v_w_in, v_q_a_norm_g, v_w_q_b, v_kv_a_norm_g, v_w_kv_b, v_hg_lower_bounds, v_hg_norm_g, v_w_out, v_final_norm_g):
    T = x.shape[1]
    tm, tq, bt = _block_sizes(T)
    nq = T // tq
    xs, tgt = x[0], loss_target[0]
    pos_f = positions.astype(F32)
    fng = final_norm_g.reshape(1, D_MODEL)

    w_in_shard_t = w_in[0].T
    gathered = _all_gather_weights([w_in_shard_t, w_q_b[0], w_kv_b[0], w_out[0]])
    w_in_t, w_q_p, w_kv_p, w_out_b = _perm_weights(*gathered)

    c_t, s1_t, s2_t = _rope_tables(pos_f, bt)
    proj, h, qn, kvn, q, k, v = _fwd_in(xs, ln_g, w_in_t, q_a_norm_g, w_q_p, kv_a_norm_g, w_kv_p, c_t, s1_t, s2_t, bt)
    hl = N_HEADS * HEAD_LANES
    v_t = v.reshape(nq, tq, MLA_WIDTH).transpose(0, 2, 1)
    k_t = k.reshape(nq, tq, hl).transpose(0, 2, 1)
    q_t = q.reshape(nq, tq, hl).transpose(0, 2, 1)
    o_mla, lse = _attn_fwd_flat(k, q_t, v_t, tq)
    o_hg, states = _hgrn_fwd(proj, hg_lower_bounds)
    dx2, d_om, dsum, d_gm, d_oh, d_gh, loss_p, d_fng, d_hgn, dw_out = _top(
        xs, tgt, o_mla, o_hg, proj, w_out_b, hg_norm_g, fng, tm)
    dsum = dsum[:, :N_HEADS].T.reshape(N_HEADS, nq, 1, tq)
    do_t = d_om.reshape(nq, tq, hl).transpose(0, 2, 1)
    dq_t, dk, dv = _attn_bwd_flat(k, v, q_t, k_t, do_t, lse, dsum, tq)
    dq = dq_t.transpose(0, 2, 1).reshape(T, N_HEADS * HEAD_LANES)
    d_hq, d_hf, d_hi, d_lb = _hgrn_bwd(proj, hg_lower_bounds, d_oh, states)
    dx, dproj, d_lng, d_qg, d_kvg, dw_q_p, dw_kv_p = _bot(
        xs, dx2, proj, qn, kvn, dq, dk, dv, d_gm, d_hq, d_hf, d_hi, d_gh, c_t, s1_t, s2_t, w_in_t, w_q_p, w_kv_p,
        ln_g, q_a_norm_g, kv_a_norm_g, tm)
    dw_in_t = _dw_in_t(dproj, h, bt)

    small = jnp.concatenate([
        d_lng.reshape(2, SMALL_W), d_fng.reshape(2, SMALL_W), d_lb, loss_p, d_hgn,
        jnp.concatenate([d_qg, d_kvg, jnp.zeros((1, SMALL_W - Q_RANK - KV_RANK), F32)], axis=1)], axis=0)
    g_in, g_q, g_kv, g_out, small_sum = _reduce_scatter(list(_grad_slabs(dw_in_t, dw_q_p, dw_kv_p, dw_out)), small)

    big = [(w_in_shard_t, g_in, m_w_in[0].T, v_w_in[0].T), (w_q_b[0], g_q, m_w_q_b[0], v_w_q_b[0]),
           (w_kv_b[0], g_kv, m_w_kv_b[0], v_w_kv_b[0]), (w_out[0], g_out, m_w_out[0], v_w_out[0])]
    small_w = [(ln_g, m_ln_g, v_ln_g),
               (fng, m_final_norm_g.reshape(1, D_MODEL), v_final_norm_g.reshape(1, D_MODEL)),
               (hg_lower_bounds, m_hg_lower_bounds, v_hg_lower_bounds), (hg_norm_g, m_hg_norm_g, v_hg_norm_g),
               (q_a_norm_g, m_q_a_norm_g, v_q_a_norm_g), (kv_a_norm_g, m_kv_a_norm_g, v_kv_a_norm_g)]
    big_res, small_res = _adamw(big, small_w, small_sum)

    loss = small_sum[5, 0]
    (r_in, r_q, r_kv, r_out) = big_res
    (s_ln, s_fn, s_lb, s_hgn, s_qg, s_kvg) = small_res
    flat = lambda t: t.reshape(D_MODEL)
    lead = lambda t: t[None]
    grads = [s_ln[0], lead(g_in.T), s_qg[0], lead(g_q), s_kvg[0], lead(g_kv), s_lb[0], s_hgn[0], lead(g_out), flat(s_fn[0])]

    def pick(i):
        return [s_ln[i + 1], lead(r_in[i].T), s_qg[i + 1], lead(r_q[i]), s_kvg[i + 1], lead(r_kv[i]), s_lb[i + 1],
                s_hgn[i + 1], lead(r_out[i]), flat(s_fn[i + 1])]

    return (loss, dx[None], *grads, *pick(0), *pick(1), *pick(2))
```

```python
import math

import numpy as np
import jax
import jax.numpy as jnp
from jax import lax
from jax.experimental import pallas as pl
from jax.experimental.pallas import tpu as pltpu

F32 = jnp.float32
BF16 = jnp.bfloat16

D_MODEL = 1024
N_HEADS = 8
NOPE = 64
ROPE = 32
HALF_ROPE = ROPE // 2
V_DIM = 64
Q_RANK = 256
KV_RANK = 128
MLA_WIDTH = N_HEADS * V_DIM
HG_HEADS = 4
HG_DIM = 128
HG_WIDTH = HG_HEADS * HG_DIM
CHUNK = 64
SUB = 16
D_IN = 2976
D_PERM = 3072
ROPE_THETA = 10000.0
EPS = 1e-6
N_DEV = 8
LANES = 128
HEAD_LANES = 128

P_GM, P_HQ, P_HF, P_HI, P_GH, P_QL, P_KVL, P_KR = 0, 512, 1024, 1536, 2048, 2560, 2816, 2944
R_QL, R_KVL, R_KR, R_MAIN = (0, 256), (256, 384), (384, 416), (416, 2976)
ROPE_LO = NOPE
SCALE = 1.0 / math.sqrt(NOPE + ROPE)

ADAM_LR = 0.001
ADAM_B1 = 0.9
ADAM_B2 = 0.999
ADAM_EPS = 1e-08
ADAM_WD = 0.01
ADAM_STEP = 10

VMEM_LIMIT = 56 * 1024 * 1024
MESH = pl.DeviceIdType.MESH

NT = (((1,), (1,)), ((), ()))
TN = (((0,), (0,)), ((), ()))


def _params(n_grid=0, **kw):
    sem = ("arbitrary",) * n_grid if n_grid else None
    return pltpu.CompilerParams(dimension_semantics=sem, vmem_limit_bytes=VMEM_LIMIT, **kw)


def _dot(a, b):
    return jnp.dot(a, b, preferred_element_type=F32)


def _dot_nt(a, b):
    return lax.dot_general(a, b, NT, preferred_element_type=F32)


def _dot_tn(a, b):
    return lax.dot_general(a, b, TN, preferred_element_type=F32)


def _sigmoid(x):
    return 1.0 / (1.0 + jnp.exp(-x))


def _rope_fwd(x, c, s1, s2):
    return x * c + pltpu.roll(x, LANES - HALF_ROPE, 1) * s1 + pltpu.roll(x, HALF_ROPE, 1) * s2


def _rope_bwd(dy, c, s1, s2):
    return dy * c - pltpu.roll(dy, LANES - HALF_ROPE, 1) * s1 - pltpu.roll(dy, HALF_ROPE, 1) * s2


def _in_proj_rows(wt_ref):
    kr = wt_ref[R_KR[0]:R_KR[1], :]
    pad = lambda n: jnp.zeros((n, D_MODEL), kr.dtype)
    return ((P_GM, wt_ref[R_MAIN[0]:R_MAIN[1], :]), (P_QL, wt_ref[R_QL[0]:R_QL[1], :]),
            (P_KVL, wt_ref[R_KVL[0]:R_KVL[1], :]),
            (P_KR, jnp.concatenate([pad(ROPE_LO), kr, pad(LANES - ROPE_LO - ROPE)], axis=0)))


def _full(shape):
    n = len(shape)
    return pl.BlockSpec(shape, lambda *_: (0,) * n)


ROPE_BLOCK = 512


def _rope_constants():
    inv = (np.float32(ROPE_THETA) ** (-np.arange(HALF_ROPE, dtype=np.float32) / np.float32(HALF_ROPE))).astype(np.float32)
    place = np.zeros((3, HALF_ROPE, LANES), np.float32)
    for i in range(HALF_ROPE):
        place[0, i, ROPE_LO + i] = place[0, i, ROPE_LO + HALF_ROPE + i] = 1.0
        place[1, i, ROPE_LO + i] = -1.0
        place[2, i, ROPE_LO + HALF_ROPE + i] = 1.0
    base = np.ones((1, LANES), np.float32)
    base[0, ROPE_LO:ROPE_LO + ROPE] = 0.0
    return jnp.asarray(inv.reshape(HALF_ROPE, 1)), jnp.asarray(place), jnp.asarray(base)


def _rope_block(pos, inv, place_ref, base):
    ang = inv * pos
    cos, sin = jnp.cos(ang), jnp.sin(ang)

    def put(v, k):
        return lax.dot_general(v, place_ref[k], TN, precision=lax.Precision.HIGHEST, preferred_element_type=F32)

    return put(cos, 0) + base, put(sin, 1), put(sin, 2)


def _all_gather_weights(shards, pos_f):
    n = len(shards)
    T = pos_f.shape[1]
    rb = min(ROPE_BLOCK, T)

    def body(*refs):
        ins, (pos_ref, inv_ref, place_ref, base_ref) = refs[:n], refs[n:n + 4]
        outs, tables = refs[n + 4:2 * n + 4], refs[2 * n + 4:2 * n + 7]
        send_sems, recv_sems = refs[2 * n + 7], refs[2 * n + 8]
        x, y, c = lax.axis_index("x"), lax.axis_index("y"), lax.axis_index("c")
        me, sibling = (x, y, c), (x, y, 1 - c)
        chips = [(1 - x, y), (x, 1 - y), (1 - x, 1 - y)]

        def idx(d):
            return 4 * d[0] + 2 * d[1] + d[2]

        def copy(a, k, block, to):
            rows = outs[a].at[idx(block)]
            return pltpu.make_async_remote_copy(src_ref=rows, dst_ref=rows, send_sem=send_sems.at[a, k],
                                                recv_sem=recv_sems.at[a, k], device_id=to, device_id_type=MESH)

        for a in range(n):
            outs[a][idx(me)] = ins[a][...].astype(BF16)
        first = []
        for a in range(n):
            first.append(copy(a, 0, me, sibling))
            first += [copy(a, 1 + j, me, (*chip, c)) for j, chip in enumerate(chips)]
        for cp in first:
            cp.start()
        for r0 in range(0, T, rb):
            for ref, tab in zip(tables, _rope_block(pos_ref[:, r0:r0 + rb], inv_ref[...], place_ref, base_ref[...])):
                ref[r0:r0 + rb, :] = tab
        passed = []
        for j, chip in enumerate(chips):
            for a in range(n):
                copy(a, 1 + j, (*chip, c), me).wait_recv()
                cp = copy(a, 4 + j, (*chip, c), sibling)
                cp.start()
                passed.append(cp)
        for a in range(n):
            copy(a, 0, sibling, me).wait_recv()
            for j, chip in enumerate(chips):
                copy(a, 4 + j, (*chip, 1 - c), me).wait_recv()
        for cp in first + passed:
            cp.wait_send()

    vm = pl.BlockSpec(memory_space=pltpu.VMEM)
    res = pl.pallas_call(
        body, name="all_gather_weights",
        in_specs=[vm] * (n + 4), out_specs=[vm] * (n + 3),
        out_shape=[jax.ShapeDtypeStruct((N_DEV,) + s.shape, BF16) for s in shards]
        + [jax.ShapeDtypeStruct((T, LANES), F32)] * 3,
        scratch_shapes=[pltpu.SemaphoreType.DMA((n, 7)), pltpu.SemaphoreType.DMA((n, 7))],
        compiler_params=pltpu.CompilerParams(vmem_limit_bytes=VMEM_LIMIT),
    )(*shards, pos_f, *_rope_constants())
    return res[:n], res[n:]


def _fwd_in(x, ln_g, w_in_t, q_g, w_q_p, kv_g, w_kv_p, c_t, s1_t, s2_t, tm):
    T = x.shape[0]

    def body(x_ref, lng_ref, win_ref, qg_ref, wq_ref, kvg_ref, wkv_ref, c_ref, s1_ref, s2_ref,
             proj_ref, h_ref, qn_ref, kvn_ref, q_ref, k_ref, v_ref):
        xv = x_ref[...]
        r = lax.rsqrt(jnp.mean(xv * xv, axis=-1, keepdims=True) + EPS)
        h = (xv * r * lng_ref[...]).astype(BF16)
        h_ref[...] = h
        for col, rows in _in_proj_rows(win_ref):
            proj_ref[:, col:col + rows.shape[0]] = _dot_nt(h, rows)
        c, s1, s2 = c_ref[...], s1_ref[...], s2_ref[...]

        ql = proj_ref[:, P_QL:P_QL + Q_RANK]
        rq = lax.rsqrt(jnp.mean(ql * ql, axis=-1, keepdims=True) + EPS)
        qn = (ql * rq * qg_ref[...]).astype(BF16)
        qn_ref[...] = qn
        q = _dot(qn, wq_ref[...])
        for hd in range(N_HEADS):
            sl = slice(hd * HEAD_LANES, (hd + 1) * HEAD_LANES)
            q_ref[:, sl] = _rope_fwd(q[:, sl], c, s1, s2).astype(BF16)

        kvl = proj_ref[:, P_KVL:P_KVL + KV_RANK]
        rk = lax.rsqrt(jnp.mean(kvl * kvl, axis=-1, keepdims=True) + EPS)
        kvn = (kvl * rk * kvg_ref[...]).astype(BF16)
        kvn_ref[...] = kvn
        kv = _dot(kvn, wkv_ref[...])
        kpe = _rope_fwd(proj_ref[:, P_KR:P_KR + LANES], c, s1, s2)
        for hd in range(N_HEADS):
            sl = slice(hd * HEAD_LANES, (hd + 1) * HEAD_LANES)
            k_ref[:, sl] = (kv[:, sl] + kpe).astype(BF16)
        v_ref[...] = kv[:, N_HEADS * HEAD_LANES:].astype(BF16)

    def row(w):
        return pl.BlockSpec((tm, w), lambda i: (i, 0))

    outs = [(D_PERM, F32), (D_MODEL, BF16), (Q_RANK, BF16), (KV_RANK, BF16),
            (N_HEADS * HEAD_LANES, BF16), (N_HEADS * HEAD_LANES, BF16), (MLA_WIDTH, BF16)]
    return pl.pallas_call(
        body, name="fwd_in", grid=(T // tm,),
        in_specs=[row(D_MODEL), _full((1, D_MODEL)), _full((D_IN, D_MODEL)), _full((1, Q_RANK)),
                  _full((Q_RANK, N_HEADS * HEAD_LANES)), _full((1, KV_RANK)),
                  _full((KV_RANK, N_HEADS * HEAD_LANES + MLA_WIDTH)), row(LANES), row(LANES), row(LANES)],
        out_specs=[row(w) for w, _ in outs],
        out_shape=[jax.ShapeDtypeStruct((T, w), dt) for w, dt in outs],
        compiler_params=_params(1),
    )(x, ln_g, w_in_t, q_g, w_q_p, kv_g, w_kv_p, c_t, s1_t, s2_t)


LOG2E = 1.4426950408889634
SCALE2 = SCALE * LOG2E


def _causal(tq):
    r = lax.broadcasted_iota(jnp.int32, (tq, tq), 0)
    c = lax.broadcasted_iota(jnp.int32, (tq, tq), 1)
    return r <= c


MASKED = -1e30


def _causal_bias(bias_ref, tq):
    bias_ref[0] = jnp.zeros((tq, tq), F32)
    bias_ref[1] = jnp.where(_causal(tq), 0.0, MASKED)


def _tile_tables(nq, by_query):
    if by_query:
        pairs = [(j, i) for i in range(nq) for j in range(i + 1)]
    else:
        pairs = [(j, i) for j in range(nq) for i in range(nq - 1, j - 1, -1)]
    pairs.append(pairs[-1])
    jj, ii = np.array(pairs, np.int32).T
    return jnp.asarray(jj), jnp.asarray(ii), len(pairs) - 1


ATTN_TRIP = 8


def _walk_tiles(n, products, tile, flush, buf_a, buf_b):
    bufs = (buf_a, buf_b)
    products(0, buf_a)

    def trip(r, carry):
        for u in range(ATTN_TRIP):
            products(ATTN_TRIP * r + u + 1, bufs[(u + 1) % 2])
            tile(ATTN_TRIP * r + u, bufs[u % 2])
        for u in range(ATTN_TRIP):
            flush(ATTN_TRIP * r + u)
        return carry

    lax.fori_loop(0, n // ATTN_TRIP, trip, 0)
    rest = n - n % ATTN_TRIP
    for u in range(n % ATTN_TRIP):
        if rest + u + 1 < n:
            products(rest + u + 1, bufs[(u + 1) % 2])
        tile(rest + u, bufs[u % 2])
    for u in range(n % ATTN_TRIP):
        flush(rest + u)


def _attn_fwd_flat(k, q_t, v_t, tq):
    T = k.shape[0]
    nq = T // tq
    jj, ii, n = _tile_tables(nq, True)
    heads = [slice(hh * HEAD_LANES, (hh + 1) * HEAD_LANES) for hh in range(2)]

    def body(jj_ref, ii_ref, k_ref, qt_ref, vt_ref, o_ref, lse_ref, sa_ref, sb_ref, m_ref, l_ref, acc_ref, bias_ref):
        def reset(st):
            m_ref[st] = jnp.full(m_ref.shape[1:], MASKED, F32)
            l_ref[st] = jnp.zeros(l_ref.shape[1:], F32)
            acc_ref[st] = jnp.zeros(acc_ref.shape[1:], F32)

        _causal_bias(bias_ref, tq)
        for st in range(ATTN_TRIP):
            reset(st)

        def products(t, buf):
            j, i = jj_ref[t], ii_ref[t]
            kj = k_ref[pl.ds(pl.multiple_of(j * tq, tq), tq), :]
            for hh, sl in enumerate(heads):
                buf[hh] = _dot(kj[:, sl], qt_ref[i, sl, :])

        def tile(t, buf):
            j, i = jj_ref[t], ii_ref[t]
            vt = vt_ref[j]
            bias = bias_ref.at[(j == i).astype(jnp.int32)]
            st = i % ATTN_TRIP
            for hh in range(2):
                s = buf[hh] * SCALE2 + bias[...]
                m = m_ref[st, hh]
                m_new = jnp.maximum(m, jnp.max(s, axis=0, keepdims=True))
                alpha = jnp.exp2(m - m_new)
                p = jnp.exp2(s - m_new)
                m_ref[st, hh] = m_new
                l_ref[st, hh] = alpha * l_ref[st, hh] + jnp.sum(p, axis=0, keepdims=True)
                acc_ref[st, hh] = alpha * acc_ref[st, hh] + _dot(vt, p.astype(BF16))

        def flush(t):
            j, i = jj_ref[t], ii_ref[t]

            @pl.when(j == i)
            def _():
                st = i % ATTN_TRIP
                first = lax.broadcasted_iota(jnp.int32, (LANES, tq), 0) < V_DIM
                out = jnp.where(first, acc_ref[st, 0] / l_ref[st, 0], acc_ref[st, 1] / l_ref[st, 1])
                o_ref[pl.ds(pl.multiple_of(i * tq, tq), tq), :] = out.T
                for hh in range(2):
                    lse_ref[hh, i] = m_ref[st, hh] + jnp.log2(l_ref[st, hh])
                reset(st)

        _walk_tiles(n, products, tile, flush, sa_ref, sb_ref)

    smem = pl.BlockSpec(memory_space=pltpu.SMEM)
    return pl.pallas_call(
        body, name="attn_fwd", grid=(N_HEADS // 2,),
        in_specs=[smem, smem,
                  pl.BlockSpec((T, 2 * HEAD_LANES), lambda p: (0, p)),
                  pl.BlockSpec((nq, 2 * HEAD_LANES, tq), lambda p: (0, p, 0)),
                  pl.BlockSpec((nq, LANES, tq), lambda p: (0, p, 0))],
        out_specs=[pl.BlockSpec((T, LANES), lambda p: (0, p)),
                   pl.BlockSpec((2, nq, 1, tq), lambda p: (p, 0, 0, 0))],
        out_shape=[jax.ShapeDtypeStruct((T, MLA_WIDTH), F32), jax.ShapeDtypeStruct((N_HEADS, nq, 1, tq), F32)],
        scratch_shapes=[pltpu.VMEM((2, tq, tq), F32), pltpu.VMEM((2, tq, tq), F32),
                        pltpu.VMEM((ATTN_TRIP, 2, 1, tq), F32), pltpu.VMEM((ATTN_TRIP, 2, 1, tq), F32),
                        pltpu.VMEM((ATTN_TRIP, 2, LANES, tq), F32), pltpu.VMEM((2, tq, tq), F32)],
        compiler_params=_params(1),
    )(jj, ii, k, q_t, v_t)


def _attn_bwd_flat(k, v, q_t, k_t, do_t, lse, dsum, tq):
    T = k.shape[0]
    nq = T // tq
    jj, ii, n = _tile_tables(nq, False)
    heads = [slice(hh * HEAD_LANES, (hh + 1) * HEAD_LANES) for hh in range(2)]

    def body(jj_ref, ii_ref, k_ref, v_ref, qt_ref, kt_ref, dot_ref, lse_ref, dsum_ref, dqt_ref, dk_ref, dv_ref,
             ba_ref, bb_ref, dkt_ref, dvt_ref, bias_ref):
        _causal_bias(bias_ref, tq)
        dqt_ref[...] = jnp.zeros_like(dqt_ref)
        dkt_ref[...] = jnp.zeros_like(dkt_ref)
        dvt_ref[...] = jnp.zeros_like(dvt_ref)

        def products(t, buf):
            j, i = jj_ref[t], ii_ref[t]
            rows = pl.ds(pl.multiple_of(j * tq, tq), tq)
            for hh, sl in enumerate(heads):
                buf[hh] = _dot(k_ref[rows, sl], qt_ref[i, sl, :])
                buf[2 + hh] = _dot(v_ref[rows, :], dot_ref[i, sl, :])

        def tile(t, buf):
            j, i = jj_ref[t], ii_ref[t]
            bias = bias_ref.at[(j == i).astype(jnp.int32)]
            st = j % ATTN_TRIP
            dv_new = None
            for hh, sl in enumerate(heads):
                p = jnp.exp2(buf[hh] * SCALE2 + bias[...] - lse_ref[hh, i])
                ds = (p * (buf[2 + hh] - dsum_ref[hh, i]) * SCALE).astype(BF16)
                dv_h = _dot_nt(dot_ref[i, sl, :], p.astype(BF16))
                dv_new = dv_h if dv_new is None else dv_new + dv_h
                dkt_ref[st, sl, :] += _dot_nt(qt_ref[i, sl, :], ds)
                dqt_ref[i, sl, :] += _dot(kt_ref[j, sl, :], ds)
            dvt_ref[st] += dv_new

        def flush(t):
            j, i = jj_ref[t], ii_ref[t]

            @pl.when(j == i)
            def _():
                st = j % ATTN_TRIP
                rows = pl.ds(pl.multiple_of(j * tq, tq), tq)
                dk_ref[rows, :] = dkt_ref[st].T
                dv_ref[rows, :] = dvt_ref[st].T
                dkt_ref[st] = jnp.zeros(dkt_ref.shape[1:], F32)
                dvt_ref[st] = jnp.zeros(dvt_ref.shape[1:], F32)

        _walk_tiles(n, products, tile, flush, ba_ref, bb_ref)

    smem = pl.BlockSpec(memory_space=pltpu.SMEM)
    stat = pl.BlockSpec((2, nq, 1, tq), lambda p: (p, 0, 0, 0))
    blocks_t = pl.BlockSpec((nq, 2 * HEAD_LANES, tq), lambda p: (0, p, 0))
    return pl.pallas_call(
        body, name="attn_bwd", grid=(N_HEADS // 2,),
        in_specs=[smem, smem,
                  pl.BlockSpec((T, 2 * HEAD_LANES), lambda p: (0, p)),
                  pl.BlockSpec((T, LANES), lambda p: (0, p)),
                  blocks_t, blocks_t, blocks_t, stat, stat],
        out_specs=[blocks_t,
                   pl.BlockSpec((T, 2 * HEAD_LANES), lambda p: (0, p)),
                   pl.BlockSpec((T, LANES), lambda p: (0, p))],
        out_shape=[jax.ShapeDtypeStruct((nq, N_HEADS * HEAD_LANES, tq), F32),
                   jax.ShapeDtypeStruct((T, N_HEADS * HEAD_LANES), F32),
                   jax.ShapeDtypeStruct((T, MLA_WIDTH), F32)],
        scratch_shapes=[pltpu.VMEM((4, tq, tq), F32), pltpu.VMEM((4, tq, tq), F32),
                        pltpu.VMEM((ATTN_TRIP, 2 * HEAD_LANES, tq), F32), pltpu.VMEM((ATTN_TRIP, LANES, tq), F32),
                        pltpu.VMEM((2, tq, tq), F32)],
        compiler_params=_params(1),
    )(jj, ii, k, v, q_t, k_t, do_t, lse, dsum)


def _lower_bound(lbp):
    a, b = lbp[0:1, :], lbp[1:2, :]
    mx = jnp.maximum(a, b)
    ea, eb = jnp.exp(a - mx), jnp.exp(b - mx)
    return ea / (ea + eb)


def _tri(lower):
    r = lax.broadcasted_iota(jnp.int32, (CHUNK, CHUNK), 0)
    c = lax.broadcasted_iota(jnp.int32, (CHUNK, CHUNK), 1)
    return (c <= r) if lower else (c >= r)


def _hg_gates(hq, hf, lb):
    sq = _sigmoid(hq)
    sf = _sigmoid(hf)
    f = lb + (1.0 - lb) * sf
    g = jnp.log(f)
    gcum = jnp.dot(_tri(True).astype(F32), g, precision=lax.Precision.HIGHEST, preferred_element_type=F32)
    return sq, sf, f, hq * sq, 1.0 - f, gcum


def _head(x, hd):
    return x[:, hd * HG_DIM:(hd + 1) * HG_DIM]


def _all_heads(fn):
    return jnp.concatenate([fn(hd) for hd in range(HG_HEADS)], axis=1)


def _hg_blocks(q, kk, gcum):
    rowi = lax.broadcasted_iota(jnp.int32, gcum.shape, 0)
    out = []
    for blk in range(CHUNK // SUB):
        lo, hi = blk * SUB, (blk + 1) * SUB
        gb = gcum[lo - 1:lo, :] if blk else jnp.zeros_like(gcum[0:1, :])
        eq = jnp.exp(gcum[lo:hi, :] - gb)
        ek = jnp.exp(jnp.where(rowi < hi, gb - gcum, 0.0))
        out.append((eq, ek, (q[lo:hi, :] * eq).astype(BF16), (kk * ek).astype(BF16)))
    return out


def _hg_scores(blocks):
    out = []
    for hd in range(HG_HEADS):
        a = jnp.concatenate([_dot_nt(_head(qb, hd), _head(kb, hd)) for _, _, qb, kb in blocks], axis=0)
        out.append(jnp.where(_tri(True), a, 0.0))
    return out


HG_STEP_CHUNKS = 8


def _hgrn_fwd(proj, lbp):
    T = proj.shape[0]
    nc = T // CHUNK
    ns = min(HG_STEP_CHUNKS, nc)
    rows = ns * CHUNK

    def body(hq_ref, hf_ref, hi_ref, lbp_ref, o_ref, st_ref, state):
        @pl.when(pl.program_id(0) == 0)
        def _():
            state[...] = jnp.zeros_like(state)

        lb = _lower_bound(lbp_ref[...])
        work = []
        for c in range(ns):
            r = slice(c * CHUNK, (c + 1) * CHUNK)
            _, _, _, q, kk, gcum = _hg_gates(hq_ref[r, :], hf_ref[r, :], lb)
            vb = hi_ref[r, :].astype(BF16)
            a = _hg_scores(_hg_blocks(q, kk, gcum))
            gend = gcum[CHUNK - 1:CHUNK, :]
            qgb = (q * jnp.exp(gcum)).astype(BF16)
            kgeb = (kk * jnp.exp(gend - gcum)).astype(BF16)
            intra = [_dot(a[hd].astype(BF16), _head(vb, hd)) for hd in range(HG_HEADS)]
            update = [_dot_tn(_head(vb, hd), _head(kgeb, hd)) for hd in range(HG_HEADS)]
            work.append((qgb, jnp.exp(gend), intra, update))
        for hd in range(HG_HEADS):
            st = state[hd]
            for c, (qgb, egend, intra, update) in enumerate(work):
                st_ref[c, hd] = st
                o_ref[c * CHUNK:(c + 1) * CHUNK, hd * HG_DIM:(hd + 1) * HG_DIM] = (
                    intra[hd] + _dot_nt(_head(qgb, hd), st.astype(BF16)))
                st = st * _head(egend, hd) + update[hd]
            state[hd] = st

    def col(cb):
        return pl.BlockSpec((rows, HG_WIDTH), lambda i: (i, cb))

    return pl.pallas_call(
        body, name="hgrn_fwd", grid=(nc // ns,),
        in_specs=[col(P_HQ // HG_WIDTH), col(P_HF // HG_WIDTH), col(P_HI // HG_WIDTH), _full((2, HG_WIDTH))],
        out_specs=[pl.BlockSpec((rows, HG_WIDTH), lambda i: (i, 0)),
                   pl.BlockSpec((ns, HG_HEADS, HG_DIM, HG_DIM), lambda i: (i, 0, 0, 0))],
        out_shape=[jax.ShapeDtypeStruct((T, HG_WIDTH), F32),
                   jax.ShapeDtypeStruct((nc, HG_HEADS, HG_DIM, HG_DIM), F32)],
        scratch_shapes=[pltpu.VMEM((HG_HEADS, HG_DIM, HG_DIM), F32)],
        compiler_params=_params(1),
    )(proj, proj, proj, lbp)


def _hgrn_bwd(proj, lbp, do_hg, states):
    T = proj.shape[0]
    nc = T // CHUNK
    ns = min(HG_STEP_CHUNKS, nc)
    rows = ns * CHUNK
    steps = nc // ns

    def body(hq_ref, hf_ref, hi_ref, lbp_ref, do_ref, st_ref, dhq_ref, dhf_ref, dhi_ref, dlb_ref, dstate):
        @pl.when(pl.program_id(0) == 0)
        def _():
            dstate[...] = jnp.zeros_like(dstate)
            dlb_ref[...] = jnp.zeros_like(dlb_ref)

        lb = _lower_bound(lbp_ref[...])

        dst_all = [dstate[hd] for hd in range(HG_HEADS)]
        dlb = jnp.zeros_like(lb)
        last = lax.broadcasted_iota(jnp.int32, (CHUNK, HG_WIDTH), 0) == CHUNK - 1
        for c in reversed(range(ns)):
            r = slice(c * CHUNK, (c + 1) * CHUNK)
            hq = hq_ref[r, :]
            sq, sf, f, q, kk, gcum = _hg_gates(hq, hf_ref[r, :], lb)
            vb = hi_ref[r, :].astype(BF16)
            dob = do_ref[r, :].astype(BF16)
            blocks = _hg_blocks(q, kk, gcum)
            a = _hg_scores(blocks)
            gend = gcum[CHUNK - 1:CHUNK, :]
            eg, egend, ekend = jnp.exp(gcum), jnp.exp(gend), jnp.exp(gend - gcum)
            qg, kge = q * eg, kk * ekend
            qgb, kgeb = qg.astype(BF16), kge.astype(BF16)

            dv, dqg, dkge, st_dst, dq_blk, dk_blk = [], [], [], [], [], []
            for hd in range(HG_HEADS):
                st = st_ref[c, hd]
                dst = dst_all[hd]
                dstb = dst.astype(BF16)
                do_h, v_h = _head(dob, hd), _head(vb, hd)
                dv.append(_dot_tn(a[hd].astype(BF16), do_h) + _dot_nt(_head(kgeb, hd), dstb))
                da = jnp.where(_tri(True), _dot_nt(do_h, v_h), 0.0).astype(BF16)
                dqg.append(_dot(do_h, st.astype(BF16)))
                dkge.append(_dot(v_h, dstb))
                st_dst.append(jnp.sum(st * dst, axis=0, keepdims=True))
                dst_all[hd] = _dot_tn(do_h, _head(qgb, hd)) + dst * _head(egend, hd)
                dq_blk.append([_dot(da[b * SUB:(b + 1) * SUB, :], _head(kb, hd)) for b, (_, _, _, kb) in enumerate(blocks)])
                dk_blk.append([_dot_tn(da[b * SUB:(b + 1) * SUB, :], _head(qb, hd)) for b, (_, _, qb, _) in enumerate(blocks)])
            dv, dqg, dkge, st_dst = (jnp.concatenate(t, axis=1) for t in (dv, dqg, dkge, st_dst))

            dq_a, dg_q = [], []
            dk_a, dg_k = jnp.zeros_like(gcum), jnp.zeros_like(gcum)
            for b, (eq, ek, qb, kb) in enumerate(blocks):
                dq_b = _all_heads(lambda hd: dq_blk[hd][b])
                dk_b = _all_heads(lambda hd: dk_blk[hd][b])
                dq_a.append(dq_b * eq)
                dk_a = dk_a + dk_b * ek
                dg_q.append(qb.astype(F32) * dq_b)
                dg_k = dg_k + kb.astype(F32) * dk_b
            dq_a = jnp.concatenate(dq_a, axis=0)

            dgend = st_dst * egend + jnp.sum(dkge * kge, axis=0, keepdims=True)
            dq = dq_a + dqg * eg
            dk = dk_a + dkge * ekend
            dgc = jnp.concatenate(dg_q, axis=0) - dg_k + dqg * qg - dkge * kge + jnp.where(last, dgend, 0.0)
            dg = jnp.dot(_tri(False).astype(F32), dgc, precision=lax.Precision.HIGHEST, preferred_element_type=F32)
            df = dg / f - dk
            dhf_ref[r, :] = df * (1.0 - lb) * sf * (1.0 - sf)
            dlb = dlb + jnp.sum(df * (1.0 - sf), axis=0, keepdims=True)
            dhq_ref[r, :] = dq * (sq * (1.0 + hq * (1.0 - sq)))
            dhi_ref[r, :] = dv
        for hd in range(HG_HEADS):
            dstate[hd] = dst_all[hd]
        dlb_ref[...] += dlb

    def col(cb):
        return pl.BlockSpec((rows, HG_WIDTH), lambda i: (steps - 1 - i, cb))

    grad = jax.ShapeDtypeStruct((T, HG_WIDTH), F32)
    return pl.pallas_call(
        body, name="hgrn_bwd", grid=(steps,),
        in_specs=[col(P_HQ // HG_WIDTH), col(P_HF // HG_WIDTH), col(P_HI // HG_WIDTH), _full((2, HG_WIDTH)),
                  col(0), pl.BlockSpec((ns, HG_HEADS, HG_DIM, HG_DIM), lambda i: (steps - 1 - i, 0, 0, 0))],
        out_specs=[col(0), col(0), col(0), _full((1, HG_WIDTH))],
        out_shape=[grad, grad, grad, jax.ShapeDtypeStruct((1, HG_WIDTH), F32)],
        scratch_shapes=[pltpu.VMEM((HG_HEADS, HG_DIM, HG_DIM), F32)],
        compiler_params=_params(1),
    )(proj, proj, proj, lbp, do_hg, states)


def _top(x, tgt, o_mla, o_hg, proj, w_out, hg_norm_g, final_g, tm):
    T = x.shape[0]

    def body(x_ref, tgt_ref, om_ref, oh_ref, gm_ref, gh_ref, wout_ref, hgn_ref, fng_ref,
             dx2_ref, dom_ref, dsum_ref, dgm_ref, doh_ref, dgh_ref, loss_ref, dfng_ref, dhgn_ref, dwout_ref, ycat_ref):
        @pl.when(pl.program_id(0) == 0)
        def _():
            for ref in (loss_ref, dfng_ref, dhgn_ref, dwout_ref):
                ref[...] = jnp.zeros_like(ref)

        gm, om = gm_ref[...], om_ref[...]
        sgm = _sigmoid(gm)
        silu_m = gm * sgm
        gh, oh, gam = gh_ref[...], oh_ref[...], hgn_ref[...]
        sgh = _sigmoid(gh)
        silu_h = gh * sgh
        rr, nn = [], []
        for hd in range(HG_HEADS):
            oh_h = oh[:, hd * HG_DIM:(hd + 1) * HG_DIM]
            r_h = lax.rsqrt(jnp.mean(oh_h * oh_h, axis=-1, keepdims=True) + EPS)
            rr.append(r_h)
            nn.append(oh_h * r_h)
        n = jnp.concatenate(nn, axis=1)
        ng = n * gam
        ycat_ref[:, :MLA_WIDTH] = (om * silu_m).astype(BF16)
        ycat_ref[:, MLA_WIDTH:] = (ng * silu_h).astype(BF16)
        wout = wout_ref[...]
        x2 = x_ref[...] + _dot(ycat_ref[...], wout)
        r = lax.rsqrt(jnp.mean(x2 * x2, axis=-1, keepdims=True) + EPS)
        xh = x2 * r
        fng = fng_ref[...]
        err = xh * fng - tgt_ref[...]
        loss_ref[...] += 0.5 * jnp.sum(jnp.mean(err * err, axis=-1, keepdims=True), axis=0, keepdims=True)
        dout = err * (1.0 / D_MODEL)
        dfng_ref[...] += jnp.sum(dout * xh, axis=0, keepdims=True)
        dxh = dout * fng
        dx2 = r * (dxh - xh * jnp.mean(dxh * xh, axis=-1, keepdims=True))
        dx2_ref[...] = dx2
        dx2b = dx2.astype(BF16)
        dwout_ref[...] += _dot_tn(ycat_ref[...], dx2b)
        dycat = _dot_nt(dx2b, wout)
        dym, dyh = dycat[:, :MLA_WIDTH], dycat[:, MLA_WIDTH:]
        dom = dym * silu_m
        first = lax.broadcasted_iota(jnp.int32, (tm, LANES), 1) < V_DIM
        for pp in range(N_HEADS // 2):
            pair = dom[:, pp * LANES:(pp + 1) * LANES]
            dom_ref[:, 2 * pp * HEAD_LANES:(2 * pp + 1) * HEAD_LANES] = jnp.where(first, pair, 0.0).astype(BF16)
            dom_ref[:, (2 * pp + 1) * HEAD_LANES:(2 * pp + 2) * HEAD_LANES] = jnp.where(first, 0.0, pair).astype(BF16)
        head_of = lax.broadcasted_iota(jnp.int32, (MLA_WIDTH, LANES), 0) // V_DIM
        pick = (head_of == lax.broadcasted_iota(jnp.int32, (MLA_WIDTH, LANES), 1)).astype(F32)
        dsum_ref[...] = jnp.dot(dom * om, pick, precision=lax.Precision.HIGHEST, preferred_element_type=F32)
        dgm_ref[...] = dym * om * (sgm * (1.0 + gm * (1.0 - sgm)))
        dgh_ref[...] = dyh * ng * (sgh * (1.0 + gh * (1.0 - sgh)))
        dng = dyh * silu_h
        dhgn_ref[...] += jnp.sum(dng * n, axis=0, keepdims=True)
        dn = dng * gam
        for hd in range(HG_HEADS):
            sl = slice(hd * HG_DIM, (hd + 1) * HG_DIM)
            dn_h, n_h = dn[:, sl], nn[hd]
            doh_ref[:, sl] = rr[hd] * (dn_h - n_h * jnp.mean(dn_h * n_h, axis=-1, keepdims=True))

    def row(w, cb=0):
        return pl.BlockSpec((tm, w), lambda i: (i, cb))

    outs = [(D_MODEL, F32), (N_HEADS * HEAD_LANES, BF16), (LANES, F32), (MLA_WIDTH, F32), (HG_WIDTH, F32), (HG_WIDTH, F32)]
    small = [(1, SMALL_W), (1, D_MODEL), (1, HG_WIDTH), (D_MODEL, D_MODEL)]
    return pl.pallas_call(
        body, name="top", grid=(T // tm,),
        in_specs=[row(D_MODEL), row(D_MODEL), row(MLA_WIDTH), row(HG_WIDTH),
                  row(MLA_WIDTH, P_GM // MLA_WIDTH), row(HG_WIDTH, P_GH // HG_WIDTH),
                  _full((D_MODEL, D_MODEL)), _full((1, HG_WIDTH)), _full((1, D_MODEL))],
        out_specs=[row(w) for w, _ in outs] + [_full(s) for s in small],
        out_shape=[jax.ShapeDtypeStruct((T, w), dt) for w, dt in outs] + [jax.ShapeDtypeStruct(s, F32) for s in small],
        scratch_shapes=[pltpu.VMEM((tm, D_MODEL), BF16)],
        compiler_params=_params(1),
    )(x, tgt, o_mla, o_hg, proj, proj, w_out, hg_norm_g, final_g)


def _bot(x, dx2, proj, qn, kvn, dq, dk, dv, dgm, dhq, dhf, dhi, dgh, c_t, s1_t, s2_t, w_in_t, w_q_p, w_kv_p, ln_g, q_g, kv_g, tm):
    T = x.shape[0]
    lat_w = D_PERM - P_QL

    def body(x_ref, dx2_ref, lat_ref, qn_ref, kvn_ref, dq_ref, dk_ref, dv_ref, dgm_ref, dhq_ref, dhf_ref, dhi_ref,
             dgh_ref, c_ref, s1_ref, s2_ref, win_ref, wq_ref, wkv_ref, lng_ref, qg_ref, kvg_ref,
             dx_ref, dproj_ref, dlng_ref, dqg_ref, dkvg_ref, dwq_ref, dwkv_ref, dqpre_ref, dkv_ref):
        @pl.when(pl.program_id(0) == 0)
        def _():
            for ref in (dlng_ref, dqg_ref, dkvg_ref, dwq_ref, dwkv_ref):
                ref[...] = jnp.zeros_like(ref)

        c, s1, s2 = c_ref[...], s1_ref[...], s2_ref[...]
        dkpe = jnp.zeros((tm, LANES), F32)
        for hd in range(N_HEADS):
            sl = slice(hd * HEAD_LANES, (hd + 1) * HEAD_LANES)
            dqpre_ref[:, sl] = _rope_bwd(dq_ref[:, sl], c, s1, s2).astype(BF16)
            dk_h = dk_ref[:, sl]
            dkpe = dkpe + dk_h
            dkv_ref[:, sl] = dk_h.astype(BF16)
        dkv_ref[:, N_HEADS * HEAD_LANES:] = dv_ref[...].astype(BF16)
        lane = lax.broadcasted_iota(jnp.int32, (tm, LANES), 1)
        rope_lanes = jnp.logical_and(lane >= ROPE_LO, lane < ROPE_LO + ROPE)
        dkr = jnp.where(rope_lanes, _rope_bwd(dkpe, c, s1, s2), 0.0)

        def norm_bwd(v, g, dy):
            r = lax.rsqrt(jnp.mean(v * v, axis=-1, keepdims=True) + EPS)
            vh = v * r
            dvh = dy * g
            return jnp.sum(dy * vh, axis=0, keepdims=True), r * (dvh - vh * jnp.mean(dvh * vh, axis=-1, keepdims=True))

        dwq_ref[...] += _dot_tn(qn_ref[...], dqpre_ref[...])
        dwkv_ref[...] += _dot_tn(kvn_ref[...], dkv_ref[...])
        dqn = _dot_nt(dqpre_ref[...], wq_ref[...])
        dg_q, dql = norm_bwd(lat_ref[:, :Q_RANK], qg_ref[...], dqn)
        dqg_ref[...] += dg_q
        dkn = _dot_nt(dkv_ref[...], wkv_ref[...])
        dg_kv, dkvl = norm_bwd(lat_ref[:, Q_RANK:Q_RANK + KV_RANK], kvg_ref[...], dkn)
        dkvg_ref[...] += dg_kv

        dproj_ref[:, P_GM:P_GM + MLA_WIDTH] = dgm_ref[...].astype(BF16)
        dproj_ref[:, P_HQ:P_HQ + HG_WIDTH] = dhq_ref[...].astype(BF16)
        dproj_ref[:, P_HF:P_HF + HG_WIDTH] = dhf_ref[...].astype(BF16)
        dproj_ref[:, P_HI:P_HI + HG_WIDTH] = dhi_ref[...].astype(BF16)
        dproj_ref[:, P_GH:P_GH + HG_WIDTH] = dgh_ref[...].astype(BF16)
        dproj_ref[:, P_QL:P_QL + Q_RANK] = dql.astype(BF16)
        dproj_ref[:, P_KVL:P_KVL + KV_RANK] = dkvl.astype(BF16)
        dproj_ref[:, P_KR:P_KR + LANES] = dkr.astype(BF16)
        dh = sum(_dot(dproj_ref[:, col:col + rows.shape[0]], rows) for col, rows in _in_proj_rows(win_ref))
        dg_ln, dxn = norm_bwd(x_ref[...], lng_ref[...], dh)
        dlng_ref[...] += dg_ln
        dx_ref[...] = dx2_ref[...] + dxn

    def row(w, cb=0):
        return pl.BlockSpec((tm, w), lambda i: (i, cb))

    hl = N_HEADS * HEAD_LANES
    outs = [(D_MODEL, F32), (D_PERM, BF16)]
    small = [(1, D_MODEL), (1, Q_RANK), (1, KV_RANK), (Q_RANK, hl), (KV_RANK, hl + MLA_WIDTH)]
    return pl.pallas_call(
        body, name="bot", grid=(T // tm,),
        in_specs=[row(D_MODEL), row(D_MODEL), row(lat_w, P_QL // lat_w), row(Q_RANK), row(KV_RANK),
                  row(hl), row(hl), row(MLA_WIDTH),
                  row(MLA_WIDTH), row(HG_WIDTH), row(HG_WIDTH), row(HG_WIDTH), row(HG_WIDTH),
                  row(LANES), row(LANES), row(LANES),
                  _full((D_IN, D_MODEL)), _full((Q_RANK, hl)), _full((KV_RANK, hl + MLA_WIDTH)),
                  _full((1, D_MODEL)), _full((1, Q_RANK)), _full((1, KV_RANK))],
        out_specs=[row(w) for w, _ in outs] + [_full(s) for s in small],
        out_shape=[jax.ShapeDtypeStruct((T, w), dt) for w, dt in outs] + [jax.ShapeDtypeStruct(s, F32) for s in small],
        scratch_shapes=[pltpu.VMEM((tm, hl), BF16), pltpu.VMEM((tm, hl + MLA_WIDTH), BF16)],
        compiler_params=_params(1),
    )(x, dx2, proj, qn, kvn, dq, dk, dv, dgm, dhq, dhf, dhi, dgh, c_t, s1_t, s2_t, w_in_t, w_q_p, w_kv_p, ln_g, q_g,
      kv_g)


DW_ROWS = 512


def _dw_in_t(dproj, h, bt):
    T = h.shape[0]
    nb, nt = D_PERM // DW_ROWS, T // bt
    assert P_QL == (nb - 1) * DW_ROWS

    def body(a_ref, b_ref, o_ref, acc_ref, sem):
        blk, t = pl.program_id(0), pl.program_id(1)

        @pl.when(t == 0)
        def _():
            acc_ref[...] = jnp.zeros_like(acc_ref)

        acc_ref[...] += _dot_tn(a_ref[...], b_ref[...])

        def put(src, dst):
            cp = pltpu.make_async_copy(src, dst, sem)
            cp.start()
            cp.wait()

        @pl.when(jnp.logical_and(t == nt - 1, blk < nb - 1))
        def _():
            put(acc_ref, o_ref.at[pl.ds(pl.multiple_of(R_MAIN[0] + blk * DW_ROWS, 8), DW_ROWS), :])

        @pl.when(jnp.logical_and(t == nt - 1, blk == nb - 1))
        def _():
            lat = R_KVL[1] - R_QL[0]
            put(acc_ref.at[0:lat, :], o_ref.at[R_QL[0]:R_KVL[1], :])
            kr = P_KR - P_QL + ROPE_LO
            put(acc_ref.at[kr:kr + ROPE, :], o_ref.at[R_KR[0]:R_KR[1], :])

    return pl.pallas_call(
        body, name="dw_in", grid=(nb, nt),
        in_specs=[pl.BlockSpec((bt, DW_ROWS), lambda n, t: (t, n)), pl.BlockSpec((bt, D_MODEL), lambda n, t: (t, 0))],
        out_specs=pl.BlockSpec(memory_space=pl.ANY),
        out_shape=jax.ShapeDtypeStruct((D_IN, D_MODEL), F32),
        scratch_shapes=[pltpu.VMEM((DW_ROWS, D_MODEL), F32), pltpu.SemaphoreType.DMA],
        compiler_params=_params(2),
    )(dproj, h)


RS_ROWS = 256


def _reduce_scatter(slabs, small):
    n = len(slabs)
    units = []
    for a, s in enumerate(slabs):
        rows, cols = s.shape[1:]
        if rows % RS_ROWS == 0 or rows < RS_ROWS:
            units += [(a, (pl.ds(r0, min(rows, RS_ROWS)), slice(None))) for r0 in range(0, rows, RS_ROWS)]
        else:
            units += [(a, (slice(None), pl.ds(c0, RS_ROWS))) for c0 in range(0, cols, RS_ROWS)]
    nu = len(units)

    def body(*refs):
        ins, small_ref = refs[:n], refs[n]
        outs, small_out = refs[n + 1:2 * n + 1], refs[2 * n + 1]
        own, sib_land, ici_out, ici_land = (refs[(2 + g) * n + 2:(3 + g) * n + 2] for g in range(4))
        small_land = refs[6 * n + 2]
        loc_sems, d2d_send, d2d_recv, ici_send, ici_recv, sm_send, sm_recv = refs[6 * n + 3:6 * n + 10]
        x, y, c = lax.axis_index("x"), lax.axis_index("y"), lax.axis_index("c")
        me = 4 * x + 2 * y + c

        def chip(k):
            return (1 - x if k & 2 else x, 1 - y if k & 1 else y)

        def block(k, core):
            px, py = chip(k)
            return 4 * px + 2 * py + core

        def part(u):
            return units[u]

        def local(u, k):
            a, rows = part(u)
            return pltpu.make_async_copy(ins[a].at[(block(k, c),) + rows], own[a].at[(k,) + rows], loc_sems.at[u, k])

        def to_sibling(u, k):
            a, rows = part(u)
            return pltpu.make_async_remote_copy(
                src_ref=ins[a].at[(block(k, 1 - c),) + rows], dst_ref=sib_land[a].at[(k,) + rows],
                send_sem=d2d_send.at[u, k], recv_sem=d2d_recv.at[u, k], device_id=(x, y, 1 - c), device_id_type=MESH)

        def to_chip(u, k):
            a, rows = part(u)
            return pltpu.make_async_remote_copy(
                src_ref=ici_out[a].at[(k - 1,) + rows], dst_ref=ici_land[a].at[(k - 1,) + rows],
                send_sem=ici_send.at[u, k - 1], recv_sem=ici_recv.at[u, k - 1], device_id=(*chip(k), c),
                device_id_type=MESH)

        def small_copy(k, receiving):
            px, py = chip(k >> 1)
            pc = 1 - c if k & 1 else c
            slot = 4 * px + 2 * py + pc if receiving else me
            return pltpu.make_async_remote_copy(
                src_ref=small_ref, dst_ref=small_land.at[slot], send_sem=sm_send.at[k - 1], recv_sem=sm_recv.at[k - 1],
                device_id=(px, py, pc), device_id_type=MESH)

        for u in range(nu):
            for k in range(4):
                local(u, k).start()
        for u in range(nu):
            for k in range(4):
                to_sibling(u, k).start()
        small_land[me] = small_ref[...]
        for k in range(1, N_DEV):
            small_copy(k, False).start()
        for u in range(nu):
            a, rows = part(u)
            for k in range(4):
                local(u, k).wait()
                to_sibling(u, k).wait_recv()
            for k in range(1, 4):
                ici_out[a][(k - 1,) + rows] = (own[a][(k,) + rows] + sib_land[a][(k,) + rows]).astype(BF16)
                to_chip(u, k).start()
        for u in range(nu):
            a, rows = part(u)
            acc = own[a][(0,) + rows] + sib_land[a][(0,) + rows]
            for k in range(1, 4):
                to_chip(u, k).wait_recv()
                acc = acc + ici_land[a][(k - 1,) + rows].astype(F32)
            outs[a][rows] = acc
        for k in range(1, N_DEV):
            small_copy(k, True).wait_recv()
        acc = small_land[0]
        for d in range(1, N_DEV):
            acc = acc + small_land[d]
        small_out[...] = acc
        for u in range(nu):
            for k in range(4):
                to_sibling(u, k).wait_send()
            for k in range(1, 4):
                to_chip(u, k).wait_send()
        for k in range(1, N_DEV):
            small_copy(k, False).wait_send()

    vm = pl.BlockSpec(memory_space=pltpu.VMEM)
    hbm = pl.BlockSpec(memory_space=pl.ANY)
    dma = pltpu.SemaphoreType.DMA
    return pl.pallas_call(
        body, name="reduce_scatter_grads",
        in_specs=[hbm] * n + [vm], out_specs=[vm] * (n + 1),
        out_shape=[jax.ShapeDtypeStruct(s.shape[1:], F32) for s in slabs] + [jax.ShapeDtypeStruct(small.shape, F32)],
        scratch_shapes=[pltpu.VMEM((4,) + s.shape[1:], F32) for s in slabs] * 2
        + [pltpu.VMEM((3,) + s.shape[1:], BF16) for s in slabs] * 2
        + [pltpu.VMEM((N_DEV,) + small.shape, F32)]
        + [dma((nu, 4)), dma((nu, 4)), dma((nu, 4)), dma((nu, 3)), dma((nu, 3)), dma((N_DEV - 1,)), dma((N_DEV - 1,))],
        compiler_params=pltpu.CompilerParams(vmem_limit_bytes=VMEM_LIMIT),
    )(*slabs, small)


def _adamw_math(w, g, m, v):
    m = ADAM_B1 * m + (1.0 - ADAM_B1) * g
    v = ADAM_B2 * v + (1.0 - ADAM_B2) * (g * g)
    m_hat = m / (1.0 - ADAM_B1 ** ADAM_STEP)
    v_hat = v / (1.0 - ADAM_B2 ** ADAM_STEP)
    delta = -ADAM_LR * (m_hat / (jnp.sqrt(v_hat) + ADAM_EPS) + ADAM_WD * w)
    return delta, m, v


SMALL_W = 512


def _adamw(big, small_w, small_g):
    nb, ns = len(big), len(small_w)

    def body(*refs):
        k = 0
        big_in = [refs[4 * i:4 * i + 4] for i in range(nb)]
        k = 4 * nb
        small_in = [refs[k + 3 * i:k + 3 * i + 3] for i in range(ns)]
        k += 3 * ns
        sg_ref = refs[k]
        k += 1
        big_out = [refs[k + 3 * i:k + 3 * i + 3] for i in range(nb)]
        k += 3 * nb
        small_out = [refs[k + 4 * i:k + 4 * i + 4] for i in range(ns)]

        for (w, g, m, v), (od, om, ov) in zip(big_in, big_out):
            od[...], om[...], ov[...] = _adamw_math(w[...], g[...], m[...], v[...])

        sg = sg_ref[...]
        lbp = small_in[2][0][...]
        lb = _lower_bound(lbp)
        t = sg[4:5, :] * lb * (1.0 - lb)
        grads = [jnp.concatenate([sg[0:1, :], sg[1:2, :]], axis=1),
                 jnp.concatenate([sg[2:3, :], sg[3:4, :]], axis=1),
                 jnp.concatenate([t, -t], axis=0),
                 sg[6:7, :], sg[7:8, 0:Q_RANK], sg[7:8, Q_RANK:Q_RANK + KV_RANK]]
        for (w, m, v), g, (og, od, om, ov) in zip(small_in, grads, small_out):
            og[...] = g
            od[...], om[...], ov[...] = _adamw_math(w[...], g, m[...], v[...])

    ins = [a for grp in big for a in grp] + [a for grp in small_w for a in grp] + [small_g]
    out_shape = ([jax.ShapeDtypeStruct(grp[0].shape, F32) for grp in big for _ in range(3)]
                 + [jax.ShapeDtypeStruct(grp[0].shape, F32) for grp in small_w for _ in range(4)])
    vm = pl.BlockSpec(memory_space=pltpu.VMEM)
    res = pl.pallas_call(
        body, name="adamw", in_specs=[vm] * len(ins), out_specs=[vm] * len(out_shape), out_shape=out_shape,
        compiler_params=pltpu.CompilerParams(vmem_limit_bytes=VMEM_LIMIT),
    )(*ins)
    big_res = [res[3 * i:3 * i + 3] for i in range(nb)]
    small_res = [res[3 * nb + 4 * i:3 * nb + 4 * i + 4] for i in range(ns)]
    return big_res, small_res


def _perm_weights(g_in_t, g_q, g_kv, g_out):
    w_in_t = g_in_t.reshape(D_IN, D_MODEL)
    wq = g_q.transpose(1, 0, 2)
    w_q_p = jnp.pad(wq, ((0, 0), (0, 0), (0, HEAD_LANES - NOPE - ROPE))).reshape(Q_RANK, N_HEADS * HEAD_LANES)
    wkv = g_kv.transpose(1, 0, 2)
    wk = jnp.pad(wkv[:, :, :NOPE], ((0, 0), (0, 0), (0, HEAD_LANES - NOPE))).reshape(KV_RANK, N_HEADS * HEAD_LANES)
    wv = wkv[:, :, NOPE:].reshape(KV_RANK, MLA_WIDTH)
    return w_in_t, w_q_p, jnp.concatenate([wk, wv], axis=1), g_out.reshape(D_MODEL, D_MODEL)


def _grad_slabs(dw_in_t, dw_q_p, dw_kv_p, dw_out):
    s_in = dw_in_t.reshape(N_DEV, D_IN // N_DEV, D_MODEL)
    s_q = dw_q_p.reshape(Q_RANK, N_HEADS, HEAD_LANES)[:, :, :NOPE + ROPE].transpose(1, 0, 2)
    hl = N_HEADS * HEAD_LANES
    dk = dw_kv_p[:, :hl].reshape(KV_RANK, N_HEADS, HEAD_LANES)[:, :, :NOPE]
    dv = dw_kv_p[:, hl:].reshape(KV_RANK, N_HEADS, V_DIM)
    s_kv = jnp.concatenate([dk, dv], axis=2).transpose(1, 0, 2)
    return s_in, s_q, s_kv, dw_out.reshape(N_DEV, D_MODEL // N_DEV, D_MODEL)


def _block_sizes(T):
    return min(256, T), min(256, T), min(512, T)


def kernel(x, positions, ln_g, w_in, q_a_norm_g, w_q_b, kv_a_norm_g, w_kv_b, hg_lower_bounds, hg_norm_g, w_out, final_norm_g, loss_target, m_ln_g, m_w_in, m_q_a_norm_g, m_w_q_b, m_kv_a_norm_g, m_w_kv_b, m_hg_lower_bounds, m_hg_norm_g, m_w_out, m_final_norm_g, v_ln_g, v_w_in, v_q_a_norm_g, v_w_q_b, v_kv_a_norm_g, v_w_kv_b, v_hg_lower_bounds, v_hg_norm_g, v_w_out, v_final_norm_g):
    T = x.shape[1]
    tm, tq, bt = _block_sizes(T)
    nq = T // tq
    xs, tgt = x[0], loss_target[0]
    pos_f = positions.astype(F32)
    fng = final_norm_g.reshape(1, D_MODEL)

    w_in_shard_t = w_in[0].T
    gathered, (c_t, s1_t, s2_t) = _all_gather_weights([w_in_shard_t, w_q_b[0], w_kv_b[0], w_out[0]], pos_f)
    w_in_t, w_q_p, w_kv_p, w_out_b = _perm_weights(*gathered)

    proj, h, qn, kvn, q, k, v = _fwd_in(xs, ln_g, w_in_t, q_a_norm_g, w_q_p, kv_a_norm_g, w_kv_p, c_t, s1_t, s2_t, bt)
    hl = N_HEADS * HEAD_LANES
    v_t = v.reshape(nq, tq, MLA_WIDTH).transpose(0, 2, 1)
    k_t = k.reshape(nq, tq, hl).transpose(0, 2, 1)
    q_t = q.reshape(nq, tq, hl).transpose(0, 2, 1)
    o_mla, lse = _attn_fwd_flat(k, q_t, v_t, tq)
    o_hg, states = _hgrn_fwd(proj, hg_lower_bounds)
    dx2, d_om, dsum, d_gm, d_oh, d_gh, loss_p, d_fng, d_hgn, dw_out = _top(
        xs, tgt, o_mla, o_hg, proj, w_out_b, hg_norm_g, fng, tm)
    dsum = dsum[:, :N_HEADS].T.reshape(N_HEADS, nq, 1, tq)
    do_t = d_om.reshape(nq, tq, hl).transpose(0, 2, 1)
    dq_t, dk, dv = _attn_bwd_flat(k, v, q_t, k_t, do_t, lse, dsum, tq)
    dq = dq_t.transpose(0, 2, 1).reshape(T, N_HEADS * HEAD_LANES)
    d_hq, d_hf, d_hi, d_lb = _hgrn_bwd(proj, hg_lower_bounds, d_oh, states)
    dx, dproj, d_lng, d_qg, d_kvg, dw_q_p, dw_kv_p = _bot(
        xs, dx2, proj, qn, kvn, dq, dk, dv, d_gm, d_hq, d_hf, d_hi, d_gh, c_t, s1_t, s2_t, w_in_t, w_q_p, w_kv_p,
        ln_g, q_a_norm_g, kv_a_norm_g, tm)
    dw_in_t = _dw_in_t(dproj, h, bt)

    small = jnp.concatenate([
        d_lng.reshape(2, SMALL_W), d_fng.reshape(2, SMALL_W), d_lb, loss_p, d_hgn,
        jnp.concatenate([d_qg, d_kvg, jnp.zeros((1, SMALL_W - Q_RANK - KV_RANK), F32)], axis=1)], axis=0)
    g_in, g_q, g_kv, g_out, small_sum = _reduce_scatter(list(_grad_slabs(dw_in_t, dw_q_p, dw_kv_p, dw_out)), small)

    big = [(w_in_shard_t, g_in, m_w_in[0].T, v_w_in[0].T), (w_q_b[0], g_q, m_w_q_b[0], v_w_q_b[0]),
           (w_kv_b[0], g_kv, m_w_kv_b[0], v_w_kv_b[0]), (w_out[0], g_out, m_w_out[0], v_w_out[0])]
    small_w = [(ln_g, m_ln_g, v_ln_g),
               (fng, m_final_norm_g.reshape(1, D_MODEL), v_final_norm_g.reshape(1, D_MODEL)),
               (hg_lower_bounds, m_hg_lower_bounds, v_hg_lower_bounds), (hg_norm_g, m_hg_norm_g, v_hg_norm_g),
               (q_a_norm_g, m_q_a_norm_g, v_q_a_norm_g), (kv_a_norm_g, m_kv_a_norm_g, v_kv_a_norm_g)]
    big_res, small_res = _adamw(big, small_w, small_sum)

    loss = small_sum[5, 0]
    (r_in, r_q, r_kv, r_out) = big_res
    (s_ln, s_fn, s_lb, s_hgn, s_qg, s_kvg) = small_res
    flat = lambda t: t.reshape(D_MODEL)
    lead = lambda t: t[None]
    grads = [s_ln[0], lead(g_in.T), s_qg[0], lead(g_q), s_kvg[0], lead(g_kv), s_lb[0], s_hgn[0], lead(g_out), flat(s_fn[0])]

    def pick(i):
        return [s_ln[i + 1], lead(r_in[i].T), s_qg[i + 1], lead(r_q[i]), s_kvg[i + 1], lead(r_kv[i]), s_lb[i + 1],
                s_hgn[i + 1], lead(r_out[i]), flat(s_fn[i + 1])]

    return (loss, dx[None], *grads, *pick(0), *pick(1), *pick(2))
```

```python
import math

import numpy as np
import jax
import jax.numpy as jnp
from jax import lax
from jax.experimental import pallas as pl
from jax.experimental.pallas import tpu as pltpu

F32 = jnp.float32
BF16 = jnp.bfloat16

D_MODEL = 1024
N_HEADS = 8
NOPE = 64
ROPE = 32
HALF_ROPE = ROPE // 2
V_DIM = 64
Q_RANK = 256
KV_RANK = 128
MLA_WIDTH = N_HEADS * V_DIM
HG_HEADS = 4
HG_DIM = 128
HG_WIDTH = HG_HEADS * HG_DIM
CHUNK = 64
SUB = 16
D_IN = 2976
D_PERM = 3072
ROPE_THETA = 10000.0
EPS = 1e-6
N_DEV = 8
LANES = 128
HEAD_LANES = 128

P_GM, P_HQ, P_HF, P_HI, P_GH, P_QL, P_KVL, P_KR = 0, 512, 1024, 1536, 2048, 2560, 2816, 2944
R_QL, R_KVL, R_KR, R_MAIN = (0, 256), (256, 384), (384, 416), (416, 2976)
ROPE_LO = NOPE
SCALE = 1.0 / math.sqrt(NOPE + ROPE)

ADAM_LR = 0.001
ADAM_B1 = 0.9
ADAM_B2 = 0.999
ADAM_EPS = 1e-08
ADAM_WD = 0.01
ADAM_STEP = 10

VMEM_LIMIT = 56 * 1024 * 1024
MESH = pl.DeviceIdType.MESH

NT = (((1,), (1,)), ((), ()))
TN = (((0,), (0,)), ((), ()))


def _params(n_grid=0, **kw):
    sem = ("arbitrary",) * n_grid if n_grid else None
    return pltpu.CompilerParams(dimension_semantics=sem, vmem_limit_bytes=VMEM_LIMIT, **kw)


def _dot(a, b):
    return jnp.dot(a, b, preferred_element_type=F32)


def _dot_nt(a, b):
    return lax.dot_general(a, b, NT, preferred_element_type=F32)


def _dot_tn(a, b):
    return lax.dot_general(a, b, TN, preferred_element_type=F32)


def _sigmoid(x):
    return 1.0 / (1.0 + jnp.exp(-x))


def _rope_fwd(x, c, s1, s2):
    return x * c + pltpu.roll(x, LANES - HALF_ROPE, 1) * s1 + pltpu.roll(x, HALF_ROPE, 1) * s2


def _rope_bwd(dy, c, s1, s2):
    return dy * c - pltpu.roll(dy, LANES - HALF_ROPE, 1) * s1 - pltpu.roll(dy, HALF_ROPE, 1) * s2


def _in_proj_rows(wt_ref):
    kr = wt_ref[R_KR[0]:R_KR[1], :]
    pad = lambda n: jnp.zeros((n, D_MODEL), kr.dtype)
    return ((P_GM, wt_ref[R_MAIN[0]:R_MAIN[1], :]), (P_QL, wt_ref[R_QL[0]:R_QL[1], :]),
            (P_KVL, wt_ref[R_KVL[0]:R_KVL[1], :]),
            (P_KR, jnp.concatenate([pad(ROPE_LO), kr, pad(LANES - ROPE_LO - ROPE)], axis=0)))


def _full(shape):
    n = len(shape)
    return pl.BlockSpec(shape, lambda *_: (0,) * n)


ROPE_BLOCK = 512


def _rope_constants():
    inv = (np.float32(ROPE_THETA) ** (-np.arange(HALF_ROPE, dtype=np.float32) / np.float32(HALF_ROPE))).astype(np.float32)
    place = np.zeros((3, HALF_ROPE, LANES), np.float32)
    for i in range(HALF_ROPE):
        place[0, i, ROPE_LO + i] = place[0, i, ROPE_LO + HALF_ROPE + i] = 1.0
        place[1, i, ROPE_LO + i] = -1.0
        place[2, i, ROPE_LO + HALF_ROPE + i] = 1.0
    base = np.ones((1, LANES), np.float32)
    base[0, ROPE_LO:ROPE_LO + ROPE] = 0.0
    return jnp.asarray(inv.reshape(HALF_ROPE, 1)), jnp.asarray(place), jnp.asarray(base)


def _rope_block(pos, inv, place_ref, base):
    ang = inv * pos
    cos, sin = jnp.cos(ang), jnp.sin(ang)

    def put(v, k):
        return lax.dot_general(v, place_ref[k], TN, precision=lax.Precision.HIGHEST, preferred_element_type=F32)

    return put(cos, 0) + base, put(sin, 1), put(sin, 2)


def _all_gather_weights(shards, pos_f):
    n = len(shards)
    T = pos_f.shape[1]
    rb = min(ROPE_BLOCK, T)

    def body(*refs):
        ins, (pos_ref, inv_ref, place_ref, base_ref) = refs[:n], refs[n:n + 4]
        outs, tables = refs[n + 4:2 * n + 4], refs[2 * n + 4:2 * n + 7]
        send_sems, recv_sems = refs[2 * n + 7], refs[2 * n + 8]
        x, y, c = lax.axis_index("x"), lax.axis_index("y"), lax.axis_index("c")
        me, sibling = (x, y, c), (x, y, 1 - c)
        chips = [(1 - x, y), (x, 1 - y), (1 - x, 1 - y)]

        def idx(d):
            return 4 * d[0] + 2 * d[1] + d[2]

        def copy(a, k, block, to):
            rows = outs[a].at[idx(block)]
            return pltpu.make_async_remote_copy(src_ref=rows, dst_ref=rows, send_sem=send_sems.at[a, k],
                                                recv_sem=recv_sems.at[a, k], device_id=to, device_id_type=MESH)

        for a in range(n):
            outs[a][idx(me)] = ins[a][...].astype(BF16)
        first = []
        for a in range(n):
            first.append(copy(a, 0, me, sibling))
            first += [copy(a, 1 + j, me, (*chip, c)) for j, chip in enumerate(chips)]
        for cp in first:
            cp.start()
        for r0 in range(0, T, rb):
            for ref, tab in zip(tables, _rope_block(pos_ref[:, r0:r0 + rb], inv_ref[...], place_ref, base_ref[...])):
                ref[r0:r0 + rb, :] = tab
        passed = []
        for j, chip in enumerate(chips):
            for a in range(n):
                copy(a, 1 + j, (*chip, c), me).wait_recv()
                cp = copy(a, 4 + j, (*chip, c), sibling)
                cp.start()
                passed.append(cp)
        for a in range(n):
            copy(a, 0, sibling, me).wait_recv()
            for j, chip in enumerate(chips):
                copy(a, 4 + j, (*chip, 1 - c), me).wait_recv()
        for cp in first + passed:
            cp.wait_send()

    vm = pl.BlockSpec(memory_space=pltpu.VMEM)
    res = pl.pallas_call(
        body, name="all_gather_weights",
        in_specs=[vm] * (n + 4), out_specs=[vm] * (n + 3),
        out_shape=[jax.ShapeDtypeStruct((N_DEV,) + s.shape, BF16) for s in shards]
        + [jax.ShapeDtypeStruct((T, LANES), F32)] * 3,
        scratch_shapes=[pltpu.SemaphoreType.DMA((n, 7)), pltpu.SemaphoreType.DMA((n, 7))],
        compiler_params=pltpu.CompilerParams(vmem_limit_bytes=VMEM_LIMIT),
    )(*shards, pos_f, *_rope_constants())
    return res[:n], res[n:]


def _fwd_in(x, ln_g, w_in_t, q_g, w_q_p, kv_g, w_kv_p, c_t, s1_t, s2_t, tm):
    T = x.shape[0]

    def body(x_ref, lng_ref, win_ref, qg_ref, wq_ref, kvg_ref, wkv_ref, c_ref, s1_ref, s2_ref,
             proj_ref, h_ref, qn_ref, kvn_ref, q_ref, k_ref, v_ref):
        xv = x_ref[...]
        r = lax.rsqrt(jnp.mean(xv * xv, axis=-1, keepdims=True) + EPS)
        h = (xv * r * lng_ref[...]).astype(BF16)
        h_ref[...] = h
        for col, rows in _in_proj_rows(win_ref):
            proj_ref[:, col:col + rows.shape[0]] = _dot_nt(h, rows)
        c, s1, s2 = c_ref[...], s1_ref[...], s2_ref[...]

        ql = proj_ref[:, P_QL:P_QL + Q_RANK]
        rq = lax.rsqrt(jnp.mean(ql * ql, axis=-1, keepdims=True) + EPS)
        qn = (ql * rq * qg_ref[...]).astype(BF16)
        qn_ref[...] = qn
        q = _dot(qn, wq_ref[...])
        for hd in range(N_HEADS):
            sl = slice(hd * HEAD_LANES, (hd + 1) * HEAD_LANES)
            q_ref[:, sl] = _rope_fwd(q[:, sl], c, s1, s2).astype(BF16)

        kvl = proj_ref[:, P_KVL:P_KVL + KV_RANK]
        rk = lax.rsqrt(jnp.mean(kvl * kvl, axis=-1, keepdims=True) + EPS)
        kvn = (kvl * rk * kvg_ref[...]).astype(BF16)
        kvn_ref[...] = kvn
        kv = _dot(kvn, wkv_ref[...])
        kpe = _rope_fwd(proj_ref[:, P_KR:P_KR + LANES], c, s1, s2)
        for hd in range(N_HEADS):
            sl = slice(hd * HEAD_LANES, (hd + 1) * HEAD_LANES)
            k_ref[:, sl] = (kv[:, sl] + kpe).astype(BF16)
        v_ref[...] = kv[:, N_HEADS * HEAD_LANES:].astype(BF16)

    def row(w):
        return pl.BlockSpec((tm, w), lambda i: (i, 0))

    outs = [(D_PERM, F32), (D_MODEL, BF16), (Q_RANK, BF16), (KV_RANK, BF16),
            (N_HEADS * HEAD_LANES, BF16), (N_HEADS * HEAD_LANES, BF16), (MLA_WIDTH, BF16)]
    return pl.pallas_call(
        body, name="fwd_in", grid=(T // tm,),
        in_specs=[row(D_MODEL), _full((1, D_MODEL)), _full((D_IN, D_MODEL)), _full((1, Q_RANK)),
                  _full((Q_RANK, N_HEADS * HEAD_LANES)), _full((1, KV_RANK)),
                  _full((KV_RANK, N_HEADS * HEAD_LANES + MLA_WIDTH)), row(LANES), row(LANES), row(LANES)],
        out_specs=[row(w) for w, _ in outs],
        out_shape=[jax.ShapeDtypeStruct((T, w), dt) for w, dt in outs],
        compiler_params=_params(1),
    )(x, ln_g, w_in_t, q_g, w_q_p, kv_g, w_kv_p, c_t, s1_t, s2_t)


LOG2E = 1.4426950408889634
SCALE2 = SCALE * LOG2E


def _causal(tq):
    r = lax.broadcasted_iota(jnp.int32, (tq, tq), 0)
    c = lax.broadcasted_iota(jnp.int32, (tq, tq), 1)
    return r <= c


MASKED = -1e30


def _causal_bias(bias_ref, tq):
    bias_ref[0] = jnp.zeros((tq, tq), F32)
    bias_ref[1] = jnp.where(_causal(tq), 0.0, MASKED)


def _tile_tables(nq, by_query):
    if by_query:
        pairs = [(j, i) for i in range(nq) for j in range(i + 1)]
    else:
        pairs = [(j, i) for j in range(nq) for i in range(nq - 1, j - 1, -1)]
    pairs.append(pairs[-1])
    jj, ii = np.array(pairs, np.int32).T
    return jnp.asarray(jj), jnp.asarray(ii), len(pairs) - 1


ATTN_TRIP = 8


def _walk_tiles(n, products, tile, flush, buf_a, buf_b):
    bufs = (buf_a, buf_b)
    products(0, buf_a)

    def trip(r, carry):
        for u in range(ATTN_TRIP):
            products(ATTN_TRIP * r + u + 1, bufs[(u + 1) % 2])
            tile(ATTN_TRIP * r + u, bufs[u % 2])
        for u in range(ATTN_TRIP):
            flush(ATTN_TRIP * r + u)
        return carry

    lax.fori_loop(0, n // ATTN_TRIP, trip, 0)
    rest = n - n % ATTN_TRIP
    for u in range(n % ATTN_TRIP):
        if rest + u + 1 < n:
            products(rest + u + 1, bufs[(u + 1) % 2])
        tile(rest + u, bufs[u % 2])
    for u in range(n % ATTN_TRIP):
        flush(rest + u)


def _attn_fwd_flat(k, q_t, v_t, tq):
    T = k.shape[0]
    nq = T // tq
    jj, ii, n = _tile_tables(nq, True)
    heads = [slice(hh * HEAD_LANES, (hh + 1) * HEAD_LANES) for hh in range(2)]

    def body(jj_ref, ii_ref, k_ref, qt_ref, vt_ref, o_ref, lse_ref, sa_ref, sb_ref, m_ref, l_ref, acc_ref, bias_ref):
        def reset(st):
            m_ref[st] = jnp.full(m_ref.shape[1:], MASKED, F32)
            l_ref[st] = jnp.zeros(l_ref.shape[1:], F32)
            acc_ref[st] = jnp.zeros(acc_ref.shape[1:], F32)

        _causal_bias(bias_ref, tq)
        for st in range(ATTN_TRIP):
            reset(st)

        def products(t, buf):
            j, i = jj_ref[t], ii_ref[t]
            kj = k_ref[pl.ds(pl.multiple_of(j * tq, tq), tq), :]
            for hh, sl in enumerate(heads):
                buf[hh] = _dot(kj[:, sl], qt_ref[i, sl, :])

        def tile(t, buf):
            j, i = jj_ref[t], ii_ref[t]
            vt = vt_ref[j]
            bias = bias_ref.at[(j == i).astype(jnp.int32)]
            st = i % ATTN_TRIP
            for hh in range(2):
                s = buf[hh] * SCALE2 + bias[...]
                m = m_ref[st, hh]
                m_new = jnp.maximum(m, jnp.max(s, axis=0, keepdims=True))
                alpha = jnp.exp2(m - m_new)
                p = jnp.exp2(s - m_new)
                m_ref[st, hh] = m_new
                l_ref[st, hh] = alpha * l_ref[st, hh] + jnp.sum(p, axis=0, keepdims=True)
                acc_ref[st, hh] = alpha * acc_ref[st, hh] + _dot(vt, p.astype(BF16))

        def flush(t):
            j, i = jj_ref[t], ii_ref[t]

            @pl.when(j == i)
            def _():
                st = i % ATTN_TRIP
                first = lax.broadcasted_iota(jnp.int32, (LANES, tq), 0) < V_DIM
                out = jnp.where(first, acc_ref[st, 0] / l_ref[st, 0], acc_ref[st, 1] / l_ref[st, 1])
                o_ref[pl.ds(pl.multiple_of(i * tq, tq), tq), :] = out.T
                for hh in range(2):
                    lse_ref[hh, i] = m_ref[st, hh] + jnp.log2(l_ref[st, hh])
                reset(st)

        _walk_tiles(n, products, tile, flush, sa_ref, sb_ref)

    smem = pl.BlockSpec(memory_space=pltpu.SMEM)
    return pl.pallas_call(
        body, name="attn_fwd", grid=(N_HEADS // 2,),
        in_specs=[smem, smem,
                  pl.BlockSpec((T, 2 * HEAD_LANES), lambda p: (0, p)),
                  pl.BlockSpec((nq, 2 * HEAD_LANES, tq), lambda p: (0, p, 0)),
                  pl.BlockSpec((nq, LANES, tq), lambda p: (0, p, 0))],
        out_specs=[pl.BlockSpec((T, LANES), lambda p: (0, p)),
                   pl.BlockSpec((2, nq, 1, tq), lambda p: (p, 0, 0, 0))],
        out_shape=[jax.ShapeDtypeStruct((T, MLA_WIDTH), F32), jax.ShapeDtypeStruct((N_HEADS, nq, 1, tq), F32)],
        scratch_shapes=[pltpu.VMEM((2, tq, tq), F32), pltpu.VMEM((2, tq, tq), F32),
                        pltpu.VMEM((ATTN_TRIP, 2, 1, tq), F32), pltpu.VMEM((ATTN_TRIP, 2, 1, tq), F32),
                        pltpu.VMEM((ATTN_TRIP, 2, LANES, tq), F32), pltpu.VMEM((2, tq, tq), F32)],
        compiler_params=_params(1),
    )(jj, ii, k, q_t, v_t)


def _attn_bwd_flat(k, v, q_t, k_t, do_t, lse, dsum, tq):
    T = k.shape[0]
    nq = T // tq
    jj, ii, n = _tile_tables(nq, False)
    heads = [slice(hh * HEAD_LANES, (hh + 1) * HEAD_LANES) for hh in range(2)]

    def body(jj_ref, ii_ref, k_ref, v_ref, qt_ref, kt_ref, dot_ref, lse_ref, dsum_ref, dqt_ref, dk_ref, dv_ref,
             ba_ref, bb_ref, dkt_ref, dvt_ref, bias_ref):
        _causal_bias(bias_ref, tq)
        dqt_ref[...] = jnp.zeros_like(dqt_ref)
        dkt_ref[...] = jnp.zeros_like(dkt_ref)
        dvt_ref[...] = jnp.zeros_like(dvt_ref)

        def products(t, buf):
            j, i = jj_ref[t], ii_ref[t]
            rows = pl.ds(pl.multiple_of(j * tq, tq), tq)
            for hh, sl in enumerate(heads):
                buf[hh] = _dot(k_ref[rows, sl], qt_ref[i, sl, :])
                buf[2 + hh] = _dot(v_ref[rows, :], dot_ref[i, sl, :])

        def tile(t, buf):
            j, i = jj_ref[t], ii_ref[t]
            bias = bias_ref.at[(j == i).astype(jnp.int32)]
            st = j % ATTN_TRIP
            dv_new = None
            for hh, sl in enumerate(heads):
                p = jnp.exp2(buf[hh] * SCALE2 + bias[...] - lse_ref[hh, i])
                ds = (p * (buf[2 + hh] - dsum_ref[hh, i]) * SCALE).astype(BF16)
                dv_h = _dot_nt(dot_ref[i, sl, :], p.astype(BF16))
                dv_new = dv_h if dv_new is None else dv_new + dv_h
                dkt_ref[st, sl, :] += _dot_nt(qt_ref[i, sl, :], ds)
                dqt_ref[i, sl, :] += _dot(kt_ref[j, sl, :], ds)
            dvt_ref[st] += dv_new

        def flush(t):
            j, i = jj_ref[t], ii_ref[t]

            @pl.when(j == i)
            def _():
                st = j % ATTN_TRIP
                rows = pl.ds(pl.multiple_of(j * tq, tq), tq)
                dk_ref[rows, :] = dkt_ref[st].T
                dv_ref[rows, :] = dvt_ref[st].T
                dkt_ref[st] = jnp.zeros(dkt_ref.shape[1:], F32)
                dvt_ref[st] = jnp.zeros(dvt_ref.shape[1:], F32)

        _walk_tiles(n, products, tile, flush, ba_ref, bb_ref)

    smem = pl.BlockSpec(memory_space=pltpu.SMEM)
    stat = pl.BlockSpec((2, nq, 1, tq), lambda p: (p, 0, 0, 0))
    blocks_t = pl.BlockSpec((nq, 2 * HEAD_LANES, tq), lambda p: (0, p, 0))
    return pl.pallas_call(
        body, name="attn_bwd", grid=(N_HEADS // 2,),
        in_specs=[smem, smem,
                  pl.BlockSpec((T, 2 * HEAD_LANES), lambda p: (0, p)),
                  pl.BlockSpec((T, LANES), lambda p: (0, p)),
                  blocks_t, blocks_t, blocks_t, stat, stat],
        out_specs=[blocks_t,
                   pl.BlockSpec((T, 2 * HEAD_LANES), lambda p: (0, p)),
                   pl.BlockSpec((T, LANES), lambda p: (0, p))],
        out_shape=[jax.ShapeDtypeStruct((nq, N_HEADS * HEAD_LANES, tq), F32),
                   jax.ShapeDtypeStruct((T, N_HEADS * HEAD_LANES), F32),
                   jax.ShapeDtypeStruct((T, MLA_WIDTH), F32)],
        scratch_shapes=[pltpu.VMEM((4, tq, tq), F32), pltpu.VMEM((4, tq, tq), F32),
                        pltpu.VMEM((ATTN_TRIP, 2 * HEAD_LANES, tq), F32), pltpu.VMEM((ATTN_TRIP, LANES, tq), F32),
                        pltpu.VMEM((2, tq, tq), F32)],
        compiler_params=_params(1),
    )(jj, ii, k, v, q_t, k_t, do_t, lse, dsum)


def _lower_bound(lbp):
    a, b = lbp[0:1, :], lbp[1:2, :]
    mx = jnp.maximum(a, b)
    ea, eb = jnp.exp(a - mx), jnp.exp(b - mx)
    return ea / (ea + eb)


def _tri(lower):
    r = lax.broadcasted_iota(jnp.int32, (CHUNK, CHUNK), 0)
    c = lax.broadcasted_iota(jnp.int32, (CHUNK, CHUNK), 1)
    return (c <= r) if lower else (c >= r)


def _running_sum(x, from_end):
    row = lax.broadcasted_iota(jnp.int32, x.shape, 0)
    step = 1
    while step < CHUNK:
        if from_end:
            x = x + jnp.where(row < CHUNK - step, pltpu.roll(x, CHUNK - step, 0), 0.0)
        else:
            x = x + jnp.where(row >= step, pltpu.roll(x, step, 0), 0.0)
        step *= 2
    return x


def _hg_gates(hq, hf, lb):
    sq = _sigmoid(hq)
    sf = _sigmoid(hf)
    f = lb + (1.0 - lb) * sf
    g = jnp.log(f)
    gcum = _running_sum(g, False)
    return sq, sf, f, hq * sq, 1.0 - f, gcum


def _head(x, hd):
    return x[:, hd * HG_DIM:(hd + 1) * HG_DIM]


def _all_heads(fn):
    return jnp.concatenate([fn(hd) for hd in range(HG_HEADS)], axis=1)


def _hg_blocks(q, kk, gcum):
    rowi = lax.broadcasted_iota(jnp.int32, gcum.shape, 0)
    out = []
    for blk in range(CHUNK // SUB):
        lo, hi = blk * SUB, (blk + 1) * SUB
        gb = gcum[lo - 1:lo, :] if blk else jnp.zeros_like(gcum[0:1, :])
        eq = jnp.exp(gcum[lo:hi, :] - gb)
        ek = jnp.exp(jnp.where(rowi < hi, gb - gcum, 0.0))
        out.append((eq, ek, (q[lo:hi, :] * eq).astype(BF16), (kk * ek).astype(BF16)))
    return out


def _hg_scores(blocks):
    out = []
    for hd in range(HG_HEADS):
        a = jnp.concatenate([_dot_nt(_head(qb, hd), _head(kb, hd)) for _, _, qb, kb in blocks], axis=0)
        out.append(jnp.where(_tri(True), a, 0.0))
    return out


HG_STEP_CHUNKS = 8


def _hgrn_fwd(proj, lbp):
    T = proj.shape[0]
    nc = T // CHUNK
    ns = min(HG_STEP_CHUNKS, nc)
    rows = ns * CHUNK

    def body(hq_ref, hf_ref, hi_ref, lbp_ref, o_ref, st_ref, state):
        @pl.when(pl.program_id(0) == 0)
        def _():
            state[...] = jnp.zeros_like(state)

        lb = _lower_bound(lbp_ref[...])
        work = []
        for c in range(ns):
            r = slice(c * CHUNK, (c + 1) * CHUNK)
            _, _, _, q, kk, gcum = _hg_gates(hq_ref[r, :], hf_ref[r, :], lb)
            vb = hi_ref[r, :].astype(BF16)
            a = _hg_scores(_hg_blocks(q, kk, gcum))
            gend = gcum[CHUNK - 1:CHUNK, :]
            qgb = (q * jnp.exp(gcum)).astype(BF16)
            kgeb = (kk * jnp.exp(gend - gcum)).astype(BF16)
            intra = [_dot(a[hd].astype(BF16), _head(vb, hd)) for hd in range(HG_HEADS)]
            update = [_dot_tn(_head(vb, hd), _head(kgeb, hd)) for hd in range(HG_HEADS)]
            work.append((qgb, jnp.exp(gend), intra, update))
        for hd in range(HG_HEADS):
            st = state[hd]
            for c, (qgb, egend, intra, update) in enumerate(work):
                st_ref[c, hd] = st
                o_ref[c * CHUNK:(c + 1) * CHUNK, hd * HG_DIM:(hd + 1) * HG_DIM] = (
                    intra[hd] + _dot_nt(_head(qgb, hd), st.astype(BF16)))
                st = st * _head(egend, hd) + update[hd]
            state[hd] = st

    def col(cb):
        return pl.BlockSpec((rows, HG_WIDTH), lambda i: (i, cb))

    return pl.pallas_call(
        body, name="hgrn_fwd", grid=(nc // ns,),
        in_specs=[col(P_HQ // HG_WIDTH), col(P_HF // HG_WIDTH), col(P_HI // HG_WIDTH), _full((2, HG_WIDTH))],
        out_specs=[pl.BlockSpec((rows, HG_WIDTH), lambda i: (i, 0)),
                   pl.BlockSpec((ns, HG_HEADS, HG_DIM, HG_DIM), lambda i: (i, 0, 0, 0))],
        out_shape=[jax.ShapeDtypeStruct((T, HG_WIDTH), F32),
                   jax.ShapeDtypeStruct((nc, HG_HEADS, HG_DIM, HG_DIM), F32)],
        scratch_shapes=[pltpu.VMEM((HG_HEADS, HG_DIM, HG_DIM), F32)],
        compiler_params=_params(1),
    )(proj, proj, proj, lbp)


def _hgrn_bwd(proj, lbp, do_hg, states):
    T = proj.shape[0]
    nc = T // CHUNK
    ns = min(HG_STEP_CHUNKS, nc)
    rows = ns * CHUNK
    steps = nc // ns

    def body(hq_ref, hf_ref, hi_ref, lbp_ref, do_ref, st_ref, dhq_ref, dhf_ref, dhi_ref, dlb_ref, dstate):
        @pl.when(pl.program_id(0) == 0)
        def _():
            dstate[...] = jnp.zeros_like(dstate)
            dlb_ref[...] = jnp.zeros_like(dlb_ref)

        lb = _lower_bound(lbp_ref[...])

        dst_all = [dstate[hd] for hd in range(HG_HEADS)]
        dlb = jnp.zeros_like(lb)
        last = lax.broadcasted_iota(jnp.int32, (CHUNK, HG_WIDTH), 0) == CHUNK - 1
        for c in reversed(range(ns)):
            r = slice(c * CHUNK, (c + 1) * CHUNK)
            hq = hq_ref[r, :]
            sq, sf, f, q, kk, gcum = _hg_gates(hq, hf_ref[r, :], lb)
            vb = hi_ref[r, :].astype(BF16)
            dob = do_ref[r, :].astype(BF16)
            blocks = _hg_blocks(q, kk, gcum)
            a = _hg_scores(blocks)
            gend = gcum[CHUNK - 1:CHUNK, :]
            eg, egend, ekend = jnp.exp(gcum), jnp.exp(gend), jnp.exp(gend - gcum)
            qg, kge = q * eg, kk * ekend
            qgb, kgeb = qg.astype(BF16), kge.astype(BF16)

            dv, dqg, dkge, st_dst, dq_blk, dk_blk = [], [], [], [], [], []
            for hd in range(HG_HEADS):
                st = st_ref[c, hd]
                dst = dst_all[hd]
                dstb = dst.astype(BF16)
                do_h, v_h = _head(dob, hd), _head(vb, hd)
                dv.append(_dot_tn(a[hd].astype(BF16), do_h) + _dot_nt(_head(kgeb, hd), dstb))
                da = jnp.where(_tri(True), _dot_nt(do_h, v_h), 0.0).astype(BF16)
                dqg.append(_dot(do_h, st.astype(BF16)))
                dkge.append(_dot(v_h, dstb))
                st_dst.append(jnp.sum(st * dst, axis=0, keepdims=True))
                dst_all[hd] = _dot_tn(do_h, _head(qgb, hd)) + dst * _head(egend, hd)
                dq_blk.append([_dot(da[b * SUB:(b + 1) * SUB, :], _head(kb, hd)) for b, (_, _, _, kb) in enumerate(blocks)])
                dk_blk.append([_dot_tn(da[b * SUB:(b + 1) * SUB, :], _head(qb, hd)) for b, (_, _, qb, _) in enumerate(blocks)])
            dv, dqg, dkge, st_dst = (jnp.concatenate(t, axis=1) for t in (dv, dqg, dkge, st_dst))

            dq_a, dg_q = [], []
            dk_a, dg_k = jnp.zeros_like(gcum), jnp.zeros_like(gcum)
            for b, (eq, ek, qb, kb) in enumerate(blocks):
                dq_b = _all_heads(lambda hd: dq_blk[hd][b])
                dk_b = _all_heads(lambda hd: dk_blk[hd][b])
                dq_a.append(dq_b * eq)
                dk_a = dk_a + dk_b * ek
                dg_q.append(qb.astype(F32) * dq_b)
                dg_k = dg_k + kb.astype(F32) * dk_b
            dq_a = jnp.concatenate(dq_a, axis=0)

            dgend = st_dst * egend + jnp.sum(dkge * kge, axis=0, keepdims=True)
            dq = dq_a + dqg * eg
            dk = dk_a + dkge * ekend
            dgc = jnp.concatenate(dg_q, axis=0) - dg_k + dqg * qg - dkge * kge + jnp.where(last, dgend, 0.0)
            dg = _running_sum(dgc, True)
            df = dg / f - dk
            dhf_ref[r, :] = df * (1.0 - lb) * sf * (1.0 - sf)
            dlb = dlb + jnp.sum(df * (1.0 - sf), axis=0, keepdims=True)
            dhq_ref[r, :] = dq * (sq * (1.0 + hq * (1.0 - sq)))
            dhi_ref[r, :] = dv
        for hd in range(HG_HEADS):
            dstate[hd] = dst_all[hd]
        dlb_ref[...] += dlb

    def col(cb):
        return pl.BlockSpec((rows, HG_WIDTH), lambda i: (steps - 1 - i, cb))

    grad = jax.ShapeDtypeStruct((T, HG_WIDTH), F32)
    return pl.pallas_call(
        body, name="hgrn_bwd", grid=(steps,),
        in_specs=[col(P_HQ // HG_WIDTH), col(P_HF // HG_WIDTH), col(P_HI // HG_WIDTH), _full((2, HG_WIDTH)),
                  col(0), pl.BlockSpec((ns, HG_HEADS, HG_DIM, HG_DIM), lambda i: (steps - 1 - i, 0, 0, 0))],
        out_specs=[col(0), col(0), col(0), _full((1, HG_WIDTH))],
        out_shape=[grad, grad, grad, jax.ShapeDtypeStruct((1, HG_WIDTH), F32)],
        scratch_shapes=[pltpu.VMEM((HG_HEADS, HG_DIM, HG_DIM), F32)],
        compiler_params=_params(1),
    )(proj, proj, proj, lbp, do_hg, states)


def _top(x, tgt, o_mla, o_hg, proj, w_out, hg_norm_g, final_g, tm):
    T = x.shape[0]

    def body(x_ref, tgt_ref, om_ref, oh_ref, gm_ref, gh_ref, wout_ref, hgn_ref, fng_ref,
             dx2_ref, dom_ref, dsum_ref, dgm_ref, doh_ref, dgh_ref, loss_ref, dfng_ref, dhgn_ref, dwout_ref, ycat_ref):
        @pl.when(pl.program_id(0) == 0)
        def _():
            for ref in (loss_ref, dfng_ref, dhgn_ref, dwout_ref):
                ref[...] = jnp.zeros_like(ref)

        gm, om = gm_ref[...], om_ref[...]
        sgm = _sigmoid(gm)
        silu_m = gm * sgm
        gh, oh, gam = gh_ref[...], oh_ref[...], hgn_ref[...]
        sgh = _sigmoid(gh)
        silu_h = gh * sgh
        rr, nn = [], []
        for hd in range(HG_HEADS):
            oh_h = oh[:, hd * HG_DIM:(hd + 1) * HG_DIM]
            r_h = lax.rsqrt(jnp.mean(oh_h * oh_h, axis=-1, keepdims=True) + EPS)
            rr.append(r_h)
            nn.append(oh_h * r_h)
        n = jnp.concatenate(nn, axis=1)
        ng = n * gam
        ycat_ref[:, :MLA_WIDTH] = (om * silu_m).astype(BF16)
        ycat_ref[:, MLA_WIDTH:] = (ng * silu_h).astype(BF16)
        wout = wout_ref[...]
        x2 = x_ref[...] + _dot(ycat_ref[...], wout)
        r = lax.rsqrt(jnp.mean(x2 * x2, axis=-1, keepdims=True) + EPS)
        xh = x2 * r
        fng = fng_ref[...]
        err = xh * fng - tgt_ref[...]
        loss_ref[...] += 0.5 * jnp.sum(jnp.mean(err * err, axis=-1, keepdims=True), axis=0, keepdims=True)
        dout = err * (1.0 / D_MODEL)
        dfng_ref[...] += jnp.sum(dout * xh, axis=0, keepdims=True)
        dxh = dout * fng
        dx2 = r * (dxh - xh * jnp.mean(dxh * xh, axis=-1, keepdims=True))
        dx2_ref[...] = dx2
        dx2b = dx2.astype(BF16)
        dwout_ref[...] += _dot_tn(ycat_ref[...], dx2b)
        dycat = _dot_nt(dx2b, wout)
        dym, dyh = dycat[:, :MLA_WIDTH], dycat[:, MLA_WIDTH:]
        dom = dym * silu_m
        first = lax.broadcasted_iota(jnp.int32, (tm, LANES), 1) < V_DIM
        for pp in range(N_HEADS // 2):
            pair = dom[:, pp * LANES:(pp + 1) * LANES]
            dom_ref[:, 2 * pp * HEAD_LANES:(2 * pp + 1) * HEAD_LANES] = jnp.where(first, pair, 0.0).astype(BF16)
            dom_ref[:, (2 * pp + 1) * HEAD_LANES:(2 * pp + 2) * HEAD_LANES] = jnp.where(first, 0.0, pair).astype(BF16)
        head_of = lax.broadcasted_iota(jnp.int32, (MLA_WIDTH, LANES), 0) // V_DIM
        pick = (head_of == lax.broadcasted_iota(jnp.int32, (MLA_WIDTH, LANES), 1)).astype(F32)
        dsum_ref[...] = jnp.dot(dom * om, pick, precision=lax.Precision.HIGHEST, preferred_element_type=F32)
        dgm_ref[...] = dym * om * (sgm * (1.0 + gm * (1.0 - sgm)))
        dgh_ref[...] = dyh * ng * (sgh * (1.0 + gh * (1.0 - sgh)))
        dng = dyh * silu_h
        dhgn_ref[...] += jnp.sum(dng * n, axis=0, keepdims=True)
        dn = dng * gam
        for hd in range(HG_HEADS):
            sl = slice(hd * HG_DIM, (hd + 1) * HG_DIM)
            dn_h, n_h = dn[:, sl], nn[hd]
            doh_ref[:, sl] = rr[hd] * (dn_h - n_h * jnp.mean(dn_h * n_h, axis=-1, keepdims=True))

    def row(w, cb=0):
        return pl.BlockSpec((tm, w), lambda i: (i, cb))

    outs = [(D_MODEL, F32), (N_HEADS * HEAD_LANES, BF16), (LANES, F32), (MLA_WIDTH, F32), (HG_WIDTH, F32), (HG_WIDTH, F32)]
    small = [(1, SMALL_W), (1, D_MODEL), (1, HG_WIDTH), (D_MODEL, D_MODEL)]
    return pl.pallas_call(
        body, name="top", grid=(T // tm,),
        in_specs=[row(D_MODEL), row(D_MODEL), row(MLA_WIDTH), row(HG_WIDTH),
                  row(MLA_WIDTH, P_GM // MLA_WIDTH), row(HG_WIDTH, P_GH // HG_WIDTH),
                  _full((D_MODEL, D_MODEL)), _full((1, HG_WIDTH)), _full((1, D_MODEL))],
        out_specs=[row(w) for w, _ in outs] + [_full(s) for s in small],
        out_shape=[jax.ShapeDtypeStruct((T, w), dt) for w, dt in outs] + [jax.ShapeDtypeStruct(s, F32) for s in small],
        scratch_shapes=[pltpu.VMEM((tm, D_MODEL), BF16)],
        compiler_params=_params(1),
    )(x, tgt, o_mla, o_hg, proj, proj, w_out, hg_norm_g, final_g)


def _bot(x, dx2, proj, qn, kvn, dq, dk, dv, dgm, dhq, dhf, dhi, dgh, c_t, s1_t, s2_t, w_in_t, w_q_p, w_kv_p, ln_g, q_g, kv_g, tm):
    T = x.shape[0]
    lat_w = D_PERM - P_QL

    def body(x_ref, dx2_ref, lat_ref, qn_ref, kvn_ref, dq_ref, dk_ref, dv_ref, dgm_ref, dhq_ref, dhf_ref, dhi_ref,
             dgh_ref, c_ref, s1_ref, s2_ref, win_ref, wq_ref, wkv_ref, lng_ref, qg_ref, kvg_ref,
             dx_ref, dproj_ref, dlng_ref, dqg_ref, dkvg_ref, dwq_ref, dwkv_ref, dqpre_ref, dkv_ref):
        @pl.when(pl.program_id(0) == 0)
        def _():
            for ref in (dlng_ref, dqg_ref, dkvg_ref, dwq_ref, dwkv_ref):
                ref[...] = jnp.zeros_like(ref)

        c, s1, s2 = c_ref[...], s1_ref[...], s2_ref[...]
        dkpe = jnp.zeros((tm, LANES), F32)
        for hd in range(N_HEADS):
            sl = slice(hd * HEAD_LANES, (hd + 1) * HEAD_LANES)
            dqpre_ref[:, sl] = _rope_bwd(dq_ref[:, sl], c, s1, s2).astype(BF16)
            dk_h = dk_ref[:, sl]
            dkpe = dkpe + dk_h
            dkv_ref[:, sl] = dk_h.astype(BF16)
        dkv_ref[:, N_HEADS * HEAD_LANES:] = dv_ref[...].astype(BF16)
        lane = lax.broadcasted_iota(jnp.int32, (tm, LANES), 1)
        rope_lanes = jnp.logical_and(lane >= ROPE_LO, lane < ROPE_LO + ROPE)
        dkr = jnp.where(rope_lanes, _rope_bwd(dkpe, c, s1, s2), 0.0)

        def norm_bwd(v, g, dy):
            r = lax.rsqrt(jnp.mean(v * v, axis=-1, keepdims=True) + EPS)
            vh = v * r
            dvh = dy * g
            return jnp.sum(dy * vh, axis=0, keepdims=True), r * (dvh - vh * jnp.mean(dvh * vh, axis=-1, keepdims=True))

        dwq_ref[...] += _dot_tn(qn_ref[...], dqpre_ref[...])
        dwkv_ref[...] += _dot_tn(kvn_ref[...], dkv_ref[...])
        dqn = _dot_nt(dqpre_ref[...], wq_ref[...])
        dg_q, dql = norm_bwd(lat_ref[:, :Q_RANK], qg_ref[...], dqn)
        dqg_ref[...] += dg_q
        dkn = _dot_nt(dkv_ref[...], wkv_ref[...])
        dg_kv, dkvl = norm_bwd(lat_ref[:, Q_RANK:Q_RANK + KV_RANK], kvg_ref[...], dkn)
        dkvg_ref[...] += dg_kv

        dproj_ref[:, P_GM:P_GM + MLA_WIDTH] = dgm_ref[...].astype(BF16)
        dproj_ref[:, P_HQ:P_HQ + HG_WIDTH] = dhq_ref[...].astype(BF16)
        dproj_ref[:, P_HF:P_HF + HG_WIDTH] = dhf_ref[...].astype(BF16)
        dproj_ref[:, P_HI:P_HI + HG_WIDTH] = dhi_ref[...].astype(BF16)
        dproj_ref[:, P_GH:P_GH + HG_WIDTH] = dgh_ref[...].astype(BF16)
        dproj_ref[:, P_QL:P_QL + Q_RANK] = dql.astype(BF16)
        dproj_ref[:, P_KVL:P_KVL + KV_RANK] = dkvl.astype(BF16)
        dproj_ref[:, P_KR:P_KR + LANES] = dkr.astype(BF16)
        dh = sum(_dot(dproj_ref[:, col:col + rows.shape[0]], rows) for col, rows in _in_proj_rows(win_ref))
        dg_ln, dxn = norm_bwd(x_ref[...], lng_ref[...], dh)
        dlng_ref[...] += dg_ln
        dx_ref[...] = dx2_ref[...] + dxn

    def row(w, cb=0):
        return pl.BlockSpec((tm, w), lambda i: (i, cb))

    hl = N_HEADS * HEAD_LANES
    outs = [(D_MODEL, F32), (D_PERM, BF16)]
    small = [(1, D_MODEL), (1, Q_RANK), (1, KV_RANK), (Q_RANK, hl), (KV_RANK, hl + MLA_WIDTH)]
    return pl.pallas_call(
        body, name="bot", grid=(T // tm,),
        in_specs=[row(D_MODEL), row(D_MODEL), row(lat_w, P_QL // lat_w), row(Q_RANK), row(KV_RANK),
                  row(hl), row(hl), row(MLA_WIDTH),
                  row(MLA_WIDTH), row(HG_WIDTH), row(HG_WIDTH), row(HG_WIDTH), row(HG_WIDTH),
                  row(LANES), row(LANES), row(LANES),
                  _full((D_IN, D_MODEL)), _full((Q_RANK, hl)), _full((KV_RANK, hl + MLA_WIDTH)),
                  _full((1, D_MODEL)), _full((1, Q_RANK)), _full((1, KV_RANK))],
        out_specs=[row(w) for w, _ in outs] + [_full(s) for s in small],
        out_shape=[jax.ShapeDtypeStruct((T, w), dt) for w, dt in outs] + [jax.ShapeDtypeStruct(s, F32) for s in small],
        scratch_shapes=[pltpu.VMEM((tm, hl), BF16), pltpu.VMEM((tm, hl + MLA_WIDTH), BF16)],
        compiler_params=_params(1),
    )(x, dx2, proj, qn, kvn, dq, dk, dv, dgm, dhq, dhf, dhi, dgh, c_t, s1_t, s2_t, w_in_t, w_q_p, w_kv_p, ln_g, q_g,
      kv_g)


DW_ROWS = 512


def _dw_in_t(dproj, h, bt):
    T = h.shape[0]
    nb, nt = D_PERM // DW_ROWS, T // bt
    assert P_QL == (nb - 1) * DW_ROWS

    def body(a_ref, b_ref, o_ref, acc_ref, sem):
        blk, t = pl.program_id(0), pl.program_id(1)

        @pl.when(t == 0)
        def _():
            acc_ref[...] = jnp.zeros_like(acc_ref)

        acc_ref[...] += _dot_tn(a_ref[...], b_ref[...])

        def put(src, dst):
            cp = pltpu.make_async_copy(src, dst, sem)
            cp.start()
            cp.wait()

        @pl.when(jnp.logical_and(t == nt - 1, blk < nb - 1))
        def _():
            put(acc_ref, o_ref.at[pl.ds(pl.multiple_of(R_MAIN[0] + blk * DW_ROWS, 8), DW_ROWS), :])

        @pl.when(jnp.logical_and(t == nt - 1, blk == nb - 1))
        def _():
            lat = R_KVL[1] - R_QL[0]
            put(acc_ref.at[0:lat, :], o_ref.at[R_QL[0]:R_KVL[1], :])
            kr = P_KR - P_QL + ROPE_LO
            put(acc_ref.at[kr:kr + ROPE, :], o_ref.at[R_KR[0]:R_KR[1], :])

    return pl.pallas_call(
        body, name="dw_in", grid=(nb, nt),
        in_specs=[pl.BlockSpec((bt, DW_ROWS), lambda n, t: (t, n)), pl.BlockSpec((bt, D_MODEL), lambda n, t: (t, 0))],
        out_specs=pl.BlockSpec(memory_space=pl.ANY),
        out_shape=jax.ShapeDtypeStruct((D_IN, D_MODEL), F32),
        scratch_shapes=[pltpu.VMEM((DW_ROWS, D_MODEL), F32), pltpu.SemaphoreType.DMA],
        compiler_params=_params(2),
    )(dproj, h)


RS_ROWS = 256


def _reduce_scatter(slabs, small):
    n = len(slabs)
    units = []
    for a, s in enumerate(slabs):
        rows, cols = s.shape[1:]
        if rows % RS_ROWS == 0 or rows < RS_ROWS:
            units += [(a, (pl.ds(r0, min(rows, RS_ROWS)), slice(None))) for r0 in range(0, rows, RS_ROWS)]
        else:
            units += [(a, (slice(None), pl.ds(c0, RS_ROWS))) for c0 in range(0, cols, RS_ROWS)]
    nu = len(units)

    def body(*refs):
        ins, small_ref = refs[:n], refs[n]
        outs, small_out = refs[n + 1:2 * n + 1], refs[2 * n + 1]
        own, sib_land, ici_out, ici_land = (refs[(2 + g) * n + 2:(3 + g) * n + 2] for g in range(4))
        small_land = refs[6 * n + 2]
        loc_sems, d2d_send, d2d_recv, ici_send, ici_recv, sm_send, sm_recv = refs[6 * n + 3:6 * n + 10]
        x, y, c = lax.axis_index("x"), lax.axis_index("y"), lax.axis_index("c")
        me = 4 * x + 2 * y + c

        def chip(k):
            return (1 - x if k & 2 else x, 1 - y if k & 1 else y)

        def block(k, core):
            px, py = chip(k)
            return 4 * px + 2 * py + core

        def part(u):
            return units[u]

        def local(u, k):
            a, rows = part(u)
            return pltpu.make_async_copy(ins[a].at[(block(k, c),) + rows], own[a].at[(k,) + rows], loc_sems.at[u, k])

        def to_sibling(u, k):
            a, rows = part(u)
            return pltpu.make_async_remote_copy(
                src_ref=ins[a].at[(block(k, 1 - c),) + rows], dst_ref=sib_land[a].at[(k,) + rows],
                send_sem=d2d_send.at[u, k], recv_sem=d2d_recv.at[u, k], device_id=(x, y, 1 - c), device_id_type=MESH)

        def to_chip(u, k):
            a, rows = part(u)
            return pltpu.make_async_remote_copy(
                src_ref=ici_out[a].at[(k - 1,) + rows], dst_ref=ici_land[a].at[(k - 1,) + rows],
                send_sem=ici_send.at[u, k - 1], recv_sem=ici_recv.at[u, k - 1], device_id=(*chip(k), c),
                device_id_type=MESH)

        def small_copy(k, receiving):
            px, py = chip(k >> 1)
            pc = 1 - c if k & 1 else c
            slot = 4 * px + 2 * py + pc if receiving else me
            return pltpu.make_async_remote_copy(
                src_ref=small_ref, dst_ref=small_land.at[slot], send_sem=sm_send.at[k - 1], recv_sem=sm_recv.at[k - 1],
                device_id=(px, py, pc), device_id_type=MESH)

        for u in range(nu):
            for k in range(4):
                local(u, k).start()
        for u in range(nu):
            for k in range(4):
                to_sibling(u, k).start()
        small_land[me] = small_ref[...]
        for k in range(1, N_DEV):
            small_copy(k, False).start()
        for u in range(nu):
            a, rows = part(u)
            for k in range(4):
                local(u, k).wait()
                to_sibling(u, k).wait_recv()
            for k in range(1, 4):
                ici_out[a][(k - 1,) + rows] = (own[a][(k,) + rows] + sib_land[a][(k,) + rows]).astype(BF16)
                to_chip(u, k).start()
        for u in range(nu):
            a, rows = part(u)
            acc = own[a][(0,) + rows] + sib_land[a][(0,) + rows]
            for k in range(1, 4):
                to_chip(u, k).wait_recv()
                acc = acc + ici_land[a][(k - 1,) + rows].astype(F32)
            outs[a][rows] = acc
        for k in range(1, N_DEV):
            small_copy(k, True).wait_recv()
        acc = small_land[0]
        for d in range(1, N_DEV):
            acc = acc + small_land[d]
        small_out[...] = acc
        for u in range(nu):
            for k in range(4):
                to_sibling(u, k).wait_send()
            for k in range(1, 4):
                to_chip(u, k).wait_send()
        for k in range(1, N_DEV):
            small_copy(k, False).wait_send()

    vm = pl.BlockSpec(memory_space=pltpu.VMEM)
    hbm = pl.BlockSpec(memory_space=pl.ANY)
    dma = pltpu.SemaphoreType.DMA
    return pl.pallas_call(
        body, name="reduce_scatter_grads",
        in_specs=[hbm] * n + [vm], out_specs=[vm] * (n + 1),
        out_shape=[jax.ShapeDtypeStruct(s.shape[1:], F32) for s in slabs] + [jax.ShapeDtypeStruct(small.shape, F32)],
        scratch_shapes=[pltpu.VMEM((4,) + s.shape[1:], F32) for s in slabs] * 2
        + [pltpu.VMEM((3,) + s.shape[1:], BF16) for s in slabs] * 2
        + [pltpu.VMEM((N_DEV,) + small.shape, F32)]
        + [dma((nu, 4)), dma((nu, 4)), dma((nu, 4)), dma((nu, 3)), dma((nu, 3)), dma((N_DEV - 1,)), dma((N_DEV - 1,))],
        compiler_params=pltpu.CompilerParams(vmem_limit_bytes=VMEM_LIMIT),
    )(*slabs, small)


def _adamw_math(w, g, m, v):
    m = ADAM_B1 * m + (1.0 - ADAM_B1) * g
    v = ADAM_B2 * v + (1.0 - ADAM_B2) * (g * g)
    m_hat = m / (1.0 - ADAM_B1 ** ADAM_STEP)
    v_hat = v / (1.0 - ADAM_B2 ** ADAM_STEP)
    delta = -ADAM_LR * (m_hat / (jnp.sqrt(v_hat) + ADAM_EPS) + ADAM_WD * w)
    return delta, m, v


SMALL_W = 512


def _adamw(big, small_w, small_g):
    nb, ns = len(big), len(small_w)

    def body(*refs):
        k = 0
        big_in = [refs[4 * i:4 * i + 4] for i in range(nb)]
        k = 4 * nb
        small_in = [refs[k + 3 * i:k + 3 * i + 3] for i in range(ns)]
        k += 3 * ns
        sg_ref = refs[k]
        k += 1
        big_out = [refs[k + 3 * i:k + 3 * i + 3] for i in range(nb)]
        k += 3 * nb
        small_out = [refs[k + 4 * i:k + 4 * i + 4] for i in range(ns)]

        for (w, g, m, v), (od, om, ov) in zip(big_in, big_out):
            od[...], om[...], ov[...] = _adamw_math(w[...], g[...], m[...], v[...])

        sg = sg_ref[...]
        lbp = small_in[2][0][...]
        lb = _lower_bound(lbp)
        t = sg[4:5, :] * lb * (1.0 - lb)
        grads = [jnp.concatenate([sg[0:1, :], sg[1:2, :]], axis=1),
                 jnp.concatenate([sg[2:3, :], sg[3:4, :]], axis=1),
                 jnp.concatenate([t, -t], axis=0),
                 sg[6:7, :], sg[7:8, 0:Q_RANK], sg[7:8, Q_RANK:Q_RANK + KV_RANK]]
        for (w, m, v), g, (og, od, om, ov) in zip(small_in, grads, small_out):
            og[...] = g
            od[...], om[...], ov[...] = _adamw_math(w[...], g, m[...], v[...])

    ins = [a for grp in big for a in grp] + [a for grp in small_w for a in grp] + [small_g]
    out_shape = ([jax.ShapeDtypeStruct(grp[0].shape, F32) for grp in big for _ in range(3)]
                 + [jax.ShapeDtypeStruct(grp[0].shape, F32) for grp in small_w for _ in range(4)])
    vm = pl.BlockSpec(memory_space=pltpu.VMEM)
    res = pl.pallas_call(
        body, name="adamw", in_specs=[vm] * len(ins), out_specs=[vm] * len(out_shape), out_shape=out_shape,
        compiler_params=pltpu.CompilerParams(vmem_limit_bytes=VMEM_LIMIT),
    )(*ins)
    big_res = [res[3 * i:3 * i + 3] for i in range(nb)]
    small_res = [res[3 * nb + 4 * i:3 * nb + 4 * i + 4] for i in range(ns)]
    return big_res, small_res


def _perm_weights(g_in_t, g_q, g_kv, g_out):
    w_in_t = g_in_t.reshape(D_IN, D_MODEL)
    wq = g_q.transpose(1, 0, 2)
    w_q_p = jnp.pad(wq, ((0, 0), (0, 0), (0, HEAD_LANES - NOPE - ROPE))).reshape(Q_RANK, N_HEADS * HEAD_LANES)
    wkv = g_kv.transpose(1, 0, 2)
    wk = jnp.pad(wkv[:, :, :NOPE], ((0, 0), (0, 0), (0, HEAD_LANES - NOPE))).reshape(KV_RANK, N_HEADS * HEAD_LANES)
    wv = wkv[:, :, NOPE:].reshape(KV_RANK, MLA_WIDTH)
    return w_in_t, w_q_p, jnp.concatenate([wk, wv], axis=1), g_out.reshape(D_MODEL, D_MODEL)


def _grad_slabs(dw_in_t, dw_q_p, dw_kv_p, dw_out):
    s_in = dw_in_t.reshape(N_DEV, D_IN // N_DEV, D_MODEL)
    s_q = dw_q_p.reshape(Q_RANK, N_HEADS, HEAD_LANES)[:, :, :NOPE + ROPE].transpose(1, 0, 2)
    hl = N_HEADS * HEAD_LANES
    dk = dw_kv_p[:, :hl].reshape(KV_RANK, N_HEADS, HEAD_LANES)[:, :, :NOPE]
    dv = dw_kv_p[:, hl:].reshape(KV_RANK, N_HEADS, V_DIM)
    s_kv = jnp.concatenate([dk, dv], axis=2).transpose(1, 0, 2)
    return s_in, s_q, s_kv, dw_out.reshape(N_DEV, D_MODEL // N_DEV, D_MODEL)


def _block_sizes(T):
    return min(256, T), min(256, T), min(512, T)


def kernel(x, positions, ln_g, w_in, q_a_norm_g, w_q_b, kv_a_norm_g, w_kv_b, hg_lower_bounds, hg_norm_g, w_out, final_norm_g, loss_target, m_ln_g, m_w_in, m_q_a_norm_g, m_w_q_b, m_kv_a_norm_g, m_w_kv_b, m_hg_lower_bounds, m_hg_norm_g, m_w_out, m_final_norm_g, v_ln_g, v_w_in, v_q_a_norm_g, v_w_q_b, v_kv_a_norm_g, v_w_kv_b, v_hg_lower_bounds, v_hg_norm_g, v_w_out, v_final_norm_g):
    T = x.shape[1]
    tm, tq, bt = _block_sizes(T)
    nq = T // tq
    xs, tgt = x[0], loss_target[0]
    pos_f = positions.astype(F32)
    fng = final_norm_g.reshape(1, D_MODEL)

    w_in_shard_t = w_in[0].T
    gathered, (c_t, s1_t, s2_t) = _all_gather_weights([w_in_shard_t, w_q_b[0], w_kv_b[0], w_out[0]], pos_f)
    w_in_t, w_q_p, w_kv_p, w_out_b = _perm_weights(*gathered)

    proj, h, qn, kvn, q, k, v = _fwd_in(xs, ln_g, w_in_t, q_a_norm_g, w_q_p, kv_a_norm_g, w_kv_p, c_t, s1_t, s2_t, bt)
    hl = N_HEADS * HEAD_LANES
    v_t = v.reshape(nq, tq, MLA_WIDTH).transpose(0, 2, 1)
    k_t = k.reshape(nq, tq, hl).transpose(0, 2, 1)
    q_t = q.reshape(nq, tq, hl).transpose(0, 2, 1)
    o_mla, lse = _attn_fwd_flat(k, q_t, v_t, tq)
    o_hg, states = _hgrn_fwd(proj, hg_lower_bounds)
    dx2, d_om, dsum, d_gm, d_oh, d_gh, loss_p, d_fng, d_hgn, dw_out = _top(
        xs, tgt, o_mla, o_hg, proj, w_out_b, hg_norm_g, fng, tm)
    dsum = dsum[:, :N_HEADS].T.reshape(N_HEADS, nq, 1, tq)
    do_t = d_om.reshape(nq, tq, hl).transpose(0, 2, 1)
    dq_t, dk, dv = _attn_bwd_flat(k, v, q_t, k_t, do_t, lse, dsum, tq)
    dq = dq_t.transpose(0, 2, 1).reshape(T, N_HEADS * HEAD_LANES)
    d_hq, d_hf, d_hi, d_lb = _hgrn_bwd(proj, hg_lower_bounds, d_oh, states)
    dx, dproj, d_lng, d_qg, d_kvg, dw_q_p, dw_kv_p = _bot(
        xs, dx2, proj, qn, kvn, dq, dk, dv, d_gm, d_hq, d_hf, d_hi, d_gh, c_t, s1_t, s2_t, w_in_t, w_q_p, w_kv_p,
        ln_g, q_a_norm_g, kv_a_norm_g, tm)
    dw_in_t = _dw_in_t(dproj, h, bt)

    small = jnp.concatenate([
        d_lng.reshape(2, SMALL_W), d_fng.reshape(2, SMALL_W), d_lb, loss_p, d_hgn,
        jnp.concatenate([d_qg, d_kvg, jnp.zeros((1, SMALL_W - Q_RANK - KV_RANK), F32)], axis=1)], axis=0)
    g_in, g_q, g_kv, g_out, small_sum = _reduce_scatter(list(_grad_slabs(dw_in_t, dw_q_p, dw_kv_p, dw_out)), small)

    big = [(w_in_shard_t, g_in, m_w_in[0].T, v_w_in[0].T), (w_q_b[0], g_q, m_w_q_b[0], v_w_q_b[0]),
           (w_kv_b[0], g_kv, m_w_kv_b[0], v_w_kv_b[0]), (w_out[0], g_out, m_w_out[0], v_w_out[0])]
    small_w = [(ln_g, m_ln_g, v_ln_g),
               (fng, m_final_norm_g.reshape(1, D_MODEL), v_final_norm_g.reshape(1, D_MODEL)),
               (hg_lower_bounds, m_hg_lower_bounds, v_hg_lower_bounds), (hg_norm_g, m_hg_norm_g, v_hg_norm_g),
               (q_a_norm_g, m_q_a_norm_g, v_q_a_norm_g), (kv_a_norm_g, m_kv_a_norm_g, v_kv_a_norm_g)]
    big_res, small_res = _adamw(big, small_w, small_sum)

    loss = small_sum[5, 0]
    (r_in, r_q, r_kv, r_out) = big_res
    (s_ln, s_fn, s_lb, s_hgn, s_qg, s_kvg) = small_res
    flat = lambda t: t.reshape(D_MODEL)
    lead = lambda t: t[None]
    grads = [s_ln[0], lead(g_in.T), s_qg[0], lead(g_q), s_kvg[0], lead(g_kv), s_lb[0], s_hgn[0], lead(g_out), flat(s_fn[0])]

    def pick(i):
        return [s_ln[i + 1], lead(r_in[i].T), s_qg[i + 1], lead(r_q[i]), s_kvg[i + 1], lead(r_kv[i]), s_lb[i + 1],
                s_hgn[i + 1], lead(r_out[i]), flat(s_fn[i + 1])]

    return (loss, dx[None], *grads, *pick(0), *pick(1), *pick(2))
```

```python
import math

import numpy as np
import jax
import jax.numpy as jnp
from jax import lax
from jax.experimental import pallas as pl
from jax.experimental.pallas import tpu as pltpu

F32 = jnp.float32
BF16 = jnp.bfloat16

D_MODEL = 1024
N_HEADS = 8
NOPE = 64
ROPE = 32
HALF_ROPE = ROPE // 2
V_DIM = 64
Q_RANK = 256
KV_RANK = 128
MLA_WIDTH = N_HEADS * V_DIM
HG_HEADS = 4
HG_DIM = 128
HG_WIDTH = HG_HEADS * HG_DIM
CHUNK = 64
SUB = 16
D_IN = 2976
D_PERM = 3072
ROPE_THETA = 10000.0
EPS = 1e-6
N_DEV = 8
LANES = 128
HEAD_LANES = 128

P_GM, P_HQ, P_HF, P_HI, P_GH, P_QL, P_KVL, P_KR = 0, 512, 1024, 1536, 2048, 2560, 2816, 2944
R_QL, R_KVL, R_KR, R_MAIN = (0, 256), (256, 384), (384, 416), (416, 2976)
ROPE_LO = NOPE
SCALE = 1.0 / math.sqrt(NOPE + ROPE)

ADAM_LR = 0.001
ADAM_B1 = 0.9
ADAM_B2 = 0.999
ADAM_EPS = 1e-08
ADAM_WD = 0.01
ADAM_STEP = 10

VMEM_LIMIT = 56 * 1024 * 1024
MESH = pl.DeviceIdType.MESH

NT = (((1,), (1,)), ((), ()))
TN = (((0,), (0,)), ((), ()))


def _params(n_grid=0, **kw):
    sem = ("arbitrary",) * n_grid if n_grid else None
    return pltpu.CompilerParams(dimension_semantics=sem, vmem_limit_bytes=VMEM_LIMIT, **kw)


def _dot(a, b):
    return jnp.dot(a, b, preferred_element_type=F32)


def _dot_nt(a, b):
    return lax.dot_general(a, b, NT, preferred_element_type=F32)


def _dot_tn(a, b):
    return lax.dot_general(a, b, TN, preferred_element_type=F32)


def _sigmoid(x):
    return 1.0 / (1.0 + jnp.exp(-x))


def _rope_fwd(x, c, s1, s2):
    return x * c + pltpu.roll(x, LANES - HALF_ROPE, 1) * s1 + pltpu.roll(x, HALF_ROPE, 1) * s2


def _rope_bwd(dy, c, s1, s2):
    return dy * c - pltpu.roll(dy, LANES - HALF_ROPE, 1) * s1 - pltpu.roll(dy, HALF_ROPE, 1) * s2


def _in_proj_rows(wt_ref):
    kr = wt_ref[R_KR[0]:R_KR[1], :]
    pad = lambda n: jnp.zeros((n, D_MODEL), kr.dtype)
    return ((P_GM, wt_ref[R_MAIN[0]:R_MAIN[1], :]), (P_QL, wt_ref[R_QL[0]:R_QL[1], :]),
            (P_KVL, wt_ref[R_KVL[0]:R_KVL[1], :]),
            (P_KR, jnp.concatenate([pad(ROPE_LO), kr, pad(LANES - ROPE_LO - ROPE)], axis=0)))


def _full(shape):
    n = len(shape)
    return pl.BlockSpec(shape, lambda *_: (0,) * n)


ROPE_BLOCK = 512


def _rope_constants():
    inv = (np.float32(ROPE_THETA) ** (-np.arange(HALF_ROPE, dtype=np.float32) / np.float32(HALF_ROPE))).astype(np.float32)
    place = np.zeros((3, HALF_ROPE, LANES), np.float32)
    for i in range(HALF_ROPE):
        place[0, i, ROPE_LO + i] = place[0, i, ROPE_LO + HALF_ROPE + i] = 1.0
        place[1, i, ROPE_LO + i] = -1.0
        place[2, i, ROPE_LO + HALF_ROPE + i] = 1.0
    base = np.ones((1, LANES), np.float32)
    base[0, ROPE_LO:ROPE_LO + ROPE] = 0.0
    return jnp.asarray(inv.reshape(HALF_ROPE, 1)), jnp.asarray(place), jnp.asarray(base)


def _rope_block(pos, inv, place_ref, base):
    ang = inv * pos
    cos, sin = jnp.cos(ang), jnp.sin(ang)

    def put(v, k):
        return lax.dot_general(v, place_ref[k], TN, precision=lax.Precision.HIGHEST, preferred_element_type=F32)

    return put(cos, 0) + base, put(sin, 1), put(sin, 2)


def _all_gather_weights(shards, pos_f):
    n = len(shards)
    T = pos_f.shape[1]
    rb = min(ROPE_BLOCK, T)

    def body(*refs):
        ins, (pos_ref, inv_ref, place_ref, base_ref) = refs[:n], refs[n:n + 4]
        outs, tables = refs[n + 4:2 * n + 4], refs[2 * n + 4:2 * n + 7]
        send_sems, recv_sems = refs[2 * n + 7], refs[2 * n + 8]
        x, y, c = lax.axis_index("x"), lax.axis_index("y"), lax.axis_index("c")
        me, sibling = (x, y, c), (x, y, 1 - c)
        chips = [(1 - x, y), (x, 1 - y), (1 - x, 1 - y)]

        def idx(d):
            return 4 * d[0] + 2 * d[1] + d[2]

        def copy(a, k, block, to):
            rows = outs[a].at[idx(block)]
            return pltpu.make_async_remote_copy(src_ref=rows, dst_ref=rows, send_sem=send_sems.at[a, k],
                                                recv_sem=recv_sems.at[a, k], device_id=to, device_id_type=MESH)

        for a in range(n):
            outs[a][idx(me)] = ins[a][...].astype(BF16)
        first = []
        for a in range(n):
            first.append(copy(a, 0, me, sibling))
            first += [copy(a, 1 + j, me, (*chip, c)) for j, chip in enumerate(chips)]
        for cp in first:
            cp.start()
        for r0 in range(0, T, rb):
            for ref, tab in zip(tables, _rope_block(pos_ref[:, r0:r0 + rb], inv_ref[...], place_ref, base_ref[...])):
                ref[r0:r0 + rb, :] = tab
        passed = []
        for j, chip in enumerate(chips):
            for a in range(n):
                copy(a, 1 + j, (*chip, c), me).wait_recv()
                cp = copy(a, 4 + j, (*chip, c), sibling)
                cp.start()
                passed.append(cp)
        for a in range(n):
            copy(a, 0, sibling, me).wait_recv()
            for j, chip in enumerate(chips):
                copy(a, 4 + j, (*chip, 1 - c), me).wait_recv()
        for cp in first + passed:
            cp.wait_send()

    vm = pl.BlockSpec(memory_space=pltpu.VMEM)
    res = pl.pallas_call(
        body, name="all_gather_weights",
        in_specs=[vm] * (n + 4), out_specs=[vm] * (n + 3),
        out_shape=[jax.ShapeDtypeStruct((N_DEV,) + s.shape, BF16) for s in shards]
        + [jax.ShapeDtypeStruct((T, LANES), F32)] * 3,
        scratch_shapes=[pltpu.SemaphoreType.DMA((n, 7)), pltpu.SemaphoreType.DMA((n, 7))],
        compiler_params=pltpu.CompilerParams(vmem_limit_bytes=VMEM_LIMIT),
    )(*shards, pos_f, *_rope_constants())
    return res[:n], res[n:]


def _fwd_in(x, ln_g, w_in_t, q_g, w_q_p, kv_g, w_kv_p, c_t, s1_t, s2_t, tm):
    T = x.shape[0]

    def body(x_ref, lng_ref, win_ref, qg_ref, wq_ref, kvg_ref, wkv_ref, c_ref, s1_ref, s2_ref,
             proj_ref, h_ref, qn_ref, kvn_ref, q_ref, k_ref, v_ref):
        xv = x_ref[...]
        r = lax.rsqrt(jnp.mean(xv * xv, axis=-1, keepdims=True) + EPS)
        h = (xv * r * lng_ref[...]).astype(BF16)
        h_ref[...] = h
        for col, rows in _in_proj_rows(win_ref):
            proj_ref[:, col:col + rows.shape[0]] = _dot_nt(h, rows)
        c, s1, s2 = c_ref[...], s1_ref[...], s2_ref[...]

        ql = proj_ref[:, P_QL:P_QL + Q_RANK]
        rq = lax.rsqrt(jnp.mean(ql * ql, axis=-1, keepdims=True) + EPS)
        qn = (ql * rq * qg_ref[...]).astype(BF16)
        qn_ref[...] = qn
        q = _dot(qn, wq_ref[...])
        for hd in range(N_HEADS):
            sl = slice(hd * HEAD_LANES, (hd + 1) * HEAD_LANES)
            q_ref[:, sl] = _rope_fwd(q[:, sl], c, s1, s2).astype(BF16)

        kvl = proj_ref[:, P_KVL:P_KVL + KV_RANK]
        rk = lax.rsqrt(jnp.mean(kvl * kvl, axis=-1, keepdims=True) + EPS)
        kvn = (kvl * rk * kvg_ref[...]).astype(BF16)
        kvn_ref[...] = kvn
        kv = _dot(kvn, wkv_ref[...])
        kpe = _rope_fwd(proj_ref[:, P_KR:P_KR + LANES], c, s1, s2)
        for hd in range(N_HEADS):
            sl = slice(hd * HEAD_LANES, (hd + 1) * HEAD_LANES)
            k_ref[:, sl] = (kv[:, sl] + kpe).astype(BF16)
        v_ref[...] = kv[:, N_HEADS * HEAD_LANES:].astype(BF16)

    def row(w):
        return pl.BlockSpec((tm, w), lambda i: (i, 0))

    outs = [(D_PERM, F32), (D_MODEL, BF16), (Q_RANK, BF16), (KV_RANK, BF16),
            (N_HEADS * HEAD_LANES, BF16), (N_HEADS * HEAD_LANES, BF16), (MLA_WIDTH, BF16)]
    return pl.pallas_call(
        body, name="fwd_in", grid=(T // tm,),
        in_specs=[row(D_MODEL), _full((1, D_MODEL)), _full((D_IN, D_MODEL)), _full((1, Q_RANK)),
                  _full((Q_RANK, N_HEADS * HEAD_LANES)), _full((1, KV_RANK)),
                  _full((KV_RANK, N_HEADS * HEAD_LANES + MLA_WIDTH)), row(LANES), row(LANES), row(LANES)],
        out_specs=[row(w) for w, _ in outs],
        out_shape=[jax.ShapeDtypeStruct((T, w), dt) for w, dt in outs],
        compiler_params=_params(1),
    )(x, ln_g, w_in_t, q_g, w_q_p, kv_g, w_kv_p, c_t, s1_t, s2_t)


LOG2E = 1.4426950408889634
SCALE2 = SCALE * LOG2E


def _causal(tq):
    r = lax.broadcasted_iota(jnp.int32, (tq, tq), 0)
    c = lax.broadcasted_iota(jnp.int32, (tq, tq), 1)
    return r <= c


MASKED = -1e30


def _causal_bias(bias_ref, tq):
    bias_ref[0] = jnp.zeros((tq, tq), F32)
    bias_ref[1] = jnp.where(_causal(tq), 0.0, MASKED)


def _tile_tables(nq, by_query):
    if by_query:
        pairs = [(j, i) for i in range(nq) for j in range(i + 1)]
    else:
        pairs = [(j, i) for j in range(nq) for i in range(nq - 1, j - 1, -1)]
    pairs.append(pairs[-1])
    jj, ii = np.array(pairs, np.int32).T
    return jnp.asarray(jj), jnp.asarray(ii), len(pairs) - 1


ATTN_TRIP = 8


def _walk_tiles(n, products, tile, flush, buf_a, buf_b):
    bufs = (buf_a, buf_b)
    products(0, buf_a)

    def trip(r, carry):
        for u in range(ATTN_TRIP):
            products(ATTN_TRIP * r + u + 1, bufs[(u + 1) % 2])
            tile(ATTN_TRIP * r + u, bufs[u % 2])
        for u in range(ATTN_TRIP):
            flush(ATTN_TRIP * r + u)
        return carry

    lax.fori_loop(0, n // ATTN_TRIP, trip, 0)
    rest = n - n % ATTN_TRIP
    for u in range(n % ATTN_TRIP):
        if rest + u + 1 < n:
            products(rest + u + 1, bufs[(u + 1) % 2])
        tile(rest + u, bufs[u % 2])
    for u in range(n % ATTN_TRIP):
        flush(rest + u)


def _attn_fwd_flat(k, q_t, v_t, tq):
    T = k.shape[0]
    nq = T // tq
    jj, ii, n = _tile_tables(nq, True)
    heads = [slice(hh * HEAD_LANES, (hh + 1) * HEAD_LANES) for hh in range(2)]

    def body(jj_ref, ii_ref, k_ref, qt_ref, vt_ref, o_ref, lse_ref, sa_ref, sb_ref, m_ref, l_ref, acc_ref, bias_ref):
        def reset(st):
            m_ref[st] = jnp.full(m_ref.shape[1:], MASKED, F32)
            l_ref[st] = jnp.zeros(l_ref.shape[1:], F32)
            acc_ref[st] = jnp.zeros(acc_ref.shape[1:], F32)

        _causal_bias(bias_ref, tq)
        for st in range(ATTN_TRIP):
            reset(st)

        def products(t, buf):
            j, i = jj_ref[t], ii_ref[t]
            kj = k_ref[pl.ds(pl.multiple_of(j * tq, tq), tq), :]
            for hh, sl in enumerate(heads):
                buf[hh] = _dot(kj[:, sl], qt_ref[i, sl, :])

        def tile(t, buf):
            j, i = jj_ref[t], ii_ref[t]
            vt = vt_ref[j]
            bias = bias_ref.at[(j == i).astype(jnp.int32)]
            st = i % ATTN_TRIP
            for hh in range(2):
                s = buf[hh] * SCALE2 + bias[...]
                m = m_ref[st, hh]
                m_new = jnp.maximum(m, jnp.max(s, axis=0, keepdims=True))
                alpha = jnp.exp2(m - m_new)
                p = jnp.exp2(s - m_new)
                m_ref[st, hh] = m_new
                l_ref[st, hh] = alpha * l_ref[st, hh] + jnp.sum(p, axis=0, keepdims=True)
                acc_ref[st, hh] = alpha * acc_ref[st, hh] + _dot(vt, p.astype(BF16))

        def flush(t):
            j, i = jj_ref[t], ii_ref[t]

            @pl.when(j == i)
            def _():
                st = i % ATTN_TRIP
                first = lax.broadcasted_iota(jnp.int32, (LANES, tq), 0) < V_DIM
                out = jnp.where(first, acc_ref[st, 0] / l_ref[st, 0], acc_ref[st, 1] / l_ref[st, 1])
                o_ref[pl.ds(pl.multiple_of(i * tq, tq), tq), :] = out.T
                for hh in range(2):
                    lse_ref[hh, i] = m_ref[st, hh] + jnp.log2(l_ref[st, hh])
                reset(st)

        _walk_tiles(n, products, tile, flush, sa_ref, sb_ref)

    smem = pl.BlockSpec(memory_space=pltpu.SMEM)
    return pl.pallas_call(
        body, name="attn_fwd", grid=(N_HEADS // 2,),
        in_specs=[smem, smem,
                  pl.BlockSpec((T, 2 * HEAD_LANES), lambda p: (0, p)),
                  pl.BlockSpec((nq, 2 * HEAD_LANES, tq), lambda p: (0, p, 0)),
                  pl.BlockSpec((nq, LANES, tq), lambda p: (0, p, 0))],
        out_specs=[pl.BlockSpec((T, LANES), lambda p: (0, p)),
                   pl.BlockSpec((2, nq, 1, tq), lambda p: (p, 0, 0, 0))],
        out_shape=[jax.ShapeDtypeStruct((T, MLA_WIDTH), F32), jax.ShapeDtypeStruct((N_HEADS, nq, 1, tq), F32)],
        scratch_shapes=[pltpu.VMEM((2, tq, tq), F32), pltpu.VMEM((2, tq, tq), F32),
                        pltpu.VMEM((ATTN_TRIP, 2, 1, tq), F32), pltpu.VMEM((ATTN_TRIP, 2, 1, tq), F32),
                        pltpu.VMEM((ATTN_TRIP, 2, LANES, tq), F32), pltpu.VMEM((2, tq, tq), F32)],
        compiler_params=_params(1),
    )(jj, ii, k, q_t, v_t)


def _attn_bwd_flat(k, v, q_t, k_t, do_t, lse, dsum, tq):
    T = k.shape[0]
    nq = T // tq
    jj, ii, n = _tile_tables(nq, False)
    heads = [slice(hh * HEAD_LANES, (hh + 1) * HEAD_LANES) for hh in range(2)]

    def body(jj_ref, ii_ref, k_ref, v_ref, qt_ref, kt_ref, dot_ref, lse_ref, dsum_ref, dqt_ref, dk_ref, dv_ref,
             ba_ref, bb_ref, dkt_ref, dvt_ref, bias_ref):
        _causal_bias(bias_ref, tq)
        dqt_ref[...] = jnp.zeros_like(dqt_ref)
        dkt_ref[...] = jnp.zeros_like(dkt_ref)
        dvt_ref[...] = jnp.zeros_like(dvt_ref)

        def products(t, buf):
            j, i = jj_ref[t], ii_ref[t]
            rows = pl.ds(pl.multiple_of(j * tq, tq), tq)
            for hh, sl in enumerate(heads):
                buf[hh] = _dot(k_ref[rows, sl], qt_ref[i, sl, :])
                buf[2 + hh] = _dot(v_ref[rows, :], dot_ref[i, sl, :])

        def tile(t, buf):
            j, i = jj_ref[t], ii_ref[t]
            bias = bias_ref.at[(j == i).astype(jnp.int32)]
            st = j % ATTN_TRIP
            dv_new = None
            for hh, sl in enumerate(heads):
                p = jnp.exp2(buf[hh] * SCALE2 + bias[...] - lse_ref[hh, i])
                ds = (p * (buf[2 + hh] - dsum_ref[hh, i]) * SCALE).astype(BF16)
                dv_h = _dot_nt(dot_ref[i, sl, :], p.astype(BF16))
                dv_new = dv_h if dv_new is None else dv_new + dv_h
                dkt_ref[st, sl, :] += _dot_nt(qt_ref[i, sl, :], ds)
                dqt_ref[i, sl, :] += _dot(kt_ref[j, sl, :], ds)
            dvt_ref[st] += dv_new

        def flush(t):
            j, i = jj_ref[t], ii_ref[t]

            @pl.when(j == i)
            def _():
                st = j % ATTN_TRIP
                rows = pl.ds(pl.multiple_of(j * tq, tq), tq)
                dk_ref[rows, :] = dkt_ref[st].T
                dv_ref[rows, :] = dvt_ref[st].T
                dkt_ref[st] = jnp.zeros(dkt_ref.shape[1:], F32)
                dvt_ref[st] = jnp.zeros(dvt_ref.shape[1:], F32)

        _walk_tiles(n, products, tile, flush, ba_ref, bb_ref)

    smem = pl.BlockSpec(memory_space=pltpu.SMEM)
    stat = pl.BlockSpec((2, nq, 1, tq), lambda p: (p, 0, 0, 0))
    blocks_t = pl.BlockSpec((nq, 2 * HEAD_LANES, tq), lambda p: (0, p, 0))
    return pl.pallas_call(
        body, name="attn_bwd", grid=(N_HEADS // 2,),
        in_specs=[smem, smem,
                  pl.BlockSpec((T, 2 * HEAD_LANES), lambda p: (0, p)),
                  pl.BlockSpec((T, LANES), lambda p: (0, p)),
                  blocks_t, blocks_t, blocks_t, stat, stat],
        out_specs=[blocks_t,
                   pl.BlockSpec((T, 2 * HEAD_LANES), lambda p: (0, p)),
                   pl.BlockSpec((T, LANES), lambda p: (0, p))],
        out_shape=[jax.ShapeDtypeStruct((nq, N_HEADS * HEAD_LANES, tq), F32),
                   jax.ShapeDtypeStruct((T, N_HEADS * HEAD_LANES), F32),
                   jax.ShapeDtypeStruct((T, MLA_WIDTH), F32)],
        scratch_shapes=[pltpu.VMEM((4, tq, tq), F32), pltpu.VMEM((4, tq, tq), F32),
                        pltpu.VMEM((ATTN_TRIP, 2 * HEAD_LANES, tq), F32), pltpu.VMEM((ATTN_TRIP, LANES, tq), F32),
                        pltpu.VMEM((2, tq, tq), F32)],
        compiler_params=_params(1),
    )(jj, ii, k, v, q_t, k_t, do_t, lse, dsum)


def _lower_bound(lbp):
    a, b = lbp[0:1, :], lbp[1:2, :]
    mx = jnp.maximum(a, b)
    ea, eb = jnp.exp(a - mx), jnp.exp(b - mx)
    return ea / (ea + eb)


def _tri(lower):
    r = lax.broadcasted_iota(jnp.int32, (CHUNK, CHUNK), 0)
    c = lax.broadcasted_iota(jnp.int32, (CHUNK, CHUNK), 1)
    return (c <= r) if lower else (c >= r)


def _running_sum(x, from_end):
    row = lax.broadcasted_iota(jnp.int32, x.shape, 0)
    step = 1
    while step < CHUNK:
        if from_end:
            x = x + jnp.where(row < CHUNK - step, pltpu.roll(x, CHUNK - step, 0), 0.0)
        else:
            x = x + jnp.where(row >= step, pltpu.roll(x, step, 0), 0.0)
        step *= 2
    return x


def _hg_gates(hq, hf, lb):
    sq = _sigmoid(hq)
    sf = _sigmoid(hf)
    f = lb + (1.0 - lb) * sf
    g = jnp.log(f)
    gcum = _running_sum(g, False)
    return sq, sf, f, hq * sq, 1.0 - f, gcum


def _head(x, hd):
    return x[:, hd * HG_DIM:(hd + 1) * HG_DIM]


def _all_heads(fn):
    return jnp.concatenate([fn(hd) for hd in range(HG_HEADS)], axis=1)


def _hg_blocks(q, kk, gcum):
    rowi = lax.broadcasted_iota(jnp.int32, gcum.shape, 0)
    out = []
    for blk in range(CHUNK // SUB):
        lo, hi = blk * SUB, (blk + 1) * SUB
        gb = gcum[lo - 1:lo, :] if blk else jnp.zeros_like(gcum[0:1, :])
        eq = jnp.exp(gcum[lo:hi, :] - gb)
        ek = jnp.exp(jnp.where(rowi < hi, gb - gcum, 0.0))
        out.append((eq, ek, (q[lo:hi, :] * eq).astype(BF16), (kk * ek).astype(BF16)))
    return out


def _hg_scores(blocks):
    out = []
    for hd in range(HG_HEADS):
        a = jnp.concatenate([_dot_nt(_head(qb, hd), _head(kb, hd)) for _, _, qb, kb in blocks], axis=0)
        out.append(jnp.where(_tri(True), a, 0.0))
    return out


HG_STEP_CHUNKS = 8


def _hgrn_fwd(proj, lbp):
    T = proj.shape[0]
    nc = T // CHUNK
    ns = min(HG_STEP_CHUNKS, nc)
    rows = ns * CHUNK

    def body(hq_ref, hf_ref, hi_ref, lbp_ref, o_ref, st_ref, state):
        @pl.when(pl.program_id(0) == 0)
        def _():
            state[...] = jnp.zeros_like(state)

        lb = _lower_bound(lbp_ref[...])
        work = []
        for c in range(ns):
            r = slice(c * CHUNK, (c + 1) * CHUNK)
            _, _, _, q, kk, gcum = _hg_gates(hq_ref[r, :], hf_ref[r, :], lb)
            vb = hi_ref[r, :].astype(BF16)
            a = _hg_scores(_hg_blocks(q, kk, gcum))
            gend = gcum[CHUNK - 1:CHUNK, :]
            qgb = (q * jnp.exp(gcum)).astype(BF16)
            kgeb = (kk * jnp.exp(gend - gcum)).astype(BF16)
            intra = [_dot(a[hd].astype(BF16), _head(vb, hd)) for hd in range(HG_HEADS)]
            update = [_dot_tn(_head(vb, hd), _head(kgeb, hd)) for hd in range(HG_HEADS)]
            work.append((qgb, jnp.exp(gend), intra, update))
        for hd in range(HG_HEADS):
            st = state[hd]
            for c, (qgb, egend, intra, update) in enumerate(work):
                st_ref[c, hd] = st
                o_ref[c * CHUNK:(c + 1) * CHUNK, hd * HG_DIM:(hd + 1) * HG_DIM] = (
                    intra[hd] + _dot_nt(_head(qgb, hd), st.astype(BF16)))
                st = st * _head(egend, hd) + update[hd]
            state[hd] = st

    def col(cb):
        return pl.BlockSpec((rows, HG_WIDTH), lambda i: (i, cb))

    return pl.pallas_call(
        body, name="hgrn_fwd", grid=(nc // ns,),
        in_specs=[col(P_HQ // HG_WIDTH), col(P_HF // HG_WIDTH), col(P_HI // HG_WIDTH), _full((2, HG_WIDTH))],
        out_specs=[pl.BlockSpec((rows, HG_WIDTH), lambda i: (i, 0)),
                   pl.BlockSpec((ns, HG_HEADS, HG_DIM, HG_DIM), lambda i: (i, 0, 0, 0))],
        out_shape=[jax.ShapeDtypeStruct((T, HG_WIDTH), F32),
                   jax.ShapeDtypeStruct((nc, HG_HEADS, HG_DIM, HG_DIM), F32)],
        scratch_shapes=[pltpu.VMEM((HG_HEADS, HG_DIM, HG_DIM), F32)],
        compiler_params=_params(1),
    )(proj, proj, proj, lbp)


def _hgrn_bwd(proj, lbp, do_hg, states):
    T = proj.shape[0]
    nc = T // CHUNK
    ns = min(HG_STEP_CHUNKS, nc)
    rows = ns * CHUNK
    steps = nc // ns

    def body(hq_ref, hf_ref, hi_ref, lbp_ref, do_ref, st_ref, dhq_ref, dhf_ref, dhi_ref, dlb_ref, dstate):
        @pl.when(pl.program_id(0) == 0)
        def _():
            dstate[...] = jnp.zeros_like(dstate)
            dlb_ref[...] = jnp.zeros_like(dlb_ref)

        lb = _lower_bound(lbp_ref[...])

        dst_all = [dstate[hd] for hd in range(HG_HEADS)]
        dlb = jnp.zeros_like(lb)
        last = lax.broadcasted_iota(jnp.int32, (CHUNK, HG_WIDTH), 0) == CHUNK - 1
        for c in reversed(range(ns)):
            r = slice(c * CHUNK, (c + 1) * CHUNK)
            hq = hq_ref[r, :]
            sq, sf, f, q, kk, gcum = _hg_gates(hq, hf_ref[r, :], lb)
            vb = hi_ref[r, :].astype(BF16)
            dob = do_ref[r, :].astype(BF16)
            blocks = _hg_blocks(q, kk, gcum)
            a = _hg_scores(blocks)
            gend = gcum[CHUNK - 1:CHUNK, :]
            eg, egend, ekend = jnp.exp(gcum), jnp.exp(gend), jnp.exp(gend - gcum)
            qg, kge = q * eg, kk * ekend
            qgb, kgeb = qg.astype(BF16), kge.astype(BF16)

            dv, dqg, dkge, st_dst, dq_blk, dk_blk = [], [], [], [], [], []
            for hd in range(HG_HEADS):
                st = st_ref[c, hd]
                dst = dst_all[hd]
                dstb = dst.astype(BF16)
                do_h, v_h = _head(dob, hd), _head(vb, hd)
                dv.append(_dot_tn(a[hd].astype(BF16), do_h) + _dot_nt(_head(kgeb, hd), dstb))
                da = jnp.where(_tri(True), _dot_nt(do_h, v_h), 0.0).astype(BF16)
                dqg.append(_dot(do_h, st.astype(BF16)))
                dkge.append(_dot(v_h, dstb))
                st_dst.append(jnp.sum(st * dst, axis=0, keepdims=True))
                dst_all[hd] = _dot_tn(do_h, _head(qgb, hd)) + dst * _head(egend, hd)
                dq_blk.append([_dot(da[b * SUB:(b + 1) * SUB, :], _head(kb, hd)) for b, (_, _, _, kb) in enumerate(blocks)])
                dk_blk.append([_dot_tn(da[b * SUB:(b + 1) * SUB, :], _head(qb, hd)) for b, (_, _, qb, _) in enumerate(blocks)])
            dv, dqg, dkge, st_dst = (jnp.concatenate(t, axis=1) for t in (dv, dqg, dkge, st_dst))

            dq_a, dg_q = [], []
            dk_a, dg_k = jnp.zeros_like(gcum), jnp.zeros_like(gcum)
            for b, (eq, ek, qb, kb) in enumerate(blocks):
                dq_b = _all_heads(lambda hd: dq_blk[hd][b])
                dk_b = _all_heads(lambda hd: dk_blk[hd][b])
                dq_a.append(dq_b * eq)
                dk_a = dk_a + dk_b * ek
                dg_q.append(qb.astype(F32) * dq_b)
                dg_k = dg_k + kb.astype(F32) * dk_b
            dq_a = jnp.concatenate(dq_a, axis=0)

            dgend = st_dst * egend + jnp.sum(dkge * kge, axis=0, keepdims=True)
            dq = dq_a + dqg * eg
            dk = dk_a + dkge * ekend
            dgc = jnp.concatenate(dg_q, axis=0) - dg_k + dqg * qg - dkge * kge + jnp.where(last, dgend, 0.0)
            dg = _running_sum(dgc, True)
            df = dg / f - dk
            dhf_ref[r, :] = df * (1.0 - lb) * sf * (1.0 - sf)
            dlb = dlb + jnp.sum(df * (1.0 - sf), axis=0, keepdims=True)
            dhq_ref[r, :] = dq * (sq * (1.0 + hq * (1.0 - sq)))
            dhi_ref[r, :] = dv
        for hd in range(HG_HEADS):
            dstate[hd] = dst_all[hd]
        dlb_ref[...] += dlb

    def col(cb):
        return pl.BlockSpec((rows, HG_WIDTH), lambda i: (steps - 1 - i, cb))

    grad = jax.ShapeDtypeStruct((T, HG_WIDTH), F32)
    return pl.pallas_call(
        body, name="hgrn_bwd", grid=(steps,),
        in_specs=[col(P_HQ // HG_WIDTH), col(P_HF // HG_WIDTH), col(P_HI // HG_WIDTH), _full((2, HG_WIDTH)),
                  col(0), pl.BlockSpec((ns, HG_HEADS, HG_DIM, HG_DIM), lambda i: (steps - 1 - i, 0, 0, 0))],
        out_specs=[col(0), col(0), col(0), _full((1, HG_WIDTH))],
        out_shape=[grad, grad, grad, jax.ShapeDtypeStruct((1, HG_WIDTH), F32)],
        scratch_shapes=[pltpu.VMEM((HG_HEADS, HG_DIM, HG_DIM), F32)],
        compiler_params=_params(1),
    )(proj, proj, proj, lbp, do_hg, states)


def _top(x, tgt, o_mla, o_hg, proj, w_out, hg_norm_g, final_g, tm):
    T = x.shape[0]

    def body(x_ref, tgt_ref, om_ref, oh_ref, gm_ref, gh_ref, wout_ref, hgn_ref, fng_ref,
             dx2_ref, dom_ref, dsum_ref, dgm_ref, doh_ref, dgh_ref, loss_ref, dfng_ref, dhgn_ref, dwout_ref, ycat_ref):
        @pl.when(pl.program_id(0) == 0)
        def _():
            for ref in (loss_ref, dfng_ref, dhgn_ref, dwout_ref):
                ref[...] = jnp.zeros_like(ref)

        gm, om = gm_ref[...], om_ref[...]
        sgm = _sigmoid(gm)
        silu_m = gm * sgm
        gh, oh, gam = gh_ref[...], oh_ref[...], hgn_ref[...]
        sgh = _sigmoid(gh)
        silu_h = gh * sgh
        rr, nn = [], []
        for hd in range(HG_HEADS):
            oh_h = oh[:, hd * HG_DIM:(hd + 1) * HG_DIM]
            r_h = lax.rsqrt(jnp.mean(oh_h * oh_h, axis=-1, keepdims=True) + EPS)
            rr.append(r_h)
            nn.append(oh_h * r_h)
        n = jnp.concatenate(nn, axis=1)
        ng = n * gam
        ycat_ref[:, :MLA_WIDTH] = (om * silu_m).astype(BF16)
        ycat_ref[:, MLA_WIDTH:] = (ng * silu_h).astype(BF16)
        wout = wout_ref[...]
        x2 = x_ref[...] + _dot(ycat_ref[...], wout)
        r = lax.rsqrt(jnp.mean(x2 * x2, axis=-1, keepdims=True) + EPS)
        xh = x2 * r
        fng = fng_ref[...]
        err = xh * fng - tgt_ref[...]
        loss_ref[...] += 0.5 * jnp.sum(jnp.mean(err * err, axis=-1, keepdims=True), axis=0, keepdims=True)
        dout = err * (1.0 / D_MODEL)
        dfng_ref[...] += jnp.sum(dout * xh, axis=0, keepdims=True)
        dxh = dout * fng
        dx2 = r * (dxh - xh * jnp.mean(dxh * xh, axis=-1, keepdims=True))
        dx2_ref[...] = dx2
        dx2b = dx2.astype(BF16)
        dwout_ref[...] += _dot_tn(ycat_ref[...], dx2b)
        dycat = _dot_nt(dx2b, wout)
        dym, dyh = dycat[:, :MLA_WIDTH], dycat[:, MLA_WIDTH:]
        dom = dym * silu_m
        first = lax.broadcasted_iota(jnp.int32, (tm, LANES), 1) < V_DIM
        for pp in range(N_HEADS // 2):
            pair = dom[:, pp * LANES:(pp + 1) * LANES]
            dom_ref[:, 2 * pp * HEAD_LANES:(2 * pp + 1) * HEAD_LANES] = jnp.where(first, pair, 0.0).astype(BF16)
            dom_ref[:, (2 * pp + 1) * HEAD_LANES:(2 * pp + 2) * HEAD_LANES] = jnp.where(first, 0.0, pair).astype(BF16)
        head_of = lax.broadcasted_iota(jnp.int32, (MLA_WIDTH, LANES), 0) // V_DIM
        pick = (head_of == lax.broadcasted_iota(jnp.int32, (MLA_WIDTH, LANES), 1)).astype(F32)
        dsum_ref[...] = jnp.dot(dom * om, pick, precision=lax.Precision.HIGHEST, preferred_element_type=F32)
        dgm_ref[...] = dym * om * (sgm * (1.0 + gm * (1.0 - sgm)))
        dgh_ref[...] = dyh * ng * (sgh * (1.0 + gh * (1.0 - sgh)))
        dng = dyh * silu_h
        dhgn_ref[...] += jnp.sum(dng * n, axis=0, keepdims=True)
        dn = dng * gam
        for hd in range(HG_HEADS):
            sl = slice(hd * HG_DIM, (hd + 1) * HG_DIM)
            dn_h, n_h = dn[:, sl], nn[hd]
            doh_ref[:, sl] = rr[hd] * (dn_h - n_h * jnp.mean(dn_h * n_h, axis=-1, keepdims=True))

    def row(w, cb=0):
        return pl.BlockSpec((tm, w), lambda i: (i, cb))

    outs = [(D_MODEL, F32), (N_HEADS * HEAD_LANES, BF16), (LANES, F32), (MLA_WIDTH, F32), (HG_WIDTH, F32), (HG_WIDTH, F32)]
    small = [(1, SMALL_W), (1, D_MODEL), (1, HG_WIDTH), (D_MODEL, D_MODEL)]
    return pl.pallas_call(
        body, name="top", grid=(T // tm,),
        in_specs=[row(D_MODEL), row(D_MODEL), row(MLA_WIDTH), row(HG_WIDTH),
                  row(MLA_WIDTH, P_GM // MLA_WIDTH), row(HG_WIDTH, P_GH // HG_WIDTH),
                  _full((D_MODEL, D_MODEL)), _full((1, HG_WIDTH)), _full((1, D_MODEL))],
        out_specs=[row(w) for w, _ in outs] + [_full(s) for s in small],
        out_shape=[jax.ShapeDtypeStruct((T, w), dt) for w, dt in outs] + [jax.ShapeDtypeStruct(s, F32) for s in small],
        scratch_shapes=[pltpu.VMEM((tm, D_MODEL), BF16)],
        compiler_params=_params(1),
    )(x, tgt, o_mla, o_hg, proj, proj, w_out, hg_norm_g, final_g)


def _bot(x, dx2, proj, h, qn, kvn, dq, dk, dv, dgm, dhq, dhf, dhi, dgh, c_t, s1_t, s2_t, w_in_t, w_q_p, w_kv_p, ln_g, q_g, kv_g, tm):
    T = x.shape[0]
    lat_w = D_PERM - P_QL
    steps = T // tm

    def body(x_ref, dx2_ref, lat_ref, h_ref, qn_ref, kvn_ref, dq_ref, dk_ref, dv_ref, dgm_ref, dhq_ref, dhf_ref,
             dhi_ref, dgh_ref, c_ref, s1_ref, s2_ref, win_ref, wq_ref, wkv_ref, lng_ref, qg_ref, kvg_ref,
             dx_ref, dlng_ref, dqg_ref, dkvg_ref, dwq_ref, dwkv_ref, dwin_ref, dqpre_ref, dkv_ref, dproj_ref,
             dwin_acc, sem):
        @pl.when(pl.program_id(0) == 0)
        def _():
            for ref in (dlng_ref, dqg_ref, dkvg_ref, dwq_ref, dwkv_ref, dwin_acc):
                ref[...] = jnp.zeros_like(ref)

        c, s1, s2 = c_ref[...], s1_ref[...], s2_ref[...]
        dkpe = jnp.zeros((tm, LANES), F32)
        for hd in range(N_HEADS):
            sl = slice(hd * HEAD_LANES, (hd + 1) * HEAD_LANES)
            dqpre_ref[:, sl] = _rope_bwd(dq_ref[:, sl], c, s1, s2).astype(BF16)
            dk_h = dk_ref[:, sl]
            dkpe = dkpe + dk_h
            dkv_ref[:, sl] = dk_h.astype(BF16)
        dkv_ref[:, N_HEADS * HEAD_LANES:] = dv_ref[...].astype(BF16)
        lane = lax.broadcasted_iota(jnp.int32, (tm, LANES), 1)
        rope_lanes = jnp.logical_and(lane >= ROPE_LO, lane < ROPE_LO + ROPE)
        dkr = jnp.where(rope_lanes, _rope_bwd(dkpe, c, s1, s2), 0.0)

        def norm_bwd(v, g, dy):
            r = lax.rsqrt(jnp.mean(v * v, axis=-1, keepdims=True) + EPS)
            vh = v * r
            dvh = dy * g
            return jnp.sum(dy * vh, axis=0, keepdims=True), r * (dvh - vh * jnp.mean(dvh * vh, axis=-1, keepdims=True))

        dwq_ref[...] += _dot_tn(qn_ref[...], dqpre_ref[...])
        dwkv_ref[...] += _dot_tn(kvn_ref[...], dkv_ref[...])
        dqn = _dot_nt(dqpre_ref[...], wq_ref[...])
        dg_q, dql = norm_bwd(lat_ref[:, :Q_RANK], qg_ref[...], dqn)
        dqg_ref[...] += dg_q
        dkn = _dot_nt(dkv_ref[...], wkv_ref[...])
        dg_kv, dkvl = norm_bwd(lat_ref[:, Q_RANK:Q_RANK + KV_RANK], kvg_ref[...], dkn)
        dkvg_ref[...] += dg_kv

        dproj_ref[:, P_GM:P_GM + MLA_WIDTH] = dgm_ref[...].astype(BF16)
        dproj_ref[:, P_HQ:P_HQ + HG_WIDTH] = dhq_ref[...].astype(BF16)
        dproj_ref[:, P_HF:P_HF + HG_WIDTH] = dhf_ref[...].astype(BF16)
        dproj_ref[:, P_HI:P_HI + HG_WIDTH] = dhi_ref[...].astype(BF16)
        dproj_ref[:, P_GH:P_GH + HG_WIDTH] = dgh_ref[...].astype(BF16)
        dproj_ref[:, P_QL:P_QL + Q_RANK] = dql.astype(BF16)
        dproj_ref[:, P_KVL:P_KVL + KV_RANK] = dkvl.astype(BF16)
        dproj_ref[:, P_KR:P_KR + LANES] = dkr.astype(BF16)
        dh = sum(_dot(dproj_ref[:, col:col + rows.shape[0]], rows) for col, rows in _in_proj_rows(win_ref))
        dg_ln, dxn = norm_bwd(x_ref[...], lng_ref[...], dh)
        dlng_ref[...] += dg_ln
        dx_ref[...] = dx2_ref[...] + dxn
        dwin_acc[...] += _dot_tn(dproj_ref[...], h_ref[...])

        @pl.when(pl.program_id(0) == steps - 1)
        def _():
            kr = P_KR + ROPE_LO
            moves = [((P_GM, P_QL), R_MAIN), ((P_QL, P_KR), (R_QL[0], R_KVL[1])), ((kr, kr + ROPE), R_KR)]
            copies = [pltpu.make_async_copy(dwin_acc.at[a:b, :], dwin_ref.at[c:d, :], sem.at[n])
                      for n, ((a, b), (c, d)) in enumerate(moves)]
            for cp in copies:
                cp.start()
            for cp in copies:
                cp.wait()

    def row(w, cb=0):
        return pl.BlockSpec((tm, w), lambda i: (i, cb))

    hl = N_HEADS * HEAD_LANES
    outs = [(D_MODEL, F32)]
    small = [(1, D_MODEL), (1, Q_RANK), (1, KV_RANK), (Q_RANK, hl), (KV_RANK, hl + MLA_WIDTH)]
    return pl.pallas_call(
        body, name="bot", grid=(steps,),
        in_specs=[row(D_MODEL), row(D_MODEL), row(lat_w, P_QL // lat_w), row(D_MODEL), row(Q_RANK), row(KV_RANK),
                  row(hl), row(hl), row(MLA_WIDTH),
                  row(MLA_WIDTH), row(HG_WIDTH), row(HG_WIDTH), row(HG_WIDTH), row(HG_WIDTH),
                  row(LANES), row(LANES), row(LANES),
                  _full((D_IN, D_MODEL)), _full((Q_RANK, hl)), _full((KV_RANK, hl + MLA_WIDTH)),
                  _full((1, D_MODEL)), _full((1, Q_RANK)), _full((1, KV_RANK))],
        out_specs=[row(w) for w, _ in outs] + [_full(s) for s in small] + [pl.BlockSpec(memory_space=pl.ANY)],
        out_shape=[jax.ShapeDtypeStruct((T, w), dt) for w, dt in outs] + [jax.ShapeDtypeStruct(s, F32) for s in small]
        + [jax.ShapeDtypeStruct((D_IN, D_MODEL), F32)],
        scratch_shapes=[pltpu.VMEM((tm, hl), BF16), pltpu.VMEM((tm, hl + MLA_WIDTH), BF16),
                        pltpu.VMEM((tm, D_PERM), BF16), pltpu.VMEM((D_PERM, D_MODEL), F32),
                        pltpu.SemaphoreType.DMA((3,))],
        compiler_params=_params(1),
    )(x, dx2, proj, h, qn, kvn, dq, dk, dv, dgm, dhq, dhf, dhi, dgh, c_t, s1_t, s2_t, w_in_t, w_q_p, w_kv_p, ln_g,
      q_g, kv_g)


RS_ROWS = 256


def _reduce_scatter(slabs, small):
    n = len(slabs)
    units = []
    for a, s in enumerate(slabs):
        rows, cols = s.shape[1:]
        if rows % RS_ROWS == 0 or rows < RS_ROWS:
            units += [(a, (pl.ds(r0, min(rows, RS_ROWS)), slice(None))) for r0 in range(0, rows, RS_ROWS)]
        else:
            units += [(a, (slice(None), pl.ds(c0, RS_ROWS))) for c0 in range(0, cols, RS_ROWS)]
    nu = len(units)

    def body(*refs):
        ins, small_ref = refs[:n], refs[n]
        outs, small_out = refs[n + 1:2 * n + 1], refs[2 * n + 1]
        own, sib_land, ici_out, ici_land = (refs[(2 + g) * n + 2:(3 + g) * n + 2] for g in range(4))
        small_land = refs[6 * n + 2]
        loc_sems, d2d_send, d2d_recv, ici_send, ici_recv, sm_send, sm_recv = refs[6 * n + 3:6 * n + 10]
        x, y, c = lax.axis_index("x"), lax.axis_index("y"), lax.axis_index("c")
        me = 4 * x + 2 * y + c

        def chip(k):
            return (1 - x if k & 2 else x, 1 - y if k & 1 else y)

        def block(k, core):
            px, py = chip(k)
            return 4 * px + 2 * py + core

        def part(u):
            return units[u]

        def local(u, k):
            a, rows = part(u)
            return pltpu.make_async_copy(ins[a].at[(block(k, c),) + rows], own[a].at[(k,) + rows], loc_sems.at[u, k])

        def to_sibling(u, k):
            a, rows = part(u)
            return pltpu.make_async_remote_copy(
                src_ref=ins[a].at[(block(k, 1 - c),) + rows], dst_ref=sib_land[a].at[(k,) + rows],
                send_sem=d2d_send.at[u, k], recv_sem=d2d_recv.at[u, k], device_id=(x, y, 1 - c), device_id_type=MESH)

        def to_chip(u, k):
            a, rows = part(u)
            return pltpu.make_async_remote_copy(
                src_ref=ici_out[a].at[(k - 1,) + rows], dst_ref=ici_land[a].at[(k - 1,) + rows],
                send_sem=ici_send.at[u, k - 1], recv_sem=ici_recv.at[u, k - 1], device_id=(*chip(k), c),
                device_id_type=MESH)

        def small_copy(k, receiving):
            px, py = chip(k >> 1)
            pc = 1 - c if k & 1 else c
            slot = 4 * px + 2 * py + pc if receiving else me
            return pltpu.make_async_remote_copy(
                src_ref=small_ref, dst_ref=small_land.at[slot], send_sem=sm_send.at[k - 1], recv_sem=sm_recv.at[k - 1],
                device_id=(px, py, pc), device_id_type=MESH)

        for u in range(nu):
            for k in range(4):
                local(u, k).start()
        for u in range(nu):
            for k in range(4):
                to_sibling(u, k).start()
        small_land[me] = small_ref[...]
        for k in range(1, N_DEV):
            small_copy(k, False).start()
        for u in range(nu):
            a, rows = part(u)
            for k in range(4):
                local(u, k).wait()
                to_sibling(u, k).wait_recv()
            for k in range(1, 4):
                ici_out[a][(k - 1,) + rows] = (own[a][(k,) + rows] + sib_land[a][(k,) + rows]).astype(BF16)
                to_chip(u, k).start()
        for u in range(nu):
            a, rows = part(u)
            acc = own[a][(0,) + rows] + sib_land[a][(0,) + rows]
            for k in range(1, 4):
                to_chip(u, k).wait_recv()
                acc = acc + ici_land[a][(k - 1,) + rows].astype(F32)
            outs[a][rows] = acc
        for k in range(1, N_DEV):
            small_copy(k, True).wait_recv()
        acc = small_land[0]
        for d in range(1, N_DEV):
            acc = acc + small_land[d]
        small_out[...] = acc
        for u in range(nu):
            for k in range(4):
                to_sibling(u, k).wait_send()
            for k in range(1, 4):
                to_chip(u, k).wait_send()
        for k in range(1, N_DEV):
            small_copy(k, False).wait_send()

    vm = pl.BlockSpec(memory_space=pltpu.VMEM)
    hbm = pl.BlockSpec(memory_space=pl.ANY)
    dma = pltpu.SemaphoreType.DMA
    return pl.pallas_call(
        body, name="reduce_scatter_grads",
        in_specs=[hbm] * n + [vm], out_specs=[vm] * (n + 1),
        out_shape=[jax.ShapeDtypeStruct(s.shape[1:], F32) for s in slabs] + [jax.ShapeDtypeStruct(small.shape, F32)],
        scratch_shapes=[pltpu.VMEM((4,) + s.shape[1:], F32) for s in slabs] * 2
        + [pltpu.VMEM((3,) + s.shape[1:], BF16) for s in slabs] * 2
        + [pltpu.VMEM((N_DEV,) + small.shape, F32)]
        + [dma((nu, 4)), dma((nu, 4)), dma((nu, 4)), dma((nu, 3)), dma((nu, 3)), dma((N_DEV - 1,)), dma((N_DEV - 1,))],
        compiler_params=pltpu.CompilerParams(vmem_limit_bytes=VMEM_LIMIT),
    )(*slabs, small)


def _adamw_math(w, g, m, v):
    m = ADAM_B1 * m + (1.0 - ADAM_B1) * g
    v = ADAM_B2 * v + (1.0 - ADAM_B2) * (g * g)
    m_hat = m / (1.0 - ADAM_B1 ** ADAM_STEP)
    v_hat = v / (1.0 - ADAM_B2 ** ADAM_STEP)
    delta = -ADAM_LR * (m_hat / (jnp.sqrt(v_hat) + ADAM_EPS) + ADAM_WD * w)
    return delta, m, v


SMALL_W = 512


def _adamw(big, small_w, small_g):
    nb, ns = len(big), len(small_w)

    def body(*refs):
        k = 0
        big_in = [refs[4 * i:4 * i + 4] for i in range(nb)]
        k = 4 * nb
        small_in = [refs[k + 3 * i:k + 3 * i + 3] for i in range(ns)]
        k += 3 * ns
        sg_ref = refs[k]
        k += 1
        big_out = [refs[k + 3 * i:k + 3 * i + 3] for i in range(nb)]
        k += 3 * nb
        small_out = [refs[k + 4 * i:k + 4 * i + 4] for i in range(ns)]

        for (w, g, m, v), (od, om, ov) in zip(big_in, big_out):
            od[...], om[...], ov[...] = _adamw_math(w[...], g[...], m[...], v[...])

        sg = sg_ref[...]
        lbp = small_in[2][0][...]
        lb = _lower_bound(lbp)
        t = sg[4:5, :] * lb * (1.0 - lb)
        grads = [jnp.concatenate([sg[0:1, :], sg[1:2, :]], axis=1),
                 jnp.concatenate([sg[2:3, :], sg[3:4, :]], axis=1),
                 jnp.concatenate([t, -t], axis=0),
                 sg[6:7, :], sg[7:8, 0:Q_RANK], sg[7:8, Q_RANK:Q_RANK + KV_RANK]]
        for (w, m, v), g, (og, od, om, ov) in zip(small_in, grads, small_out):
            og[...] = g
            od[...], om[...], ov[...] = _adamw_math(w[...], g, m[...], v[...])

    ins = [a for grp in big for a in grp] + [a for grp in small_w for a in grp] + [small_g]
    out_shape = ([jax.ShapeDtypeStruct(grp[0].shape, F32) for grp in big for _ in range(3)]
                 + [jax.ShapeDtypeStruct(grp[0].shape, F32) for grp in small_w for _ in range(4)])
    vm = pl.BlockSpec(memory_space=pltpu.VMEM)
    res = pl.pallas_call(
        body, name="adamw", in_specs=[vm] * len(ins), out_specs=[vm] * len(out_shape), out_shape=out_shape,
        compiler_params=pltpu.CompilerParams(vmem_limit_bytes=VMEM_LIMIT),
    )(*ins)
    big_res = [res[3 * i:3 * i + 3] for i in range(nb)]
    small_res = [res[3 * nb + 4 * i:3 * nb + 4 * i + 4] for i in range(ns)]
    return big_res, small_res


def _perm_weights(g_in_t, g_q, g_kv, g_out):
    w_in_t = g_in_t.reshape(D_IN, D_MODEL)
    wq = g_q.transpose(1, 0, 2)
    w_q_p = jnp.pad(wq, ((0, 0), (0, 0), (0, HEAD_LANES - NOPE - ROPE))).reshape(Q_RANK, N_HEADS * HEAD_LANES)
    wkv = g_kv.transpose(1, 0, 2)
    wk = jnp.pad(wkv[:, :, :NOPE], ((0, 0), (0, 0), (0, HEAD_LANES - NOPE))).reshape(KV_RANK, N_HEADS * HEAD_LANES)
    wv = wkv[:, :, NOPE:].reshape(KV_RANK, MLA_WIDTH)
    return w_in_t, w_q_p, jnp.concatenate([wk, wv], axis=1), g_out.reshape(D_MODEL, D_MODEL)


def _grad_slabs(dw_in_t, dw_q_p, dw_kv_p, dw_out):
    s_in = dw_in_t.reshape(N_DEV, D_IN // N_DEV, D_MODEL)
    s_q = dw_q_p.reshape(Q_RANK, N_HEADS, HEAD_LANES)[:, :, :NOPE + ROPE].transpose(1, 0, 2)
    hl = N_HEADS * HEAD_LANES
    dk = dw_kv_p[:, :hl].reshape(KV_RANK, N_HEADS, HEAD_LANES)[:, :, :NOPE]
    dv = dw_kv_p[:, hl:].reshape(KV_RANK, N_HEADS, V_DIM)
    s_kv = jnp.concatenate([dk, dv], axis=2).transpose(1, 0, 2)
    return s_in, s_q, s_kv, dw_out.reshape(N_DEV, D_MODEL // N_DEV, D_MODEL)


def _block_sizes(T):
    return min(256, T), min(256, T), min(512, T)


def kernel(x, positions, ln_g, w_in, q_a_norm_g, w_q_b, kv_a_norm_g, w_kv_b, hg_lower_bounds, hg_norm_g, w_out, final_norm_g, loss_target, m_ln_g, m_w_in, m_q_a_norm_g, m_w_q_b, m_kv_a_norm_g, m_w_kv_b, m_hg_lower_bounds, m_hg_norm_g, m_w_out, m_final_norm_g, v_ln_g, v_w_in, v_q_a_norm_g, v_w_q_b, v_kv_a_norm_g, v_w_kv_b, v_hg_lower_bounds, v_hg_norm_g, v_w_out, v_final_norm_g):
    T = x.shape[1]
    tm, tq, bt = _block_sizes(T)
    nq = T // tq
    xs, tgt = x[0], loss_target[0]
    pos_f = positions.astype(F32)
    fng = final_norm_g.reshape(1, D_MODEL)

    w_in_shard_t = w_in[0].T
    gathered, (c_t, s1_t, s2_t) = _all_gather_weights([w_in_shard_t, w_q_b[0], w_kv_b[0], w_out[0]], pos_f)
    w_in_t, w_q_p, w_kv_p, w_out_b = _perm_weights(*gathered)

    proj, h, qn, kvn, q, k, v = _fwd_in(xs, ln_g, w_in_t, q_a_norm_g, w_q_p, kv_a_norm_g, w_kv_p, c_t, s1_t, s2_t, bt)
    hl = N_HEADS * HEAD_LANES
    v_t = v.reshape(nq, tq, MLA_WIDTH).transpose(0, 2, 1)
    k_t = k.reshape(nq, tq, hl).transpose(0, 2, 1)
    q_t = q.reshape(nq, tq, hl).transpose(0, 2, 1)
    o_mla, lse = _attn_fwd_flat(k, q_t, v_t, tq)
    o_hg, states = _hgrn_fwd(proj, hg_lower_bounds)
    dx2, d_om, dsum, d_gm, d_oh, d_gh, loss_p, d_fng, d_hgn, dw_out = _top(
        xs, tgt, o_mla, o_hg, proj, w_out_b, hg_norm_g, fng, tm)
    dsum = dsum[:, :N_HEADS].T.reshape(N_HEADS, nq, 1, tq)
    do_t = d_om.reshape(nq, tq, hl).transpose(0, 2, 1)
    dq_t, dk, dv = _attn_bwd_flat(k, v, q_t, k_t, do_t, lse, dsum, tq)
    dq = dq_t.transpose(0, 2, 1).reshape(T, N_HEADS * HEAD_LANES)
    d_hq, d_hf, d_hi, d_lb = _hgrn_bwd(proj, hg_lower_bounds, d_oh, states)
    dx, d_lng, d_qg, d_kvg, dw_q_p, dw_kv_p, dw_in_t = _bot(
        xs, dx2, proj, h, qn, kvn, dq, dk, dv, d_gm, d_hq, d_hf, d_hi, d_gh, c_t, s1_t, s2_t, w_in_t, w_q_p, w_kv_p,
        ln_g, q_a_norm_g, kv_a_norm_g, tm)

    small = jnp.concatenate([
        d_lng.reshape(2, SMALL_W), d_fng.reshape(2, SMALL_W), d_lb, loss_p, d_hgn,
        jnp.concatenate([d_qg, d_kvg, jnp.zeros((1, SMALL_W - Q_RANK - KV_RANK), F32)], axis=1)], axis=0)
    g_in, g_q, g_kv, g_out, small_sum = _reduce_scatter(list(_grad_slabs(dw_in_t, dw_q_p, dw_kv_p, dw_out)), small)

    big = [(w_in_shard_t, g_in, m_w_in[0].T, v_w_in[0].T), (w_q_b[0], g_q, m_w_q_b[0], v_w_q_b[0]),
           (w_kv_b[0], g_kv, m_w_kv_b[0], v_w_kv_b[0]), (w_out[0], g_out, m_w_out[0], v_w_out[0])]
    small_w = [(ln_g, m_ln_g, v_ln_g),
               (fng, m_final_norm_g.reshape(1, D_MODEL), v_final_norm_g.reshape(1, D_MODEL)),
               (hg_lower_bounds, m_hg_lower_bounds, v_hg_lower_bounds), (hg_norm_g, m_hg_norm_g, v_hg_norm_g),
               (q_a_norm_g, m_q_a_norm_g, v_q_a_norm_g), (kv_a_norm_g, m_kv_a_norm_g, v_kv_a_norm_g)]
    big_res, small_res = _adamw(big, small_w, small_sum)

    loss = small_sum[5, 0]
    (r_in, r_q, r_kv, r_out) = big_res
    (s_ln, s_fn, s_lb, s_hgn, s_qg, s_kvg) = small_res
    flat = lambda t: t.reshape(D_MODEL)
    lead = lambda t: t[None]
    grads = [s_ln[0], lead(g_in.T), s_qg[0], lead(g_q), s_kvg[0], lead(g_kv), s_lb[0], s_hgn[0], lead(g_out), flat(s_fn[0])]

    def pick(i):
        return [s_ln[i + 1], lead(r_in[i].T), s_qg[i + 1], lead(r_q[i]), s_kvg[i + 1], lead(r_kv[i]), s_lb[i + 1],
                s_hgn[i + 1], lead(r_out[i]), flat(s_fn[i + 1])]

    return (loss, dx[None], *grads, *pick(0), *pick(1), *pick(2))
```

```python
import math

import numpy as np
import jax
import jax.numpy as jnp
from jax import lax
from jax.experimental import pallas as pl
from jax.experimental.pallas import tpu as pltpu

F32 = jnp.float32
BF16 = jnp.bfloat16

D_MODEL = 1024
N_HEADS = 8
NOPE = 64
ROPE = 32
HALF_ROPE = ROPE // 2
V_DIM = 64
Q_RANK = 256
KV_RANK = 128
MLA_WIDTH = N_HEADS * V_DIM
HG_HEADS = 4
HG_DIM = 128
HG_WIDTH = HG_HEADS * HG_DIM
CHUNK = 64
SUB = 16
D_IN = 2976
D_PERM = 3072
ROPE_THETA = 10000.0
EPS = 1e-6
N_DEV = 8
LANES = 128
HEAD_LANES = 128

P_GM, P_HQ, P_HF, P_HI, P_GH, P_QL, P_KVL, P_KR = 0, 512, 1024, 1536, 2048, 2560, 2816, 2944
R_QL, R_KVL, R_KR, R_MAIN = (0, 256), (256, 384), (384, 416), (416, 2976)
ROPE_LO = NOPE
SCALE = 1.0 / math.sqrt(NOPE + ROPE)

ADAM_LR = 0.001
ADAM_B1 = 0.9
ADAM_B2 = 0.999
ADAM_EPS = 1e-08
ADAM_WD = 0.01
ADAM_STEP = 10

VMEM_LIMIT = 56 * 1024 * 1024
MESH = pl.DeviceIdType.MESH

NT = (((1,), (1,)), ((), ()))
TN = (((0,), (0,)), ((), ()))


def _params(n_grid=0, **kw):
    sem = ("arbitrary",) * n_grid if n_grid else None
    return pltpu.CompilerParams(dimension_semantics=sem, vmem_limit_bytes=VMEM_LIMIT, **kw)


def _dot(a, b):
    return jnp.dot(a, b, preferred_element_type=F32)


def _dot_nt(a, b):
    return lax.dot_general(a, b, NT, preferred_element_type=F32)


def _dot_tn(a, b):
    return lax.dot_general(a, b, TN, preferred_element_type=F32)


def _sigmoid(x):
    return 1.0 / (1.0 + jnp.exp(-x))


def _rope_fwd(x, c, s1, s2):
    return x * c + pltpu.roll(x, LANES - HALF_ROPE, 1) * s1 + pltpu.roll(x, HALF_ROPE, 1) * s2


def _rope_bwd(dy, c, s1, s2):
    return dy * c - pltpu.roll(dy, LANES - HALF_ROPE, 1) * s1 - pltpu.roll(dy, HALF_ROPE, 1) * s2


def _in_proj_rows(wt_ref):
    kr = wt_ref[R_KR[0]:R_KR[1], :]
    pad = lambda n: jnp.zeros((n, D_MODEL), kr.dtype)
    return ((P_GM, wt_ref[R_MAIN[0]:R_MAIN[1], :]), (P_QL, wt_ref[R_QL[0]:R_QL[1], :]),
            (P_KVL, wt_ref[R_KVL[0]:R_KVL[1], :]),
            (P_KR, jnp.concatenate([pad(ROPE_LO), kr, pad(LANES - ROPE_LO - ROPE)], axis=0)))


def _full(shape):
    n = len(shape)
    return pl.BlockSpec(shape, lambda *_: (0,) * n)


ROPE_BLOCK = 512


def _rope_constants():
    inv = (np.float32(ROPE_THETA) ** (-np.arange(HALF_ROPE, dtype=np.float32) / np.float32(HALF_ROPE))).astype(np.float32)
    place = np.zeros((3, HALF_ROPE, LANES), np.float32)
    for i in range(HALF_ROPE):
        place[0, i, ROPE_LO + i] = place[0, i, ROPE_LO + HALF_ROPE + i] = 1.0
        place[1, i, ROPE_LO + i] = -1.0
        place[2, i, ROPE_LO + HALF_ROPE + i] = 1.0
    base = np.ones((1, LANES), np.float32)
    base[0, ROPE_LO:ROPE_LO + ROPE] = 0.0
    return jnp.asarray(inv.reshape(HALF_ROPE, 1)), jnp.asarray(place), jnp.asarray(base)


def _rope_block(pos, inv, place_ref, base):
    ang = inv * pos
    cos, sin = jnp.cos(ang), jnp.sin(ang)

    def put(v, k):
        return lax.dot_general(v, place_ref[k], TN, precision=lax.Precision.HIGHEST, preferred_element_type=F32)

    return put(cos, 0) + base, put(sin, 1), put(sin, 2)


def _all_gather_weights(shards, pos_f):
    n = len(shards)
    T = pos_f.shape[1]
    rb = min(ROPE_BLOCK, T)

    def body(*refs):
        ins, (pos_ref, inv_ref, place_ref, base_ref) = refs[:n], refs[n:n + 4]
        outs, tables = refs[n + 4:2 * n + 4], refs[2 * n + 4:2 * n + 7]
        send_sems, recv_sems = refs[2 * n + 7], refs[2 * n + 8]
        x, y, c = lax.axis_index("x"), lax.axis_index("y"), lax.axis_index("c")
        me, sibling = (x, y, c), (x, y, 1 - c)
        chips = [(1 - x, y), (x, 1 - y), (1 - x, 1 - y)]

        def idx(d):
            return 4 * d[0] + 2 * d[1] + d[2]

        def copy(a, k, block, to):
            rows = outs[a].at[idx(block)]
            return pltpu.make_async_remote_copy(src_ref=rows, dst_ref=rows, send_sem=send_sems.at[a, k],
                                                recv_sem=recv_sems.at[a, k], device_id=to, device_id_type=MESH)

        for a in range(n):
            outs[a][idx(me)] = ins[a][...].astype(BF16)
        first = []
        for a in range(n):
            first.append(copy(a, 0, me, sibling))
            first += [copy(a, 1 + j, me, (*chip, c)) for j, chip in enumerate(chips)]
        for cp in first:
            cp.start()
        for r0 in range(0, T, rb):
            for ref, tab in zip(tables, _rope_block(pos_ref[:, r0:r0 + rb], inv_ref[...], place_ref, base_ref[...])):
                ref[r0:r0 + rb, :] = tab
        passed = []
        for j, chip in enumerate(chips):
            for a in range(n):
                copy(a, 1 + j, (*chip, c), me).wait_recv()
                cp = copy(a, 4 + j, (*chip, c), sibling)
                cp.start()
                passed.append(cp)
        for a in range(n):
            copy(a, 0, sibling, me).wait_recv()
            for j, chip in enumerate(chips):
                copy(a, 4 + j, (*chip, 1 - c), me).wait_recv()
        for cp in first + passed:
            cp.wait_send()

    vm = pl.BlockSpec(memory_space=pltpu.VMEM)
    res = pl.pallas_call(
        body, name="all_gather_weights",
        in_specs=[vm] * (n + 4), out_specs=[vm] * (n + 3),
        out_shape=[jax.ShapeDtypeStruct((N_DEV,) + s.shape, BF16) for s in shards]
        + [jax.ShapeDtypeStruct((T, LANES), F32)] * 3,
        scratch_shapes=[pltpu.SemaphoreType.DMA((n, 7)), pltpu.SemaphoreType.DMA((n, 7))],
        compiler_params=pltpu.CompilerParams(vmem_limit_bytes=VMEM_LIMIT),
    )(*shards, pos_f, *_rope_constants())
    return res[:n], res[n:]


def _fwd_in(x, ln_g, w_in_t, q_g, w_q_p, kv_g, w_kv_p, c_t, s1_t, s2_t, tm):
    T = x.shape[0]

    def body(x_ref, lng_ref, win_ref, qg_ref, wq_ref, kvg_ref, wkv_ref, c_ref, s1_ref, s2_ref,
             proj_ref, h_ref, qn_ref, kvn_ref, q_ref, k_ref, v_ref):
        xv = x_ref[...]
        r = lax.rsqrt(jnp.mean(xv * xv, axis=-1, keepdims=True) + EPS)
        h = (xv * r * lng_ref[...]).astype(BF16)
        h_ref[...] = h
        for col, rows in _in_proj_rows(win_ref):
            proj_ref[:, col:col + rows.shape[0]] = _dot_nt(h, rows)
        c, s1, s2 = c_ref[...], s1_ref[...], s2_ref[...]

        ql = proj_ref[:, P_QL:P_QL + Q_RANK]
        rq = lax.rsqrt(jnp.mean(ql * ql, axis=-1, keepdims=True) + EPS)
        qn = (ql * rq * qg_ref[...]).astype(BF16)
        qn_ref[...] = qn
        q = _dot(qn, wq_ref[...])
        for hd in range(N_HEADS):
            sl = slice(hd * HEAD_LANES, (hd + 1) * HEAD_LANES)
            q_ref[:, sl] = _rope_fwd(q[:, sl], c, s1, s2).astype(BF16)

        kvl = proj_ref[:, P_KVL:P_KVL + KV_RANK]
        rk = lax.rsqrt(jnp.mean(kvl * kvl, axis=-1, keepdims=True) + EPS)
        kvn = (kvl * rk * kvg_ref[...]).astype(BF16)
        kvn_ref[...] = kvn
        kv = _dot(kvn, wkv_ref[...])
        kpe = _rope_fwd(proj_ref[:, P_KR:P_KR + LANES], c, s1, s2)
        for hd in range(N_HEADS):
            sl = slice(hd * HEAD_LANES, (hd + 1) * HEAD_LANES)
            k_ref[:, sl] = (kv[:, sl] + kpe).astype(BF16)
        v_ref[...] = kv[:, N_HEADS * HEAD_LANES:].astype(BF16)

    def row(w):
        return pl.BlockSpec((tm, w), lambda i: (i, 0))

    outs = [(D_PERM, F32), (D_MODEL, BF16), (Q_RANK, BF16), (KV_RANK, BF16),
            (N_HEADS * HEAD_LANES, BF16), (N_HEADS * HEAD_LANES, BF16), (MLA_WIDTH, BF16)]
    return pl.pallas_call(
        body, name="fwd_in", grid=(T // tm,),
        in_specs=[row(D_MODEL), _full((1, D_MODEL)), _full((D_IN, D_MODEL)), _full((1, Q_RANK)),
                  _full((Q_RANK, N_HEADS * HEAD_LANES)), _full((1, KV_RANK)),
                  _full((KV_RANK, N_HEADS * HEAD_LANES + MLA_WIDTH)), row(LANES), row(LANES), row(LANES)],
        out_specs=[row(w) for w, _ in outs],
        out_shape=[jax.ShapeDtypeStruct((T, w), dt) for w, dt in outs],
        compiler_params=_params(1),
    )(x, ln_g, w_in_t, q_g, w_q_p, kv_g, w_kv_p, c_t, s1_t, s2_t)


LOG2E = 1.4426950408889634
SCALE2 = SCALE * LOG2E


def _causal(tq):
    r = lax.broadcasted_iota(jnp.int32, (tq, tq), 0)
    c = lax.broadcasted_iota(jnp.int32, (tq, tq), 1)
    return r <= c


MASKED = -1e30


def _causal_bias(bias_ref, tq):
    bias_ref[0] = jnp.zeros((tq, tq), F32)
    bias_ref[1] = jnp.where(_causal(tq), 0.0, MASKED)


def _tile_tables(nq, by_query):
    if by_query:
        pairs = [(j, i) for i in range(nq) for j in range(i + 1)]
    else:
        pairs = [(j, i) for j in range(nq) for i in range(nq - 1, j - 1, -1)]
    pairs.append(pairs[-1])
    jj, ii = np.array(pairs, np.int32).T
    return jnp.asarray(jj), jnp.asarray(ii), len(pairs) - 1


ATTN_TRIP = 8


def _walk_tiles(n, products, tile, flush, buf_a, buf_b):
    bufs = (buf_a, buf_b)
    products(0, buf_a)

    def trip(r, carry):
        for u in range(ATTN_TRIP):
            products(ATTN_TRIP * r + u + 1, bufs[(u + 1) % 2])
            tile(ATTN_TRIP * r + u, bufs[u % 2])
        for u in range(ATTN_TRIP):
            flush(ATTN_TRIP * r + u)
        return carry

    lax.fori_loop(0, n // ATTN_TRIP, trip, 0)
    rest = n - n % ATTN_TRIP
    for u in range(n % ATTN_TRIP):
        if rest + u + 1 < n:
            products(rest + u + 1, bufs[(u + 1) % 2])
        tile(rest + u, bufs[u % 2])
    for u in range(n % ATTN_TRIP):
        flush(rest + u)


def _attn_fwd_flat(k, q_t, v_t, tq):
    T = k.shape[0]
    nq = T // tq
    jj, ii, n = _tile_tables(nq, True)
    heads = [slice(hh * HEAD_LANES, (hh + 1) * HEAD_LANES) for hh in range(2)]

    def body(jj_ref, ii_ref, k_ref, qt_ref, vt_ref, o_ref, lse_ref, sa_ref, sb_ref, m_ref, l_ref, acc_ref, bias_ref):
        def reset(st):
            m_ref[st] = jnp.full(m_ref.shape[1:], MASKED, F32)
            l_ref[st] = jnp.zeros(l_ref.shape[1:], F32)
            acc_ref[st] = jnp.zeros(acc_ref.shape[1:], F32)

        _causal_bias(bias_ref, tq)
        for st in range(ATTN_TRIP):
            reset(st)

        def products(t, buf):
            j, i = jj_ref[t], ii_ref[t]
            kj = k_ref[pl.ds(pl.multiple_of(j * tq, tq), tq), :]
            for hh, sl in enumerate(heads):
                buf[hh] = _dot(kj[:, sl], qt_ref[i, sl, :])

        def tile(t, buf):
            j, i = jj_ref[t], ii_ref[t]
            vt = vt_ref[j]
            bias = bias_ref.at[(j == i).astype(jnp.int32)]
            st = i % ATTN_TRIP
            for hh in range(2):
                s = buf[hh] * SCALE2 + bias[...]
                m = m_ref[st, hh]
                m_new = jnp.maximum(m, jnp.max(s, axis=0, keepdims=True))
                alpha = jnp.exp2(m - m_new)
                p = jnp.exp2(s - m_new)
                m_ref[st, hh] = m_new
                l_ref[st, hh] = alpha * l_ref[st, hh] + jnp.sum(p, axis=0, keepdims=True)
                acc_ref[st, hh] = alpha * acc_ref[st, hh] + _dot(vt[hh * V_DIM:(hh + 1) * V_DIM, :], p.astype(BF16))

        def flush(t):
            j, i = jj_ref[t], ii_ref[t]

            @pl.when(j == i)
            def _():
                st = i % ATTN_TRIP
                out = jnp.concatenate([acc_ref[st, hh] / l_ref[st, hh] for hh in range(2)], axis=0)
                o_ref[pl.ds(pl.multiple_of(i * tq, tq), tq), :] = out.T
                for hh in range(2):
                    lse_ref[hh, i] = m_ref[st, hh] + jnp.log2(l_ref[st, hh])
                reset(st)

        _walk_tiles(n, products, tile, flush, sa_ref, sb_ref)

    smem = pl.BlockSpec(memory_space=pltpu.SMEM)
    return pl.pallas_call(
        body, name="attn_fwd", grid=(N_HEADS // 2,),
        in_specs=[smem, smem,
                  pl.BlockSpec((T, 2 * HEAD_LANES), lambda p: (0, p)),
                  pl.BlockSpec((nq, 2 * HEAD_LANES, tq), lambda p: (0, p, 0)),
                  pl.BlockSpec((nq, LANES, tq), lambda p: (0, p, 0))],
        out_specs=[pl.BlockSpec((T, LANES), lambda p: (0, p)),
                   pl.BlockSpec((2, nq, 1, tq), lambda p: (p, 0, 0, 0))],
        out_shape=[jax.ShapeDtypeStruct((T, MLA_WIDTH), F32), jax.ShapeDtypeStruct((N_HEADS, nq, 1, tq), F32)],
        scratch_shapes=[pltpu.VMEM((2, tq, tq), F32), pltpu.VMEM((2, tq, tq), F32),
                        pltpu.VMEM((ATTN_TRIP, 2, 1, tq), F32), pltpu.VMEM((ATTN_TRIP, 2, 1, tq), F32),
                        pltpu.VMEM((ATTN_TRIP, 2, V_DIM, tq), F32), pltpu.VMEM((2, tq, tq), F32)],
        compiler_params=_params(1),
    )(jj, ii, k, q_t, v_t)


def _attn_bwd_flat(k, v, q_t, k_t, do_t, lse, dsum, tq):
    T = k.shape[0]
    nq = T // tq
    jj, ii, n = _tile_tables(nq, False)
    heads = [slice(hh * HEAD_LANES, (hh + 1) * HEAD_LANES) for hh in range(2)]

    def body(jj_ref, ii_ref, k_ref, v_ref, qt_ref, kt_ref, dot_ref, lse_ref, dsum_ref, dqt_ref, dk_ref, dv_ref,
             ba_ref, bb_ref, dkt_ref, dvt_ref, bias_ref):
        _causal_bias(bias_ref, tq)
        dqt_ref[...] = jnp.zeros_like(dqt_ref)
        dkt_ref[...] = jnp.zeros_like(dkt_ref)
        dvt_ref[...] = jnp.zeros_like(dvt_ref)

        def products(t, buf):
            j, i = jj_ref[t], ii_ref[t]
            rows = pl.ds(pl.multiple_of(j * tq, tq), tq)
            for hh, sl in enumerate(heads):
                buf[hh] = _dot(k_ref[rows, sl], qt_ref[i, sl, :])
                buf[2 + hh] = _dot(v_ref[rows, :], dot_ref[i, sl, :])

        def tile(t, buf):
            j, i = jj_ref[t], ii_ref[t]
            bias = bias_ref.at[(j == i).astype(jnp.int32)]
            st = j % ATTN_TRIP
            for hh, sl in enumerate(heads):
                p = jnp.exp2(buf[hh] * SCALE2 + bias[...] - lse_ref[hh, i])
                ds = (p * (buf[2 + hh] - dsum_ref[hh, i]) * SCALE).astype(BF16)
                own = slice(hh * V_DIM, (hh + 1) * V_DIM)
                do_h = dot_ref[i, hh * HEAD_LANES + own.start:hh * HEAD_LANES + own.stop, :]
                dvt_ref[st, own, :] += _dot_nt(do_h, p.astype(BF16))
                used = slice(sl.start, sl.start + NOPE + ROPE)
                dkt_ref[st, used, :] += _dot_nt(qt_ref[i, used, :], ds)
                dqt_ref[i, used, :] += _dot(kt_ref[j, used, :], ds)

        def flush(t):
            j, i = jj_ref[t], ii_ref[t]

            @pl.when(j == i)
            def _():
                st = j % ATTN_TRIP
                rows = pl.ds(pl.multiple_of(j * tq, tq), tq)
                dk_ref[rows, :] = dkt_ref[st].T
                dv_ref[rows, :] = dvt_ref[st].T
                dkt_ref[st] = jnp.zeros(dkt_ref.shape[1:], F32)
                dvt_ref[st] = jnp.zeros(dvt_ref.shape[1:], F32)

        _walk_tiles(n, products, tile, flush, ba_ref, bb_ref)

    smem = pl.BlockSpec(memory_space=pltpu.SMEM)
    stat = pl.BlockSpec((2, nq, 1, tq), lambda p: (p, 0, 0, 0))
    blocks_t = pl.BlockSpec((nq, 2 * HEAD_LANES, tq), lambda p: (0, p, 0))
    return pl.pallas_call(
        body, name="attn_bwd", grid=(N_HEADS // 2,),
        in_specs=[smem, smem,
                  pl.BlockSpec((T, 2 * HEAD_LANES), lambda p: (0, p)),
                  pl.BlockSpec((T, LANES), lambda p: (0, p)),
                  blocks_t, blocks_t, blocks_t, stat, stat],
        out_specs=[blocks_t,
                   pl.BlockSpec((T, 2 * HEAD_LANES), lambda p: (0, p)),
                   pl.BlockSpec((T, LANES), lambda p: (0, p))],
        out_shape=[jax.ShapeDtypeStruct((nq, N_HEADS * HEAD_LANES, tq), F32),
                   jax.ShapeDtypeStruct((T, N_HEADS * HEAD_LANES), F32),
                   jax.ShapeDtypeStruct((T, MLA_WIDTH), F32)],
        scratch_shapes=[pltpu.VMEM((4, tq, tq), F32), pltpu.VMEM((4, tq, tq), F32),
                        pltpu.VMEM((ATTN_TRIP, 2 * HEAD_LANES, tq), F32), pltpu.VMEM((ATTN_TRIP, LANES, tq), F32),
                        pltpu.VMEM((2, tq, tq), F32)],
        compiler_params=_params(1),
    )(jj, ii, k, v, q_t, k_t, do_t, lse, dsum)


def _lower_bound(lbp):
    a, b = lbp[0:1, :], lbp[1:2, :]
    mx = jnp.maximum(a, b)
    ea, eb = jnp.exp(a - mx), jnp.exp(b - mx)
    return ea / (ea + eb)


def _tri(lower):
    r = lax.broadcasted_iota(jnp.int32, (CHUNK, CHUNK), 0)
    c = lax.broadcasted_iota(jnp.int32, (CHUNK, CHUNK), 1)
    return (c <= r) if lower else (c >= r)


def _running_sum(x, from_end):
    row = lax.broadcasted_iota(jnp.int32, x.shape, 0)
    step = 1
    while step < CHUNK:
        if from_end:
            x = x + jnp.where(row < CHUNK - step, pltpu.roll(x, CHUNK - step, 0), 0.0)
        else:
            x = x + jnp.where(row >= step, pltpu.roll(x, step, 0), 0.0)
        step *= 2
    return x


def _hg_gates(hq, hf, lb):
    sq = _sigmoid(hq)
    sf = _sigmoid(hf)
    f = lb + (1.0 - lb) * sf
    g = jnp.log(f)
    gcum = _running_sum(g, False)
    return sq, sf, f, hq * sq, 1.0 - f, gcum


def _head(x, hd):
    return x[:, hd * HG_DIM:(hd + 1) * HG_DIM]


def _all_heads(fn):
    return jnp.concatenate([fn(hd) for hd in range(HG_HEADS)], axis=1)


def _hg_blocks(q, kk, gcum):
    rowi = lax.broadcasted_iota(jnp.int32, gcum.shape, 0)
    out = []
    for blk in range(CHUNK // SUB):
        lo, hi = blk * SUB, (blk + 1) * SUB
        gb = gcum[lo - 1:lo, :] if blk else jnp.zeros_like(gcum[0:1, :])
        eq = jnp.exp(gcum[lo:hi, :] - gb)
        ek = jnp.exp(jnp.where(rowi < hi, gb - gcum, 0.0))
        out.append((eq, ek, (q[lo:hi, :] * eq).astype(BF16), (kk * ek).astype(BF16)))
    return out


def _hg_scores(blocks):
    out = []
    for hd in range(HG_HEADS):
        a = jnp.concatenate([_dot_nt(_head(qb, hd), _head(kb, hd)) for _, _, qb, kb in blocks], axis=0)
        out.append(jnp.where(_tri(True), a, 0.0))
    return out


HG_STEP_CHUNKS = 8


def _hgrn_fwd(proj, lbp):
    T = proj.shape[0]
    nc = T // CHUNK
    ns = min(HG_STEP_CHUNKS, nc)
    rows = ns * CHUNK

    def body(hq_ref, hf_ref, hi_ref, lbp_ref, o_ref, st_ref, state):
        @pl.when(pl.program_id(0) == 0)
        def _():
            state[...] = jnp.zeros_like(state)

        lb = _lower_bound(lbp_ref[...])
        work = []
        for c in range(ns):
            r = slice(c * CHUNK, (c + 1) * CHUNK)
            _, _, _, q, kk, gcum = _hg_gates(hq_ref[r, :], hf_ref[r, :], lb)
            vb = hi_ref[r, :].astype(BF16)
            a = _hg_scores(_hg_blocks(q, kk, gcum))
            gend = gcum[CHUNK - 1:CHUNK, :]
            qgb = (q * jnp.exp(gcum)).astype(BF16)
            kgeb = (kk * jnp.exp(gend - gcum)).astype(BF16)
            intra = [_dot(a[hd].astype(BF16), _head(vb, hd)) for hd in range(HG_HEADS)]
            update = [_dot_tn(_head(vb, hd), _head(kgeb, hd)) for hd in range(HG_HEADS)]
            work.append((qgb, jnp.exp(gend), intra, update))
        for hd in range(HG_HEADS):
            st = state[hd]
            for c, (qgb, egend, intra, update) in enumerate(work):
                st_ref[c, hd] = st
                o_ref[c * CHUNK:(c + 1) * CHUNK, hd * HG_DIM:(hd + 1) * HG_DIM] = (
                    intra[hd] + _dot_nt(_head(qgb, hd), st.astype(BF16)))
                st = st * _head(egend, hd) + update[hd]
            state[hd] = st

    def col(cb):
        return pl.BlockSpec((rows, HG_WIDTH), lambda i: (i, cb))

    return pl.pallas_call(
        body, name="hgrn_fwd", grid=(nc // ns,),
        in_specs=[col(P_HQ // HG_WIDTH), col(P_HF // HG_WIDTH), col(P_HI // HG_WIDTH), _full((2, HG_WIDTH))],
        out_specs=[pl.BlockSpec((rows, HG_WIDTH), lambda i: (i, 0)),
                   pl.BlockSpec((ns, HG_HEADS, HG_DIM, HG_DIM), lambda i: (i, 0, 0, 0))],
        out_shape=[jax.ShapeDtypeStruct((T, HG_WIDTH), F32),
                   jax.ShapeDtypeStruct((nc, HG_HEADS, HG_DIM, HG_DIM), F32)],
        scratch_shapes=[pltpu.VMEM((HG_HEADS, HG_DIM, HG_DIM), F32)],
        compiler_params=_params(1),
    )(proj, proj, proj, lbp)


def _hgrn_bwd(proj, lbp, do_hg, states):
    T = proj.shape[0]
    nc = T // CHUNK
    ns = min(HG_STEP_CHUNKS, nc)
    rows = ns * CHUNK
    steps = nc // ns

    def body(hq_ref, hf_ref, hi_ref, lbp_ref, do_ref, st_ref, dhq_ref, dhf_ref, dhi_ref, dlb_ref, dstate):
        @pl.when(pl.program_id(0) == 0)
        def _():
            dstate[...] = jnp.zeros_like(dstate)
            dlb_ref[...] = jnp.zeros_like(dlb_ref)

        lb = _lower_bound(lbp_ref[...])

        dst_all = [dstate[hd] for hd in range(HG_HEADS)]
        dlb = jnp.zeros_like(lb)
        last = lax.broadcasted_iota(jnp.int32, (CHUNK, HG_WIDTH), 0) == CHUNK - 1
        for c in reversed(range(ns)):
            r = slice(c * CHUNK, (c + 1) * CHUNK)
            hq = hq_ref[r, :]
            sq, sf, f, q, kk, gcum = _hg_gates(hq, hf_ref[r, :], lb)
            vb = hi_ref[r, :].astype(BF16)
            dob = do_ref[r, :].astype(BF16)
            blocks = _hg_blocks(q, kk, gcum)
            a = _hg_scores(blocks)
            gend = gcum[CHUNK - 1:CHUNK, :]
            eg, egend, ekend = jnp.exp(gcum), jnp.exp(gend), jnp.exp(gend - gcum)
            qg, kge = q * eg, kk * ekend
            qgb, kgeb = qg.astype(BF16), kge.astype(BF16)

            dv, dqg, dkge, st_dst, dq_blk, dk_blk = [], [], [], [], [], []
            for hd in range(HG_HEADS):
                st = st_ref[c, hd]
                dst = dst_all[hd]
                dstb = dst.astype(BF16)
                do_h, v_h = _head(dob, hd), _head(vb, hd)
                dv.append(_dot_tn(a[hd].astype(BF16), do_h) + _dot_nt(_head(kgeb, hd), dstb))
                da = jnp.where(_tri(True), _dot_nt(do_h, v_h), 0.0).astype(BF16)
                dqg.append(_dot(do_h, st.astype(BF16)))
                dkge.append(_dot(v_h, dstb))
                st_dst.append(jnp.sum(st * dst, axis=0, keepdims=True))
                dst_all[hd] = _dot_tn(do_h, _head(qgb, hd)) + dst * _head(egend, hd)
                dq_blk.append([_dot(da[b * SUB:(b + 1) * SUB, :], _head(kb, hd)) for b, (_, _, _, kb) in enumerate(blocks)])
                dk_blk.append([_dot_tn(da[b * SUB:(b + 1) * SUB, :], _head(qb, hd)) for b, (_, _, qb, _) in enumerate(blocks)])
            dv, dqg, dkge, st_dst = (jnp.concatenate(t, axis=1) for t in (dv, dqg, dkge, st_dst))

            dq_a, dg_q = [], []
            dk_a, dg_k = jnp.zeros_like(gcum), jnp.zeros_like(gcum)
            for b, (eq, ek, qb, kb) in enumerate(blocks):
                dq_b = _all_heads(lambda hd: dq_blk[hd][b])
                dk_b = _all_heads(lambda hd: dk_blk[hd][b])
                dq_a.append(dq_b * eq)
                dk_a = dk_a + dk_b * ek
                dg_q.append(qb.astype(F32) * dq_b)
                dg_k = dg_k + kb.astype(F32) * dk_b
            dq_a = jnp.concatenate(dq_a, axis=0)

            dgend = st_dst * egend + jnp.sum(dkge * kge, axis=0, keepdims=True)
            dq = dq_a + dqg * eg
            dk = dk_a + dkge * ekend
            dgc = jnp.concatenate(dg_q, axis=0) - dg_k + dqg * qg - dkge * kge + jnp.where(last, dgend, 0.0)
            dg = _running_sum(dgc, True)
            df = dg / f - dk
            dhf_ref[r, :] = df * (1.0 - lb) * sf * (1.0 - sf)
            dlb = dlb + jnp.sum(df * (1.0 - sf), axis=0, keepdims=True)
            dhq_ref[r, :] = dq * (sq * (1.0 + hq * (1.0 - sq)))
            dhi_ref[r, :] = dv
        for hd in range(HG_HEADS):
            dstate[hd] = dst_all[hd]
        dlb_ref[...] += dlb

    def col(cb):
        return pl.BlockSpec((rows, HG_WIDTH), lambda i: (steps - 1 - i, cb))

    grad = jax.ShapeDtypeStruct((T, HG_WIDTH), F32)
    return pl.pallas_call(
        body, name="hgrn_bwd", grid=(steps,),
        in_specs=[col(P_HQ // HG_WIDTH), col(P_HF // HG_WIDTH), col(P_HI // HG_WIDTH), _full((2, HG_WIDTH)),
                  col(0), pl.BlockSpec((ns, HG_HEADS, HG_DIM, HG_DIM), lambda i: (steps - 1 - i, 0, 0, 0))],
        out_specs=[col(0), col(0), col(0), _full((1, HG_WIDTH))],
        out_shape=[grad, grad, grad, jax.ShapeDtypeStruct((1, HG_WIDTH), F32)],
        scratch_shapes=[pltpu.VMEM((HG_HEADS, HG_DIM, HG_DIM), F32)],
        compiler_params=_params(1),
    )(proj, proj, proj, lbp, do_hg, states)


def _top(x, tgt, o_mla, o_hg, proj, w_out, hg_norm_g, final_g, tm):
    T = x.shape[0]

    def body(x_ref, tgt_ref, om_ref, oh_ref, gm_ref, gh_ref, wout_ref, hgn_ref, fng_ref,
             dx2_ref, dom_ref, dsum_ref, dgm_ref, doh_ref, dgh_ref, loss_ref, dfng_ref, dhgn_ref, dwout_ref, ycat_ref):
        @pl.when(pl.program_id(0) == 0)
        def _():
            for ref in (loss_ref, dfng_ref, dhgn_ref, dwout_ref):
                ref[...] = jnp.zeros_like(ref)

        gm, om = gm_ref[...], om_ref[...]
        sgm = _sigmoid(gm)
        silu_m = gm * sgm
        gh, oh, gam = gh_ref[...], oh_ref[...], hgn_ref[...]
        sgh = _sigmoid(gh)
        silu_h = gh * sgh
        rr, nn = [], []
        for hd in range(HG_HEADS):
            oh_h = oh[:, hd * HG_DIM:(hd + 1) * HG_DIM]
            r_h = lax.rsqrt(jnp.mean(oh_h * oh_h, axis=-1, keepdims=True) + EPS)
            rr.append(r_h)
            nn.append(oh_h * r_h)
        n = jnp.concatenate(nn, axis=1)
        ng = n * gam
        ycat_ref[:, :MLA_WIDTH] = (om * silu_m).astype(BF16)
        ycat_ref[:, MLA_WIDTH:] = (ng * silu_h).astype(BF16)
        wout = wout_ref[...]
        x2 = x_ref[...] + _dot(ycat_ref[...], wout)
        r = lax.rsqrt(jnp.mean(x2 * x2, axis=-1, keepdims=True) + EPS)
        xh = x2 * r
        fng = fng_ref[...]
        err = xh * fng - tgt_ref[...]
        loss_ref[...] += 0.5 * jnp.sum(jnp.mean(err * err, axis=-1, keepdims=True), axis=0, keepdims=True)
        dout = err * (1.0 / D_MODEL)
        dfng_ref[...] += jnp.sum(dout * xh, axis=0, keepdims=True)
        dxh = dout * fng
        dx2 = r * (dxh - xh * jnp.mean(dxh * xh, axis=-1, keepdims=True))
        dx2_ref[...] = dx2
        dx2b = dx2.astype(BF16)
        dwout_ref[...] += _dot_tn(ycat_ref[...], dx2b)
        dycat = _dot_nt(dx2b, wout)
        dym, dyh = dycat[:, :MLA_WIDTH], dycat[:, MLA_WIDTH:]
        dom = dym * silu_m
        first = lax.broadcasted_iota(jnp.int32, (tm, LANES), 1) < V_DIM
        for pp in range(N_HEADS // 2):
            pair = dom[:, pp * LANES:(pp + 1) * LANES]
            dom_ref[:, 2 * pp * HEAD_LANES:(2 * pp + 1) * HEAD_LANES] = jnp.where(first, pair, 0.0).astype(BF16)
            dom_ref[:, (2 * pp + 1) * HEAD_LANES:(2 * pp + 2) * HEAD_LANES] = jnp.where(first, 0.0, pair).astype(BF16)
        head_of = lax.broadcasted_iota(jnp.int32, (MLA_WIDTH, LANES), 0) // V_DIM
        pick = (head_of == lax.broadcasted_iota(jnp.int32, (MLA_WIDTH, LANES), 1)).astype(F32)
        dsum_ref[...] = jnp.dot(dom * om, pick, precision=lax.Precision.HIGHEST, preferred_element_type=F32)
        dgm_ref[...] = dym * om * (sgm * (1.0 + gm * (1.0 - sgm)))
        dgh_ref[...] = dyh * ng * (sgh * (1.0 + gh * (1.0 - sgh)))
        dng = dyh * silu_h
        dhgn_ref[...] += jnp.sum(dng * n, axis=0, keepdims=True)
        dn = dng * gam
        for hd in range(HG_HEADS):
            sl = slice(hd * HG_DIM, (hd + 1) * HG_DIM)
            dn_h, n_h = dn[:, sl], nn[hd]
            doh_ref[:, sl] = rr[hd] * (dn_h - n_h * jnp.mean(dn_h * n_h, axis=-1, keepdims=True))

    def row(w, cb=0):
        return pl.BlockSpec((tm, w), lambda i: (i, cb))

    outs = [(D_MODEL, F32), (N_HEADS * HEAD_LANES, BF16), (LANES, F32), (MLA_WIDTH, F32), (HG_WIDTH, F32), (HG_WIDTH, F32)]
    small = [(1, SMALL_W), (1, D_MODEL), (1, HG_WIDTH), (D_MODEL, D_MODEL)]
    return pl.pallas_call(
        body, name="top", grid=(T // tm,),
        in_specs=[row(D_MODEL), row(D_MODEL), row(MLA_WIDTH), row(HG_WIDTH),
                  row(MLA_WIDTH, P_GM // MLA_WIDTH), row(HG_WIDTH, P_GH // HG_WIDTH),
                  _full((D_MODEL, D_MODEL)), _full((1, HG_WIDTH)), _full((1, D_MODEL))],
        out_specs=[row(w) for w, _ in outs] + [_full(s) for s in small],
        out_shape=[jax.ShapeDtypeStruct((T, w), dt) for w, dt in outs] + [jax.ShapeDtypeStruct(s, F32) for s in small],
        scratch_shapes=[pltpu.VMEM((tm, D_MODEL), BF16)],
        compiler_params=_params(1),
    )(x, tgt, o_mla, o_hg, proj, proj, w_out, hg_norm_g, final_g)


def _bot(x, dx2, proj, h, qn, kvn, dq, dk, dv, dgm, dhq, dhf, dhi, dgh, c_t, s1_t, s2_t, w_in_t, w_q_p, w_kv_p, ln_g, q_g, kv_g, tm):
    T = x.shape[0]
    lat_w = D_PERM - P_QL
    steps = T // tm

    def body(x_ref, dx2_ref, lat_ref, h_ref, qn_ref, kvn_ref, dq_ref, dk_ref, dv_ref, dgm_ref, dhq_ref, dhf_ref,
             dhi_ref, dgh_ref, c_ref, s1_ref, s2_ref, win_ref, wq_ref, wkv_ref, lng_ref, qg_ref, kvg_ref,
             dx_ref, dlng_ref, dqg_ref, dkvg_ref, dwq_ref, dwkv_ref, dwin_ref, dqpre_ref, dkv_ref, dproj_ref,
             dwin_acc, sem):
        @pl.when(pl.program_id(0) == 0)
        def _():
            for ref in (dlng_ref, dqg_ref, dkvg_ref, dwq_ref, dwkv_ref, dwin_acc):
                ref[...] = jnp.zeros_like(ref)

        c, s1, s2 = c_ref[...], s1_ref[...], s2_ref[...]
        dkpe = jnp.zeros((tm, LANES), F32)
        for hd in range(N_HEADS):
            sl = slice(hd * HEAD_LANES, (hd + 1) * HEAD_LANES)
            dqpre_ref[:, sl] = _rope_bwd(dq_ref[:, sl], c, s1, s2).astype(BF16)
            dk_h = dk_ref[:, sl]
            dkpe = dkpe + dk_h
            dkv_ref[:, sl] = dk_h.astype(BF16)
        dkv_ref[:, N_HEADS * HEAD_LANES:] = dv_ref[...].astype(BF16)
        lane = lax.broadcasted_iota(jnp.int32, (tm, LANES), 1)
        rope_lanes = jnp.logical_and(lane >= ROPE_LO, lane < ROPE_LO + ROPE)
        dkr = jnp.where(rope_lanes, _rope_bwd(dkpe, c, s1, s2), 0.0)

        def norm_bwd(v, g, dy):
            r = lax.rsqrt(jnp.mean(v * v, axis=-1, keepdims=True) + EPS)
            vh = v * r
            dvh = dy * g
            return jnp.sum(dy * vh, axis=0, keepdims=True), r * (dvh - vh * jnp.mean(dvh * vh, axis=-1, keepdims=True))

        dwq_ref[...] += _dot_tn(qn_ref[...], dqpre_ref[...])
        dwkv_ref[...] += _dot_tn(kvn_ref[...], dkv_ref[...])
        dqn = _dot_nt(dqpre_ref[...], wq_ref[...])
        dg_q, dql = norm_bwd(lat_ref[:, :Q_RANK], qg_ref[...], dqn)
        dqg_ref[...] += dg_q
        dkn = _dot_nt(dkv_ref[...], wkv_ref[...])
        dg_kv, dkvl = norm_bwd(lat_ref[:, Q_RANK:Q_RANK + KV_RANK], kvg_ref[...], dkn)
        dkvg_ref[...] += dg_kv

        dproj_ref[:, P_GM:P_GM + MLA_WIDTH] = dgm_ref[...].astype(BF16)
        dproj_ref[:, P_HQ:P_HQ + HG_WIDTH] = dhq_ref[...].astype(BF16)
        dproj_ref[:, P_HF:P_HF + HG_WIDTH] = dhf_ref[...].astype(BF16)
        dproj_ref[:, P_HI:P_HI + HG_WIDTH] = dhi_ref[...].astype(BF16)
        dproj_ref[:, P_GH:P_GH + HG_WIDTH] = dgh_ref[...].astype(BF16)
        dproj_ref[:, P_QL:P_QL + Q_RANK] = dql.astype(BF16)
        dproj_ref[:, P_KVL:P_KVL + KV_RANK] = dkvl.astype(BF16)
        dproj_ref[:, P_KR:P_KR + LANES] = dkr.astype(BF16)
        dh = sum(_dot(dproj_ref[:, col:col + rows.shape[0]], rows) for col, rows in _in_proj_rows(win_ref))
        dg_ln, dxn = norm_bwd(x_ref[...], lng_ref[...], dh)
        dlng_ref[...] += dg_ln
        dx_ref[...] = dx2_ref[...] + dxn
        dwin_acc[...] += _dot_tn(dproj_ref[...], h_ref[...])

        @pl.when(pl.program_id(0) == steps - 1)
        def _():
            kr = P_KR + ROPE_LO
            moves = [((P_GM, P_QL), R_MAIN), ((P_QL, P_KR), (R_QL[0], R_KVL[1])), ((kr, kr + ROPE), R_KR)]
            copies = [pltpu.make_async_copy(dwin_acc.at[a:b, :], dwin_ref.at[c:d, :], sem.at[n])
                      for n, ((a, b), (c, d)) in enumerate(moves)]
            for cp in copies:
                cp.start()
            for cp in copies:
                cp.wait()

    def row(w, cb=0):
        return pl.BlockSpec((tm, w), lambda i: (i, cb))

    hl = N_HEADS * HEAD_LANES
    outs = [(D_MODEL, F32)]
    small = [(1, D_MODEL), (1, Q_RANK), (1, KV_RANK), (Q_RANK, hl), (KV_RANK, hl + MLA_WIDTH)]
    return pl.pallas_call(
        body, name="bot", grid=(steps,),
        in_specs=[row(D_MODEL), row(D_MODEL), row(lat_w, P_QL // lat_w), row(D_MODEL), row(Q_RANK), row(KV_RANK),
                  row(hl), row(hl), row(MLA_WIDTH),
                  row(MLA_WIDTH), row(HG_WIDTH), row(HG_WIDTH), row(HG_WIDTH), row(HG_WIDTH),
                  row(LANES), row(LANES), row(LANES),
                  _full((D_IN, D_MODEL)), _full((Q_RANK, hl)), _full((KV_RANK, hl + MLA_WIDTH)),
                  _full((1, D_MODEL)), _full((1, Q_RANK)), _full((1, KV_RANK))],
        out_specs=[row(w) for w, _ in outs] + [_full(s) for s in small] + [pl.BlockSpec(memory_space=pl.ANY)],
        out_shape=[jax.ShapeDtypeStruct((T, w), dt) for w, dt in outs] + [jax.ShapeDtypeStruct(s, F32) for s in small]
        + [jax.ShapeDtypeStruct((D_IN, D_MODEL), F32)],
        scratch_shapes=[pltpu.VMEM((tm, hl), BF16), pltpu.VMEM((tm, hl + MLA_WIDTH), BF16),
                        pltpu.VMEM((tm, D_PERM), BF16), pltpu.VMEM((D_PERM, D_MODEL), F32),
                        pltpu.SemaphoreType.DMA((3,))],
        compiler_params=_params(1),
    )(x, dx2, proj, h, qn, kvn, dq, dk, dv, dgm, dhq, dhf, dhi, dgh, c_t, s1_t, s2_t, w_in_t, w_q_p, w_kv_p, ln_g,
      q_g, kv_g)


RS_ROWS = 256


def _reduce_scatter(slabs, small):
    n = len(slabs)
    units = []
    for a, s in enumerate(slabs):
        rows, cols = s.shape[1:]
        if rows % RS_ROWS == 0 or rows < RS_ROWS:
            units += [(a, (pl.ds(r0, min(rows, RS_ROWS)), slice(None))) for r0 in range(0, rows, RS_ROWS)]
        else:
            units += [(a, (slice(None), pl.ds(c0, RS_ROWS))) for c0 in range(0, cols, RS_ROWS)]
    nu = len(units)

    def body(*refs):
        ins, small_ref = refs[:n], refs[n]
        outs, small_out = refs[n + 1:2 * n + 1], refs[2 * n + 1]
        own, sib_land, ici_out, ici_land = (refs[(2 + g) * n + 2:(3 + g) * n + 2] for g in range(4))
        small_land = refs[6 * n + 2]
        loc_sems, d2d_send, d2d_recv, ici_send, ici_recv, sm_send, sm_recv = refs[6 * n + 3:6 * n + 10]
        x, y, c = lax.axis_index("x"), lax.axis_index("y"), lax.axis_index("c")
        me = 4 * x + 2 * y + c

        def chip(k):
            return (1 - x if k & 2 else x, 1 - y if k & 1 else y)

        def block(k, core):
            px, py = chip(k)
            return 4 * px + 2 * py + core

        def part(u):
            return units[u]

        def local(u, k):
            a, rows = part(u)
            return pltpu.make_async_copy(ins[a].at[(block(k, c),) + rows], own[a].at[(k,) + rows], loc_sems.at[u, k])

        def to_sibling(u, k):
            a, rows = part(u)
            return pltpu.make_async_remote_copy(
                src_ref=ins[a].at[(block(k, 1 - c),) + rows], dst_ref=sib_land[a].at[(k,) + rows],
                send_sem=d2d_send.at[u, k], recv_sem=d2d_recv.at[u, k], device_id=(x, y, 1 - c), device_id_type=MESH)

        def to_chip(u, k):
            a, rows = part(u)
            return pltpu.make_async_remote_copy(
                src_ref=ici_out[a].at[(k - 1,) + rows], dst_ref=ici_land[a].at[(k - 1,) + rows],
                send_sem=ici_send.at[u, k - 1], recv_sem=ici_recv.at[u, k - 1], device_id=(*chip(k), c),
                device_id_type=MESH)

        def small_copy(k, receiving):
            px, py = chip(k >> 1)
            pc = 1 - c if k & 1 else c
            slot = 4 * px + 2 * py + pc if receiving else me
            return pltpu.make_async_remote_copy(
                src_ref=small_ref, dst_ref=small_land.at[slot], send_sem=sm_send.at[k - 1], recv_sem=sm_recv.at[k - 1],
                device_id=(px, py, pc), device_id_type=MESH)

        for u in range(nu):
            for k in range(4):
                local(u, k).start()
        for u in range(nu):
            for k in range(4):
                to_sibling(u, k).start()
        small_land[me] = small_ref[...]
        for k in range(1, N_DEV):
            small_copy(k, False).start()
        for u in range(nu):
            a, rows = part(u)
            for k in range(4):
                local(u, k).wait()
                to_sibling(u, k).wait_recv()
            for k in range(1, 4):
                ici_out[a][(k - 1,) + rows] = (own[a][(k,) + rows] + sib_land[a][(k,) + rows]).astype(BF16)
                to_chip(u, k).start()
        for u in range(nu):
            a, rows = part(u)
            acc = own[a][(0,) + rows] + sib_land[a][(0,) + rows]
            for k in range(1, 4):
                to_chip(u, k).wait_recv()
                acc = acc + ici_land[a][(k - 1,) + rows].astype(F32)
            outs[a][rows] = acc
        for k in range(1, N_DEV):
            small_copy(k, True).wait_recv()
        acc = small_land[0]
        for d in range(1, N_DEV):
            acc = acc + small_land[d]
        small_out[...] = acc
        for u in range(nu):
            for k in range(4):
                to_sibling(u, k).wait_send()
            for k in range(1, 4):
                to_chip(u, k).wait_send()
        for k in range(1, N_DEV):
            small_copy(k, False).wait_send()

    vm = pl.BlockSpec(memory_space=pltpu.VMEM)
    hbm = pl.BlockSpec(memory_space=pl.ANY)
    dma = pltpu.SemaphoreType.DMA
    return pl.pallas_call(
        body, name="reduce_scatter_grads",
        in_specs=[hbm] * n + [vm], out_specs=[vm] * (n + 1),
        out_shape=[jax.ShapeDtypeStruct(s.shape[1:], F32) for s in slabs] + [jax.ShapeDtypeStruct(small.shape, F32)],
        scratch_shapes=[pltpu.VMEM((4,) + s.shape[1:], F32) for s in slabs] * 2
        + [pltpu.VMEM((3,) + s.shape[1:], BF16) for s in slabs] * 2
        + [pltpu.VMEM((N_DEV,) + small.shape, F32)]
        + [dma((nu, 4)), dma((nu, 4)), dma((nu, 4)), dma((nu, 3)), dma((nu, 3)), dma((N_DEV - 1,)), dma((N_DEV - 1,))],
        compiler_params=pltpu.CompilerParams(vmem_limit_bytes=VMEM_LIMIT),
    )(*slabs, small)


def _adamw_math(w, g, m, v):
    m = ADAM_B1 * m + (1.0 - ADAM_B1) * g
    v = ADAM_B2 * v + (1.0 - ADAM_B2) * (g * g)
    m_hat = m / (1.0 - ADAM_B1 ** ADAM_STEP)
    v_hat = v / (1.0 - ADAM_B2 ** ADAM_STEP)
    delta = -ADAM_LR * (m_hat / (jnp.sqrt(v_hat) + ADAM_EPS) + ADAM_WD * w)
    return delta, m, v


SMALL_W = 512


def _adamw(big, small_w, small_g):
    nb, ns = len(big), len(small_w)

    def body(*refs):
        k = 0
        big_in = [refs[4 * i:4 * i + 4] for i in range(nb)]
        k = 4 * nb
        small_in = [refs[k + 3 * i:k + 3 * i + 3] for i in range(ns)]
        k += 3 * ns
        sg_ref = refs[k]
        k += 1
        big_out = [refs[k + 3 * i:k + 3 * i + 3] for i in range(nb)]
        k += 3 * nb
        small_out = [refs[k + 4 * i:k + 4 * i + 4] for i in range(ns)]

        for (w, g, m, v), (od, om, ov) in zip(big_in, big_out):
            od[...], om[...], ov[...] = _adamw_math(w[...], g[...], m[...], v[...])

        sg = sg_ref[...]
        lbp = small_in[2][0][...]
        lb = _lower_bound(lbp)
        t = sg[4:5, :] * lb * (1.0 - lb)
        grads = [jnp.concatenate([sg[0:1, :], sg[1:2, :]], axis=1),
                 jnp.concatenate([sg[2:3, :], sg[3:4, :]], axis=1),
                 jnp.concatenate([t, -t], axis=0),
                 sg[6:7, :], sg[7:8, 0:Q_RANK], sg[7:8, Q_RANK:Q_RANK + KV_RANK]]
        for (w, m, v), g, (og, od, om, ov) in zip(small_in, grads, small_out):
            og[...] = g
            od[...], om[...], ov[...] = _adamw_math(w[...], g, m[...], v[...])

    ins = [a for grp in big for a in grp] + [a for grp in small_w for a in grp] + [small_g]
    out_shape = ([jax.ShapeDtypeStruct(grp[0].shape, F32) for grp in big for _ in range(3)]
                 + [jax.ShapeDtypeStruct(grp[0].shape, F32) for grp in small_w for _ in range(4)])
    vm = pl.BlockSpec(memory_space=pltpu.VMEM)
    res = pl.pallas_call(
        body, name="adamw", in_specs=[vm] * len(ins), out_specs=[vm] * len(out_shape), out_shape=out_shape,
        compiler_params=pltpu.CompilerParams(vmem_limit_bytes=VMEM_LIMIT),
    )(*ins)
    big_res = [res[3 * i:3 * i + 3] for i in range(nb)]
    small_res = [res[3 * nb + 4 * i:3 * nb + 4 * i + 4] for i in range(ns)]
    return big_res, small_res


def _perm_weights(g_in_t, g_q, g_kv, g_out):
    w_in_t = g_in_t.reshape(D_IN, D_MODEL)
    wq = g_q.transpose(1, 0, 2)
    w_q_p = jnp.pad(wq, ((0, 0), (0, 0), (0, HEAD_LANES - NOPE - ROPE))).reshape(Q_RANK, N_HEADS * HEAD_LANES)
    wkv = g_kv.transpose(1, 0, 2)
    wk = jnp.pad(wkv[:, :, :NOPE], ((0, 0), (0, 0), (0, HEAD_LANES - NOPE))).reshape(KV_RANK, N_HEADS * HEAD_LANES)
    wv = wkv[:, :, NOPE:].reshape(KV_RANK, MLA_WIDTH)
    return w_in_t, w_q_p, jnp.concatenate([wk, wv], axis=1), g_out.reshape(D_MODEL, D_MODEL)


def _grad_slabs(dw_in_t, dw_q_p, dw_kv_p, dw_out):
    s_in = dw_in_t.reshape(N_DEV, D_IN // N_DEV, D_MODEL)
    s_q = dw_q_p.reshape(Q_RANK, N_HEADS, HEAD_LANES)[:, :, :NOPE + ROPE].transpose(1, 0, 2)
    hl = N_HEADS * HEAD_LANES
    dk = dw_kv_p[:, :hl].reshape(KV_RANK, N_HEADS, HEAD_LANES)[:, :, :NOPE]
    dv = dw_kv_p[:, hl:].reshape(KV_RANK, N_HEADS, V_DIM)
    s_kv = jnp.concatenate([dk, dv], axis=2).transpose(1, 0, 2)
    return s_in, s_q, s_kv, dw_out.reshape(N_DEV, D_MODEL // N_DEV, D_MODEL)


def _block_sizes(T):
    return min(256, T), min(256, T), min(512, T)


def kernel(x, positions, ln_g, w_in, q_a_norm_g, w_q_b, kv_a_norm_g, w_kv_b, hg_lower_bounds, hg_norm_g, w_out, final_norm_g, loss_target, m_ln_g, m_w_in, m_q_a_norm_g, m_w_q_b, m_kv_a_norm_g, m_w_kv_b, m_hg_lower_bounds, m_hg_norm_g, m_w_out, m_final_norm_g, v_ln_g, v_w_in, v_q_a_norm_g, v_w_q_b, v_kv_a_norm_g, v_w_kv_b, v_hg_lower_bounds, v_hg_norm_g, v_w_out, v_final_norm_g):
    T = x.shape[1]
    tm, tq, bt = _block_sizes(T)
    nq = T // tq
    xs, tgt = x[0], loss_target[0]
    pos_f = positions.astype(F32)
    fng = final_norm_g.reshape(1, D_MODEL)

    w_in_shard_t = w_in[0].T
    gathered, (c_t, s1_t, s2_t) = _all_gather_weights([w_in_shard_t, w_q_b[0], w_kv_b[0], w_out[0]], pos_f)
    w_in_t, w_q_p, w_kv_p, w_out_b = _perm_weights(*gathered)

    proj, h, qn, kvn, q, k, v = _fwd_in(xs, ln_g, w_in_t, q_a_norm_g, w_q_p, kv_a_norm_g, w_kv_p, c_t, s1_t, s2_t, bt)
    hl = N_HEADS * HEAD_LANES
    v_t = v.reshape(nq, tq, MLA_WIDTH).transpose(0, 2, 1)
    k_t = k.reshape(nq, tq, hl).transpose(0, 2, 1)
    q_t = q.reshape(nq, tq, hl).transpose(0, 2, 1)
    o_mla, lse = _attn_fwd_flat(k, q_t, v_t, tq)
    o_hg, states = _hgrn_fwd(proj, hg_lower_bounds)
    dx2, d_om, dsum, d_gm, d_oh, d_gh, loss_p, d_fng, d_hgn, dw_out = _top(
        xs, tgt, o_mla, o_hg, proj, w_out_b, hg_norm_g, fng, tm)
    dsum = dsum[:, :N_HEADS].T.reshape(N_HEADS, nq, 1, tq)
    do_t = d_om.reshape(nq, tq, hl).transpose(0, 2, 1)
    dq_t, dk, dv = _attn_bwd_flat(k, v, q_t, k_t, do_t, lse, dsum, tq)
    dq = dq_t.transpose(0, 2, 1).reshape(T, N_HEADS * HEAD_LANES)
    d_hq, d_hf, d_hi, d_lb = _hgrn_bwd(proj, hg_lower_bounds, d_oh, states)
    dx, d_lng, d_qg, d_kvg, dw_q_p, dw_kv_p, dw_in_t = _bot(
        xs, dx2, proj, h, qn, kvn, dq, dk, dv, d_gm, d_hq, d_hf, d_hi, d_gh, c_t, s1_t, s2_t, w_in_t, w_q_p, w_kv_p,
        ln_g, q_a_norm_g, kv_a_norm_g, tm)

    small = jnp.concatenate([
        d_lng.reshape(2, SMALL_W), d_fng.reshape(2, SMALL_W), d_lb, loss_p, d_hgn,
        jnp.concatenate([d_qg, d_kvg, jnp.zeros((1, SMALL_W - Q_RANK - KV_RANK), F32)], axis=1)], axis=0)
    g_in, g_q, g_kv, g_out, small_sum = _reduce_scatter(list(_grad_slabs(dw_in_t, dw_q_p, dw_kv_p, dw_out)), small)

    big = [(w_in_shard_t, g_in, m_w_in[0].T, v_w_in[0].T), (w_q_b[0], g_q, m_w_q_b[0], v_w_q_b[0]),
           (w_kv_b[0], g_kv, m_w_kv_b[0], v_w_kv_b[0]), (w_out[0], g_out, m_w_out[0], v_w_out[0])]
    small_w = [(ln_g, m_ln_g, v_ln_g),
               (fng, m_final_norm_g.reshape(1, D_MODEL), v_final_norm_g.reshape(1, D_MODEL)),
               (hg_lower_bounds, m_hg_lower_bounds, v_hg_lower_bounds), (hg_norm_g, m_hg_norm_g, v_hg_norm_g),
               (q_a_norm_g, m_q_a_norm_g, v_q_a_norm_g), (kv_a_norm_g, m_kv_a_norm_g, v_kv_a_norm_g)]
    big_res, small_res = _adamw(big, small_w, small_sum)

    loss = small_sum[5, 0]
    (r_in, r_q, r_kv, r_out) = big_res
    (s_ln, s_fn, s_lb, s_hgn, s_qg, s_kvg) = small_res
    flat = lambda t: t.reshape(D_MODEL)
    lead = lambda t: t[None]
    grads = [s_ln[0], lead(g_in.T), s_qg[0], lead(g_q), s_kvg[0], lead(g_kv), s_lb[0], s_hgn[0], lead(g_out), flat(s_fn[0])]

    def pick(i):
        return [s_ln[i + 1], lead(r_in[i].T), s_qg[i + 1], lead(r_q[i]), s_kvg[i + 1], lead(r_kv[i]), s_lb[i + 1],
                s_hgn[i + 1], lead(r_out[i]), flat(s_fn[i + 1])]

    return (loss, dx[None], *grads, *pick(0), *pick(1), *pick(2))
```

```python
import math

import numpy as np
import jax
import jax.numpy as jnp
from jax import lax
from jax.experimental import pallas as pl
from jax.experimental.pallas import tpu as pltpu

F32 = jnp.float32
BF16 = jnp.bfloat16

D_MODEL = 1024
N_HEADS = 8
NOPE = 64
ROPE = 32
HALF_ROPE = ROPE // 2
V_DIM = 64
Q_RANK = 256
KV_RANK = 128
MLA_WIDTH = N_HEADS * V_DIM
HG_HEADS = 4
HG_DIM = 128
HG_WIDTH = HG_HEADS * HG_DIM
CHUNK = 64
SUB = 16
D_IN = 2976
D_PERM = 3072
ROPE_THETA = 10000.0
EPS = 1e-6
N_DEV = 8
LANES = 128
HEAD_LANES = 128

P_GM, P_HQ, P_HF, P_HI, P_GH, P_QL, P_KVL, P_KR = 0, 512, 1024, 1536, 2048, 2560, 2816, 2944
R_QL, R_KVL, R_KR, R_MAIN = (0, 256), (256, 384), (384, 416), (416, 2976)
ROPE_LO = NOPE
SCALE = 1.0 / math.sqrt(NOPE + ROPE)

ADAM_LR = 0.001
ADAM_B1 = 0.9
ADAM_B2 = 0.999
ADAM_EPS = 1e-08
ADAM_WD = 0.01
ADAM_STEP = 10

VMEM_LIMIT = 56 * 1024 * 1024
MESH = pl.DeviceIdType.MESH

NT = (((1,), (1,)), ((), ()))
TN = (((0,), (0,)), ((), ()))


def _params(n_grid=0, **kw):
    sem = ("arbitrary",) * n_grid if n_grid else None
    return pltpu.CompilerParams(dimension_semantics=sem, vmem_limit_bytes=VMEM_LIMIT, **kw)


def _dot(a, b):
    return jnp.dot(a, b, preferred_element_type=F32)


def _dot_nt(a, b):
    return lax.dot_general(a, b, NT, preferred_element_type=F32)


def _dot_tn(a, b):
    return lax.dot_general(a, b, TN, preferred_element_type=F32)


def _sigmoid(x):
    return 1.0 / (1.0 + jnp.exp(-x))


def _rope_fwd(x, c, s1, s2):
    return x * c + pltpu.roll(x, LANES - HALF_ROPE, 1) * s1 + pltpu.roll(x, HALF_ROPE, 1) * s2


def _rope_bwd(dy, c, s1, s2):
    return dy * c - pltpu.roll(dy, LANES - HALF_ROPE, 1) * s1 - pltpu.roll(dy, HALF_ROPE, 1) * s2


def _in_proj_rows(wt_ref):
    kr = wt_ref[R_KR[0]:R_KR[1], :]
    pad = lambda n: jnp.zeros((n, D_MODEL), kr.dtype)
    return ((P_GM, wt_ref[R_MAIN[0]:R_MAIN[1], :]), (P_QL, wt_ref[R_QL[0]:R_QL[1], :]),
            (P_KVL, wt_ref[R_KVL[0]:R_KVL[1], :]),
            (P_KR, jnp.concatenate([pad(ROPE_LO), kr, pad(LANES - ROPE_LO - ROPE)], axis=0)))


def _full(shape):
    n = len(shape)
    return pl.BlockSpec(shape, lambda *_: (0,) * n)


ROPE_BLOCK = 512


def _rope_constants():
    inv = (np.float32(ROPE_THETA) ** (-np.arange(HALF_ROPE, dtype=np.float32) / np.float32(HALF_ROPE))).astype(np.float32)
    place = np.zeros((3, HALF_ROPE, LANES), np.float32)
    for i in range(HALF_ROPE):
        place[0, i, ROPE_LO + i] = place[0, i, ROPE_LO + HALF_ROPE + i] = 1.0
        place[1, i, ROPE_LO + i] = -1.0
        place[2, i, ROPE_LO + HALF_ROPE + i] = 1.0
    base = np.ones((1, LANES), np.float32)
    base[0, ROPE_LO:ROPE_LO + ROPE] = 0.0
    return jnp.asarray(inv.reshape(HALF_ROPE, 1)), jnp.asarray(place), jnp.asarray(base)


def _rope_block(pos, inv, place_ref, base):
    ang = inv * pos
    cos, sin = jnp.cos(ang), jnp.sin(ang)

    def put(v, k):
        return lax.dot_general(v, place_ref[k], TN, precision=lax.Precision.HIGHEST, preferred_element_type=F32)

    return put(cos, 0) + base, put(sin, 1), put(sin, 2)


def _all_gather_weights(shards, pos_f):
    n = len(shards)
    T = pos_f.shape[1]
    rb = min(ROPE_BLOCK, T)

    def body(*refs):
        ins, (pos_ref, inv_ref, place_ref, base_ref) = refs[:n], refs[n:n + 4]
        outs, tables = refs[n + 4:2 * n + 4], refs[2 * n + 4:2 * n + 7]
        send_sems, recv_sems = refs[2 * n + 7], refs[2 * n + 8]
        x, y, c = lax.axis_index("x"), lax.axis_index("y"), lax.axis_index("c")
        me, sibling = (x, y, c), (x, y, 1 - c)
        chips = [(1 - x, y), (x, 1 - y), (1 - x, 1 - y)]

        def idx(d):
            return 4 * d[0] + 2 * d[1] + d[2]

        def copy(a, k, block, to):
            rows = outs[a].at[idx(block)]
            return pltpu.make_async_remote_copy(src_ref=rows, dst_ref=rows, send_sem=send_sems.at[a, k],
                                                recv_sem=recv_sems.at[a, k], device_id=to, device_id_type=MESH)

        for a in range(n):
            outs[a][idx(me)] = ins[a][...].astype(BF16)
        first = []
        for a in range(n):
            first.append(copy(a, 0, me, sibling))
            first += [copy(a, 1 + j, me, (*chip, c)) for j, chip in enumerate(chips)]
        for cp in first:
            cp.start()
        for r0 in range(0, T, rb):
            for ref, tab in zip(tables, _rope_block(pos_ref[:, r0:r0 + rb], inv_ref[...], place_ref, base_ref[...])):
                ref[r0:r0 + rb, :] = tab
        passed = []
        for j, chip in enumerate(chips):
            for a in range(n):
                copy(a, 1 + j, (*chip, c), me).wait_recv()
                cp = copy(a, 4 + j, (*chip, c), sibling)
                cp.start()
                passed.append(cp)
        for a in range(n):
            copy(a, 0, sibling, me).wait_recv()
            for j, chip in enumerate(chips):
                copy(a, 4 + j, (*chip, 1 - c), me).wait_recv()
        for cp in first + passed:
            cp.wait_send()

    vm = pl.BlockSpec(memory_space=pltpu.VMEM)
    res = pl.pallas_call(
        body, name="all_gather_weights",
        in_specs=[vm] * (n + 4), out_specs=[vm] * (n + 3),
        out_shape=[jax.ShapeDtypeStruct((N_DEV,) + s.shape, BF16) for s in shards]
        + [jax.ShapeDtypeStruct((T, LANES), F32)] * 3,
        scratch_shapes=[pltpu.SemaphoreType.DMA((n, 7)), pltpu.SemaphoreType.DMA((n, 7))],
        compiler_params=pltpu.CompilerParams(vmem_limit_bytes=VMEM_LIMIT),
    )(*shards, pos_f, *_rope_constants())
    return res[:n], res[n:]


def _fwd_in(x, ln_g, w_in_t, q_g, w_q_p, kv_g, w_kv_p, c_t, s1_t, s2_t, tm):
    T = x.shape[0]

    def body(x_ref, lng_ref, win_ref, qg_ref, wq_ref, kvg_ref, wkv_ref, c_ref, s1_ref, s2_ref,
             proj_ref, h_ref, qn_ref, kvn_ref, q_ref, k_ref, v_ref):
        xv = x_ref[...]
        r = lax.rsqrt(jnp.mean(xv * xv, axis=-1, keepdims=True) + EPS)
        h = (xv * r * lng_ref[...]).astype(BF16)
        h_ref[...] = h
        for col, rows in _in_proj_rows(win_ref):
            proj_ref[:, col:col + rows.shape[0]] = _dot_nt(h, rows)
        c, s1, s2 = c_ref[...], s1_ref[...], s2_ref[...]

        ql = proj_ref[:, P_QL:P_QL + Q_RANK]
        rq = lax.rsqrt(jnp.mean(ql * ql, axis=-1, keepdims=True) + EPS)
        qn = (ql * rq * qg_ref[...]).astype(BF16)
        qn_ref[...] = qn
        q = _dot(qn, wq_ref[...])
        for hd in range(N_HEADS):
            sl = slice(hd * HEAD_LANES, (hd + 1) * HEAD_LANES)
            q_ref[:, sl] = _rope_fwd(q[:, sl], c, s1, s2).astype(BF16)

        kvl = proj_ref[:, P_KVL:P_KVL + KV_RANK]
        rk = lax.rsqrt(jnp.mean(kvl * kvl, axis=-1, keepdims=True) + EPS)
        kvn = (kvl * rk * kvg_ref[...]).astype(BF16)
        kvn_ref[...] = kvn
        kv = _dot(kvn, wkv_ref[...])
        kpe = _rope_fwd(proj_ref[:, P_KR:P_KR + LANES], c, s1, s2)
        for hd in range(N_HEADS):
            sl = slice(hd * HEAD_LANES, (hd + 1) * HEAD_LANES)
            k_ref[:, sl] = (kv[:, sl] + kpe).astype(BF16)
        v_ref[...] = kv[:, N_HEADS * HEAD_LANES:].astype(BF16)

    def row(w):
        return pl.BlockSpec((tm, w), lambda i: (i, 0))

    outs = [(D_PERM, F32), (D_MODEL, BF16), (Q_RANK, BF16), (KV_RANK, BF16),
            (N_HEADS * HEAD_LANES, BF16), (N_HEADS * HEAD_LANES, BF16), (MLA_WIDTH, BF16)]
    return pl.pallas_call(
        body, name="fwd_in", grid=(T // tm,),
        in_specs=[row(D_MODEL), _full((1, D_MODEL)), _full((D_IN, D_MODEL)), _full((1, Q_RANK)),
                  _full((Q_RANK, N_HEADS * HEAD_LANES)), _full((1, KV_RANK)),
                  _full((KV_RANK, N_HEADS * HEAD_LANES + MLA_WIDTH)), row(LANES), row(LANES), row(LANES)],
        out_specs=[row(w) for w, _ in outs],
        out_shape=[jax.ShapeDtypeStruct((T, w), dt) for w, dt in outs],
        compiler_params=_params(1),
    )(x, ln_g, w_in_t, q_g, w_q_p, kv_g, w_kv_p, c_t, s1_t, s2_t)


LOG2E = 1.4426950408889634
SCALE2 = SCALE * LOG2E


def _causal(tq):
    r = lax.broadcasted_iota(jnp.int32, (tq, tq), 0)
    c = lax.broadcasted_iota(jnp.int32, (tq, tq), 1)
    return r <= c


MASKED = -1e30


def _causal_bias(bias_ref, tq):
    bias_ref[0] = jnp.zeros((tq, tq), F32)
    bias_ref[1] = jnp.where(_causal(tq), 0.0, MASKED)


def _tile_tables(nq, by_query):
    if by_query:
        pairs = [(j, i) for i in range(nq) for j in range(i + 1)]
    else:
        pairs = [(j, i) for j in range(nq) for i in range(nq - 1, j - 1, -1)]
    pairs.append(pairs[-1])
    jj, ii = np.array(pairs, np.int32).T
    return jnp.asarray(jj), jnp.asarray(ii), len(pairs) - 1


ATTN_TRIP = 8


def _walk_tiles(n, products, tile, flush, buf_a, buf_b):
    bufs = (buf_a, buf_b)
    products(0, buf_a)

    def trip(r, carry):
        for u in range(ATTN_TRIP):
            products(ATTN_TRIP * r + u + 1, bufs[(u + 1) % 2])
            tile(ATTN_TRIP * r + u, bufs[u % 2])
        for u in range(ATTN_TRIP):
            flush(ATTN_TRIP * r + u)
        return carry

    lax.fori_loop(0, n // ATTN_TRIP, trip, 0)
    rest = n - n % ATTN_TRIP
    for u in range(n % ATTN_TRIP):
        if rest + u + 1 < n:
            products(rest + u + 1, bufs[(u + 1) % 2])
        tile(rest + u, bufs[u % 2])
    for u in range(n % ATTN_TRIP):
        flush(rest + u)


V_ROWS = V_DIM + 16


def _attn_fwd_flat(k, q_t, v_t, tq):
    T = k.shape[0]
    nq = T // tq
    jj, ii, n = _tile_tables(nq, True)
    heads = [slice(hh * HEAD_LANES, (hh + 1) * HEAD_LANES) for hh in range(2)]

    def body(jj_ref, ii_ref, k_ref, qt_ref, vt_ref, o_ref, lse_ref, sa_ref, sb_ref, m_ref, acc_ref, bias_ref):
        def reset(st):
            m_ref[st] = jnp.full(m_ref.shape[1:], MASKED, F32)
            acc_ref[st] = jnp.zeros(acc_ref.shape[1:], F32)

        _causal_bias(bias_ref, tq)
        for st in range(ATTN_TRIP):
            reset(st)

        def products(t, buf):
            j, i = jj_ref[t], ii_ref[t]
            kj = k_ref[pl.ds(pl.multiple_of(j * tq, tq), tq), :]
            for hh, sl in enumerate(heads):
                buf[hh] = _dot(kj[:, sl], qt_ref[i, sl, :])

        def tile(t, buf):
            j, i = jj_ref[t], ii_ref[t]
            vt = vt_ref[j]
            bias = bias_ref.at[(j == i).astype(jnp.int32)]
            st = i % ATTN_TRIP
            for hh in range(2):
                s = buf[hh] * SCALE2 + bias[...]
                m = m_ref[st, hh]
                m_new = jnp.maximum(m, jnp.max(s, axis=0, keepdims=True))
                alpha = jnp.exp2(m - m_new)
                p = jnp.exp2(s - m_new)
                m_ref[st, hh] = m_new
                acc_ref[st, hh] = alpha * acc_ref[st, hh] + _dot(vt[hh * V_ROWS:(hh + 1) * V_ROWS, :], p.astype(BF16))

        def flush(t):
            j, i = jj_ref[t], ii_ref[t]

            @pl.when(j == i)
            def _():
                st = i % ATTN_TRIP
                den = [acc_ref[st, hh, V_DIM:V_DIM + 1, :] for hh in range(2)]
                out = jnp.concatenate([acc_ref[st, hh, :V_DIM, :] / den[hh] for hh in range(2)], axis=0)
                o_ref[pl.ds(pl.multiple_of(i * tq, tq), tq), :] = out.T
                for hh in range(2):
                    lse_ref[hh, i] = m_ref[st, hh] + jnp.log2(den[hh])
                reset(st)

        _walk_tiles(n, products, tile, flush, sa_ref, sb_ref)

    smem = pl.BlockSpec(memory_space=pltpu.SMEM)
    return pl.pallas_call(
        body, name="attn_fwd", grid=(N_HEADS // 2,),
        in_specs=[smem, smem,
                  pl.BlockSpec((T, 2 * HEAD_LANES), lambda p: (0, p)),
                  pl.BlockSpec((nq, 2 * HEAD_LANES, tq), lambda p: (0, p, 0)),
                  pl.BlockSpec((nq, 2 * V_ROWS, tq), lambda p: (0, p, 0))],
        out_specs=[pl.BlockSpec((T, LANES), lambda p: (0, p)),
                   pl.BlockSpec((2, nq, 1, tq), lambda p: (p, 0, 0, 0))],
        out_shape=[jax.ShapeDtypeStruct((T, MLA_WIDTH), F32), jax.ShapeDtypeStruct((N_HEADS, nq, 1, tq), F32)],
        scratch_shapes=[pltpu.VMEM((2, tq, tq), F32), pltpu.VMEM((2, tq, tq), F32),
                        pltpu.VMEM((ATTN_TRIP, 2, 1, tq), F32),
                        pltpu.VMEM((ATTN_TRIP, 2, V_ROWS, tq), F32), pltpu.VMEM((2, tq, tq), F32)],
        compiler_params=_params(1),
    )(jj, ii, k, q_t, v_t)


def _attn_bwd_flat(k, v, q_t, k_t, do_t, lse, dsum, tq):
    T = k.shape[0]
    nq = T // tq
    jj, ii, n = _tile_tables(nq, False)
    heads = [slice(hh * HEAD_LANES, (hh + 1) * HEAD_LANES) for hh in range(2)]

    def body(jj_ref, ii_ref, k_ref, v_ref, qt_ref, kt_ref, dot_ref, lse_ref, dsum_ref, dqt_ref, dk_ref, dv_ref,
             ba_ref, bb_ref, dkt_ref, dvt_ref, bias_ref):
        _causal_bias(bias_ref, tq)
        dqt_ref[...] = jnp.zeros_like(dqt_ref)
        dkt_ref[...] = jnp.zeros_like(dkt_ref)
        dvt_ref[...] = jnp.zeros_like(dvt_ref)

        def products(t, buf):
            j, i = jj_ref[t], ii_ref[t]
            rows = pl.ds(pl.multiple_of(j * tq, tq), tq)
            for hh, sl in enumerate(heads):
                buf[hh] = _dot(k_ref[rows, sl], qt_ref[i, sl, :])
                buf[2 + hh] = _dot(v_ref[rows, :], dot_ref[i, sl, :])

        def tile(t, buf):
            j, i = jj_ref[t], ii_ref[t]
            bias = bias_ref.at[(j == i).astype(jnp.int32)]
            st = j % ATTN_TRIP
            for hh, sl in enumerate(heads):
                p = jnp.exp2(buf[hh] * SCALE2 + bias[...] - lse_ref[hh, i])
                ds = (p * (buf[2 + hh] - dsum_ref[hh, i]) * SCALE).astype(BF16)
                own = slice(hh * V_DIM, (hh + 1) * V_DIM)
                do_h = dot_ref[i, hh * HEAD_LANES + own.start:hh * HEAD_LANES + own.stop, :]
                dvt_ref[st, own, :] += _dot_nt(do_h, p.astype(BF16))
                used = slice(sl.start, sl.start + NOPE + ROPE)
                dkt_ref[st, used, :] += _dot_nt(qt_ref[i, used, :], ds)
                dqt_ref[i, used, :] += _dot(kt_ref[j, used, :], ds)

        def flush(t):
            j, i = jj_ref[t], ii_ref[t]

            @pl.when(j == i)
            def _():
                st = j % ATTN_TRIP
                rows = pl.ds(pl.multiple_of(j * tq, tq), tq)
                dk_ref[rows, :] = dkt_ref[st].T
                dv_ref[rows, :] = dvt_ref[st].T
                dkt_ref[st] = jnp.zeros(dkt_ref.shape[1:], F32)
                dvt_ref[st] = jnp.zeros(dvt_ref.shape[1:], F32)

        _walk_tiles(n, products, tile, flush, ba_ref, bb_ref)

    smem = pl.BlockSpec(memory_space=pltpu.SMEM)
    stat = pl.BlockSpec((2, nq, 1, tq), lambda p: (p, 0, 0, 0))
    blocks_t = pl.BlockSpec((nq, 2 * HEAD_LANES, tq), lambda p: (0, p, 0))
    return pl.pallas_call(
        body, name="attn_bwd", grid=(N_HEADS // 2,),
        in_specs=[smem, smem,
                  pl.BlockSpec((T, 2 * HEAD_LANES), lambda p: (0, p)),
                  pl.BlockSpec((T, LANES), lambda p: (0, p)),
                  blocks_t, blocks_t, blocks_t, stat, stat],
        out_specs=[blocks_t,
                   pl.BlockSpec((T, 2 * HEAD_LANES), lambda p: (0, p)),
                   pl.BlockSpec((T, LANES), lambda p: (0, p))],
        out_shape=[jax.ShapeDtypeStruct((nq, N_HEADS * HEAD_LANES, tq), F32),
                   jax.ShapeDtypeStruct((T, N_HEADS * HEAD_LANES), F32),
                   jax.ShapeDtypeStruct((T, MLA_WIDTH), F32)],
        scratch_shapes=[pltpu.VMEM((4, tq, tq), F32), pltpu.VMEM((4, tq, tq), F32),
                        pltpu.VMEM((ATTN_TRIP, 2 * HEAD_LANES, tq), F32), pltpu.VMEM((ATTN_TRIP, LANES, tq), F32),
                        pltpu.VMEM((2, tq, tq), F32)],
        compiler_params=_params(1),
    )(jj, ii, k, v, q_t, k_t, do_t, lse, dsum)


def _lower_bound(lbp):
    a, b = lbp[0:1, :], lbp[1:2, :]
    mx = jnp.maximum(a, b)
    ea, eb = jnp.exp(a - mx), jnp.exp(b - mx)
    return ea / (ea + eb)


def _tri(lower):
    r = lax.broadcasted_iota(jnp.int32, (CHUNK, CHUNK), 0)
    c = lax.broadcasted_iota(jnp.int32, (CHUNK, CHUNK), 1)
    return (c <= r) if lower else (c >= r)


def _running_sum(x, from_end):
    row = lax.broadcasted_iota(jnp.int32, x.shape, 0)
    step = 1
    while step < CHUNK:
        if from_end:
            x = x + jnp.where(row < CHUNK - step, pltpu.roll(x, CHUNK - step, 0), 0.0)
        else:
            x = x + jnp.where(row >= step, pltpu.roll(x, step, 0), 0.0)
        step *= 2
    return x


def _hg_gates(hq, hf, lb):
    sq = _sigmoid(hq)
    sf = _sigmoid(hf)
    f = lb + (1.0 - lb) * sf
    g = jnp.log(f)
    gcum = _running_sum(g, False)
    return sq, sf, f, hq * sq, 1.0 - f, gcum


def _head(x, hd):
    return x[:, hd * HG_DIM:(hd + 1) * HG_DIM]


def _all_heads(fn):
    return jnp.concatenate([fn(hd) for hd in range(HG_HEADS)], axis=1)


def _hg_blocks(q, kk, gcum):
    rowi = lax.broadcasted_iota(jnp.int32, gcum.shape, 0)
    out = []
    for blk in range(CHUNK // SUB):
        lo, hi = blk * SUB, (blk + 1) * SUB
        gb = gcum[lo - 1:lo, :] if blk else jnp.zeros_like(gcum[0:1, :])
        eq = jnp.exp(gcum[lo:hi, :] - gb)
        ek = jnp.exp(jnp.where(rowi < hi, gb - gcum, 0.0))
        out.append((eq, ek, (q[lo:hi, :] * eq).astype(BF16), (kk * ek).astype(BF16)))
    return out


def _hg_scores(blocks):
    out = []
    for hd in range(HG_HEADS):
        a = jnp.concatenate([_dot_nt(_head(qb, hd), _head(kb, hd)) for _, _, qb, kb in blocks], axis=0)
        out.append(jnp.where(_tri(True), a, 0.0))
    return out


HG_STEP_CHUNKS = 8


def _hgrn_fwd(proj, lbp):
    T = proj.shape[0]
    nc = T // CHUNK
    ns = min(HG_STEP_CHUNKS, nc)
    rows = ns * CHUNK

    def body(hq_ref, hf_ref, hi_ref, lbp_ref, o_ref, st_ref, state):
        @pl.when(pl.program_id(0) == 0)
        def _():
            state[...] = jnp.zeros_like(state)

        lb = _lower_bound(lbp_ref[...])
        work = []
        for c in range(ns):
            r = slice(c * CHUNK, (c + 1) * CHUNK)
            _, _, _, q, kk, gcum = _hg_gates(hq_ref[r, :], hf_ref[r, :], lb)
            vb = hi_ref[r, :].astype(BF16)
            a = _hg_scores(_hg_blocks(q, kk, gcum))
            gend = gcum[CHUNK - 1:CHUNK, :]
            qgb = (q * jnp.exp(gcum)).astype(BF16)
            kgeb = (kk * jnp.exp(gend - gcum)).astype(BF16)
            intra = [_dot(a[hd].astype(BF16), _head(vb, hd)) for hd in range(HG_HEADS)]
            update = [_dot_tn(_head(vb, hd), _head(kgeb, hd)) for hd in range(HG_HEADS)]
            work.append((qgb, jnp.exp(gend), intra, update))
        for hd in range(HG_HEADS):
            st = state[hd]
            for c, (qgb, egend, intra, update) in enumerate(work):
                st_ref[c, hd] = st
                o_ref[c * CHUNK:(c + 1) * CHUNK, hd * HG_DIM:(hd + 1) * HG_DIM] = (
                    intra[hd] + _dot_nt(_head(qgb, hd), st.astype(BF16)))
                st = st * _head(egend, hd) + update[hd]
            state[hd] = st

    def col(cb):
        return pl.BlockSpec((rows, HG_WIDTH), lambda i: (i, cb))

    return pl.pallas_call(
        body, name="hgrn_fwd", grid=(nc // ns,),
        in_specs=[col(P_HQ // HG_WIDTH), col(P_HF // HG_WIDTH), col(P_HI // HG_WIDTH), _full((2, HG_WIDTH))],
        out_specs=[pl.BlockSpec((rows, HG_WIDTH), lambda i: (i, 0)),
                   pl.BlockSpec((ns, HG_HEADS, HG_DIM, HG_DIM), lambda i: (i, 0, 0, 0))],
        out_shape=[jax.ShapeDtypeStruct((T, HG_WIDTH), F32),
                   jax.ShapeDtypeStruct((nc, HG_HEADS, HG_DIM, HG_DIM), F32)],
        scratch_shapes=[pltpu.VMEM((HG_HEADS, HG_DIM, HG_DIM), F32)],
        compiler_params=_params(1),
    )(proj, proj, proj, lbp)


def _hgrn_bwd(proj, lbp, do_hg, states):
    T = proj.shape[0]
    nc = T // CHUNK
    ns = min(HG_STEP_CHUNKS, nc)
    rows = ns * CHUNK
    steps = nc // ns

    def body(hq_ref, hf_ref, hi_ref, lbp_ref, do_ref, st_ref, dhq_ref, dhf_ref, dhi_ref, dlb_ref, dstate):
        @pl.when(pl.program_id(0) == 0)
        def _():
            dstate[...] = jnp.zeros_like(dstate)
            dlb_ref[...] = jnp.zeros_like(dlb_ref)

        lb = _lower_bound(lbp_ref[...])

        dst_all = [dstate[hd] for hd in range(HG_HEADS)]
        dlb = jnp.zeros_like(lb)
        last = lax.broadcasted_iota(jnp.int32, (CHUNK, HG_WIDTH), 0) == CHUNK - 1
        for c in reversed(range(ns)):
            r = slice(c * CHUNK, (c + 1) * CHUNK)
            hq = hq_ref[r, :]
            sq, sf, f, q, kk, gcum = _hg_gates(hq, hf_ref[r, :], lb)
            vb = hi_ref[r, :].astype(BF16)
            dob = do_ref[r, :].astype(BF16)
            blocks = _hg_blocks(q, kk, gcum)
            a = _hg_scores(blocks)
            gend = gcum[CHUNK - 1:CHUNK, :]
            eg, egend, ekend = jnp.exp(gcum), jnp.exp(gend), jnp.exp(gend - gcum)
            qg, kge = q * eg, kk * ekend
            qgb, kgeb = qg.astype(BF16), kge.astype(BF16)

            dv, dqg, dkge, st_dst, dq_blk, dk_blk = [], [], [], [], [], []
            for hd in range(HG_HEADS):
                st = st_ref[c, hd]
                dst = dst_all[hd]
                dstb = dst.astype(BF16)
                do_h, v_h = _head(dob, hd), _head(vb, hd)
                dv.append(_dot_tn(a[hd].astype(BF16), do_h) + _dot_nt(_head(kgeb, hd), dstb))
                da = jnp.where(_tri(True), _dot_nt(do_h, v_h), 0.0).astype(BF16)
                dqg.append(_dot(do_h, st.astype(BF16)))
                dkge.append(_dot(v_h, dstb))
                st_dst.append(jnp.sum(st * dst, axis=0, keepdims=True))
                dst_all[hd] = _dot_tn(do_h, _head(qgb, hd)) + dst * _head(egend, hd)
                dq_blk.append([_dot(da[b * SUB:(b + 1) * SUB, :], _head(kb, hd)) for b, (_, _, _, kb) in enumerate(blocks)])
                dk_blk.append([_dot_tn(da[b * SUB:(b + 1) * SUB, :], _head(qb, hd)) for b, (_, _, qb, _) in enumerate(blocks)])
            dv, dqg, dkge, st_dst = (jnp.concatenate(t, axis=1) for t in (dv, dqg, dkge, st_dst))

            dq_a, dg_q = [], []
            dk_a, dg_k = jnp.zeros_like(gcum), jnp.zeros_like(gcum)
            for b, (eq, ek, qb, kb) in enumerate(blocks):
                dq_b = _all_heads(lambda hd: dq_blk[hd][b])
                dk_b = _all_heads(lambda hd: dk_blk[hd][b])
                dq_a.append(dq_b * eq)
                dk_a = dk_a + dk_b * ek
                dg_q.append(qb.astype(F32) * dq_b)
                dg_k = dg_k + kb.astype(F32) * dk_b
            dq_a = jnp.concatenate(dq_a, axis=0)

            dgend = st_dst * egend + jnp.sum(dkge * kge, axis=0, keepdims=True)
            dq = dq_a + dqg * eg
            dk = dk_a + dkge * ekend
            dgc = jnp.concatenate(dg_q, axis=0) - dg_k + dqg * qg - dkge * kge + jnp.where(last, dgend, 0.0)
            dg = _running_sum(dgc, True)
            df = dg / f - dk
            dhf_ref[r, :] = df * (1.0 - lb) * sf * (1.0 - sf)
            dlb = dlb + jnp.sum(df * (1.0 - sf), axis=0, keepdims=True)
            dhq_ref[r, :] = dq * (sq * (1.0 + hq * (1.0 - sq)))
            dhi_ref[r, :] = dv
        for hd in range(HG_HEADS):
            dstate[hd] = dst_all[hd]
        dlb_ref[...] += dlb

    def col(cb):
        return pl.BlockSpec((rows, HG_WIDTH), lambda i: (steps - 1 - i, cb))

    grad = jax.ShapeDtypeStruct((T, HG_WIDTH), F32)
    return pl.pallas_call(
        body, name="hgrn_bwd", grid=(steps,),
        in_specs=[col(P_HQ // HG_WIDTH), col(P_HF // HG_WIDTH), col(P_HI // HG_WIDTH), _full((2, HG_WIDTH)),
                  col(0), pl.BlockSpec((ns, HG_HEADS, HG_DIM, HG_DIM), lambda i: (steps - 1 - i, 0, 0, 0))],
        out_specs=[col(0), col(0), col(0), _full((1, HG_WIDTH))],
        out_shape=[grad, grad, grad, jax.ShapeDtypeStruct((1, HG_WIDTH), F32)],
        scratch_shapes=[pltpu.VMEM((HG_HEADS, HG_DIM, HG_DIM), F32)],
        compiler_params=_params(1),
    )(proj, proj, proj, lbp, do_hg, states)


def _top(x, tgt, o_mla, o_hg, proj, w_out, hg_norm_g, final_g, tm):
    T = x.shape[0]

    def body(x_ref, tgt_ref, om_ref, oh_ref, gm_ref, gh_ref, wout_ref, hgn_ref, fng_ref,
             dx2_ref, dom_ref, dsum_ref, dgm_ref, doh_ref, dgh_ref, loss_ref, dfng_ref, dhgn_ref, dwout_ref, ycat_ref):
        @pl.when(pl.program_id(0) == 0)
        def _():
            for ref in (loss_ref, dfng_ref, dhgn_ref, dwout_ref):
                ref[...] = jnp.zeros_like(ref)

        gm, om = gm_ref[...], om_ref[...]
        sgm = _sigmoid(gm)
        silu_m = gm * sgm
        gh, oh, gam = gh_ref[...], oh_ref[...], hgn_ref[...]
        sgh = _sigmoid(gh)
        silu_h = gh * sgh
        rr, nn = [], []
        for hd in range(HG_HEADS):
            oh_h = oh[:, hd * HG_DIM:(hd + 1) * HG_DIM]
            r_h = lax.rsqrt(jnp.mean(oh_h * oh_h, axis=-1, keepdims=True) + EPS)
            rr.append(r_h)
            nn.append(oh_h * r_h)
        n = jnp.concatenate(nn, axis=1)
        ng = n * gam
        ycat_ref[:, :MLA_WIDTH] = (om * silu_m).astype(BF16)
        ycat_ref[:, MLA_WIDTH:] = (ng * silu_h).astype(BF16)
        wout = wout_ref[...]
        x2 = x_ref[...] + _dot(ycat_ref[...], wout)
        r = lax.rsqrt(jnp.mean(x2 * x2, axis=-1, keepdims=True) + EPS)
        xh = x2 * r
        fng = fng_ref[...]
        err = xh * fng - tgt_ref[...]
        loss_ref[...] += 0.5 * jnp.sum(jnp.mean(err * err, axis=-1, keepdims=True), axis=0, keepdims=True)
        dout = err * (1.0 / D_MODEL)
        dfng_ref[...] += jnp.sum(dout * xh, axis=0, keepdims=True)
        dxh = dout * fng
        dx2 = r * (dxh - xh * jnp.mean(dxh * xh, axis=-1, keepdims=True))
        dx2_ref[...] = dx2
        dx2b = dx2.astype(BF16)
        dwout_ref[...] += _dot_tn(ycat_ref[...], dx2b)
        dycat = _dot_nt(dx2b, wout)
        dym, dyh = dycat[:, :MLA_WIDTH], dycat[:, MLA_WIDTH:]
        dom = dym * silu_m
        first = lax.broadcasted_iota(jnp.int32, (tm, LANES), 1) < V_DIM
        for pp in range(N_HEADS // 2):
            pair = dom[:, pp * LANES:(pp + 1) * LANES]
            dom_ref[:, 2 * pp * HEAD_LANES:(2 * pp + 1) * HEAD_LANES] = jnp.where(first, pair, 0.0).astype(BF16)
            dom_ref[:, (2 * pp + 1) * HEAD_LANES:(2 * pp + 2) * HEAD_LANES] = jnp.where(first, 0.0, pair).astype(BF16)
        head_of = lax.broadcasted_iota(jnp.int32, (MLA_WIDTH, LANES), 0) // V_DIM
        pick = (head_of == lax.broadcasted_iota(jnp.int32, (MLA_WIDTH, LANES), 1)).astype(F32)
        dsum_ref[...] = jnp.dot(dom * om, pick, precision=lax.Precision.HIGHEST, preferred_element_type=F32)
        dgm_ref[...] = dym * om * (sgm * (1.0 + gm * (1.0 - sgm)))
        dgh_ref[...] = dyh * ng * (sgh * (1.0 + gh * (1.0 - sgh)))
        dng = dyh * silu_h
        dhgn_ref[...] += jnp.sum(dng * n, axis=0, keepdims=True)
        dn = dng * gam
        for hd in range(HG_HEADS):
            sl = slice(hd * HG_DIM, (hd + 1) * HG_DIM)
            dn_h, n_h = dn[:, sl], nn[hd]
            doh_ref[:, sl] = rr[hd] * (dn_h - n_h * jnp.mean(dn_h * n_h, axis=-1, keepdims=True))

    def row(w, cb=0):
        return pl.BlockSpec((tm, w), lambda i: (i, cb))

    outs = [(D_MODEL, F32), (N_HEADS * HEAD_LANES, BF16), (LANES, F32), (MLA_WIDTH, F32), (HG_WIDTH, F32), (HG_WIDTH, F32)]
    small = [(1, SMALL_W), (1, D_MODEL), (1, HG_WIDTH), (D_MODEL, D_MODEL)]
    return pl.pallas_call(
        body, name="top", grid=(T // tm,),
        in_specs=[row(D_MODEL), row(D_MODEL), row(MLA_WIDTH), row(HG_WIDTH),
                  row(MLA_WIDTH, P_GM // MLA_WIDTH), row(HG_WIDTH, P_GH // HG_WIDTH),
                  _full((D_MODEL, D_MODEL)), _full((1, HG_WIDTH)), _full((1, D_MODEL))],
        out_specs=[row(w) for w, _ in outs] + [_full(s) for s in small],
        out_shape=[jax.ShapeDtypeStruct((T, w), dt) for w, dt in outs] + [jax.ShapeDtypeStruct(s, F32) for s in small],
        scratch_shapes=[pltpu.VMEM((tm, D_MODEL), BF16)],
        compiler_params=_params(1),
    )(x, tgt, o_mla, o_hg, proj, proj, w_out, hg_norm_g, final_g)


def _bot(x, dx2, proj, h, qn, kvn, dq, dk, dv, dgm, dhq, dhf, dhi, dgh, c_t, s1_t, s2_t, w_in_t, w_q_p, w_kv_p, ln_g, q_g, kv_g, tm):
    T = x.shape[0]
    lat_w = D_PERM - P_QL
    steps = T // tm

    def body(x_ref, dx2_ref, lat_ref, h_ref, qn_ref, kvn_ref, dq_ref, dk_ref, dv_ref, dgm_ref, dhq_ref, dhf_ref,
             dhi_ref, dgh_ref, c_ref, s1_ref, s2_ref, win_ref, wq_ref, wkv_ref, lng_ref, qg_ref, kvg_ref,
             dx_ref, dlng_ref, dqg_ref, dkvg_ref, dwq_ref, dwkv_ref, dwin_ref, dqpre_ref, dkv_ref, dproj_ref,
             dwin_acc, sem):
        @pl.when(pl.program_id(0) == 0)
        def _():
            for ref in (dlng_ref, dqg_ref, dkvg_ref, dwq_ref, dwkv_ref, dwin_acc):
                ref[...] = jnp.zeros_like(ref)

        c, s1, s2 = c_ref[...], s1_ref[...], s2_ref[...]
        dkpe = jnp.zeros((tm, LANES), F32)
        for hd in range(N_HEADS):
            sl = slice(hd * HEAD_LANES, (hd + 1) * HEAD_LANES)
            dqpre_ref[:, sl] = _rope_bwd(dq_ref[:, sl], c, s1, s2).astype(BF16)
            dk_h = dk_ref[:, sl]
            dkpe = dkpe + dk_h
            dkv_ref[:, sl] = dk_h.astype(BF16)
        dkv_ref[:, N_HEADS * HEAD_LANES:] = dv_ref[...].astype(BF16)
        lane = lax.broadcasted_iota(jnp.int32, (tm, LANES), 1)
        rope_lanes = jnp.logical_and(lane >= ROPE_LO, lane < ROPE_LO + ROPE)
        dkr = jnp.where(rope_lanes, _rope_bwd(dkpe, c, s1, s2), 0.0)

        def norm_bwd(v, g, dy):
            r = lax.rsqrt(jnp.mean(v * v, axis=-1, keepdims=True) + EPS)
            vh = v * r
            dvh = dy * g
            return jnp.sum(dy * vh, axis=0, keepdims=True), r * (dvh - vh * jnp.mean(dvh * vh, axis=-1, keepdims=True))

        dwq_ref[...] += _dot_tn(qn_ref[...], dqpre_ref[...])
        dwkv_ref[...] += _dot_tn(kvn_ref[...], dkv_ref[...])
        dqn = _dot_nt(dqpre_ref[...], wq_ref[...])
        dg_q, dql = norm_bwd(lat_ref[:, :Q_RANK], qg_ref[...], dqn)
        dqg_ref[...] += dg_q
        dkn = _dot_nt(dkv_ref[...], wkv_ref[...])
        dg_kv, dkvl = norm_bwd(lat_ref[:, Q_RANK:Q_RANK + KV_RANK], kvg_ref[...], dkn)
        dkvg_ref[...] += dg_kv

        dproj_ref[:, P_GM:P_GM + MLA_WIDTH] = dgm_ref[...].astype(BF16)
        dproj_ref[:, P_HQ:P_HQ + HG_WIDTH] = dhq_ref[...].astype(BF16)
        dproj_ref[:, P_HF:P_HF + HG_WIDTH] = dhf_ref[...].astype(BF16)
        dproj_ref[:, P_HI:P_HI + HG_WIDTH] = dhi_ref[...].astype(BF16)
        dproj_ref[:, P_GH:P_GH + HG_WIDTH] = dgh_ref[...].astype(BF16)
        dproj_ref[:, P_QL:P_QL + Q_RANK] = dql.astype(BF16)
        dproj_ref[:, P_KVL:P_KVL + KV_RANK] = dkvl.astype(BF16)
        dproj_ref[:, P_KR:P_KR + LANES] = dkr.astype(BF16)
        dh = sum(_dot(dproj_ref[:, col:col + rows.shape[0]], rows) for col, rows in _in_proj_rows(win_ref))
        dg_ln, dxn = norm_bwd(x_ref[...], lng_ref[...], dh)
        dlng_ref[...] += dg_ln
        dx_ref[...] = dx2_ref[...] + dxn
        dwin_acc[...] += _dot_tn(dproj_ref[...], h_ref[...])

        @pl.when(pl.program_id(0) == steps - 1)
        def _():
            kr = P_KR + ROPE_LO
            moves = [((P_GM, P_QL), R_MAIN), ((P_QL, P_KR), (R_QL[0], R_KVL[1])), ((kr, kr + ROPE), R_KR)]
            copies = [pltpu.make_async_copy(dwin_acc.at[a:b, :], dwin_ref.at[c:d, :], sem.at[n])
                      for n, ((a, b), (c, d)) in enumerate(moves)]
            for cp in copies:
                cp.start()
            for cp in copies:
                cp.wait()

    def row(w, cb=0):
        return pl.BlockSpec((tm, w), lambda i: (i, cb))

    hl = N_HEADS * HEAD_LANES
    outs = [(D_MODEL, F32)]
    small = [(1, D_MODEL), (1, Q_RANK), (1, KV_RANK), (Q_RANK, hl), (KV_RANK, hl + MLA_WIDTH)]
    return pl.pallas_call(
        body, name="bot", grid=(steps,),
        in_specs=[row(D_MODEL), row(D_MODEL), row(lat_w, P_QL // lat_w), row(D_MODEL), row(Q_RANK), row(KV_RANK),
                  row(hl), row(hl), row(MLA_WIDTH),
                  row(MLA_WIDTH), row(HG_WIDTH), row(HG_WIDTH), row(HG_WIDTH), row(HG_WIDTH),
                  row(LANES), row(LANES), row(LANES),
                  _full((D_IN, D_MODEL)), _full((Q_RANK, hl)), _full((KV_RANK, hl + MLA_WIDTH)),
                  _full((1, D_MODEL)), _full((1, Q_RANK)), _full((1, KV_RANK))],
        out_specs=[row(w) for w, _ in outs] + [_full(s) for s in small] + [pl.BlockSpec(memory_space=pl.ANY)],
        out_shape=[jax.ShapeDtypeStruct((T, w), dt) for w, dt in outs] + [jax.ShapeDtypeStruct(s, F32) for s in small]
        + [jax.ShapeDtypeStruct((D_IN, D_MODEL), F32)],
        scratch_shapes=[pltpu.VMEM((tm, hl), BF16), pltpu.VMEM((tm, hl + MLA_WIDTH), BF16),
                        pltpu.VMEM((tm, D_PERM), BF16), pltpu.VMEM((D_PERM, D_MODEL), F32),
                        pltpu.SemaphoreType.DMA((3,))],
        compiler_params=_params(1),
    )(x, dx2, proj, h, qn, kvn, dq, dk, dv, dgm, dhq, dhf, dhi, dgh, c_t, s1_t, s2_t, w_in_t, w_q_p, w_kv_p, ln_g,
      q_g, kv_g)


RS_ROWS = 256


def _reduce_scatter(slabs, small):
    n = len(slabs)
    units = []
    for a, s in enumerate(slabs):
        rows, cols = s.shape[1:]
        if rows % RS_ROWS == 0 or rows < RS_ROWS:
            units += [(a, (pl.ds(r0, min(rows, RS_ROWS)), slice(None))) for r0 in range(0, rows, RS_ROWS)]
        else:
            units += [(a, (slice(None), pl.ds(c0, RS_ROWS))) for c0 in range(0, cols, RS_ROWS)]
    nu = len(units)

    def body(*refs):
        ins, small_ref = refs[:n], refs[n]
        outs, small_out = refs[n + 1:2 * n + 1], refs[2 * n + 1]
        own, sib_land, ici_out, ici_land = (refs[(2 + g) * n + 2:(3 + g) * n + 2] for g in range(4))
        small_land = refs[6 * n + 2]
        loc_sems, d2d_send, d2d_recv, ici_send, ici_recv, sm_send, sm_recv = refs[6 * n + 3:6 * n + 10]
        x, y, c = lax.axis_index("x"), lax.axis_index("y"), lax.axis_index("c")
        me = 4 * x + 2 * y + c

        def chip(k):
            return (1 - x if k & 2 else x, 1 - y if k & 1 else y)

        def block(k, core):
            px, py = chip(k)
            return 4 * px + 2 * py + core

        def part(u):
            return units[u]

        def local(u, k):
            a, rows = part(u)
            return pltpu.make_async_copy(ins[a].at[(block(k, c),) + rows], own[a].at[(k,) + rows], loc_sems.at[u, k])

        def to_sibling(u, k):
            a, rows = part(u)
            return pltpu.make_async_remote_copy(
                src_ref=ins[a].at[(block(k, 1 - c),) + rows], dst_ref=sib_land[a].at[(k,) + rows],
                send_sem=d2d_send.at[u, k], recv_sem=d2d_recv.at[u, k], device_id=(x, y, 1 - c), device_id_type=MESH)

        def to_chip(u, k):
            a, rows = part(u)
            return pltpu.make_async_remote_copy(
                src_ref=ici_out[a].at[(k - 1,) + rows], dst_ref=ici_land[a].at[(k - 1,) + rows],
                send_sem=ici_send.at[u, k - 1], recv_sem=ici_recv.at[u, k - 1], device_id=(*chip(k), c),
                device_id_type=MESH)

        def small_copy(k, receiving):
            px, py = chip(k >> 1)
            pc = 1 - c if k & 1 else c
            slot = 4 * px + 2 * py + pc if receiving else me
            return pltpu.make_async_remote_copy(
                src_ref=small_ref, dst_ref=small_land.at[slot], send_sem=sm_send.at[k - 1], recv_sem=sm_recv.at[k - 1],
                device_id=(px, py, pc), device_id_type=MESH)

        for u in range(nu):
            for k in range(4):
                local(u, k).start()
        for u in range(nu):
            for k in range(4):
                to_sibling(u, k).start()
        small_land[me] = small_ref[...]
        for k in range(1, N_DEV):
            small_copy(k, False).start()
        for u in range(nu):
            a, rows = part(u)
            for k in range(4):
                local(u, k).wait()
                to_sibling(u, k).wait_recv()
            for k in range(1, 4):
                ici_out[a][(k - 1,) + rows] = (own[a][(k,) + rows] + sib_land[a][(k,) + rows]).astype(BF16)
                to_chip(u, k).start()
        for u in range(nu):
            a, rows = part(u)
            acc = own[a][(0,) + rows] + sib_land[a][(0,) + rows]
            for k in range(1, 4):
                to_chip(u, k).wait_recv()
                acc = acc + ici_land[a][(k - 1,) + rows].astype(F32)
            outs[a][rows] = acc
        for k in range(1, N_DEV):
            small_copy(k, True).wait_recv()
        acc = small_land[0]
        for d in range(1, N_DEV):
            acc = acc + small_land[d]
        small_out[...] = acc
        for u in range(nu):
            for k in range(4):
                to_sibling(u, k).wait_send()
            for k in range(1, 4):
                to_chip(u, k).wait_send()
        for k in range(1, N_DEV):
            small_copy(k, False).wait_send()

    vm = pl.BlockSpec(memory_space=pltpu.VMEM)
    hbm = pl.BlockSpec(memory_space=pl.ANY)
    dma = pltpu.SemaphoreType.DMA
    return pl.pallas_call(
        body, name="reduce_scatter_grads",
        in_specs=[hbm] * n + [vm], out_specs=[vm] * (n + 1),
        out_shape=[jax.ShapeDtypeStruct(s.shape[1:], F32) for s in slabs] + [jax.ShapeDtypeStruct(small.shape, F32)],
        scratch_shapes=[pltpu.VMEM((4,) + s.shape[1:], F32) for s in slabs] * 2
        + [pltpu.VMEM((3,) + s.shape[1:], BF16) for s in slabs] * 2
        + [pltpu.VMEM((N_DEV,) + small.shape, F32)]
        + [dma((nu, 4)), dma((nu, 4)), dma((nu, 4)), dma((nu, 3)), dma((nu, 3)), dma((N_DEV - 1,)), dma((N_DEV - 1,))],
        compiler_params=pltpu.CompilerParams(vmem_limit_bytes=VMEM_LIMIT),
    )(*slabs, small)


def _adamw_math(w, g, m, v):
    m = ADAM_B1 * m + (1.0 - ADAM_B1) * g
    v = ADAM_B2 * v + (1.0 - ADAM_B2) * (g * g)
    m_hat = m / (1.0 - ADAM_B1 ** ADAM_STEP)
    v_hat = v / (1.0 - ADAM_B2 ** ADAM_STEP)
    delta = -ADAM_LR * (m_hat / (jnp.sqrt(v_hat) + ADAM_EPS) + ADAM_WD * w)
    return delta, m, v


SMALL_W = 512


def _adamw(big, small_w, small_g):
    nb, ns = len(big), len(small_w)

    def body(*refs):
        k = 0
        big_in = [refs[4 * i:4 * i + 4] for i in range(nb)]
        k = 4 * nb
        small_in = [refs[k + 3 * i:k + 3 * i + 3] for i in range(ns)]
        k += 3 * ns
        sg_ref = refs[k]
        k += 1
        big_out = [refs[k + 3 * i:k + 3 * i + 3] for i in range(nb)]
        k += 3 * nb
        small_out = [refs[k + 4 * i:k + 4 * i + 4] for i in range(ns)]

        for (w, g, m, v), (od, om, ov) in zip(big_in, big_out):
            od[...], om[...], ov[...] = _adamw_math(w[...], g[...], m[...], v[...])

        sg = sg_ref[...]
        lbp = small_in[2][0][...]
        lb = _lower_bound(lbp)
        t = sg[4:5, :] * lb * (1.0 - lb)
        grads = [jnp.concatenate([sg[0:1, :], sg[1:2, :]], axis=1),
                 jnp.concatenate([sg[2:3, :], sg[3:4, :]], axis=1),
                 jnp.concatenate([t, -t], axis=0),
                 sg[6:7, :], sg[7:8, 0:Q_RANK], sg[7:8, Q_RANK:Q_RANK + KV_RANK]]
        for (w, m, v), g, (og, od, om, ov) in zip(small_in, grads, small_out):
            og[...] = g
            od[...], om[...], ov[...] = _adamw_math(w[...], g, m[...], v[...])

    ins = [a for grp in big for a in grp] + [a for grp in small_w for a in grp] + [small_g]
    out_shape = ([jax.ShapeDtypeStruct(grp[0].shape, F32) for grp in big for _ in range(3)]
                 + [jax.ShapeDtypeStruct(grp[0].shape, F32) for grp in small_w for _ in range(4)])
    vm = pl.BlockSpec(memory_space=pltpu.VMEM)
    res = pl.pallas_call(
        body, name="adamw", in_specs=[vm] * len(ins), out_specs=[vm] * len(out_shape), out_shape=out_shape,
        compiler_params=pltpu.CompilerParams(vmem_limit_bytes=VMEM_LIMIT),
    )(*ins)
    big_res = [res[3 * i:3 * i + 3] for i in range(nb)]
    small_res = [res[3 * nb + 4 * i:3 * nb + 4 * i + 4] for i in range(ns)]
    return big_res, small_res


def _perm_weights(g_in_t, g_q, g_kv, g_out):
    w_in_t = g_in_t.reshape(D_IN, D_MODEL)
    wq = g_q.transpose(1, 0, 2)
    w_q_p = jnp.pad(wq, ((0, 0), (0, 0), (0, HEAD_LANES - NOPE - ROPE))).reshape(Q_RANK, N_HEADS * HEAD_LANES)
    wkv = g_kv.transpose(1, 0, 2)
    wk = jnp.pad(wkv[:, :, :NOPE], ((0, 0), (0, 0), (0, HEAD_LANES - NOPE))).reshape(KV_RANK, N_HEADS * HEAD_LANES)
    wv = wkv[:, :, NOPE:].reshape(KV_RANK, MLA_WIDTH)
    return w_in_t, w_q_p, jnp.concatenate([wk, wv], axis=1), g_out.reshape(D_MODEL, D_MODEL)


def _grad_slabs(dw_in_t, dw_q_p, dw_kv_p, dw_out):
    s_in = dw_in_t.reshape(N_DEV, D_IN // N_DEV, D_MODEL)
    s_q = dw_q_p.reshape(Q_RANK, N_HEADS, HEAD_LANES)[:, :, :NOPE + ROPE].transpose(1, 0, 2)
    hl = N_HEADS * HEAD_LANES
    dk = dw_kv_p[:, :hl].reshape(KV_RANK, N_HEADS, HEAD_LANES)[:, :, :NOPE]
    dv = dw_kv_p[:, hl:].reshape(KV_RANK, N_HEADS, V_DIM)
    s_kv = jnp.concatenate([dk, dv], axis=2).transpose(1, 0, 2)
    return s_in, s_q, s_kv, dw_out.reshape(N_DEV, D_MODEL // N_DEV, D_MODEL)


def _block_sizes(T):
    return min(256, T), min(256, T), min(512, T)


def kernel(x, positions, ln_g, w_in, q_a_norm_g, w_q_b, kv_a_norm_g, w_kv_b, hg_lower_bounds, hg_norm_g, w_out, final_norm_g, loss_target, m_ln_g, m_w_in, m_q_a_norm_g, m_w_q_b, m_kv_a_norm_g, m_w_kv_b, m_hg_lower_bounds, m_hg_norm_g, m_w_out, m_final_norm_g, v_ln_g, v_w_in, v_q_a_norm_g, v_w_q_b, v_kv_a_norm_g, v_w_kv_b, v_hg_lower_bounds, v_hg_norm_g, v_w_out, v_final_norm_g):
    T = x.shape[1]
    tm, tq, bt = _block_sizes(T)
    nq = T // tq
    xs, tgt = x[0], loss_target[0]
    pos_f = positions.astype(F32)
    fng = final_norm_g.reshape(1, D_MODEL)

    w_in_shard_t = w_in[0].T
    gathered, (c_t, s1_t, s2_t) = _all_gather_weights([w_in_shard_t, w_q_b[0], w_kv_b[0], w_out[0]], pos_f)
    w_in_t, w_q_p, w_kv_p, w_out_b = _perm_weights(*gathered)

    proj, h, qn, kvn, q, k, v = _fwd_in(xs, ln_g, w_in_t, q_a_norm_g, w_q_p, kv_a_norm_g, w_kv_p, c_t, s1_t, s2_t, bt)
    hl = N_HEADS * HEAD_LANES
    v_t = v.reshape(nq, tq, N_HEADS, V_DIM).transpose(0, 2, 3, 1)
    ones = jnp.zeros((nq, N_HEADS, V_ROWS - V_DIM, tq), BF16).at[:, :, 0, :].set(1.0)
    v_t = jnp.concatenate([v_t, ones], axis=2).reshape(nq, N_HEADS * V_ROWS, tq)
    k_t = k.reshape(nq, tq, hl).transpose(0, 2, 1)
    q_t = q.reshape(nq, tq, hl).transpose(0, 2, 1)
    o_mla, lse = _attn_fwd_flat(k, q_t, v_t, tq)
    o_hg, states = _hgrn_fwd(proj, hg_lower_bounds)
    dx2, d_om, dsum, d_gm, d_oh, d_gh, loss_p, d_fng, d_hgn, dw_out = _top(
        xs, tgt, o_mla, o_hg, proj, w_out_b, hg_norm_g, fng, tm)
    dsum = dsum[:, :N_HEADS].T.reshape(N_HEADS, nq, 1, tq)
    do_t = d_om.reshape(nq, tq, hl).transpose(0, 2, 1)
    dq_t, dk, dv = _attn_bwd_flat(k, v, q_t, k_t, do_t, lse, dsum, tq)
    dq = dq_t.transpose(0, 2, 1).reshape(T, N_HEADS * HEAD_LANES)
    d_hq, d_hf, d_hi, d_lb = _hgrn_bwd(proj, hg_lower_bounds, d_oh, states)
    dx, d_lng, d_qg, d_kvg, dw_q_p, dw_kv_p, dw_in_t = _bot(
        xs, dx2, proj, h, qn, kvn, dq, dk, dv, d_gm, d_hq, d_hf, d_hi, d_gh, c_t, s1_t, s2_t, w_in_t, w_q_p, w_kv_p,
        ln_g, q_a_norm_g, kv_a_norm_g, tm)

    small = jnp.concatenate([
        d_lng.reshape(2, SMALL_W), d_fng.reshape(2, SMALL_W), d_lb, loss_p, d_hgn,
        jnp.concatenate([d_qg, d_kvg, jnp.zeros((1, SMALL_W - Q_RANK - KV_RANK), F32)], axis=1)], axis=0)
    g_in, g_q, g_kv, g_out, small_sum = _reduce_scatter(list(_grad_slabs(dw_in_t, dw_q_p, dw_kv_p, dw_out)), small)

    big = [(w_in_shard_t, g_in, m_w_in[0].T, v_w_in[0].T), (w_q_b[0], g_q, m_w_q_b[0], v_w_q_b[0]),
           (w_kv_b[0], g_kv, m_w_kv_b[0], v_w_kv_b[0]), (w_out[0], g_out, m_w_out[0], v_w_out[0])]
    small_w = [(ln_g, m_ln_g, v_ln_g),
               (fng, m_final_norm_g.reshape(1, D_MODEL), v_final_norm_g.reshape(1, D_MODEL)),
               (hg_lower_bounds, m_hg_lower_bounds, v_hg_lower_bounds), (hg_norm_g, m_hg_norm_g, v_hg_norm_g),
               (q_a_norm_g, m_q_a_norm_g, v_q_a_norm_g), (kv_a_norm_g, m_kv_a_norm_g, v_kv_a_norm_g)]
    big_res, small_res = _adamw(big, small_w, small_sum)

    loss = small_sum[5, 0]
    (r_in, r_q, r_kv, r_out) = big_res
    (s_ln, s_fn, s_lb, s_hgn, s_qg, s_kvg) = small_res
    flat = lambda t: t.reshape(D_MODEL)
    lead = lambda t: t[None]
    grads = [s_ln[0], lead(g_in.T), s_qg[0], lead(g_q), s_kvg[0], lead(g_kv), s_lb[0], s_hgn[0], lead(g_out), flat(s_fn[0])]

    def pick(i):
        return [s_ln[i + 1], lead(r_in[i].T), s_qg[i + 1], lead(r_q[i]), s_kvg[i + 1], lead(r_kv[i]), s_lb[i + 1],
                s_hgn[i + 1], lead(r_out[i]), flat(s_fn[i + 1])]

    return (loss, dx[None], *grads, *pick(0), *pick(1), *pick(2))
```

```python
import math

import numpy as np
import jax
import jax.numpy as jnp
from jax import lax
from jax.experimental import pallas as pl
from jax.experimental.pallas import tpu as pltpu

F32 = jnp.float32
BF16 = jnp.bfloat16

D_MODEL = 1024
N_HEADS = 8
NOPE = 64
ROPE = 32
HALF_ROPE = ROPE // 2
V_DIM = 64
Q_RANK = 256
KV_RANK = 128
MLA_WIDTH = N_HEADS * V_DIM
HG_HEADS = 4
HG_DIM = 128
HG_WIDTH = HG_HEADS * HG_DIM
CHUNK = 64
SUB = 16
D_IN = 2976
D_PERM = 3072
ROPE_THETA = 10000.0
EPS = 1e-6
N_DEV = 8
LANES = 128
HEAD_LANES = 128

P_GM, P_HQ, P_HF, P_HI, P_GH, P_QL, P_KVL, P_KR = 0, 512, 1024, 1536, 2048, 2560, 2816, 2944
R_QL, R_KVL, R_KR, R_MAIN = (0, 256), (256, 384), (384, 416), (416, 2976)
ROPE_LO = NOPE
SCALE = 1.0 / math.sqrt(NOPE + ROPE)

ADAM_LR = 0.001
ADAM_B1 = 0.9
ADAM_B2 = 0.999
ADAM_EPS = 1e-08
ADAM_WD = 0.01
ADAM_STEP = 10

VMEM_LIMIT = 56 * 1024 * 1024
MESH = pl.DeviceIdType.MESH

NT = (((1,), (1,)), ((), ()))
TN = (((0,), (0,)), ((), ()))


def _params(n_grid=0, **kw):
    sem = ("arbitrary",) * n_grid if n_grid else None
    return pltpu.CompilerParams(dimension_semantics=sem, vmem_limit_bytes=VMEM_LIMIT, **kw)


def _dot(a, b):
    return jnp.dot(a, b, preferred_element_type=F32)


def _dot_nt(a, b):
    return lax.dot_general(a, b, NT, preferred_element_type=F32)


def _dot_tn(a, b):
    return lax.dot_general(a, b, TN, preferred_element_type=F32)


def _sigmoid(x):
    return 1.0 / (1.0 + jnp.exp(-x))


def _rope_fwd(x, c, s1, s2):
    return x * c + pltpu.roll(x, LANES - HALF_ROPE, 1) * s1 + pltpu.roll(x, HALF_ROPE, 1) * s2


def _rope_bwd(dy, c, s1, s2):
    return dy * c - pltpu.roll(dy, LANES - HALF_ROPE, 1) * s1 - pltpu.roll(dy, HALF_ROPE, 1) * s2


def _in_proj_rows(wt_ref):
    kr = wt_ref[R_KR[0]:R_KR[1], :]
    pad = lambda n: jnp.zeros((n, D_MODEL), kr.dtype)
    return ((P_GM, wt_ref[R_MAIN[0]:R_MAIN[1], :]), (P_QL, wt_ref[R_QL[0]:R_QL[1], :]),
            (P_KVL, wt_ref[R_KVL[0]:R_KVL[1], :]),
            (P_KR, jnp.concatenate([pad(ROPE_LO), kr, pad(LANES - ROPE_LO - ROPE)], axis=0)))


def _full(shape):
    n = len(shape)
    return pl.BlockSpec(shape, lambda *_: (0,) * n)


ROPE_BLOCK = 512


def _rope_constants():
    inv = (np.float32(ROPE_THETA) ** (-np.arange(HALF_ROPE, dtype=np.float32) / np.float32(HALF_ROPE))).astype(np.float32)
    place = np.zeros((3, HALF_ROPE, LANES), np.float32)
    for i in range(HALF_ROPE):
        place[0, i, ROPE_LO + i] = place[0, i, ROPE_LO + HALF_ROPE + i] = 1.0
        place[1, i, ROPE_LO + i] = -1.0
        place[2, i, ROPE_LO + HALF_ROPE + i] = 1.0
    base = np.ones((1, LANES), np.float32)
    base[0, ROPE_LO:ROPE_LO + ROPE] = 0.0
    return jnp.asarray(inv.reshape(HALF_ROPE, 1)), jnp.asarray(place), jnp.asarray(base)


def _rope_block(pos, inv, place_ref, base):
    ang = inv * pos
    cos, sin = jnp.cos(ang), jnp.sin(ang)

    def put(v, k):
        return lax.dot_general(v, place_ref[k], TN, precision=lax.Precision.HIGHEST, preferred_element_type=F32)

    return put(cos, 0) + base, put(sin, 1), put(sin, 2)


def _all_gather_weights(shards, pos_f):
    n = len(shards)
    T = pos_f.shape[1]
    rb = min(ROPE_BLOCK, T)

    def body(*refs):
        ins, (pos_ref, inv_ref, place_ref, base_ref) = refs[:n], refs[n:n + 4]
        outs, tables = refs[n + 4:2 * n + 4], refs[2 * n + 4:2 * n + 7]
        send_sems, recv_sems = refs[2 * n + 7], refs[2 * n + 8]
        x, y, c = lax.axis_index("x"), lax.axis_index("y"), lax.axis_index("c")
        me, sibling = (x, y, c), (x, y, 1 - c)
        chips = [(1 - x, y), (x, 1 - y), (1 - x, 1 - y)]

        def idx(d):
            return 4 * d[0] + 2 * d[1] + d[2]

        def copy(a, k, block, to):
            rows = outs[a].at[idx(block)]
            return pltpu.make_async_remote_copy(src_ref=rows, dst_ref=rows, send_sem=send_sems.at[a, k],
                                                recv_sem=recv_sems.at[a, k], device_id=to, device_id_type=MESH)

        for a in range(n):
            outs[a][idx(me)] = ins[a][...].astype(BF16)
        first = []
        for a in range(n):
            first.append(copy(a, 0, me, sibling))
            first += [copy(a, 1 + j, me, (*chip, c)) for j, chip in enumerate(chips)]
        for cp in first:
            cp.start()
        for r0 in range(0, T, rb):
            for ref, tab in zip(tables, _rope_block(pos_ref[:, r0:r0 + rb], inv_ref[...], place_ref, base_ref[...])):
                ref[r0:r0 + rb, :] = tab
        passed = []
        for j, chip in enumerate(chips):
            for a in range(n):
                copy(a, 1 + j, (*chip, c), me).wait_recv()
                cp = copy(a, 4 + j, (*chip, c), sibling)
                cp.start()
                passed.append(cp)
        for a in range(n):
            copy(a, 0, sibling, me).wait_recv()
            for j, chip in enumerate(chips):
                copy(a, 4 + j, (*chip, 1 - c), me).wait_recv()
        for cp in first + passed:
            cp.wait_send()

    vm = pl.BlockSpec(memory_space=pltpu.VMEM)
    res = pl.pallas_call(
        body, name="all_gather_weights",
        in_specs=[vm] * (n + 4), out_specs=[vm] * (n + 3),
        out_shape=[jax.ShapeDtypeStruct((N_DEV,) + s.shape, BF16) for s in shards]
        + [jax.ShapeDtypeStruct((T, LANES), F32)] * 3,
        scratch_shapes=[pltpu.SemaphoreType.DMA((n, 7)), pltpu.SemaphoreType.DMA((n, 7))],
        compiler_params=pltpu.CompilerParams(vmem_limit_bytes=VMEM_LIMIT),
    )(*shards, pos_f, *_rope_constants())
    return res[:n], res[n:]


def _fwd_in(x, ln_g, w_in_t, q_g, w_q_p, kv_g, w_kv_p, c_t, s1_t, s2_t, tm):
    T = x.shape[0]

    def body(x_ref, lng_ref, win_ref, qg_ref, wq_ref, kvg_ref, wkv_ref, c_ref, s1_ref, s2_ref,
             proj_ref, h_ref, qn_ref, kvn_ref, q_ref, k_ref, v_ref):
        xv = x_ref[...]
        r = lax.rsqrt(jnp.mean(xv * xv, axis=-1, keepdims=True) + EPS)
        h = (xv * r * lng_ref[...]).astype(BF16)
        h_ref[...] = h
        for col, rows in _in_proj_rows(win_ref):
            proj_ref[:, col:col + rows.shape[0]] = _dot_nt(h, rows)
        c, s1, s2 = c_ref[...], s1_ref[...], s2_ref[...]

        ql = proj_ref[:, P_QL:P_QL + Q_RANK]
        rq = lax.rsqrt(jnp.mean(ql * ql, axis=-1, keepdims=True) + EPS)
        qn = (ql * rq * qg_ref[...]).astype(BF16)
        qn_ref[...] = qn
        q = _dot(qn, wq_ref[...])
        for hd in range(N_HEADS):
            sl = slice(hd * HEAD_LANES, (hd + 1) * HEAD_LANES)
            q_ref[:, sl] = _rope_fwd(q[:, sl], c, s1, s2).astype(BF16)

        kvl = proj_ref[:, P_KVL:P_KVL + KV_RANK]
        rk = lax.rsqrt(jnp.mean(kvl * kvl, axis=-1, keepdims=True) + EPS)
        kvn = (kvl * rk * kvg_ref[...]).astype(BF16)
        kvn_ref[...] = kvn
        kv = _dot(kvn, wkv_ref[...])
        kpe = _rope_fwd(proj_ref[:, P_KR:P_KR + LANES], c, s1, s2)
        for hd in range(N_HEADS):
            sl = slice(hd * HEAD_LANES, (hd + 1) * HEAD_LANES)
            k_ref[:, sl] = (kv[:, sl] + kpe).astype(BF16)
        v_ref[...] = kv[:, N_HEADS * HEAD_LANES:].astype(BF16)

    def row(w):
        return pl.BlockSpec((tm, w), lambda i: (i, 0))

    outs = [(D_PERM, F32), (D_MODEL, BF16), (Q_RANK, BF16), (KV_RANK, BF16),
            (N_HEADS * HEAD_LANES, BF16), (N_HEADS * HEAD_LANES, BF16), (MLA_WIDTH, BF16)]
    return pl.pallas_call(
        body, name="fwd_in", grid=(T // tm,),
        in_specs=[row(D_MODEL), _full((1, D_MODEL)), _full((D_IN, D_MODEL)), _full((1, Q_RANK)),
                  _full((Q_RANK, N_HEADS * HEAD_LANES)), _full((1, KV_RANK)),
                  _full((KV_RANK, N_HEADS * HEAD_LANES + MLA_WIDTH)), row(LANES), row(LANES), row(LANES)],
        out_specs=[row(w) for w, _ in outs],
        out_shape=[jax.ShapeDtypeStruct((T, w), dt) for w, dt in outs],
        compiler_params=_params(1),
    )(x, ln_g, w_in_t, q_g, w_q_p, kv_g, w_kv_p, c_t, s1_t, s2_t)


LOG2E = 1.4426950408889634
SCALE2 = SCALE * LOG2E


def _causal(tq):
    r = lax.broadcasted_iota(jnp.int32, (tq, tq), 0)
    c = lax.broadcasted_iota(jnp.int32, (tq, tq), 1)
    return r <= c


MASKED = -1e30


def _causal_bias(bias_ref, tq):
    bias_ref[0] = jnp.zeros((tq, tq), F32)
    bias_ref[1] = jnp.where(_causal(tq), 0.0, MASKED)


def _tile_tables(nq, by_query):
    if by_query:
        pairs = [(j, i) for i in range(nq) for j in range(i + 1)]
    else:
        pairs = [(j, i) for j in range(nq) for i in range(nq - 1, j - 1, -1)]
    pairs.append(pairs[-1])
    jj, ii = np.array(pairs, np.int32).T
    return jnp.asarray(jj), jnp.asarray(ii), len(pairs) - 1


ATTN_TRIP = 8


def _walk_tiles(n, products, tile, flush, buf_a, buf_b):
    bufs = (buf_a, buf_b)
    products(0, buf_a)

    def trip(r, carry):
        for u in range(ATTN_TRIP):
            products(ATTN_TRIP * r + u + 1, bufs[(u + 1) % 2])
            tile(ATTN_TRIP * r + u, bufs[u % 2])
        for u in range(ATTN_TRIP):
            flush(ATTN_TRIP * r + u)
        return carry

    lax.fori_loop(0, n // ATTN_TRIP, trip, 0)
    rest = n - n % ATTN_TRIP
    for u in range(n % ATTN_TRIP):
        if rest + u + 1 < n:
            products(rest + u + 1, bufs[(u + 1) % 2])
        tile(rest + u, bufs[u % 2])
    for u in range(n % ATTN_TRIP):
        flush(rest + u)


V_ROWS = V_DIM + 16


def _attn_fwd_flat(k, q_t, v_t, tq):
    T = k.shape[0]
    nq = T // tq
    jj, ii, n = _tile_tables(nq, True)
    heads = [slice(hh * HEAD_LANES, (hh + 1) * HEAD_LANES) for hh in range(2)]

    def body(jj_ref, ii_ref, k_ref, qt_ref, vt_ref, o_ref, lse_ref, sa_ref, sb_ref, m_ref, acc_ref, bias_ref):
        def reset(st):
            m_ref[st] = jnp.full(m_ref.shape[1:], MASKED, F32)
            acc_ref[st] = jnp.zeros(acc_ref.shape[1:], F32)

        _causal_bias(bias_ref, tq)
        for st in range(ATTN_TRIP):
            reset(st)
        extra = (lax.broadcasted_iota(jnp.int32, (V_ROWS - V_DIM, tq), 0) == 0).astype(BF16)

        def products(t, buf):
            j, i = jj_ref[t], ii_ref[t]
            kj = k_ref[pl.ds(pl.multiple_of(j * tq, tq), tq), :]
            for hh, sl in enumerate(heads):
                buf[hh] = _dot(kj[:, sl], qt_ref[i, sl, :])

        def tile(t, buf):
            j, i = jj_ref[t], ii_ref[t]
            vt = vt_ref[j]
            bias = bias_ref.at[(j == i).astype(jnp.int32)]
            st = i % ATTN_TRIP
            for hh in range(2):
                s = buf[hh] * SCALE2 + bias[...]
                m = m_ref[st, hh]
                m_new = jnp.maximum(m, jnp.max(s, axis=0, keepdims=True))
                alpha = jnp.exp2(m - m_new)
                p = jnp.exp2(s - m_new)
                m_ref[st, hh] = m_new
                v_h = jnp.concatenate([vt[hh * V_DIM:(hh + 1) * V_DIM, :], extra], axis=0)
                acc_ref[st, hh] = alpha * acc_ref[st, hh] + _dot(v_h, p.astype(BF16))

        def flush(t):
            j, i = jj_ref[t], ii_ref[t]

            @pl.when(j == i)
            def _():
                st = i % ATTN_TRIP
                den = [acc_ref[st, hh, V_DIM:V_DIM + 1, :] for hh in range(2)]
                out = jnp.concatenate([acc_ref[st, hh, :V_DIM, :] / den[hh] for hh in range(2)], axis=0)
                o_ref[pl.ds(pl.multiple_of(i * tq, tq), tq), :] = out.T
                for hh in range(2):
                    lse_ref[hh, i] = m_ref[st, hh] + jnp.log2(den[hh])
                reset(st)

        _walk_tiles(n, products, tile, flush, sa_ref, sb_ref)

    smem = pl.BlockSpec(memory_space=pltpu.SMEM)
    return pl.pallas_call(
        body, name="attn_fwd", grid=(N_HEADS // 2,),
        in_specs=[smem, smem,
                  pl.BlockSpec((T, 2 * HEAD_LANES), lambda p: (0, p)),
                  pl.BlockSpec((nq, 2 * HEAD_LANES, tq), lambda p: (0, p, 0)),
                  pl.BlockSpec((nq, LANES, tq), lambda p: (0, p, 0))],
        out_specs=[pl.BlockSpec((T, LANES), lambda p: (0, p)),
                   pl.BlockSpec((2, nq, 1, tq), lambda p: (p, 0, 0, 0))],
        out_shape=[jax.ShapeDtypeStruct((T, MLA_WIDTH), F32), jax.ShapeDtypeStruct((N_HEADS, nq, 1, tq), F32)],
        scratch_shapes=[pltpu.VMEM((2, tq, tq), F32), pltpu.VMEM((2, tq, tq), F32),
                        pltpu.VMEM((ATTN_TRIP, 2, 1, tq), F32),
                        pltpu.VMEM((ATTN_TRIP, 2, V_ROWS, tq), F32), pltpu.VMEM((2, tq, tq), F32)],
        compiler_params=_params(1),
    )(jj, ii, k, q_t, v_t)


def _attn_bwd_flat(k, v, q_t, k_t, do_t, lse, dsum, tq):
    T = k.shape[0]
    nq = T // tq
    jj, ii, n = _tile_tables(nq, False)
    heads = [slice(hh * HEAD_LANES, (hh + 1) * HEAD_LANES) for hh in range(2)]

    def body(jj_ref, ii_ref, k_ref, v_ref, qt_ref, kt_ref, dot_ref, lse_ref, dsum_ref, dqt_ref, dk_ref, dv_ref,
             ba_ref, bb_ref, dkt_ref, dvt_ref, bias_ref):
        _causal_bias(bias_ref, tq)
        dqt_ref[...] = jnp.zeros_like(dqt_ref)
        dkt_ref[...] = jnp.zeros_like(dkt_ref)
        dvt_ref[...] = jnp.zeros_like(dvt_ref)

        def products(t, buf):
            j, i = jj_ref[t], ii_ref[t]
            rows = pl.ds(pl.multiple_of(j * tq, tq), tq)
            for hh, sl in enumerate(heads):
                buf[hh] = _dot(k_ref[rows, sl], qt_ref[i, sl, :])
                buf[2 + hh] = _dot(v_ref[rows, :], dot_ref[i, sl, :])

        def tile(t, buf):
            j, i = jj_ref[t], ii_ref[t]
            bias = bias_ref.at[(j == i).astype(jnp.int32)]
            st = j % ATTN_TRIP
            for hh, sl in enumerate(heads):
                p = jnp.exp2(buf[hh] * SCALE2 + bias[...] - lse_ref[hh, i])
                ds = (p * (buf[2 + hh] - dsum_ref[hh, i]) * SCALE).astype(BF16)
                own = slice(hh * V_DIM, (hh + 1) * V_DIM)
                do_h = dot_ref[i, hh * HEAD_LANES + own.start:hh * HEAD_LANES + own.stop, :]
                dvt_ref[st, own, :] += _dot_nt(do_h, p.astype(BF16))
                used = slice(sl.start, sl.start + NOPE + ROPE)
                dkt_ref[st, used, :] += _dot_nt(qt_ref[i, used, :], ds)
                dqt_ref[i, used, :] += _dot(kt_ref[j, used, :], ds)

        def flush(t):
            j, i = jj_ref[t], ii_ref[t]

            @pl.when(j == i)
            def _():
                st = j % ATTN_TRIP
                rows = pl.ds(pl.multiple_of(j * tq, tq), tq)
                dk_ref[rows, :] = dkt_ref[st].T
                dv_ref[rows, :] = dvt_ref[st].T
                dkt_ref[st] = jnp.zeros(dkt_ref.shape[1:], F32)
                dvt_ref[st] = jnp.zeros(dvt_ref.shape[1:], F32)

        _walk_tiles(n, products, tile, flush, ba_ref, bb_ref)

    smem = pl.BlockSpec(memory_space=pltpu.SMEM)
    stat = pl.BlockSpec((2, nq, 1, tq), lambda p: (p, 0, 0, 0))
    blocks_t = pl.BlockSpec((nq, 2 * HEAD_LANES, tq), lambda p: (0, p, 0))
    return pl.pallas_call(
        body, name="attn_bwd", grid=(N_HEADS // 2,),
        in_specs=[smem, smem,
                  pl.BlockSpec((T, 2 * HEAD_LANES), lambda p: (0, p)),
                  pl.BlockSpec((T, LANES), lambda p: (0, p)),
                  blocks_t, blocks_t, blocks_t, stat, stat],
        out_specs=[blocks_t,
                   pl.BlockSpec((T, 2 * HEAD_LANES), lambda p: (0, p)),
                   pl.BlockSpec((T, LANES), lambda p: (0, p))],
        out_shape=[jax.ShapeDtypeStruct((nq, N_HEADS * HEAD_LANES, tq), F32),
                   jax.ShapeDtypeStruct((T, N_HEADS * HEAD_LANES), F32),
                   jax.ShapeDtypeStruct((T, MLA_WIDTH), F32)],
        scratch_shapes=[pltpu.VMEM((4, tq, tq), F32), pltpu.VMEM((4, tq, tq), F32),
                        pltpu.VMEM((ATTN_TRIP, 2 * HEAD_LANES, tq), F32), pltpu.VMEM((ATTN_TRIP, LANES, tq), F32),
                        pltpu.VMEM((2, tq, tq), F32)],
        compiler_params=_params(1),
    )(jj, ii, k, v, q_t, k_t, do_t, lse, dsum)


def _lower_bound(lbp):
    a, b = lbp[0:1, :], lbp[1:2, :]
    mx = jnp.maximum(a, b)
    ea, eb = jnp.exp(a - mx), jnp.exp(b - mx)
    return ea / (ea + eb)


def _tri(lower):
    r = lax.broadcasted_iota(jnp.int32, (CHUNK, CHUNK), 0)
    c = lax.broadcasted_iota(jnp.int32, (CHUNK, CHUNK), 1)
    return (c <= r) if lower else (c >= r)


def _running_sum(x, from_end):
    row = lax.broadcasted_iota(jnp.int32, x.shape, 0)
    step = 1
    while step < CHUNK:
        if from_end:
            x = x + jnp.where(row < CHUNK - step, pltpu.roll(x, CHUNK - step, 0), 0.0)
        else:
            x = x + jnp.where(row >= step, pltpu.roll(x, step, 0), 0.0)
        step *= 2
    return x


def _hg_gates(hq, hf, lb):
    sq = _sigmoid(hq)
    sf = _sigmoid(hf)
    f = lb + (1.0 - lb) * sf
    g = jnp.log(f)
    gcum = _running_sum(g, False)
    return sq, sf, f, hq * sq, 1.0 - f, gcum


def _head(x, hd):
    return x[:, hd * HG_DIM:(hd + 1) * HG_DIM]


def _all_heads(fn):
    return jnp.concatenate([fn(hd) for hd in range(HG_HEADS)], axis=1)


def _hg_blocks(q, kk, gcum):
    rowi = lax.broadcasted_iota(jnp.int32, gcum.shape, 0)
    out = []
    for blk in range(CHUNK // SUB):
        lo, hi = blk * SUB, (blk + 1) * SUB
        gb = gcum[lo - 1:lo, :] if blk else jnp.zeros_like(gcum[0:1, :])
        eq = jnp.exp(gcum[lo:hi, :] - gb)
        ek = jnp.exp(jnp.where(rowi < hi, gb - gcum, 0.0))
        out.append((eq, ek, (q[lo:hi, :] * eq).astype(BF16), (kk * ek).astype(BF16)))
    return out


def _hg_scores(blocks):
    out = []
    for hd in range(HG_HEADS):
        a = jnp.concatenate([_dot_nt(_head(qb, hd), _head(kb, hd)) for _, _, qb, kb in blocks], axis=0)
        out.append(jnp.where(_tri(True), a, 0.0))
    return out


HG_STEP_CHUNKS = 8


def _hgrn_fwd(proj, lbp):
    T = proj.shape[0]
    nc = T // CHUNK
    ns = min(HG_STEP_CHUNKS, nc)
    rows = ns * CHUNK

    def body(hq_ref, hf_ref, hi_ref, lbp_ref, o_ref, st_ref, state):
        @pl.when(pl.program_id(0) == 0)
        def _():
            state[...] = jnp.zeros_like(state)

        lb = _lower_bound(lbp_ref[...])
        work = []
        for c in range(ns):
            r = slice(c * CHUNK, (c + 1) * CHUNK)
            _, _, _, q, kk, gcum = _hg_gates(hq_ref[r, :], hf_ref[r, :], lb)
            vb = hi_ref[r, :].astype(BF16)
            a = _hg_scores(_hg_blocks(q, kk, gcum))
            gend = gcum[CHUNK - 1:CHUNK, :]
            qgb = (q * jnp.exp(gcum)).astype(BF16)
            kgeb = (kk * jnp.exp(gend - gcum)).astype(BF16)
            intra = [_dot(a[hd].astype(BF16), _head(vb, hd)) for hd in range(HG_HEADS)]
            update = [_dot_tn(_head(vb, hd), _head(kgeb, hd)) for hd in range(HG_HEADS)]
            work.append((qgb, jnp.exp(gend), intra, update))
        for hd in range(HG_HEADS):
            st = state[hd]
            for c, (qgb, egend, intra, update) in enumerate(work):
                st_ref[c, hd] = st
                o_ref[c * CHUNK:(c + 1) * CHUNK, hd * HG_DIM:(hd + 1) * HG_DIM] = (
                    intra[hd] + _dot_nt(_head(qgb, hd), st.astype(BF16)))
                st = st * _head(egend, hd) + update[hd]
            state[hd] = st

    def col(cb):
        return pl.BlockSpec((rows, HG_WIDTH), lambda i: (i, cb))

    return pl.pallas_call(
        body, name="hgrn_fwd", grid=(nc // ns,),
        in_specs=[col(P_HQ // HG_WIDTH), col(P_HF // HG_WIDTH), col(P_HI // HG_WIDTH), _full((2, HG_WIDTH))],
        out_specs=[pl.BlockSpec((rows, HG_WIDTH), lambda i: (i, 0)),
                   pl.BlockSpec((ns, HG_HEADS, HG_DIM, HG_DIM), lambda i: (i, 0, 0, 0))],
        out_shape=[jax.ShapeDtypeStruct((T, HG_WIDTH), F32),
                   jax.ShapeDtypeStruct((nc, HG_HEADS, HG_DIM, HG_DIM), F32)],
        scratch_shapes=[pltpu.VMEM((HG_HEADS, HG_DIM, HG_DIM), F32)],
        compiler_params=_params(1),
    )(proj, proj, proj, lbp)


def _hgrn_bwd(proj, lbp, do_hg, states):
    T = proj.shape[0]
    nc = T // CHUNK
    ns = min(HG_STEP_CHUNKS, nc)
    rows = ns * CHUNK
    steps = nc // ns

    def body(hq_ref, hf_ref, hi_ref, lbp_ref, do_ref, st_ref, dhq_ref, dhf_ref, dhi_ref, dlb_ref, dstate):
        @pl.when(pl.program_id(0) == 0)
        def _():
            dstate[...] = jnp.zeros_like(dstate)
            dlb_ref[...] = jnp.zeros_like(dlb_ref)

        lb = _lower_bound(lbp_ref[...])

        dst_all = [dstate[hd] for hd in range(HG_HEADS)]
        dlb = jnp.zeros_like(lb)
        last = lax.broadcasted_iota(jnp.int32, (CHUNK, HG_WIDTH), 0) == CHUNK - 1
        for c in reversed(range(ns)):
            r = slice(c * CHUNK, (c + 1) * CHUNK)
            hq = hq_ref[r, :]
            sq, sf, f, q, kk, gcum = _hg_gates(hq, hf_ref[r, :], lb)
            vb = hi_ref[r, :].astype(BF16)
            dob = do_ref[r, :].astype(BF16)
            blocks = _hg_blocks(q, kk, gcum)
            a = _hg_scores(blocks)
            gend = gcum[CHUNK - 1:CHUNK, :]
            eg, egend, ekend = jnp.exp(gcum), jnp.exp(gend), jnp.exp(gend - gcum)
            qg, kge = q * eg, kk * ekend
            qgb, kgeb = qg.astype(BF16), kge.astype(BF16)

            dv, dqg, dkge, st_dst, dq_blk, dk_blk = [], [], [], [], [], []
            for hd in range(HG_HEADS):
                st = st_ref[c, hd]
                dst = dst_all[hd]
                dstb = dst.astype(BF16)
                do_h, v_h = _head(dob, hd), _head(vb, hd)
                dv.append(_dot_tn(a[hd].astype(BF16), do_h) + _dot_nt(_head(kgeb, hd), dstb))
                da = jnp.where(_tri(True), _dot_nt(do_h, v_h), 0.0).astype(BF16)
                dqg.append(_dot(do_h, st.astype(BF16)))
                dkge.append(_dot(v_h, dstb))
                st_dst.append(jnp.sum(st * dst, axis=0, keepdims=True))
                dst_all[hd] = _dot_tn(do_h, _head(qgb, hd)) + dst * _head(egend, hd)
                dq_blk.append([_dot(da[b * SUB:(b + 1) * SUB, :], _head(kb, hd)) for b, (_, _, _, kb) in enumerate(blocks)])
                dk_blk.append([_dot_tn(da[b * SUB:(b + 1) * SUB, :], _head(qb, hd)) for b, (_, _, qb, _) in enumerate(blocks)])
            dv, dqg, dkge, st_dst = (jnp.concatenate(t, axis=1) for t in (dv, dqg, dkge, st_dst))

            dq_a, dg_q = [], []
            dk_a, dg_k = jnp.zeros_like(gcum), jnp.zeros_like(gcum)
            for b, (eq, ek, qb, kb) in enumerate(blocks):
                dq_b = _all_heads(lambda hd: dq_blk[hd][b])
                dk_b = _all_heads(lambda hd: dk_blk[hd][b])
                dq_a.append(dq_b * eq)
                dk_a = dk_a + dk_b * ek
                dg_q.append(qb.astype(F32) * dq_b)
                dg_k = dg_k + kb.astype(F32) * dk_b
            dq_a = jnp.concatenate(dq_a, axis=0)

            dgend = st_dst * egend + jnp.sum(dkge * kge, axis=0, keepdims=True)
            dq = dq_a + dqg * eg
            dk = dk_a + dkge * ekend
            dgc = jnp.concatenate(dg_q, axis=0) - dg_k + dqg * qg - dkge * kge + jnp.where(last, dgend, 0.0)
            dg = _running_sum(dgc, True)
            df = dg / f - dk
            dhf_ref[r, :] = df * (1.0 - lb) * sf * (1.0 - sf)
            dlb = dlb + jnp.sum(df * (1.0 - sf), axis=0, keepdims=True)
            dhq_ref[r, :] = dq * (sq * (1.0 + hq * (1.0 - sq)))
            dhi_ref[r, :] = dv
        for hd in range(HG_HEADS):
            dstate[hd] = dst_all[hd]
        dlb_ref[...] += dlb

    def col(cb):
        return pl.BlockSpec((rows, HG_WIDTH), lambda i: (steps - 1 - i, cb))

    grad = jax.ShapeDtypeStruct((T, HG_WIDTH), F32)
    return pl.pallas_call(
        body, name="hgrn_bwd", grid=(steps,),
        in_specs=[col(P_HQ // HG_WIDTH), col(P_HF // HG_WIDTH), col(P_HI // HG_WIDTH), _full((2, HG_WIDTH)),
                  col(0), pl.BlockSpec((ns, HG_HEADS, HG_DIM, HG_DIM), lambda i: (steps - 1 - i, 0, 0, 0))],
        out_specs=[col(0), col(0), col(0), _full((1, HG_WIDTH))],
        out_shape=[grad, grad, grad, jax.ShapeDtypeStruct((1, HG_WIDTH), F32)],
        scratch_shapes=[pltpu.VMEM((HG_HEADS, HG_DIM, HG_DIM), F32)],
        compiler_params=_params(1),
    )(proj, proj, proj, lbp, do_hg, states)


def _top(x, tgt, o_mla, o_hg, proj, w_out, hg_norm_g, final_g, tm):
    T = x.shape[0]

    def body(x_ref, tgt_ref, om_ref, oh_ref, gm_ref, gh_ref, wout_ref, hgn_ref, fng_ref,
             dx2_ref, dom_ref, dsum_ref, dgm_ref, doh_ref, dgh_ref, loss_ref, dfng_ref, dhgn_ref, dwout_ref, ycat_ref):
        @pl.when(pl.program_id(0) == 0)
        def _():
            for ref in (loss_ref, dfng_ref, dhgn_ref, dwout_ref):
                ref[...] = jnp.zeros_like(ref)

        gm, om = gm_ref[...], om_ref[...]
        sgm = _sigmoid(gm)
        silu_m = gm * sgm
        gh, oh, gam = gh_ref[...], oh_ref[...], hgn_ref[...]
        sgh = _sigmoid(gh)
        silu_h = gh * sgh
        rr, nn = [], []
        for hd in range(HG_HEADS):
            oh_h = oh[:, hd * HG_DIM:(hd + 1) * HG_DIM]
            r_h = lax.rsqrt(jnp.mean(oh_h * oh_h, axis=-1, keepdims=True) + EPS)
            rr.append(r_h)
            nn.append(oh_h * r_h)
        n = jnp.concatenate(nn, axis=1)
        ng = n * gam
        ycat_ref[:, :MLA_WIDTH] = (om * silu_m).astype(BF16)
        ycat_ref[:, MLA_WIDTH:] = (ng * silu_h).astype(BF16)
        wout = wout_ref[...]
        x2 = x_ref[...] + _dot(ycat_ref[...], wout)
        r = lax.rsqrt(jnp.mean(x2 * x2, axis=-1, keepdims=True) + EPS)
        xh = x2 * r
        fng = fng_ref[...]
        err = xh * fng - tgt_ref[...]
        loss_ref[...] += 0.5 * jnp.sum(jnp.mean(err * err, axis=-1, keepdims=True), axis=0, keepdims=True)
        dout = err * (1.0 / D_MODEL)
        dfng_ref[...] += jnp.sum(dout * xh, axis=0, keepdims=True)
        dxh = dout * fng
        dx2 = r * (dxh - xh * jnp.mean(dxh * xh, axis=-1, keepdims=True))
        dx2_ref[...] = dx2
        dx2b = dx2.astype(BF16)
        dwout_ref[...] += _dot_tn(ycat_ref[...], dx2b)
        dycat = _dot_nt(dx2b, wout)
        dym, dyh = dycat[:, :MLA_WIDTH], dycat[:, MLA_WIDTH:]
        dom = dym * silu_m
        first = lax.broadcasted_iota(jnp.int32, (tm, LANES), 1) < V_DIM
        for pp in range(N_HEADS // 2):
            pair = dom[:, pp * LANES:(pp + 1) * LANES]
            dom_ref[:, 2 * pp * HEAD_LANES:(2 * pp + 1) * HEAD_LANES] = jnp.where(first, pair, 0.0).astype(BF16)
            dom_ref[:, (2 * pp + 1) * HEAD_LANES:(2 * pp + 2) * HEAD_LANES] = jnp.where(first, 0.0, pair).astype(BF16)
        head_of = lax.broadcasted_iota(jnp.int32, (MLA_WIDTH, LANES), 0) // V_DIM
        pick = (head_of == lax.broadcasted_iota(jnp.int32, (MLA_WIDTH, LANES), 1)).astype(F32)
        dsum_ref[...] = jnp.dot(dom * om, pick, precision=lax.Precision.HIGHEST, preferred_element_type=F32)
        dgm_ref[...] = dym * om * (sgm * (1.0 + gm * (1.0 - sgm)))
        dgh_ref[...] = dyh * ng * (sgh * (1.0 + gh * (1.0 - sgh)))
        dng = dyh * silu_h
        dhgn_ref[...] += jnp.sum(dng * n, axis=0, keepdims=True)
        dn = dng * gam
        for hd in range(HG_HEADS):
            sl = slice(hd * HG_DIM, (hd + 1) * HG_DIM)
            dn_h, n_h = dn[:, sl], nn[hd]
            doh_ref[:, sl] = rr[hd] * (dn_h - n_h * jnp.mean(dn_h * n_h, axis=-1, keepdims=True))

    def row(w, cb=0):
        return pl.BlockSpec((tm, w), lambda i: (i, cb))

    outs = [(D_MODEL, F32), (N_HEADS * HEAD_LANES, BF16), (LANES, F32), (MLA_WIDTH, F32), (HG_WIDTH, F32), (HG_WIDTH, F32)]
    small = [(1, SMALL_W), (1, D_MODEL), (1, HG_WIDTH), (D_MODEL, D_MODEL)]
    return pl.pallas_call(
        body, name="top", grid=(T // tm,),
        in_specs=[row(D_MODEL), row(D_MODEL), row(MLA_WIDTH), row(HG_WIDTH),
                  row(MLA_WIDTH, P_GM // MLA_WIDTH), row(HG_WIDTH, P_GH // HG_WIDTH),
                  _full((D_MODEL, D_MODEL)), _full((1, HG_WIDTH)), _full((1, D_MODEL))],
        out_specs=[row(w) for w, _ in outs] + [_full(s) for s in small],
        out_shape=[jax.ShapeDtypeStruct((T, w), dt) for w, dt in outs] + [jax.ShapeDtypeStruct(s, F32) for s in small],
        scratch_shapes=[pltpu.VMEM((tm, D_MODEL), BF16)],
        compiler_params=_params(1),
    )(x, tgt, o_mla, o_hg, proj, proj, w_out, hg_norm_g, final_g)


def _bot(x, dx2, proj, h, qn, kvn, dq, dk, dv, dgm, dhq, dhf, dhi, dgh, c_t, s1_t, s2_t, w_in_t, w_q_p, w_kv_p, ln_g, q_g, kv_g, tm):
    T = x.shape[0]
    lat_w = D_PERM - P_QL
    steps = T // tm

    def body(x_ref, dx2_ref, lat_ref, h_ref, qn_ref, kvn_ref, dq_ref, dk_ref, dv_ref, dgm_ref, dhq_ref, dhf_ref,
             dhi_ref, dgh_ref, c_ref, s1_ref, s2_ref, win_ref, wq_ref, wkv_ref, lng_ref, qg_ref, kvg_ref,
             dx_ref, dlng_ref, dqg_ref, dkvg_ref, dwq_ref, dwkv_ref, dwin_ref, dqpre_ref, dkv_ref, dproj_ref,
             dwin_acc, sem):
        @pl.when(pl.program_id(0) == 0)
        def _():
            for ref in (dlng_ref, dqg_ref, dkvg_ref, dwq_ref, dwkv_ref, dwin_acc):
                ref[...] = jnp.zeros_like(ref)

        c, s1, s2 = c_ref[...], s1_ref[...], s2_ref[...]
        dkpe = jnp.zeros((tm, LANES), F32)
        for hd in range(N_HEADS):
            sl = slice(hd * HEAD_LANES, (hd + 1) * HEAD_LANES)
            dqpre_ref[:, sl] = _rope_bwd(dq_ref[:, sl], c, s1, s2).astype(BF16)
            dk_h = dk_ref[:, sl]
            dkpe = dkpe + dk_h
            dkv_ref[:, sl] = dk_h.astype(BF16)
        dkv_ref[:, N_HEADS * HEAD_LANES:] = dv_ref[...].astype(BF16)
        lane = lax.broadcasted_iota(jnp.int32, (tm, LANES), 1)
        rope_lanes = jnp.logical_and(lane >= ROPE_LO, lane < ROPE_LO + ROPE)
        dkr = jnp.where(rope_lanes, _rope_bwd(dkpe, c, s1, s2), 0.0)

        def norm_bwd(v, g, dy):
            r = lax.rsqrt(jnp.mean(v * v, axis=-1, keepdims=True) + EPS)
            vh = v * r
            dvh = dy * g
            return jnp.sum(dy * vh, axis=0, keepdims=True), r * (dvh - vh * jnp.mean(dvh * vh, axis=-1, keepdims=True))

        dwq_ref[...] += _dot_tn(qn_ref[...], dqpre_ref[...])
        dwkv_ref[...] += _dot_tn(kvn_ref[...], dkv_ref[...])
        dqn = _dot_nt(dqpre_ref[...], wq_ref[...])
        dg_q, dql = norm_bwd(lat_ref[:, :Q_RANK], qg_ref[...], dqn)
        dqg_ref[...] += dg_q
        dkn = _dot_nt(dkv_ref[...], wkv_ref[...])
        dg_kv, dkvl = norm_bwd(lat_ref[:, Q_RANK:Q_RANK + KV_RANK], kvg_ref[...], dkn)
        dkvg_ref[...] += dg_kv

        dproj_ref[:, P_GM:P_GM + MLA_WIDTH] = dgm_ref[...].astype(BF16)
        dproj_ref[:, P_HQ:P_HQ + HG_WIDTH] = dhq_ref[...].astype(BF16)
        dproj_ref[:, P_HF:P_HF + HG_WIDTH] = dhf_ref[...].astype(BF16)
        dproj_ref[:, P_HI:P_HI + HG_WIDTH] = dhi_ref[...].astype(BF16)
        dproj_ref[:, P_GH:P_GH + HG_WIDTH] = dgh_ref[...].astype(BF16)
        dproj_ref[:, P_QL:P_QL + Q_RANK] = dql.astype(BF16)
        dproj_ref[:, P_KVL:P_KVL + KV_RANK] = dkvl.astype(BF16)
        dproj_ref[:, P_KR:P_KR + LANES] = dkr.astype(BF16)
        dh = sum(_dot(dproj_ref[:, col:col + rows.shape[0]], rows) for col, rows in _in_proj_rows(win_ref))
        dg_ln, dxn = norm_bwd(x_ref[...], lng_ref[...], dh)
        dlng_ref[...] += dg_ln
        dx_ref[...] = dx2_ref[...] + dxn
        dwin_acc[...] += _dot_tn(dproj_ref[...], h_ref[...])

        @pl.when(pl.program_id(0) == steps - 1)
        def _():
            kr = P_KR + ROPE_LO
            moves = [((P_GM, P_QL), R_MAIN), ((P_QL, P_KR), (R_QL[0], R_KVL[1])), ((kr, kr + ROPE), R_KR)]
            copies = [pltpu.make_async_copy(dwin_acc.at[a:b, :], dwin_ref.at[c:d, :], sem.at[n])
                      for n, ((a, b), (c, d)) in enumerate(moves)]
            for cp in copies:
                cp.start()
            for cp in copies:
                cp.wait()

    def row(w, cb=0):
        return pl.BlockSpec((tm, w), lambda i: (i, cb))

    hl = N_HEADS * HEAD_LANES
    outs = [(D_MODEL, F32)]
    small = [(1, D_MODEL), (1, Q_RANK), (1, KV_RANK), (Q_RANK, hl), (KV_RANK, hl + MLA_WIDTH)]
    return pl.pallas_call(
        body, name="bot", grid=(steps,),
        in_specs=[row(D_MODEL), row(D_MODEL), row(lat_w, P_QL // lat_w), row(D_MODEL), row(Q_RANK), row(KV_RANK),
                  row(hl), row(hl), row(MLA_WIDTH),
                  row(MLA_WIDTH), row(HG_WIDTH), row(HG_WIDTH), row(HG_WIDTH), row(HG_WIDTH),
                  row(LANES), row(LANES), row(LANES),
                  _full((D_IN, D_MODEL)), _full((Q_RANK, hl)), _full((KV_RANK, hl + MLA_WIDTH)),
                  _full((1, D_MODEL)), _full((1, Q_RANK)), _full((1, KV_RANK))],
        out_specs=[row(w) for w, _ in outs] + [_full(s) for s in small] + [pl.BlockSpec(memory_space=pl.ANY)],
        out_shape=[jax.ShapeDtypeStruct((T, w), dt) for w, dt in outs] + [jax.ShapeDtypeStruct(s, F32) for s in small]
        + [jax.ShapeDtypeStruct((D_IN, D_MODEL), F32)],
        scratch_shapes=[pltpu.VMEM((tm, hl), BF16), pltpu.VMEM((tm, hl + MLA_WIDTH), BF16),
                        pltpu.VMEM((tm, D_PERM), BF16), pltpu.VMEM((D_PERM, D_MODEL), F32),
                        pltpu.SemaphoreType.DMA((3,))],
        compiler_params=_params(1),
    )(x, dx2, proj, h, qn, kvn, dq, dk, dv, dgm, dhq, dhf, dhi, dgh, c_t, s1_t, s2_t, w_in_t, w_q_p, w_kv_p, ln_g,
      q_g, kv_g)


RS_ROWS = 256


def _reduce_scatter(slabs, small):
    n = len(slabs)
    units = []
    for a, s in enumerate(slabs):
        rows, cols = s.shape[1:]
        if rows % RS_ROWS == 0 or rows < RS_ROWS:
            units += [(a, (pl.ds(r0, min(rows, RS_ROWS)), slice(None))) for r0 in range(0, rows, RS_ROWS)]
        else:
            units += [(a, (slice(None), pl.ds(c0, RS_ROWS))) for c0 in range(0, cols, RS_ROWS)]
    nu = len(units)

    def body(*refs):
        ins, small_ref = refs[:n], refs[n]
        outs, small_out = refs[n + 1:2 * n + 1], refs[2 * n + 1]
        own, sib_land, ici_out, ici_land = (refs[(2 + g) * n + 2:(3 + g) * n + 2] for g in range(4))
        small_land = refs[6 * n + 2]
        loc_sems, d2d_send, d2d_recv, ici_send, ici_recv, sm_send, sm_recv = refs[6 * n + 3:6 * n + 10]
        x, y, c = lax.axis_index("x"), lax.axis_index("y"), lax.axis_index("c")
        me = 4 * x + 2 * y + c

        def chip(k):
            return (1 - x if k & 2 else x, 1 - y if k & 1 else y)

        def block(k, core):
            px, py = chip(k)
            return 4 * px + 2 * py + core

        def part(u):
            return units[u]

        def local(u, k):
            a, rows = part(u)
            return pltpu.make_async_copy(ins[a].at[(block(k, c),) + rows], own[a].at[(k,) + rows], loc_sems.at[u, k])

        def to_sibling(u, k):
            a, rows = part(u)
            return pltpu.make_async_remote_copy(
                src_ref=ins[a].at[(block(k, 1 - c),) + rows], dst_ref=sib_land[a].at[(k,) + rows],
                send_sem=d2d_send.at[u, k], recv_sem=d2d_recv.at[u, k], device_id=(x, y, 1 - c), device_id_type=MESH)

        def to_chip(u, k):
            a, rows = part(u)
            return pltpu.make_async_remote_copy(
                src_ref=ici_out[a].at[(k - 1,) + rows], dst_ref=ici_land[a].at[(k - 1,) + rows],
                send_sem=ici_send.at[u, k - 1], recv_sem=ici_recv.at[u, k - 1], device_id=(*chip(k), c),
                device_id_type=MESH)

        def small_copy(k, receiving):
            px, py = chip(k >> 1)
            pc = 1 - c if k & 1 else c
            slot = 4 * px + 2 * py + pc if receiving else me
            return pltpu.make_async_remote_copy(
                src_ref=small_ref, dst_ref=small_land.at[slot], send_sem=sm_send.at[k - 1], recv_sem=sm_recv.at[k - 1],
                device_id=(px, py, pc), device_id_type=MESH)

        for u in range(nu):
            for k in range(4):
                local(u, k).start()
        for u in range(nu):
            for k in range(4):
                to_sibling(u, k).start()
        small_land[me] = small_ref[...]
        for k in range(1, N_DEV):
            small_copy(k, False).start()
        for u in range(nu):
            a, rows = part(u)
            for k in range(4):
                local(u, k).wait()
                to_sibling(u, k).wait_recv()
            for k in range(1, 4):
                ici_out[a][(k - 1,) + rows] = (own[a][(k,) + rows] + sib_land[a][(k,) + rows]).astype(BF16)
                to_chip(u, k).start()
        for u in range(nu):
            a, rows = part(u)
            acc = own[a][(0,) + rows] + sib_land[a][(0,) + rows]
            for k in range(1, 4):
                to_chip(u, k).wait_recv()
                acc = acc + ici_land[a][(k - 1,) + rows].astype(F32)
            outs[a][rows] = acc
        for k in range(1, N_DEV):
            small_copy(k, True).wait_recv()
        acc = small_land[0]
        for d in range(1, N_DEV):
            acc = acc + small_land[d]
        small_out[...] = acc
        for u in range(nu):
            for k in range(4):
                to_sibling(u, k).wait_send()
            for k in range(1, 4):
                to_chip(u, k).wait_send()
        for k in range(1, N_DEV):
            small_copy(k, False).wait_send()

    vm = pl.BlockSpec(memory_space=pltpu.VMEM)
    hbm = pl.BlockSpec(memory_space=pl.ANY)
    dma = pltpu.SemaphoreType.DMA
    return pl.pallas_call(
        body, name="reduce_scatter_grads",
        in_specs=[hbm] * n + [vm], out_specs=[vm] * (n + 1),
        out_shape=[jax.ShapeDtypeStruct(s.shape[1:], F32) for s in slabs] + [jax.ShapeDtypeStruct(small.shape, F32)],
        scratch_shapes=[pltpu.VMEM((4,) + s.shape[1:], F32) for s in slabs] * 2
        + [pltpu.VMEM((3,) + s.shape[1:], BF16) for s in slabs] * 2
        + [pltpu.VMEM((N_DEV,) + small.shape, F32)]
        + [dma((nu, 4)), dma((nu, 4)), dma((nu, 4)), dma((nu, 3)), dma((nu, 3)), dma((N_DEV - 1,)), dma((N_DEV - 1,))],
        compiler_params=pltpu.CompilerParams(vmem_limit_bytes=VMEM_LIMIT),
    )(*slabs, small)


def _adamw_math(w, g, m, v):
    m = ADAM_B1 * m + (1.0 - ADAM_B1) * g
    v = ADAM_B2 * v + (1.0 - ADAM_B2) * (g * g)
    m_hat = m / (1.0 - ADAM_B1 ** ADAM_STEP)
    v_hat = v / (1.0 - ADAM_B2 ** ADAM_STEP)
    delta = -ADAM_LR * (m_hat / (jnp.sqrt(v_hat) + ADAM_EPS) + ADAM_WD * w)
    return delta, m, v


SMALL_W = 512


def _adamw(big, small_w, small_g):
    nb, ns = len(big), len(small_w)

    def body(*refs):
        k = 0
        big_in = [refs[4 * i:4 * i + 4] for i in range(nb)]
        k = 4 * nb
        small_in = [refs[k + 3 * i:k + 3 * i + 3] for i in range(ns)]
        k += 3 * ns
        sg_ref = refs[k]
        k += 1
        big_out = [refs[k + 3 * i:k + 3 * i + 3] for i in range(nb)]
        k += 3 * nb
        small_out = [refs[k + 4 * i:k + 4 * i + 4] for i in range(ns)]

        for (w, g, m, v), (od, om, ov) in zip(big_in, big_out):
            od[...], om[...], ov[...] = _adamw_math(w[...], g[...], m[...], v[...])

        sg = sg_ref[...]
        lbp = small_in[2][0][...]
        lb = _lower_bound(lbp)
        t = sg[4:5, :] * lb * (1.0 - lb)
        grads = [jnp.concatenate([sg[0:1, :], sg[1:2, :]], axis=1),
                 jnp.concatenate([sg[2:3, :], sg[3:4, :]], axis=1),
                 jnp.concatenate([t, -t], axis=0),
                 sg[6:7, :], sg[7:8, 0:Q_RANK], sg[7:8, Q_RANK:Q_RANK + KV_RANK]]
        for (w, m, v), g, (og, od, om, ov) in zip(small_in, grads, small_out):
            og[...] = g
            od[...], om[...], ov[...] = _adamw_math(w[...], g, m[...], v[...])

    ins = [a for grp in big for a in grp] + [a for grp in small_w for a in grp] + [small_g]
    out_shape = ([jax.ShapeDtypeStruct(grp[0].shape, F32) for grp in big for _ in range(3)]
                 + [jax.ShapeDtypeStruct(grp[0].shape, F32) for grp in small_w for _ in range(4)])
    vm = pl.BlockSpec(memory_space=pltpu.VMEM)
    res = pl.pallas_call(
        body, name="adamw", in_specs=[vm] * len(ins), out_specs=[vm] * len(out_shape), out_shape=out_shape,
        compiler_params=pltpu.CompilerParams(vmem_limit_bytes=VMEM_LIMIT),
    )(*ins)
    big_res = [res[3 * i:3 * i + 3] for i in range(nb)]
    small_res = [res[3 * nb + 4 * i:3 * nb + 4 * i + 4] for i in range(ns)]
    return big_res, small_res


def _perm_weights(g_in_t, g_q, g_kv, g_out):
    w_in_t = g_in_t.reshape(D_IN, D_MODEL)
    wq = g_q.transpose(1, 0, 2)
    w_q_p = jnp.pad(wq, ((0, 0), (0, 0), (0, HEAD_LANES - NOPE - ROPE))).reshape(Q_RANK, N_HEADS * HEAD_LANES)
    wkv = g_kv.transpose(1, 0, 2)
    wk = jnp.pad(wkv[:, :, :NOPE], ((0, 0), (0, 0), (0, HEAD_LANES - NOPE))).reshape(KV_RANK, N_HEADS * HEAD_LANES)
    wv = wkv[:, :, NOPE:].reshape(KV_RANK, MLA_WIDTH)
    return w_in_t, w_q_p, jnp.concatenate([wk, wv], axis=1), g_out.reshape(D_MODEL, D_MODEL)


def _grad_slabs(dw_in_t, dw_q_p, dw_kv_p, dw_out):
    s_in = dw_in_t.reshape(N_DEV, D_IN // N_DEV, D_MODEL)
    s_q = dw_q_p.reshape(Q_RANK, N_HEADS, HEAD_LANES)[:, :, :NOPE + ROPE].transpose(1, 0, 2)
    hl = N_HEADS * HEAD_LANES
    dk = dw_kv_p[:, :hl].reshape(KV_RANK, N_HEADS, HEAD_LANES)[:, :, :NOPE]
    dv = dw_kv_p[:, hl:].reshape(KV_RANK, N_HEADS, V_DIM)
    s_kv = jnp.concatenate([dk, dv], axis=2).transpose(1, 0, 2)
    return s_in, s_q, s_kv, dw_out.reshape(N_DEV, D_MODEL // N_DEV, D_MODEL)


def _block_sizes(T):
    return min(256, T), min(256, T), min(512, T)


def kernel(x, positions, ln_g, w_in, q_a_norm_g, w_q_b, kv_a_norm_g, w_kv_b, hg_lower_bounds, hg_norm_g, w_out, final_norm_g, loss_target, m_ln_g, m_w_in, m_q_a_norm_g, m_w_q_b, m_kv_a_norm_g, m_w_kv_b, m_hg_lower_bounds, m_hg_norm_g, m_w_out, m_final_norm_g, v_ln_g, v_w_in, v_q_a_norm_g, v_w_q_b, v_kv_a_norm_g, v_w_kv_b, v_hg_lower_bounds, v_hg_norm_g, v_w_out, v_final_norm_g):
    T = x.shape[1]
    tm, tq, bt = _block_sizes(T)
    nq = T // tq
    xs, tgt = x[0], loss_target[0]
    pos_f = positions.astype(F32)
    fng = final_norm_g.reshape(1, D_MODEL)

    w_in_shard_t = w_in[0].T
    gathered, (c_t, s1_t, s2_t) = _all_gather_weights([w_in_shard_t, w_q_b[0], w_kv_b[0], w_out[0]], pos_f)
    w_in_t, w_q_p, w_kv_p, w_out_b = _perm_weights(*gathered)

    proj, h, qn, kvn, q, k, v = _fwd_in(xs, ln_g, w_in_t, q_a_norm_g, w_q_p, kv_a_norm_g, w_kv_p, c_t, s1_t, s2_t, bt)
    hl = N_HEADS * HEAD_LANES
    v_t = v.reshape(nq, tq, MLA_WIDTH).transpose(0, 2, 1)
    k_t = k.reshape(nq, tq, hl).transpose(0, 2, 1)
    q_t = q.reshape(nq, tq, hl).transpose(0, 2, 1)
    o_mla, lse = _attn_fwd_flat(k, q_t, v_t, tq)
    o_hg, states = _hgrn_fwd(proj, hg_lower_bounds)
    dx2, d_om, dsum, d_gm, d_oh, d_gh, loss_p, d_fng, d_hgn, dw_out = _top(
        xs, tgt, o_mla, o_hg, proj, w_out_b, hg_norm_g, fng, tm)
    dsum = dsum[:, :N_HEADS].T.reshape(N_HEADS, nq, 1, tq)
    do_t = d_om.reshape(nq, tq, hl).transpose(0, 2, 1)
    dq_t, dk, dv = _attn_bwd_flat(k, v, q_t, k_t, do_t, lse, dsum, tq)
    dq = dq_t.transpose(0, 2, 1).reshape(T, N_HEADS * HEAD_LANES)
    d_hq, d_hf, d_hi, d_lb = _hgrn_bwd(proj, hg_lower_bounds, d_oh, states)
    dx, d_lng, d_qg, d_kvg, dw_q_p, dw_kv_p, dw_in_t = _bot(
        xs, dx2, proj, h, qn, kvn, dq, dk, dv, d_gm, d_hq, d_hf, d_hi, d_gh, c_t, s1_t, s2_t, w_in_t, w_q_p, w_kv_p,
        ln_g, q_a_norm_g, kv_a_norm_g, tm)

    small = jnp.concatenate([
        d_lng.reshape(2, SMALL_W), d_fng.reshape(2, SMALL_W), d_lb, loss_p, d_hgn,
        jnp.concatenate([d_qg, d_kvg, jnp.zeros((1, SMALL_W - Q_RANK - KV_RANK), F32)], axis=1)], axis=0)
    g_in, g_q, g_kv, g_out, small_sum = _reduce_scatter(list(_grad_slabs(dw_in_t, dw_q_p, dw_kv_p, dw_out)), small)

    big = [(w_in_shard_t, g_in, m_w_in[0].T, v_w_in[0].T), (w_q_b[0], g_q, m_w_q_b[0], v_w_q_b[0]),
           (w_kv_b[0], g_kv, m_w_kv_b[0], v_w_kv_b[0]), (w_out[0], g_out, m_w_out[0], v_w_out[0])]
    small_w = [(ln_g, m_ln_g, v_ln_g),
               (fng, m_final_norm_g.reshape(1, D_MODEL), v_final_norm_g.reshape(1, D_MODEL)),
               (hg_lower_bounds, m_hg_lower_bounds, v_hg_lower_bounds), (hg_norm_g, m_hg_norm_g, v_hg_norm_g),
               (q_a_norm_g, m_q_a_norm_g, v_q_a_norm_g), (kv_a_norm_g, m_kv_a_norm_g, v_kv_a_norm_g)]
    big_res, small_res = _adamw(big, small_w, small_sum)

    loss = small_sum[5, 0]
    (r_in, r_q, r_kv, r_out) = big_res
    (s_ln, s_fn, s_lb, s_hgn, s_qg, s_kvg) = small_res
    flat = lambda t: t.reshape(D_MODEL)
    lead = lambda t: t[None]
    grads = [s_ln[0], lead(g_in.T), s_qg[0], lead(g_q), s_kvg[0], lead(g_kv), s_lb[0], s_hgn[0], lead(g_out), flat(s_fn[0])]

    def pick(i):
        return [s_ln[i + 1], lead(r_in[i].T), s_qg[i + 1], lead(r_q[i]), s_kvg[i + 1], lead(r_kv[i]), s_lb[i + 1],
                s_hgn[i + 1], lead(r_out[i]), flat(s_fn[i + 1])]

    return (loss, dx[None], *grads, *pick(0), *pick(1), *pick(2))
```

```python
import math

import numpy as np
import jax
import jax.numpy as jnp
from jax import lax
from jax.experimental import pallas as pl
from jax.experimental.pallas import tpu as pltpu

F32 = jnp.float32
BF16 = jnp.bfloat16

D_MODEL = 1024
N_HEADS = 8
NOPE = 64
ROPE = 32
HALF_ROPE = ROPE // 2
V_DIM = 64
Q_RANK = 256
KV_RANK = 128
MLA_WIDTH = N_HEADS * V_DIM
HG_HEADS = 4
HG_DIM = 128
HG_WIDTH = HG_HEADS * HG_DIM
CHUNK = 64
SUB = 16
D_IN = 2976
D_PERM = 3072
ROPE_THETA = 10000.0
EPS = 1e-6
N_DEV = 8
LANES = 128
HEAD_LANES = 128

P_GM, P_HQ, P_HF, P_HI, P_GH, P_QL, P_KVL, P_KR = 0, 512, 1024, 1536, 2048, 2560, 2816, 2944
R_QL, R_KVL, R_KR, R_MAIN = (0, 256), (256, 384), (384, 416), (416, 2976)
ROPE_LO = NOPE
SCALE = 1.0 / math.sqrt(NOPE + ROPE)

ADAM_LR = 0.001
ADAM_B1 = 0.9
ADAM_B2 = 0.999
ADAM_EPS = 1e-08
ADAM_WD = 0.01
ADAM_STEP = 10

VMEM_LIMIT = 56 * 1024 * 1024
MESH = pl.DeviceIdType.MESH

NT = (((1,), (1,)), ((), ()))
TN = (((0,), (0,)), ((), ()))


def _params(n_grid=0, **kw):
    sem = ("arbitrary",) * n_grid if n_grid else None
    return pltpu.CompilerParams(dimension_semantics=sem, vmem_limit_bytes=VMEM_LIMIT, **kw)


def _dot(a, b):
    return jnp.dot(a, b, preferred_element_type=F32)


def _dot_nt(a, b):
    return lax.dot_general(a, b, NT, preferred_element_type=F32)


def _dot_tn(a, b):
    return lax.dot_general(a, b, TN, preferred_element_type=F32)


def _sigmoid(x):
    return 1.0 / (1.0 + jnp.exp(-x))


def _rope_fwd(x, c, s1, s2):
    return x * c + pltpu.roll(x, LANES - HALF_ROPE, 1) * s1 + pltpu.roll(x, HALF_ROPE, 1) * s2


def _rope_bwd(dy, c, s1, s2):
    return dy * c - pltpu.roll(dy, LANES - HALF_ROPE, 1) * s1 - pltpu.roll(dy, HALF_ROPE, 1) * s2


def _in_proj_rows(wt_ref):
    kr = wt_ref[R_KR[0]:R_KR[1], :]
    pad = lambda n: jnp.zeros((n, D_MODEL), kr.dtype)
    return ((P_GM, wt_ref[R_MAIN[0]:R_MAIN[1], :]), (P_QL, wt_ref[R_QL[0]:R_QL[1], :]),
            (P_KVL, wt_ref[R_KVL[0]:R_KVL[1], :]),
            (P_KR, jnp.concatenate([pad(ROPE_LO), kr, pad(LANES - ROPE_LO - ROPE)], axis=0)))


def _full(shape):
    n = len(shape)
    return pl.BlockSpec(shape, lambda *_: (0,) * n)


ROPE_BLOCK = 512


def _rope_constants():
    inv = (np.float32(ROPE_THETA) ** (-np.arange(HALF_ROPE, dtype=np.float32) / np.float32(HALF_ROPE))).astype(np.float32)
    place = np.zeros((3, HALF_ROPE, LANES), np.float32)
    for i in range(HALF_ROPE):
        place[0, i, ROPE_LO + i] = place[0, i, ROPE_LO + HALF_ROPE + i] = 1.0
        place[1, i, ROPE_LO + i] = -1.0
        place[2, i, ROPE_LO + HALF_ROPE + i] = 1.0
    base = np.ones((1, LANES), np.float32)
    base[0, ROPE_LO:ROPE_LO + ROPE] = 0.0
    return jnp.asarray(inv.reshape(HALF_ROPE, 1)), jnp.asarray(place), jnp.asarray(base)


def _rope_block(pos, inv, place_ref, base):
    ang = inv * pos
    cos, sin = jnp.cos(ang), jnp.sin(ang)

    def put(v, k):
        return lax.dot_general(v, place_ref[k], TN, precision=lax.Precision.HIGHEST, preferred_element_type=F32)

    return put(cos, 0) + base, put(sin, 1), put(sin, 2)


def _all_gather_weights(shards, pos_f):
    n = len(shards)
    T = pos_f.shape[1]
    rb = min(ROPE_BLOCK, T)

    def body(*refs):
        ins, (pos_ref, inv_ref, place_ref, base_ref) = refs[:n], refs[n:n + 4]
        outs, tables = refs[n + 4:2 * n + 4], refs[2 * n + 4:2 * n + 7]
        send_sems, recv_sems = refs[2 * n + 7], refs[2 * n + 8]
        x, y, c = lax.axis_index("x"), lax.axis_index("y"), lax.axis_index("c")
        me, sibling = (x, y, c), (x, y, 1 - c)
        chips = [(1 - x, y), (x, 1 - y), (1 - x, 1 - y)]

        def idx(d):
            return 4 * d[0] + 2 * d[1] + d[2]

        def copy(a, k, block, to):
            rows = outs[a].at[idx(block)]
            return pltpu.make_async_remote_copy(src_ref=rows, dst_ref=rows, send_sem=send_sems.at[a, k],
                                                recv_sem=recv_sems.at[a, k], device_id=to, device_id_type=MESH)

        for a in range(n):
            outs[a][idx(me)] = ins[a][...].astype(BF16)
        first = []
        for a in range(n):
            first.append(copy(a, 0, me, sibling))
            first += [copy(a, 1 + j, me, (*chip, c)) for j, chip in enumerate(chips)]
        for cp in first:
            cp.start()
        for r0 in range(0, T, rb):
            for ref, tab in zip(tables, _rope_block(pos_ref[:, r0:r0 + rb], inv_ref[...], place_ref, base_ref[...])):
                ref[r0:r0 + rb, :] = tab
        passed = []
        for j, chip in enumerate(chips):
            for a in range(n):
                copy(a, 1 + j, (*chip, c), me).wait_recv()
                cp = copy(a, 4 + j, (*chip, c), sibling)
                cp.start()
                passed.append(cp)
        for a in range(n):
            copy(a, 0, sibling, me).wait_recv()
            for j, chip in enumerate(chips):
                copy(a, 4 + j, (*chip, 1 - c), me).wait_recv()
        for cp in first + passed:
            cp.wait_send()

    vm = pl.BlockSpec(memory_space=pltpu.VMEM)
    res = pl.pallas_call(
        body, name="all_gather_weights",
        in_specs=[vm] * (n + 4), out_specs=[vm] * (n + 3),
        out_shape=[jax.ShapeDtypeStruct((N_DEV,) + s.shape, BF16) for s in shards]
        + [jax.ShapeDtypeStruct((T, LANES), F32)] * 3,
        scratch_shapes=[pltpu.SemaphoreType.DMA((n, 7)), pltpu.SemaphoreType.DMA((n, 7))],
        compiler_params=pltpu.CompilerParams(vmem_limit_bytes=VMEM_LIMIT),
    )(*shards, pos_f, *_rope_constants())
    return res[:n], res[n:]


def _fwd_in(x, ln_g, w_in_t, q_g, w_q_p, kv_g, w_kv_p, c_t, s1_t, s2_t, tm, tq):
    T = x.shape[0]
    assert tm % tq == 0

    def body(x_ref, lng_ref, win_ref, qg_ref, wq_ref, kvg_ref, wkv_ref, c_ref, s1_ref, s2_ref,
             proj_ref, h_ref, qn_ref, kvn_ref, q_ref, k_ref, v_ref, vt_ref):
        xv = x_ref[...]
        r = lax.rsqrt(jnp.mean(xv * xv, axis=-1, keepdims=True) + EPS)
        h = (xv * r * lng_ref[...]).astype(BF16)
        h_ref[...] = h
        for col, rows in _in_proj_rows(win_ref):
            proj_ref[:, col:col + rows.shape[0]] = _dot_nt(h, rows)
        c, s1, s2 = c_ref[...], s1_ref[...], s2_ref[...]

        ql = proj_ref[:, P_QL:P_QL + Q_RANK]
        rq = lax.rsqrt(jnp.mean(ql * ql, axis=-1, keepdims=True) + EPS)
        qn = (ql * rq * qg_ref[...]).astype(BF16)
        qn_ref[...] = qn
        q = _dot(qn, wq_ref[...])
        for hd in range(N_HEADS):
            sl = slice(hd * HEAD_LANES, (hd + 1) * HEAD_LANES)
            q_ref[:, sl] = _rope_fwd(q[:, sl], c, s1, s2).astype(BF16)

        kvl = proj_ref[:, P_KVL:P_KVL + KV_RANK]
        rk = lax.rsqrt(jnp.mean(kvl * kvl, axis=-1, keepdims=True) + EPS)
        kvn = (kvl * rk * kvg_ref[...]).astype(BF16)
        kvn_ref[...] = kvn
        kv = _dot(kvn, wkv_ref[...])
        kpe = _rope_fwd(proj_ref[:, P_KR:P_KR + LANES], c, s1, s2)
        for hd in range(N_HEADS):
            sl = slice(hd * HEAD_LANES, (hd + 1) * HEAD_LANES)
            k_ref[:, sl] = (kv[:, sl] + kpe).astype(BF16)
        v_ref[...] = kv[:, N_HEADS * HEAD_LANES:].astype(BF16)
        for b in range(tm // tq):
            vt_ref[b] = kv[b * tq:(b + 1) * tq, N_HEADS * HEAD_LANES:].T.astype(BF16)

    def row(w):
        return pl.BlockSpec((tm, w), lambda i: (i, 0))

    outs = [(D_PERM, F32), (D_MODEL, BF16), (Q_RANK, BF16), (KV_RANK, BF16),
            (N_HEADS * HEAD_LANES, BF16), (N_HEADS * HEAD_LANES, BF16), (MLA_WIDTH, BF16)]
    return pl.pallas_call(
        body, name="fwd_in", grid=(T // tm,),
        in_specs=[row(D_MODEL), _full((1, D_MODEL)), _full((D_IN, D_MODEL)), _full((1, Q_RANK)),
                  _full((Q_RANK, N_HEADS * HEAD_LANES)), _full((1, KV_RANK)),
                  _full((KV_RANK, N_HEADS * HEAD_LANES + MLA_WIDTH)), row(LANES), row(LANES), row(LANES)],
        out_specs=[row(w) for w, _ in outs] + [pl.BlockSpec((tm // tq, MLA_WIDTH, tq), lambda i: (i, 0, 0))],
        out_shape=[jax.ShapeDtypeStruct((T, w), dt) for w, dt in outs]
        + [jax.ShapeDtypeStruct((T // tq, MLA_WIDTH, tq), BF16)],
        compiler_params=_params(1),
    )(x, ln_g, w_in_t, q_g, w_q_p, kv_g, w_kv_p, c_t, s1_t, s2_t)


LOG2E = 1.4426950408889634
SCALE2 = SCALE * LOG2E


def _causal(tq):
    r = lax.broadcasted_iota(jnp.int32, (tq, tq), 0)
    c = lax.broadcasted_iota(jnp.int32, (tq, tq), 1)
    return r <= c


MASKED = -1e30


def _causal_bias(bias_ref, tq):
    bias_ref[0] = jnp.zeros((tq, tq), F32)
    bias_ref[1] = jnp.where(_causal(tq), 0.0, MASKED)


def _tile_tables(nq, by_query):
    if by_query:
        pairs = [(j, i) for i in range(nq) for j in range(i + 1)]
    else:
        pairs = [(j, i) for j in range(nq) for i in range(nq - 1, j - 1, -1)]
    pairs.append(pairs[-1])
    jj, ii = np.array(pairs, np.int32).T
    return jnp.asarray(jj), jnp.asarray(ii), len(pairs) - 1


ATTN_TRIP = 8


def _walk_tiles(n, products, tile, flush, buf_a, buf_b):
    bufs = (buf_a, buf_b)
    products(0, buf_a)

    def trip(r, carry):
        for u in range(ATTN_TRIP):
            products(ATTN_TRIP * r + u + 1, bufs[(u + 1) % 2])
            tile(ATTN_TRIP * r + u, bufs[u % 2])
        for u in range(ATTN_TRIP):
            flush(ATTN_TRIP * r + u)
        return carry

    lax.fori_loop(0, n // ATTN_TRIP, trip, 0)
    rest = n - n % ATTN_TRIP
    for u in range(n % ATTN_TRIP):
        if rest + u + 1 < n:
            products(rest + u + 1, bufs[(u + 1) % 2])
        tile(rest + u, bufs[u % 2])
    for u in range(n % ATTN_TRIP):
        flush(rest + u)


V_ROWS = V_DIM + 16


def _attn_fwd_flat(k, q_t, v_t, tq):
    T = k.shape[0]
    nq = T // tq
    jj, ii, n = _tile_tables(nq, True)
    heads = [slice(hh * HEAD_LANES, (hh + 1) * HEAD_LANES) for hh in range(2)]

    def body(jj_ref, ii_ref, k_ref, qt_ref, vt_ref, o_ref, lse_ref, sa_ref, sb_ref, m_ref, acc_ref, bias_ref):
        def reset(st):
            m_ref[st] = jnp.full(m_ref.shape[1:], MASKED, F32)
            acc_ref[st] = jnp.zeros(acc_ref.shape[1:], F32)

        _causal_bias(bias_ref, tq)
        for st in range(ATTN_TRIP):
            reset(st)
        extra = (lax.broadcasted_iota(jnp.int32, (V_ROWS - V_DIM, tq), 0) == 0).astype(BF16)

        def products(t, buf):
            j, i = jj_ref[t], ii_ref[t]
            kj = k_ref[pl.ds(pl.multiple_of(j * tq, tq), tq), :]
            for hh, sl in enumerate(heads):
                buf[hh] = _dot(kj[:, sl], qt_ref[i, sl, :])

        def tile(t, buf):
            j, i = jj_ref[t], ii_ref[t]
            vt = vt_ref[j]
            bias = bias_ref.at[(j == i).astype(jnp.int32)]
            st = i % ATTN_TRIP
            for hh in range(2):
                s = buf[hh] * SCALE2 + bias[...]
                m = m_ref[st, hh]
                m_new = jnp.maximum(m, jnp.max(s, axis=0, keepdims=True))
                alpha = jnp.exp2(m - m_new)
                p = jnp.exp2(s - m_new)
                m_ref[st, hh] = m_new
                v_h = jnp.concatenate([vt[hh * V_DIM:(hh + 1) * V_DIM, :], extra], axis=0)
                acc_ref[st, hh] = alpha * acc_ref[st, hh] + _dot(v_h, p.astype(BF16))

        def flush(t):
            j, i = jj_ref[t], ii_ref[t]

            @pl.when(j == i)
            def _():
                st = i % ATTN_TRIP
                den = [acc_ref[st, hh, V_DIM:V_DIM + 1, :] for hh in range(2)]
                out = jnp.concatenate([acc_ref[st, hh, :V_DIM, :] / den[hh] for hh in range(2)], axis=0)
                o_ref[pl.ds(pl.multiple_of(i * tq, tq), tq), :] = out.T
                for hh in range(2):
                    lse_ref[hh, i] = m_ref[st, hh] + jnp.log2(den[hh])
                reset(st)

        _walk_tiles(n, products, tile, flush, sa_ref, sb_ref)

    smem = pl.BlockSpec(memory_space=pltpu.SMEM)
    return pl.pallas_call(
        body, name="attn_fwd", grid=(N_HEADS // 2,),
        in_specs=[smem, smem,
                  pl.BlockSpec((T, 2 * HEAD_LANES), lambda p: (0, p)),
                  pl.BlockSpec((nq, 2 * HEAD_LANES, tq), lambda p: (0, p, 0)),
                  pl.BlockSpec((nq, LANES, tq), lambda p: (0, p, 0))],
        out_specs=[pl.BlockSpec((T, LANES), lambda p: (0, p)),
                   pl.BlockSpec((2, nq, 1, tq), lambda p: (p, 0, 0, 0))],
        out_shape=[jax.ShapeDtypeStruct((T, MLA_WIDTH), F32), jax.ShapeDtypeStruct((N_HEADS, nq, 1, tq), F32)],
        scratch_shapes=[pltpu.VMEM((2, tq, tq), F32), pltpu.VMEM((2, tq, tq), F32),
                        pltpu.VMEM((ATTN_TRIP, 2, 1, tq), F32),
                        pltpu.VMEM((ATTN_TRIP, 2, V_ROWS, tq), F32), pltpu.VMEM((2, tq, tq), F32)],
        compiler_params=_params(1),
    )(jj, ii, k, q_t, v_t)


def _attn_bwd_flat(k, v, q_t, k_t, do_t, lse, dsum, tq):
    T = k.shape[0]
    nq = T // tq
    jj, ii, n = _tile_tables(nq, False)
    heads = [slice(hh * HEAD_LANES, (hh + 1) * HEAD_LANES) for hh in range(2)]

    def body(jj_ref, ii_ref, k_ref, v_ref, qt_ref, kt_ref, dot_ref, lse_ref, dsum_ref, dqt_ref, dk_ref, dv_ref,
             ba_ref, bb_ref, dkt_ref, dvt_ref, bias_ref):
        _causal_bias(bias_ref, tq)
        dqt_ref[...] = jnp.zeros_like(dqt_ref)
        dkt_ref[...] = jnp.zeros_like(dkt_ref)
        dvt_ref[...] = jnp.zeros_like(dvt_ref)

        def products(t, buf):
            j, i = jj_ref[t], ii_ref[t]
            rows = pl.ds(pl.multiple_of(j * tq, tq), tq)
            for hh, sl in enumerate(heads):
                buf[hh] = _dot(k_ref[rows, sl], qt_ref[i, sl, :])
                buf[2 + hh] = _dot(v_ref[rows, :], dot_ref[i, sl, :])

        def tile(t, buf):
            j, i = jj_ref[t], ii_ref[t]
            bias = bias_ref.at[(j == i).astype(jnp.int32)]
            st = j % ATTN_TRIP
            for hh, sl in enumerate(heads):
                p = jnp.exp2(buf[hh] * SCALE2 + bias[...] - lse_ref[hh, i])
                ds = (p * (buf[2 + hh] - dsum_ref[hh, i]) * SCALE).astype(BF16)
                own = slice(hh * V_DIM, (hh + 1) * V_DIM)
                do_h = dot_ref[i, hh * HEAD_LANES + own.start:hh * HEAD_LANES + own.stop, :]
                dvt_ref[st, own, :] += _dot_nt(do_h, p.astype(BF16))
                used = slice(sl.start, sl.start + NOPE + ROPE)
                dkt_ref[st, used, :] += _dot_nt(qt_ref[i, used, :], ds)
                dqt_ref[i, used, :] += _dot(kt_ref[j, used, :], ds)

        def flush(t):
            j, i = jj_ref[t], ii_ref[t]

            @pl.when(j == i)
            def _():
                st = j % ATTN_TRIP
                rows = pl.ds(pl.multiple_of(j * tq, tq), tq)
                dk_ref[rows, :] = dkt_ref[st].T
                dv_ref[rows, :] = dvt_ref[st].T
                dkt_ref[st] = jnp.zeros(dkt_ref.shape[1:], F32)
                dvt_ref[st] = jnp.zeros(dvt_ref.shape[1:], F32)

        _walk_tiles(n, products, tile, flush, ba_ref, bb_ref)

    smem = pl.BlockSpec(memory_space=pltpu.SMEM)
    stat = pl.BlockSpec((2, nq, 1, tq), lambda p: (p, 0, 0, 0))
    blocks_t = pl.BlockSpec((nq, 2 * HEAD_LANES, tq), lambda p: (0, p, 0))
    return pl.pallas_call(
        body, name="attn_bwd", grid=(N_HEADS // 2,),
        in_specs=[smem, smem,
                  pl.BlockSpec((T, 2 * HEAD_LANES), lambda p: (0, p)),
                  pl.BlockSpec((T, LANES), lambda p: (0, p)),
                  blocks_t, blocks_t, blocks_t, stat, stat],
        out_specs=[blocks_t,
                   pl.BlockSpec((T, 2 * HEAD_LANES), lambda p: (0, p)),
                   pl.BlockSpec((T, LANES), lambda p: (0, p))],
        out_shape=[jax.ShapeDtypeStruct((nq, N_HEADS * HEAD_LANES, tq), F32),
                   jax.ShapeDtypeStruct((T, N_HEADS * HEAD_LANES), F32),
                   jax.ShapeDtypeStruct((T, MLA_WIDTH), F32)],
        scratch_shapes=[pltpu.VMEM((4, tq, tq), F32), pltpu.VMEM((4, tq, tq), F32),
                        pltpu.VMEM((ATTN_TRIP, 2 * HEAD_LANES, tq), F32), pltpu.VMEM((ATTN_TRIP, LANES, tq), F32),
                        pltpu.VMEM((2, tq, tq), F32)],
        compiler_params=_params(1),
    )(jj, ii, k, v, q_t, k_t, do_t, lse, dsum)


def _lower_bound(lbp):
    a, b = lbp[0:1, :], lbp[1:2, :]
    mx = jnp.maximum(a, b)
    ea, eb = jnp.exp(a - mx), jnp.exp(b - mx)
    return ea / (ea + eb)


def _tri(lower):
    r = lax.broadcasted_iota(jnp.int32, (CHUNK, CHUNK), 0)
    c = lax.broadcasted_iota(jnp.int32, (CHUNK, CHUNK), 1)
    return (c <= r) if lower else (c >= r)


def _running_sum(x, from_end):
    row = lax.broadcasted_iota(jnp.int32, x.shape, 0)
    step = 1
    while step < CHUNK:
        if from_end:
            x = x + jnp.where(row < CHUNK - step, pltpu.roll(x, CHUNK - step, 0), 0.0)
        else:
            x = x + jnp.where(row >= step, pltpu.roll(x, step, 0), 0.0)
        step *= 2
    return x


def _hg_gates(hq, hf, lb):
    sq = _sigmoid(hq)
    sf = _sigmoid(hf)
    f = lb + (1.0 - lb) * sf
    g = jnp.log(f)
    gcum = _running_sum(g, False)
    return sq, sf, f, hq * sq, 1.0 - f, gcum


def _head(x, hd):
    return x[:, hd * HG_DIM:(hd + 1) * HG_DIM]


def _all_heads(fn):
    return jnp.concatenate([fn(hd) for hd in range(HG_HEADS)], axis=1)


def _hg_blocks(q, kk, gcum):
    rowi = lax.broadcasted_iota(jnp.int32, gcum.shape, 0)
    out = []
    for blk in range(CHUNK // SUB):
        lo, hi = blk * SUB, (blk + 1) * SUB
        gb = gcum[lo - 1:lo, :] if blk else jnp.zeros_like(gcum[0:1, :])
        eq = jnp.exp(gcum[lo:hi, :] - gb)
        ek = jnp.exp(jnp.where(rowi < hi, gb - gcum, 0.0))
        out.append((eq, ek, (q[lo:hi, :] * eq).astype(BF16), (kk * ek).astype(BF16)))
    return out


def _hg_scores(blocks):
    out = []
    for hd in range(HG_HEADS):
        a = jnp.concatenate([_dot_nt(_head(qb, hd), _head(kb, hd)) for _, _, qb, kb in blocks], axis=0)
        out.append(jnp.where(_tri(True), a, 0.0))
    return out


HG_STEP_CHUNKS = 8


def _hgrn_fwd(proj, lbp):
    T = proj.shape[0]
    nc = T // CHUNK
    ns = min(HG_STEP_CHUNKS, nc)
    rows = ns * CHUNK

    def body(hq_ref, hf_ref, hi_ref, lbp_ref, o_ref, st_ref, state):
        @pl.when(pl.program_id(0) == 0)
        def _():
            state[...] = jnp.zeros_like(state)

        lb = _lower_bound(lbp_ref[...])
        work = []
        for c in range(ns):
            r = slice(c * CHUNK, (c + 1) * CHUNK)
            _, _, _, q, kk, gcum = _hg_gates(hq_ref[r, :], hf_ref[r, :], lb)
            vb = hi_ref[r, :].astype(BF16)
            a = _hg_scores(_hg_blocks(q, kk, gcum))
            gend = gcum[CHUNK - 1:CHUNK, :]
            qgb = (q * jnp.exp(gcum)).astype(BF16)
            kgeb = (kk * jnp.exp(gend - gcum)).astype(BF16)
            intra = [_dot(a[hd].astype(BF16), _head(vb, hd)) for hd in range(HG_HEADS)]
            update = [_dot_tn(_head(vb, hd), _head(kgeb, hd)) for hd in range(HG_HEADS)]
            work.append((qgb, jnp.exp(gend), intra, update))
        for hd in range(HG_HEADS):
            st = state[hd]
            for c, (qgb, egend, intra, update) in enumerate(work):
                st_ref[c, hd] = st
                o_ref[c * CHUNK:(c + 1) * CHUNK, hd * HG_DIM:(hd + 1) * HG_DIM] = (
                    intra[hd] + _dot_nt(_head(qgb, hd), st.astype(BF16)))
                st = st * _head(egend, hd) + update[hd]
            state[hd] = st

    def col(cb):
        return pl.BlockSpec((rows, HG_WIDTH), lambda i: (i, cb))

    return pl.pallas_call(
        body, name="hgrn_fwd", grid=(nc // ns,),
        in_specs=[col(P_HQ // HG_WIDTH), col(P_HF // HG_WIDTH), col(P_HI // HG_WIDTH), _full((2, HG_WIDTH))],
        out_specs=[pl.BlockSpec((rows, HG_WIDTH), lambda i: (i, 0)),
                   pl.BlockSpec((ns, HG_HEADS, HG_DIM, HG_DIM), lambda i: (i, 0, 0, 0))],
        out_shape=[jax.ShapeDtypeStruct((T, HG_WIDTH), F32),
                   jax.ShapeDtypeStruct((nc, HG_HEADS, HG_DIM, HG_DIM), F32)],
        scratch_shapes=[pltpu.VMEM((HG_HEADS, HG_DIM, HG_DIM), F32)],
        compiler_params=_params(1),
    )(proj, proj, proj, lbp)


def _hgrn_bwd(proj, lbp, do_hg, states):
    T = proj.shape[0]
    nc = T // CHUNK
    ns = min(HG_STEP_CHUNKS, nc)
    rows = ns * CHUNK
    steps = nc // ns

    def body(hq_ref, hf_ref, hi_ref, lbp_ref, do_ref, st_ref, dhq_ref, dhf_ref, dhi_ref, dlb_ref, dstate):
        @pl.when(pl.program_id(0) == 0)
        def _():
            dstate[...] = jnp.zeros_like(dstate)
            dlb_ref[...] = jnp.zeros_like(dlb_ref)

        lb = _lower_bound(lbp_ref[...])

        dst_all = [dstate[hd] for hd in range(HG_HEADS)]
        dlb = jnp.zeros_like(lb)
        last = lax.broadcasted_iota(jnp.int32, (CHUNK, HG_WIDTH), 0) == CHUNK - 1
        for c in reversed(range(ns)):
            r = slice(c * CHUNK, (c + 1) * CHUNK)
            hq = hq_ref[r, :]
            sq, sf, f, q, kk, gcum = _hg_gates(hq, hf_ref[r, :], lb)
            vb = hi_ref[r, :].astype(BF16)
            dob = do_ref[r, :].astype(BF16)
            blocks = _hg_blocks(q, kk, gcum)
            a = _hg_scores(blocks)
            gend = gcum[CHUNK - 1:CHUNK, :]
            eg, egend, ekend = jnp.exp(gcum), jnp.exp(gend), jnp.exp(gend - gcum)
            qg, kge = q * eg, kk * ekend
            qgb, kgeb = qg.astype(BF16), kge.astype(BF16)

            dv, dqg, dkge, st_dst, dq_blk, dk_blk = [], [], [], [], [], []
            for hd in range(HG_HEADS):
                st = st_ref[c, hd]
                dst = dst_all[hd]
                dstb = dst.astype(BF16)
                do_h, v_h = _head(dob, hd), _head(vb, hd)
                dv.append(_dot_tn(a[hd].astype(BF16), do_h) + _dot_nt(_head(kgeb, hd), dstb))
                da = jnp.where(_tri(True), _dot_nt(do_h, v_h), 0.0).astype(BF16)
                dqg.append(_dot(do_h, st.astype(BF16)))
                dkge.append(_dot(v_h, dstb))
                st_dst.append(jnp.sum(st * dst, axis=0, keepdims=True))
                dst_all[hd] = _dot_tn(do_h, _head(qgb, hd)) + dst * _head(egend, hd)
                dq_blk.append([_dot(da[b * SUB:(b + 1) * SUB, :], _head(kb, hd)) for b, (_, _, _, kb) in enumerate(blocks)])
                dk_blk.append([_dot_tn(da[b * SUB:(b + 1) * SUB, :], _head(qb, hd)) for b, (_, _, qb, _) in enumerate(blocks)])
            dv, dqg, dkge, st_dst = (jnp.concatenate(t, axis=1) for t in (dv, dqg, dkge, st_dst))

            dq_a, dg_q = [], []
            dk_a, dg_k = jnp.zeros_like(gcum), jnp.zeros_like(gcum)
            for b, (eq, ek, qb, kb) in enumerate(blocks):
                dq_b = _all_heads(lambda hd: dq_blk[hd][b])
                dk_b = _all_heads(lambda hd: dk_blk[hd][b])
                dq_a.append(dq_b * eq)
                dk_a = dk_a + dk_b * ek
                dg_q.append(qb.astype(F32) * dq_b)
                dg_k = dg_k + kb.astype(F32) * dk_b
            dq_a = jnp.concatenate(dq_a, axis=0)

            dgend = st_dst * egend + jnp.sum(dkge * kge, axis=0, keepdims=True)
            dq = dq_a + dqg * eg
            dk = dk_a + dkge * ekend
            dgc = jnp.concatenate(dg_q, axis=0) - dg_k + dqg * qg - dkge * kge + jnp.where(last, dgend, 0.0)
            dg = _running_sum(dgc, True)
            df = dg / f - dk
            dhf_ref[r, :] = df * (1.0 - lb) * sf * (1.0 - sf)
            dlb = dlb + jnp.sum(df * (1.0 - sf), axis=0, keepdims=True)
            dhq_ref[r, :] = dq * (sq * (1.0 + hq * (1.0 - sq)))
            dhi_ref[r, :] = dv
        for hd in range(HG_HEADS):
            dstate[hd] = dst_all[hd]
        dlb_ref[...] += dlb

    def col(cb):
        return pl.BlockSpec((rows, HG_WIDTH), lambda i: (steps - 1 - i, cb))

    grad = jax.ShapeDtypeStruct((T, HG_WIDTH), F32)
    return pl.pallas_call(
        body, name="hgrn_bwd", grid=(steps,),
        in_specs=[col(P_HQ // HG_WIDTH), col(P_HF // HG_WIDTH), col(P_HI // HG_WIDTH), _full((2, HG_WIDTH)),
                  col(0), pl.BlockSpec((ns, HG_HEADS, HG_DIM, HG_DIM), lambda i: (steps - 1 - i, 0, 0, 0))],
        out_specs=[col(0), col(0), col(0), _full((1, HG_WIDTH))],
        out_shape=[grad, grad, grad, jax.ShapeDtypeStruct((1, HG_WIDTH), F32)],
        scratch_shapes=[pltpu.VMEM((HG_HEADS, HG_DIM, HG_DIM), F32)],
        compiler_params=_params(1),
    )(proj, proj, proj, lbp, do_hg, states)


def _top(x, tgt, o_mla, o_hg, proj, w_out, hg_norm_g, final_g, tm, tq):
    T = x.shape[0]
    assert tm % tq == 0

    def body(x_ref, tgt_ref, om_ref, oh_ref, gm_ref, gh_ref, wout_ref, hgn_ref, fng_ref,
             dx2_ref, dot_ref, dsum_ref, dgm_ref, doh_ref, dgh_ref, loss_ref, dfng_ref, dhgn_ref, dwout_ref, ycat_ref):
        @pl.when(pl.program_id(0) == 0)
        def _():
            for ref in (loss_ref, dfng_ref, dhgn_ref, dwout_ref):
                ref[...] = jnp.zeros_like(ref)

        gm, om = gm_ref[...], om_ref[...]
        sgm = _sigmoid(gm)
        silu_m = gm * sgm
        gh, oh, gam = gh_ref[...], oh_ref[...], hgn_ref[...]
        sgh = _sigmoid(gh)
        silu_h = gh * sgh
        rr, nn = [], []
        for hd in range(HG_HEADS):
            oh_h = oh[:, hd * HG_DIM:(hd + 1) * HG_DIM]
            r_h = lax.rsqrt(jnp.mean(oh_h * oh_h, axis=-1, keepdims=True) + EPS)
            rr.append(r_h)
            nn.append(oh_h * r_h)
        n = jnp.concatenate(nn, axis=1)
        ng = n * gam
        ycat_ref[:, :MLA_WIDTH] = (om * silu_m).astype(BF16)
        ycat_ref[:, MLA_WIDTH:] = (ng * silu_h).astype(BF16)
        wout = wout_ref[...]
        x2 = x_ref[...] + _dot(ycat_ref[...], wout)
        r = lax.rsqrt(jnp.mean(x2 * x2, axis=-1, keepdims=True) + EPS)
        xh = x2 * r
        fng = fng_ref[...]
        err = xh * fng - tgt_ref[...]
        loss_ref[...] += 0.5 * jnp.sum(jnp.mean(err * err, axis=-1, keepdims=True), axis=0, keepdims=True)
        dout = err * (1.0 / D_MODEL)
        dfng_ref[...] += jnp.sum(dout * xh, axis=0, keepdims=True)
        dxh = dout * fng
        dx2 = r * (dxh - xh * jnp.mean(dxh * xh, axis=-1, keepdims=True))
        dx2_ref[...] = dx2
        dx2b = dx2.astype(BF16)
        dwout_ref[...] += _dot_tn(ycat_ref[...], dx2b)
        dycat = _dot_nt(dx2b, wout)
        dym, dyh = dycat[:, :MLA_WIDTH], dycat[:, MLA_WIDTH:]
        dom = dym * silu_m
        first = lax.broadcasted_iota(jnp.int32, (tm, LANES), 1) < V_DIM
        for hd in range(N_HEADS):
            pair = dom[:, hd // 2 * LANES:(hd // 2 + 1) * LANES]
            own = jnp.where(first, pair, 0.0) if hd % 2 == 0 else jnp.where(first, 0.0, pair)
            for b in range(tm // tq):
                dot_ref[b, hd * HEAD_LANES:(hd + 1) * HEAD_LANES, :] = own[b * tq:(b + 1) * tq, :].T.astype(BF16)
        head_of = lax.broadcasted_iota(jnp.int32, (MLA_WIDTH, LANES), 0) // V_DIM
        pick = (head_of == lax.broadcasted_iota(jnp.int32, (MLA_WIDTH, LANES), 1)).astype(F32)
        dsum_ref[...] = jnp.dot(dom * om, pick, precision=lax.Precision.HIGHEST, preferred_element_type=F32)
        dgm_ref[...] = dym * om * (sgm * (1.0 + gm * (1.0 - sgm)))
        dgh_ref[...] = dyh * ng * (sgh * (1.0 + gh * (1.0 - sgh)))
        dng = dyh * silu_h
        dhgn_ref[...] += jnp.sum(dng * n, axis=0, keepdims=True)
        dn = dng * gam
        for hd in range(HG_HEADS):
            sl = slice(hd * HG_DIM, (hd + 1) * HG_DIM)
            dn_h, n_h = dn[:, sl], nn[hd]
            doh_ref[:, sl] = rr[hd] * (dn_h - n_h * jnp.mean(dn_h * n_h, axis=-1, keepdims=True))

    def row(w, cb=0):
        return pl.BlockSpec((tm, w), lambda i: (i, cb))

    hl = N_HEADS * HEAD_LANES
    blocks_t = pl.BlockSpec((tm // tq, hl, tq), lambda i: (i, 0, 0))
    outs = [(D_MODEL, F32), (0, BF16), (LANES, F32), (MLA_WIDTH, F32), (HG_WIDTH, F32), (HG_WIDTH, F32)]
    small = [(1, SMALL_W), (1, D_MODEL), (1, HG_WIDTH), (D_MODEL, D_MODEL)]
    return pl.pallas_call(
        body, name="top", grid=(T // tm,),
        in_specs=[row(D_MODEL), row(D_MODEL), row(MLA_WIDTH), row(HG_WIDTH),
                  row(MLA_WIDTH, P_GM // MLA_WIDTH), row(HG_WIDTH, P_GH // HG_WIDTH),
                  _full((D_MODEL, D_MODEL)), _full((1, HG_WIDTH)), _full((1, D_MODEL))],
        out_specs=[row(w) if w else blocks_t for w, _ in outs] + [_full(s) for s in small],
        out_shape=[jax.ShapeDtypeStruct((T, w) if w else (T // tq, hl, tq), dt) for w, dt in outs]
        + [jax.ShapeDtypeStruct(s, F32) for s in small],
        scratch_shapes=[pltpu.VMEM((tm, D_MODEL), BF16)],
        compiler_params=_params(1),
    )(x, tgt, o_mla, o_hg, proj, proj, w_out, hg_norm_g, final_g)


def _bot(x, dx2, proj, h, qn, kvn, dq, dk, dv, dgm, dhq, dhf, dhi, dgh, c_t, s1_t, s2_t, w_in_t, w_q_p, w_kv_p, ln_g, q_g, kv_g, tm):
    T = x.shape[0]
    lat_w = D_PERM - P_QL
    steps = T // tm

    def body(x_ref, dx2_ref, lat_ref, h_ref, qn_ref, kvn_ref, dq_ref, dk_ref, dv_ref, dgm_ref, dhq_ref, dhf_ref,
             dhi_ref, dgh_ref, c_ref, s1_ref, s2_ref, win_ref, wq_ref, wkv_ref, lng_ref, qg_ref, kvg_ref,
             dx_ref, dlng_ref, dqg_ref, dkvg_ref, dwq_ref, dwkv_ref, dwin_ref, dqpre_ref, dkv_ref, dproj_ref,
             dwin_acc, sem):
        @pl.when(pl.program_id(0) == 0)
        def _():
            for ref in (dlng_ref, dqg_ref, dkvg_ref, dwq_ref, dwkv_ref, dwin_acc):
                ref[...] = jnp.zeros_like(ref)

        c, s1, s2 = c_ref[...], s1_ref[...], s2_ref[...]
        dkpe = jnp.zeros((tm, LANES), F32)
        for hd in range(N_HEADS):
            sl = slice(hd * HEAD_LANES, (hd + 1) * HEAD_LANES)
            dqpre_ref[:, sl] = _rope_bwd(dq_ref[:, sl], c, s1, s2).astype(BF16)
            dk_h = dk_ref[:, sl]
            dkpe = dkpe + dk_h
            dkv_ref[:, sl] = dk_h.astype(BF16)
        dkv_ref[:, N_HEADS * HEAD_LANES:] = dv_ref[...].astype(BF16)
        lane = lax.broadcasted_iota(jnp.int32, (tm, LANES), 1)
        rope_lanes = jnp.logical_and(lane >= ROPE_LO, lane < ROPE_LO + ROPE)
        dkr = jnp.where(rope_lanes, _rope_bwd(dkpe, c, s1, s2), 0.0)

        def norm_bwd(v, g, dy):
            r = lax.rsqrt(jnp.mean(v * v, axis=-1, keepdims=True) + EPS)
            vh = v * r
            dvh = dy * g
            return jnp.sum(dy * vh, axis=0, keepdims=True), r * (dvh - vh * jnp.mean(dvh * vh, axis=-1, keepdims=True))

        dwq_ref[...] += _dot_tn(qn_ref[...], dqpre_ref[...])
        dwkv_ref[...] += _dot_tn(kvn_ref[...], dkv_ref[...])
        dqn = _dot_nt(dqpre_ref[...], wq_ref[...])
        dg_q, dql = norm_bwd(lat_ref[:, :Q_RANK], qg_ref[...], dqn)
        dqg_ref[...] += dg_q
        dkn = _dot_nt(dkv_ref[...], wkv_ref[...])
        dg_kv, dkvl = norm_bwd(lat_ref[:, Q_RANK:Q_RANK + KV_RANK], kvg_ref[...], dkn)
        dkvg_ref[...] += dg_kv

        dproj_ref[:, P_GM:P_GM + MLA_WIDTH] = dgm_ref[...].astype(BF16)
        dproj_ref[:, P_HQ:P_HQ + HG_WIDTH] = dhq_ref[...].astype(BF16)
        dproj_ref[:, P_HF:P_HF + HG_WIDTH] = dhf_ref[...].astype(BF16)
        dproj_ref[:, P_HI:P_HI + HG_WIDTH] = dhi_ref[...].astype(BF16)
        dproj_ref[:, P_GH:P_GH + HG_WIDTH] = dgh_ref[...].astype(BF16)
        dproj_ref[:, P_QL:P_QL + Q_RANK] = dql.astype(BF16)
        dproj_ref[:, P_KVL:P_KVL + KV_RANK] = dkvl.astype(BF16)
        dproj_ref[:, P_KR:P_KR + LANES] = dkr.astype(BF16)
        dh = sum(_dot(dproj_ref[:, col:col + rows.shape[0]], rows) for col, rows in _in_proj_rows(win_ref))
        dg_ln, dxn = norm_bwd(x_ref[...], lng_ref[...], dh)
        dlng_ref[...] += dg_ln
        dx_ref[...] = dx2_ref[...] + dxn
        dwin_acc[...] += _dot_tn(dproj_ref[...], h_ref[...])

        @pl.when(pl.program_id(0) == steps - 1)
        def _():
            kr = P_KR + ROPE_LO
            moves = [((P_GM, P_QL), R_MAIN), ((P_QL, P_KR), (R_QL[0], R_KVL[1])), ((kr, kr + ROPE), R_KR)]
            copies = [pltpu.make_async_copy(dwin_acc.at[a:b, :], dwin_ref.at[c:d, :], sem.at[n])
                      for n, ((a, b), (c, d)) in enumerate(moves)]
            for cp in copies:
                cp.start()
            for cp in copies:
                cp.wait()

    def row(w, cb=0):
        return pl.BlockSpec((tm, w), lambda i: (i, cb))

    hl = N_HEADS * HEAD_LANES
    outs = [(D_MODEL, F32)]
    small = [(1, D_MODEL), (1, Q_RANK), (1, KV_RANK), (Q_RANK, hl), (KV_RANK, hl + MLA_WIDTH)]
    return pl.pallas_call(
        body, name="bot", grid=(steps,),
        in_specs=[row(D_MODEL), row(D_MODEL), row(lat_w, P_QL // lat_w), row(D_MODEL), row(Q_RANK), row(KV_RANK),
                  row(hl), row(hl), row(MLA_WIDTH),
                  row(MLA_WIDTH), row(HG_WIDTH), row(HG_WIDTH), row(HG_WIDTH), row(HG_WIDTH),
                  row(LANES), row(LANES), row(LANES),
                  _full((D_IN, D_MODEL)), _full((Q_RANK, hl)), _full((KV_RANK, hl + MLA_WIDTH)),
                  _full((1, D_MODEL)), _full((1, Q_RANK)), _full((1, KV_RANK))],
        out_specs=[row(w) for w, _ in outs] + [_full(s) for s in small] + [pl.BlockSpec(memory_space=pl.ANY)],
        out_shape=[jax.ShapeDtypeStruct((T, w), dt) for w, dt in outs] + [jax.ShapeDtypeStruct(s, F32) for s in small]
        + [jax.ShapeDtypeStruct((D_IN, D_MODEL), F32)],
        scratch_shapes=[pltpu.VMEM((tm, hl), BF16), pltpu.VMEM((tm, hl + MLA_WIDTH), BF16),
                        pltpu.VMEM((tm, D_PERM), BF16), pltpu.VMEM((D_PERM, D_MODEL), F32),
                        pltpu.SemaphoreType.DMA((3,))],
        compiler_params=_params(1),
    )(x, dx2, proj, h, qn, kvn, dq, dk, dv, dgm, dhq, dhf, dhi, dgh, c_t, s1_t, s2_t, w_in_t, w_q_p, w_kv_p, ln_g,
      q_g, kv_g)


RS_ROWS = 256


def _reduce_scatter(slabs, small):
    n = len(slabs)
    units = []
    for a, s in enumerate(slabs):
        rows, cols = s.shape[1:]
        if rows % RS_ROWS == 0 or rows < RS_ROWS:
            units += [(a, (pl.ds(r0, min(rows, RS_ROWS)), slice(None))) for r0 in range(0, rows, RS_ROWS)]
        else:
            units += [(a, (slice(None), pl.ds(c0, RS_ROWS))) for c0 in range(0, cols, RS_ROWS)]
    nu = len(units)

    def body(*refs):
        ins, small_ref = refs[:n], refs[n]
        outs, small_out = refs[n + 1:2 * n + 1], refs[2 * n + 1]
        own, sib_land, ici_out, ici_land = (refs[(2 + g) * n + 2:(3 + g) * n + 2] for g in range(4))
        small_land = refs[6 * n + 2]
        loc_sems, d2d_send, d2d_recv, ici_send, ici_recv, sm_send, sm_recv = refs[6 * n + 3:6 * n + 10]
        x, y, c = lax.axis_index("x"), lax.axis_index("y"), lax.axis_index("c")
        me = 4 * x + 2 * y + c

        def chip(k):
            return (1 - x if k & 2 else x, 1 - y if k & 1 else y)

        def block(k, core):
            px, py = chip(k)
            return 4 * px + 2 * py + core

        def part(u):
            return units[u]

        def local(u, k):
            a, rows = part(u)
            return pltpu.make_async_copy(ins[a].at[(block(k, c),) + rows], own[a].at[(k,) + rows], loc_sems.at[u, k])

        def to_sibling(u, k):
            a, rows = part(u)
            return pltpu.make_async_remote_copy(
                src_ref=ins[a].at[(block(k, 1 - c),) + rows], dst_ref=sib_land[a].at[(k,) + rows],
                send_sem=d2d_send.at[u, k], recv_sem=d2d_recv.at[u, k], device_id=(x, y, 1 - c), device_id_type=MESH)

        def to_chip(u, k):
            a, rows = part(u)
            return pltpu.make_async_remote_copy(
                src_ref=ici_out[a].at[(k - 1,) + rows], dst_ref=ici_land[a].at[(k - 1,) + rows],
                send_sem=ici_send.at[u, k - 1], recv_sem=ici_recv.at[u, k - 1], device_id=(*chip(k), c),
                device_id_type=MESH)

        def small_copy(k, receiving):
            px, py = chip(k >> 1)
            pc = 1 - c if k & 1 else c
            slot = 4 * px + 2 * py + pc if receiving else me
            return pltpu.make_async_remote_copy(
                src_ref=small_ref, dst_ref=small_land.at[slot], send_sem=sm_send.at[k - 1], recv_sem=sm_recv.at[k - 1],
                device_id=(px, py, pc), device_id_type=MESH)

        for u in range(nu):
            for k in range(4):
                local(u, k).start()
        for u in range(nu):
            for k in range(4):
                to_sibling(u, k).start()
        small_land[me] = small_ref[...]
        for k in range(1, N_DEV):
            small_copy(k, False).start()
        for u in range(nu):
            a, rows = part(u)
            for k in range(4):
                local(u, k).wait()
                to_sibling(u, k).wait_recv()
            for k in range(1, 4):
                ici_out[a][(k - 1,) + rows] = (own[a][(k,) + rows] + sib_land[a][(k,) + rows]).astype(BF16)
                to_chip(u, k).start()
        for u in range(nu):
            a, rows = part(u)
            acc = own[a][(0,) + rows] + sib_land[a][(0,) + rows]
            for k in range(1, 4):
                to_chip(u, k).wait_recv()
                acc = acc + ici_land[a][(k - 1,) + rows].astype(F32)
            outs[a][rows] = acc
        for k in range(1, N_DEV):
            small_copy(k, True).wait_recv()
        acc = small_land[0]
        for d in range(1, N_DEV):
            acc = acc + small_land[d]
        small_out[...] = acc
        for u in range(nu):
            for k in range(4):
                to_sibling(u, k).wait_send()
            for k in range(1, 4):
                to_chip(u, k).wait_send()
        for k in range(1, N_DEV):
            small_copy(k, False).wait_send()

    vm = pl.BlockSpec(memory_space=pltpu.VMEM)
    hbm = pl.BlockSpec(memory_space=pl.ANY)
    dma = pltpu.SemaphoreType.DMA
    return pl.pallas_call(
        body, name="reduce_scatter_grads",
        in_specs=[hbm] * n + [vm], out_specs=[vm] * (n + 1),
        out_shape=[jax.ShapeDtypeStruct(s.shape[1:], F32) for s in slabs] + [jax.ShapeDtypeStruct(small.shape, F32)],
        scratch_shapes=[pltpu.VMEM((4,) + s.shape[1:], F32) for s in slabs] * 2
        + [pltpu.VMEM((3,) + s.shape[1:], BF16) for s in slabs] * 2
        + [pltpu.VMEM((N_DEV,) + small.shape, F32)]
        + [dma((nu, 4)), dma((nu, 4)), dma((nu, 4)), dma((nu, 3)), dma((nu, 3)), dma((N_DEV - 1,)), dma((N_DEV - 1,))],
        compiler_params=pltpu.CompilerParams(vmem_limit_bytes=VMEM_LIMIT),
    )(*slabs, small)


def _adamw_math(w, g, m, v):
    m = ADAM_B1 * m + (1.0 - ADAM_B1) * g
    v = ADAM_B2 * v + (1.0 - ADAM_B2) * (g * g)
    m_hat = m / (1.0 - ADAM_B1 ** ADAM_STEP)
    v_hat = v / (1.0 - ADAM_B2 ** ADAM_STEP)
    delta = -ADAM_LR * (m_hat / (jnp.sqrt(v_hat) + ADAM_EPS) + ADAM_WD * w)
    return delta, m, v


SMALL_W = 512


def _adamw(big, small_w, small_g):
    nb, ns = len(big), len(small_w)

    def body(*refs):
        k = 0
        big_in = [refs[4 * i:4 * i + 4] for i in range(nb)]
        k = 4 * nb
        small_in = [refs[k + 3 * i:k + 3 * i + 3] for i in range(ns)]
        k += 3 * ns
        sg_ref = refs[k]
        k += 1
        big_out = [refs[k + 3 * i:k + 3 * i + 3] for i in range(nb)]
        k += 3 * nb
        small_out = [refs[k + 4 * i:k + 4 * i + 4] for i in range(ns)]

        for (w, g, m, v), (od, om, ov) in zip(big_in, big_out):
            od[...], om[...], ov[...] = _adamw_math(w[...], g[...], m[...], v[...])

        sg = sg_ref[...]
        lbp = small_in[2][0][...]
        lb = _lower_bound(lbp)
        t = sg[4:5, :] * lb * (1.0 - lb)
        grads = [jnp.concatenate([sg[0:1, :], sg[1:2, :]], axis=1),
                 jnp.concatenate([sg[2:3, :], sg[3:4, :]], axis=1),
                 jnp.concatenate([t, -t], axis=0),
                 sg[6:7, :], sg[7:8, 0:Q_RANK], sg[7:8, Q_RANK:Q_RANK + KV_RANK]]
        for (w, m, v), g, (og, od, om, ov) in zip(small_in, grads, small_out):
            og[...] = g
            od[...], om[...], ov[...] = _adamw_math(w[...], g, m[...], v[...])

    ins = [a for grp in big for a in grp] + [a for grp in small_w for a in grp] + [small_g]
    out_shape = ([jax.ShapeDtypeStruct(grp[0].shape, F32) for grp in big for _ in range(3)]
                 + [jax.ShapeDtypeStruct(grp[0].shape, F32) for grp in small_w for _ in range(4)])
    vm = pl.BlockSpec(memory_space=pltpu.VMEM)
    res = pl.pallas_call(
        body, name="adamw", in_specs=[vm] * len(ins), out_specs=[vm] * len(out_shape), out_shape=out_shape,
        compiler_params=pltpu.CompilerParams(vmem_limit_bytes=VMEM_LIMIT),
    )(*ins)
    big_res = [res[3 * i:3 * i + 3] for i in range(nb)]
    small_res = [res[3 * nb + 4 * i:3 * nb + 4 * i + 4] for i in range(ns)]
    return big_res, small_res


def _perm_weights(g_in_t, g_q, g_kv, g_out):
    w_in_t = g_in_t.reshape(D_IN, D_MODEL)
    wq = g_q.transpose(1, 0, 2)
    w_q_p = jnp.pad(wq, ((0, 0), (0, 0), (0, HEAD_LANES - NOPE - ROPE))).reshape(Q_RANK, N_HEADS * HEAD_LANES)
    wkv = g_kv.transpose(1, 0, 2)
    wk = jnp.pad(wkv[:, :, :NOPE], ((0, 0), (0, 0), (0, HEAD_LANES - NOPE))).reshape(KV_RANK, N_HEADS * HEAD_LANES)
    wv = wkv[:, :, NOPE:].reshape(KV_RANK, MLA_WIDTH)
    return w_in_t, w_q_p, jnp.concatenate([wk, wv], axis=1), g_out.reshape(D_MODEL, D_MODEL)


def _grad_slabs(dw_in_t, dw_q_p, dw_kv_p, dw_out):
    s_in = dw_in_t.reshape(N_DEV, D_IN // N_DEV, D_MODEL)
    s_q = dw_q_p.reshape(Q_RANK, N_HEADS, HEAD_LANES)[:, :, :NOPE + ROPE].transpose(1, 0, 2)
    hl = N_HEADS * HEAD_LANES
    dk = dw_kv_p[:, :hl].reshape(KV_RANK, N_HEADS, HEAD_LANES)[:, :, :NOPE]
    dv = dw_kv_p[:, hl:].reshape(KV_RANK, N_HEADS, V_DIM)
    s_kv = jnp.concatenate([dk, dv], axis=2).transpose(1, 0, 2)
    return s_in, s_q, s_kv, dw_out.reshape(N_DEV, D_MODEL // N_DEV, D_MODEL)


def _block_sizes(T):
    return min(256, T), min(256, T), min(512, T)


def kernel(x, positions, ln_g, w_in, q_a_norm_g, w_q_b, kv_a_norm_g, w_kv_b, hg_lower_bounds, hg_norm_g, w_out, final_norm_g, loss_target, m_ln_g, m_w_in, m_q_a_norm_g, m_w_q_b, m_kv_a_norm_g, m_w_kv_b, m_hg_lower_bounds, m_hg_norm_g, m_w_out, m_final_norm_g, v_ln_g, v_w_in, v_q_a_norm_g, v_w_q_b, v_kv_a_norm_g, v_w_kv_b, v_hg_lower_bounds, v_hg_norm_g, v_w_out, v_final_norm_g):
    T = x.shape[1]
    tm, tq, bt = _block_sizes(T)
    nq = T // tq
    xs, tgt = x[0], loss_target[0]
    pos_f = positions.astype(F32)
    fng = final_norm_g.reshape(1, D_MODEL)

    w_in_shard_t = w_in[0].T
    gathered, (c_t, s1_t, s2_t) = _all_gather_weights([w_in_shard_t, w_q_b[0], w_kv_b[0], w_out[0]], pos_f)
    w_in_t, w_q_p, w_kv_p, w_out_b = _perm_weights(*gathered)

    proj, h, qn, kvn, q, k, v, v_t = _fwd_in(
        xs, ln_g, w_in_t, q_a_norm_g, w_q_p, kv_a_norm_g, w_kv_p, c_t, s1_t, s2_t, bt, tq)
    hl = N_HEADS * HEAD_LANES
    k_t = k.reshape(nq, tq, hl).transpose(0, 2, 1)
    q_t = q.reshape(nq, tq, hl).transpose(0, 2, 1)
    o_mla, lse = _attn_fwd_flat(k, q_t, v_t, tq)
    o_hg, states = _hgrn_fwd(proj, hg_lower_bounds)
    dx2, do_t, dsum, d_gm, d_oh, d_gh, loss_p, d_fng, d_hgn, dw_out = _top(
        xs, tgt, o_mla, o_hg, proj, w_out_b, hg_norm_g, fng, tm, tq)
    dsum = dsum[:, :N_HEADS].T.reshape(N_HEADS, nq, 1, tq)
    dq_t, dk, dv = _attn_bwd_flat(k, v, q_t, k_t, do_t, lse, dsum, tq)
    dq = dq_t.transpose(0, 2, 1).reshape(T, N_HEADS * HEAD_LANES)
    d_hq, d_hf, d_hi, d_lb = _hgrn_bwd(proj, hg_lower_bounds, d_oh, states)
    dx, d_lng, d_qg, d_kvg, dw_q_p, dw_kv_p, dw_in_t = _bot(
        xs, dx2, proj, h, qn, kvn, dq, dk, dv, d_gm, d_hq, d_hf, d_hi, d_gh, c_t, s1_t, s2_t, w_in_t, w_q_p, w_kv_p,
        ln_g, q_a_norm_g, kv_a_norm_g, tm)

    small = jnp.concatenate([
        d_lng.reshape(2, SMALL_W), d_fng.reshape(2, SMALL_W), d_lb, loss_p, d_hgn,
        jnp.concatenate([d_qg, d_kvg, jnp.zeros((1, SMALL_W - Q_RANK - KV_RANK), F32)], axis=1)], axis=0)
    g_in, g_q, g_kv, g_out, small_sum = _reduce_scatter(list(_grad_slabs(dw_in_t, dw_q_p, dw_kv_p, dw_out)), small)

    big = [(w_in_shard_t, g_in, m_w_in[0].T, v_w_in[0].T), (w_q_b[0], g_q, m_w_q_b[0], v_w_q_b[0]),
           (w_kv_b[0], g_kv, m_w_kv_b[0], v_w_kv_b[0]), (w_out[0], g_out, m_w_out[0], v_w_out[0])]
    small_w = [(ln_g, m_ln_g, v_ln_g),
               (fng, m_final_norm_g.reshape(1, D_MODEL), v_final_norm_g.reshape(1, D_MODEL)),
               (hg_lower_bounds, m_hg_lower_bounds, v_hg_lower_bounds), (hg_norm_g, m_hg_norm_g, v_hg_norm_g),
               (q_a_norm_g, m_q_a_norm_g, v_q_a_norm_g), (kv_a_norm_g, m_kv_a_norm_g, v_kv_a_norm_g)]
    big_res, small_res = _adamw(big, small_w, small_sum)

    loss = small_sum[5, 0]
    (r_in, r_q, r_kv, r_out) = big_res
    (s_ln, s_fn, s_lb, s_hgn, s_qg, s_kvg) = small_res
    flat = lambda t: t.reshape(D_MODEL)
    lead = lambda t: t[None]
    grads = [s_ln[0], lead(g_in.T), s_qg[0], lead(g_q), s_kvg[0], lead(g_kv), s_lb[0], s_hgn[0], lead(g_out), flat(s_fn[0])]

    def pick(i):
        return [s_ln[i + 1], lead(r_in[i].T), s_qg[i + 1], lead(r_q[i]), s_kvg[i + 1], lead(r_kv[i]), s_lb[i + 1],
                s_hgn[i + 1], lead(r_out[i]), flat(s_fn[i + 1])]

    return (loss, dx[None], *grads, *pick(0), *pick(1), *pick(2))
```

```python
import math

import numpy as np
import jax
import jax.numpy as jnp
from jax import lax
from jax.experimental import pallas as pl
from jax.experimental.pallas import tpu as pltpu

F32 = jnp.float32
BF16 = jnp.bfloat16

D_MODEL = 1024
N_HEADS = 8
NOPE = 64
ROPE = 32
HALF_ROPE = ROPE // 2
V_DIM = 64
Q_RANK = 256
KV_RANK = 128
MLA_WIDTH = N_HEADS * V_DIM
HG_HEADS = 4
HG_DIM = 128
HG_WIDTH = HG_HEADS * HG_DIM
CHUNK = 64
SUB = 16
D_IN = 2976
D_PERM = 3072
ROPE_THETA = 10000.0
EPS = 1e-6
N_DEV = 8
LANES = 128
HEAD_LANES = 128

P_GM, P_HQ, P_HF, P_HI, P_GH, P_QL, P_KVL, P_KR = 0, 512, 1024, 1536, 2048, 2560, 2816, 2944
R_QL, R_KVL, R_KR, R_MAIN = (0, 256), (256, 384), (384, 416), (416, 2976)
ROPE_LO = NOPE
SCALE = 1.0 / math.sqrt(NOPE + ROPE)

ADAM_LR = 0.001
ADAM_B1 = 0.9
ADAM_B2 = 0.999
ADAM_EPS = 1e-08
ADAM_WD = 0.01
ADAM_STEP = 10

VMEM_LIMIT = 56 * 1024 * 1024
MESH = pl.DeviceIdType.MESH

NT = (((1,), (1,)), ((), ()))
TN = (((0,), (0,)), ((), ()))


def _params(n_grid=0, **kw):
    sem = ("arbitrary",) * n_grid if n_grid else None
    return pltpu.CompilerParams(dimension_semantics=sem, vmem_limit_bytes=VMEM_LIMIT, **kw)


def _dot(a, b):
    return jnp.dot(a, b, preferred_element_type=F32)


def _dot_nt(a, b):
    return lax.dot_general(a, b, NT, preferred_element_type=F32)


def _dot_tn(a, b):
    return lax.dot_general(a, b, TN, preferred_element_type=F32)


def _sigmoid(x):
    return 1.0 / (1.0 + jnp.exp(-x))


def _rope_fwd(x, c, s1, s2):
    return x * c + pltpu.roll(x, LANES - HALF_ROPE, 1) * s1 + pltpu.roll(x, HALF_ROPE, 1) * s2


def _rope_bwd(dy, c, s1, s2):
    return dy * c - pltpu.roll(dy, LANES - HALF_ROPE, 1) * s1 - pltpu.roll(dy, HALF_ROPE, 1) * s2


def _in_proj_rows(wt_ref):
    kr = wt_ref[R_KR[0]:R_KR[1], :]
    pad = lambda n: jnp.zeros((n, D_MODEL), kr.dtype)
    return ((P_GM, wt_ref[R_MAIN[0]:R_MAIN[1], :]), (P_QL, wt_ref[R_QL[0]:R_QL[1], :]),
            (P_KVL, wt_ref[R_KVL[0]:R_KVL[1], :]),
            (P_KR, jnp.concatenate([pad(ROPE_LO), kr, pad(LANES - ROPE_LO - ROPE)], axis=0)))


def _full(shape):
    n = len(shape)
    return pl.BlockSpec(shape, lambda *_: (0,) * n)


ROPE_BLOCK = 512


def _rope_constants():
    inv = (np.float32(ROPE_THETA) ** (-np.arange(HALF_ROPE, dtype=np.float32) / np.float32(HALF_ROPE))).astype(np.float32)
    place = np.zeros((3, HALF_ROPE, LANES), np.float32)
    for i in range(HALF_ROPE):
        place[0, i, ROPE_LO + i] = place[0, i, ROPE_LO + HALF_ROPE + i] = 1.0
        place[1, i, ROPE_LO + i] = -1.0
        place[2, i, ROPE_LO + HALF_ROPE + i] = 1.0
    base = np.ones((1, LANES), np.float32)
    base[0, ROPE_LO:ROPE_LO + ROPE] = 0.0
    return jnp.asarray(inv.reshape(HALF_ROPE, 1)), jnp.asarray(place), jnp.asarray(base)


def _rope_block(pos, inv, place_ref, base):
    ang = inv * pos
    cos, sin = jnp.cos(ang), jnp.sin(ang)

    def put(v, k):
        return lax.dot_general(v, place_ref[k], TN, precision=lax.Precision.HIGHEST, preferred_element_type=F32)

    return put(cos, 0) + base, put(sin, 1), put(sin, 2)


def _all_gather_weights(shards, pos_f):
    n = len(shards)
    T = pos_f.shape[1]
    rb = min(ROPE_BLOCK, T)

    def body(*refs):
        ins, (pos_ref, inv_ref, place_ref, base_ref) = refs[:n], refs[n:n + 4]
        outs, tables = refs[n + 4:2 * n + 4], refs[2 * n + 4:2 * n + 7]
        send_sems, recv_sems = refs[2 * n + 7], refs[2 * n + 8]
        x, y, c = lax.axis_index("x"), lax.axis_index("y"), lax.axis_index("c")
        me, sibling = (x, y, c), (x, y, 1 - c)
        chips = [(1 - x, y), (x, 1 - y), (1 - x, 1 - y)]

        def idx(d):
            return 4 * d[0] + 2 * d[1] + d[2]

        def copy(a, k, block, to):
            rows = outs[a].at[idx(block)]
            return pltpu.make_async_remote_copy(src_ref=rows, dst_ref=rows, send_sem=send_sems.at[a, k],
                                                recv_sem=recv_sems.at[a, k], device_id=to, device_id_type=MESH)

        for a in range(n):
            outs[a][idx(me)] = ins[a][...].astype(BF16)
        first = []
        for a in range(n):
            first.append(copy(a, 0, me, sibling))
            first += [copy(a, 1 + j, me, (*chip, c)) for j, chip in enumerate(chips)]
        for cp in first:
            cp.start()
        for r0 in range(0, T, rb):
            for ref, tab in zip(tables, _rope_block(pos_ref[:, r0:r0 + rb], inv_ref[...], place_ref, base_ref[...])):
                ref[r0:r0 + rb, :] = tab
        passed = []
        for j, chip in enumerate(chips):
            for a in range(n):
                copy(a, 1 + j, (*chip, c), me).wait_recv()
                cp = copy(a, 4 + j, (*chip, c), sibling)
                cp.start()
                passed.append(cp)
        for a in range(n):
            copy(a, 0, sibling, me).wait_recv()
            for j, chip in enumerate(chips):
                copy(a, 4 + j, (*chip, 1 - c), me).wait_recv()
        for cp in first + passed:
            cp.wait_send()

    vm = pl.BlockSpec(memory_space=pltpu.VMEM)
    res = pl.pallas_call(
        body, name="all_gather_weights",
        in_specs=[vm] * (n + 4), out_specs=[vm] * (n + 3),
        out_shape=[jax.ShapeDtypeStruct((N_DEV,) + s.shape, BF16) for s in shards]
        + [jax.ShapeDtypeStruct((T, LANES), F32)] * 3,
        scratch_shapes=[pltpu.SemaphoreType.DMA((n, 7)), pltpu.SemaphoreType.DMA((n, 7))],
        compiler_params=pltpu.CompilerParams(vmem_limit_bytes=VMEM_LIMIT),
    )(*shards, pos_f, *_rope_constants())
    return res[:n], res[n:]


def _fwd_in(x, ln_g, w_in_t, q_g, w_q_p, kv_g, w_kv_p, c_t, s1_t, s2_t, tm, tq):
    T = x.shape[0]
    assert tm % tq == 0

    def body(x_ref, lng_ref, win_ref, qg_ref, wq_ref, kvg_ref, wkv_ref, c_ref, s1_ref, s2_ref,
             proj_ref, h_ref, qn_ref, kvn_ref, q_ref, k_ref, v_ref, vt_ref):
        xv = x_ref[...]
        r = lax.rsqrt(jnp.mean(xv * xv, axis=-1, keepdims=True) + EPS)
        h = (xv * r * lng_ref[...]).astype(BF16)
        h_ref[...] = h
        for col, rows in _in_proj_rows(win_ref):
            proj_ref[:, col:col + rows.shape[0]] = _dot_nt(h, rows)
        c, s1, s2 = c_ref[...], s1_ref[...], s2_ref[...]

        ql = proj_ref[:, P_QL:P_QL + Q_RANK]
        rq = lax.rsqrt(jnp.mean(ql * ql, axis=-1, keepdims=True) + EPS)
        qn = (ql * rq * qg_ref[...]).astype(BF16)
        qn_ref[...] = qn
        q = _dot(qn, wq_ref[...])
        for hd in range(N_HEADS):
            sl = slice(hd * HEAD_LANES, (hd + 1) * HEAD_LANES)
            q_ref[:, sl] = _rope_fwd(q[:, sl], c, s1, s2).astype(BF16)

        kvl = proj_ref[:, P_KVL:P_KVL + KV_RANK]
        rk = lax.rsqrt(jnp.mean(kvl * kvl, axis=-1, keepdims=True) + EPS)
        kvn = (kvl * rk * kvg_ref[...]).astype(BF16)
        kvn_ref[...] = kvn
        kv = _dot(kvn, wkv_ref[...])
        kpe = _rope_fwd(proj_ref[:, P_KR:P_KR + LANES], c, s1, s2)
        for hd in range(N_HEADS):
            sl = slice(hd * HEAD_LANES, (hd + 1) * HEAD_LANES)
            k_ref[:, sl] = (kv[:, sl] + kpe).astype(BF16)
        v_ref[...] = kv[:, N_HEADS * HEAD_LANES:].astype(BF16)
        for b in range(tm // tq):
            vt_ref[b] = kv[b * tq:(b + 1) * tq, N_HEADS * HEAD_LANES:].T.astype(BF16)

    def row(w):
        return pl.BlockSpec((tm, w), lambda i: (i, 0))

    outs = [(D_PERM, F32), (D_MODEL, BF16), (Q_RANK, BF16), (KV_RANK, BF16),
            (N_HEADS * HEAD_LANES, BF16), (N_HEADS * HEAD_LANES, BF16), (MLA_WIDTH, BF16)]
    return pl.pallas_call(
        body, name="fwd_in", grid=(T // tm,),
        in_specs=[row(D_MODEL), _full((1, D_MODEL)), _full((D_IN, D_MODEL)), _full((1, Q_RANK)),
                  _full((Q_RANK, N_HEADS * HEAD_LANES)), _full((1, KV_RANK)),
                  _full((KV_RANK, N_HEADS * HEAD_LANES + MLA_WIDTH)), row(LANES), row(LANES), row(LANES)],
        out_specs=[row(w) for w, _ in outs] + [pl.BlockSpec((tm // tq, MLA_WIDTH, tq), lambda i: (i, 0, 0))],
        out_shape=[jax.ShapeDtypeStruct((T, w), dt) for w, dt in outs]
        + [jax.ShapeDtypeStruct((T // tq, MLA_WIDTH, tq), BF16)],
        compiler_params=_params(1),
    )(x, ln_g, w_in_t, q_g, w_q_p, kv_g, w_kv_p, c_t, s1_t, s2_t)


LOG2E = 1.4426950408889634
SCALE2 = SCALE * LOG2E


def _causal(tq):
    r = lax.broadcasted_iota(jnp.int32, (tq, tq), 0)
    c = lax.broadcasted_iota(jnp.int32, (tq, tq), 1)
    return r <= c


MASKED = -1e30


def _causal_bias(bias_ref, tq):
    bias_ref[0] = jnp.zeros((tq, tq), F32)
    bias_ref[1] = jnp.where(_causal(tq), 0.0, MASKED)


def _tile_tables(nq, by_query):
    if by_query:
        pairs = [(j, i) for i in range(nq) for j in range(i + 1)]
    else:
        pairs = [(j, i) for j in range(nq) for i in range(nq - 1, j - 1, -1)]
    pairs.append(pairs[-1])
    jj, ii = np.array(pairs, np.int32).T
    return jnp.asarray(jj), jnp.asarray(ii), len(pairs) - 1


ATTN_TRIP = 8


def _walk_tiles(n, products, tile, flush, buf_a, buf_b):
    bufs = (buf_a, buf_b)
    products(0, buf_a)

    def trip(r, carry):
        for u in range(ATTN_TRIP):
            products(ATTN_TRIP * r + u + 1, bufs[(u + 1) % 2])
            tile(ATTN_TRIP * r + u, bufs[u % 2])
        for u in range(ATTN_TRIP):
            flush(ATTN_TRIP * r + u)
        return carry

    lax.fori_loop(0, n // ATTN_TRIP, trip, 0)
    rest = n - n % ATTN_TRIP
    for u in range(n % ATTN_TRIP):
        if rest + u + 1 < n:
            products(rest + u + 1, bufs[(u + 1) % 2])
        tile(rest + u, bufs[u % 2])
    for u in range(n % ATTN_TRIP):
        flush(rest + u)


V_ROWS = V_DIM + 16


def _attn_fwd_flat(k, q_t, v_t, tq):
    T = k.shape[0]
    nq = T // tq
    jj, ii, n = _tile_tables(nq, True)
    heads = [slice(hh * HEAD_LANES, (hh + 1) * HEAD_LANES) for hh in range(2)]

    def body(jj_ref, ii_ref, k_ref, qt_ref, vt_ref, o_ref, ot_ref, lse_ref, sa_ref, sb_ref, m_ref, acc_ref, bias_ref):
        def reset(st):
            m_ref[st] = jnp.full(m_ref.shape[1:], MASKED, F32)
            acc_ref[st] = jnp.zeros(acc_ref.shape[1:], F32)

        _causal_bias(bias_ref, tq)
        for st in range(ATTN_TRIP):
            reset(st)
        extra = (lax.broadcasted_iota(jnp.int32, (V_ROWS - V_DIM, tq), 0) == 0).astype(BF16)

        def products(t, buf):
            j, i = jj_ref[t], ii_ref[t]
            kj = k_ref[pl.ds(pl.multiple_of(j * tq, tq), tq), :]
            for hh, sl in enumerate(heads):
                buf[hh] = _dot(kj[:, sl], qt_ref[i, sl, :])

        def tile(t, buf):
            j, i = jj_ref[t], ii_ref[t]
            vt = vt_ref[j]
            bias = bias_ref.at[(j == i).astype(jnp.int32)]
            st = i % ATTN_TRIP
            for hh in range(2):
                s = buf[hh] * SCALE2 + bias[...]
                m = m_ref[st, hh]
                m_new = jnp.maximum(m, jnp.max(s, axis=0, keepdims=True))
                alpha = jnp.exp2(m - m_new)
                p = jnp.exp2(s - m_new)
                m_ref[st, hh] = m_new
                v_h = jnp.concatenate([vt[hh * V_DIM:(hh + 1) * V_DIM, :], extra], axis=0)
                acc_ref[st, hh] = alpha * acc_ref[st, hh] + _dot(v_h, p.astype(BF16))

        def flush(t):
            j, i = jj_ref[t], ii_ref[t]

            @pl.when(j == i)
            def _():
                st = i % ATTN_TRIP
                den = [acc_ref[st, hh, V_DIM:V_DIM + 1, :] for hh in range(2)]
                out = jnp.concatenate([acc_ref[st, hh, :V_DIM, :] / den[hh] for hh in range(2)], axis=0)
                o_ref[pl.ds(pl.multiple_of(i * tq, tq), tq), :] = out.T
                ot_ref[i] = out
                for hh in range(2):
                    lse_ref[hh, i] = m_ref[st, hh] + jnp.log2(den[hh])
                reset(st)

        _walk_tiles(n, products, tile, flush, sa_ref, sb_ref)

    smem = pl.BlockSpec(memory_space=pltpu.SMEM)
    return pl.pallas_call(
        body, name="attn_fwd", grid=(N_HEADS // 2,),
        in_specs=[smem, smem,
                  pl.BlockSpec((T, 2 * HEAD_LANES), lambda p: (0, p)),
                  pl.BlockSpec((nq, 2 * HEAD_LANES, tq), lambda p: (0, p, 0)),
                  pl.BlockSpec((nq, LANES, tq), lambda p: (0, p, 0))],
        out_specs=[pl.BlockSpec((T, LANES), lambda p: (0, p)), pl.BlockSpec((nq, LANES, tq), lambda p: (0, p, 0)),
                   pl.BlockSpec((2, nq, 1, tq), lambda p: (p, 0, 0, 0))],
        out_shape=[jax.ShapeDtypeStruct((T, MLA_WIDTH), F32), jax.ShapeDtypeStruct((nq, MLA_WIDTH, tq), F32),
                   jax.ShapeDtypeStruct((N_HEADS, nq, 1, tq), F32)],
        scratch_shapes=[pltpu.VMEM((2, tq, tq), F32), pltpu.VMEM((2, tq, tq), F32),
                        pltpu.VMEM((ATTN_TRIP, 2, 1, tq), F32),
                        pltpu.VMEM((ATTN_TRIP, 2, V_ROWS, tq), F32), pltpu.VMEM((2, tq, tq), F32)],
        compiler_params=_params(1),
    )(jj, ii, k, q_t, v_t)


def _attn_bwd_flat(k, v, q_t, k_t, do_t, o_t, lse, tq):
    T = k.shape[0]
    nq = T // tq
    jj, ii, n = _tile_tables(nq, False)
    heads = [slice(hh * HEAD_LANES, (hh + 1) * HEAD_LANES) for hh in range(2)]

    def body(jj_ref, ii_ref, k_ref, v_ref, qt_ref, kt_ref, dot_ref, ot_ref, lse_ref, dqt_ref, dk_ref, dv_ref,
             ba_ref, bb_ref, dkt_ref, dvt_ref, bias_ref, dsum_ref):
        _causal_bias(bias_ref, tq)

        def row_dots(i, carry):
            for hh in range(2):
                own = slice(hh * V_DIM, (hh + 1) * V_DIM)
                do_h = dot_ref[i, hh * HEAD_LANES + own.start:hh * HEAD_LANES + own.stop, :]
                dsum_ref[hh, i] = jnp.sum(do_h.astype(F32) * ot_ref[i, own, :], axis=0, keepdims=True)
            return carry

        lax.fori_loop(0, nq, row_dots, 0)
        dqt_ref[...] = jnp.zeros_like(dqt_ref)
        dkt_ref[...] = jnp.zeros_like(dkt_ref)
        dvt_ref[...] = jnp.zeros_like(dvt_ref)

        def products(t, buf):
            j, i = jj_ref[t], ii_ref[t]
            rows = pl.ds(pl.multiple_of(j * tq, tq), tq)
            for hh, sl in enumerate(heads):
                buf[hh] = _dot(k_ref[rows, sl], qt_ref[i, sl, :])
                buf[2 + hh] = _dot(v_ref[rows, :], dot_ref[i, sl, :])

        def tile(t, buf):
            j, i = jj_ref[t], ii_ref[t]
            bias = bias_ref.at[(j == i).astype(jnp.int32)]
            st = j % ATTN_TRIP
            for hh, sl in enumerate(heads):
                p = jnp.exp2(buf[hh] * SCALE2 + bias[...] - lse_ref[hh, i])
                ds = (p * (buf[2 + hh] - dsum_ref[hh, i]) * SCALE).astype(BF16)
                own = slice(hh * V_DIM, (hh + 1) * V_DIM)
                do_h = dot_ref[i, hh * HEAD_LANES + own.start:hh * HEAD_LANES + own.stop, :]
                dvt_ref[st, own, :] += _dot_nt(do_h, p.astype(BF16))
                used = slice(sl.start, sl.start + NOPE + ROPE)
                dkt_ref[st, used, :] += _dot_nt(qt_ref[i, used, :], ds)
                dqt_ref[i, used, :] += _dot(kt_ref[j, used, :], ds)

        def flush(t):
            j, i = jj_ref[t], ii_ref[t]

            @pl.when(j == i)
            def _():
                st = j % ATTN_TRIP
                rows = pl.ds(pl.multiple_of(j * tq, tq), tq)
                dk_ref[rows, :] = dkt_ref[st].T
                dv_ref[rows, :] = dvt_ref[st].T
                dkt_ref[st] = jnp.zeros(dkt_ref.shape[1:], F32)
                dvt_ref[st] = jnp.zeros(dvt_ref.shape[1:], F32)

        _walk_tiles(n, products, tile, flush, ba_ref, bb_ref)

    smem = pl.BlockSpec(memory_space=pltpu.SMEM)
    stat = pl.BlockSpec((2, nq, 1, tq), lambda p: (p, 0, 0, 0))
    blocks_t = pl.BlockSpec((nq, 2 * HEAD_LANES, tq), lambda p: (0, p, 0))
    return pl.pallas_call(
        body, name="attn_bwd", grid=(N_HEADS // 2,),
        in_specs=[smem, smem,
                  pl.BlockSpec((T, 2 * HEAD_LANES), lambda p: (0, p)),
                  pl.BlockSpec((T, LANES), lambda p: (0, p)),
                  blocks_t, blocks_t, blocks_t, pl.BlockSpec((nq, LANES, tq), lambda p: (0, p, 0)), stat],
        out_specs=[blocks_t,
                   pl.BlockSpec((T, 2 * HEAD_LANES), lambda p: (0, p)),
                   pl.BlockSpec((T, LANES), lambda p: (0, p))],
        out_shape=[jax.ShapeDtypeStruct((nq, N_HEADS * HEAD_LANES, tq), F32),
                   jax.ShapeDtypeStruct((T, N_HEADS * HEAD_LANES), F32),
                   jax.ShapeDtypeStruct((T, MLA_WIDTH), F32)],
        scratch_shapes=[pltpu.VMEM((4, tq, tq), F32), pltpu.VMEM((4, tq, tq), F32),
                        pltpu.VMEM((ATTN_TRIP, 2 * HEAD_LANES, tq), F32), pltpu.VMEM((ATTN_TRIP, LANES, tq), F32),
                        pltpu.VMEM((2, tq, tq), F32), pltpu.VMEM((2, nq, 1, tq), F32)],
        compiler_params=_params(1),
    )(jj, ii, k, v, q_t, k_t, do_t, o_t, lse)


def _lower_bound(lbp):
    a, b = lbp[0:1, :], lbp[1:2, :]
    mx = jnp.maximum(a, b)
    ea, eb = jnp.exp(a - mx), jnp.exp(b - mx)
    return ea / (ea + eb)


def _tri(lower):
    r = lax.broadcasted_iota(jnp.int32, (CHUNK, CHUNK), 0)
    c = lax.broadcasted_iota(jnp.int32, (CHUNK, CHUNK), 1)
    return (c <= r) if lower else (c >= r)


def _running_sum(x, from_end):
    row = lax.broadcasted_iota(jnp.int32, x.shape, 0)
    step = 1
    while step < CHUNK:
        if from_end:
            x = x + jnp.where(row < CHUNK - step, pltpu.roll(x, CHUNK - step, 0), 0.0)
        else:
            x = x + jnp.where(row >= step, pltpu.roll(x, step, 0), 0.0)
        step *= 2
    return x


def _hg_gates(hq, hf, lb):
    sq = _sigmoid(hq)
    sf = _sigmoid(hf)
    f = lb + (1.0 - lb) * sf
    g = jnp.log(f)
    gcum = _running_sum(g, False)
    return sq, sf, f, hq * sq, 1.0 - f, gcum


def _head(x, hd):
    return x[:, hd * HG_DIM:(hd + 1) * HG_DIM]


def _all_heads(fn):
    return jnp.concatenate([fn(hd) for hd in range(HG_HEADS)], axis=1)


def _hg_blocks(q, kk, gcum):
    rowi = lax.broadcasted_iota(jnp.int32, gcum.shape, 0)
    out = []
    for blk in range(CHUNK // SUB):
        lo, hi = blk * SUB, (blk + 1) * SUB
        gb = gcum[lo - 1:lo, :] if blk else jnp.zeros_like(gcum[0:1, :])
        eq = jnp.exp(gcum[lo:hi, :] - gb)
        ek = jnp.exp(jnp.where(rowi < hi, gb - gcum, 0.0))
        out.append((eq, ek, (q[lo:hi, :] * eq).astype(BF16), (kk * ek).astype(BF16)))
    return out


def _hg_scores(blocks):
    out = []
    for hd in range(HG_HEADS):
        a = jnp.concatenate([_dot_nt(_head(qb, hd), _head(kb, hd)) for _, _, qb, kb in blocks], axis=0)
        out.append(jnp.where(_tri(True), a, 0.0))
    return out


HG_STEP_CHUNKS = 8


def _hgrn_fwd(proj, lbp):
    T = proj.shape[0]
    nc = T // CHUNK
    ns = min(HG_STEP_CHUNKS, nc)
    rows = ns * CHUNK

    def body(hq_ref, hf_ref, hi_ref, lbp_ref, o_ref, st_ref, state):
        @pl.when(pl.program_id(0) == 0)
        def _():
            state[...] = jnp.zeros_like(state)

        lb = _lower_bound(lbp_ref[...])
        work = []
        for c in range(ns):
            r = slice(c * CHUNK, (c + 1) * CHUNK)
            _, _, _, q, kk, gcum = _hg_gates(hq_ref[r, :], hf_ref[r, :], lb)
            vb = hi_ref[r, :].astype(BF16)
            a = _hg_scores(_hg_blocks(q, kk, gcum))
            gend = gcum[CHUNK - 1:CHUNK, :]
            qgb = (q * jnp.exp(gcum)).astype(BF16)
            kgeb = (kk * jnp.exp(gend - gcum)).astype(BF16)
            intra = [_dot(a[hd].astype(BF16), _head(vb, hd)) for hd in range(HG_HEADS)]
            update = [_dot_tn(_head(vb, hd), _head(kgeb, hd)) for hd in range(HG_HEADS)]
            work.append((qgb, jnp.exp(gend), intra, update))
        for hd in range(HG_HEADS):
            st = state[hd]
            for c, (qgb, egend, intra, update) in enumerate(work):
                st_ref[c, hd] = st
                o_ref[c * CHUNK:(c + 1) * CHUNK, hd * HG_DIM:(hd + 1) * HG_DIM] = (
                    intra[hd] + _dot_nt(_head(qgb, hd), st.astype(BF16)))
                st = st * _head(egend, hd) + update[hd]
            state[hd] = st

    def col(cb):
        return pl.BlockSpec((rows, HG_WIDTH), lambda i: (i, cb))

    return pl.pallas_call(
        body, name="hgrn_fwd", grid=(nc // ns,),
        in_specs=[col(P_HQ // HG_WIDTH), col(P_HF // HG_WIDTH), col(P_HI // HG_WIDTH), _full((2, HG_WIDTH))],
        out_specs=[pl.BlockSpec((rows, HG_WIDTH), lambda i: (i, 0)),
                   pl.BlockSpec((ns, HG_HEADS, HG_DIM, HG_DIM), lambda i: (i, 0, 0, 0))],
        out_shape=[jax.ShapeDtypeStruct((T, HG_WIDTH), F32),
                   jax.ShapeDtypeStruct((nc, HG_HEADS, HG_DIM, HG_DIM), F32)],
        scratch_shapes=[pltpu.VMEM((HG_HEADS, HG_DIM, HG_DIM), F32)],
        compiler_params=_params(1),
    )(proj, proj, proj, lbp)


def _hgrn_bwd(proj, lbp, do_hg, states):
    T = proj.shape[0]
    nc = T // CHUNK
    ns = min(HG_STEP_CHUNKS, nc)
    rows = ns * CHUNK
    steps = nc // ns

    def body(hq_ref, hf_ref, hi_ref, lbp_ref, do_ref, st_ref, dhq_ref, dhf_ref, dhi_ref, dlb_ref, dstate):
        @pl.when(pl.program_id(0) == 0)
        def _():
            dstate[...] = jnp.zeros_like(dstate)
            dlb_ref[...] = jnp.zeros_like(dlb_ref)

        lb = _lower_bound(lbp_ref[...])

        dst_all = [dstate[hd] for hd in range(HG_HEADS)]
        dlb = jnp.zeros_like(lb)
        last = lax.broadcasted_iota(jnp.int32, (CHUNK, HG_WIDTH), 0) == CHUNK - 1
        for c in reversed(range(ns)):
            r = slice(c * CHUNK, (c + 1) * CHUNK)
            hq = hq_ref[r, :]
            sq, sf, f, q, kk, gcum = _hg_gates(hq, hf_ref[r, :], lb)
            vb = hi_ref[r, :].astype(BF16)
            dob = do_ref[r, :].astype(BF16)
            blocks = _hg_blocks(q, kk, gcum)
            a = _hg_scores(blocks)
            gend = gcum[CHUNK - 1:CHUNK, :]
            eg, egend, ekend = jnp.exp(gcum), jnp.exp(gend), jnp.exp(gend - gcum)
            qg, kge = q * eg, kk * ekend
            qgb, kgeb = qg.astype(BF16), kge.astype(BF16)

            dv, dqg, dkge, st_dst, dq_blk, dk_blk = [], [], [], [], [], []
            for hd in range(HG_HEADS):
                st = st_ref[c, hd]
                dst = dst_all[hd]
                dstb = dst.astype(BF16)
                do_h, v_h = _head(dob, hd), _head(vb, hd)
                dv.append(_dot_tn(a[hd].astype(BF16), do_h) + _dot_nt(_head(kgeb, hd), dstb))
                da = jnp.where(_tri(True), _dot_nt(do_h, v_h), 0.0).astype(BF16)
                dqg.append(_dot(do_h, st.astype(BF16)))
                dkge.append(_dot(v_h, dstb))
                st_dst.append(jnp.sum(st * dst, axis=0, keepdims=True))
                dst_all[hd] = _dot_tn(do_h, _head(qgb, hd)) + dst * _head(egend, hd)
                dq_blk.append([_dot(da[b * SUB:(b + 1) * SUB, :], _head(kb, hd)) for b, (_, _, _, kb) in enumerate(blocks)])
                dk_blk.append([_dot_tn(da[b * SUB:(b + 1) * SUB, :], _head(qb, hd)) for b, (_, _, qb, _) in enumerate(blocks)])
            dv, dqg, dkge, st_dst = (jnp.concatenate(t, axis=1) for t in (dv, dqg, dkge, st_dst))

            dq_a, dg_q = [], []
            dk_a, dg_k = jnp.zeros_like(gcum), jnp.zeros_like(gcum)
            for b, (eq, ek, qb, kb) in enumerate(blocks):
                dq_b = _all_heads(lambda hd: dq_blk[hd][b])
                dk_b = _all_heads(lambda hd: dk_blk[hd][b])
                dq_a.append(dq_b * eq)
                dk_a = dk_a + dk_b * ek
                dg_q.append(qb.astype(F32) * dq_b)
                dg_k = dg_k + kb.astype(F32) * dk_b
            dq_a = jnp.concatenate(dq_a, axis=0)

            dgend = st_dst * egend + jnp.sum(dkge * kge, axis=0, keepdims=True)
            dq = dq_a + dqg * eg
            dk = dk_a + dkge * ekend
            dgc = jnp.concatenate(dg_q, axis=0) - dg_k + dqg * qg - dkge * kge + jnp.where(last, dgend, 0.0)
            dg = _running_sum(dgc, True)
            df = dg / f - dk
            dhf_ref[r, :] = df * (1.0 - lb) * sf * (1.0 - sf)
            dlb = dlb + jnp.sum(df * (1.0 - sf), axis=0, keepdims=True)
            dhq_ref[r, :] = dq * (sq * (1.0 + hq * (1.0 - sq)))
            dhi_ref[r, :] = dv
        for hd in range(HG_HEADS):
            dstate[hd] = dst_all[hd]
        dlb_ref[...] += dlb

    def col(cb):
        return pl.BlockSpec((rows, HG_WIDTH), lambda i: (steps - 1 - i, cb))

    grad = jax.ShapeDtypeStruct((T, HG_WIDTH), F32)
    return pl.pallas_call(
        body, name="hgrn_bwd", grid=(steps,),
        in_specs=[col(P_HQ // HG_WIDTH), col(P_HF // HG_WIDTH), col(P_HI // HG_WIDTH), _full((2, HG_WIDTH)),
                  col(0), pl.BlockSpec((ns, HG_HEADS, HG_DIM, HG_DIM), lambda i: (steps - 1 - i, 0, 0, 0))],
        out_specs=[col(0), col(0), col(0), _full((1, HG_WIDTH))],
        out_shape=[grad, grad, grad, jax.ShapeDtypeStruct((1, HG_WIDTH), F32)],
        scratch_shapes=[pltpu.VMEM((HG_HEADS, HG_DIM, HG_DIM), F32)],
        compiler_params=_params(1),
    )(proj, proj, proj, lbp, do_hg, states)


def _top(x, tgt, o_mla, o_hg, proj, w_out, hg_norm_g, final_g, tm, tq):
    T = x.shape[0]
    assert tm % tq == 0

    def body(x_ref, tgt_ref, om_ref, oh_ref, gm_ref, gh_ref, wout_ref, hgn_ref, fng_ref,
             dx2_ref, dot_ref, dgm_ref, doh_ref, dgh_ref, loss_ref, dfng_ref, dhgn_ref, dwout_ref, ycat_ref):
        @pl.when(pl.program_id(0) == 0)
        def _():
            for ref in (loss_ref, dfng_ref, dhgn_ref, dwout_ref):
                ref[...] = jnp.zeros_like(ref)

        gm, om = gm_ref[...], om_ref[...]
        sgm = _sigmoid(gm)
        silu_m = gm * sgm
        gh, oh, gam = gh_ref[...], oh_ref[...], hgn_ref[...]
        sgh = _sigmoid(gh)
        silu_h = gh * sgh
        rr, nn = [], []
        for hd in range(HG_HEADS):
            oh_h = oh[:, hd * HG_DIM:(hd + 1) * HG_DIM]
            r_h = lax.rsqrt(jnp.mean(oh_h * oh_h, axis=-1, keepdims=True) + EPS)
            rr.append(r_h)
            nn.append(oh_h * r_h)
        n = jnp.concatenate(nn, axis=1)
        ng = n * gam
        ycat_ref[:, :MLA_WIDTH] = (om * silu_m).astype(BF16)
        ycat_ref[:, MLA_WIDTH:] = (ng * silu_h).astype(BF16)
        wout = wout_ref[...]
        x2 = x_ref[...] + _dot(ycat_ref[...], wout)
        r = lax.rsqrt(jnp.mean(x2 * x2, axis=-1, keepdims=True) + EPS)
        xh = x2 * r
        fng = fng_ref[...]
        err = xh * fng - tgt_ref[...]
        loss_ref[...] += 0.5 * jnp.sum(jnp.mean(err * err, axis=-1, keepdims=True), axis=0, keepdims=True)
        dout = err * (1.0 / D_MODEL)
        dfng_ref[...] += jnp.sum(dout * xh, axis=0, keepdims=True)
        dxh = dout * fng
        dx2 = r * (dxh - xh * jnp.mean(dxh * xh, axis=-1, keepdims=True))
        dx2_ref[...] = dx2
        dx2b = dx2.astype(BF16)
        dwout_ref[...] += _dot_tn(ycat_ref[...], dx2b)
        dycat = _dot_nt(dx2b, wout)
        dym, dyh = dycat[:, :MLA_WIDTH], dycat[:, MLA_WIDTH:]
        dom = dym * silu_m
        first = lax.broadcasted_iota(jnp.int32, (tm, LANES), 1) < V_DIM
        for hd in range(N_HEADS):
            pair = dom[:, hd // 2 * LANES:(hd // 2 + 1) * LANES]
            own = jnp.where(first, pair, 0.0) if hd % 2 == 0 else jnp.where(first, 0.0, pair)
            for b in range(tm // tq):
                dot_ref[b, hd * HEAD_LANES:(hd + 1) * HEAD_LANES, :] = own[b * tq:(b + 1) * tq, :].T.astype(BF16)
        dgm_ref[...] = dym * om * (sgm * (1.0 + gm * (1.0 - sgm)))
        dgh_ref[...] = dyh * ng * (sgh * (1.0 + gh * (1.0 - sgh)))
        dng = dyh * silu_h
        dhgn_ref[...] += jnp.sum(dng * n, axis=0, keepdims=True)
        dn = dng * gam
        for hd in range(HG_HEADS):
            sl = slice(hd * HG_DIM, (hd + 1) * HG_DIM)
            dn_h, n_h = dn[:, sl], nn[hd]
            doh_ref[:, sl] = rr[hd] * (dn_h - n_h * jnp.mean(dn_h * n_h, axis=-1, keepdims=True))

    def row(w, cb=0):
        return pl.BlockSpec((tm, w), lambda i: (i, cb))

    hl = N_HEADS * HEAD_LANES
    blocks_t = pl.BlockSpec((tm // tq, hl, tq), lambda i: (i, 0, 0))
    outs = [(D_MODEL, F32), (0, BF16), (MLA_WIDTH, F32), (HG_WIDTH, F32), (HG_WIDTH, F32)]
    small = [(1, SMALL_W), (1, D_MODEL), (1, HG_WIDTH), (D_MODEL, D_MODEL)]
    return pl.pallas_call(
        body, name="top", grid=(T // tm,),
        in_specs=[row(D_MODEL), row(D_MODEL), row(MLA_WIDTH), row(HG_WIDTH),
                  row(MLA_WIDTH, P_GM // MLA_WIDTH), row(HG_WIDTH, P_GH // HG_WIDTH),
                  _full((D_MODEL, D_MODEL)), _full((1, HG_WIDTH)), _full((1, D_MODEL))],
        out_specs=[row(w) if w else blocks_t for w, _ in outs] + [_full(s) for s in small],
        out_shape=[jax.ShapeDtypeStruct((T, w) if w else (T // tq, hl, tq), dt) for w, dt in outs]
        + [jax.ShapeDtypeStruct(s, F32) for s in small],
        scratch_shapes=[pltpu.VMEM((tm, D_MODEL), BF16)],
        compiler_params=_params(1),
    )(x, tgt, o_mla, o_hg, proj, proj, w_out, hg_norm_g, final_g)


def _bot(x, dx2, proj, h, qn, kvn, dq, dk, dv, dgm, dhq, dhf, dhi, dgh, c_t, s1_t, s2_t, w_in_t, w_q_p, w_kv_p, ln_g, q_g, kv_g, tm):
    T = x.shape[0]
    lat_w = D_PERM - P_QL
    steps = T // tm

    def body(x_ref, dx2_ref, lat_ref, h_ref, qn_ref, kvn_ref, dq_ref, dk_ref, dv_ref, dgm_ref, dhq_ref, dhf_ref,
             dhi_ref, dgh_ref, c_ref, s1_ref, s2_ref, win_ref, wq_ref, wkv_ref, lng_ref, qg_ref, kvg_ref,
             dx_ref, dlng_ref, dqg_ref, dkvg_ref, dwq_ref, dwkv_ref, dwin_ref, dqpre_ref, dkv_ref, dproj_ref,
             dwin_acc, sem):
        @pl.when(pl.program_id(0) == 0)
        def _():
            for ref in (dlng_ref, dqg_ref, dkvg_ref, dwq_ref, dwkv_ref, dwin_acc):
                ref[...] = jnp.zeros_like(ref)

        c, s1, s2 = c_ref[...], s1_ref[...], s2_ref[...]
        dkpe = jnp.zeros((tm, LANES), F32)
        for hd in range(N_HEADS):
            sl = slice(hd * HEAD_LANES, (hd + 1) * HEAD_LANES)
            dqpre_ref[:, sl] = _rope_bwd(dq_ref[:, sl], c, s1, s2).astype(BF16)
            dk_h = dk_ref[:, sl]
            dkpe = dkpe + dk_h
            dkv_ref[:, sl] = dk_h.astype(BF16)
        dkv_ref[:, N_HEADS * HEAD_LANES:] = dv_ref[...].astype(BF16)
        lane = lax.broadcasted_iota(jnp.int32, (tm, LANES), 1)
        rope_lanes = jnp.logical_and(lane >= ROPE_LO, lane < ROPE_LO + ROPE)
        dkr = jnp.where(rope_lanes, _rope_bwd(dkpe, c, s1, s2), 0.0)

        def norm_bwd(v, g, dy):
            r = lax.rsqrt(jnp.mean(v * v, axis=-1, keepdims=True) + EPS)
            vh = v * r
            dvh = dy * g
            return jnp.sum(dy * vh, axis=0, keepdims=True), r * (dvh - vh * jnp.mean(dvh * vh, axis=-1, keepdims=True))

        dwq_ref[...] += _dot_tn(qn_ref[...], dqpre_ref[...])
        dwkv_ref[...] += _dot_tn(kvn_ref[...], dkv_ref[...])
        dqn = _dot_nt(dqpre_ref[...], wq_ref[...])
        dg_q, dql = norm_bwd(lat_ref[:, :Q_RANK], qg_ref[...], dqn)
        dqg_ref[...] += dg_q
        dkn = _dot_nt(dkv_ref[...], wkv_ref[...])
        dg_kv, dkvl = norm_bwd(lat_ref[:, Q_RANK:Q_RANK + KV_RANK], kvg_ref[...], dkn)
        dkvg_ref[...] += dg_kv

        dproj_ref[:, P_GM:P_GM + MLA_WIDTH] = dgm_ref[...].astype(BF16)
        dproj_ref[:, P_HQ:P_HQ + HG_WIDTH] = dhq_ref[...].astype(BF16)
        dproj_ref[:, P_HF:P_HF + HG_WIDTH] = dhf_ref[...].astype(BF16)
        dproj_ref[:, P_HI:P_HI + HG_WIDTH] = dhi_ref[...].astype(BF16)
        dproj_ref[:, P_GH:P_GH + HG_WIDTH] = dgh_ref[...].astype(BF16)
        dproj_ref[:, P_QL:P_QL + Q_RANK] = dql.astype(BF16)
        dproj_ref[:, P_KVL:P_KVL + KV_RANK] = dkvl.astype(BF16)
        dproj_ref[:, P_KR:P_KR + LANES] = dkr.astype(BF16)
        dh = sum(_dot(dproj_ref[:, col:col + rows.shape[0]], rows) for col, rows in _in_proj_rows(win_ref))
        dg_ln, dxn = norm_bwd(x_ref[...], lng_ref[...], dh)
        dlng_ref[...] += dg_ln
        dx_ref[...] = dx2_ref[...] + dxn
        dwin_acc[...] += _dot_tn(dproj_ref[...], h_ref[...])

        @pl.when(pl.program_id(0) == steps - 1)
        def _():
            kr = P_KR + ROPE_LO
            moves = [((P_GM, P_QL), R_MAIN), ((P_QL, P_KR), (R_QL[0], R_KVL[1])), ((kr, kr + ROPE), R_KR)]
            copies = [pltpu.make_async_copy(dwin_acc.at[a:b, :], dwin_ref.at[c:d, :], sem.at[n])
                      for n, ((a, b), (c, d)) in enumerate(moves)]
            for cp in copies:
                cp.start()
            for cp in copies:
                cp.wait()

    def row(w, cb=0):
        return pl.BlockSpec((tm, w), lambda i: (i, cb))

    hl = N_HEADS * HEAD_LANES
    outs = [(D_MODEL, F32)]
    small = [(1, D_MODEL), (1, Q_RANK), (1, KV_RANK), (Q_RANK, hl), (KV_RANK, hl + MLA_WIDTH)]
    return pl.pallas_call(
        body, name="bot", grid=(steps,),
        in_specs=[row(D_MODEL), row(D_MODEL), row(lat_w, P_QL // lat_w), row(D_MODEL), row(Q_RANK), row(KV_RANK),
                  row(hl), row(hl), row(MLA_WIDTH),
                  row(MLA_WIDTH), row(HG_WIDTH), row(HG_WIDTH), row(HG_WIDTH), row(HG_WIDTH),
                  row(LANES), row(LANES), row(LANES),
                  _full((D_IN, D_MODEL)), _full((Q_RANK, hl)), _full((KV_RANK, hl + MLA_WIDTH)),
                  _full((1, D_MODEL)), _full((1, Q_RANK)), _full((1, KV_RANK))],
        out_specs=[row(w) for w, _ in outs] + [_full(s) for s in small] + [pl.BlockSpec(memory_space=pl.ANY)],
        out_shape=[jax.ShapeDtypeStruct((T, w), dt) for w, dt in outs] + [jax.ShapeDtypeStruct(s, F32) for s in small]
        + [jax.ShapeDtypeStruct((D_IN, D_MODEL), F32)],
        scratch_shapes=[pltpu.VMEM((tm, hl), BF16), pltpu.VMEM((tm, hl + MLA_WIDTH), BF16),
                        pltpu.VMEM((tm, D_PERM), BF16), pltpu.VMEM((D_PERM, D_MODEL), F32),
                        pltpu.SemaphoreType.DMA((3,))],
        compiler_params=_params(1),
    )(x, dx2, proj, h, qn, kvn, dq, dk, dv, dgm, dhq, dhf, dhi, dgh, c_t, s1_t, s2_t, w_in_t, w_q_p, w_kv_p, ln_g,
      q_g, kv_g)


RS_ROWS = 256


def _reduce_scatter(slabs, small):
    n = len(slabs)
    units = []
    for a, s in enumerate(slabs):
        rows, cols = s.shape[1:]
        if rows % RS_ROWS == 0 or rows < RS_ROWS:
            units += [(a, (pl.ds(r0, min(rows, RS_ROWS)), slice(None))) for r0 in range(0, rows, RS_ROWS)]
        else:
            units += [(a, (slice(None), pl.ds(c0, RS_ROWS))) for c0 in range(0, cols, RS_ROWS)]
    nu = len(units)

    def body(*refs):
        ins, small_ref = refs[:n], refs[n]
        outs, small_out = refs[n + 1:2 * n + 1], refs[2 * n + 1]
        own, sib_land, ici_out, ici_land = (refs[(2 + g) * n + 2:(3 + g) * n + 2] for g in range(4))
        small_land = refs[6 * n + 2]
        loc_sems, d2d_send, d2d_recv, ici_send, ici_recv, sm_send, sm_recv = refs[6 * n + 3:6 * n + 10]
        x, y, c = lax.axis_index("x"), lax.axis_index("y"), lax.axis_index("c")
        me = 4 * x + 2 * y + c

        def chip(k):
            return (1 - x if k & 2 else x, 1 - y if k & 1 else y)

        def block(k, core):
            px, py = chip(k)
            return 4 * px + 2 * py + core

        def part(u):
            return units[u]

        def local(u, k):
            a, rows = part(u)
            return pltpu.make_async_copy(ins[a].at[(block(k, c),) + rows], own[a].at[(k,) + rows], loc_sems.at[u, k])

        def to_sibling(u, k):
            a, rows = part(u)
            return pltpu.make_async_remote_copy(
                src_ref=ins[a].at[(block(k, 1 - c),) + rows], dst_ref=sib_land[a].at[(k,) + rows],
                send_sem=d2d_send.at[u, k], recv_sem=d2d_recv.at[u, k], device_id=(x, y, 1 - c), device_id_type=MESH)

        def to_chip(u, k):
            a, rows = part(u)
            return pltpu.make_async_remote_copy(
                src_ref=ici_out[a].at[(k - 1,) + rows], dst_ref=ici_land[a].at[(k - 1,) + rows],
                send_sem=ici_send.at[u, k - 1], recv_sem=ici_recv.at[u, k - 1], device_id=(*chip(k), c),
                device_id_type=MESH)

        def small_copy(k, receiving):
            px, py = chip(k >> 1)
            pc = 1 - c if k & 1 else c
            slot = 4 * px + 2 * py + pc if receiving else me
            return pltpu.make_async_remote_copy(
                src_ref=small_ref, dst_ref=small_land.at[slot], send_sem=sm_send.at[k - 1], recv_sem=sm_recv.at[k - 1],
                device_id=(px, py, pc), device_id_type=MESH)

        for u in range(nu):
            for k in range(4):
                local(u, k).start()
        for u in range(nu):
            for k in range(4):
                to_sibling(u, k).start()
        small_land[me] = small_ref[...]
        for k in range(1, N_DEV):
            small_copy(k, False).start()
        for u in range(nu):
            a, rows = part(u)
            for k in range(4):
                local(u, k).wait()
                to_sibling(u, k).wait_recv()
            for k in range(1, 4):
                ici_out[a][(k - 1,) + rows] = (own[a][(k,) + rows] + sib_land[a][(k,) + rows]).astype(BF16)
                to_chip(u, k).start()
        for u in range(nu):
            a, rows = part(u)
            acc = own[a][(0,) + rows] + sib_land[a][(0,) + rows]
            for k in range(1, 4):
                to_chip(u, k).wait_recv()
                acc = acc + ici_land[a][(k - 1,) + rows].astype(F32)
            outs[a][rows] = acc
        for k in range(1, N_DEV):
            small_copy(k, True).wait_recv()
        acc = small_land[0]
        for d in range(1, N_DEV):
            acc = acc + small_land[d]
        small_out[...] = acc
        for u in range(nu):
            for k in range(4):
                to_sibling(u, k).wait_send()
            for k in range(1, 4):
                to_chip(u, k).wait_send()
        for k in range(1, N_DEV):
            small_copy(k, False).wait_send()

    vm = pl.BlockSpec(memory_space=pltpu.VMEM)
    hbm = pl.BlockSpec(memory_space=pl.ANY)
    dma = pltpu.SemaphoreType.DMA
    return pl.pallas_call(
        body, name="reduce_scatter_grads",
        in_specs=[hbm] * n + [vm], out_specs=[vm] * (n + 1),
        out_shape=[jax.ShapeDtypeStruct(s.shape[1:], F32) for s in slabs] + [jax.ShapeDtypeStruct(small.shape, F32)],
        scratch_shapes=[pltpu.VMEM((4,) + s.shape[1:], F32) for s in slabs] * 2
        + [pltpu.VMEM((3,) + s.shape[1:], BF16) for s in slabs] * 2
        + [pltpu.VMEM((N_DEV,) + small.shape, F32)]
        + [dma((nu, 4)), dma((nu, 4)), dma((nu, 4)), dma((nu, 3)), dma((nu, 3)), dma((N_DEV - 1,)), dma((N_DEV - 1,))],
        compiler_params=pltpu.CompilerParams(vmem_limit_bytes=VMEM_LIMIT),
    )(*slabs, small)


def _adamw_math(w, g, m, v):
    m = ADAM_B1 * m + (1.0 - ADAM_B1) * g
    v = ADAM_B2 * v + (1.0 - ADAM_B2) * (g * g)
    m_hat = m / (1.0 - ADAM_B1 ** ADAM_STEP)
    v_hat = v / (1.0 - ADAM_B2 ** ADAM_STEP)
    delta = -ADAM_LR * (m_hat / (jnp.sqrt(v_hat) + ADAM_EPS) + ADAM_WD * w)
    return delta, m, v


SMALL_W = 512


def _adamw(big, small_w, small_g):
    nb, ns = len(big), len(small_w)

    def body(*refs):
        k = 0
        big_in = [refs[4 * i:4 * i + 4] for i in range(nb)]
        k = 4 * nb
        small_in = [refs[k + 3 * i:k + 3 * i + 3] for i in range(ns)]
        k += 3 * ns
        sg_ref = refs[k]
        k += 1
        big_out = [refs[k + 3 * i:k + 3 * i + 3] for i in range(nb)]
        k += 3 * nb
        small_out = [refs[k + 4 * i:k + 4 * i + 4] for i in range(ns)]

        for (w, g, m, v), (od, om, ov) in zip(big_in, big_out):
            od[...], om[...], ov[...] = _adamw_math(w[...], g[...], m[...], v[...])

        sg = sg_ref[...]
        lbp = small_in[2][0][...]
        lb = _lower_bound(lbp)
        t = sg[4:5, :] * lb * (1.0 - lb)
        grads = [jnp.concatenate([sg[0:1, :], sg[1:2, :]], axis=1),
                 jnp.concatenate([sg[2:3, :], sg[3:4, :]], axis=1),
                 jnp.concatenate([t, -t], axis=0),
                 sg[6:7, :], sg[7:8, 0:Q_RANK], sg[7:8, Q_RANK:Q_RANK + KV_RANK]]
        for (w, m, v), g, (og, od, om, ov) in zip(small_in, grads, small_out):
            og[...] = g
            od[...], om[...], ov[...] = _adamw_math(w[...], g, m[...], v[...])

    ins = [a for grp in big for a in grp] + [a for grp in small_w for a in grp] + [small_g]
    out_shape = ([jax.ShapeDtypeStruct(grp[0].shape, F32) for grp in big for _ in range(3)]
                 + [jax.ShapeDtypeStruct(grp[0].shape, F32) for grp in small_w for _ in range(4)])
    vm = pl.BlockSpec(memory_space=pltpu.VMEM)
    res = pl.pallas_call(
        body, name="adamw", in_specs=[vm] * len(ins), out_specs=[vm] * len(out_shape), out_shape=out_shape,
        compiler_params=pltpu.CompilerParams(vmem_limit_bytes=VMEM_LIMIT),
    )(*ins)
    big_res = [res[3 * i:3 * i + 3] for i in range(nb)]
    small_res = [res[3 * nb + 4 * i:3 * nb + 4 * i + 4] for i in range(ns)]
    return big_res, small_res


def _perm_weights(g_in_t, g_q, g_kv, g_out):
    w_in_t = g_in_t.reshape(D_IN, D_MODEL)
    wq = g_q.transpose(1, 0, 2)
    w_q_p = jnp.pad(wq, ((0, 0), (0, 0), (0, HEAD_LANES - NOPE - ROPE))).reshape(Q_RANK, N_HEADS * HEAD_LANES)
    wkv = g_kv.transpose(1, 0, 2)
    wk = jnp.pad(wkv[:, :, :NOPE], ((0, 0), (0, 0), (0, HEAD_LANES - NOPE))).reshape(KV_RANK, N_HEADS * HEAD_LANES)
    wv = wkv[:, :, NOPE:].reshape(KV_RANK, MLA_WIDTH)
    return w_in_t, w_q_p, jnp.concatenate([wk, wv], axis=1), g_out.reshape(D_MODEL, D_MODEL)


def _grad_slabs(dw_in_t, dw_q_p, dw_kv_p, dw_out):
    s_in = dw_in_t.reshape(N_DEV, D_IN // N_DEV, D_MODEL)
    s_q = dw_q_p.reshape(Q_RANK, N_HEADS, HEAD_LANES)[:, :, :NOPE + ROPE].transpose(1, 0, 2)
    hl = N_HEADS * HEAD_LANES
    dk = dw_kv_p[:, :hl].reshape(KV_RANK, N_HEADS, HEAD_LANES)[:, :, :NOPE]
    dv = dw_kv_p[:, hl:].reshape(KV_RANK, N_HEADS, V_DIM)
    s_kv = jnp.concatenate([dk, dv], axis=2).transpose(1, 0, 2)
    return s_in, s_q, s_kv, dw_out.reshape(N_DEV, D_MODEL // N_DEV, D_MODEL)


def _block_sizes(T):
    return min(256, T), min(256, T), min(512, T)


def kernel(x, positions, ln_g, w_in, q_a_norm_g, w_q_b, kv_a_norm_g, w_kv_b, hg_lower_bounds, hg_norm_g, w_out, final_norm_g, loss_target, m_ln_g, m_w_in, m_q_a_norm_g, m_w_q_b, m_kv_a_norm_g, m_w_kv_b, m_hg_lower_bounds, m_hg_norm_g, m_w_out, m_final_norm_g, v_ln_g, v_w_in, v_q_a_norm_g, v_w_q_b, v_kv_a_norm_g, v_w_kv_b, v_hg_lower_bounds, v_hg_norm_g, v_w_out, v_final_norm_g):
    T = x.shape[1]
    tm, tq, bt = _block_sizes(T)
    nq = T // tq
    xs, tgt = x[0], loss_target[0]
    pos_f = positions.astype(F32)
    fng = final_norm_g.reshape(1, D_MODEL)

    w_in_shard_t = w_in[0].T
    gathered, (c_t, s1_t, s2_t) = _all_gather_weights([w_in_shard_t, w_q_b[0], w_kv_b[0], w_out[0]], pos_f)
    w_in_t, w_q_p, w_kv_p, w_out_b = _perm_weights(*gathered)

    proj, h, qn, kvn, q, k, v, v_t = _fwd_in(
        xs, ln_g, w_in_t, q_a_norm_g, w_q_p, kv_a_norm_g, w_kv_p, c_t, s1_t, s2_t, bt, tq)
    hl = N_HEADS * HEAD_LANES
    k_t = k.reshape(nq, tq, hl).transpose(0, 2, 1)
    q_t = q.reshape(nq, tq, hl).transpose(0, 2, 1)
    o_mla, o_t, lse = _attn_fwd_flat(k, q_t, v_t, tq)
    o_hg, states = _hgrn_fwd(proj, hg_lower_bounds)
    dx2, do_t, d_gm, d_oh, d_gh, loss_p, d_fng, d_hgn, dw_out = _top(
        xs, tgt, o_mla, o_hg, proj, w_out_b, hg_norm_g, fng, tm, tq)
    dq_t, dk, dv = _attn_bwd_flat(k, v, q_t, k_t, do_t, o_t, lse, tq)
    dq = dq_t.transpose(0, 2, 1).reshape(T, N_HEADS * HEAD_LANES)
    d_hq, d_hf, d_hi, d_lb = _hgrn_bwd(proj, hg_lower_bounds, d_oh, states)
    dx, d_lng, d_qg, d_kvg, dw_q_p, dw_kv_p, dw_in_t = _bot(
        xs, dx2, proj, h, qn, kvn, dq, dk, dv, d_gm, d_hq, d_hf, d_hi, d_gh, c_t, s1_t, s2_t, w_in_t, w_q_p, w_kv_p,
        ln_g, q_a_norm_g, kv_a_norm_g, tm)

    small = jnp.concatenate([
        d_lng.reshape(2, SMALL_W), d_fng.reshape(2, SMALL_W), d_lb, loss_p, d_hgn,
        jnp.concatenate([d_qg, d_kvg, jnp.zeros((1, SMALL_W - Q_RANK - KV_RANK), F32)], axis=1)], axis=0)
    g_in, g_q, g_kv, g_out, small_sum = _reduce_scatter(list(_grad_slabs(dw_in_t, dw_q_p, dw_kv_p, dw_out)), small)

    big = [(w_in_shard_t, g_in, m_w_in[0].T, v_w_in[0].T), (w_q_b[0], g_q, m_w_q_b[0], v_w_q_b[0]),
           (w_kv_b[0], g_kv, m_w_kv_b[0], v_w_kv_b[0]), (w_out[0], g_out, m_w_out[0], v_w_out[0])]
    small_w = [(ln_g, m_ln_g, v_ln_g),
               (fng, m_final_norm_g.reshape(1, D_MODEL), v_final_norm_g.reshape(1, D_MODEL)),
               (hg_lower_bounds, m_hg_lower_bounds, v_hg_lower_bounds), (hg_norm_g, m_hg_norm_g, v_hg_norm_g),
               (q_a_norm_g, m_q_a_norm_g, v_q_a_norm_g), (kv_a_norm_g, m_kv_a_norm_g, v_kv_a_norm_g)]
    big_res, small_res = _adamw(big, small_w, small_sum)

    loss = small_sum[5, 0]
    (r_in, r_q, r_kv, r_out) = big_res
    (s_ln, s_fn, s_lb, s_hgn, s_qg, s_kvg) = small_res
    flat = lambda t: t.reshape(D_MODEL)
    lead = lambda t: t[None]
    grads = [s_ln[0], lead(g_in.T), s_qg[0], lead(g_q), s_kvg[0], lead(g_kv), s_lb[0], s_hgn[0], lead(g_out), flat(s_fn[0])]

    def pick(i):
        return [s_ln[i + 1], lead(r_in[i].T), s_qg[i + 1], lead(r_q[i]), s_kvg[i + 1], lead(r_kv[i]), s_lb[i + 1],
                s_hgn[i + 1], lead(r_out[i]), flat(s_fn[i + 1])]

    return (loss, dx[None], *grads, *pick(0), *pick(1), *pick(2))
```

```python
import math

import numpy as np
import jax
import jax.numpy as jnp
from jax import lax
from jax.experimental import pallas as pl
from jax.experimental.pallas import tpu as pltpu

F32 = jnp.float32
BF16 = jnp.bfloat16

D_MODEL = 1024
N_HEADS = 8
NOPE = 64
ROPE = 32
HALF_ROPE = ROPE // 2
V_DIM = 64
Q_RANK = 256
KV_RANK = 128
MLA_WIDTH = N_HEADS * V_DIM
HG_HEADS = 4
HG_DIM = 128
HG_WIDTH = HG_HEADS * HG_DIM
CHUNK = 64
SUB = 16
D_IN = 2976
D_PERM = 3072
ROPE_THETA = 10000.0
EPS = 1e-6
N_DEV = 8
LANES = 128
HEAD_LANES = 128

P_GM, P_HQ, P_HF, P_HI, P_GH, P_QL, P_KVL, P_KR = 0, 512, 1024, 1536, 2048, 2560, 2816, 2944
R_QL, R_KVL, R_KR, R_MAIN = (0, 256), (256, 384), (384, 416), (416, 2976)
ROPE_LO = NOPE
SCALE = 1.0 / math.sqrt(NOPE + ROPE)

ADAM_LR = 0.001
ADAM_B1 = 0.9
ADAM_B2 = 0.999
ADAM_EPS = 1e-08
ADAM_WD = 0.01
ADAM_STEP = 10

VMEM_LIMIT = 56 * 1024 * 1024
MESH = pl.DeviceIdType.MESH

NT = (((1,), (1,)), ((), ()))
TN = (((0,), (0,)), ((), ()))


def _params(n_grid=0, **kw):
    sem = ("arbitrary",) * n_grid if n_grid else None
    return pltpu.CompilerParams(dimension_semantics=sem, vmem_limit_bytes=VMEM_LIMIT, **kw)


def _dot(a, b):
    return jnp.dot(a, b, preferred_element_type=F32)


def _dot_nt(a, b):
    return lax.dot_general(a, b, NT, preferred_element_type=F32)


def _dot_tn(a, b):
    return lax.dot_general(a, b, TN, preferred_element_type=F32)


def _sigmoid(x):
    return 1.0 / (1.0 + jnp.exp(-x))


def _rope_fwd(x, c, s1, s2):
    return x * c + pltpu.roll(x, LANES - HALF_ROPE, 1) * s1 + pltpu.roll(x, HALF_ROPE, 1) * s2


def _rope_bwd(dy, c, s1, s2):
    return dy * c - pltpu.roll(dy, LANES - HALF_ROPE, 1) * s1 - pltpu.roll(dy, HALF_ROPE, 1) * s2


def _in_proj_rows(wt_ref):
    kr = wt_ref[R_KR[0]:R_KR[1], :]
    pad = lambda n: jnp.zeros((n, D_MODEL), kr.dtype)
    return ((P_QL, wt_ref[R_QL[0]:R_QL[1], :]), (P_KVL, wt_ref[R_KVL[0]:R_KVL[1], :]),
            (P_KR, jnp.concatenate([pad(ROPE_LO), kr, pad(LANES - ROPE_LO - ROPE)], axis=0)),
            (P_GM, wt_ref[R_MAIN[0]:R_MAIN[1], :]))


def _full(shape):
    n = len(shape)
    return pl.BlockSpec(shape, lambda *_: (0,) * n)


ROPE_BLOCK = 512


def _rope_constants():
    inv = (np.float32(ROPE_THETA) ** (-np.arange(HALF_ROPE, dtype=np.float32) / np.float32(HALF_ROPE))).astype(np.float32)
    place = np.zeros((3, HALF_ROPE, LANES), np.float32)
    for i in range(HALF_ROPE):
        place[0, i, ROPE_LO + i] = place[0, i, ROPE_LO + HALF_ROPE + i] = 1.0
        place[1, i, ROPE_LO + i] = -1.0
        place[2, i, ROPE_LO + HALF_ROPE + i] = 1.0
    base = np.ones((1, LANES), np.float32)
    base[0, ROPE_LO:ROPE_LO + ROPE] = 0.0
    return jnp.asarray(inv.reshape(HALF_ROPE, 1)), jnp.asarray(place), jnp.asarray(base)


def _rope_block(pos, inv, place_ref, base):
    ang = inv * pos
    cos, sin = jnp.cos(ang), jnp.sin(ang)

    def put(v, k):
        return lax.dot_general(v, place_ref[k], TN, precision=lax.Precision.HIGHEST, preferred_element_type=F32)

    return put(cos, 0) + base, put(sin, 1), put(sin, 2)


def _all_gather_weights(shards, pos_f):
    n = len(shards)
    T = pos_f.shape[1]
    rb = min(ROPE_BLOCK, T)

    def body(*refs):
        ins, (pos_ref, inv_ref, place_ref, base_ref) = refs[:n], refs[n:n + 4]
        outs, tables = refs[n + 4:2 * n + 4], refs[2 * n + 4:2 * n + 7]
        send_sems, recv_sems = refs[2 * n + 7], refs[2 * n + 8]
        x, y, c = lax.axis_index("x"), lax.axis_index("y"), lax.axis_index("c")
        me, sibling = (x, y, c), (x, y, 1 - c)
        chips = [(1 - x, y), (x, 1 - y), (1 - x, 1 - y)]

        def idx(d):
            return 4 * d[0] + 2 * d[1] + d[2]

        def copy(a, k, block, to):
            rows = outs[a].at[idx(block)]
            return pltpu.make_async_remote_copy(src_ref=rows, dst_ref=rows, send_sem=send_sems.at[a, k],
                                                recv_sem=recv_sems.at[a, k], device_id=to, device_id_type=MESH)

        for a in range(n):
            outs[a][idx(me)] = ins[a][...].astype(BF16)
        first = []
        for a in range(n):
            first.append(copy(a, 0, me, sibling))
            first += [copy(a, 1 + j, me, (*chip, c)) for j, chip in enumerate(chips)]
        for cp in first:
            cp.start()
        for r0 in range(0, T, rb):
            for ref, tab in zip(tables, _rope_block(pos_ref[:, r0:r0 + rb], inv_ref[...], place_ref, base_ref[...])):
                ref[r0:r0 + rb, :] = tab
        passed = []
        for j, chip in enumerate(chips):
            for a in range(n):
                copy(a, 1 + j, (*chip, c), me).wait_recv()
                cp = copy(a, 4 + j, (*chip, c), sibling)
                cp.start()
                passed.append(cp)
        for a in range(n):
            copy(a, 0, sibling, me).wait_recv()
            for j, chip in enumerate(chips):
                copy(a, 4 + j, (*chip, 1 - c), me).wait_recv()
        for cp in first + passed:
            cp.wait_send()

    vm = pl.BlockSpec(memory_space=pltpu.VMEM)
    res = pl.pallas_call(
        body, name="all_gather_weights",
        in_specs=[vm] * (n + 4), out_specs=[vm] * (n + 3),
        out_shape=[jax.ShapeDtypeStruct((N_DEV,) + s.shape, BF16) for s in shards]
        + [jax.ShapeDtypeStruct((T, LANES), F32)] * 3,
        scratch_shapes=[pltpu.SemaphoreType.DMA((n, 7)), pltpu.SemaphoreType.DMA((n, 7))],
        compiler_params=pltpu.CompilerParams(vmem_limit_bytes=VMEM_LIMIT),
    )(*shards, pos_f, *_rope_constants())
    return res[:n], res[n:]


def _fwd_in(x, ln_g, w_in_t, q_g, w_q_p, kv_g, w_kv_p, c_t, s1_t, s2_t, tm, tq):
    T = x.shape[0]
    assert tm % tq == 0

    def body(x_ref, lng_ref, win_ref, qg_ref, wq_ref, kvg_ref, wkv_ref, c_ref, s1_ref, s2_ref,
             proj_ref, h_ref, qn_ref, kvn_ref, q_ref, k_ref, v_ref, vt_ref):
        xv = x_ref[...]
        r = lax.rsqrt(jnp.mean(xv * xv, axis=-1, keepdims=True) + EPS)
        h = (xv * r * lng_ref[...]).astype(BF16)
        h_ref[...] = h
        for col, rows in _in_proj_rows(win_ref):
            proj_ref[:, col:col + rows.shape[0]] = _dot_nt(h, rows)
        c, s1, s2 = c_ref[...], s1_ref[...], s2_ref[...]

        ql = proj_ref[:, P_QL:P_QL + Q_RANK]
        rq = lax.rsqrt(jnp.mean(ql * ql, axis=-1, keepdims=True) + EPS)
        qn = (ql * rq * qg_ref[...]).astype(BF16)
        qn_ref[...] = qn
        q = _dot(qn, wq_ref[...])
        for hd in range(N_HEADS):
            sl = slice(hd * HEAD_LANES, (hd + 1) * HEAD_LANES)
            q_ref[:, sl] = _rope_fwd(q[:, sl], c, s1, s2).astype(BF16)

        kvl = proj_ref[:, P_KVL:P_KVL + KV_RANK]
        rk = lax.rsqrt(jnp.mean(kvl * kvl, axis=-1, keepdims=True) + EPS)
        kvn = (kvl * rk * kvg_ref[...]).astype(BF16)
        kvn_ref[...] = kvn
        kv = _dot(kvn, wkv_ref[...])
        kpe = _rope_fwd(proj_ref[:, P_KR:P_KR + LANES], c, s1, s2)
        for hd in range(N_HEADS):
            sl = slice(hd * HEAD_LANES, (hd + 1) * HEAD_LANES)
            k_ref[:, sl] = (kv[:, sl] + kpe).astype(BF16)
        v_ref[...] = kv[:, N_HEADS * HEAD_LANES:].astype(BF16)
        for b in range(tm // tq):
            vt_ref[b] = kv[b * tq:(b + 1) * tq, N_HEADS * HEAD_LANES:].T.astype(BF16)

    def row(w):
        return pl.BlockSpec((tm, w), lambda i: (i, 0))

    outs = [(D_PERM, F32), (D_MODEL, BF16), (Q_RANK, BF16), (KV_RANK, BF16),
            (N_HEADS * HEAD_LANES, BF16), (N_HEADS * HEAD_LANES, BF16), (MLA_WIDTH, BF16)]
    return pl.pallas_call(
        body, name="fwd_in", grid=(T // tm,),
        in_specs=[row(D_MODEL), _full((1, D_MODEL)), _full((D_IN, D_MODEL)), _full((1, Q_RANK)),
                  _full((Q_RANK, N_HEADS * HEAD_LANES)), _full((1, KV_RANK)),
                  _full((KV_RANK, N_HEADS * HEAD_LANES + MLA_WIDTH)), row(LANES), row(LANES), row(LANES)],
        out_specs=[row(w) for w, _ in outs] + [pl.BlockSpec((tm // tq, MLA_WIDTH, tq), lambda i: (i, 0, 0))],
        out_shape=[jax.ShapeDtypeStruct((T, w), dt) for w, dt in outs]
        + [jax.ShapeDtypeStruct((T // tq, MLA_WIDTH, tq), BF16)],
        compiler_params=_params(1),
    )(x, ln_g, w_in_t, q_g, w_q_p, kv_g, w_kv_p, c_t, s1_t, s2_t)


LOG2E = 1.4426950408889634
SCALE2 = SCALE * LOG2E


def _causal(tq):
    r = lax.broadcasted_iota(jnp.int32, (tq, tq), 0)
    c = lax.broadcasted_iota(jnp.int32, (tq, tq), 1)
    return r <= c


MASKED = -1e30


def _causal_bias(bias_ref, tq):
    bias_ref[0] = jnp.zeros((tq, tq), F32)
    bias_ref[1] = jnp.where(_causal(tq), 0.0, MASKED)


def _tile_tables(nq, by_query):
    if by_query:
        pairs = [(j, i) for i in range(nq) for j in range(i + 1)]
    else:
        pairs = [(j, i) for j in range(nq) for i in range(nq - 1, j - 1, -1)]
    pairs.append(pairs[-1])
    jj, ii = np.array(pairs, np.int32).T
    return jnp.asarray(jj), jnp.asarray(ii), len(pairs) - 1


ATTN_TRIP = 8


def _walk_tiles(n, products, tile, flush, buf_a, buf_b):
    bufs = (buf_a, buf_b)
    products(0, buf_a)

    def trip(r, carry):
        for u in range(ATTN_TRIP):
            products(ATTN_TRIP * r + u + 1, bufs[(u + 1) % 2])
            tile(ATTN_TRIP * r + u, bufs[u % 2])
        for u in range(ATTN_TRIP):
            flush(ATTN_TRIP * r + u)
        return carry

    lax.fori_loop(0, n // ATTN_TRIP, trip, 0)
    rest = n - n % ATTN_TRIP
    for u in range(n % ATTN_TRIP):
        if rest + u + 1 < n:
            products(rest + u + 1, bufs[(u + 1) % 2])
        tile(rest + u, bufs[u % 2])
    for u in range(n % ATTN_TRIP):
        flush(rest + u)


V_ROWS = V_DIM + 16


def _attn_fwd_flat(k, q_t, v_t, tq):
    T = k.shape[0]
    nq = T // tq
    jj, ii, n = _tile_tables(nq, True)
    heads = [slice(hh * HEAD_LANES, (hh + 1) * HEAD_LANES) for hh in range(2)]

    def body(jj_ref, ii_ref, k_ref, qt_ref, vt_ref, o_ref, ot_ref, lse_ref, sa_ref, sb_ref, m_ref, acc_ref, bias_ref):
        def reset(st):
            m_ref[st] = jnp.full(m_ref.shape[1:], MASKED, F32)
            acc_ref[st] = jnp.zeros(acc_ref.shape[1:], F32)

        _causal_bias(bias_ref, tq)
        for st in range(ATTN_TRIP):
            reset(st)
        extra = (lax.broadcasted_iota(jnp.int32, (V_ROWS - V_DIM, tq), 0) == 0).astype(BF16)

        def products(t, buf):
            j, i = jj_ref[t], ii_ref[t]
            kj = k_ref[pl.ds(pl.multiple_of(j * tq, tq), tq), :]
            for hh, sl in enumerate(heads):
                buf[hh] = _dot(kj[:, sl], qt_ref[i, sl, :])

        def tile(t, buf):
            j, i = jj_ref[t], ii_ref[t]
            vt = vt_ref[j]
            bias = bias_ref.at[(j == i).astype(jnp.int32)]
            st = i % ATTN_TRIP
            for hh in range(2):
                s = buf[hh] * SCALE2 + bias[...]
                m = m_ref[st, hh]
                m_new = jnp.maximum(m, jnp.max(s, axis=0, keepdims=True))
                alpha = jnp.exp2(m - m_new)
                p = jnp.exp2(s - m_new)
                m_ref[st, hh] = m_new
                v_h = jnp.concatenate([vt[hh * V_DIM:(hh + 1) * V_DIM, :], extra], axis=0)
                acc_ref[st, hh] = alpha * acc_ref[st, hh] + _dot(v_h, p.astype(BF16))

        def flush(t):
            j, i = jj_ref[t], ii_ref[t]

            @pl.when(j == i)
            def _():
                st = i % ATTN_TRIP
                den = [acc_ref[st, hh, V_DIM:V_DIM + 1, :] for hh in range(2)]
                out = jnp.concatenate([acc_ref[st, hh, :V_DIM, :] / den[hh] for hh in range(2)], axis=0)
                o_ref[pl.ds(pl.multiple_of(i * tq, tq), tq), :] = out.T
                ot_ref[i] = out
                for hh in range(2):
                    lse_ref[hh, i] = m_ref[st, hh] + jnp.log2(den[hh])
                reset(st)

        _walk_tiles(n, products, tile, flush, sa_ref, sb_ref)

    smem = pl.BlockSpec(memory_space=pltpu.SMEM)
    return pl.pallas_call(
        body, name="attn_fwd", grid=(N_HEADS // 2,),
        in_specs=[smem, smem,
                  pl.BlockSpec((T, 2 * HEAD_LANES), lambda p: (0, p)),
                  pl.BlockSpec((nq, 2 * HEAD_LANES, tq), lambda p: (0, p, 0)),
                  pl.BlockSpec((nq, LANES, tq), lambda p: (0, p, 0))],
        out_specs=[pl.BlockSpec((T, LANES), lambda p: (0, p)), pl.BlockSpec((nq, LANES, tq), lambda p: (0, p, 0)),
                   pl.BlockSpec((2, nq, 1, tq), lambda p: (p, 0, 0, 0))],
        out_shape=[jax.ShapeDtypeStruct((T, MLA_WIDTH), F32), jax.ShapeDtypeStruct((nq, MLA_WIDTH, tq), F32),
                   jax.ShapeDtypeStruct((N_HEADS, nq, 1, tq), F32)],
        scratch_shapes=[pltpu.VMEM((2, tq, tq), F32), pltpu.VMEM((2, tq, tq), F32),
                        pltpu.VMEM((ATTN_TRIP, 2, 1, tq), F32),
                        pltpu.VMEM((ATTN_TRIP, 2, V_ROWS, tq), F32), pltpu.VMEM((2, tq, tq), F32)],
        compiler_params=_params(1),
    )(jj, ii, k, q_t, v_t)


def _attn_bwd_flat(k, v, q_t, k_t, do_t, o_t, lse, tq):
    T = k.shape[0]
    nq = T // tq
    jj, ii, n = _tile_tables(nq, False)
    heads = [slice(hh * HEAD_LANES, (hh + 1) * HEAD_LANES) for hh in range(2)]

    def body(jj_ref, ii_ref, k_ref, v_ref, qt_ref, kt_ref, dot_ref, ot_ref, lse_ref, dqt_ref, dk_ref, dv_ref,
             ba_ref, bb_ref, dkt_ref, dvt_ref, bias_ref, dsum_ref):
        _causal_bias(bias_ref, tq)

        def row_dots(i, carry):
            for hh in range(2):
                own = slice(hh * V_DIM, (hh + 1) * V_DIM)
                do_h = dot_ref[i, hh * HEAD_LANES + own.start:hh * HEAD_LANES + own.stop, :]
                dsum_ref[hh, i] = jnp.sum(do_h.astype(F32) * ot_ref[i, own, :], axis=0, keepdims=True)
            return carry

        lax.fori_loop(0, nq, row_dots, 0)
        dqt_ref[...] = jnp.zeros_like(dqt_ref)
        dkt_ref[...] = jnp.zeros_like(dkt_ref)
        dvt_ref[...] = jnp.zeros_like(dvt_ref)

        def products(t, buf):
            j, i = jj_ref[t], ii_ref[t]
            rows = pl.ds(pl.multiple_of(j * tq, tq), tq)
            for hh, sl in enumerate(heads):
                buf[hh] = _dot(k_ref[rows, sl], qt_ref[i, sl, :])
                buf[2 + hh] = _dot(v_ref[rows, :], dot_ref[i, sl, :])

        def tile(t, buf):
            j, i = jj_ref[t], ii_ref[t]
            bias = bias_ref.at[(j == i).astype(jnp.int32)]
            st = j % ATTN_TRIP
            for hh, sl in enumerate(heads):
                p = jnp.exp2(buf[hh] * SCALE2 + bias[...] - lse_ref[hh, i])
                ds = (p * (buf[2 + hh] - dsum_ref[hh, i]) * SCALE).astype(BF16)
                own = slice(hh * V_DIM, (hh + 1) * V_DIM)
                do_h = dot_ref[i, hh * HEAD_LANES + own.start:hh * HEAD_LANES + own.stop, :]
                dvt_ref[st, own, :] += _dot_nt(do_h, p.astype(BF16))
                used = slice(sl.start, sl.start + NOPE + ROPE)
                dkt_ref[st, used, :] += _dot_nt(qt_ref[i, used, :], ds)
                dqt_ref[i, used, :] += _dot(kt_ref[j, used, :], ds)

        def flush(t):
            j, i = jj_ref[t], ii_ref[t]

            @pl.when(j == i)
            def _():
                st = j % ATTN_TRIP
                rows = pl.ds(pl.multiple_of(j * tq, tq), tq)
                dk_ref[rows, :] = dkt_ref[st].T
                dv_ref[rows, :] = dvt_ref[st].T
                dkt_ref[st] = jnp.zeros(dkt_ref.shape[1:], F32)
                dvt_ref[st] = jnp.zeros(dvt_ref.shape[1:], F32)

        _walk_tiles(n, products, tile, flush, ba_ref, bb_ref)

    smem = pl.BlockSpec(memory_space=pltpu.SMEM)
    stat = pl.BlockSpec((2, nq, 1, tq), lambda p: (p, 0, 0, 0))
    blocks_t = pl.BlockSpec((nq, 2 * HEAD_LANES, tq), lambda p: (0, p, 0))
    return pl.pallas_call(
        body, name="attn_bwd", grid=(N_HEADS // 2,),
        in_specs=[smem, smem,
                  pl.BlockSpec((T, 2 * HEAD_LANES), lambda p: (0, p)),
                  pl.BlockSpec((T, LANES), lambda p: (0, p)),
                  blocks_t, blocks_t, blocks_t, pl.BlockSpec((nq, LANES, tq), lambda p: (0, p, 0)), stat],
        out_specs=[blocks_t,
                   pl.BlockSpec((T, 2 * HEAD_LANES), lambda p: (0, p)),
                   pl.BlockSpec((T, LANES), lambda p: (0, p))],
        out_shape=[jax.ShapeDtypeStruct((nq, N_HEADS * HEAD_LANES, tq), F32),
                   jax.ShapeDtypeStruct((T, N_HEADS * HEAD_LANES), F32),
                   jax.ShapeDtypeStruct((T, MLA_WIDTH), F32)],
        scratch_shapes=[pltpu.VMEM((4, tq, tq), F32), pltpu.VMEM((4, tq, tq), F32),
                        pltpu.VMEM((ATTN_TRIP, 2 * HEAD_LANES, tq), F32), pltpu.VMEM((ATTN_TRIP, LANES, tq), F32),
                        pltpu.VMEM((2, tq, tq), F32), pltpu.VMEM((2, nq, 1, tq), F32)],
        compiler_params=_params(1),
    )(jj, ii, k, v, q_t, k_t, do_t, o_t, lse)


def _lower_bound(lbp):
    a, b = lbp[0:1, :], lbp[1:2, :]
    mx = jnp.maximum(a, b)
    ea, eb = jnp.exp(a - mx), jnp.exp(b - mx)
    return ea / (ea + eb)


def _tri(lower):
    r = lax.broadcasted_iota(jnp.int32, (CHUNK, CHUNK), 0)
    c = lax.broadcasted_iota(jnp.int32, (CHUNK, CHUNK), 1)
    return (c <= r) if lower else (c >= r)


def _running_sum(x, from_end):
    row = lax.broadcasted_iota(jnp.int32, x.shape, 0)
    step = 1
    while step < CHUNK:
        if from_end:
            x = x + jnp.where(row < CHUNK - step, pltpu.roll(x, CHUNK - step, 0), 0.0)
        else:
            x = x + jnp.where(row >= step, pltpu.roll(x, step, 0), 0.0)
        step *= 2
    return x


def _hg_gates(hq, hf, lb):
    sq = _sigmoid(hq)
    sf = _sigmoid(hf)
    f = lb + (1.0 - lb) * sf
    g = jnp.log(f)
    gcum = _running_sum(g, False)
    return sq, sf, f, hq * sq, 1.0 - f, gcum


def _head(x, hd):
    return x[:, hd * HG_DIM:(hd + 1) * HG_DIM]


def _all_heads(fn):
    return jnp.concatenate([fn(hd) for hd in range(HG_HEADS)], axis=1)


def _hg_blocks(q, kk, gcum):
    rowi = lax.broadcasted_iota(jnp.int32, gcum.shape, 0)
    out = []
    for blk in range(CHUNK // SUB):
        lo, hi = blk * SUB, (blk + 1) * SUB
        gb = gcum[lo - 1:lo, :] if blk else jnp.zeros_like(gcum[0:1, :])
        eq = jnp.exp(gcum[lo:hi, :] - gb)
        ek = jnp.exp(jnp.where(rowi < hi, gb - gcum, 0.0))
        out.append((eq, ek, (q[lo:hi, :] * eq).astype(BF16), (kk * ek).astype(BF16)))
    return out


def _hg_scores(blocks):
    out = []
    for hd in range(HG_HEADS):
        a = jnp.concatenate([_dot_nt(_head(qb, hd), _head(kb, hd)) for _, _, qb, kb in blocks], axis=0)
        out.append(jnp.where(_tri(True), a, 0.0))
    return out


HG_STEP_CHUNKS = 8


def _hgrn_fwd(proj, lbp):
    T = proj.shape[0]
    nc = T // CHUNK
    ns = min(HG_STEP_CHUNKS, nc)
    rows = ns * CHUNK

    def body(hq_ref, hf_ref, hi_ref, lbp_ref, o_ref, st_ref, state):
        @pl.when(pl.program_id(0) == 0)
        def _():
            state[...] = jnp.zeros_like(state)

        lb = _lower_bound(lbp_ref[...])
        work = []
        for c in range(ns):
            r = slice(c * CHUNK, (c + 1) * CHUNK)
            _, _, _, q, kk, gcum = _hg_gates(hq_ref[r, :], hf_ref[r, :], lb)
            vb = hi_ref[r, :].astype(BF16)
            a = _hg_scores(_hg_blocks(q, kk, gcum))
            gend = gcum[CHUNK - 1:CHUNK, :]
            qgb = (q * jnp.exp(gcum)).astype(BF16)
            kgeb = (kk * jnp.exp(gend - gcum)).astype(BF16)
            intra = [_dot(a[hd].astype(BF16), _head(vb, hd)) for hd in range(HG_HEADS)]
            update = [_dot_tn(_head(vb, hd), _head(kgeb, hd)) for hd in range(HG_HEADS)]
            work.append((qgb, jnp.exp(gend), intra, update))
        for hd in range(HG_HEADS):
            st = state[hd]
            for c, (qgb, egend, intra, update) in enumerate(work):
                st_ref[c, hd] = st
                o_ref[c * CHUNK:(c + 1) * CHUNK, hd * HG_DIM:(hd + 1) * HG_DIM] = (
                    intra[hd] + _dot_nt(_head(qgb, hd), st.astype(BF16)))
                st = st * _head(egend, hd) + update[hd]
            state[hd] = st

    def col(cb):
        return pl.BlockSpec((rows, HG_WIDTH), lambda i: (i, cb))

    return pl.pallas_call(
        body, name="hgrn_fwd", grid=(nc // ns,),
        in_specs=[col(P_HQ // HG_WIDTH), col(P_HF // HG_WIDTH), col(P_HI // HG_WIDTH), _full((2, HG_WIDTH))],
        out_specs=[pl.BlockSpec((rows, HG_WIDTH), lambda i: (i, 0)),
                   pl.BlockSpec((ns, HG_HEADS, HG_DIM, HG_DIM), lambda i: (i, 0, 0, 0))],
        out_shape=[jax.ShapeDtypeStruct((T, HG_WIDTH), F32),
                   jax.ShapeDtypeStruct((nc, HG_HEADS, HG_DIM, HG_DIM), F32)],
        scratch_shapes=[pltpu.VMEM((HG_HEADS, HG_DIM, HG_DIM), F32)],
        compiler_params=_params(1),
    )(proj, proj, proj, lbp)


def _hgrn_bwd(proj, lbp, do_hg, states):
    T = proj.shape[0]
    nc = T // CHUNK
    ns = min(HG_STEP_CHUNKS, nc)
    rows = ns * CHUNK
    steps = nc // ns

    def body(hq_ref, hf_ref, hi_ref, lbp_ref, do_ref, st_ref, dhq_ref, dhf_ref, dhi_ref, dlb_ref, dstate):
        @pl.when(pl.program_id(0) == 0)
        def _():
            dstate[...] = jnp.zeros_like(dstate)
            dlb_ref[...] = jnp.zeros_like(dlb_ref)

        lb = _lower_bound(lbp_ref[...])

        dst_all = [dstate[hd] for hd in range(HG_HEADS)]
        dlb = jnp.zeros_like(lb)
        last = lax.broadcasted_iota(jnp.int32, (CHUNK, HG_WIDTH), 0) == CHUNK - 1
        for c in reversed(range(ns)):
            r = slice(c * CHUNK, (c + 1) * CHUNK)
            hq = hq_ref[r, :]
            sq, sf, f, q, kk, gcum = _hg_gates(hq, hf_ref[r, :], lb)
            vb = hi_ref[r, :].astype(BF16)
            dob = do_ref[r, :].astype(BF16)
            blocks = _hg_blocks(q, kk, gcum)
            a = _hg_scores(blocks)
            gend = gcum[CHUNK - 1:CHUNK, :]
            eg, egend, ekend = jnp.exp(gcum), jnp.exp(gend), jnp.exp(gend - gcum)
            qg, kge = q * eg, kk * ekend
            qgb, kgeb = qg.astype(BF16), kge.astype(BF16)

            dv, dqg, dkge, st_dst, dq_blk, dk_blk = [], [], [], [], [], []
            for hd in range(HG_HEADS):
                st = st_ref[c, hd]
                dst = dst_all[hd]
                dstb = dst.astype(BF16)
                do_h, v_h = _head(dob, hd), _head(vb, hd)
                dv.append(_dot_tn(a[hd].astype(BF16), do_h) + _dot_nt(_head(kgeb, hd), dstb))
                da = jnp.where(_tri(True), _dot_nt(do_h, v_h), 0.0).astype(BF16)
                dqg.append(_dot(do_h, st.astype(BF16)))
                dkge.append(_dot(v_h, dstb))
                st_dst.append(jnp.sum(st * dst, axis=0, keepdims=True))
                dst_all[hd] = _dot_tn(do_h, _head(qgb, hd)) + dst * _head(egend, hd)
                dq_blk.append([_dot(da[b * SUB:(b + 1) * SUB, :], _head(kb, hd)) for b, (_, _, _, kb) in enumerate(blocks)])
                dk_blk.append([_dot_tn(da[b * SUB:(b + 1) * SUB, :], _head(qb, hd)) for b, (_, _, qb, _) in enumerate(blocks)])
            dv, dqg, dkge, st_dst = (jnp.concatenate(t, axis=1) for t in (dv, dqg, dkge, st_dst))

            dq_a, dg_q = [], []
            dk_a, dg_k = jnp.zeros_like(gcum), jnp.zeros_like(gcum)
            for b, (eq, ek, qb, kb) in enumerate(blocks):
                dq_b = _all_heads(lambda hd: dq_blk[hd][b])
                dk_b = _all_heads(lambda hd: dk_blk[hd][b])
                dq_a.append(dq_b * eq)
                dk_a = dk_a + dk_b * ek
                dg_q.append(qb.astype(F32) * dq_b)
                dg_k = dg_k + kb.astype(F32) * dk_b
            dq_a = jnp.concatenate(dq_a, axis=0)

            dgend = st_dst * egend + jnp.sum(dkge * kge, axis=0, keepdims=True)
            dq = dq_a + dqg * eg
            dk = dk_a + dkge * ekend
            dgc = jnp.concatenate(dg_q, axis=0) - dg_k + dqg * qg - dkge * kge + jnp.where(last, dgend, 0.0)
            dg = _running_sum(dgc, True)
            df = dg / f - dk
            dhf_ref[r, :] = df * (1.0 - lb) * sf * (1.0 - sf)
            dlb = dlb + jnp.sum(df * (1.0 - sf), axis=0, keepdims=True)
            dhq_ref[r, :] = dq * (sq * (1.0 + hq * (1.0 - sq)))
            dhi_ref[r, :] = dv
        for hd in range(HG_HEADS):
            dstate[hd] = dst_all[hd]
        dlb_ref[...] += dlb

    def col(cb):
        return pl.BlockSpec((rows, HG_WIDTH), lambda i: (steps - 1 - i, cb))

    grad = jax.ShapeDtypeStruct((T, HG_WIDTH), F32)
    return pl.pallas_call(
        body, name="hgrn_bwd", grid=(steps,),
        in_specs=[col(P_HQ // HG_WIDTH), col(P_HF // HG_WIDTH), col(P_HI // HG_WIDTH), _full((2, HG_WIDTH)),
                  col(0), pl.BlockSpec((ns, HG_HEADS, HG_DIM, HG_DIM), lambda i: (steps - 1 - i, 0, 0, 0))],
        out_specs=[col(0), col(0), col(0), _full((1, HG_WIDTH))],
        out_shape=[grad, grad, grad, jax.ShapeDtypeStruct((1, HG_WIDTH), F32)],
        scratch_shapes=[pltpu.VMEM((HG_HEADS, HG_DIM, HG_DIM), F32)],
        compiler_params=_params(1),
    )(proj, proj, proj, lbp, do_hg, states)


def _top(x, tgt, o_mla, o_hg, proj, w_out, hg_norm_g, final_g, tm, tq):
    T = x.shape[0]
    assert tm % tq == 0

    def body(x_ref, tgt_ref, om_ref, oh_ref, gm_ref, gh_ref, wout_ref, hgn_ref, fng_ref,
             dx2_ref, dot_ref, dgm_ref, doh_ref, dgh_ref, loss_ref, dfng_ref, dhgn_ref, dwout_ref, ycat_ref):
        @pl.when(pl.program_id(0) == 0)
        def _():
            for ref in (loss_ref, dfng_ref, dhgn_ref, dwout_ref):
                ref[...] = jnp.zeros_like(ref)

        gm, om = gm_ref[...], om_ref[...]
        sgm = _sigmoid(gm)
        silu_m = gm * sgm
        gh, oh, gam = gh_ref[...], oh_ref[...], hgn_ref[...]
        sgh = _sigmoid(gh)
        silu_h = gh * sgh
        rr, nn = [], []
        for hd in range(HG_HEADS):
            oh_h = oh[:, hd * HG_DIM:(hd + 1) * HG_DIM]
            r_h = lax.rsqrt(jnp.mean(oh_h * oh_h, axis=-1, keepdims=True) + EPS)
            rr.append(r_h)
            nn.append(oh_h * r_h)
        n = jnp.concatenate(nn, axis=1)
        ng = n * gam
        ycat_ref[:, :MLA_WIDTH] = (om * silu_m).astype(BF16)
        ycat_ref[:, MLA_WIDTH:] = (ng * silu_h).astype(BF16)
        wout = wout_ref[...]
        x2 = x_ref[...] + (_dot(ycat_ref[:, :MLA_WIDTH], wout[:MLA_WIDTH, :])
                           + _dot(ycat_ref[:, MLA_WIDTH:], wout[MLA_WIDTH:, :]))
        r = lax.rsqrt(jnp.mean(x2 * x2, axis=-1, keepdims=True) + EPS)
        xh = x2 * r
        fng = fng_ref[...]
        err = xh * fng - tgt_ref[...]
        loss_ref[...] += 0.5 * jnp.sum(jnp.mean(err * err, axis=-1, keepdims=True), axis=0, keepdims=True)
        dout = err * (1.0 / D_MODEL)
        dfng_ref[...] += jnp.sum(dout * xh, axis=0, keepdims=True)
        dxh = dout * fng
        dx2 = r * (dxh - xh * jnp.mean(dxh * xh, axis=-1, keepdims=True))
        dx2_ref[...] = dx2
        dx2b = dx2.astype(BF16)
        dwout_ref[...] += _dot_tn(ycat_ref[...], dx2b)
        dym, dyh = _dot_nt(dx2b, wout[:MLA_WIDTH, :]), _dot_nt(dx2b, wout[MLA_WIDTH:, :])
        dom = dym * silu_m
        first = lax.broadcasted_iota(jnp.int32, (tm, LANES), 1) < V_DIM
        for hd in range(N_HEADS):
            pair = dom[:, hd // 2 * LANES:(hd // 2 + 1) * LANES]
            own = jnp.where(first, pair, 0.0) if hd % 2 == 0 else jnp.where(first, 0.0, pair)
            for b in range(tm // tq):
                dot_ref[b, hd * HEAD_LANES:(hd + 1) * HEAD_LANES, :] = own[b * tq:(b + 1) * tq, :].T.astype(BF16)
        dgm_ref[...] = dym * om * (sgm * (1.0 + gm * (1.0 - sgm)))
        dgh_ref[...] = dyh * ng * (sgh * (1.0 + gh * (1.0 - sgh)))
        dng = dyh * silu_h
        dhgn_ref[...] += jnp.sum(dng * n, axis=0, keepdims=True)
        dn = dng * gam
        for hd in range(HG_HEADS):
            sl = slice(hd * HG_DIM, (hd + 1) * HG_DIM)
            dn_h, n_h = dn[:, sl], nn[hd]
            doh_ref[:, sl] = rr[hd] * (dn_h - n_h * jnp.mean(dn_h * n_h, axis=-1, keepdims=True))

    def row(w, cb=0):
        return pl.BlockSpec((tm, w), lambda i: (i, cb))

    hl = N_HEADS * HEAD_LANES
    blocks_t = pl.BlockSpec((tm // tq, hl, tq), lambda i: (i, 0, 0))
    outs = [(D_MODEL, F32), (0, BF16), (MLA_WIDTH, F32), (HG_WIDTH, F32), (HG_WIDTH, F32)]
    small = [(1, SMALL_W), (1, D_MODEL), (1, HG_WIDTH), (D_MODEL, D_MODEL)]
    return pl.pallas_call(
        body, name="top", grid=(T // tm,),
        in_specs=[row(D_MODEL), row(D_MODEL), row(MLA_WIDTH), row(HG_WIDTH),
                  row(MLA_WIDTH, P_GM // MLA_WIDTH), row(HG_WIDTH, P_GH // HG_WIDTH),
                  _full((D_MODEL, D_MODEL)), _full((1, HG_WIDTH)), _full((1, D_MODEL))],
        out_specs=[row(w) if w else blocks_t for w, _ in outs] + [_full(s) for s in small],
        out_shape=[jax.ShapeDtypeStruct((T, w) if w else (T // tq, hl, tq), dt) for w, dt in outs]
        + [jax.ShapeDtypeStruct(s, F32) for s in small],
        scratch_shapes=[pltpu.VMEM((tm, D_MODEL), BF16)],
        compiler_params=_params(1),
    )(x, tgt, o_mla, o_hg, proj, proj, w_out, hg_norm_g, final_g)


def _bot(x, dx2, proj, h, qn, kvn, dq, dk, dv, dgm, dhq, dhf, dhi, dgh, c_t, s1_t, s2_t, w_in_t, w_q_p, w_kv_p, ln_g, q_g, kv_g, tm):
    T = x.shape[0]
    lat_w = D_PERM - P_QL
    steps = T // tm

    def body(x_ref, dx2_ref, lat_ref, h_ref, qn_ref, kvn_ref, dq_ref, dk_ref, dv_ref, dgm_ref, dhq_ref, dhf_ref,
             dhi_ref, dgh_ref, c_ref, s1_ref, s2_ref, win_ref, wq_ref, wkv_ref, lng_ref, qg_ref, kvg_ref,
             dx_ref, dlng_ref, dqg_ref, dkvg_ref, dwq_ref, dwkv_ref, dwin_ref, dqpre_ref, dkv_ref, dproj_ref,
             dwin_acc, sem):
        @pl.when(pl.program_id(0) == 0)
        def _():
            for ref in (dlng_ref, dqg_ref, dkvg_ref, dwq_ref, dwkv_ref, dwin_acc):
                ref[...] = jnp.zeros_like(ref)

        c, s1, s2 = c_ref[...], s1_ref[...], s2_ref[...]
        dkpe = jnp.zeros((tm, LANES), F32)
        for hd in range(N_HEADS):
            sl = slice(hd * HEAD_LANES, (hd + 1) * HEAD_LANES)
            dqpre_ref[:, sl] = _rope_bwd(dq_ref[:, sl], c, s1, s2).astype(BF16)
            dk_h = dk_ref[:, sl]
            dkpe = dkpe + dk_h
            dkv_ref[:, sl] = dk_h.astype(BF16)
        dkv_ref[:, N_HEADS * HEAD_LANES:] = dv_ref[...].astype(BF16)
        lane = lax.broadcasted_iota(jnp.int32, (tm, LANES), 1)
        rope_lanes = jnp.logical_and(lane >= ROPE_LO, lane < ROPE_LO + ROPE)
        dkr = jnp.where(rope_lanes, _rope_bwd(dkpe, c, s1, s2), 0.0)

        def norm_bwd(v, g, dy):
            r = lax.rsqrt(jnp.mean(v * v, axis=-1, keepdims=True) + EPS)
            vh = v * r
            dvh = dy * g
            return jnp.sum(dy * vh, axis=0, keepdims=True), r * (dvh - vh * jnp.mean(dvh * vh, axis=-1, keepdims=True))

        dwq_ref[...] += _dot_tn(qn_ref[...], dqpre_ref[...])
        dwkv_ref[...] += _dot_tn(kvn_ref[...], dkv_ref[...])
        dqn = _dot_nt(dqpre_ref[...], wq_ref[...])
        dg_q, dql = norm_bwd(lat_ref[:, :Q_RANK], qg_ref[...], dqn)
        dqg_ref[...] += dg_q
        dkn = _dot_nt(dkv_ref[...], wkv_ref[...])
        dg_kv, dkvl = norm_bwd(lat_ref[:, Q_RANK:Q_RANK + KV_RANK], kvg_ref[...], dkn)
        dkvg_ref[...] += dg_kv

        dproj_ref[:, P_GM:P_GM + MLA_WIDTH] = dgm_ref[...].astype(BF16)
        dproj_ref[:, P_HQ:P_HQ + HG_WIDTH] = dhq_ref[...].astype(BF16)
        dproj_ref[:, P_HF:P_HF + HG_WIDTH] = dhf_ref[...].astype(BF16)
        dproj_ref[:, P_HI:P_HI + HG_WIDTH] = dhi_ref[...].astype(BF16)
        dproj_ref[:, P_GH:P_GH + HG_WIDTH] = dgh_ref[...].astype(BF16)
        dproj_ref[:, P_QL:P_QL + Q_RANK] = dql.astype(BF16)
        dproj_ref[:, P_KVL:P_KVL + KV_RANK] = dkvl.astype(BF16)
        dproj_ref[:, P_KR:P_KR + LANES] = dkr.astype(BF16)
        dh = sum(_dot(dproj_ref[:, col:col + rows.shape[0]], rows) for col, rows in _in_proj_rows(win_ref))
        dg_ln, dxn = norm_bwd(x_ref[...], lng_ref[...], dh)
        dlng_ref[...] += dg_ln
        dx_ref[...] = dx2_ref[...] + dxn
        dwin_acc[...] += _dot_tn(dproj_ref[...], h_ref[...])

        @pl.when(pl.program_id(0) == steps - 1)
        def _():
            kr = P_KR + ROPE_LO
            moves = [((P_GM, P_QL), R_MAIN), ((P_QL, P_KR), (R_QL[0], R_KVL[1])), ((kr, kr + ROPE), R_KR)]
            copies = [pltpu.make_async_copy(dwin_acc.at[a:b, :], dwin_ref.at[c:d, :], sem.at[n])
                      for n, ((a, b), (c, d)) in enumerate(moves)]
            for cp in copies:
                cp.start()
            for cp in copies:
                cp.wait()

    def row(w, cb=0):
        return pl.BlockSpec((tm, w), lambda i: (i, cb))

    hl = N_HEADS * HEAD_LANES
    outs = [(D_MODEL, F32)]
    small = [(1, D_MODEL), (1, Q_RANK), (1, KV_RANK), (Q_RANK, hl), (KV_RANK, hl + MLA_WIDTH)]
    return pl.pallas_call(
        body, name="bot", grid=(steps,),
        in_specs=[row(D_MODEL), row(D_MODEL), row(lat_w, P_QL // lat_w), row(D_MODEL), row(Q_RANK), row(KV_RANK),
                  row(hl), row(hl), row(MLA_WIDTH),
                  row(MLA_WIDTH), row(HG_WIDTH), row(HG_WIDTH), row(HG_WIDTH), row(HG_WIDTH),
                  row(LANES), row(LANES), row(LANES),
                  _full((D_IN, D_MODEL)), _full((Q_RANK, hl)), _full((KV_RANK, hl + MLA_WIDTH)),
                  _full((1, D_MODEL)), _full((1, Q_RANK)), _full((1, KV_RANK))],
        out_specs=[row(w) for w, _ in outs] + [_full(s) for s in small] + [pl.BlockSpec(memory_space=pl.ANY)],
        out_shape=[jax.ShapeDtypeStruct((T, w), dt) for w, dt in outs] + [jax.ShapeDtypeStruct(s, F32) for s in small]
        + [jax.ShapeDtypeStruct((D_IN, D_MODEL), F32)],
        scratch_shapes=[pltpu.VMEM((tm, hl), BF16), pltpu.VMEM((tm, hl + MLA_WIDTH), BF16),
                        pltpu.VMEM((tm, D_PERM), BF16), pltpu.VMEM((D_PERM, D_MODEL), F32),
                        pltpu.SemaphoreType.DMA((3,))],
        compiler_params=_params(1),
    )(x, dx2, proj, h, qn, kvn, dq, dk, dv, dgm, dhq, dhf, dhi, dgh, c_t, s1_t, s2_t, w_in_t, w_q_p, w_kv_p, ln_g,
      q_g, kv_g)


RS_ROWS = 256


def _reduce_scatter(slabs, small):
    n = len(slabs)
    units = []
    for a, s in enumerate(slabs):
        rows, cols = s.shape[1:]
        if rows % RS_ROWS == 0 or rows < RS_ROWS:
            units += [(a, (pl.ds(r0, min(rows, RS_ROWS)), slice(None))) for r0 in range(0, rows, RS_ROWS)]
        else:
            units += [(a, (slice(None), pl.ds(c0, RS_ROWS))) for c0 in range(0, cols, RS_ROWS)]
    nu = len(units)

    def body(*refs):
        ins, small_ref = refs[:n], refs[n]
        outs, small_out = refs[n + 1:2 * n + 1], refs[2 * n + 1]
        own, sib_land, ici_out, ici_land = (refs[(2 + g) * n + 2:(3 + g) * n + 2] for g in range(4))
        small_land = refs[6 * n + 2]
        loc_sems, d2d_send, d2d_recv, ici_send, ici_recv, sm_send, sm_recv = refs[6 * n + 3:6 * n + 10]
        x, y, c = lax.axis_index("x"), lax.axis_index("y"), lax.axis_index("c")
        me = 4 * x + 2 * y + c

        def chip(k):
            return (1 - x if k & 2 else x, 1 - y if k & 1 else y)

        def block(k, core):
            px, py = chip(k)
            return 4 * px + 2 * py + core

        def part(u):
            return units[u]

        def local(u, k):
            a, rows = part(u)
            return pltpu.make_async_copy(ins[a].at[(block(k, c),) + rows], own[a].at[(k,) + rows], loc_sems.at[u, k])

        def to_sibling(u, k):
            a, rows = part(u)
            return pltpu.make_async_remote_copy(
                src_ref=ins[a].at[(block(k, 1 - c),) + rows], dst_ref=sib_land[a].at[(k,) + rows],
                send_sem=d2d_send.at[u, k], recv_sem=d2d_recv.at[u, k], device_id=(x, y, 1 - c), device_id_type=MESH)

        def to_chip(u, k):
            a, rows = part(u)
            return pltpu.make_async_remote_copy(
                src_ref=ici_out[a].at[(k - 1,) + rows], dst_ref=ici_land[a].at[(k - 1,) + rows],
                send_sem=ici_send.at[u, k - 1], recv_sem=ici_recv.at[u, k - 1], device_id=(*chip(k), c),
                device_id_type=MESH)

        def small_copy(k, receiving):
            px, py = chip(k >> 1)
            pc = 1 - c if k & 1 else c
            slot = 4 * px + 2 * py + pc if receiving else me
            return pltpu.make_async_remote_copy(
                src_ref=small_ref, dst_ref=small_land.at[slot], send_sem=sm_send.at[k - 1], recv_sem=sm_recv.at[k - 1],
                device_id=(px, py, pc), device_id_type=MESH)

        for u in range(nu):
            for k in range(4):
                local(u, k).start()
        for u in range(nu):
            for k in range(4):
                to_sibling(u, k).start()
        small_land[me] = small_ref[...]
        for k in range(1, N_DEV):
            small_copy(k, False).start()
        for u in range(nu):
            a, rows = part(u)
            for k in range(4):
                local(u, k).wait()
                to_sibling(u, k).wait_recv()
            for k in range(1, 4):
                ici_out[a][(k - 1,) + rows] = (own[a][(k,) + rows] + sib_land[a][(k,) + rows]).astype(BF16)
                to_chip(u, k).start()
        for u in range(nu):
            a, rows = part(u)
            acc = own[a][(0,) + rows] + sib_land[a][(0,) + rows]
            for k in range(1, 4):
                to_chip(u, k).wait_recv()
                acc = acc + ici_land[a][(k - 1,) + rows].astype(F32)
            outs[a][rows] = acc
        for k in range(1, N_DEV):
            small_copy(k, True).wait_recv()
        acc = small_land[0]
        for d in range(1, N_DEV):
            acc = acc + small_land[d]
        small_out[...] = acc
        for u in range(nu):
            for k in range(4):
                to_sibling(u, k).wait_send()
            for k in range(1, 4):
                to_chip(u, k).wait_send()
        for k in range(1, N_DEV):
            small_copy(k, False).wait_send()

    vm = pl.BlockSpec(memory_space=pltpu.VMEM)
    hbm = pl.BlockSpec(memory_space=pl.ANY)
    dma = pltpu.SemaphoreType.DMA
    return pl.pallas_call(
        body, name="reduce_scatter_grads",
        in_specs=[hbm] * n + [vm], out_specs=[vm] * (n + 1),
        out_shape=[jax.ShapeDtypeStruct(s.shape[1:], F32) for s in slabs] + [jax.ShapeDtypeStruct(small.shape, F32)],
        scratch_shapes=[pltpu.VMEM((4,) + s.shape[1:], F32) for s in slabs] * 2
        + [pltpu.VMEM((3,) + s.shape[1:], BF16) for s in slabs] * 2
        + [pltpu.VMEM((N_DEV,) + small.shape, F32)]
        + [dma((nu, 4)), dma((nu, 4)), dma((nu, 4)), dma((nu, 3)), dma((nu, 3)), dma((N_DEV - 1,)), dma((N_DEV - 1,))],
        compiler_params=pltpu.CompilerParams(vmem_limit_bytes=VMEM_LIMIT),
    )(*slabs, small)


def _adamw_math(w, g, m, v):
    m = ADAM_B1 * m + (1.0 - ADAM_B1) * g
    v = ADAM_B2 * v + (1.0 - ADAM_B2) * (g * g)
    m_hat = m / (1.0 - ADAM_B1 ** ADAM_STEP)
    v_hat = v / (1.0 - ADAM_B2 ** ADAM_STEP)
    delta = -ADAM_LR * (m_hat / (jnp.sqrt(v_hat) + ADAM_EPS) + ADAM_WD * w)
    return delta, m, v


SMALL_W = 512


def _adamw(big, small_w, small_g):
    nb, ns = len(big), len(small_w)

    def body(*refs):
        k = 0
        big_in = [refs[4 * i:4 * i + 4] for i in range(nb)]
        k = 4 * nb
        small_in = [refs[k + 3 * i:k + 3 * i + 3] for i in range(ns)]
        k += 3 * ns
        sg_ref = refs[k]
        k += 1
        big_out = [refs[k + 3 * i:k + 3 * i + 3] for i in range(nb)]
        k += 3 * nb
        small_out = [refs[k + 4 * i:k + 4 * i + 4] for i in range(ns)]

        for (w, g, m, v), (od, om, ov) in zip(big_in, big_out):
            od[...], om[...], ov[...] = _adamw_math(w[...], g[...], m[...], v[...])

        sg = sg_ref[...]
        lbp = small_in[2][0][...]
        lb = _lower_bound(lbp)
        t = sg[4:5, :] * lb * (1.0 - lb)
        grads = [jnp.concatenate([sg[0:1, :], sg[1:2, :]], axis=1),
                 jnp.concatenate([sg[2:3, :], sg[3:4, :]], axis=1),
                 jnp.concatenate([t, -t], axis=0),
                 sg[6:7, :], sg[7:8, 0:Q_RANK], sg[7:8, Q_RANK:Q_RANK + KV_RANK]]
        for (w, m, v), g, (og, od, om, ov) in zip(small_in, grads, small_out):
            og[...] = g
            od[...], om[...], ov[...] = _adamw_math(w[...], g, m[...], v[...])

    ins = [a for grp in big for a in grp] + [a for grp in small_w for a in grp] + [small_g]
    out_shape = ([jax.ShapeDtypeStruct(grp[0].shape, F32) for grp in big for _ in range(3)]
                 + [jax.ShapeDtypeStruct(grp[0].shape, F32) for grp in small_w for _ in range(4)])
    vm = pl.BlockSpec(memory_space=pltpu.VMEM)
    res = pl.pallas_call(
        body, name="adamw", in_specs=[vm] * len(ins), out_specs=[vm] * len(out_shape), out_shape=out_shape,
        compiler_params=pltpu.CompilerParams(vmem_limit_bytes=VMEM_LIMIT),
    )(*ins)
    big_res = [res[3 * i:3 * i + 3] for i in range(nb)]
    small_res = [res[3 * nb + 4 * i:3 * nb + 4 * i + 4] for i in range(ns)]
    return big_res, small_res


def _perm_weights(g_in_t, g_q, g_kv, g_out):
    w_in_t = g_in_t.reshape(D_IN, D_MODEL)
    wq = g_q.transpose(1, 0, 2)
    w_q_p = jnp.pad(wq, ((0, 0), (0, 0), (0, HEAD_LANES - NOPE - ROPE))).reshape(Q_RANK, N_HEADS * HEAD_LANES)
    wkv = g_kv.transpose(1, 0, 2)
    wk = jnp.pad(wkv[:, :, :NOPE], ((0, 0), (0, 0), (0, HEAD_LANES - NOPE))).reshape(KV_RANK, N_HEADS * HEAD_LANES)
    wv = wkv[:, :, NOPE:].reshape(KV_RANK, MLA_WIDTH)
    return w_in_t, w_q_p, jnp.concatenate([wk, wv], axis=1), g_out.reshape(D_MODEL, D_MODEL)


def _grad_slabs(dw_in_t, dw_q_p, dw_kv_p, dw_out):
    s_in = dw_in_t.reshape(N_DEV, D_IN // N_DEV, D_MODEL)
    s_q = dw_q_p.reshape(Q_RANK, N_HEADS, HEAD_LANES)[:, :, :NOPE + ROPE].transpose(1, 0, 2)
    hl = N_HEADS * HEAD_LANES
    dk = dw_kv_p[:, :hl].reshape(KV_RANK, N_HEADS, HEAD_LANES)[:, :, :NOPE]
    dv = dw_kv_p[:, hl:].reshape(KV_RANK, N_HEADS, V_DIM)
    s_kv = jnp.concatenate([dk, dv], axis=2).transpose(1, 0, 2)
    return s_in, s_q, s_kv, dw_out.reshape(N_DEV, D_MODEL // N_DEV, D_MODEL)


def _block_sizes(T):
    return min(256, T), min(256, T), min(512, T)


def kernel(x, positions, ln_g, w_in, q_a_norm_g, w_q_b, kv_a_norm_g, w_kv_b, hg_lower_bounds, hg_norm_g, w_out, final_norm_g, loss_target, m_ln_g, m_w_in, m_q_a_norm_g, m_w_q_b, m_kv_a_norm_g, m_w_kv_b, m_hg_lower_bounds, m_hg_norm_g, m_w_out, m_final_norm_g, v_ln_g, v_w_in, v_q_a_norm_g, v_w_q_b, v_kv_a_norm_g, v_w_kv_b, v_hg_lower_bounds, v_hg_norm_g, v_w_out, v_final_norm_g):
    T = x.shape[1]
    tm, tq, bt = _block_sizes(T)
    nq = T // tq
    xs, tgt = x[0], loss_target[0]
    pos_f = positions.astype(F32)
    fng = final_norm_g.reshape(1, D_MODEL)

    w_in_shard_t = w_in[0].T
    gathered, (c_t, s1_t, s2_t) = _all_gather_weights([w_in_shard_t, w_q_b[0], w_kv_b[0], w_out[0]], pos_f)
    w_in_t, w_q_p, w_kv_p, w_out_b = _perm_weights(*gathered)

    proj, h, qn, kvn, q, k, v, v_t = _fwd_in(
        xs, ln_g, w_in_t, q_a_norm_g, w_q_p, kv_a_norm_g, w_kv_p, c_t, s1_t, s2_t, bt, tq)
    hl = N_HEADS * HEAD_LANES
    k_t = k.reshape(nq, tq, hl).transpose(0, 2, 1)
    q_t = q.reshape(nq, tq, hl).transpose(0, 2, 1)
    o_mla, o_t, lse = _attn_fwd_flat(k, q_t, v_t, tq)
    o_hg, states = _hgrn_fwd(proj, hg_lower_bounds)
    dx2, do_t, d_gm, d_oh, d_gh, loss_p, d_fng, d_hgn, dw_out = _top(
        xs, tgt, o_mla, o_hg, proj, w_out_b, hg_norm_g, fng, tm, tq)
    dq_t, dk, dv = _attn_bwd_flat(k, v, q_t, k_t, do_t, o_t, lse, tq)
    dq = dq_t.transpose(0, 2, 1).reshape(T, N_HEADS * HEAD_LANES)
    d_hq, d_hf, d_hi, d_lb = _hgrn_bwd(proj, hg_lower_bounds, d_oh, states)
    dx, d_lng, d_qg, d_kvg, dw_q_p, dw_kv_p, dw_in_t = _bot(
        xs, dx2, proj, h, qn, kvn, dq, dk, dv, d_gm, d_hq, d_hf, d_hi, d_gh, c_t, s1_t, s2_t, w_in_t, w_q_p, w_kv_p,
        ln_g, q_a_norm_g, kv_a_norm_g, tm)

    small = jnp.concatenate([
        d_lng.reshape(2, SMALL_W), d_fng.reshape(2, SMALL_W), d_lb, loss_p, d_hgn,
        jnp.concatenate([d_qg, d_kvg, jnp.zeros((1, SMALL_W - Q_RANK - KV_RANK), F32)], axis=1)], axis=0)
    g_in, g_q, g_kv, g_out, small_sum = _reduce_scatter(list(_grad_slabs(dw_in_t, dw_q_p, dw_kv_p, dw_out)), small)

    big = [(w_in_shard_t, g_in, m_w_in[0].T, v_w_in[0].T), (w_q_b[0], g_q, m_w_q_b[0], v_w_q_b[0]),
           (w_kv_b[0], g_kv, m_w_kv_b[0], v_w_kv_b[0]), (w_out[0], g_out, m_w_out[0], v_w_out[0])]
    small_w = [(ln_g, m_ln_g, v_ln_g),
               (fng, m_final_norm_g.reshape(1, D_MODEL), v_final_norm_g.reshape(1, D_MODEL)),
               (hg_lower_bounds, m_hg_lower_bounds, v_hg_lower_bounds), (hg_norm_g, m_hg_norm_g, v_hg_norm_g),
               (q_a_norm_g, m_q_a_norm_g, v_q_a_norm_g), (kv_a_norm_g, m_kv_a_norm_g, v_kv_a_norm_g)]
    big_res, small_res = _adamw(big, small_w, small_sum)

    loss = small_sum[5, 0]
    (r_in, r_q, r_kv, r_out) = big_res
    (s_ln, s_fn, s_lb, s_hgn, s_qg, s_kvg) = small_res
    flat = lambda t: t.reshape(D_MODEL)
    lead = lambda t: t[None]
    grads = [s_ln[0], lead(g_in.T), s_qg[0], lead(g_q), s_kvg[0], lead(g_kv), s_lb[0], s_hgn[0], lead(g_out), flat(s_fn[0])]

    def pick(i):
        return [s_ln[i + 1], lead(r_in[i].T), s_qg[i + 1], lead(r_q[i]), s_kvg[i + 1], lead(r_kv[i]), s_lb[i + 1],
                s_hgn[i + 1], lead(r_out[i]), flat(s_fn[i + 1])]

    return (loss, dx[None], *grads, *pick(0), *pick(1), *pick(2))
```

```python
import math

import numpy as np
import jax
import jax.numpy as jnp
from jax import lax
from jax.experimental import pallas as pl
from jax.experimental.pallas import tpu as pltpu

F32 = jnp.float32
BF16 = jnp.bfloat16

D_MODEL = 1024
N_HEADS = 8
NOPE = 64
ROPE = 32
HALF_ROPE = ROPE // 2
V_DIM = 64
Q_RANK = 256
KV_RANK = 128
MLA_WIDTH = N_HEADS * V_DIM
HG_HEADS = 4
HG_DIM = 128
HG_WIDTH = HG_HEADS * HG_DIM
CHUNK = 64
SUB = 16
D_IN = 2976
D_PERM = 3072
ROPE_THETA = 10000.0
EPS = 1e-6
N_DEV = 8
LANES = 128
HEAD_LANES = 128

P_GM, P_HQ, P_HF, P_HI, P_GH, P_QL, P_KVL, P_KR = 0, 512, 1024, 1536, 2048, 2560, 2816, 2944
R_QL, R_KVL, R_KR, R_MAIN = (0, 256), (256, 384), (384, 416), (416, 2976)
ROPE_LO = NOPE
SCALE = 1.0 / math.sqrt(NOPE + ROPE)

ADAM_LR = 0.001
ADAM_B1 = 0.9
ADAM_B2 = 0.999
ADAM_EPS = 1e-08
ADAM_WD = 0.01
ADAM_STEP = 10

VMEM_LIMIT = 56 * 1024 * 1024
MESH = pl.DeviceIdType.MESH

NT = (((1,), (1,)), ((), ()))
TN = (((0,), (0,)), ((), ()))


def _params(n_grid=0, **kw):
    sem = ("arbitrary",) * n_grid if n_grid else None
    return pltpu.CompilerParams(dimension_semantics=sem, vmem_limit_bytes=VMEM_LIMIT, **kw)


def _dot(a, b):
    return jnp.dot(a, b, preferred_element_type=F32)


def _dot_nt(a, b):
    return lax.dot_general(a, b, NT, preferred_element_type=F32)


def _dot_tn(a, b):
    return lax.dot_general(a, b, TN, preferred_element_type=F32)


def _sigmoid(x):
    return 1.0 / (1.0 + jnp.exp(-x))


def _rope_fwd(x, c, s1, s2):
    return x * c + pltpu.roll(x, LANES - HALF_ROPE, 1) * s1 + pltpu.roll(x, HALF_ROPE, 1) * s2


def _rope_bwd(dy, c, s1, s2):
    return dy * c - pltpu.roll(dy, LANES - HALF_ROPE, 1) * s1 - pltpu.roll(dy, HALF_ROPE, 1) * s2


def _in_proj_rows(wt_ref):
    kr = wt_ref[R_KR[0]:R_KR[1], :]
    pad = lambda n: jnp.zeros((n, D_MODEL), kr.dtype)
    return ((P_GM, wt_ref[R_MAIN[0]:R_MAIN[1], :]), (P_QL, wt_ref[R_QL[0]:R_QL[1], :]),
            (P_KVL, wt_ref[R_KVL[0]:R_KVL[1], :]),
            (P_KR, jnp.concatenate([pad(ROPE_LO), kr, pad(LANES - ROPE_LO - ROPE)], axis=0)))


def _full(shape):
    n = len(shape)
    return pl.BlockSpec(shape, lambda *_: (0,) * n)


ROPE_BLOCK = 512


def _rope_constants():
    inv = (np.float32(ROPE_THETA) ** (-np.arange(HALF_ROPE, dtype=np.float32) / np.float32(HALF_ROPE))).astype(np.float32)
    place = np.zeros((3, HALF_ROPE, LANES), np.float32)
    for i in range(HALF_ROPE):
        place[0, i, ROPE_LO + i] = place[0, i, ROPE_LO + HALF_ROPE + i] = 1.0
        place[1, i, ROPE_LO + i] = -1.0
        place[2, i, ROPE_LO + HALF_ROPE + i] = 1.0
    base = np.ones((1, LANES), np.float32)
    base[0, ROPE_LO:ROPE_LO + ROPE] = 0.0
    return jnp.asarray(inv.reshape(HALF_ROPE, 1)), jnp.asarray(place), jnp.asarray(base)


def _rope_block(pos, inv, place_ref, base):
    ang = inv * pos
    cos, sin = jnp.cos(ang), jnp.sin(ang)

    def put(v, k):
        return lax.dot_general(v, place_ref[k], TN, precision=lax.Precision.HIGHEST, preferred_element_type=F32)

    return put(cos, 0) + base, put(sin, 1), put(sin, 2)


def _all_gather_weights(shards, pos_f):
    n = len(shards)
    T = pos_f.shape[1]
    rb = min(ROPE_BLOCK, T)

    def body(*refs):
        ins, (pos_ref, inv_ref, place_ref, base_ref) = refs[:n], refs[n:n + 4]
        outs, tables = refs[n + 4:2 * n + 4], refs[2 * n + 4:2 * n + 7]
        send_sems, recv_sems = refs[2 * n + 7], refs[2 * n + 8]
        x, y, c = lax.axis_index("x"), lax.axis_index("y"), lax.axis_index("c")
        me, sibling = (x, y, c), (x, y, 1 - c)
        chips = [(1 - x, y), (x, 1 - y), (1 - x, 1 - y)]

        def idx(d):
            return 4 * d[0] + 2 * d[1] + d[2]

        def copy(a, k, block, to):
            rows = outs[a].at[idx(block)]
            return pltpu.make_async_remote_copy(src_ref=rows, dst_ref=rows, send_sem=send_sems.at[a, k],
                                                recv_sem=recv_sems.at[a, k], device_id=to, device_id_type=MESH)

        for a in range(n):
            outs[a][idx(me)] = ins[a][...].astype(BF16)
        first = []
        for a in range(n):
            first.append(copy(a, 0, me, sibling))
            first += [copy(a, 1 + j, me, (*chip, c)) for j, chip in enumerate(chips)]
        for cp in first:
            cp.start()
        for r0 in range(0, T, rb):
            for ref, tab in zip(tables, _rope_block(pos_ref[:, r0:r0 + rb], inv_ref[...], place_ref, base_ref[...])):
                ref[r0:r0 + rb, :] = tab
        passed = []
        for j, chip in enumerate(chips):
            for a in range(n):
                copy(a, 1 + j, (*chip, c), me).wait_recv()
                cp = copy(a, 4 + j, (*chip, c), sibling)
                cp.start()
                passed.append(cp)
        for a in range(n):
            copy(a, 0, sibling, me).wait_recv()
            for j, chip in enumerate(chips):
                copy(a, 4 + j, (*chip, 1 - c), me).wait_recv()
        for cp in first + passed:
            cp.wait_send()

    vm = pl.BlockSpec(memory_space=pltpu.VMEM)
    res = pl.pallas_call(
        body, name="all_gather_weights",
        in_specs=[vm] * (n + 4), out_specs=[vm] * (n + 3),
        out_shape=[jax.ShapeDtypeStruct((N_DEV,) + s.shape, BF16) for s in shards]
        + [jax.ShapeDtypeStruct((T, LANES), F32)] * 3,
        scratch_shapes=[pltpu.SemaphoreType.DMA((n, 7)), pltpu.SemaphoreType.DMA((n, 7))],
        compiler_params=pltpu.CompilerParams(vmem_limit_bytes=VMEM_LIMIT),
    )(*shards, pos_f, *_rope_constants())
    return res[:n], res[n:]


def _fwd_in(x, ln_g, w_in_t, q_g, w_q_p, kv_g, w_kv_p, c_t, s1_t, s2_t, tm, tq):
    T = x.shape[0]
    assert tm % tq == 0

    def body(x_ref, lng_ref, win_ref, qg_ref, wq_ref, kvg_ref, wkv_ref, c_ref, s1_ref, s2_ref,
             proj_ref, h_ref, qn_ref, kvn_ref, q_ref, k_ref, v_ref, vt_ref):
        xv = x_ref[...]
        r = lax.rsqrt(jnp.mean(xv * xv, axis=-1, keepdims=True) + EPS)
        h = (xv * r * lng_ref[...]).astype(BF16)
        h_ref[...] = h
        for col, rows in _in_proj_rows(win_ref):
            proj_ref[:, col:col + rows.shape[0]] = _dot_nt(h, rows)
        c, s1, s2 = c_ref[...], s1_ref[...], s2_ref[...]

        ql = proj_ref[:, P_QL:P_QL + Q_RANK]
        rq = lax.rsqrt(jnp.mean(ql * ql, axis=-1, keepdims=True) + EPS)
        qn = (ql * rq * qg_ref[...]).astype(BF16)
        qn_ref[...] = qn
        q = _dot(qn, wq_ref[...])
        for hd in range(N_HEADS):
            sl = slice(hd * HEAD_LANES, (hd + 1) * HEAD_LANES)
            q_ref[:, sl] = _rope_fwd(q[:, sl], c, s1, s2).astype(BF16)

        kvl = proj_ref[:, P_KVL:P_KVL + KV_RANK]
        rk = lax.rsqrt(jnp.mean(kvl * kvl, axis=-1, keepdims=True) + EPS)
        kvn = (kvl * rk * kvg_ref[...]).astype(BF16)
        kvn_ref[...] = kvn
        kv = _dot(kvn, wkv_ref[...])
        kpe = _rope_fwd(proj_ref[:, P_KR:P_KR + LANES], c, s1, s2)
        for hd in range(N_HEADS):
            sl = slice(hd * HEAD_LANES, (hd + 1) * HEAD_LANES)
            k_ref[:, sl] = (kv[:, sl] + kpe).astype(BF16)
        v_ref[...] = kv[:, N_HEADS * HEAD_LANES:].astype(BF16)
        for b in range(tm // tq):
            vt_ref[b] = kv[b * tq:(b + 1) * tq, N_HEADS * HEAD_LANES:].T.astype(BF16)

    def row(w):
        return pl.BlockSpec((tm, w), lambda i: (i, 0))

    outs = [(D_PERM, F32), (D_MODEL, BF16), (Q_RANK, BF16), (KV_RANK, BF16),
            (N_HEADS * HEAD_LANES, BF16), (N_HEADS * HEAD_LANES, BF16), (MLA_WIDTH, BF16)]
    return pl.pallas_call(
        body, name="fwd_in", grid=(T // tm,),
        in_specs=[row(D_MODEL), _full((1, D_MODEL)), _full((D_IN, D_MODEL)), _full((1, Q_RANK)),
                  _full((Q_RANK, N_HEADS * HEAD_LANES)), _full((1, KV_RANK)),
                  _full((KV_RANK, N_HEADS * HEAD_LANES + MLA_WIDTH)), row(LANES), row(LANES), row(LANES)],
        out_specs=[row(w) for w, _ in outs] + [pl.BlockSpec((tm // tq, MLA_WIDTH, tq), lambda i: (i, 0, 0))],
        out_shape=[jax.ShapeDtypeStruct((T, w), dt) for w, dt in outs]
        + [jax.ShapeDtypeStruct((T // tq, MLA_WIDTH, tq), BF16)],
        compiler_params=_params(1),
    )(x, ln_g, w_in_t, q_g, w_q_p, kv_g, w_kv_p, c_t, s1_t, s2_t)


LOG2E = 1.4426950408889634
SCALE2 = SCALE * LOG2E


def _causal(tq):
    r = lax.broadcasted_iota(jnp.int32, (tq, tq), 0)
    c = lax.broadcasted_iota(jnp.int32, (tq, tq), 1)
    return r <= c


MASKED = -1e30


def _causal_bias(bias_ref, tq):
    bias_ref[0] = jnp.zeros((tq, tq), F32)
    bias_ref[1] = jnp.where(_causal(tq), 0.0, MASKED)


def _tile_tables(nq, by_query):
    if by_query:
        pairs = [(j, i) for i in range(nq) for j in range(i + 1)]
    else:
        pairs = [(j, i) for j in range(nq) for i in range(nq - 1, j - 1, -1)]
    pairs.append(pairs[-1])
    jj, ii = np.array(pairs, np.int32).T
    return jnp.asarray(jj), jnp.asarray(ii), len(pairs) - 1


ATTN_TRIP = 8


def _walk_tiles(n, products, tile, flush, buf_a, buf_b):
    bufs = (buf_a, buf_b)
    products(0, buf_a)

    def trip(r, carry):
        for u in range(ATTN_TRIP):
            products(ATTN_TRIP * r + u + 1, bufs[(u + 1) % 2])
            tile(ATTN_TRIP * r + u, bufs[u % 2])
        for u in range(ATTN_TRIP):
            flush(ATTN_TRIP * r + u)
        return carry

    lax.fori_loop(0, n // ATTN_TRIP, trip, 0)
    rest = n - n % ATTN_TRIP
    for u in range(n % ATTN_TRIP):
        if rest + u + 1 < n:
            products(rest + u + 1, bufs[(u + 1) % 2])
        tile(rest + u, bufs[u % 2])
    for u in range(n % ATTN_TRIP):
        flush(rest + u)


V_ROWS = V_DIM + 16


def _attn_fwd_flat(k, q_t, v_t, tq):
    T = k.shape[0]
    nq = T // tq
    jj, ii, n = _tile_tables(nq, True)
    heads = [slice(hh * HEAD_LANES, (hh + 1) * HEAD_LANES) for hh in range(2)]

    def body(jj_ref, ii_ref, k_ref, qt_ref, vt_ref, o_ref, ot_ref, lse_ref, sa_ref, sb_ref, m_ref, acc_ref, bias_ref):
        def reset(st):
            m_ref[st] = jnp.full(m_ref.shape[1:], MASKED, F32)
            acc_ref[st] = jnp.zeros(acc_ref.shape[1:], F32)

        _causal_bias(bias_ref, tq)
        for st in range(ATTN_TRIP):
            reset(st)
        extra = (lax.broadcasted_iota(jnp.int32, (V_ROWS - V_DIM, tq), 0) == 0).astype(BF16)

        def products(t, buf):
            j, i = jj_ref[t], ii_ref[t]
            kj = k_ref[pl.ds(pl.multiple_of(j * tq, tq), tq), :]
            for hh, sl in enumerate(heads):
                buf[hh] = _dot(kj[:, sl], qt_ref[i, sl, :])

        def tile(t, buf):
            j, i = jj_ref[t], ii_ref[t]
            vt = vt_ref[j]
            bias = bias_ref.at[(j == i).astype(jnp.int32)]
            st = i % ATTN_TRIP
            for hh in range(2):
                s = buf[hh] * SCALE2 + bias[...]
                m = m_ref[st, hh]
                m_new = jnp.maximum(m, jnp.max(s, axis=0, keepdims=True))
                alpha = jnp.exp2(m - m_new)
                p = jnp.exp2(s - m_new)
                m_ref[st, hh] = m_new
                v_h = jnp.concatenate([vt[hh * V_DIM:(hh + 1) * V_DIM, :], extra], axis=0)
                acc_ref[st, hh] = alpha * acc_ref[st, hh] + _dot(v_h, p.astype(BF16))

        def flush(t):
            j, i = jj_ref[t], ii_ref[t]

            @pl.when(j == i)
            def _():
                st = i % ATTN_TRIP
                den = [acc_ref[st, hh, V_DIM:V_DIM + 1, :] for hh in range(2)]
                out = jnp.concatenate([acc_ref[st, hh, :V_DIM, :] / den[hh] for hh in range(2)], axis=0)
                o_ref[pl.ds(pl.multiple_of(i * tq, tq), tq), :] = out.T
                ot_ref[i] = out
                for hh in range(2):
                    lse_ref[hh, i] = m_ref[st, hh] + jnp.log2(den[hh])
                reset(st)

        _walk_tiles(n, products, tile, flush, sa_ref, sb_ref)

    smem = pl.BlockSpec(memory_space=pltpu.SMEM)
    return pl.pallas_call(
        body, name="attn_fwd", grid=(N_HEADS // 2,),
        in_specs=[smem, smem,
                  pl.BlockSpec((T, 2 * HEAD_LANES), lambda p: (0, p)),
                  pl.BlockSpec((nq, 2 * HEAD_LANES, tq), lambda p: (0, p, 0)),
                  pl.BlockSpec((nq, LANES, tq), lambda p: (0, p, 0))],
        out_specs=[pl.BlockSpec((T, LANES), lambda p: (0, p)), pl.BlockSpec((nq, LANES, tq), lambda p: (0, p, 0)),
                   pl.BlockSpec((2, nq, 1, tq), lambda p: (p, 0, 0, 0))],
        out_shape=[jax.ShapeDtypeStruct((T, MLA_WIDTH), F32), jax.ShapeDtypeStruct((nq, MLA_WIDTH, tq), F32),
                   jax.ShapeDtypeStruct((N_HEADS, nq, 1, tq), F32)],
        scratch_shapes=[pltpu.VMEM((2, tq, tq), F32), pltpu.VMEM((2, tq, tq), F32),
                        pltpu.VMEM((ATTN_TRIP, 2, 1, tq), F32),
                        pltpu.VMEM((ATTN_TRIP, 2, V_ROWS, tq), F32), pltpu.VMEM((2, tq, tq), F32)],
        compiler_params=_params(1),
    )(jj, ii, k, q_t, v_t)


def _attn_bwd_flat(k, v, q_t, k_t, do_t, o_t, lse, tq):
    T = k.shape[0]
    nq = T // tq
    jj, ii, n = _tile_tables(nq, False)
    heads = [slice(hh * HEAD_LANES, (hh + 1) * HEAD_LANES) for hh in range(2)]

    def body(jj_ref, ii_ref, k_ref, v_ref, qt_ref, kt_ref, dot_ref, ot_ref, lse_ref, dqt_ref, dk_ref, dv_ref,
             ba_ref, bb_ref, dkt_ref, dvt_ref, bias_ref, dsum_ref):
        _causal_bias(bias_ref, tq)

        def row_dots(i, carry):
            for hh in range(2):
                own = slice(hh * V_DIM, (hh + 1) * V_DIM)
                do_h = dot_ref[i, hh * HEAD_LANES + own.start:hh * HEAD_LANES + own.stop, :]
                dsum_ref[hh, i] = jnp.sum(do_h.astype(F32) * ot_ref[i, own, :], axis=0, keepdims=True)
            return carry

        lax.fori_loop(0, nq, row_dots, 0)
        dqt_ref[...] = jnp.zeros_like(dqt_ref)
        dkt_ref[...] = jnp.zeros_like(dkt_ref)
        dvt_ref[...] = jnp.zeros_like(dvt_ref)

        def products(t, buf):
            j, i = jj_ref[t], ii_ref[t]
            rows = pl.ds(pl.multiple_of(j * tq, tq), tq)
            for hh, sl in enumerate(heads):
                buf[hh] = _dot(k_ref[rows, sl], qt_ref[i, sl, :])
                buf[2 + hh] = _dot(v_ref[rows, :], dot_ref[i, sl, :])

        def tile(t, buf):
            j, i = jj_ref[t], ii_ref[t]
            bias = bias_ref.at[(j == i).astype(jnp.int32)]
            st = j % ATTN_TRIP
            for hh, sl in enumerate(heads):
                p = jnp.exp2(buf[hh] * SCALE2 + bias[...] - lse_ref[hh, i])
                ds = (p * (buf[2 + hh] - dsum_ref[hh, i]) * SCALE).astype(BF16)
                own = slice(hh * V_DIM, (hh + 1) * V_DIM)
                do_h = dot_ref[i, hh * HEAD_LANES + own.start:hh * HEAD_LANES + own.stop, :]
                dvt_ref[st, own, :] += _dot_nt(do_h, p.astype(BF16))
                used = slice(sl.start, sl.start + NOPE + ROPE)
                dkt_ref[st, used, :] += _dot_nt(qt_ref[i, used, :], ds)
                dqt_ref[i, used, :] += _dot(kt_ref[j, used, :], ds)

        def flush(t):
            j, i = jj_ref[t], ii_ref[t]

            @pl.when(j == i)
            def _():
                st = j % ATTN_TRIP
                rows = pl.ds(pl.multiple_of(j * tq, tq), tq)
                dk_ref[rows, :] = dkt_ref[st].T
                dv_ref[rows, :] = dvt_ref[st].T.astype(BF16)
                dkt_ref[st] = jnp.zeros(dkt_ref.shape[1:], F32)
                dvt_ref[st] = jnp.zeros(dvt_ref.shape[1:], F32)

        _walk_tiles(n, products, tile, flush, ba_ref, bb_ref)

    smem = pl.BlockSpec(memory_space=pltpu.SMEM)
    stat = pl.BlockSpec((2, nq, 1, tq), lambda p: (p, 0, 0, 0))
    blocks_t = pl.BlockSpec((nq, 2 * HEAD_LANES, tq), lambda p: (0, p, 0))
    return pl.pallas_call(
        body, name="attn_bwd", grid=(N_HEADS // 2,),
        in_specs=[smem, smem,
                  pl.BlockSpec((T, 2 * HEAD_LANES), lambda p: (0, p)),
                  pl.BlockSpec((T, LANES), lambda p: (0, p)),
                  blocks_t, blocks_t, blocks_t, pl.BlockSpec((nq, LANES, tq), lambda p: (0, p, 0)), stat],
        out_specs=[blocks_t,
                   pl.BlockSpec((T, 2 * HEAD_LANES), lambda p: (0, p)),
                   pl.BlockSpec((T, LANES), lambda p: (0, p))],
        out_shape=[jax.ShapeDtypeStruct((nq, N_HEADS * HEAD_LANES, tq), F32),
                   jax.ShapeDtypeStruct((T, N_HEADS * HEAD_LANES), F32),
                   jax.ShapeDtypeStruct((T, MLA_WIDTH), BF16)],
        scratch_shapes=[pltpu.VMEM((4, tq, tq), F32), pltpu.VMEM((4, tq, tq), F32),
                        pltpu.VMEM((ATTN_TRIP, 2 * HEAD_LANES, tq), F32), pltpu.VMEM((ATTN_TRIP, LANES, tq), F32),
                        pltpu.VMEM((2, tq, tq), F32), pltpu.VMEM((2, nq, 1, tq), F32)],
        compiler_params=_params(1),
    )(jj, ii, k, v, q_t, k_t, do_t, o_t, lse)


def _lower_bound(lbp):
    a, b = lbp[0:1, :], lbp[1:2, :]
    mx = jnp.maximum(a, b)
    ea, eb = jnp.exp(a - mx), jnp.exp(b - mx)
    return ea / (ea + eb)


def _tri(lower):
    r = lax.broadcasted_iota(jnp.int32, (CHUNK, CHUNK), 0)
    c = lax.broadcasted_iota(jnp.int32, (CHUNK, CHUNK), 1)
    return (c <= r) if lower else (c >= r)


def _running_sum(x, from_end):
    row = lax.broadcasted_iota(jnp.int32, x.shape, 0)
    step = 1
    while step < CHUNK:
        if from_end:
            x = x + jnp.where(row < CHUNK - step, pltpu.roll(x, CHUNK - step, 0), 0.0)
        else:
            x = x + jnp.where(row >= step, pltpu.roll(x, step, 0), 0.0)
        step *= 2
    return x


def _hg_gates(hq, hf, lb):
    sq = _sigmoid(hq)
    sf = _sigmoid(hf)
    f = lb + (1.0 - lb) * sf
    g = jnp.log(f)
    gcum = _running_sum(g, False)
    return sq, sf, f, hq * sq, 1.0 - f, gcum


def _head(x, hd):
    return x[:, hd * HG_DIM:(hd + 1) * HG_DIM]


def _all_heads(fn):
    return jnp.concatenate([fn(hd) for hd in range(HG_HEADS)], axis=1)


def _hg_blocks(q, kk, gcum):
    rowi = lax.broadcasted_iota(jnp.int32, gcum.shape, 0)
    out = []
    for blk in range(CHUNK // SUB):
        lo, hi = blk * SUB, (blk + 1) * SUB
        gb = gcum[lo - 1:lo, :] if blk else jnp.zeros_like(gcum[0:1, :])
        eq = jnp.exp(gcum[lo:hi, :] - gb)
        ek = jnp.exp(jnp.where(rowi < hi, gb - gcum, 0.0))
        out.append((eq, ek, (q[lo:hi, :] * eq).astype(BF16), (kk * ek).astype(BF16)))
    return out


def _hg_scores(blocks):
    out = []
    for hd in range(HG_HEADS):
        a = jnp.concatenate([_dot_nt(_head(qb, hd), _head(kb, hd)) for _, _, qb, kb in blocks], axis=0)
        out.append(jnp.where(_tri(True), a, 0.0))
    return out


HG_STEP_CHUNKS = 8


def _hgrn_fwd(proj, lbp):
    T = proj.shape[0]
    nc = T // CHUNK
    ns = min(HG_STEP_CHUNKS, nc)
    rows = ns * CHUNK

    def body(hq_ref, hf_ref, hi_ref, lbp_ref, o_ref, st_ref, state):
        @pl.when(pl.program_id(0) == 0)
        def _():
            state[...] = jnp.zeros_like(state)

        lb = _lower_bound(lbp_ref[...])
        work = []
        for c in range(ns):
            r = slice(c * CHUNK, (c + 1) * CHUNK)
            _, _, _, q, kk, gcum = _hg_gates(hq_ref[r, :], hf_ref[r, :], lb)
            vb = hi_ref[r, :].astype(BF16)
            a = _hg_scores(_hg_blocks(q, kk, gcum))
            gend = gcum[CHUNK - 1:CHUNK, :]
            qgb = (q * jnp.exp(gcum)).astype(BF16)
            kgeb = (kk * jnp.exp(gend - gcum)).astype(BF16)
            intra = [_dot(a[hd].astype(BF16), _head(vb, hd)) for hd in range(HG_HEADS)]
            update = [_dot_tn(_head(vb, hd), _head(kgeb, hd)) for hd in range(HG_HEADS)]
            work.append((qgb, jnp.exp(gend), intra, update))
        for hd in range(HG_HEADS):
            st = state[hd]
            for c, (qgb, egend, intra, update) in enumerate(work):
                st_ref[c, hd] = st
                o_ref[c * CHUNK:(c + 1) * CHUNK, hd * HG_DIM:(hd + 1) * HG_DIM] = (
                    intra[hd] + _dot_nt(_head(qgb, hd), st.astype(BF16)))
                st = st * _head(egend, hd) + update[hd]
            state[hd] = st

    def col(cb):
        return pl.BlockSpec((rows, HG_WIDTH), lambda i: (i, cb))

    return pl.pallas_call(
        body, name="hgrn_fwd", grid=(nc // ns,),
        in_specs=[col(P_HQ // HG_WIDTH), col(P_HF // HG_WIDTH), col(P_HI // HG_WIDTH), _full((2, HG_WIDTH))],
        out_specs=[pl.BlockSpec((rows, HG_WIDTH), lambda i: (i, 0)),
                   pl.BlockSpec((ns, HG_HEADS, HG_DIM, HG_DIM), lambda i: (i, 0, 0, 0))],
        out_shape=[jax.ShapeDtypeStruct((T, HG_WIDTH), F32),
                   jax.ShapeDtypeStruct((nc, HG_HEADS, HG_DIM, HG_DIM), F32)],
        scratch_shapes=[pltpu.VMEM((HG_HEADS, HG_DIM, HG_DIM), F32)],
        compiler_params=_params(1),
    )(proj, proj, proj, lbp)


def _hgrn_bwd(proj, lbp, do_hg, states):
    T = proj.shape[0]
    nc = T // CHUNK
    ns = min(HG_STEP_CHUNKS, nc)
    rows = ns * CHUNK
    steps = nc // ns

    def body(hq_ref, hf_ref, hi_ref, lbp_ref, do_ref, st_ref, dhq_ref, dhf_ref, dhi_ref, dlb_ref, dstate):
        @pl.when(pl.program_id(0) == 0)
        def _():
            dstate[...] = jnp.zeros_like(dstate)
            dlb_ref[...] = jnp.zeros_like(dlb_ref)

        lb = _lower_bound(lbp_ref[...])

        dst_all = [dstate[hd] for hd in range(HG_HEADS)]
        dlb = jnp.zeros_like(lb)
        last = lax.broadcasted_iota(jnp.int32, (CHUNK, HG_WIDTH), 0) == CHUNK - 1
        for c in reversed(range(ns)):
            r = slice(c * CHUNK, (c + 1) * CHUNK)
            hq = hq_ref[r, :]
            sq, sf, f, q, kk, gcum = _hg_gates(hq, hf_ref[r, :], lb)
            vb = hi_ref[r, :].astype(BF16)
            dob = do_ref[r, :].astype(BF16)
            blocks = _hg_blocks(q, kk, gcum)
            a = _hg_scores(blocks)
            gend = gcum[CHUNK - 1:CHUNK, :]
            eg, egend, ekend = jnp.exp(gcum), jnp.exp(gend), jnp.exp(gend - gcum)
            qg, kge = q * eg, kk * ekend
            qgb, kgeb = qg.astype(BF16), kge.astype(BF16)

            dv, dqg, dkge, st_dst, dq_blk, dk_blk = [], [], [], [], [], []
            for hd in range(HG_HEADS):
                st = st_ref[c, hd]
                dst = dst_all[hd]
                dstb = dst.astype(BF16)
                do_h, v_h = _head(dob, hd), _head(vb, hd)
                dv.append(_dot_tn(a[hd].astype(BF16), do_h) + _dot_nt(_head(kgeb, hd), dstb))
                da = jnp.where(_tri(True), _dot_nt(do_h, v_h), 0.0).astype(BF16)
                dqg.append(_dot(do_h, st.astype(BF16)))
                dkge.append(_dot(v_h, dstb))
                st_dst.append(jnp.sum(st * dst, axis=0, keepdims=True))
                dst_all[hd] = _dot_tn(do_h, _head(qgb, hd)) + dst * _head(egend, hd)
                dq_blk.append([_dot(da[b * SUB:(b + 1) * SUB, :], _head(kb, hd)) for b, (_, _, _, kb) in enumerate(blocks)])
                dk_blk.append([_dot_tn(da[b * SUB:(b + 1) * SUB, :], _head(qb, hd)) for b, (_, _, qb, _) in enumerate(blocks)])
            dv, dqg, dkge, st_dst = (jnp.concatenate(t, axis=1) for t in (dv, dqg, dkge, st_dst))

            dq_a, dg_q = [], []
            dk_a, dg_k = jnp.zeros_like(gcum), jnp.zeros_like(gcum)
            for b, (eq, ek, qb, kb) in enumerate(blocks):
                dq_b = _all_heads(lambda hd: dq_blk[hd][b])
                dk_b = _all_heads(lambda hd: dk_blk[hd][b])
                dq_a.append(dq_b * eq)
                dk_a = dk_a + dk_b * ek
                dg_q.append(qb.astype(F32) * dq_b)
                dg_k = dg_k + kb.astype(F32) * dk_b
            dq_a = jnp.concatenate(dq_a, axis=0)

            dgend = st_dst * egend + jnp.sum(dkge * kge, axis=0, keepdims=True)
            dq = dq_a + dqg * eg
            dk = dk_a + dkge * ekend
            dgc = jnp.concatenate(dg_q, axis=0) - dg_k + dqg * qg - dkge * kge + jnp.where(last, dgend, 0.0)
            dg = _running_sum(dgc, True)
            df = dg / f - dk
            dhf_ref[r, :] = (df * (1.0 - lb) * sf * (1.0 - sf)).astype(BF16)
            dlb = dlb + jnp.sum(df * (1.0 - sf), axis=0, keepdims=True)
            dhq_ref[r, :] = (dq * (sq * (1.0 + hq * (1.0 - sq)))).astype(BF16)
            dhi_ref[r, :] = dv.astype(BF16)
        for hd in range(HG_HEADS):
            dstate[hd] = dst_all[hd]
        dlb_ref[...] += dlb

    def col(cb):
        return pl.BlockSpec((rows, HG_WIDTH), lambda i: (steps - 1 - i, cb))

    grad = jax.ShapeDtypeStruct((T, HG_WIDTH), BF16)
    return pl.pallas_call(
        body, name="hgrn_bwd", grid=(steps,),
        in_specs=[col(P_HQ // HG_WIDTH), col(P_HF // HG_WIDTH), col(P_HI // HG_WIDTH), _full((2, HG_WIDTH)),
                  col(0), pl.BlockSpec((ns, HG_HEADS, HG_DIM, HG_DIM), lambda i: (steps - 1 - i, 0, 0, 0))],
        out_specs=[col(0), col(0), col(0), _full((1, HG_WIDTH))],
        out_shape=[grad, grad, grad, jax.ShapeDtypeStruct((1, HG_WIDTH), F32)],
        scratch_shapes=[pltpu.VMEM((HG_HEADS, HG_DIM, HG_DIM), F32)],
        compiler_params=_params(1),
    )(proj, proj, proj, lbp, do_hg, states)


def _top(x, tgt, o_mla, o_hg, proj, w_out, hg_norm_g, final_g, tm, tq):
    T = x.shape[0]
    assert tm % tq == 0

    def body(x_ref, tgt_ref, om_ref, oh_ref, gm_ref, gh_ref, wout_ref, hgn_ref, fng_ref,
             dx2_ref, dot_ref, dgm_ref, doh_ref, dgh_ref, loss_ref, dfng_ref, dhgn_ref, dwout_ref, ycat_ref):
        @pl.when(pl.program_id(0) == 0)
        def _():
            for ref in (loss_ref, dfng_ref, dhgn_ref, dwout_ref):
                ref[...] = jnp.zeros_like(ref)

        gm, om = gm_ref[...], om_ref[...]
        sgm = _sigmoid(gm)
        silu_m = gm * sgm
        gh, oh, gam = gh_ref[...], oh_ref[...], hgn_ref[...]
        sgh = _sigmoid(gh)
        silu_h = gh * sgh
        rr, nn = [], []
        for hd in range(HG_HEADS):
            oh_h = oh[:, hd * HG_DIM:(hd + 1) * HG_DIM]
            r_h = lax.rsqrt(jnp.mean(oh_h * oh_h, axis=-1, keepdims=True) + EPS)
            rr.append(r_h)
            nn.append(oh_h * r_h)
        n = jnp.concatenate(nn, axis=1)
        ng = n * gam
        ycat_ref[:, :MLA_WIDTH] = (om * silu_m).astype(BF16)
        ycat_ref[:, MLA_WIDTH:] = (ng * silu_h).astype(BF16)
        wout = wout_ref[...]
        x2 = x_ref[...] + _dot(ycat_ref[...], wout)
        r = lax.rsqrt(jnp.mean(x2 * x2, axis=-1, keepdims=True) + EPS)
        xh = x2 * r
        fng = fng_ref[...]
        err = xh * fng - tgt_ref[...]
        loss_ref[...] += 0.5 * jnp.sum(jnp.mean(err * err, axis=-1, keepdims=True), axis=0, keepdims=True)
        dout = err * (1.0 / D_MODEL)
        dfng_ref[...] += jnp.sum(dout * xh, axis=0, keepdims=True)
        dxh = dout * fng
        dx2 = r * (dxh - xh * jnp.mean(dxh * xh, axis=-1, keepdims=True))
        dx2_ref[...] = dx2
        dx2b = dx2.astype(BF16)
        dwout_ref[...] += _dot_tn(ycat_ref[...], dx2b)
        dycat = _dot_nt(dx2b, wout)
        dym, dyh = dycat[:, :MLA_WIDTH], dycat[:, MLA_WIDTH:]
        dom = dym * silu_m
        first = lax.broadcasted_iota(jnp.int32, (tm, LANES), 1) < V_DIM
        for hd in range(N_HEADS):
            pair = dom[:, hd // 2 * LANES:(hd // 2 + 1) * LANES]
            own = jnp.where(first, pair, 0.0) if hd % 2 == 0 else jnp.where(first, 0.0, pair)
            for b in range(tm // tq):
                dot_ref[b, hd * HEAD_LANES:(hd + 1) * HEAD_LANES, :] = own[b * tq:(b + 1) * tq, :].T.astype(BF16)
        dgm_ref[...] = (dym * om * (sgm * (1.0 + gm * (1.0 - sgm)))).astype(BF16)
        dgh_ref[...] = (dyh * ng * (sgh * (1.0 + gh * (1.0 - sgh)))).astype(BF16)
        dng = dyh * silu_h
        dhgn_ref[...] += jnp.sum(dng * n, axis=0, keepdims=True)
        dn = dng * gam
        for hd in range(HG_HEADS):
            sl = slice(hd * HG_DIM, (hd + 1) * HG_DIM)
            dn_h, n_h = dn[:, sl], nn[hd]
            doh_ref[:, sl] = rr[hd] * (dn_h - n_h * jnp.mean(dn_h * n_h, axis=-1, keepdims=True))

    def row(w, cb=0):
        return pl.BlockSpec((tm, w), lambda i: (i, cb))

    hl = N_HEADS * HEAD_LANES
    blocks_t = pl.BlockSpec((tm // tq, hl, tq), lambda i: (i, 0, 0))
    outs = [(D_MODEL, F32), (0, BF16), (MLA_WIDTH, BF16), (HG_WIDTH, F32), (HG_WIDTH, BF16)]
    small = [(1, SMALL_W), (1, D_MODEL), (1, HG_WIDTH), (D_MODEL, D_MODEL)]
    return pl.pallas_call(
        body, name="top", grid=(T // tm,),
        in_specs=[row(D_MODEL), row(D_MODEL), row(MLA_WIDTH), row(HG_WIDTH),
                  row(MLA_WIDTH, P_GM // MLA_WIDTH), row(HG_WIDTH, P_GH // HG_WIDTH),
                  _full((D_MODEL, D_MODEL)), _full((1, HG_WIDTH)), _full((1, D_MODEL))],
        out_specs=[row(w) if w else blocks_t for w, _ in outs] + [_full(s) for s in small],
        out_shape=[jax.ShapeDtypeStruct((T, w) if w else (T // tq, hl, tq), dt) for w, dt in outs]
        + [jax.ShapeDtypeStruct(s, F32) for s in small],
        scratch_shapes=[pltpu.VMEM((tm, D_MODEL), BF16)],
        compiler_params=_params(1),
    )(x, tgt, o_mla, o_hg, proj, proj, w_out, hg_norm_g, final_g)


def _bot(x, dx2, proj, h, qn, kvn, dq, dk, dv, dgm, dhq, dhf, dhi, dgh, c_t, s1_t, s2_t, w_in_t, w_q_p, w_kv_p, ln_g, q_g, kv_g, tm):
    T = x.shape[0]
    lat_w = D_PERM - P_QL
    steps = T // tm

    def body(x_ref, dx2_ref, lat_ref, h_ref, qn_ref, kvn_ref, dq_ref, dk_ref, dv_ref, dgm_ref, dhq_ref, dhf_ref,
             dhi_ref, dgh_ref, c_ref, s1_ref, s2_ref, win_ref, wq_ref, wkv_ref, lng_ref, qg_ref, kvg_ref,
             dx_ref, dlng_ref, dqg_ref, dkvg_ref, dwq_ref, dwkv_ref, dwin_ref, dqpre_ref, dkv_ref, dproj_ref,
             dwin_acc, sem):
        @pl.when(pl.program_id(0) == 0)
        def _():
            for ref in (dlng_ref, dqg_ref, dkvg_ref, dwq_ref, dwkv_ref, dwin_acc):
                ref[...] = jnp.zeros_like(ref)

        c, s1, s2 = c_ref[...], s1_ref[...], s2_ref[...]
        dkpe = jnp.zeros((tm, LANES), F32)
        for hd in range(N_HEADS):
            sl = slice(hd * HEAD_LANES, (hd + 1) * HEAD_LANES)
            dqpre_ref[:, sl] = _rope_bwd(dq_ref[:, sl], c, s1, s2).astype(BF16)
            dk_h = dk_ref[:, sl]
            dkpe = dkpe + dk_h
            dkv_ref[:, sl] = dk_h.astype(BF16)
        dkv_ref[:, N_HEADS * HEAD_LANES:] = dv_ref[...].astype(BF16)
        lane = lax.broadcasted_iota(jnp.int32, (tm, LANES), 1)
        rope_lanes = jnp.logical_and(lane >= ROPE_LO, lane < ROPE_LO + ROPE)
        dkr = jnp.where(rope_lanes, _rope_bwd(dkpe, c, s1, s2), 0.0)

        def norm_bwd(v, g, dy):
            r = lax.rsqrt(jnp.mean(v * v, axis=-1, keepdims=True) + EPS)
            vh = v * r
            dvh = dy * g
            return jnp.sum(dy * vh, axis=0, keepdims=True), r * (dvh - vh * jnp.mean(dvh * vh, axis=-1, keepdims=True))

        dwq_ref[...] += _dot_tn(qn_ref[...], dqpre_ref[...])
        dwkv_ref[...] += _dot_tn(kvn_ref[...], dkv_ref[...])
        dqn = _dot_nt(dqpre_ref[...], wq_ref[...])
        dg_q, dql = norm_bwd(lat_ref[:, :Q_RANK], qg_ref[...], dqn)
        dqg_ref[...] += dg_q
        dkn = _dot_nt(dkv_ref[...], wkv_ref[...])
        dg_kv, dkvl = norm_bwd(lat_ref[:, Q_RANK:Q_RANK + KV_RANK], kvg_ref[...], dkn)
        dkvg_ref[...] += dg_kv

        dproj_ref[:, P_GM:P_GM + MLA_WIDTH] = dgm_ref[...].astype(BF16)
        dproj_ref[:, P_HQ:P_HQ + HG_WIDTH] = dhq_ref[...].astype(BF16)
        dproj_ref[:, P_HF:P_HF + HG_WIDTH] = dhf_ref[...].astype(BF16)
        dproj_ref[:, P_HI:P_HI + HG_WIDTH] = dhi_ref[...].astype(BF16)
        dproj_ref[:, P_GH:P_GH + HG_WIDTH] = dgh_ref[...].astype(BF16)
        dproj_ref[:, P_QL:P_QL + Q_RANK] = dql.astype(BF16)
        dproj_ref[:, P_KVL:P_KVL + KV_RANK] = dkvl.astype(BF16)
        dproj_ref[:, P_KR:P_KR + LANES] = dkr.astype(BF16)
        dh = sum(_dot(dproj_ref[:, col:col + rows.shape[0]], rows) for col, rows in _in_proj_rows(win_ref))
        dg_ln, dxn = norm_bwd(x_ref[...], lng_ref[...], dh)
        dlng_ref[...] += dg_ln
        dx_ref[...] = dx2_ref[...] + dxn
        dwin_acc[...] += _dot_tn(dproj_ref[...], h_ref[...])

        @pl.when(pl.program_id(0) == steps - 1)
        def _():
            kr = P_KR + ROPE_LO
            moves = [((P_GM, P_QL), R_MAIN), ((P_QL, P_KR), (R_QL[0], R_KVL[1])), ((kr, kr + ROPE), R_KR)]
            copies = [pltpu.make_async_copy(dwin_acc.at[a:b, :], dwin_ref.at[c:d, :], sem.at[n])
                      for n, ((a, b), (c, d)) in enumerate(moves)]
            for cp in copies:
                cp.start()
            for cp in copies:
                cp.wait()

    def row(w, cb=0):
        return pl.BlockSpec((tm, w), lambda i: (i, cb))

    hl = N_HEADS * HEAD_LANES
    outs = [(D_MODEL, F32)]
    small = [(1, D_MODEL), (1, Q_RANK), (1, KV_RANK), (Q_RANK, hl), (KV_RANK, hl + MLA_WIDTH)]
    return pl.pallas_call(
        body, name="bot", grid=(steps,),
        in_specs=[row(D_MODEL), row(D_MODEL), row(lat_w, P_QL // lat_w), row(D_MODEL), row(Q_RANK), row(KV_RANK),
                  row(hl), row(hl), row(MLA_WIDTH),
                  row(MLA_WIDTH), row(HG_WIDTH), row(HG_WIDTH), row(HG_WIDTH), row(HG_WIDTH),
                  row(LANES), row(LANES), row(LANES),
                  _full((D_IN, D_MODEL)), _full((Q_RANK, hl)), _full((KV_RANK, hl + MLA_WIDTH)),
                  _full((1, D_MODEL)), _full((1, Q_RANK)), _full((1, KV_RANK))],
        out_specs=[row(w) for w, _ in outs] + [_full(s) for s in small] + [pl.BlockSpec(memory_space=pl.ANY)],
        out_shape=[jax.ShapeDtypeStruct((T, w), dt) for w, dt in outs] + [jax.ShapeDtypeStruct(s, F32) for s in small]
        + [jax.ShapeDtypeStruct((D_IN, D_MODEL), F32)],
        scratch_shapes=[pltpu.VMEM((tm, hl), BF16), pltpu.VMEM((tm, hl + MLA_WIDTH), BF16),
                        pltpu.VMEM((tm, D_PERM), BF16), pltpu.VMEM((D_PERM, D_MODEL), F32),
                        pltpu.SemaphoreType.DMA((3,))],
        compiler_params=_params(1),
    )(x, dx2, proj, h, qn, kvn, dq, dk, dv, dgm, dhq, dhf, dhi, dgh, c_t, s1_t, s2_t, w_in_t, w_q_p, w_kv_p, ln_g,
      q_g, kv_g)


RS_ROWS = 256


def _reduce_scatter(slabs, small):
    n = len(slabs)
    units = []
    for a, s in enumerate(slabs):
        rows, cols = s.shape[1:]
        if rows % RS_ROWS == 0 or rows < RS_ROWS:
            units += [(a, (pl.ds(r0, min(rows, RS_ROWS)), slice(None))) for r0 in range(0, rows, RS_ROWS)]
        else:
            units += [(a, (slice(None), pl.ds(c0, RS_ROWS))) for c0 in range(0, cols, RS_ROWS)]
    nu = len(units)

    def body(*refs):
        ins, small_ref = refs[:n], refs[n]
        outs, small_out = refs[n + 1:2 * n + 1], refs[2 * n + 1]
        own, sib_land, ici_out, ici_land = (refs[(2 + g) * n + 2:(3 + g) * n + 2] for g in range(4))
        small_land = refs[6 * n + 2]
        loc_sems, d2d_send, d2d_recv, ici_send, ici_recv, sm_send, sm_recv = refs[6 * n + 3:6 * n + 10]
        x, y, c = lax.axis_index("x"), lax.axis_index("y"), lax.axis_index("c")
        me = 4 * x + 2 * y + c

        def chip(k):
            return (1 - x if k & 2 else x, 1 - y if k & 1 else y)

        def block(k, core):
            px, py = chip(k)
            return 4 * px + 2 * py + core

        def part(u):
            return units[u]

        def local(u, k):
            a, rows = part(u)
            return pltpu.make_async_copy(ins[a].at[(block(k, c),) + rows], own[a].at[(k,) + rows], loc_sems.at[u, k])

        def to_sibling(u, k):
            a, rows = part(u)
            return pltpu.make_async_remote_copy(
                src_ref=ins[a].at[(block(k, 1 - c),) + rows], dst_ref=sib_land[a].at[(k,) + rows],
                send_sem=d2d_send.at[u, k], recv_sem=d2d_recv.at[u, k], device_id=(x, y, 1 - c), device_id_type=MESH)

        def to_chip(u, k):
            a, rows = part(u)
            return pltpu.make_async_remote_copy(
                src_ref=ici_out[a].at[(k - 1,) + rows], dst_ref=ici_land[a].at[(k - 1,) + rows],
                send_sem=ici_send.at[u, k - 1], recv_sem=ici_recv.at[u, k - 1], device_id=(*chip(k), c),
                device_id_type=MESH)

        def small_copy(k, receiving):
            px, py = chip(k >> 1)
            pc = 1 - c if k & 1 else c
            slot = 4 * px + 2 * py + pc if receiving else me
            return pltpu.make_async_remote_copy(
                src_ref=small_ref, dst_ref=small_land.at[slot], send_sem=sm_send.at[k - 1], recv_sem=sm_recv.at[k - 1],
                device_id=(px, py, pc), device_id_type=MESH)

        for u in range(nu):
            for k in range(4):
                local(u, k).start()
        for u in range(nu):
            for k in range(4):
                to_sibling(u, k).start()
        small_land[me] = small_ref[...]
        for k in range(1, N_DEV):
            small_copy(k, False).start()
        for u in range(nu):
            a, rows = part(u)
            for k in range(4):
                local(u, k).wait()
                to_sibling(u, k).wait_recv()
            for k in range(1, 4):
                ici_out[a][(k - 1,) + rows] = (own[a][(k,) + rows] + sib_land[a][(k,) + rows]).astype(BF16)
                to_chip(u, k).start()
        for u in range(nu):
            a, rows = part(u)
            acc = own[a][(0,) + rows] + sib_land[a][(0,) + rows]
            for k in range(1, 4):
                to_chip(u, k).wait_recv()
                acc = acc + ici_land[a][(k - 1,) + rows].astype(F32)
            outs[a][rows] = acc
        for k in range(1, N_DEV):
            small_copy(k, True).wait_recv()
        acc = small_land[0]
        for d in range(1, N_DEV):
            acc = acc + small_land[d]
        small_out[...] = acc
        for u in range(nu):
            for k in range(4):
                to_sibling(u, k).wait_send()
            for k in range(1, 4):
                to_chip(u, k).wait_send()
        for k in range(1, N_DEV):
            small_copy(k, False).wait_send()

    vm = pl.BlockSpec(memory_space=pltpu.VMEM)
    hbm = pl.BlockSpec(memory_space=pl.ANY)
    dma = pltpu.SemaphoreType.DMA
    return pl.pallas_call(
        body, name="reduce_scatter_grads",
        in_specs=[hbm] * n + [vm], out_specs=[vm] * (n + 1),
        out_shape=[jax.ShapeDtypeStruct(s.shape[1:], F32) for s in slabs] + [jax.ShapeDtypeStruct(small.shape, F32)],
        scratch_shapes=[pltpu.VMEM((4,) + s.shape[1:], F32) for s in slabs] * 2
        + [pltpu.VMEM((3,) + s.shape[1:], BF16) for s in slabs] * 2
        + [pltpu.VMEM((N_DEV,) + small.shape, F32)]
        + [dma((nu, 4)), dma((nu, 4)), dma((nu, 4)), dma((nu, 3)), dma((nu, 3)), dma((N_DEV - 1,)), dma((N_DEV - 1,))],
        compiler_params=pltpu.CompilerParams(vmem_limit_bytes=VMEM_LIMIT),
    )(*slabs, small)


def _adamw_math(w, g, m, v):
    m = ADAM_B1 * m + (1.0 - ADAM_B1) * g
    v = ADAM_B2 * v + (1.0 - ADAM_B2) * (g * g)
    m_hat = m / (1.0 - ADAM_B1 ** ADAM_STEP)
    v_hat = v / (1.0 - ADAM_B2 ** ADAM_STEP)
    delta = -ADAM_LR * (m_hat / (jnp.sqrt(v_hat) + ADAM_EPS) + ADAM_WD * w)
    return delta, m, v


SMALL_W = 512


def _adamw(big, small_w, small_g):
    nb, ns = len(big), len(small_w)

    def body(*refs):
        k = 0
        big_in = [refs[4 * i:4 * i + 4] for i in range(nb)]
        k = 4 * nb
        small_in = [refs[k + 3 * i:k + 3 * i + 3] for i in range(ns)]
        k += 3 * ns
        sg_ref = refs[k]
        k += 1
        big_out = [refs[k + 3 * i:k + 3 * i + 3] for i in range(nb)]
        k += 3 * nb
        small_out = [refs[k + 4 * i:k + 4 * i + 4] for i in range(ns)]

        for (w, g, m, v), (od, om, ov) in zip(big_in, big_out):
            od[...], om[...], ov[...] = _adamw_math(w[...], g[...], m[...], v[...])

        sg = sg_ref[...]
        lbp = small_in[2][0][...]
        lb = _lower_bound(lbp)
        t = sg[4:5, :] * lb * (1.0 - lb)
        grads = [jnp.concatenate([sg[0:1, :], sg[1:2, :]], axis=1),
                 jnp.concatenate([sg[2:3, :], sg[3:4, :]], axis=1),
                 jnp.concatenate([t, -t], axis=0),
                 sg[6:7, :], sg[7:8, 0:Q_RANK], sg[7:8, Q_RANK:Q_RANK + KV_RANK]]
        for (w, m, v), g, (og, od, om, ov) in zip(small_in, grads, small_out):
            og[...] = g
            od[...], om[...], ov[...] = _adamw_math(w[...], g, m[...], v[...])

    ins = [a for grp in big for a in grp] + [a for grp in small_w for a in grp] + [small_g]
    out_shape = ([jax.ShapeDtypeStruct(grp[0].shape, F32) for grp in big for _ in range(3)]
                 + [jax.ShapeDtypeStruct(grp[0].shape, F32) for grp in small_w for _ in range(4)])
    vm = pl.BlockSpec(memory_space=pltpu.VMEM)
    res = pl.pallas_call(
        body, name="adamw", in_specs=[vm] * len(ins), out_specs=[vm] * len(out_shape), out_shape=out_shape,
        compiler_params=pltpu.CompilerParams(vmem_limit_bytes=VMEM_LIMIT),
    )(*ins)
    big_res = [res[3 * i:3 * i + 3] for i in range(nb)]
    small_res = [res[3 * nb + 4 * i:3 * nb + 4 * i + 4] for i in range(ns)]
    return big_res, small_res


def _perm_weights(g_in_t, g_q, g_kv, g_out):
    w_in_t = g_in_t.reshape(D_IN, D_MODEL)
    wq = g_q.transpose(1, 0, 2)
    w_q_p = jnp.pad(wq, ((0, 0), (0, 0), (0, HEAD_LANES - NOPE - ROPE))).reshape(Q_RANK, N_HEADS * HEAD_LANES)
    wkv = g_kv.transpose(1, 0, 2)
    wk = jnp.pad(wkv[:, :, :NOPE], ((0, 0), (0, 0), (0, HEAD_LANES - NOPE))).reshape(KV_RANK, N_HEADS * HEAD_LANES)
    wv = wkv[:, :, NOPE:].reshape(KV_RANK, MLA_WIDTH)
    return w_in_t, w_q_p, jnp.concatenate([wk, wv], axis=1), g_out.reshape(D_MODEL, D_MODEL)


def _grad_slabs(dw_in_t, dw_q_p, dw_kv_p, dw_out):
    s_in = dw_in_t.reshape(N_DEV, D_IN // N_DEV, D_MODEL)
    s_q = dw_q_p.reshape(Q_RANK, N_HEADS, HEAD_LANES)[:, :, :NOPE + ROPE].transpose(1, 0, 2)
    hl = N_HEADS * HEAD_LANES
    dk = dw_kv_p[:, :hl].reshape(KV_RANK, N_HEADS, HEAD_LANES)[:, :, :NOPE]
    dv = dw_kv_p[:, hl:].reshape(KV_RANK, N_HEADS, V_DIM)
    s_kv = jnp.concatenate([dk, dv], axis=2).transpose(1, 0, 2)
    return s_in, s_q, s_kv, dw_out.reshape(N_DEV, D_MODEL // N_DEV, D_MODEL)


def _block_sizes(T):
    return min(256, T), min(256, T), min(512, T)


def kernel(x, positions, ln_g, w_in, q_a_norm_g, w_q_b, kv_a_norm_g, w_kv_b, hg_lower_bounds, hg_norm_g, w_out, final_norm_g, loss_target, m_ln_g, m_w_in, m_q_a_norm_g, m_w_q_b, m_kv_a_norm_g, m_w_kv_b, m_hg_lower_bounds, m_hg_norm_g, m_w_out, m_final_norm_g, v_ln_g, v_w_in, v_q_a_norm_g, v_w_q_b, v_kv_a_norm_g, v_w_kv_b, v_hg_lower_bounds, v_hg_norm_g, v_w_out, v_final_norm_g):
    T = x.shape[1]
    tm, tq, bt = _block_sizes(T)
    nq = T // tq
    xs, tgt = x[0], loss_target[0]
    pos_f = positions.astype(F32)
    fng = final_norm_g.reshape(1, D_MODEL)

    w_in_shard_t = w_in[0].T
    gathered, (c_t, s1_t, s2_t) = _all_gather_weights([w_in_shard_t, w_q_b[0], w_kv_b[0], w_out[0]], pos_f)
    w_in_t, w_q_p, w_kv_p, w_out_b = _perm_weights(*gathered)

    proj, h, qn, kvn, q, k, v, v_t = _fwd_in(
        xs, ln_g, w_in_t, q_a_norm_g, w_q_p, kv_a_norm_g, w_kv_p, c_t, s1_t, s2_t, bt, tq)
    hl = N_HEADS * HEAD_LANES
    k_t = k.reshape(nq, tq, hl).transpose(0, 2, 1)
    q_t = q.reshape(nq, tq, hl).transpose(0, 2, 1)
    o_mla, o_t, lse = _attn_fwd_flat(k, q_t, v_t, tq)
    o_hg, states = _hgrn_fwd(proj, hg_lower_bounds)
    dx2, do_t, d_gm, d_oh, d_gh, loss_p, d_fng, d_hgn, dw_out = _top(
        xs, tgt, o_mla, o_hg, proj, w_out_b, hg_norm_g, fng, tm, tq)
    dq_t, dk, dv = _attn_bwd_flat(k, v, q_t, k_t, do_t, o_t, lse, tq)
    dq = dq_t.transpose(0, 2, 1).reshape(T, N_HEADS * HEAD_LANES)
    d_hq, d_hf, d_hi, d_lb = _hgrn_bwd(proj, hg_lower_bounds, d_oh, states)
    dx, d_lng, d_qg, d_kvg, dw_q_p, dw_kv_p, dw_in_t = _bot(
        xs, dx2, proj, h, qn, kvn, dq, dk, dv, d_gm, d_hq, d_hf, d_hi, d_gh, c_t, s1_t, s2_t, w_in_t, w_q_p, w_kv_p,
        ln_g, q_a_norm_g, kv_a_norm_g, tm)

    small = jnp.concatenate([
        d_lng.reshape(2, SMALL_W), d_fng.reshape(2, SMALL_W), d_lb, loss_p, d_hgn,
        jnp.concatenate([d_qg, d_kvg, jnp.zeros((1, SMALL_W - Q_RANK - KV_RANK), F32)], axis=1)], axis=0)
    g_in, g_q, g_kv, g_out, small_sum = _reduce_scatter(list(_grad_slabs(dw_in_t, dw_q_p, dw_kv_p, dw_out)), small)

    big = [(w_in_shard_t, g_in, m_w_in[0].T, v_w_in[0].T), (w_q_b[0], g_q, m_w_q_b[0], v_w_q_b[0]),
           (w_kv_b[0], g_kv, m_w_kv_b[0], v_w_kv_b[0]), (w_out[0], g_out, m_w_out[0], v_w_out[0])]
    small_w = [(ln_g, m_ln_g, v_ln_g),
               (fng, m_final_norm_g.reshape(1, D_MODEL), v_final_norm_g.reshape(1, D_MODEL)),
               (hg_lower_bounds, m_hg_lower_bounds, v_hg_lower_bounds), (hg_norm_g, m_hg_norm_g, v_hg_norm_g),
               (q_a_norm_g, m_q_a_norm_g, v_q_a_norm_g), (kv_a_norm_g, m_kv_a_norm_g, v_kv_a_norm_g)]
    big_res, small_res = _adamw(big, small_w, small_sum)

    loss = small_sum[5, 0]
    (r_in, r_q, r_kv, r_out) = big_res
    (s_ln, s_fn, s_lb, s_hgn, s_qg, s_kvg) = small_res
    flat = lambda t: t.reshape(D_MODEL)
    lead = lambda t: t[None]
    grads = [s_ln[0], lead(g_in.T), s_qg[0], lead(g_q), s_kvg[0], lead(g_kv), s_lb[0], s_hgn[0], lead(g_out), flat(s_fn[0])]

    def pick(i):
        return [s_ln[i + 1], lead(r_in[i].T), s_qg[i + 1], lead(r_q[i]), s_kvg[i + 1], lead(r_kv[i]), s_lb[i + 1],
                s_hgn[i + 1], lead(r_out[i]), flat(s_fn[i + 1])]

    return (loss, dx[None], *grads, *pick(0), *pick(1), *pick(2))
```

```python
import math

import numpy as np
import jax
import jax.numpy as jnp
from jax import lax
from jax.experimental import pallas as pl
from jax.experimental.pallas import tpu as pltpu

F32 = jnp.float32
BF16 = jnp.bfloat16

D_MODEL = 1024
N_HEADS = 8
NOPE = 64
ROPE = 32
HALF_ROPE = ROPE // 2
V_DIM = 64
Q_RANK = 256
KV_RANK = 128
MLA_WIDTH = N_HEADS * V_DIM
HG_HEADS = 4
HG_DIM = 128
HG_WIDTH = HG_HEADS * HG_DIM
CHUNK = 64
SUB = 16
D_IN = 2976
D_PERM = 3072
ROPE_THETA = 10000.0
EPS = 1e-6
N_DEV = 8
LANES = 128
HEAD_LANES = 128

P_GM, P_HQ, P_HF, P_HI, P_GH, P_QL, P_KVL, P_KR = 0, 512, 1024, 1536, 2048, 2560, 2816, 2944
R_QL, R_KVL, R_KR, R_MAIN = (0, 256), (256, 384), (384, 416), (416, 2976)
ROPE_LO = NOPE
SCALE = 1.0 / math.sqrt(NOPE + ROPE)

ADAM_LR = 0.001
ADAM_B1 = 0.9
ADAM_B2 = 0.999
ADAM_EPS = 1e-08
ADAM_WD = 0.01
ADAM_STEP = 10

VMEM_LIMIT = 56 * 1024 * 1024
MESH = pl.DeviceIdType.MESH

NT = (((1,), (1,)), ((), ()))
TN = (((0,), (0,)), ((), ()))


def _params(n_grid=0, **kw):
    sem = ("arbitrary",) * n_grid if n_grid else None
    return pltpu.CompilerParams(dimension_semantics=sem, vmem_limit_bytes=VMEM_LIMIT, **kw)


def _dot(a, b):
    return jnp.dot(a, b, preferred_element_type=F32)


def _dot_nt(a, b):
    return lax.dot_general(a, b, NT, preferred_element_type=F32)


def _dot_tn(a, b):
    return lax.dot_general(a, b, TN, preferred_element_type=F32)


def _sigmoid(x):
    return 1.0 / (1.0 + jnp.exp(-x))


def _rope_fwd(x, c, s1, s2):
    return x * c + pltpu.roll(x, LANES - HALF_ROPE, 1) * s1 + pltpu.roll(x, HALF_ROPE, 1) * s2


def _rope_bwd(dy, c, s1, s2):
    return dy * c - pltpu.roll(dy, LANES - HALF_ROPE, 1) * s1 - pltpu.roll(dy, HALF_ROPE, 1) * s2


def _in_proj_rows(wt_ref):
    kr = wt_ref[R_KR[0]:R_KR[1], :]
    pad = lambda n: jnp.zeros((n, D_MODEL), kr.dtype)
    return ((P_GM, wt_ref[R_MAIN[0]:R_MAIN[1], :]), (P_QL, wt_ref[R_QL[0]:R_QL[1], :]),
            (P_KVL, wt_ref[R_KVL[0]:R_KVL[1], :]),
            (P_KR, jnp.concatenate([pad(ROPE_LO), kr, pad(LANES - ROPE_LO - ROPE)], axis=0)))


def _full(shape):
    n = len(shape)
    return pl.BlockSpec(shape, lambda *_: (0,) * n)


ROPE_BLOCK = 512


def _rope_constants():
    inv = (np.float32(ROPE_THETA) ** (-np.arange(HALF_ROPE, dtype=np.float32) / np.float32(HALF_ROPE))).astype(np.float32)
    place = np.zeros((3, HALF_ROPE, LANES), np.float32)
    for i in range(HALF_ROPE):
        place[0, i, ROPE_LO + i] = place[0, i, ROPE_LO + HALF_ROPE + i] = 1.0
        place[1, i, ROPE_LO + i] = -1.0
        place[2, i, ROPE_LO + HALF_ROPE + i] = 1.0
    base = np.ones((1, LANES), np.float32)
    base[0, ROPE_LO:ROPE_LO + ROPE] = 0.0
    return jnp.asarray(inv.reshape(HALF_ROPE, 1)), jnp.asarray(place), jnp.asarray(base)


def _rope_block(pos, inv, place_ref, base):
    ang = inv * pos
    cos, sin = jnp.cos(ang), jnp.sin(ang)

    def put(v, k):
        return lax.dot_general(v, place_ref[k], TN, precision=lax.Precision.HIGHEST, preferred_element_type=F32)

    return put(cos, 0) + base, put(sin, 1), put(sin, 2)


def _all_gather_weights(shards, pos_f):
    n = len(shards)
    T = pos_f.shape[1]
    rb = min(ROPE_BLOCK, T)

    def body(*refs):
        ins, (pos_ref, inv_ref, place_ref, base_ref) = refs[:n], refs[n:n + 4]
        outs, tables = refs[n + 4:2 * n + 4], refs[2 * n + 4:2 * n + 7]
        send_sems, recv_sems = refs[2 * n + 7], refs[2 * n + 8]
        x, y, c = lax.axis_index("x"), lax.axis_index("y"), lax.axis_index("c")
        me, sibling = (x, y, c), (x, y, 1 - c)
        chips = [(1 - x, y), (x, 1 - y), (1 - x, 1 - y)]

        def idx(d):
            return 4 * d[0] + 2 * d[1] + d[2]

        def copy(a, k, block, to):
            rows = outs[a].at[idx(block)]
            return pltpu.make_async_remote_copy(src_ref=rows, dst_ref=rows, send_sem=send_sems.at[a, k],
                                                recv_sem=recv_sems.at[a, k], device_id=to, device_id_type=MESH)

        for a in range(n):
            outs[a][idx(me)] = ins[a][...].astype(BF16)
        first = []
        for a in range(n):
            first.append(copy(a, 0, me, sibling))
            first += [copy(a, 1 + j, me, (*chip, c)) for j, chip in enumerate(chips)]
        for cp in first:
            cp.start()
        for r0 in range(0, T, rb):
            for ref, tab in zip(tables, _rope_block(pos_ref[:, r0:r0 + rb], inv_ref[...], place_ref, base_ref[...])):
                ref[r0:r0 + rb, :] = tab
        passed = []
        for j, chip in enumerate(chips):
            for a in range(n):
                copy(a, 1 + j, (*chip, c), me).wait_recv()
                cp = copy(a, 4 + j, (*chip, c), sibling)
                cp.start()
                passed.append(cp)
        for a in range(n):
            copy(a, 0, sibling, me).wait_recv()
            for j, chip in enumerate(chips):
                copy(a, 4 + j, (*chip, 1 - c), me).wait_recv()
        for cp in first + passed:
            cp.wait_send()

    vm = pl.BlockSpec(memory_space=pltpu.VMEM)
    res = pl.pallas_call(
        body, name="all_gather_weights",
        in_specs=[vm] * (n + 4), out_specs=[vm] * (n + 3),
        out_shape=[jax.ShapeDtypeStruct((N_DEV,) + s.shape, BF16) for s in shards]
        + [jax.ShapeDtypeStruct((T, LANES), F32)] * 3,
        scratch_shapes=[pltpu.SemaphoreType.DMA((n, 7)), pltpu.SemaphoreType.DMA((n, 7))],
        compiler_params=pltpu.CompilerParams(vmem_limit_bytes=VMEM_LIMIT),
    )(*shards, pos_f, *_rope_constants())
    return res[:n], res[n:]


def _fwd_in(x, ln_g, w_in_t, q_g, w_q_p, kv_g, w_kv_p, c_t, s1_t, s2_t, tm, tq):
    T = x.shape[0]
    assert tm % tq == 0

    def body(x_ref, lng_ref, win_ref, qg_ref, wq_ref, kvg_ref, wkv_ref, c_ref, s1_ref, s2_ref,
             proj_ref, qn_ref, kvn_ref, q_ref, k_ref, v_ref, vt_ref):
        xv = x_ref[...]
        r = lax.rsqrt(jnp.mean(xv * xv, axis=-1, keepdims=True) + EPS)
        h = (xv * r * lng_ref[...]).astype(BF16)
        for col, rows in _in_proj_rows(win_ref):
            proj_ref[:, col:col + rows.shape[0]] = _dot_nt(h, rows)
        c, s1, s2 = c_ref[...], s1_ref[...], s2_ref[...]

        ql = proj_ref[:, P_QL:P_QL + Q_RANK]
        rq = lax.rsqrt(jnp.mean(ql * ql, axis=-1, keepdims=True) + EPS)
        qn = (ql * rq * qg_ref[...]).astype(BF16)
        qn_ref[...] = qn
        q = _dot(qn, wq_ref[...])
        for hd in range(N_HEADS):
            sl = slice(hd * HEAD_LANES, (hd + 1) * HEAD_LANES)
            q_ref[:, sl] = _rope_fwd(q[:, sl], c, s1, s2).astype(BF16)

        kvl = proj_ref[:, P_KVL:P_KVL + KV_RANK]
        rk = lax.rsqrt(jnp.mean(kvl * kvl, axis=-1, keepdims=True) + EPS)
        kvn = (kvl * rk * kvg_ref[...]).astype(BF16)
        kvn_ref[...] = kvn
        kv = _dot(kvn, wkv_ref[...])
        kpe = _rope_fwd(proj_ref[:, P_KR:P_KR + LANES], c, s1, s2)
        for hd in range(N_HEADS):
            sl = slice(hd * HEAD_LANES, (hd + 1) * HEAD_LANES)
            k_ref[:, sl] = (kv[:, sl] + kpe).astype(BF16)
        v_ref[...] = kv[:, N_HEADS * HEAD_LANES:].astype(BF16)
        for b in range(tm // tq):
            vt_ref[b] = kv[b * tq:(b + 1) * tq, N_HEADS * HEAD_LANES:].T.astype(BF16)

    def row(w):
        return pl.BlockSpec((tm, w), lambda i: (i, 0))

    outs = [(D_PERM, F32), (Q_RANK, BF16), (KV_RANK, BF16),
            (N_HEADS * HEAD_LANES, BF16), (N_HEADS * HEAD_LANES, BF16), (MLA_WIDTH, BF16)]
    return pl.pallas_call(
        body, name="fwd_in", grid=(T // tm,),
        in_specs=[row(D_MODEL), _full((1, D_MODEL)), _full((D_IN, D_MODEL)), _full((1, Q_RANK)),
                  _full((Q_RANK, N_HEADS * HEAD_LANES)), _full((1, KV_RANK)),
                  _full((KV_RANK, N_HEADS * HEAD_LANES + MLA_WIDTH)), row(LANES), row(LANES), row(LANES)],
        out_specs=[row(w) for w, _ in outs] + [pl.BlockSpec((tm // tq, MLA_WIDTH, tq), lambda i: (i, 0, 0))],
        out_shape=[jax.ShapeDtypeStruct((T, w), dt) for w, dt in outs]
        + [jax.ShapeDtypeStruct((T // tq, MLA_WIDTH, tq), BF16)],
        compiler_params=_params(1),
    )(x, ln_g, w_in_t, q_g, w_q_p, kv_g, w_kv_p, c_t, s1_t, s2_t)


LOG2E = 1.4426950408889634
SCALE2 = SCALE * LOG2E


def _causal(tq):
    r = lax.broadcasted_iota(jnp.int32, (tq, tq), 0)
    c = lax.broadcasted_iota(jnp.int32, (tq, tq), 1)
    return r <= c


MASKED = -1e30


def _causal_bias(bias_ref, tq):
    bias_ref[0] = jnp.zeros((tq, tq), F32)
    bias_ref[1] = jnp.where(_causal(tq), 0.0, MASKED)


def _tile_tables(nq, by_query):
    if by_query:
        pairs = [(j, i) for i in range(nq) for j in range(i + 1)]
    else:
        pairs = [(j, i) for j in range(nq) for i in range(nq - 1, j - 1, -1)]
    pairs.append(pairs[-1])
    jj, ii = np.array(pairs, np.int32).T
    return jnp.asarray(jj), jnp.asarray(ii), len(pairs) - 1


ATTN_TRIP = 8


def _walk_tiles(n, products, tile, flush, buf_a, buf_b):
    bufs = (buf_a, buf_b)
    products(0, buf_a)

    def trip(r, carry):
        for u in range(ATTN_TRIP):
            products(ATTN_TRIP * r + u + 1, bufs[(u + 1) % 2])
            tile(ATTN_TRIP * r + u, bufs[u % 2])
        for u in range(ATTN_TRIP):
            flush(ATTN_TRIP * r + u)
        return carry

    lax.fori_loop(0, n // ATTN_TRIP, trip, 0)
    rest = n - n % ATTN_TRIP
    for u in range(n % ATTN_TRIP):
        if rest + u + 1 < n:
            products(rest + u + 1, bufs[(u + 1) % 2])
        tile(rest + u, bufs[u % 2])
    for u in range(n % ATTN_TRIP):
        flush(rest + u)


V_ROWS = V_DIM + 16


def _attn_fwd_flat(k, q_t, v_t, tq):
    T = k.shape[0]
    nq = T // tq
    jj, ii, n = _tile_tables(nq, True)
    heads = [slice(hh * HEAD_LANES, (hh + 1) * HEAD_LANES) for hh in range(2)]

    def body(jj_ref, ii_ref, k_ref, qt_ref, vt_ref, o_ref, ot_ref, lse_ref, sa_ref, sb_ref, m_ref, acc_ref, bias_ref):
        def reset(st):
            m_ref[st] = jnp.full(m_ref.shape[1:], MASKED, F32)
            acc_ref[st] = jnp.zeros(acc_ref.shape[1:], F32)

        _causal_bias(bias_ref, tq)
        for st in range(ATTN_TRIP):
            reset(st)
        extra = (lax.broadcasted_iota(jnp.int32, (V_ROWS - V_DIM, tq), 0) == 0).astype(BF16)

        def products(t, buf):
            j, i = jj_ref[t], ii_ref[t]
            kj = k_ref[pl.ds(pl.multiple_of(j * tq, tq), tq), :]
            for hh, sl in enumerate(heads):
                buf[hh] = _dot(kj[:, sl], qt_ref[i, sl, :])

        def tile(t, buf):
            j, i = jj_ref[t], ii_ref[t]
            vt = vt_ref[j]
            bias = bias_ref.at[(j == i).astype(jnp.int32)]
            st = i % ATTN_TRIP
            for hh in range(2):
                s = buf[hh] * SCALE2 + bias[...]
                m = m_ref[st, hh]
                m_new = jnp.maximum(m, jnp.max(s, axis=0, keepdims=True))
                alpha = jnp.exp2(m - m_new)
                p = jnp.exp2(s - m_new)
                m_ref[st, hh] = m_new
                v_h = jnp.concatenate([vt[hh * V_DIM:(hh + 1) * V_DIM, :], extra], axis=0)
                acc_ref[st, hh] = alpha * acc_ref[st, hh] + _dot(v_h, p.astype(BF16))

        def flush(t):
            j, i = jj_ref[t], ii_ref[t]

            @pl.when(j == i)
            def _():
                st = i % ATTN_TRIP
                den = [acc_ref[st, hh, V_DIM:V_DIM + 1, :] for hh in range(2)]
                out = jnp.concatenate([acc_ref[st, hh, :V_DIM, :] / den[hh] for hh in range(2)], axis=0)
                o_ref[pl.ds(pl.multiple_of(i * tq, tq), tq), :] = out.T
                ot_ref[i] = out
                for hh in range(2):
                    lse_ref[hh, i] = m_ref[st, hh] + jnp.log2(den[hh])
                reset(st)

        _walk_tiles(n, products, tile, flush, sa_ref, sb_ref)

    smem = pl.BlockSpec(memory_space=pltpu.SMEM)
    return pl.pallas_call(
        body, name="attn_fwd", grid=(N_HEADS // 2,),
        in_specs=[smem, smem,
                  pl.BlockSpec((T, 2 * HEAD_LANES), lambda p: (0, p)),
                  pl.BlockSpec((nq, 2 * HEAD_LANES, tq), lambda p: (0, p, 0)),
                  pl.BlockSpec((nq, LANES, tq), lambda p: (0, p, 0))],
        out_specs=[pl.BlockSpec((T, LANES), lambda p: (0, p)), pl.BlockSpec((nq, LANES, tq), lambda p: (0, p, 0)),
                   pl.BlockSpec((2, nq, 1, tq), lambda p: (p, 0, 0, 0))],
        out_shape=[jax.ShapeDtypeStruct((T, MLA_WIDTH), F32), jax.ShapeDtypeStruct((nq, MLA_WIDTH, tq), F32),
                   jax.ShapeDtypeStruct((N_HEADS, nq, 1, tq), F32)],
        scratch_shapes=[pltpu.VMEM((2, tq, tq), F32), pltpu.VMEM((2, tq, tq), F32),
                        pltpu.VMEM((ATTN_TRIP, 2, 1, tq), F32),
                        pltpu.VMEM((ATTN_TRIP, 2, V_ROWS, tq), F32), pltpu.VMEM((2, tq, tq), F32)],
        compiler_params=_params(1),
    )(jj, ii, k, q_t, v_t)


def _attn_bwd_flat(k, v, q_t, k_t, do_t, o_t, lse, tq):
    T = k.shape[0]
    nq = T // tq
    jj, ii, n = _tile_tables(nq, False)
    heads = [slice(hh * HEAD_LANES, (hh + 1) * HEAD_LANES) for hh in range(2)]

    def body(jj_ref, ii_ref, k_ref, v_ref, qt_ref, kt_ref, dot_ref, ot_ref, lse_ref, dqt_ref, dk_ref, dv_ref,
             ba_ref, bb_ref, dkt_ref, dvt_ref, bias_ref, dsum_ref):
        _causal_bias(bias_ref, tq)

        def row_dots(i, carry):
            for hh in range(2):
                own = slice(hh * V_DIM, (hh + 1) * V_DIM)
                do_h = dot_ref[i, hh * HEAD_LANES + own.start:hh * HEAD_LANES + own.stop, :]
                dsum_ref[hh, i] = jnp.sum(do_h.astype(F32) * ot_ref[i, own, :], axis=0, keepdims=True)
            return carry

        lax.fori_loop(0, nq, row_dots, 0)
        dqt_ref[...] = jnp.zeros_like(dqt_ref)
        dkt_ref[...] = jnp.zeros_like(dkt_ref)
        dvt_ref[...] = jnp.zeros_like(dvt_ref)

        def products(t, buf):
            j, i = jj_ref[t], ii_ref[t]
            rows = pl.ds(pl.multiple_of(j * tq, tq), tq)
            for hh, sl in enumerate(heads):
                buf[hh] = _dot(k_ref[rows, sl], qt_ref[i, sl, :])
                buf[2 + hh] = _dot(v_ref[rows, :], dot_ref[i, sl, :])

        def tile(t, buf):
            j, i = jj_ref[t], ii_ref[t]
            bias = bias_ref.at[(j == i).astype(jnp.int32)]
            st = j % ATTN_TRIP
            for hh, sl in enumerate(heads):
                p = jnp.exp2(buf[hh] * SCALE2 + bias[...] - lse_ref[hh, i])
                ds = (p * (buf[2 + hh] - dsum_ref[hh, i]) * SCALE).astype(BF16)
                own = slice(hh * V_DIM, (hh + 1) * V_DIM)
                do_h = dot_ref[i, hh * HEAD_LANES + own.start:hh * HEAD_LANES + own.stop, :]
                dvt_ref[st, own, :] += _dot_nt(do_h, p.astype(BF16))
                used = slice(sl.start, sl.start + NOPE + ROPE)
                dkt_ref[st, used, :] += _dot_nt(qt_ref[i, used, :], ds)
                dqt_ref[i, used, :] += _dot(kt_ref[j, used, :], ds)

        def flush(t):
            j, i = jj_ref[t], ii_ref[t]

            @pl.when(j == i)
            def _():
                st = j % ATTN_TRIP
                rows = pl.ds(pl.multiple_of(j * tq, tq), tq)
                dk_ref[rows, :] = dkt_ref[st].T
                dv_ref[rows, :] = dvt_ref[st].T.astype(BF16)
                dkt_ref[st] = jnp.zeros(dkt_ref.shape[1:], F32)
                dvt_ref[st] = jnp.zeros(dvt_ref.shape[1:], F32)

        _walk_tiles(n, products, tile, flush, ba_ref, bb_ref)

    smem = pl.BlockSpec(memory_space=pltpu.SMEM)
    stat = pl.BlockSpec((2, nq, 1, tq), lambda p: (p, 0, 0, 0))
    blocks_t = pl.BlockSpec((nq, 2 * HEAD_LANES, tq), lambda p: (0, p, 0))
    return pl.pallas_call(
        body, name="attn_bwd", grid=(N_HEADS // 2,),
        in_specs=[smem, smem,
                  pl.BlockSpec((T, 2 * HEAD_LANES), lambda p: (0, p)),
                  pl.BlockSpec((T, LANES), lambda p: (0, p)),
                  blocks_t, blocks_t, blocks_t, pl.BlockSpec((nq, LANES, tq), lambda p: (0, p, 0)), stat],
        out_specs=[blocks_t,
                   pl.BlockSpec((T, 2 * HEAD_LANES), lambda p: (0, p)),
                   pl.BlockSpec((T, LANES), lambda p: (0, p))],
        out_shape=[jax.ShapeDtypeStruct((nq, N_HEADS * HEAD_LANES, tq), F32),
                   jax.ShapeDtypeStruct((T, N_HEADS * HEAD_LANES), F32),
                   jax.ShapeDtypeStruct((T, MLA_WIDTH), BF16)],
        scratch_shapes=[pltpu.VMEM((4, tq, tq), F32), pltpu.VMEM((4, tq, tq), F32),
                        pltpu.VMEM((ATTN_TRIP, 2 * HEAD_LANES, tq), F32), pltpu.VMEM((ATTN_TRIP, LANES, tq), F32),
                        pltpu.VMEM((2, tq, tq), F32), pltpu.VMEM((2, nq, 1, tq), F32)],
        compiler_params=_params(1),
    )(jj, ii, k, v, q_t, k_t, do_t, o_t, lse)


def _lower_bound(lbp):
    a, b = lbp[0:1, :], lbp[1:2, :]
    mx = jnp.maximum(a, b)
    ea, eb = jnp.exp(a - mx), jnp.exp(b - mx)
    return ea / (ea + eb)


def _tri(lower):
    r = lax.broadcasted_iota(jnp.int32, (CHUNK, CHUNK), 0)
    c = lax.broadcasted_iota(jnp.int32, (CHUNK, CHUNK), 1)
    return (c <= r) if lower else (c >= r)


def _running_sum(x, from_end):
    row = lax.broadcasted_iota(jnp.int32, x.shape, 0)
    step = 1
    while step < CHUNK:
        if from_end:
            x = x + jnp.where(row < CHUNK - step, pltpu.roll(x, CHUNK - step, 0), 0.0)
        else:
            x = x + jnp.where(row >= step, pltpu.roll(x, step, 0), 0.0)
        step *= 2
    return x


def _hg_gates(hq, hf, lb):
    sq = _sigmoid(hq)
    sf = _sigmoid(hf)
    f = lb + (1.0 - lb) * sf
    g = jnp.log(f)
    gcum = _running_sum(g, False)
    return sq, sf, f, hq * sq, 1.0 - f, gcum


def _head(x, hd):
    return x[:, hd * HG_DIM:(hd + 1) * HG_DIM]


def _all_heads(fn):
    return jnp.concatenate([fn(hd) for hd in range(HG_HEADS)], axis=1)


def _hg_blocks(q, kk, gcum):
    rowi = lax.broadcasted_iota(jnp.int32, gcum.shape, 0)
    out = []
    for blk in range(CHUNK // SUB):
        lo, hi = blk * SUB, (blk + 1) * SUB
        gb = gcum[lo - 1:lo, :] if blk else jnp.zeros_like(gcum[0:1, :])
        eq = jnp.exp(gcum[lo:hi, :] - gb)
        ek = jnp.exp(jnp.where(rowi < hi, gb - gcum, 0.0))
        out.append((eq, ek, (q[lo:hi, :] * eq).astype(BF16), (kk * ek).astype(BF16)))
    return out


def _hg_scores(blocks):
    out = []
    for hd in range(HG_HEADS):
        a = jnp.concatenate([_dot_nt(_head(qb, hd), _head(kb, hd)) for _, _, qb, kb in blocks], axis=0)
        out.append(jnp.where(_tri(True), a, 0.0))
    return out


HG_STEP_CHUNKS = 8


def _hgrn_fwd(proj, lbp):
    T = proj.shape[0]
    nc = T // CHUNK
    ns = min(HG_STEP_CHUNKS, nc)
    rows = ns * CHUNK

    def body(hq_ref, hf_ref, hi_ref, lbp_ref, o_ref, st_ref, state):
        @pl.when(pl.program_id(0) == 0)
        def _():
            state[...] = jnp.zeros_like(state)

        lb = _lower_bound(lbp_ref[...])
        work = []
        for c in range(ns):
            r = slice(c * CHUNK, (c + 1) * CHUNK)
            _, _, _, q, kk, gcum = _hg_gates(hq_ref[r, :], hf_ref[r, :], lb)
            vb = hi_ref[r, :].astype(BF16)
            a = _hg_scores(_hg_blocks(q, kk, gcum))
            gend = gcum[CHUNK - 1:CHUNK, :]
            qgb = (q * jnp.exp(gcum)).astype(BF16)
            kgeb = (kk * jnp.exp(gend - gcum)).astype(BF16)
            intra = [_dot(a[hd].astype(BF16), _head(vb, hd)) for hd in range(HG_HEADS)]
            update = [_dot_tn(_head(vb, hd), _head(kgeb, hd)) for hd in range(HG_HEADS)]
            work.append((qgb, jnp.exp(gend), intra, update))
        for hd in range(HG_HEADS):
            st = state[hd]
            for c, (qgb, egend, intra, update) in enumerate(work):
                st_ref[c, hd] = st
                o_ref[c * CHUNK:(c + 1) * CHUNK, hd * HG_DIM:(hd + 1) * HG_DIM] = (
                    intra[hd] + _dot_nt(_head(qgb, hd), st.astype(BF16)))
                st = st * _head(egend, hd) + update[hd]
            state[hd] = st

    def col(cb):
        return pl.BlockSpec((rows, HG_WIDTH), lambda i: (i, cb))

    return pl.pallas_call(
        body, name="hgrn_fwd", grid=(nc // ns,),
        in_specs=[col(P_HQ // HG_WIDTH), col(P_HF // HG_WIDTH), col(P_HI // HG_WIDTH), _full((2, HG_WIDTH))],
        out_specs=[pl.BlockSpec((rows, HG_WIDTH), lambda i: (i, 0)),
                   pl.BlockSpec((ns, HG_HEADS, HG_DIM, HG_DIM), lambda i: (i, 0, 0, 0))],
        out_shape=[jax.ShapeDtypeStruct((T, HG_WIDTH), F32),
                   jax.ShapeDtypeStruct((nc, HG_HEADS, HG_DIM, HG_DIM), F32)],
        scratch_shapes=[pltpu.VMEM((HG_HEADS, HG_DIM, HG_DIM), F32)],
        compiler_params=_params(1),
    )(proj, proj, proj, lbp)


def _hgrn_bwd(proj, lbp, do_hg, states):
    T = proj.shape[0]
    nc = T // CHUNK
    ns = min(HG_STEP_CHUNKS, nc)
    rows = ns * CHUNK
    steps = nc // ns

    def body(hq_ref, hf_ref, hi_ref, lbp_ref, do_ref, st_ref, dhq_ref, dhf_ref, dhi_ref, dlb_ref, dstate):
        @pl.when(pl.program_id(0) == 0)
        def _():
            dstate[...] = jnp.zeros_like(dstate)
            dlb_ref[...] = jnp.zeros_like(dlb_ref)

        lb = _lower_bound(lbp_ref[...])

        dst_all = [dstate[hd] for hd in range(HG_HEADS)]
        dlb = jnp.zeros_like(lb)
        last = lax.broadcasted_iota(jnp.int32, (CHUNK, HG_WIDTH), 0) == CHUNK - 1
        for c in reversed(range(ns)):
            r = slice(c * CHUNK, (c + 1) * CHUNK)
            hq = hq_ref[r, :]
            sq, sf, f, q, kk, gcum = _hg_gates(hq, hf_ref[r, :], lb)
            vb = hi_ref[r, :].astype(BF16)
            dob = do_ref[r, :].astype(BF16)
            blocks = _hg_blocks(q, kk, gcum)
            a = _hg_scores(blocks)
            gend = gcum[CHUNK - 1:CHUNK, :]
            eg, egend, ekend = jnp.exp(gcum), jnp.exp(gend), jnp.exp(gend - gcum)
            qg, kge = q * eg, kk * ekend
            qgb, kgeb = qg.astype(BF16), kge.astype(BF16)

            dv, dqg, dkge, st_dst, dq_blk, dk_blk = [], [], [], [], [], []
            for hd in range(HG_HEADS):
                st = st_ref[c, hd]
                dst = dst_all[hd]
                dstb = dst.astype(BF16)
                do_h, v_h = _head(dob, hd), _head(vb, hd)
                dv.append(_dot_tn(a[hd].astype(BF16), do_h) + _dot_nt(_head(kgeb, hd), dstb))
                da = jnp.where(_tri(True), _dot_nt(do_h, v_h), 0.0).astype(BF16)
                dqg.append(_dot(do_h, st.astype(BF16)))
                dkge.append(_dot(v_h, dstb))
                st_dst.append(jnp.sum(st * dst, axis=0, keepdims=True))
                dst_all[hd] = _dot_tn(do_h, _head(qgb, hd)) + dst * _head(egend, hd)
                dq_blk.append([_dot(da[b * SUB:(b + 1) * SUB, :], _head(kb, hd)) for b, (_, _, _, kb) in enumerate(blocks)])
                dk_blk.append([_dot_tn(da[b * SUB:(b + 1) * SUB, :], _head(qb, hd)) for b, (_, _, qb, _) in enumerate(blocks)])
            dv, dqg, dkge, st_dst = (jnp.concatenate(t, axis=1) for t in (dv, dqg, dkge, st_dst))

            dq_a, dg_q = [], []
            dk_a, dg_k = jnp.zeros_like(gcum), jnp.zeros_like(gcum)
            for b, (eq, ek, qb, kb) in enumerate(blocks):
                dq_b = _all_heads(lambda hd: dq_blk[hd][b])
                dk_b = _all_heads(lambda hd: dk_blk[hd][b])
                dq_a.append(dq_b * eq)
                dk_a = dk_a + dk_b * ek
                dg_q.append(qb.astype(F32) * dq_b)
                dg_k = dg_k + kb.astype(F32) * dk_b
            dq_a = jnp.concatenate(dq_a, axis=0)

            dgend = st_dst * egend + jnp.sum(dkge * kge, axis=0, keepdims=True)
            dq = dq_a + dqg * eg
            dk = dk_a + dkge * ekend
            dgc = jnp.concatenate(dg_q, axis=0) - dg_k + dqg * qg - dkge * kge + jnp.where(last, dgend, 0.0)
            dg = _running_sum(dgc, True)
            df = dg / f - dk
            dhf_ref[r, :] = (df * (1.0 - lb) * sf * (1.0 - sf)).astype(BF16)
            dlb = dlb + jnp.sum(df * (1.0 - sf), axis=0, keepdims=True)
            dhq_ref[r, :] = (dq * (sq * (1.0 + hq * (1.0 - sq)))).astype(BF16)
            dhi_ref[r, :] = dv.astype(BF16)
        for hd in range(HG_HEADS):
            dstate[hd] = dst_all[hd]
        dlb_ref[...] += dlb

    def col(cb):
        return pl.BlockSpec((rows, HG_WIDTH), lambda i: (steps - 1 - i, cb))

    grad = jax.ShapeDtypeStruct((T, HG_WIDTH), BF16)
    return pl.pallas_call(
        body, name="hgrn_bwd", grid=(steps,),
        in_specs=[col(P_HQ // HG_WIDTH), col(P_HF // HG_WIDTH), col(P_HI // HG_WIDTH), _full((2, HG_WIDTH)),
                  col(0), pl.BlockSpec((ns, HG_HEADS, HG_DIM, HG_DIM), lambda i: (steps - 1 - i, 0, 0, 0))],
        out_specs=[col(0), col(0), col(0), _full((1, HG_WIDTH))],
        out_shape=[grad, grad, grad, jax.ShapeDtypeStruct((1, HG_WIDTH), F32)],
        scratch_shapes=[pltpu.VMEM((HG_HEADS, HG_DIM, HG_DIM), F32)],
        compiler_params=_params(1),
    )(proj, proj, proj, lbp, do_hg, states)


def _top(x, tgt, o_mla, o_hg, proj, w_out, hg_norm_g, final_g, tm, tq):
    T = x.shape[0]
    assert tm % tq == 0

    def body(x_ref, tgt_ref, om_ref, oh_ref, gm_ref, gh_ref, wout_ref, hgn_ref, fng_ref,
             dx2_ref, dot_ref, dgm_ref, doh_ref, dgh_ref, loss_ref, dfng_ref, dhgn_ref, dwout_ref, ycat_ref):
        @pl.when(pl.program_id(0) == 0)
        def _():
            for ref in (loss_ref, dfng_ref, dhgn_ref, dwout_ref):
                ref[...] = jnp.zeros_like(ref)

        gm, om = gm_ref[...], om_ref[...]
        sgm = _sigmoid(gm)
        silu_m = gm * sgm
        gh, oh, gam = gh_ref[...], oh_ref[...], hgn_ref[...]
        sgh = _sigmoid(gh)
        silu_h = gh * sgh
        rr, nn = [], []
        for hd in range(HG_HEADS):
            oh_h = oh[:, hd * HG_DIM:(hd + 1) * HG_DIM]
            r_h = lax.rsqrt(jnp.mean(oh_h * oh_h, axis=-1, keepdims=True) + EPS)
            rr.append(r_h)
            nn.append(oh_h * r_h)
        n = jnp.concatenate(nn, axis=1)
        ng = n * gam
        ycat_ref[:, :MLA_WIDTH] = (om * silu_m).astype(BF16)
        ycat_ref[:, MLA_WIDTH:] = (ng * silu_h).astype(BF16)
        wout = wout_ref[...]
        x2 = x_ref[...] + _dot(ycat_ref[...], wout)
        r = lax.rsqrt(jnp.mean(x2 * x2, axis=-1, keepdims=True) + EPS)
        xh = x2 * r
        fng = fng_ref[...]
        err = xh * fng - tgt_ref[...]
        loss_ref[...] += 0.5 * jnp.sum(jnp.mean(err * err, axis=-1, keepdims=True), axis=0, keepdims=True)
        dout = err * (1.0 / D_MODEL)
        dfng_ref[...] += jnp.sum(dout * xh, axis=0, keepdims=True)
        dxh = dout * fng
        dx2 = r * (dxh - xh * jnp.mean(dxh * xh, axis=-1, keepdims=True))
        dx2_ref[...] = dx2
        dx2b = dx2.astype(BF16)
        dwout_ref[...] += _dot_tn(ycat_ref[...], dx2b)
        dycat = _dot_nt(dx2b, wout)
        dym, dyh = dycat[:, :MLA_WIDTH], dycat[:, MLA_WIDTH:]
        dom = dym * silu_m
        first = lax.broadcasted_iota(jnp.int32, (tm, LANES), 1) < V_DIM
        for hd in range(N_HEADS):
            pair = dom[:, hd // 2 * LANES:(hd // 2 + 1) * LANES]
            own = jnp.where(first, pair, 0.0) if hd % 2 == 0 else jnp.where(first, 0.0, pair)
            for b in range(tm // tq):
                dot_ref[b, hd * HEAD_LANES:(hd + 1) * HEAD_LANES, :] = own[b * tq:(b + 1) * tq, :].T.astype(BF16)
        dgm_ref[...] = (dym * om * (sgm * (1.0 + gm * (1.0 - sgm)))).astype(BF16)
        dgh_ref[...] = (dyh * ng * (sgh * (1.0 + gh * (1.0 - sgh)))).astype(BF16)
        dng = dyh * silu_h
        dhgn_ref[...] += jnp.sum(dng * n, axis=0, keepdims=True)
        dn = dng * gam
        for hd in range(HG_HEADS):
            sl = slice(hd * HG_DIM, (hd + 1) * HG_DIM)
            dn_h, n_h = dn[:, sl], nn[hd]
            doh_ref[:, sl] = rr[hd] * (dn_h - n_h * jnp.mean(dn_h * n_h, axis=-1, keepdims=True))

    def row(w, cb=0):
        return pl.BlockSpec((tm, w), lambda i: (i, cb))

    hl = N_HEADS * HEAD_LANES
    blocks_t = pl.BlockSpec((tm // tq, hl, tq), lambda i: (i, 0, 0))
    outs = [(D_MODEL, F32), (0, BF16), (MLA_WIDTH, BF16), (HG_WIDTH, F32), (HG_WIDTH, BF16)]
    small = [(1, SMALL_W), (1, D_MODEL), (1, HG_WIDTH), (D_MODEL, D_MODEL)]
    return pl.pallas_call(
        body, name="top", grid=(T // tm,),
        in_specs=[row(D_MODEL), row(D_MODEL), row(MLA_WIDTH), row(HG_WIDTH),
                  row(MLA_WIDTH, P_GM // MLA_WIDTH), row(HG_WIDTH, P_GH // HG_WIDTH),
                  _full((D_MODEL, D_MODEL)), _full((1, HG_WIDTH)), _full((1, D_MODEL))],
        out_specs=[row(w) if w else blocks_t for w, _ in outs] + [_full(s) for s in small],
        out_shape=[jax.ShapeDtypeStruct((T, w) if w else (T // tq, hl, tq), dt) for w, dt in outs]
        + [jax.ShapeDtypeStruct(s, F32) for s in small],
        scratch_shapes=[pltpu.VMEM((tm, D_MODEL), BF16)],
        compiler_params=_params(1),
    )(x, tgt, o_mla, o_hg, proj, proj, w_out, hg_norm_g, final_g)


def _bot(x, dx2, proj, qn, kvn, dq, dk, dv, dgm, dhq, dhf, dhi, dgh, c_t, s1_t, s2_t, w_in_t, w_q_p, w_kv_p, ln_g, q_g, kv_g, tm):
    T = x.shape[0]
    lat_w = D_PERM - P_QL
    steps = T // tm

    def body(x_ref, dx2_ref, lat_ref, qn_ref, kvn_ref, dq_ref, dk_ref, dv_ref, dgm_ref, dhq_ref, dhf_ref,
             dhi_ref, dgh_ref, c_ref, s1_ref, s2_ref, win_ref, wq_ref, wkv_ref, lng_ref, qg_ref, kvg_ref,
             dx_ref, dlng_ref, dqg_ref, dkvg_ref, dwq_ref, dwkv_ref, dwin_ref, dqpre_ref, dkv_ref, dproj_ref,
             dwin_acc, sem):
        @pl.when(pl.program_id(0) == 0)
        def _():
            for ref in (dlng_ref, dqg_ref, dkvg_ref, dwq_ref, dwkv_ref, dwin_acc):
                ref[...] = jnp.zeros_like(ref)

        c, s1, s2 = c_ref[...], s1_ref[...], s2_ref[...]
        dkpe = jnp.zeros((tm, LANES), F32)
        for hd in range(N_HEADS):
            sl = slice(hd * HEAD_LANES, (hd + 1) * HEAD_LANES)
            dqpre_ref[:, sl] = _rope_bwd(dq_ref[:, sl], c, s1, s2).astype(BF16)
            dk_h = dk_ref[:, sl]
            dkpe = dkpe + dk_h
            dkv_ref[:, sl] = dk_h.astype(BF16)
        dkv_ref[:, N_HEADS * HEAD_LANES:] = dv_ref[...].astype(BF16)
        lane = lax.broadcasted_iota(jnp.int32, (tm, LANES), 1)
        rope_lanes = jnp.logical_and(lane >= ROPE_LO, lane < ROPE_LO + ROPE)
        dkr = jnp.where(rope_lanes, _rope_bwd(dkpe, c, s1, s2), 0.0)

        def norm_bwd(v, g, dy):
            r = lax.rsqrt(jnp.mean(v * v, axis=-1, keepdims=True) + EPS)
            vh = v * r
            dvh = dy * g
            return jnp.sum(dy * vh, axis=0, keepdims=True), r * (dvh - vh * jnp.mean(dvh * vh, axis=-1, keepdims=True))

        dwq_ref[...] += _dot_tn(qn_ref[...], dqpre_ref[...])
        dwkv_ref[...] += _dot_tn(kvn_ref[...], dkv_ref[...])
        dqn = _dot_nt(dqpre_ref[...], wq_ref[...])
        dg_q, dql = norm_bwd(lat_ref[:, :Q_RANK], qg_ref[...], dqn)
        dqg_ref[...] += dg_q
        dkn = _dot_nt(dkv_ref[...], wkv_ref[...])
        dg_kv, dkvl = norm_bwd(lat_ref[:, Q_RANK:Q_RANK + KV_RANK], kvg_ref[...], dkn)
        dkvg_ref[...] += dg_kv

        dproj_ref[:, P_GM:P_GM + MLA_WIDTH] = dgm_ref[...].astype(BF16)
        dproj_ref[:, P_HQ:P_HQ + HG_WIDTH] = dhq_ref[...].astype(BF16)
        dproj_ref[:, P_HF:P_HF + HG_WIDTH] = dhf_ref[...].astype(BF16)
        dproj_ref[:, P_HI:P_HI + HG_WIDTH] = dhi_ref[...].astype(BF16)
        dproj_ref[:, P_GH:P_GH + HG_WIDTH] = dgh_ref[...].astype(BF16)
        dproj_ref[:, P_QL:P_QL + Q_RANK] = dql.astype(BF16)
        dproj_ref[:, P_KVL:P_KVL + KV_RANK] = dkvl.astype(BF16)
        dproj_ref[:, P_KR:P_KR + LANES] = dkr.astype(BF16)
        dh = sum(_dot(dproj_ref[:, col:col + rows.shape[0]], rows) for col, rows in _in_proj_rows(win_ref))
        dg_ln, dxn = norm_bwd(x_ref[...], lng_ref[...], dh)
        dlng_ref[...] += dg_ln
        dx_ref[...] = dx2_ref[...] + dxn
        xv = x_ref[...]
        h = (xv * lax.rsqrt(jnp.mean(xv * xv, axis=-1, keepdims=True) + EPS) * lng_ref[...]).astype(BF16)
        dwin_acc[...] += _dot_tn(dproj_ref[...], h)

        @pl.when(pl.program_id(0) == steps - 1)
        def _():
            kr = P_KR + ROPE_LO
            moves = [((P_GM, P_QL), R_MAIN), ((P_QL, P_KR), (R_QL[0], R_KVL[1])), ((kr, kr + ROPE), R_KR)]
            copies = [pltpu.make_async_copy(dwin_acc.at[a:b, :], dwin_ref.at[c:d, :], sem.at[n])
                      for n, ((a, b), (c, d)) in enumerate(moves)]
            for cp in copies:
                cp.start()
            for cp in copies:
                cp.wait()

    def row(w, cb=0):
        return pl.BlockSpec((tm, w), lambda i: (i, cb))

    hl = N_HEADS * HEAD_LANES
    outs = [(D_MODEL, F32)]
    small = [(1, D_MODEL), (1, Q_RANK), (1, KV_RANK), (Q_RANK, hl), (KV_RANK, hl + MLA_WIDTH)]
    return pl.pallas_call(
        body, name="bot", grid=(steps,),
        in_specs=[row(D_MODEL), row(D_MODEL), row(lat_w, P_QL // lat_w), row(Q_RANK), row(KV_RANK),
                  row(hl), row(hl), row(MLA_WIDTH),
                  row(MLA_WIDTH), row(HG_WIDTH), row(HG_WIDTH), row(HG_WIDTH), row(HG_WIDTH),
                  row(LANES), row(LANES), row(LANES),
                  _full((D_IN, D_MODEL)), _full((Q_RANK, hl)), _full((KV_RANK, hl + MLA_WIDTH)),
                  _full((1, D_MODEL)), _full((1, Q_RANK)), _full((1, KV_RANK))],
        out_specs=[row(w) for w, _ in outs] + [_full(s) for s in small] + [pl.BlockSpec(memory_space=pl.ANY)],
        out_shape=[jax.ShapeDtypeStruct((T, w), dt) for w, dt in outs] + [jax.ShapeDtypeStruct(s, F32) for s in small]
        + [jax.ShapeDtypeStruct((D_IN, D_MODEL), F32)],
        scratch_shapes=[pltpu.VMEM((tm, hl), BF16), pltpu.VMEM((tm, hl + MLA_WIDTH), BF16),
                        pltpu.VMEM((tm, D_PERM), BF16), pltpu.VMEM((D_PERM, D_MODEL), F32),
                        pltpu.SemaphoreType.DMA((3,))],
        compiler_params=_params(1),
    )(x, dx2, proj, qn, kvn, dq, dk, dv, dgm, dhq, dhf, dhi, dgh, c_t, s1_t, s2_t, w_in_t, w_q_p, w_kv_p, ln_g,
      q_g, kv_g)


RS_ROWS = 256


def _reduce_scatter(slabs, small):
    n = len(slabs)
    units = []
    for a, s in enumerate(slabs):
        rows, cols = s.shape[1:]
        if rows % RS_ROWS == 0 or rows < RS_ROWS:
            units += [(a, (pl.ds(r0, min(rows, RS_ROWS)), slice(None))) for r0 in range(0, rows, RS_ROWS)]
        else:
            units += [(a, (slice(None), pl.ds(c0, RS_ROWS))) for c0 in range(0, cols, RS_ROWS)]
    nu = len(units)

    def body(*refs):
        ins, small_ref = refs[:n], refs[n]
        outs, small_out = refs[n + 1:2 * n + 1], refs[2 * n + 1]
        own, sib_land, ici_out, ici_land = (refs[(2 + g) * n + 2:(3 + g) * n + 2] for g in range(4))
        small_land = refs[6 * n + 2]
        loc_sems, d2d_send, d2d_recv, ici_send, ici_recv, sm_send, sm_recv = refs[6 * n + 3:6 * n + 10]
        x, y, c = lax.axis_index("x"), lax.axis_index("y"), lax.axis_index("c")
        me = 4 * x + 2 * y + c

        def chip(k):
            return (1 - x if k & 2 else x, 1 - y if k & 1 else y)

        def block(k, core):
            px, py = chip(k)
            return 4 * px + 2 * py + core

        def part(u):
            return units[u]

        def local(u, k):
            a, rows = part(u)
            return pltpu.make_async_copy(ins[a].at[(block(k, c),) + rows], own[a].at[(k,) + rows], loc_sems.at[u, k])

        def to_sibling(u, k):
            a, rows = part(u)
            return pltpu.make_async_remote_copy(
                src_ref=ins[a].at[(block(k, 1 - c),) + rows], dst_ref=sib_land[a].at[(k,) + rows],
                send_sem=d2d_send.at[u, k], recv_sem=d2d_recv.at[u, k], device_id=(x, y, 1 - c), device_id_type=MESH)

        def to_chip(u, k):
            a, rows = part(u)
            return pltpu.make_async_remote_copy(
                src_ref=ici_out[a].at[(k - 1,) + rows], dst_ref=ici_land[a].at[(k - 1,) + rows],
                send_sem=ici_send.at[u, k - 1], recv_sem=ici_recv.at[u, k - 1], device_id=(*chip(k), c),
                device_id_type=MESH)

        def small_copy(k, receiving):
            px, py = chip(k >> 1)
            pc = 1 - c if k & 1 else c
            slot = 4 * px + 2 * py + pc if receiving else me
            return pltpu.make_async_remote_copy(
                src_ref=small_ref, dst_ref=small_land.at[slot], send_sem=sm_send.at[k - 1], recv_sem=sm_recv.at[k - 1],
                device_id=(px, py, pc), device_id_type=MESH)

        for u in range(nu):
            for k in range(4):
                local(u, k).start()
        for u in range(nu):
            for k in range(4):
                to_sibling(u, k).start()
        small_land[me] = small_ref[...]
        for k in range(1, N_DEV):
            small_copy(k, False).start()
        for u in range(nu):
            a, rows = part(u)
            for k in range(4):
                local(u, k).wait()
                to_sibling(u, k).wait_recv()
            for k in range(1, 4):
                ici_out[a][(k - 1,) + rows] = (own[a][(k,) + rows] + sib_land[a][(k,) + rows]).astype(BF16)
                to_chip(u, k).start()
        for u in range(nu):
            a, rows = part(u)
            acc = own[a][(0,) + rows] + sib_land[a][(0,) + rows]
            for k in range(1, 4):
                to_chip(u, k).wait_recv()
                acc = acc + ici_land[a][(k - 1,) + rows].astype(F32)
            outs[a][rows] = acc
        for k in range(1, N_DEV):
            small_copy(k, True).wait_recv()
        acc = small_land[0]
        for d in range(1, N_DEV):
            acc = acc + small_land[d]
        small_out[...] = acc
        for u in range(nu):
            for k in range(4):
                to_sibling(u, k).wait_send()
            for k in range(1, 4):
                to_chip(u, k).wait_send()
        for k in range(1, N_DEV):
            small_copy(k, False).wait_send()

    vm = pl.BlockSpec(memory_space=pltpu.VMEM)
    hbm = pl.BlockSpec(memory_space=pl.ANY)
    dma = pltpu.SemaphoreType.DMA
    return pl.pallas_call(
        body, name="reduce_scatter_grads",
        in_specs=[hbm] * n + [vm], out_specs=[vm] * (n + 1),
        out_shape=[jax.ShapeDtypeStruct(s.shape[1:], F32) for s in slabs] + [jax.ShapeDtypeStruct(small.shape, F32)],
        scratch_shapes=[pltpu.VMEM((4,) + s.shape[1:], F32) for s in slabs] * 2
        + [pltpu.VMEM((3,) + s.shape[1:], BF16) for s in slabs] * 2
        + [pltpu.VMEM((N_DEV,) + small.shape, F32)]
        + [dma((nu, 4)), dma((nu, 4)), dma((nu, 4)), dma((nu, 3)), dma((nu, 3)), dma((N_DEV - 1,)), dma((N_DEV - 1,))],
        compiler_params=pltpu.CompilerParams(vmem_limit_bytes=VMEM_LIMIT),
    )(*slabs, small)


def _adamw_math(w, g, m, v):
    m = ADAM_B1 * m + (1.0 - ADAM_B1) * g
    v = ADAM_B2 * v + (1.0 - ADAM_B2) * (g * g)
    m_hat = m / (1.0 - ADAM_B1 ** ADAM_STEP)
    v_hat = v / (1.0 - ADAM_B2 ** ADAM_STEP)
    delta = -ADAM_LR * (m_hat / (jnp.sqrt(v_hat) + ADAM_EPS) + ADAM_WD * w)
    return delta, m, v


SMALL_W = 512


def _adamw(big, small_w, small_g):
    nb, ns = len(big), len(small_w)

    def body(*refs):
        k = 0
        big_in = [refs[4 * i:4 * i + 4] for i in range(nb)]
        k = 4 * nb
        small_in = [refs[k + 3 * i:k + 3 * i + 3] for i in range(ns)]
        k += 3 * ns
        sg_ref = refs[k]
        k += 1
        big_out = [refs[k + 3 * i:k + 3 * i + 3] for i in range(nb)]
        k += 3 * nb
        small_out = [refs[k + 4 * i:k + 4 * i + 4] for i in range(ns)]

        for (w, g, m, v), (od, om, ov) in zip(big_in, big_out):
            od[...], om[...], ov[...] = _adamw_math(w[...], g[...], m[...], v[...])

        sg = sg_ref[...]
        lbp = small_in[2][0][...]
        lb = _lower_bound(lbp)
        t = sg[4:5, :] * lb * (1.0 - lb)
        grads = [jnp.concatenate([sg[0:1, :], sg[1:2, :]], axis=1),
                 jnp.concatenate([sg[2:3, :], sg[3:4, :]], axis=1),
                 jnp.concatenate([t, -t], axis=0),
                 sg[6:7, :], sg[7:8, 0:Q_RANK], sg[7:8, Q_RANK:Q_RANK + KV_RANK]]
        for (w, m, v), g, (og, od, om, ov) in zip(small_in, grads, small_out):
            og[...] = g
            od[...], om[...], ov[...] = _adamw_math(w[...], g, m[...], v[...])

    ins = [a for grp in big for a in grp] + [a for grp in small_w for a in grp] + [small_g]
    out_shape = ([jax.ShapeDtypeStruct(grp[0].shape, F32) for grp in big for _ in range(3)]
                 + [jax.ShapeDtypeStruct(grp[0].shape, F32) for grp in small_w for _ in range(4)])
    vm = pl.BlockSpec(memory_space=pltpu.VMEM)
    res = pl.pallas_call(
        body, name="adamw", in_specs=[vm] * len(ins), out_specs=[vm] * len(out_shape), out_shape=out_shape,
        compiler_params=pltpu.CompilerParams(vmem_limit_bytes=VMEM_LIMIT),
    )(*ins)
    big_res = [res[3 * i:3 * i + 3] for i in range(nb)]
    small_res = [res[3 * nb + 4 * i:3 * nb + 4 * i + 4] for i in range(ns)]
    return big_res, small_res


def _perm_weights(g_in_t, g_q, g_kv, g_out):
    w_in_t = g_in_t.reshape(D_IN, D_MODEL)
    wq = g_q.transpose(1, 0, 2)
    w_q_p = jnp.pad(wq, ((0, 0), (0, 0), (0, HEAD_LANES - NOPE - ROPE))).reshape(Q_RANK, N_HEADS * HEAD_LANES)
    wkv = g_kv.transpose(1, 0, 2)
    wk = jnp.pad(wkv[:, :, :NOPE], ((0, 0), (0, 0), (0, HEAD_LANES - NOPE))).reshape(KV_RANK, N_HEADS * HEAD_LANES)
    wv = wkv[:, :, NOPE:].reshape(KV_RANK, MLA_WIDTH)
    return w_in_t, w_q_p, jnp.concatenate([wk, wv], axis=1), g_out.reshape(D_MODEL, D_MODEL)


def _grad_slabs(dw_in_t, dw_q_p, dw_kv_p, dw_out):
    s_in = dw_in_t.reshape(N_DEV, D_IN // N_DEV, D_MODEL)
    s_q = dw_q_p.reshape(Q_RANK, N_HEADS, HEAD_LANES)[:, :, :NOPE + ROPE].transpose(1, 0, 2)
    hl = N_HEADS * HEAD_LANES
    dk = dw_kv_p[:, :hl].reshape(KV_RANK, N_HEADS, HEAD_LANES)[:, :, :NOPE]
    dv = dw_kv_p[:, hl:].reshape(KV_RANK, N_HEADS, V_DIM)
    s_kv = jnp.concatenate([dk, dv], axis=2).transpose(1, 0, 2)
    return s_in, s_q, s_kv, dw_out.reshape(N_DEV, D_MODEL // N_DEV, D_MODEL)


def _block_sizes(T):
    return min(256, T), min(256, T), min(512, T)


def kernel(x, positions, ln_g, w_in, q_a_norm_g, w_q_b, kv_a_norm_g, w_kv_b, hg_lower_bounds, hg_norm_g, w_out, final_norm_g, loss_target, m_ln_g, m_w_in, m_q_a_norm_g, m_w_q_b, m_kv_a_norm_g, m_w_kv_b, m_hg_lower_bounds, m_hg_norm_g, m_w_out, m_final_norm_g, v_ln_g, v_w_in, v_q_a_norm_g, v_w_q_b, v_kv_a_norm_g, v_w_kv_b, v_hg_lower_bounds, v_hg_norm_g, v_w_out, v_final_norm_g):
    T = x.shape[1]
    tm, tq, bt = _block_sizes(T)
    nq = T // tq
    xs, tgt = x[0], loss_target[0]
    pos_f = positions.astype(F32)
    fng = final_norm_g.reshape(1, D_MODEL)

    w_in_shard_t = w_in[0].T
    gathered, (c_t, s1_t, s2_t) = _all_gather_weights([w_in_shard_t, w_q_b[0], w_kv_b[0], w_out[0]], pos_f)
    w_in_t, w_q_p, w_kv_p, w_out_b = _perm_weights(*gathered)

    proj, qn, kvn, q, k, v, v_t = _fwd_in(
        xs, ln_g, w_in_t, q_a_norm_g, w_q_p, kv_a_norm_g, w_kv_p, c_t, s1_t, s2_t, bt, tq)
    hl = N_HEADS * HEAD_LANES
    k_t = k.reshape(nq, tq, hl).transpose(0, 2, 1)
    q_t = q.reshape(nq, tq, hl).transpose(0, 2, 1)
    o_mla, o_t, lse = _attn_fwd_flat(k, q_t, v_t, tq)
    o_hg, states = _hgrn_fwd(proj, hg_lower_bounds)
    dx2, do_t, d_gm, d_oh, d_gh, loss_p, d_fng, d_hgn, dw_out = _top(
        xs, tgt, o_mla, o_hg, proj, w_out_b, hg_norm_g, fng, tm, tq)
    dq_t, dk, dv = _attn_bwd_flat(k, v, q_t, k_t, do_t, o_t, lse, tq)
    dq = dq_t.transpose(0, 2, 1).reshape(T, N_HEADS * HEAD_LANES)
    d_hq, d_hf, d_hi, d_lb = _hgrn_bwd(proj, hg_lower_bounds, d_oh, states)
    dx, d_lng, d_qg, d_kvg, dw_q_p, dw_kv_p, dw_in_t = _bot(
        xs, dx2, proj, qn, kvn, dq, dk, dv, d_gm, d_hq, d_hf, d_hi, d_gh, c_t, s1_t, s2_t, w_in_t, w_q_p, w_kv_p,
        ln_g, q_a_norm_g, kv_a_norm_g, tm)

    small = jnp.concatenate([
        d_lng.reshape(2, SMALL_W), d_fng.reshape(2, SMALL_W), d_lb, loss_p, d_hgn,
        jnp.concatenate([d_qg, d_kvg, jnp.zeros((1, SMALL_W - Q_RANK - KV_RANK), F32)], axis=1)], axis=0)
    g_in, g_q, g_kv, g_out, small_sum = _reduce_scatter(list(_grad_slabs(dw_in_t, dw_q_p, dw_kv_p, dw_out)), small)

    big = [(w_in_shard_t, g_in, m_w_in[0].T, v_w_in[0].T), (w_q_b[0], g_q, m_w_q_b[0], v_w_q_b[0]),
           (w_kv_b[0], g_kv, m_w_kv_b[0], v_w_kv_b[0]), (w_out[0], g_out, m_w_out[0], v_w_out[0])]
    small_w = [(ln_g, m_ln_g, v_ln_g),
               (fng, m_final_norm_g.reshape(1, D_MODEL), v_final_norm_g.reshape(1, D_MODEL)),
               (hg_lower_bounds, m_hg_lower_bounds, v_hg_lower_bounds), (hg_norm_g, m_hg_norm_g, v_hg_norm_g),
               (q_a_norm_g, m_q_a_norm_g, v_q_a_norm_g), (kv_a_norm_g, m_kv_a_norm_g, v_kv_a_norm_g)]
    big_res, small_res = _adamw(big, small_w, small_sum)

    loss = small_sum[5, 0]
    (r_in, r_q, r_kv, r_out) = big_res
    (s_ln, s_fn, s_lb, s_hgn, s_qg, s_kvg) = small_res
    flat = lambda t: t.reshape(D_MODEL)
    lead = lambda t: t[None]
    grads = [s_ln[0], lead(g_in.T), s_qg[0], lead(g_q), s_kvg[0], lead(g_kv), s_lb[0], s_hgn[0], lead(g_out), flat(s_fn[0])]

    def pick(i):
        return [s_ln[i + 1], lead(r_in[i].T), s_qg[i + 1], lead(r_q[i]), s_kvg[i + 1], lead(r_kv[i]), s_lb[i + 1],
                s_hgn[i + 1], lead(r_out[i]), flat(s_fn[i + 1])]

    return (loss, dx[None], *grads, *pick(0), *pick(1), *pick(2))
```

```python
import math

import numpy as np
import jax
import jax.numpy as jnp
from jax import lax
from jax.experimental import pallas as pl
from jax.experimental.pallas import tpu as pltpu

F32 = jnp.float32
BF16 = jnp.bfloat16

D_MODEL = 1024
N_HEADS = 8
NOPE = 64
ROPE = 32
HALF_ROPE = ROPE // 2
V_DIM = 64
Q_RANK = 256
KV_RANK = 128
MLA_WIDTH = N_HEADS * V_DIM
HG_HEADS = 4
HG_DIM = 128
HG_WIDTH = HG_HEADS * HG_DIM
CHUNK = 64
SUB = 16
D_IN = 2976
D_PERM = 3072
ROPE_THETA = 10000.0
EPS = 1e-6
N_DEV = 8
LANES = 128
HEAD_LANES = 128

P_GM, P_HQ, P_HF, P_HI, P_GH, P_QL, P_KVL, P_KR = 0, 512, 1024, 1536, 2048, 2560, 2816, 2944
R_QL, R_KVL, R_KR, R_MAIN = (0, 256), (256, 384), (384, 416), (416, 2976)
ROPE_LO = NOPE
SCALE = 1.0 / math.sqrt(NOPE + ROPE)

ADAM_LR = 0.001
ADAM_B1 = 0.9
ADAM_B2 = 0.999
ADAM_EPS = 1e-08
ADAM_WD = 0.01
ADAM_STEP = 10

VMEM_LIMIT = 56 * 1024 * 1024
MESH = pl.DeviceIdType.MESH

NT = (((1,), (1,)), ((), ()))
TN = (((0,), (0,)), ((), ()))


def _params(n_grid=0, **kw):
    sem = ("arbitrary",) * n_grid if n_grid else None
    return pltpu.CompilerParams(dimension_semantics=sem, vmem_limit_bytes=VMEM_LIMIT, **kw)


def _dot(a, b):
    return jnp.dot(a, b, preferred_element_type=F32)


def _dot_nt(a, b):
    return lax.dot_general(a, b, NT, preferred_element_type=F32)


def _dot_tn(a, b):
    return lax.dot_general(a, b, TN, preferred_element_type=F32)


def _sigmoid(x):
    return 1.0 / (1.0 + jnp.exp(-x))


def _rope_fwd(x, c, s1, s2):
    return x * c + pltpu.roll(x, LANES - HALF_ROPE, 1) * s1 + pltpu.roll(x, HALF_ROPE, 1) * s2


def _rope_bwd(dy, c, s1, s2):
    return dy * c - pltpu.roll(dy, LANES - HALF_ROPE, 1) * s1 - pltpu.roll(dy, HALF_ROPE, 1) * s2


def _in_proj_rows(wt_ref):
    kr = wt_ref[R_KR[0]:R_KR[1], :]
    pad = lambda n: jnp.zeros((n, D_MODEL), kr.dtype)
    return ((P_GM, wt_ref[R_MAIN[0]:R_MAIN[1], :]), (P_QL, wt_ref[R_QL[0]:R_QL[1], :]),
            (P_KVL, wt_ref[R_KVL[0]:R_KVL[1], :]),
            (P_KR, jnp.concatenate([pad(ROPE_LO), kr, pad(LANES - ROPE_LO - ROPE)], axis=0)))


def _full(shape):
    n = len(shape)
    return pl.BlockSpec(shape, lambda *_: (0,) * n)


ROPE_BLOCK = 512


def _rope_constants():
    inv = (np.float32(ROPE_THETA) ** (-np.arange(HALF_ROPE, dtype=np.float32) / np.float32(HALF_ROPE))).astype(np.float32)
    place = np.zeros((3, HALF_ROPE, LANES), np.float32)
    for i in range(HALF_ROPE):
        place[0, i, ROPE_LO + i] = place[0, i, ROPE_LO + HALF_ROPE + i] = 1.0
        place[1, i, ROPE_LO + i] = -1.0
        place[2, i, ROPE_LO + HALF_ROPE + i] = 1.0
    base = np.ones((1, LANES), np.float32)
    base[0, ROPE_LO:ROPE_LO + ROPE] = 0.0
    return jnp.asarray(inv.reshape(HALF_ROPE, 1)), jnp.asarray(place), jnp.asarray(base)


def _rope_block(pos, inv, place_ref, base):
    ang = inv * pos
    cos, sin = jnp.cos(ang), jnp.sin(ang)

    def put(v, k):
        return lax.dot_general(v, place_ref[k], TN, precision=lax.Precision.HIGHEST, preferred_element_type=F32)

    return put(cos, 0) + base, put(sin, 1), put(sin, 2)


def _all_gather_weights(shards, pos_f):
    n = len(shards)
    T = pos_f.shape[1]
    rb = min(ROPE_BLOCK, T)

    def body(*refs):
        ins, (pos_ref, inv_ref, place_ref, base_ref) = refs[:n], refs[n:n + 4]
        outs, tables = refs[n + 4:2 * n + 4], refs[2 * n + 4:2 * n + 7]
        send_sems, recv_sems = refs[2 * n + 7], refs[2 * n + 8]
        x, y, c = lax.axis_index("x"), lax.axis_index("y"), lax.axis_index("c")
        me, sibling = (x, y, c), (x, y, 1 - c)
        chips = [(1 - x, y), (x, 1 - y), (1 - x, 1 - y)]

        def idx(d):
            return 4 * d[0] + 2 * d[1] + d[2]

        def copy(a, k, block, to):
            rows = outs[a].at[idx(block)]
            return pltpu.make_async_remote_copy(src_ref=rows, dst_ref=rows, send_sem=send_sems.at[a, k],
                                                recv_sem=recv_sems.at[a, k], device_id=to, device_id_type=MESH)

        for a in range(n):
            outs[a][idx(me)] = ins[a][...].astype(BF16)
        first = []
        for a in range(n):
            first.append(copy(a, 0, me, sibling))
            first += [copy(a, 1 + j, me, (*chip, c)) for j, chip in enumerate(chips)]
        for cp in first:
            cp.start()
        for r0 in range(0, T, rb):
            for ref, tab in zip(tables, _rope_block(pos_ref[:, r0:r0 + rb], inv_ref[...], place_ref, base_ref[...])):
                ref[r0:r0 + rb, :] = tab
        passed = []
        for j, chip in enumerate(chips):
            for a in range(n):
                copy(a, 1 + j, (*chip, c), me).wait_recv()
                cp = copy(a, 4 + j, (*chip, c), sibling)
                cp.start()
                passed.append(cp)
        for a in range(n):
            copy(a, 0, sibling, me).wait_recv()
            for j, chip in enumerate(chips):
                copy(a, 4 + j, (*chip, 1 - c), me).wait_recv()
        for cp in first + passed:
            cp.wait_send()

    vm = pl.BlockSpec(memory_space=pltpu.VMEM)
    res = pl.pallas_call(
        body, name="all_gather_weights",
        in_specs=[vm] * (n + 4), out_specs=[vm] * (n + 3),
        out_shape=[jax.ShapeDtypeStruct((N_DEV,) + s.shape, BF16) for s in shards]
        + [jax.ShapeDtypeStruct((T, LANES), F32)] * 3,
        scratch_shapes=[pltpu.SemaphoreType.DMA((n, 7)), pltpu.SemaphoreType.DMA((n, 7))],
        compiler_params=pltpu.CompilerParams(vmem_limit_bytes=VMEM_LIMIT),
    )(*shards, pos_f, *_rope_constants())
    return res[:n], res[n:]


def _fwd_in(x, ln_g, w_in_t, q_g, w_q_p, kv_g, w_kv_p, c_t, s1_t, s2_t, tm, tq):
    T = x.shape[0]
    assert tm % tq == 0

    def body(x_ref, lng_ref, win_ref, qg_ref, wq_ref, kvg_ref, wkv_ref, c_ref, s1_ref, s2_ref,
             proj_ref, h_ref, qn_ref, kvn_ref, q_ref, k_ref, v_ref, vt_ref):
        xv = x_ref[...]
        r = lax.rsqrt(jnp.mean(xv * xv, axis=-1, keepdims=True) + EPS)
        h = (xv * r * lng_ref[...]).astype(BF16)
        h_ref[...] = h
        for col, rows in _in_proj_rows(win_ref):
            proj_ref[:, col:col + rows.shape[0]] = _dot_nt(h, rows)
        c, s1, s2 = c_ref[...], s1_ref[...], s2_ref[...]

        ql = proj_ref[:, P_QL:P_QL + Q_RANK]
        rq = lax.rsqrt(jnp.mean(ql * ql, axis=-1, keepdims=True) + EPS)
        qn = (ql * rq * qg_ref[...]).astype(BF16)
        qn_ref[...] = qn
        q = _dot(qn, wq_ref[...])
        for hd in range(N_HEADS):
            sl = slice(hd * HEAD_LANES, (hd + 1) * HEAD_LANES)
            q_ref[:, sl] = _rope_fwd(q[:, sl], c, s1, s2).astype(BF16)

        kvl = proj_ref[:, P_KVL:P_KVL + KV_RANK]
        rk = lax.rsqrt(jnp.mean(kvl * kvl, axis=-1, keepdims=True) + EPS)
        kvn = (kvl * rk * kvg_ref[...]).astype(BF16)
        kvn_ref[...] = kvn
        kv = _dot(kvn, wkv_ref[...])
        kpe = _rope_fwd(proj_ref[:, P_KR:P_KR + LANES], c, s1, s2)
        for hd in range(N_HEADS):
            sl = slice(hd * HEAD_LANES, (hd + 1) * HEAD_LANES)
            k_ref[:, sl] = (kv[:, sl] + kpe).astype(BF16)
        v_ref[...] = kv[:, N_HEADS * HEAD_LANES:].astype(BF16)
        for b in range(tm // tq):
            vt_ref[b] = kv[b * tq:(b + 1) * tq, N_HEADS * HEAD_LANES:].T.astype(BF16)

    def row(w):
        return pl.BlockSpec((tm, w), lambda i: (i, 0))

    outs = [(D_PERM, F32), (D_MODEL, BF16), (Q_RANK, BF16), (KV_RANK, BF16),
            (N_HEADS * HEAD_LANES, BF16), (N_HEADS * HEAD_LANES, BF16), (MLA_WIDTH, BF16)]
    return pl.pallas_call(
        body, name="fwd_in", grid=(T // tm,),
        in_specs=[row(D_MODEL), _full((1, D_MODEL)), _full((D_IN, D_MODEL)), _full((1, Q_RANK)),
                  _full((Q_RANK, N_HEADS * HEAD_LANES)), _full((1, KV_RANK)),
                  _full((KV_RANK, N_HEADS * HEAD_LANES + MLA_WIDTH)), row(LANES), row(LANES), row(LANES)],
        out_specs=[row(w) for w, _ in outs] + [pl.BlockSpec((tm // tq, MLA_WIDTH, tq), lambda i: (i, 0, 0))],
        out_shape=[jax.ShapeDtypeStruct((T, w), dt) for w, dt in outs]
        + [jax.ShapeDtypeStruct((T // tq, MLA_WIDTH, tq), BF16)],
        compiler_params=_params(1),
    )(x, ln_g, w_in_t, q_g, w_q_p, kv_g, w_kv_p, c_t, s1_t, s2_t)


LOG2E = 1.4426950408889634
SCALE2 = SCALE * LOG2E


def _causal(tq):
    r = lax.broadcasted_iota(jnp.int32, (tq, tq), 0)
    c = lax.broadcasted_iota(jnp.int32, (tq, tq), 1)
    return r <= c


MASKED = -1e30


def _causal_bias(bias_ref, tq):
    bias_ref[0] = jnp.zeros((tq, tq), F32)
    bias_ref[1] = jnp.where(_causal(tq), 0.0, MASKED)


def _tile_tables(nq, by_query):
    if by_query:
        pairs = [(j, i) for i in range(nq) for j in range(i + 1)]
    else:
        pairs = [(j, i) for j in range(nq) for i in range(nq - 1, j - 1, -1)]
    pairs.append(pairs[-1])
    jj, ii = np.array(pairs, np.int32).T
    return jnp.asarray(jj), jnp.asarray(ii), len(pairs) - 1


ATTN_TRIP = 8


def _walk_tiles(n, products, tile, flush, buf_a, buf_b):
    bufs = (buf_a, buf_b)
    products(0, buf_a)

    def trip(r, carry):
        for u in range(ATTN_TRIP):
            products(ATTN_TRIP * r + u + 1, bufs[(u + 1) % 2])
            tile(ATTN_TRIP * r + u, bufs[u % 2])
        for u in range(ATTN_TRIP):
            flush(ATTN_TRIP * r + u)
        return carry

    lax.fori_loop(0, n // ATTN_TRIP, trip, 0)
    rest = n - n % ATTN_TRIP
    for u in range(n % ATTN_TRIP):
        if rest + u + 1 < n:
            products(rest + u + 1, bufs[(u + 1) % 2])
        tile(rest + u, bufs[u % 2])
    for u in range(n % ATTN_TRIP):
        flush(rest + u)


V_ROWS = V_DIM + 16


def _attn_fwd_flat(k, q_t, v_t, tq):
    T = k.shape[0]
    nq = T // tq
    jj, ii, n = _tile_tables(nq, True)
    heads = [slice(hh * HEAD_LANES, (hh + 1) * HEAD_LANES) for hh in range(2)]

    def body(jj_ref, ii_ref, k_ref, qt_ref, vt_ref, o_ref, ot_ref, lse_ref, sa_ref, sb_ref, m_ref, acc_ref, bias_ref):
        def reset(st):
            m_ref[st] = jnp.full(m_ref.shape[1:], MASKED, F32)
            acc_ref[st] = jnp.zeros(acc_ref.shape[1:], F32)

        _causal_bias(bias_ref, tq)
        for st in range(ATTN_TRIP):
            reset(st)
        extra = (lax.broadcasted_iota(jnp.int32, (V_ROWS - V_DIM, tq), 0) == 0).astype(BF16)

        def products(t, buf):
            j, i = jj_ref[t], ii_ref[t]
            kj = k_ref[pl.ds(pl.multiple_of(j * tq, tq), tq), :]
            for hh, sl in enumerate(heads):
                buf[hh] = _dot(kj[:, sl], qt_ref[i, sl, :])

        def tile(t, buf):
            j, i = jj_ref[t], ii_ref[t]
            vt = vt_ref[j]
            bias = bias_ref.at[(j == i).astype(jnp.int32)]
            st = i % ATTN_TRIP
            for hh in range(2):
                s = buf[hh] * SCALE2 + bias[...]
                m = m_ref[st, hh]
                m_new = jnp.maximum(m, jnp.max(s, axis=0, keepdims=True))
                alpha = jnp.exp2(m - m_new)
                p = jnp.exp2(s - m_new)
                m_ref[st, hh] = m_new
                v_h = jnp.concatenate([vt[hh * V_DIM:(hh + 1) * V_DIM, :], extra], axis=0)
                acc_ref[st, hh] = alpha * acc_ref[st, hh] + _dot(v_h, p.astype(BF16))

        def flush(t):
            j, i = jj_ref[t], ii_ref[t]

            @pl.when(j == i)
            def _():
                st = i % ATTN_TRIP
                den = [acc_ref[st, hh, V_DIM:V_DIM + 1, :] for hh in range(2)]
                out = jnp.concatenate([acc_ref[st, hh, :V_DIM, :] / den[hh] for hh in range(2)], axis=0)
                o_ref[pl.ds(pl.multiple_of(i * tq, tq), tq), :] = out.T
                ot_ref[i] = out
                for hh in range(2):
                    lse_ref[hh, i] = m_ref[st, hh] + jnp.log2(den[hh])
                reset(st)

        _walk_tiles(n, products, tile, flush, sa_ref, sb_ref)

    smem = pl.BlockSpec(memory_space=pltpu.SMEM)
    return pl.pallas_call(
        body, name="attn_fwd", grid=(N_HEADS // 2,),
        in_specs=[smem, smem,
                  pl.BlockSpec((T, 2 * HEAD_LANES), lambda p: (0, p)),
                  pl.BlockSpec((nq, 2 * HEAD_LANES, tq), lambda p: (0, p, 0)),
                  pl.BlockSpec((nq, LANES, tq), lambda p: (0, p, 0))],
        out_specs=[pl.BlockSpec((T, LANES), lambda p: (0, p)), pl.BlockSpec((nq, LANES, tq), lambda p: (0, p, 0)),
                   pl.BlockSpec((2, nq, 1, tq), lambda p: (p, 0, 0, 0))],
        out_shape=[jax.ShapeDtypeStruct((T, MLA_WIDTH), F32), jax.ShapeDtypeStruct((nq, MLA_WIDTH, tq), F32),
                   jax.ShapeDtypeStruct((N_HEADS, nq, 1, tq), F32)],
        scratch_shapes=[pltpu.VMEM((2, tq, tq), F32), pltpu.VMEM((2, tq, tq), F32),
                        pltpu.VMEM((ATTN_TRIP, 2, 1, tq), F32),
                        pltpu.VMEM((ATTN_TRIP, 2, V_ROWS, tq), F32), pltpu.VMEM((2, tq, tq), F32)],
        compiler_params=_params(1),
    )(jj, ii, k, q_t, v_t)


def _attn_bwd_flat(k, v, q_t, k_t, do_t, o_t, lse, tq):
    T = k.shape[0]
    nq = T // tq
    jj, ii, n = _tile_tables(nq, False)
    heads = [slice(hh * HEAD_LANES, (hh + 1) * HEAD_LANES) for hh in range(2)]

    def body(jj_ref, ii_ref, k_ref, v_ref, qt_ref, kt_ref, dot_ref, ot_ref, lse_ref, dqt_ref, dk_ref, dv_ref,
             ba_ref, bb_ref, dkt_ref, dvt_ref, bias_ref, dsum_ref):
        _causal_bias(bias_ref, tq)

        def row_dots(i, carry):
            for hh in range(2):
                own = slice(hh * V_DIM, (hh + 1) * V_DIM)
                do_h = dot_ref[i, hh * HEAD_LANES + own.start:hh * HEAD_LANES + own.stop, :]
                dsum_ref[hh, i] = jnp.sum(do_h.astype(F32) * ot_ref[i, own, :], axis=0, keepdims=True)
            return carry

        lax.fori_loop(0, nq, row_dots, 0)
        dqt_ref[...] = jnp.zeros_like(dqt_ref)
        dkt_ref[...] = jnp.zeros_like(dkt_ref)
        dvt_ref[...] = jnp.zeros_like(dvt_ref)

        def products(t, buf):
            j, i = jj_ref[t], ii_ref[t]
            rows = pl.ds(pl.multiple_of(j * tq, tq), tq)
            for hh, sl in enumerate(heads):
                buf[hh] = _dot(k_ref[rows, sl], qt_ref[i, sl, :])
                buf[2 + hh] = _dot(v_ref[rows, :], dot_ref[i, sl, :])

        def tile(t, buf):
            j, i = jj_ref[t], ii_ref[t]
            bias = bias_ref.at[(j == i).astype(jnp.int32)]
            st = j % ATTN_TRIP
            for hh, sl in enumerate(heads):
                p = jnp.exp2(buf[hh] * SCALE2 + bias[...] - lse_ref[hh, i])
                ds = (p * (buf[2 + hh] - dsum_ref[hh, i]) * SCALE).astype(BF16)
                own = slice(hh * V_DIM, (hh + 1) * V_DIM)
                do_h = dot_ref[i, hh * HEAD_LANES + own.start:hh * HEAD_LANES + own.stop, :]
                dvt_ref[st, own, :] += _dot_nt(do_h, p.astype(BF16))
                used = slice(sl.start, sl.start + NOPE + ROPE)
                dkt_ref[st, used, :] += _dot_nt(qt_ref[i, used, :], ds)
                dqt_ref[i, used, :] += _dot(kt_ref[j, used, :], ds)

        def flush(t):
            j, i = jj_ref[t], ii_ref[t]

            @pl.when(j == i)
            def _():
                st = j % ATTN_TRIP
                rows = pl.ds(pl.multiple_of(j * tq, tq), tq)
                dk_ref[rows, :] = dkt_ref[st].T
                dv_ref[rows, :] = dvt_ref[st].T.astype(BF16)
                dkt_ref[st] = jnp.zeros(dkt_ref.shape[1:], F32)
                dvt_ref[st] = jnp.zeros(dvt_ref.shape[1:], F32)

        _walk_tiles(n, products, tile, flush, ba_ref, bb_ref)

    smem = pl.BlockSpec(memory_space=pltpu.SMEM)
    stat = pl.BlockSpec((2, nq, 1, tq), lambda p: (p, 0, 0, 0))
    blocks_t = pl.BlockSpec((nq, 2 * HEAD_LANES, tq), lambda p: (0, p, 0))
    return pl.pallas_call(
        body, name="attn_bwd", grid=(N_HEADS // 2,),
        in_specs=[smem, smem,
                  pl.BlockSpec((T, 2 * HEAD_LANES), lambda p: (0, p)),
                  pl.BlockSpec((T, LANES), lambda p: (0, p)),
                  blocks_t, blocks_t, blocks_t, pl.BlockSpec((nq, LANES, tq), lambda p: (0, p, 0)), stat],
        out_specs=[blocks_t,
                   pl.BlockSpec((T, 2 * HEAD_LANES), lambda p: (0, p)),
                   pl.BlockSpec((T, LANES), lambda p: (0, p))],
        out_shape=[jax.ShapeDtypeStruct((nq, N_HEADS * HEAD_LANES, tq), F32),
                   jax.ShapeDtypeStruct((T, N_HEADS * HEAD_LANES), F32),
                   jax.ShapeDtypeStruct((T, MLA_WIDTH), BF16)],
        scratch_shapes=[pltpu.VMEM((4, tq, tq), F32), pltpu.VMEM((4, tq, tq), F32),
                        pltpu.VMEM((ATTN_TRIP, 2 * HEAD_LANES, tq), F32), pltpu.VMEM((ATTN_TRIP, LANES, tq), F32),
                        pltpu.VMEM((2, tq, tq), F32), pltpu.VMEM((2, nq, 1, tq), F32)],
        compiler_params=_params(1),
    )(jj, ii, k, v, q_t, k_t, do_t, o_t, lse)


def _lower_bound(lbp):
    a, b = lbp[0:1, :], lbp[1:2, :]
    mx = jnp.maximum(a, b)
    ea, eb = jnp.exp(a - mx), jnp.exp(b - mx)
    return ea / (ea + eb)


def _tri(lower):
    r = lax.broadcasted_iota(jnp.int32, (CHUNK, CHUNK), 0)
    c = lax.broadcasted_iota(jnp.int32, (CHUNK, CHUNK), 1)
    return (c <= r) if lower else (c >= r)


def _running_sum(x, from_end):
    row = lax.broadcasted_iota(jnp.int32, x.shape, 0)
    step = 1
    while step < CHUNK:
        if from_end:
            x = x + jnp.where(row < CHUNK - step, pltpu.roll(x, CHUNK - step, 0), 0.0)
        else:
            x = x + jnp.where(row >= step, pltpu.roll(x, step, 0), 0.0)
        step *= 2
    return x


def _hg_gates(hq, hf, lb):
    sq = _sigmoid(hq)
    sf = _sigmoid(hf)
    f = lb + (1.0 - lb) * sf
    g = jnp.log(f)
    gcum = _running_sum(g, False)
    return sq, sf, f, hq * sq, 1.0 - f, gcum


def _head(x, hd):
    return x[:, hd * HG_DIM:(hd + 1) * HG_DIM]


def _all_heads(fn):
    return jnp.concatenate([fn(hd) for hd in range(HG_HEADS)], axis=1)


def _hg_blocks(q, kk, gcum):
    rowi = lax.broadcasted_iota(jnp.int32, gcum.shape, 0)
    out = []
    for blk in range(CHUNK // SUB):
        lo, hi = blk * SUB, (blk + 1) * SUB
        gb = gcum[lo - 1:lo, :] if blk else jnp.zeros_like(gcum[0:1, :])
        eq = jnp.exp(gcum[lo:hi, :] - gb)
        ek = jnp.exp(jnp.where(rowi < hi, gb - gcum, 0.0))
        out.append((eq, ek, (q[lo:hi, :] * eq).astype(BF16), (kk * ek).astype(BF16)))
    return out


def _hg_scores(blocks):
    out = []
    for hd in range(HG_HEADS):
        a = jnp.concatenate([_dot_nt(_head(qb, hd), _head(kb, hd)) for _, _, qb, kb in blocks], axis=0)
        out.append(jnp.where(_tri(True), a, 0.0))
    return out


HG_STEP_CHUNKS = 8


def _hgrn_fwd(proj, lbp):
    T = proj.shape[0]
    nc = T // CHUNK
    ns = min(HG_STEP_CHUNKS, nc)
    rows = ns * CHUNK

    def body(hq_ref, hf_ref, hi_ref, lbp_ref, o_ref, st_ref, state):
        @pl.when(pl.program_id(0) == 0)
        def _():
            state[...] = jnp.zeros_like(state)

        lb = _lower_bound(lbp_ref[...])
        work = []
        for c in range(ns):
            r = slice(c * CHUNK, (c + 1) * CHUNK)
            _, _, _, q, kk, gcum = _hg_gates(hq_ref[r, :], hf_ref[r, :], lb)
            vb = hi_ref[r, :].astype(BF16)
            a = _hg_scores(_hg_blocks(q, kk, gcum))
            gend = gcum[CHUNK - 1:CHUNK, :]
            qgb = (q * jnp.exp(gcum)).astype(BF16)
            kgeb = (kk * jnp.exp(gend - gcum)).astype(BF16)
            intra = [_dot(a[hd].astype(BF16), _head(vb, hd)) for hd in range(HG_HEADS)]
            update = [_dot_tn(_head(vb, hd), _head(kgeb, hd)) for hd in range(HG_HEADS)]
            work.append((qgb, jnp.exp(gend), intra, update))
        for hd in range(HG_HEADS):
            st = state[hd]
            for c, (qgb, egend, intra, update) in enumerate(work):
                st_ref[c, hd] = st
                o_ref[c * CHUNK:(c + 1) * CHUNK, hd * HG_DIM:(hd + 1) * HG_DIM] = (
                    intra[hd] + _dot_nt(_head(qgb, hd), st.astype(BF16)))
                st = st * _head(egend, hd) + update[hd]
            state[hd] = st

    def col(cb):
        return pl.BlockSpec((rows, HG_WIDTH), lambda i: (i, cb))

    return pl.pallas_call(
        body, name="hgrn_fwd", grid=(nc // ns,),
        in_specs=[col(P_HQ // HG_WIDTH), col(P_HF // HG_WIDTH), col(P_HI // HG_WIDTH), _full((2, HG_WIDTH))],
        out_specs=[pl.BlockSpec((rows, HG_WIDTH), lambda i: (i, 0)),
                   pl.BlockSpec((ns, HG_HEADS, HG_DIM, HG_DIM), lambda i: (i, 0, 0, 0))],
        out_shape=[jax.ShapeDtypeStruct((T, HG_WIDTH), F32),
                   jax.ShapeDtypeStruct((nc, HG_HEADS, HG_DIM, HG_DIM), F32)],
        scratch_shapes=[pltpu.VMEM((HG_HEADS, HG_DIM, HG_DIM), F32)],
        compiler_params=_params(1),
    )(proj, proj, proj, lbp)


def _hgrn_bwd(proj, lbp, do_hg, states):
    T = proj.shape[0]
    nc = T // CHUNK
    ns = min(HG_STEP_CHUNKS, nc)
    rows = ns * CHUNK
    steps = nc // ns

    def body(hq_ref, hf_ref, hi_ref, lbp_ref, do_ref, st_ref, dhq_ref, dhf_ref, dhi_ref, dlb_ref, dstate):
        @pl.when(pl.program_id(0) == 0)
        def _():
            dstate[...] = jnp.zeros_like(dstate)
            dlb_ref[...] = jnp.zeros_like(dlb_ref)

        lb = _lower_bound(lbp_ref[...])

        dst_all = [dstate[hd] for hd in range(HG_HEADS)]
        dlb = jnp.zeros_like(lb)
        last = lax.broadcasted_iota(jnp.int32, (CHUNK, HG_WIDTH), 0) == CHUNK - 1
        for c in reversed(range(ns)):
            r = slice(c * CHUNK, (c + 1) * CHUNK)
            hq = hq_ref[r, :]
            sq, sf, f, q, kk, gcum = _hg_gates(hq, hf_ref[r, :], lb)
            vb = hi_ref[r, :].astype(BF16)
            dob = do_ref[r, :].astype(BF16)
            blocks = _hg_blocks(q, kk, gcum)
            a = _hg_scores(blocks)
            gend = gcum[CHUNK - 1:CHUNK, :]
            eg, egend, ekend = jnp.exp(gcum), jnp.exp(gend), jnp.exp(gend - gcum)
            qg, kge = q * eg, kk * ekend
            qgb, kgeb = qg.astype(BF16), kge.astype(BF16)

            dv, dqg, dkge, st_dst, dq_blk, dk_blk = [], [], [], [], [], []
            for hd in range(HG_HEADS):
                st = st_ref[c, hd]
                dst = dst_all[hd]
                dstb = dst.astype(BF16)
                do_h, v_h = _head(dob, hd), _head(vb, hd)
                dv.append(_dot_tn(a[hd].astype(BF16), do_h) + _dot_nt(_head(kgeb, hd), dstb))
                da = jnp.where(_tri(True), _dot_nt(do_h, v_h), 0.0).astype(BF16)
                dqg.append(_dot(do_h, st.astype(BF16)))
                dkge.append(_dot(v_h, dstb))
                st_dst.append(jnp.sum(st * dst, axis=0, keepdims=True))
                dst_all[hd] = _dot_tn(do_h, _head(qgb, hd)) + dst * _head(egend, hd)
                dq_blk.append([_dot(da[b * SUB:(b + 1) * SUB, :], _head(kb, hd)) for b, (_, _, _, kb) in enumerate(blocks)])
                dk_blk.append([_dot_tn(da[b * SUB:(b + 1) * SUB, :], _head(qb, hd)) for b, (_, _, qb, _) in enumerate(blocks)])
            dv, dqg, dkge, st_dst = (jnp.concatenate(t, axis=1) for t in (dv, dqg, dkge, st_dst))

            dq_a, dg_q = [], []
            dk_a, dg_k = jnp.zeros_like(gcum), jnp.zeros_like(gcum)
            for b, (eq, ek, qb, kb) in enumerate(blocks):
                dq_b = _all_heads(lambda hd: dq_blk[hd][b])
                dk_b = _all_heads(lambda hd: dk_blk[hd][b])
                dq_a.append(dq_b * eq)
                dk_a = dk_a + dk_b * ek
                dg_q.append(qb.astype(F32) * dq_b)
                dg_k = dg_k + kb.astype(F32) * dk_b
            dq_a = jnp.concatenate(dq_a, axis=0)

            dgend = st_dst * egend + jnp.sum(dkge * kge, axis=0, keepdims=True)
            dq = dq_a + dqg * eg
            dk = dk_a + dkge * ekend
            dgc = jnp.concatenate(dg_q, axis=0) - dg_k + dqg * qg - dkge * kge + jnp.where(last, dgend, 0.0)
            dg = _running_sum(dgc, True)
            df = dg / f - dk
            dhf_ref[r, :] = (df * (1.0 - lb) * sf * (1.0 - sf)).astype(BF16)
            dlb = dlb + jnp.sum(df * (1.0 - sf), axis=0, keepdims=True)
            dhq_ref[r, :] = (dq * (sq * (1.0 + hq * (1.0 - sq)))).astype(BF16)
            dhi_ref[r, :] = dv.astype(BF16)
        for hd in range(HG_HEADS):
            dstate[hd] = dst_all[hd]
        dlb_ref[...] += dlb

    def col(cb):
        return pl.BlockSpec((rows, HG_WIDTH), lambda i: (steps - 1 - i, cb))

    grad = jax.ShapeDtypeStruct((T, HG_WIDTH), BF16)
    return pl.pallas_call(
        body, name="hgrn_bwd", grid=(steps,),
        in_specs=[col(P_HQ // HG_WIDTH), col(P_HF // HG_WIDTH), col(P_HI // HG_WIDTH), _full((2, HG_WIDTH)),
                  col(0), pl.BlockSpec((ns, HG_HEADS, HG_DIM, HG_DIM), lambda i: (steps - 1 - i, 0, 0, 0))],
        out_specs=[col(0), col(0), col(0), _full((1, HG_WIDTH))],
        out_shape=[grad, grad, grad, jax.ShapeDtypeStruct((1, HG_WIDTH), F32)],
        scratch_shapes=[pltpu.VMEM((HG_HEADS, HG_DIM, HG_DIM), F32)],
        compiler_params=_params(1),
    )(proj, proj, proj, lbp, do_hg, states)


def _top(x, tgt, o_mla, o_hg, proj, w_out, hg_norm_g, final_g, tm, tq):
    T = x.shape[0]
    assert tm % tq == 0

    def body(x_ref, tgt_ref, om_ref, oh_ref, gm_ref, gh_ref, wout_ref, hgn_ref, fng_ref,
             dx2_ref, dot_ref, dgm_ref, doh_ref, dgh_ref, loss_ref, dfng_ref, dhgn_ref, dwout_ref, ycat_ref):
        @pl.when(pl.program_id(0) == 0)
        def _():
            for ref in (loss_ref, dfng_ref, dhgn_ref, dwout_ref):
                ref[...] = jnp.zeros_like(ref)

        gm, om = gm_ref[...], om_ref[...]
        sgm = _sigmoid(gm)
        silu_m = gm * sgm
        gh, oh, gam = gh_ref[...], oh_ref[...], hgn_ref[...]
        sgh = _sigmoid(gh)
        silu_h = gh * sgh
        rr, nn = [], []
        for hd in range(HG_HEADS):
            oh_h = oh[:, hd * HG_DIM:(hd + 1) * HG_DIM]
            r_h = lax.rsqrt(jnp.mean(oh_h * oh_h, axis=-1, keepdims=True) + EPS)
            rr.append(r_h)
            nn.append(oh_h * r_h)
        n = jnp.concatenate(nn, axis=1)
        ng = n * gam
        ycat_ref[:, :MLA_WIDTH] = (om * silu_m).astype(BF16)
        ycat_ref[:, MLA_WIDTH:] = (ng * silu_h).astype(BF16)
        wout = wout_ref[...]
        x2 = x_ref[...] + _dot(ycat_ref[...], wout)
        r = lax.rsqrt(jnp.mean(x2 * x2, axis=-1, keepdims=True) + EPS)
        xh = x2 * r
        fng = fng_ref[...]
        err = xh * fng - tgt_ref[...]
        loss_ref[...] += 0.5 * jnp.sum(jnp.mean(err * err, axis=-1, keepdims=True), axis=0, keepdims=True)
        dout = err * (1.0 / D_MODEL)
        dfng_ref[...] += jnp.sum(dout * xh, axis=0, keepdims=True)
        dxh = dout * fng
        dx2 = r * (dxh - xh * jnp.mean(dxh * xh, axis=-1, keepdims=True))
        dx2_ref[...] = dx2
        dx2b = dx2.astype(BF16)
        dwout_ref[...] += _dot_tn(ycat_ref[...], dx2b)
        dycat = _dot_nt(dx2b, wout)
        dym, dyh = dycat[:, :MLA_WIDTH], dycat[:, MLA_WIDTH:]
        dom = dym * silu_m
        first = lax.broadcasted_iota(jnp.int32, (tm, LANES), 1) < V_DIM
        for hd in range(N_HEADS):
            pair = dom[:, hd // 2 * LANES:(hd // 2 + 1) * LANES]
            own = jnp.where(first, pair, 0.0) if hd % 2 == 0 else jnp.where(first, 0.0, pair)
            for b in range(tm // tq):
                dot_ref[b, hd * HEAD_LANES:(hd + 1) * HEAD_LANES, :] = own[b * tq:(b + 1) * tq, :].T.astype(BF16)
        dgm_ref[...] = (dym * om * (sgm * (1.0 + gm * (1.0 - sgm)))).astype(BF16)
        dgh_ref[...] = (dyh * ng * (sgh * (1.0 + gh * (1.0 - sgh)))).astype(BF16)
        dng = dyh * silu_h
        dhgn_ref[...] += jnp.sum(dng * n, axis=0, keepdims=True)
        dn = dng * gam
        for hd in range(HG_HEADS):
            sl = slice(hd * HG_DIM, (hd + 1) * HG_DIM)
            dn_h, n_h = dn[:, sl], nn[hd]
            doh_ref[:, sl] = rr[hd] * (dn_h - n_h * jnp.mean(dn_h * n_h, axis=-1, keepdims=True))

    def row(w, cb=0):
        return pl.BlockSpec((tm, w), lambda i: (i, cb))

    hl = N_HEADS * HEAD_LANES
    blocks_t = pl.BlockSpec((tm // tq, hl, tq), lambda i: (i, 0, 0))
    outs = [(D_MODEL, F32), (0, BF16), (MLA_WIDTH, BF16), (HG_WIDTH, F32), (HG_WIDTH, BF16)]
    small = [(1, SMALL_W), (1, D_MODEL), (1, HG_WIDTH), (D_MODEL, D_MODEL)]
    return pl.pallas_call(
        body, name="top", grid=(T // tm,),
        in_specs=[row(D_MODEL), row(D_MODEL), row(MLA_WIDTH), row(HG_WIDTH),
                  row(MLA_WIDTH, P_GM // MLA_WIDTH), row(HG_WIDTH, P_GH // HG_WIDTH),
                  _full((D_MODEL, D_MODEL)), _full((1, HG_WIDTH)), _full((1, D_MODEL))],
        out_specs=[row(w) if w else blocks_t for w, _ in outs] + [_full(s) for s in small],
        out_shape=[jax.ShapeDtypeStruct((T, w) if w else (T // tq, hl, tq), dt) for w, dt in outs]
        + [jax.ShapeDtypeStruct(s, F32) for s in small],
        scratch_shapes=[pltpu.VMEM((tm, D_MODEL), BF16)],
        compiler_params=_params(1),
    )(x, tgt, o_mla, o_hg, proj, proj, w_out, hg_norm_g, final_g)


def _bot(x, dx2, proj, h, qn, kvn, dq, dk, dv, dgm, dhq, dhf, dhi, dgh, c_t, s1_t, s2_t, w_in_t, w_q_p, w_kv_p, ln_g, q_g, kv_g, tm):
    T = x.shape[0]
    lat_w = D_PERM - P_QL
    steps = T // tm

    def body(x_ref, dx2_ref, lat_ref, h_ref, qn_ref, kvn_ref, dq_ref, dk_ref, dv_ref, dgm_ref, dhq_ref, dhf_ref,
             dhi_ref, dgh_ref, c_ref, s1_ref, s2_ref, win_ref, wq_ref, wkv_ref, lng_ref, qg_ref, kvg_ref,
             dx_ref, dlng_ref, dqg_ref, dkvg_ref, dwq_ref, dwkv_ref, dwin_ref, dqpre_ref, dkv_ref, dproj_ref,
             dwin_acc, sem):
        @pl.when(pl.program_id(0) == 0)
        def _():
            for ref in (dlng_ref, dqg_ref, dkvg_ref, dwq_ref, dwkv_ref, dwin_acc):
                ref[...] = jnp.zeros_like(ref)

        c, s1, s2 = c_ref[...], s1_ref[...], s2_ref[...]
        dkpe = jnp.zeros((tm, LANES), F32)
        for hd in range(N_HEADS):
            sl = slice(hd * HEAD_LANES, (hd + 1) * HEAD_LANES)
            dqpre_ref[:, sl] = _rope_bwd(dq_ref[:, sl], c, s1, s2).astype(BF16)
            dk_h = dk_ref[:, sl]
            dkpe = dkpe + dk_h
            dkv_ref[:, sl] = dk_h.astype(BF16)
        dkv_ref[:, N_HEADS * HEAD_LANES:] = dv_ref[...].astype(BF16)
        lane = lax.broadcasted_iota(jnp.int32, (tm, LANES), 1)
        rope_lanes = jnp.logical_and(lane >= ROPE_LO, lane < ROPE_LO + ROPE)
        dkr = jnp.where(rope_lanes, _rope_bwd(dkpe, c, s1, s2), 0.0)

        def norm_bwd(v, g, dy):
            r = lax.rsqrt(jnp.mean(v * v, axis=-1, keepdims=True) + EPS)
            vh = v * r
            dvh = dy * g
            return jnp.sum(dy * vh, axis=0, keepdims=True), r * (dvh - vh * jnp.mean(dvh * vh, axis=-1, keepdims=True))

        dwq_ref[...] += _dot_tn(qn_ref[...], dqpre_ref[...])
        dwkv_ref[...] += _dot_tn(kvn_ref[...], dkv_ref[...])
        dqn = _dot_nt(dqpre_ref[...], wq_ref[...])
        dg_q, dql = norm_bwd(lat_ref[:, :Q_RANK], qg_ref[...], dqn)
        dqg_ref[...] += dg_q
        dkn = _dot_nt(dkv_ref[...], wkv_ref[...])
        dg_kv, dkvl = norm_bwd(lat_ref[:, Q_RANK:Q_RANK + KV_RANK], kvg_ref[...], dkn)
        dkvg_ref[...] += dg_kv

        dproj_ref[:, P_GM:P_GM + MLA_WIDTH] = dgm_ref[...].astype(BF16)
        dproj_ref[:, P_HQ:P_HQ + HG_WIDTH] = dhq_ref[...].astype(BF16)
        dproj_ref[:, P_HF:P_HF + HG_WIDTH] = dhf_ref[...].astype(BF16)
        dproj_ref[:, P_HI:P_HI + HG_WIDTH] = dhi_ref[...].astype(BF16)
        dproj_ref[:, P_GH:P_GH + HG_WIDTH] = dgh_ref[...].astype(BF16)
        dproj_ref[:, P_QL:P_QL + Q_RANK] = dql.astype(BF16)
        dproj_ref[:, P_KVL:P_KVL + KV_RANK] = dkvl.astype(BF16)
        dproj_ref[:, P_KR:P_KR + LANES] = dkr.astype(BF16)
        dh = sum(_dot(dproj_ref[:, col:col + rows.shape[0]], rows) for col, rows in _in_proj_rows(win_ref))
        dg_ln, dxn = norm_bwd(x_ref[...], lng_ref[...], dh)
        dlng_ref[...] += dg_ln
        dx_ref[...] = dx2_ref[...] + dxn
        dwin_acc[...] += _dot_tn(dproj_ref[...], h_ref[...])

        @pl.when(pl.program_id(0) == steps - 1)
        def _():
            kr = P_KR + ROPE_LO
            moves = [((P_GM, P_QL), R_MAIN), ((P_QL, P_KR), (R_QL[0], R_KVL[1])), ((kr, kr + ROPE), R_KR)]
            copies = [pltpu.make_async_copy(dwin_acc.at[a:b, :], dwin_ref.at[c:d, :], sem.at[n])
                      for n, ((a, b), (c, d)) in enumerate(moves)]
            for cp in copies:
                cp.start()
            for cp in copies:
                cp.wait()

    def row(w, cb=0):
        return pl.BlockSpec((tm, w), lambda i: (i, cb))

    hl = N_HEADS * HEAD_LANES
    outs = [(D_MODEL, F32)]
    small = [(1, D_MODEL), (1, Q_RANK), (1, KV_RANK), (Q_RANK, hl), (KV_RANK, hl + MLA_WIDTH)]
    return pl.pallas_call(
        body, name="bot", grid=(steps,),
        in_specs=[row(D_MODEL), row(D_MODEL), row(lat_w, P_QL // lat_w), row(D_MODEL), row(Q_RANK), row(KV_RANK),
                  row(hl), row(hl), row(MLA_WIDTH),
                  row(MLA_WIDTH), row(HG_WIDTH), row(HG_WIDTH), row(HG_WIDTH), row(HG_WIDTH),
                  row(LANES), row(LANES), row(LANES),
                  _full((D_IN, D_MODEL)), _full((Q_RANK, hl)), _full((KV_RANK, hl + MLA_WIDTH)),
                  _full((1, D_MODEL)), _full((1, Q_RANK)), _full((1, KV_RANK))],
        out_specs=[row(w) for w, _ in outs] + [_full(s) for s in small] + [pl.BlockSpec(memory_space=pl.ANY)],
        out_shape=[jax.ShapeDtypeStruct((T, w), dt) for w, dt in outs] + [jax.ShapeDtypeStruct(s, F32) for s in small]
        + [jax.ShapeDtypeStruct((D_IN, D_MODEL), F32)],
        scratch_shapes=[pltpu.VMEM((tm, hl), BF16), pltpu.VMEM((tm, hl + MLA_WIDTH), BF16),
                        pltpu.VMEM((tm, D_PERM), BF16), pltpu.VMEM((D_PERM, D_MODEL), F32),
                        pltpu.SemaphoreType.DMA((3,))],
        compiler_params=_params(1),
    )(x, dx2, proj, h, qn, kvn, dq, dk, dv, dgm, dhq, dhf, dhi, dgh, c_t, s1_t, s2_t, w_in_t, w_q_p, w_kv_p, ln_g,
      q_g, kv_g)


RS_ROWS = 256


def _reduce_scatter(slabs, small):
    n = len(slabs)
    units = []
    for a, s in enumerate(slabs):
        rows, cols = s.shape[1:]
        if rows % RS_ROWS == 0 or rows < RS_ROWS:
            units += [(a, (pl.ds(r0, min(rows, RS_ROWS)), slice(None))) for r0 in range(0, rows, RS_ROWS)]
        else:
            units += [(a, (slice(None), pl.ds(c0, RS_ROWS))) for c0 in range(0, cols, RS_ROWS)]
    nu = len(units)

    def body(*refs):
        ins, small_ref = refs[:n], refs[n]
        outs, small_out = refs[n + 1:2 * n + 1], refs[2 * n + 1]
        own, sib_land, ici_out, ici_land = (refs[(2 + g) * n + 2:(3 + g) * n + 2] for g in range(4))
        small_land = refs[6 * n + 2]
        loc_sems, d2d_send, d2d_recv, ici_send, ici_recv, sm_send, sm_recv = refs[6 * n + 3:6 * n + 10]
        x, y, c = lax.axis_index("x"), lax.axis_index("y"), lax.axis_index("c")
        me = 4 * x + 2 * y + c

        def chip(k):
            return (1 - x if k & 2 else x, 1 - y if k & 1 else y)

        def block(k, core):
            px, py = chip(k)
            return 4 * px + 2 * py + core

        def part(u):
            return units[u]

        def local(u, k):
            a, rows = part(u)
            return pltpu.make_async_copy(ins[a].at[(block(k, c),) + rows], own[a].at[(k,) + rows], loc_sems.at[u, k])

        def to_sibling(u, k):
            a, rows = part(u)
            return pltpu.make_async_remote_copy(
                src_ref=ins[a].at[(block(k, 1 - c),) + rows], dst_ref=sib_land[a].at[(k,) + rows],
                send_sem=d2d_send.at[u, k], recv_sem=d2d_recv.at[u, k], device_id=(x, y, 1 - c), device_id_type=MESH)

        def to_chip(u, k):
            a, rows = part(u)
            return pltpu.make_async_remote_copy(
                src_ref=ici_out[a].at[(k - 1,) + rows], dst_ref=ici_land[a].at[(k - 1,) + rows],
                send_sem=ici_send.at[u, k - 1], recv_sem=ici_recv.at[u, k - 1], device_id=(*chip(k), c),
                device_id_type=MESH)

        def small_copy(k, receiving):
            px, py = chip(k >> 1)
            pc = 1 - c if k & 1 else c
            slot = 4 * px + 2 * py + pc if receiving else me
            return pltpu.make_async_remote_copy(
                src_ref=small_ref, dst_ref=small_land.at[slot], send_sem=sm_send.at[k - 1], recv_sem=sm_recv.at[k - 1],
                device_id=(px, py, pc), device_id_type=MESH)

        for u in range(nu):
            for k in range(4):
                local(u, k).start()
        for u in range(nu):
            for k in range(4):
                to_sibling(u, k).start()
        small_land[me] = small_ref[...]
        for k in range(1, N_DEV):
            small_copy(k, False).start()
        for u in range(nu):
            a, rows = part(u)
            for k in range(4):
                local(u, k).wait()
                to_sibling(u, k).wait_recv()
            for k in range(1, 4):
                ici_out[a][(k - 1,) + rows] = (own[a][(k,) + rows] + sib_land[a][(k,) + rows]).astype(BF16)
                to_chip(u, k).start()
        for u in range(nu):
            a, rows = part(u)
            acc = own[a][(0,) + rows] + sib_land[a][(0,) + rows]
            for k in range(1, 4):
                to_chip(u, k).wait_recv()
                acc = acc + ici_land[a][(k - 1,) + rows].astype(F32)
            outs[a][rows] = acc
        for k in range(1, N_DEV):
            small_copy(k, True).wait_recv()
        acc = small_land[0]
        for d in range(1, N_DEV):
            acc = acc + small_land[d]
        small_out[...] = acc
        for u in range(nu):
            for k in range(4):
                to_sibling(u, k).wait_send()
            for k in range(1, 4):
                to_chip(u, k).wait_send()
        for k in range(1, N_DEV):
            small_copy(k, False).wait_send()

    vm = pl.BlockSpec(memory_space=pltpu.VMEM)
    hbm = pl.BlockSpec(memory_space=pl.ANY)
    dma = pltpu.SemaphoreType.DMA
    return pl.pallas_call(
        body, name="reduce_scatter_grads",
        in_specs=[hbm] * n + [vm], out_specs=[vm] * (n + 1),
        out_shape=[jax.ShapeDtypeStruct(s.shape[1:], F32) for s in slabs] + [jax.ShapeDtypeStruct(small.shape, F32)],
        scratch_shapes=[pltpu.VMEM((4,) + s.shape[1:], F32) for s in slabs] * 2
        + [pltpu.VMEM((3,) + s.shape[1:], BF16) for s in slabs] * 2
        + [pltpu.VMEM((N_DEV,) + small.shape, F32)]
        + [dma((nu, 4)), dma((nu, 4)), dma((nu, 4)), dma((nu, 3)), dma((nu, 3)), dma((N_DEV - 1,)), dma((N_DEV - 1,))],
        compiler_params=pltpu.CompilerParams(vmem_limit_bytes=VMEM_LIMIT),
    )(*slabs, small)


def _adamw_math(w, g, m, v):
    m = ADAM_B1 * m + (1.0 - ADAM_B1) * g
    v = ADAM_B2 * v + (1.0 - ADAM_B2) * (g * g)
    m_hat = m / (1.0 - ADAM_B1 ** ADAM_STEP)
    v_hat = v / (1.0 - ADAM_B2 ** ADAM_STEP)
    delta = -ADAM_LR * (m_hat / (jnp.sqrt(v_hat) + ADAM_EPS) + ADAM_WD * w)
    return delta, m, v


SMALL_W = 512


def _adamw(big, small_w, small_g):
    nb, ns = len(big), len(small_w)

    def body(*refs):
        k = 0
        big_in = [refs[4 * i:4 * i + 4] for i in range(nb)]
        k = 4 * nb
        small_in = [refs[k + 3 * i:k + 3 * i + 3] for i in range(ns)]
        k += 3 * ns
        sg_ref = refs[k]
        k += 1
        big_out = [refs[k + 3 * i:k + 3 * i + 3] for i in range(nb)]
        k += 3 * nb
        small_out = [refs[k + 4 * i:k + 4 * i + 4] for i in range(ns)]

        for (w, g, m, v), (od, om, ov) in zip(big_in, big_out):
            od[...], om[...], ov[...] = _adamw_math(w[...], g[...], m[...], v[...])

        sg = sg_ref[...]
        lbp = small_in[2][0][...]
        lb = _lower_bound(lbp)
        t = sg[4:5, :] * lb * (1.0 - lb)
        grads = [jnp.concatenate([sg[0:1, :], sg[1:2, :]], axis=1),
                 jnp.concatenate([sg[2:3, :], sg[3:4, :]], axis=1),
                 jnp.concatenate([t, -t], axis=0),
                 sg[6:7, :], sg[7:8, 0:Q_RANK], sg[7:8, Q_RANK:Q_RANK + KV_RANK]]
        for (w, m, v), g, (og, od, om, ov) in zip(small_in, grads, small_out):
            og[...] = g
            od[...], om[...], ov[...] = _adamw_math(w[...], g, m[...], v[...])

    ins = [a for grp in big for a in grp] + [a for grp in small_w for a in grp] + [small_g]
    out_shape = ([jax.ShapeDtypeStruct(grp[0].shape, F32) for grp in big for _ in range(3)]
                 + [jax.ShapeDtypeStruct(grp[0].shape, F32) for grp in small_w for _ in range(4)])
    vm = pl.BlockSpec(memory_space=pltpu.VMEM)
    res = pl.pallas_call(
        body, name="adamw", in_specs=[vm] * len(ins), out_specs=[vm] * len(out_shape), out_shape=out_shape,
        compiler_params=pltpu.CompilerParams(vmem_limit_bytes=VMEM_LIMIT),
    )(*ins)
    big_res = [res[3 * i:3 * i + 3] for i in range(nb)]
    small_res = [res[3 * nb + 4 * i:3 * nb + 4 * i + 4] for i in range(ns)]
    return big_res, small_res


def _perm_weights(g_in_t, g_q, g_kv, g_out):
    w_in_t = g_in_t.reshape(D_IN, D_MODEL)
    wq = g_q.transpose(1, 0, 2)
    w_q_p = jnp.pad(wq, ((0, 0), (0, 0), (0, HEAD_LANES - NOPE - ROPE))).reshape(Q_RANK, N_HEADS * HEAD_LANES)
    wkv = g_kv.transpose(1, 0, 2)
    wk = jnp.pad(wkv[:, :, :NOPE], ((0, 0), (0, 0), (0, HEAD_LANES - NOPE))).reshape(KV_RANK, N_HEADS * HEAD_LANES)
    wv = wkv[:, :, NOPE:].reshape(KV_RANK, MLA_WIDTH)
    return w_in_t, w_q_p, jnp.concatenate([wk, wv], axis=1), g_out.reshape(D_MODEL, D_MODEL)


def _grad_slabs(dw_in_t, dw_q_p, dw_kv_p, dw_out):
    s_in = dw_in_t.reshape(N_DEV, D_IN // N_DEV, D_MODEL)
    s_q = dw_q_p.reshape(Q_RANK, N_HEADS, HEAD_LANES)[:, :, :NOPE + ROPE].transpose(1, 0, 2)
    hl = N_HEADS * HEAD_LANES
    dk = dw_kv_p[:, :hl].reshape(KV_RANK, N_HEADS, HEAD_LANES)[:, :, :NOPE]
    dv = dw_kv_p[:, hl:].reshape(KV_RANK, N_HEADS, V_DIM)
    s_kv = jnp.concatenate([dk, dv], axis=2).transpose(1, 0, 2)
    return s_in, s_q, s_kv, dw_out.reshape(N_DEV, D_MODEL // N_DEV, D_MODEL)


def _block_sizes(T):
    return min(256, T), min(256, T), min(512, T)


def kernel(x, positions, ln_g, w_in, q_a_norm_g, w_q_b, kv_a_norm_g, w_kv_b, hg_lower_bounds, hg_norm_g, w_out, final_norm_g, loss_target, m_ln_g, m_w_in, m_q_a_norm_g, m_w_q_b, m_kv_a_norm_g, m_w_kv_b, m_hg_lower_bounds, m_hg_norm_g, m_w_out, m_final_norm_g, v_ln_g, v_w_in, v_q_a_norm_g, v_w_q_b, v_kv_a_norm_g, v_w_kv_b, v_hg_lower_bounds, v_hg_norm_g, v_w_out, v_final_norm_g):
    T = x.shape[1]
    tm, tq, bt = _block_sizes(T)
    nq = T // tq
    xs, tgt = x[0], loss_target[0]
    pos_f = positions.astype(F32)
    fng = final_norm_g.reshape(1, D_MODEL)

    w_in_shard_t = w_in[0].T
    gathered, (c_t, s1_t, s2_t) = _all_gather_weights([w_in_shard_t, w_q_b[0], w_kv_b[0], w_out[0]], pos_f)
    w_in_t, w_q_p, w_kv_p, w_out_b = _perm_weights(*gathered)

    proj, h, qn, kvn, q, k, v, v_t = _fwd_in(
        xs, ln_g, w_in_t, q_a_norm_g, w_q_p, kv_a_norm_g, w_kv_p, c_t, s1_t, s2_t, bt, tq)
    hl = N_HEADS * HEAD_LANES
    k_t = k.reshape(nq, tq, hl).transpose(0, 2, 1)
    q_t = q.reshape(nq, tq, hl).transpose(0, 2, 1)
    o_mla, o_t, lse = _attn_fwd_flat(k, q_t, v_t, tq)
    o_hg, states = _hgrn_fwd(proj, hg_lower_bounds)
    dx2, do_t, d_gm, d_oh, d_gh, loss_p, d_fng, d_hgn, dw_out = _top(
        xs, tgt, o_mla, o_hg, proj, w_out_b, hg_norm_g, fng, tm, tq)
    dq_t, dk, dv = _attn_bwd_flat(k, v, q_t, k_t, do_t, o_t, lse, tq)
    dq = dq_t.transpose(0, 2, 1).reshape(T, N_HEADS * HEAD_LANES)
    d_hq, d_hf, d_hi, d_lb = _hgrn_bwd(proj, hg_lower_bounds, d_oh, states)
    dx, d_lng, d_qg, d_kvg, dw_q_p, dw_kv_p, dw_in_t = _bot(
        xs, dx2, proj, h, qn, kvn, dq, dk, dv, d_gm, d_hq, d_hf, d_hi, d_gh, c_t, s1_t, s2_t, w_in_t, w_q_p, w_kv_p,
        ln_g, q_a_norm_g, kv_a_norm_g, tm)

    small = jnp.concatenate([
        d_lng.reshape(2, SMALL_W), d_fng.reshape(2, SMALL_W), d_lb, loss_p, d_hgn,
        jnp.concatenate([d_qg, d_kvg, jnp.zeros((1, SMALL_W - Q_RANK - KV_RANK), F32)], axis=1)], axis=0)
    g_in, g_q, g_kv, g_out, small_sum = _reduce_scatter(list(_grad_slabs(dw_in_t, dw_q_p, dw_kv_p, dw_out)), small)

    dense = lambda t: t.reshape(-1, LANES)
    big = [(dense(w_in[0].T), dense(g_in), dense(m_w_in[0].T), dense(v_w_in[0].T)), (w_q_b[0], g_q, m_w_q_b[0], v_w_q_b[0]),
           (w_kv_b[0], g_kv, m_w_kv_b[0], v_w_kv_b[0]), (w_out[0], g_out, m_w_out[0], v_w_out[0])]
    small_w = [(ln_g, m_ln_g, v_ln_g),
               (fng, m_final_norm_g.reshape(1, D_MODEL), v_final_norm_g.reshape(1, D_MODEL)),
               (hg_lower_bounds, m_hg_lower_bounds, v_hg_lower_bounds), (hg_norm_g, m_hg_norm_g, v_hg_norm_g),
               (q_a_norm_g, m_q_a_norm_g, v_q_a_norm_g), (kv_a_norm_g, m_kv_a_norm_g, v_kv_a_norm_g)]
    big_res, small_res = _adamw(big, small_w, small_sum)

    loss = small_sum[5, 0]
    (r_in, r_q, r_kv, r_out) = big_res
    (s_ln, s_fn, s_lb, s_hgn, s_qg, s_kvg) = small_res
    flat = lambda t: t.reshape(D_MODEL)
    lead = lambda t: t[None]
    grads = [s_ln[0], lead(g_in.T), s_qg[0], lead(g_q), s_kvg[0], lead(g_kv), s_lb[0], s_hgn[0], lead(g_out), flat(s_fn[0])]

    def pick(i):
        return [s_ln[i + 1], lead(r_in[i].reshape(g_in.shape).T), s_qg[i + 1], lead(r_q[i]), s_kvg[i + 1], lead(r_kv[i]), s_lb[i + 1],
                s_hgn[i + 1], lead(r_out[i]), flat(s_fn[i + 1])]

    return (loss, dx[None], *grads, *pick(0), *pick(1), *pick(2))
```

```python
import math

import numpy as np
import jax
import jax.numpy as jnp
from jax import lax
from jax.experimental import pallas as pl
from jax.experimental.pallas import tpu as pltpu

F32 = jnp.float32
BF16 = jnp.bfloat16

D_MODEL = 1024
N_HEADS = 8
NOPE = 64
ROPE = 32
HALF_ROPE = ROPE // 2
V_DIM = 64
Q_RANK = 256
KV_RANK = 128
MLA_WIDTH = N_HEADS * V_DIM
HG_HEADS = 4
HG_DIM = 128
HG_WIDTH = HG_HEADS * HG_DIM
CHUNK = 64
SUB = 16
D_IN = 2976
D_PERM = 3072
ROPE_THETA = 10000.0
EPS = 1e-6
N_DEV = 8
LANES = 128
HEAD_LANES = 128

P_GM, P_HQ, P_HF, P_HI, P_GH, P_QL, P_KVL, P_KR = 0, 512, 1024, 1536, 2048, 2560, 2816, 2944
R_QL, R_KVL, R_KR, R_MAIN = (0, 256), (256, 384), (384, 416), (416, 2976)
ROPE_LO = NOPE
SCALE = 1.0 / math.sqrt(NOPE + ROPE)

ADAM_LR = 0.001
ADAM_B1 = 0.9
ADAM_B2 = 0.999
ADAM_EPS = 1e-08
ADAM_WD = 0.01
ADAM_STEP = 10

VMEM_LIMIT = 56 * 1024 * 1024
MESH = pl.DeviceIdType.MESH

NT = (((1,), (1,)), ((), ()))
TN = (((0,), (0,)), ((), ()))


def _params(n_grid=0, **kw):
    sem = ("arbitrary",) * n_grid if n_grid else None
    return pltpu.CompilerParams(dimension_semantics=sem, vmem_limit_bytes=VMEM_LIMIT, **kw)


def _dot(a, b):
    return jnp.dot(a, b, preferred_element_type=F32)


def _dot_nt(a, b):
    return lax.dot_general(a, b, NT, preferred_element_type=F32)


def _dot_tn(a, b):
    return lax.dot_general(a, b, TN, preferred_element_type=F32)


def _sigmoid(x):
    return 1.0 / (1.0 + jnp.exp(-x))


def _rope_fwd(x, c, s1, s2):
    return x * c + pltpu.roll(x, LANES - HALF_ROPE, 1) * s1 + pltpu.roll(x, HALF_ROPE, 1) * s2


def _rope_bwd(dy, c, s1, s2):
    return dy * c - pltpu.roll(dy, LANES - HALF_ROPE, 1) * s1 - pltpu.roll(dy, HALF_ROPE, 1) * s2


def _in_proj_rows(wt_ref):
    kr = wt_ref[R_KR[0]:R_KR[1], :]
    pad = lambda n: jnp.zeros((n, D_MODEL), kr.dtype)
    return ((P_GM, wt_ref[R_MAIN[0]:R_MAIN[1], :]), (P_QL, wt_ref[R_QL[0]:R_QL[1], :]),
            (P_KVL, wt_ref[R_KVL[0]:R_KVL[1], :]),
            (P_KR, jnp.concatenate([pad(ROPE_LO), kr, pad(LANES - ROPE_LO - ROPE)], axis=0)))


def _full(shape):
    n = len(shape)
    return pl.BlockSpec(shape, lambda *_: (0,) * n)


ROPE_BLOCK = 512


def _rope_constants():
    inv = (np.float32(ROPE_THETA) ** (-np.arange(HALF_ROPE, dtype=np.float32) / np.float32(HALF_ROPE))).astype(np.float32)
    place = np.zeros((3, HALF_ROPE, LANES), np.float32)
    for i in range(HALF_ROPE):
        place[0, i, ROPE_LO + i] = place[0, i, ROPE_LO + HALF_ROPE + i] = 1.0
        place[1, i, ROPE_LO + i] = -1.0
        place[2, i, ROPE_LO + HALF_ROPE + i] = 1.0
    base = np.ones((1, LANES), np.float32)
    base[0, ROPE_LO:ROPE_LO + ROPE] = 0.0
    return jnp.asarray(inv.reshape(HALF_ROPE, 1)), jnp.asarray(place), jnp.asarray(base)


def _rope_block(pos, inv, place_ref, base):
    ang = inv * pos
    cos, sin = jnp.cos(ang), jnp.sin(ang)

    def put(v, k):
        return lax.dot_general(v, place_ref[k], TN, precision=lax.Precision.HIGHEST, preferred_element_type=F32)

    return put(cos, 0) + base, put(sin, 1), put(sin, 2)


def _all_gather_weights(shards, pos_f):
    n = len(shards)
    T = pos_f.shape[1]
    rb = min(ROPE_BLOCK, T)

    def body(*refs):
        ins, (pos_ref, inv_ref, place_ref, base_ref) = refs[:n], refs[n:n + 4]
        outs, tables = refs[n + 4:2 * n + 4], refs[2 * n + 4:2 * n + 7]
        send_sems, recv_sems = refs[2 * n + 7], refs[2 * n + 8]
        x, y, c = lax.axis_index("x"), lax.axis_index("y"), lax.axis_index("c")
        me, sibling = (x, y, c), (x, y, 1 - c)
        chips = [(1 - x, y), (x, 1 - y), (1 - x, 1 - y)]

        def idx(d):
            return 4 * d[0] + 2 * d[1] + d[2]

        def copy(a, k, block, to):
            rows = outs[a].at[idx(block)]
            return pltpu.make_async_remote_copy(src_ref=rows, dst_ref=rows, send_sem=send_sems.at[a, k],
                                                recv_sem=recv_sems.at[a, k], device_id=to, device_id_type=MESH)

        for a in range(n):
            outs[a][idx(me)] = ins[a][...].astype(BF16)
        first = []
        for a in range(n):
            first.append(copy(a, 0, me, sibling))
            first += [copy(a, 1 + j, me, (*chip, c)) for j, chip in enumerate(chips)]
        for cp in first:
            cp.start()
        for r0 in range(0, T, rb):
            for ref, tab in zip(tables, _rope_block(pos_ref[:, r0:r0 + rb], inv_ref[...], place_ref, base_ref[...])):
                ref[r0:r0 + rb, :] = tab
        passed = []
        for j, chip in enumerate(chips):
            for a in range(n):
                copy(a, 1 + j, (*chip, c), me).wait_recv()
                cp = copy(a, 4 + j, (*chip, c), sibling)
                cp.start()
                passed.append(cp)
        for a in range(n):
            copy(a, 0, sibling, me).wait_recv()
            for j, chip in enumerate(chips):
                copy(a, 4 + j, (*chip, 1 - c), me).wait_recv()
        for cp in first + passed:
            cp.wait_send()

    vm = pl.BlockSpec(memory_space=pltpu.VMEM)
    res = pl.pallas_call(
        body, name="all_gather_weights",
        in_specs=[vm] * (n + 4), out_specs=[vm] * (n + 3),
        out_shape=[jax.ShapeDtypeStruct((N_DEV,) + s.shape, BF16) for s in shards]
        + [jax.ShapeDtypeStruct((T, LANES), F32)] * 3,
        scratch_shapes=[pltpu.SemaphoreType.DMA((n, 7)), pltpu.SemaphoreType.DMA((n, 7))],
        compiler_params=pltpu.CompilerParams(vmem_limit_bytes=VMEM_LIMIT),
    )(*shards, pos_f, *_rope_constants())
    return res[:n], res[n:]


def _fwd_in(x, ln_g, w_in_t, q_g, w_q_p, kv_g, w_kv_p, c_t, s1_t, s2_t, lbp, tm, tq):
    T = x.shape[0]
    assert tm % tq == 0 and tm % CHUNK == 0
    ns = tm // CHUNK

    def body(x_ref, lng_ref, win_ref, qg_ref, wq_ref, kvg_ref, wkv_ref, c_ref, s1_ref, s2_ref, lbp_ref,
             proj_ref, h_ref, qn_ref, kvn_ref, q_ref, k_ref, v_ref, vt_ref, ohg_ref, st_ref, state):
        @pl.when(pl.program_id(0) == 0)
        def _():
            state[...] = jnp.zeros_like(state)

        xv = x_ref[...]
        r = lax.rsqrt(jnp.mean(xv * xv, axis=-1, keepdims=True) + EPS)
        h = (xv * r * lng_ref[...]).astype(BF16)
        h_ref[...] = h
        for col, rows in _in_proj_rows(win_ref):
            proj_ref[:, col:col + rows.shape[0]] = _dot_nt(h, rows)
        c, s1, s2 = c_ref[...], s1_ref[...], s2_ref[...]

        ql = proj_ref[:, P_QL:P_QL + Q_RANK]
        rq = lax.rsqrt(jnp.mean(ql * ql, axis=-1, keepdims=True) + EPS)
        qn = (ql * rq * qg_ref[...]).astype(BF16)
        qn_ref[...] = qn
        q = _dot(qn, wq_ref[...])
        for hd in range(N_HEADS):
            sl = slice(hd * HEAD_LANES, (hd + 1) * HEAD_LANES)
            q_ref[:, sl] = _rope_fwd(q[:, sl], c, s1, s2).astype(BF16)

        kvl = proj_ref[:, P_KVL:P_KVL + KV_RANK]
        rk = lax.rsqrt(jnp.mean(kvl * kvl, axis=-1, keepdims=True) + EPS)
        kvn = (kvl * rk * kvg_ref[...]).astype(BF16)
        kvn_ref[...] = kvn
        kv = _dot(kvn, wkv_ref[...])
        kpe = _rope_fwd(proj_ref[:, P_KR:P_KR + LANES], c, s1, s2)
        for hd in range(N_HEADS):
            sl = slice(hd * HEAD_LANES, (hd + 1) * HEAD_LANES)
            k_ref[:, sl] = (kv[:, sl] + kpe).astype(BF16)
        v_ref[...] = kv[:, N_HEADS * HEAD_LANES:].astype(BF16)
        for b in range(tm // tq):
            vt_ref[b] = kv[b * tq:(b + 1) * tq, N_HEADS * HEAD_LANES:].T.astype(BF16)

        at = {"hq": P_HQ, "hf": P_HF, "hi": P_HI}
        _hgrn_fwd_step(lambda name, r: proj_ref[r, at[name]:at[name] + HG_WIDTH], lbp_ref, ohg_ref, st_ref, state, ns)

    def row(w):
        return pl.BlockSpec((tm, w), lambda i: (i, 0))

    outs = [(D_PERM, F32), (D_MODEL, BF16), (Q_RANK, BF16), (KV_RANK, BF16),
            (N_HEADS * HEAD_LANES, BF16), (N_HEADS * HEAD_LANES, BF16), (MLA_WIDTH, BF16)]
    return pl.pallas_call(
        body, name="fwd_in", grid=(T // tm,),
        in_specs=[row(D_MODEL), _full((1, D_MODEL)), _full((D_IN, D_MODEL)), _full((1, Q_RANK)),
                  _full((Q_RANK, N_HEADS * HEAD_LANES)), _full((1, KV_RANK)),
                  _full((KV_RANK, N_HEADS * HEAD_LANES + MLA_WIDTH)), row(LANES), row(LANES), row(LANES),
                  _full((2, HG_WIDTH))],
        out_specs=[row(w) for w, _ in outs] + [pl.BlockSpec((tm // tq, MLA_WIDTH, tq), lambda i: (i, 0, 0)),
                                               row(HG_WIDTH),
                                               pl.BlockSpec((ns, HG_HEADS, HG_DIM, HG_DIM), lambda i: (i, 0, 0, 0))],
        out_shape=[jax.ShapeDtypeStruct((T, w), dt) for w, dt in outs]
        + [jax.ShapeDtypeStruct((T // tq, MLA_WIDTH, tq), BF16), jax.ShapeDtypeStruct((T, HG_WIDTH), F32),
           jax.ShapeDtypeStruct((T // CHUNK, HG_HEADS, HG_DIM, HG_DIM), F32)],
        scratch_shapes=[pltpu.VMEM((HG_HEADS, HG_DIM, HG_DIM), F32)],
        compiler_params=_params(1),
    )(x, ln_g, w_in_t, q_g, w_q_p, kv_g, w_kv_p, c_t, s1_t, s2_t, lbp)


LOG2E = 1.4426950408889634
SCALE2 = SCALE * LOG2E


def _causal(tq):
    r = lax.broadcasted_iota(jnp.int32, (tq, tq), 0)
    c = lax.broadcasted_iota(jnp.int32, (tq, tq), 1)
    return r <= c


MASKED = -1e30


def _causal_bias(bias_ref, tq):
    bias_ref[0] = jnp.zeros((tq, tq), F32)
    bias_ref[1] = jnp.where(_causal(tq), 0.0, MASKED)


def _tile_tables(nq, by_query):
    if by_query:
        pairs = [(j, i) for i in range(nq) for j in range(i + 1)]
    else:
        pairs = [(j, i) for j in range(nq) for i in range(nq - 1, j - 1, -1)]
    pairs.append(pairs[-1])
    jj, ii = np.array(pairs, np.int32).T
    return jnp.asarray(jj), jnp.asarray(ii), len(pairs) - 1


ATTN_TRIP = 8


def _walk_tiles(n, products, tile, flush, buf_a, buf_b):
    bufs = (buf_a, buf_b)
    products(0, buf_a)

    def trip(r, carry):
        for u in range(ATTN_TRIP):
            products(ATTN_TRIP * r + u + 1, bufs[(u + 1) % 2])
            tile(ATTN_TRIP * r + u, bufs[u % 2])
        for u in range(ATTN_TRIP):
            flush(ATTN_TRIP * r + u)
        return carry

    lax.fori_loop(0, n // ATTN_TRIP, trip, 0)
    rest = n - n % ATTN_TRIP
    for u in range(n % ATTN_TRIP):
        if rest + u + 1 < n:
            products(rest + u + 1, bufs[(u + 1) % 2])
        tile(rest + u, bufs[u % 2])
    for u in range(n % ATTN_TRIP):
        flush(rest + u)


V_ROWS = V_DIM + 16


def _attn_fwd_flat(k, q_t, v_t, tq):
    T = k.shape[0]
    nq = T // tq
    jj, ii, n = _tile_tables(nq, True)
    heads = [slice(hh * HEAD_LANES, (hh + 1) * HEAD_LANES) for hh in range(2)]

    def body(jj_ref, ii_ref, k_ref, qt_ref, vt_ref, o_ref, ot_ref, lse_ref, sa_ref, sb_ref, m_ref, acc_ref, bias_ref):
        def reset(st):
            m_ref[st] = jnp.full(m_ref.shape[1:], MASKED, F32)
            acc_ref[st] = jnp.zeros(acc_ref.shape[1:], F32)

        _causal_bias(bias_ref, tq)
        for st in range(ATTN_TRIP):
            reset(st)
        extra = (lax.broadcasted_iota(jnp.int32, (V_ROWS - V_DIM, tq), 0) == 0).astype(BF16)

        def products(t, buf):
            j, i = jj_ref[t], ii_ref[t]
            kj = k_ref[pl.ds(pl.multiple_of(j * tq, tq), tq), :]
            for hh, sl in enumerate(heads):
                buf[hh] = _dot(kj[:, sl], qt_ref[i, sl, :])

        def tile(t, buf):
            j, i = jj_ref[t], ii_ref[t]
            vt = vt_ref[j]
            bias = bias_ref.at[(j == i).astype(jnp.int32)]
            st = i % ATTN_TRIP
            for hh in range(2):
                s = buf[hh] * SCALE2 + bias[...]
                m = m_ref[st, hh]
                m_new = jnp.maximum(m, jnp.max(s, axis=0, keepdims=True))
                alpha = jnp.exp2(m - m_new)
                p = jnp.exp2(s - m_new)
                m_ref[st, hh] = m_new
                v_h = jnp.concatenate([vt[hh * V_DIM:(hh + 1) * V_DIM, :], extra], axis=0)
                acc_ref[st, hh] = alpha * acc_ref[st, hh] + _dot(v_h, p.astype(BF16))

        def flush(t):
            j, i = jj_ref[t], ii_ref[t]

            @pl.when(j == i)
            def _():
                st = i % ATTN_TRIP
                den = [acc_ref[st, hh, V_DIM:V_DIM + 1, :] for hh in range(2)]
                out = jnp.concatenate([acc_ref[st, hh, :V_DIM, :] / den[hh] for hh in range(2)], axis=0)
                o_ref[pl.ds(pl.multiple_of(i * tq, tq), tq), :] = out.T
                ot_ref[i] = out
                for hh in range(2):
                    lse_ref[hh, i] = m_ref[st, hh] + jnp.log2(den[hh])
                reset(st)

        _walk_tiles(n, products, tile, flush, sa_ref, sb_ref)

    smem = pl.BlockSpec(memory_space=pltpu.SMEM)
    return pl.pallas_call(
        body, name="attn_fwd", grid=(N_HEADS // 2,),
        in_specs=[smem, smem,
                  pl.BlockSpec((T, 2 * HEAD_LANES), lambda p: (0, p)),
                  pl.BlockSpec((nq, 2 * HEAD_LANES, tq), lambda p: (0, p, 0)),
                  pl.BlockSpec((nq, LANES, tq), lambda p: (0, p, 0))],
        out_specs=[pl.BlockSpec((T, LANES), lambda p: (0, p)), pl.BlockSpec((nq, LANES, tq), lambda p: (0, p, 0)),
                   pl.BlockSpec((2, nq, 1, tq), lambda p: (p, 0, 0, 0))],
        out_shape=[jax.ShapeDtypeStruct((T, MLA_WIDTH), F32), jax.ShapeDtypeStruct((nq, MLA_WIDTH, tq), F32),
                   jax.ShapeDtypeStruct((N_HEADS, nq, 1, tq), F32)],
        scratch_shapes=[pltpu.VMEM((2, tq, tq), F32), pltpu.VMEM((2, tq, tq), F32),
                        pltpu.VMEM((ATTN_TRIP, 2, 1, tq), F32),
                        pltpu.VMEM((ATTN_TRIP, 2, V_ROWS, tq), F32), pltpu.VMEM((2, tq, tq), F32)],
        compiler_params=_params(1),
    )(jj, ii, k, q_t, v_t)


def _attn_bwd_flat(k, v, q_t, k_t, do_t, o_t, lse, tq):
    T = k.shape[0]
    nq = T // tq
    jj, ii, n = _tile_tables(nq, False)
    heads = [slice(hh * HEAD_LANES, (hh + 1) * HEAD_LANES) for hh in range(2)]

    def body(jj_ref, ii_ref, k_ref, v_ref, qt_ref, kt_ref, dot_ref, ot_ref, lse_ref, dqt_ref, dk_ref, dv_ref,
             ba_ref, bb_ref, dkt_ref, dvt_ref, bias_ref, dsum_ref):
        _causal_bias(bias_ref, tq)

        def row_dots(i, carry):
            for hh in range(2):
                own = slice(hh * V_DIM, (hh + 1) * V_DIM)
                do_h = dot_ref[i, hh * HEAD_LANES + own.start:hh * HEAD_LANES + own.stop, :]
                dsum_ref[hh, i] = jnp.sum(do_h.astype(F32) * ot_ref[i, own, :], axis=0, keepdims=True)
            return carry

        lax.fori_loop(0, nq, row_dots, 0)
        dqt_ref[...] = jnp.zeros_like(dqt_ref)
        dkt_ref[...] = jnp.zeros_like(dkt_ref)
        dvt_ref[...] = jnp.zeros_like(dvt_ref)

        def products(t, buf):
            j, i = jj_ref[t], ii_ref[t]
            rows = pl.ds(pl.multiple_of(j * tq, tq), tq)
            for hh, sl in enumerate(heads):
                buf[hh] = _dot(k_ref[rows, sl], qt_ref[i, sl, :])
                buf[2 + hh] = _dot(v_ref[rows, :], dot_ref[i, sl, :])

        def tile(t, buf):
            j, i = jj_ref[t], ii_ref[t]
            bias = bias_ref.at[(j == i).astype(jnp.int32)]
            st = j % ATTN_TRIP
            for hh, sl in enumerate(heads):
                p = jnp.exp2(buf[hh] * SCALE2 + bias[...] - lse_ref[hh, i])
                ds = (p * (buf[2 + hh] - dsum_ref[hh, i]) * SCALE).astype(BF16)
                own = slice(hh * V_DIM, (hh + 1) * V_DIM)
                do_h = dot_ref[i, hh * HEAD_LANES + own.start:hh * HEAD_LANES + own.stop, :]
                dvt_ref[st, own, :] += _dot_nt(do_h, p.astype(BF16))
                used = slice(sl.start, sl.start + NOPE + ROPE)
                dkt_ref[st, used, :] += _dot_nt(qt_ref[i, used, :], ds)
                dqt_ref[i, used, :] += _dot(kt_ref[j, used, :], ds)

        def flush(t):
            j, i = jj_ref[t], ii_ref[t]

            @pl.when(j == i)
            def _():
                st = j % ATTN_TRIP
                rows = pl.ds(pl.multiple_of(j * tq, tq), tq)
                dk_ref[rows, :] = dkt_ref[st].T
                dv_ref[rows, :] = dvt_ref[st].T.astype(BF16)
                dkt_ref[st] = jnp.zeros(dkt_ref.shape[1:], F32)
                dvt_ref[st] = jnp.zeros(dvt_ref.shape[1:], F32)

        _walk_tiles(n, products, tile, flush, ba_ref, bb_ref)

    smem = pl.BlockSpec(memory_space=pltpu.SMEM)
    stat = pl.BlockSpec((2, nq, 1, tq), lambda p: (p, 0, 0, 0))
    blocks_t = pl.BlockSpec((nq, 2 * HEAD_LANES, tq), lambda p: (0, p, 0))
    return pl.pallas_call(
        body, name="attn_bwd", grid=(N_HEADS // 2,),
        in_specs=[smem, smem,
                  pl.BlockSpec((T, 2 * HEAD_LANES), lambda p: (0, p)),
                  pl.BlockSpec((T, LANES), lambda p: (0, p)),
                  blocks_t, blocks_t, blocks_t, pl.BlockSpec((nq, LANES, tq), lambda p: (0, p, 0)), stat],
        out_specs=[blocks_t,
                   pl.BlockSpec((T, 2 * HEAD_LANES), lambda p: (0, p)),
                   pl.BlockSpec((T, LANES), lambda p: (0, p))],
        out_shape=[jax.ShapeDtypeStruct((nq, N_HEADS * HEAD_LANES, tq), F32),
                   jax.ShapeDtypeStruct((T, N_HEADS * HEAD_LANES), F32),
                   jax.ShapeDtypeStruct((T, MLA_WIDTH), BF16)],
        scratch_shapes=[pltpu.VMEM((4, tq, tq), F32), pltpu.VMEM((4, tq, tq), F32),
                        pltpu.VMEM((ATTN_TRIP, 2 * HEAD_LANES, tq), F32), pltpu.VMEM((ATTN_TRIP, LANES, tq), F32),
                        pltpu.VMEM((2, tq, tq), F32), pltpu.VMEM((2, nq, 1, tq), F32)],
        compiler_params=_params(1),
    )(jj, ii, k, v, q_t, k_t, do_t, o_t, lse)


def _lower_bound(lbp):
    a, b = lbp[0:1, :], lbp[1:2, :]
    mx = jnp.maximum(a, b)
    ea, eb = jnp.exp(a - mx), jnp.exp(b - mx)
    return ea / (ea + eb)


def _tri(lower):
    r = lax.broadcasted_iota(jnp.int32, (CHUNK, CHUNK), 0)
    c = lax.broadcasted_iota(jnp.int32, (CHUNK, CHUNK), 1)
    return (c <= r) if lower else (c >= r)


def _running_sum(x, from_end):
    row = lax.broadcasted_iota(jnp.int32, x.shape, 0)
    step = 1
    while step < CHUNK:
        if from_end:
            x = x + jnp.where(row < CHUNK - step, pltpu.roll(x, CHUNK - step, 0), 0.0)
        else:
            x = x + jnp.where(row >= step, pltpu.roll(x, step, 0), 0.0)
        step *= 2
    return x


def _hg_gates(hq, hf, lb):
    sq = _sigmoid(hq)
    sf = _sigmoid(hf)
    f = lb + (1.0 - lb) * sf
    g = jnp.log(f)
    gcum = _running_sum(g, False)
    return sq, sf, f, hq * sq, 1.0 - f, gcum


def _head(x, hd):
    return x[:, hd * HG_DIM:(hd + 1) * HG_DIM]


def _all_heads(fn):
    return jnp.concatenate([fn(hd) for hd in range(HG_HEADS)], axis=1)


def _hg_blocks(q, kk, gcum):
    rowi = lax.broadcasted_iota(jnp.int32, gcum.shape, 0)
    out = []
    for blk in range(CHUNK // SUB):
        lo, hi = blk * SUB, (blk + 1) * SUB
        gb = gcum[lo - 1:lo, :] if blk else jnp.zeros_like(gcum[0:1, :])
        eq = jnp.exp(gcum[lo:hi, :] - gb)
        ek = jnp.exp(jnp.where(rowi < hi, gb - gcum, 0.0))
        out.append((eq, ek, (q[lo:hi, :] * eq).astype(BF16), (kk * ek).astype(BF16)))
    return out


def _hg_scores(blocks):
    out = []
    for hd in range(HG_HEADS):
        a = jnp.concatenate([_dot_nt(_head(qb, hd), _head(kb, hd)) for _, _, qb, kb in blocks], axis=0)
        out.append(jnp.where(_tri(True), a, 0.0))
    return out


HG_STEP_CHUNKS = 8


def _hgrn_fwd_step(cols, lbp_ref, o_ref, st_ref, state, ns):
    lb = _lower_bound(lbp_ref[...])
    work = []
    for c in range(ns):
        r = slice(c * CHUNK, (c + 1) * CHUNK)
        _, _, _, q, kk, gcum = _hg_gates(cols("hq", r), cols("hf", r), lb)
        vb = cols("hi", r).astype(BF16)
        a = _hg_scores(_hg_blocks(q, kk, gcum))
        gend = gcum[CHUNK - 1:CHUNK, :]
        qgb = (q * jnp.exp(gcum)).astype(BF16)
        kgeb = (kk * jnp.exp(gend - gcum)).astype(BF16)
        intra = [_dot(a[hd].astype(BF16), _head(vb, hd)) for hd in range(HG_HEADS)]
        update = [_dot_tn(_head(vb, hd), _head(kgeb, hd)) for hd in range(HG_HEADS)]
        work.append((qgb, jnp.exp(gend), intra, update))
    for hd in range(HG_HEADS):
        st = state[hd]
        for c, (qgb, egend, intra, update) in enumerate(work):
            st_ref[c, hd] = st
            o_ref[c * CHUNK:(c + 1) * CHUNK, hd * HG_DIM:(hd + 1) * HG_DIM] = (
                intra[hd] + _dot_nt(_head(qgb, hd), st.astype(BF16)))
            st = st * _head(egend, hd) + update[hd]
        state[hd] = st


def _hgrn_bwd(proj, lbp, do_hg, states):
    T = proj.shape[0]
    nc = T // CHUNK
    ns = min(HG_STEP_CHUNKS, nc)
    rows = ns * CHUNK
    steps = nc // ns

    def body(hq_ref, hf_ref, hi_ref, lbp_ref, do_ref, st_ref, dhq_ref, dhf_ref, dhi_ref, dlb_ref, dstate):
        @pl.when(pl.program_id(0) == 0)
        def _():
            dstate[...] = jnp.zeros_like(dstate)
            dlb_ref[...] = jnp.zeros_like(dlb_ref)

        lb = _lower_bound(lbp_ref[...])

        dst_all = [dstate[hd] for hd in range(HG_HEADS)]
        dlb = jnp.zeros_like(lb)
        last = lax.broadcasted_iota(jnp.int32, (CHUNK, HG_WIDTH), 0) == CHUNK - 1
        for c in reversed(range(ns)):
            r = slice(c * CHUNK, (c + 1) * CHUNK)
            hq = hq_ref[r, :]
            sq, sf, f, q, kk, gcum = _hg_gates(hq, hf_ref[r, :], lb)
            vb = hi_ref[r, :].astype(BF16)
            dob = do_ref[r, :].astype(BF16)
            blocks = _hg_blocks(q, kk, gcum)
            a = _hg_scores(blocks)
            gend = gcum[CHUNK - 1:CHUNK, :]
            eg, egend, ekend = jnp.exp(gcum), jnp.exp(gend), jnp.exp(gend - gcum)
            qg, kge = q * eg, kk * ekend
            qgb, kgeb = qg.astype(BF16), kge.astype(BF16)

            dv, dqg, dkge, st_dst, dq_blk, dk_blk = [], [], [], [], [], []
            for hd in range(HG_HEADS):
                st = st_ref[c, hd]
                dst = dst_all[hd]
                dstb = dst.astype(BF16)
                do_h, v_h = _head(dob, hd), _head(vb, hd)
                dv.append(_dot_tn(a[hd].astype(BF16), do_h) + _dot_nt(_head(kgeb, hd), dstb))
                da = jnp.where(_tri(True), _dot_nt(do_h, v_h), 0.0).astype(BF16)
                dqg.append(_dot(do_h, st.astype(BF16)))
                dkge.append(_dot(v_h, dstb))
                st_dst.append(jnp.sum(st * dst, axis=0, keepdims=True))
                dst_all[hd] = _dot_tn(do_h, _head(qgb, hd)) + dst * _head(egend, hd)
                dq_blk.append([_dot(da[b * SUB:(b + 1) * SUB, :], _head(kb, hd)) for b, (_, _, _, kb) in enumerate(blocks)])
                dk_blk.append([_dot_tn(da[b * SUB:(b + 1) * SUB, :], _head(qb, hd)) for b, (_, _, qb, _) in enumerate(blocks)])
            dv, dqg, dkge, st_dst = (jnp.concatenate(t, axis=1) for t in (dv, dqg, dkge, st_dst))

            dq_a, dg_q = [], []
            dk_a, dg_k = jnp.zeros_like(gcum), jnp.zeros_like(gcum)
            for b, (eq, ek, qb, kb) in enumerate(blocks):
                dq_b = _all_heads(lambda hd: dq_blk[hd][b])
                dk_b = _all_heads(lambda hd: dk_blk[hd][b])
                dq_a.append(dq_b * eq)
                dk_a = dk_a + dk_b * ek
                dg_q.append(qb.astype(F32) * dq_b)
                dg_k = dg_k + kb.astype(F32) * dk_b
            dq_a = jnp.concatenate(dq_a, axis=0)

            dgend = st_dst * egend + jnp.sum(dkge * kge, axis=0, keepdims=True)
            dq = dq_a + dqg * eg
            dk = dk_a + dkge * ekend
            dgc = jnp.concatenate(dg_q, axis=0) - dg_k + dqg * qg - dkge * kge + jnp.where(last, dgend, 0.0)
            dg = _running_sum(dgc, True)
            df = dg / f - dk
            dhf_ref[r, :] = (df * (1.0 - lb) * sf * (1.0 - sf)).astype(BF16)
            dlb = dlb + jnp.sum(df * (1.0 - sf), axis=0, keepdims=True)
            dhq_ref[r, :] = (dq * (sq * (1.0 + hq * (1.0 - sq)))).astype(BF16)
            dhi_ref[r, :] = dv.astype(BF16)
        for hd in range(HG_HEADS):
            dstate[hd] = dst_all[hd]
        dlb_ref[...] += dlb

    def col(cb):
        return pl.BlockSpec((rows, HG_WIDTH), lambda i: (steps - 1 - i, cb))

    grad = jax.ShapeDtypeStruct((T, HG_WIDTH), BF16)
    return pl.pallas_call(
        body, name="hgrn_bwd", grid=(steps,),
        in_specs=[col(P_HQ // HG_WIDTH), col(P_HF // HG_WIDTH), col(P_HI // HG_WIDTH), _full((2, HG_WIDTH)),
                  col(0), pl.BlockSpec((ns, HG_HEADS, HG_DIM, HG_DIM), lambda i: (steps - 1 - i, 0, 0, 0))],
        out_specs=[col(0), col(0), col(0), _full((1, HG_WIDTH))],
        out_shape=[grad, grad, grad, jax.ShapeDtypeStruct((1, HG_WIDTH), F32)],
        scratch_shapes=[pltpu.VMEM((HG_HEADS, HG_DIM, HG_DIM), F32)],
        compiler_params=_params(1),
    )(proj, proj, proj, lbp, do_hg, states)


def _top(x, tgt, o_mla, o_hg, proj, w_out, hg_norm_g, final_g, tm, tq):
    T = x.shape[0]
    assert tm % tq == 0

    def body(x_ref, tgt_ref, om_ref, oh_ref, gm_ref, gh_ref, wout_ref, hgn_ref, fng_ref,
             dx2_ref, dot_ref, dgm_ref, doh_ref, dgh_ref, loss_ref, dfng_ref, dhgn_ref, dwout_ref, ycat_ref):
        @pl.when(pl.program_id(0) == 0)
        def _():
            for ref in (loss_ref, dfng_ref, dhgn_ref, dwout_ref):
                ref[...] = jnp.zeros_like(ref)

        gm, om = gm_ref[...], om_ref[...]
        sgm = _sigmoid(gm)
        silu_m = gm * sgm
        gh, oh, gam = gh_ref[...], oh_ref[...], hgn_ref[...]
        sgh = _sigmoid(gh)
        silu_h = gh * sgh
        rr, nn = [], []
        for hd in range(HG_HEADS):
            oh_h = oh[:, hd * HG_DIM:(hd + 1) * HG_DIM]
            r_h = lax.rsqrt(jnp.mean(oh_h * oh_h, axis=-1, keepdims=True) + EPS)
            rr.append(r_h)
            nn.append(oh_h * r_h)
        n = jnp.concatenate(nn, axis=1)
        ng = n * gam
        ycat_ref[:, :MLA_WIDTH] = (om * silu_m).astype(BF16)
        ycat_ref[:, MLA_WIDTH:] = (ng * silu_h).astype(BF16)
        wout = wout_ref[...]
        x2 = x_ref[...] + _dot(ycat_ref[...], wout)
        r = lax.rsqrt(jnp.mean(x2 * x2, axis=-1, keepdims=True) + EPS)
        xh = x2 * r
        fng = fng_ref[...]
        err = xh * fng - tgt_ref[...]
        loss_ref[...] += 0.5 * jnp.sum(jnp.mean(err * err, axis=-1, keepdims=True), axis=0, keepdims=True)
        dout = err * (1.0 / D_MODEL)
        dfng_ref[...] += jnp.sum(dout * xh, axis=0, keepdims=True)
        dxh = dout * fng
        dx2 = r * (dxh - xh * jnp.mean(dxh * xh, axis=-1, keepdims=True))
        dx2_ref[...] = dx2
        dx2b = dx2.astype(BF16)
        dwout_ref[...] += _dot_tn(ycat_ref[...], dx2b)
        dycat = _dot_nt(dx2b, wout)
        dym, dyh = dycat[:, :MLA_WIDTH], dycat[:, MLA_WIDTH:]
        dom = dym * silu_m
        first = lax.broadcasted_iota(jnp.int32, (tm, LANES), 1) < V_DIM
        for hd in range(N_HEADS):
            pair = dom[:, hd // 2 * LANES:(hd // 2 + 1) * LANES]
            own = jnp.where(first, pair, 0.0) if hd % 2 == 0 else jnp.where(first, 0.0, pair)
            for b in range(tm // tq):
                dot_ref[b, hd * HEAD_LANES:(hd + 1) * HEAD_LANES, :] = own[b * tq:(b + 1) * tq, :].T.astype(BF16)
        dgm_ref[...] = (dym * om * (sgm * (1.0 + gm * (1.0 - sgm)))).astype(BF16)
        dgh_ref[...] = (dyh * ng * (sgh * (1.0 + gh * (1.0 - sgh)))).astype(BF16)
        dng = dyh * silu_h
        dhgn_ref[...] += jnp.sum(dng * n, axis=0, keepdims=True)
        dn = dng * gam
        for hd in range(HG_HEADS):
            sl = slice(hd * HG_DIM, (hd + 1) * HG_DIM)
            dn_h, n_h = dn[:, sl], nn[hd]
            doh_ref[:, sl] = rr[hd] * (dn_h - n_h * jnp.mean(dn_h * n_h, axis=-1, keepdims=True))

    def row(w, cb=0):
        return pl.BlockSpec((tm, w), lambda i: (i, cb))

    hl = N_HEADS * HEAD_LANES
    blocks_t = pl.BlockSpec((tm // tq, hl, tq), lambda i: (i, 0, 0))
    outs = [(D_MODEL, F32), (0, BF16), (MLA_WIDTH, BF16), (HG_WIDTH, F32), (HG_WIDTH, BF16)]
    small = [(1, SMALL_W), (1, D_MODEL), (1, HG_WIDTH), (D_MODEL, D_MODEL)]
    return pl.pallas_call(
        body, name="top", grid=(T // tm,),
        in_specs=[row(D_MODEL), row(D_MODEL), row(MLA_WIDTH), row(HG_WIDTH),
                  row(MLA_WIDTH, P_GM // MLA_WIDTH), row(HG_WIDTH, P_GH // HG_WIDTH),
                  _full((D_MODEL, D_MODEL)), _full((1, HG_WIDTH)), _full((1, D_MODEL))],
        out_specs=[row(w) if w else blocks_t for w, _ in outs] + [_full(s) for s in small],
        out_shape=[jax.ShapeDtypeStruct((T, w) if w else (T // tq, hl, tq), dt) for w, dt in outs]
        + [jax.ShapeDtypeStruct(s, F32) for s in small],
        scratch_shapes=[pltpu.VMEM((tm, D_MODEL), BF16)],
        compiler_params=_params(1),
    )(x, tgt, o_mla, o_hg, proj, proj, w_out, hg_norm_g, final_g)


def _bot(x, dx2, proj, h, qn, kvn, dq, dk, dv, dgm, dhq, dhf, dhi, dgh, c_t, s1_t, s2_t, w_in_t, w_q_p, w_kv_p, ln_g, q_g, kv_g, tm):
    T = x.shape[0]
    lat_w = D_PERM - P_QL
    steps = T // tm

    def body(x_ref, dx2_ref, lat_ref, h_ref, qn_ref, kvn_ref, dq_ref, dk_ref, dv_ref, dgm_ref, dhq_ref, dhf_ref,
             dhi_ref, dgh_ref, c_ref, s1_ref, s2_ref, win_ref, wq_ref, wkv_ref, lng_ref, qg_ref, kvg_ref,
             dx_ref, dlng_ref, dqg_ref, dkvg_ref, dwq_ref, dwkv_ref, dwin_ref, dqpre_ref, dkv_ref, dproj_ref,
             dwin_acc, sem):
        @pl.when(pl.program_id(0) == 0)
        def _():
            for ref in (dlng_ref, dqg_ref, dkvg_ref, dwq_ref, dwkv_ref, dwin_acc):
                ref[...] = jnp.zeros_like(ref)

        c, s1, s2 = c_ref[...], s1_ref[...], s2_ref[...]
        dkpe = jnp.zeros((tm, LANES), F32)
        for hd in range(N_HEADS):
            sl = slice(hd * HEAD_LANES, (hd + 1) * HEAD_LANES)
            dqpre_ref[:, sl] = _rope_bwd(dq_ref[:, sl], c, s1, s2).astype(BF16)
            dk_h = dk_ref[:, sl]
            dkpe = dkpe + dk_h
            dkv_ref[:, sl] = dk_h.astype(BF16)
        dkv_ref[:, N_HEADS * HEAD_LANES:] = dv_ref[...].astype(BF16)
        lane = lax.broadcasted_iota(jnp.int32, (tm, LANES), 1)
        rope_lanes = jnp.logical_and(lane >= ROPE_LO, lane < ROPE_LO + ROPE)
        dkr = jnp.where(rope_lanes, _rope_bwd(dkpe, c, s1, s2), 0.0)

        def norm_bwd(v, g, dy):
            r = lax.rsqrt(jnp.mean(v * v, axis=-1, keepdims=True) + EPS)
            vh = v * r
            dvh = dy * g
            return jnp.sum(dy * vh, axis=0, keepdims=True), r * (dvh - vh * jnp.mean(dvh * vh, axis=-1, keepdims=True))

        dwq_ref[...] += _dot_tn(qn_ref[...], dqpre_ref[...])
        dwkv_ref[...] += _dot_tn(kvn_ref[...], dkv_ref[...])
        dqn = _dot_nt(dqpre_ref[...], wq_ref[...])
        dg_q, dql = norm_bwd(lat_ref[:, :Q_RANK], qg_ref[...], dqn)
        dqg_ref[...] += dg_q
        dkn = _dot_nt(dkv_ref[...], wkv_ref[...])
        dg_kv, dkvl = norm_bwd(lat_ref[:, Q_RANK:Q_RANK + KV_RANK], kvg_ref[...], dkn)
        dkvg_ref[...] += dg_kv

        dproj_ref[:, P_GM:P_GM + MLA_WIDTH] = dgm_ref[...].astype(BF16)
        dproj_ref[:, P_HQ:P_HQ + HG_WIDTH] = dhq_ref[...].astype(BF16)
        dproj_ref[:, P_HF:P_HF + HG_WIDTH] = dhf_ref[...].astype(BF16)
        dproj_ref[:, P_HI:P_HI + HG_WIDTH] = dhi_ref[...].astype(BF16)
        dproj_ref[:, P_GH:P_GH + HG_WIDTH] = dgh_ref[...].astype(BF16)
        dproj_ref[:, P_QL:P_QL + Q_RANK] = dql.astype(BF16)
        dproj_ref[:, P_KVL:P_KVL + KV_RANK] = dkvl.astype(BF16)
        dproj_ref[:, P_KR:P_KR + LANES] = dkr.astype(BF16)
        dh = sum(_dot(dproj_ref[:, col:col + rows.shape[0]], rows) for col, rows in _in_proj_rows(win_ref))
        dg_ln, dxn = norm_bwd(x_ref[...], lng_ref[...], dh)
        dlng_ref[...] += dg_ln
        dx_ref[...] = dx2_ref[...] + dxn
        dwin_acc[...] += _dot_tn(dproj_ref[...], h_ref[...])

        @pl.when(pl.program_id(0) == steps - 1)
        def _():
            kr = P_KR + ROPE_LO
            moves = [((P_GM, P_QL), R_MAIN), ((P_QL, P_KR), (R_QL[0], R_KVL[1])), ((kr, kr + ROPE), R_KR)]
            copies = [pltpu.make_async_copy(dwin_acc.at[a:b, :], dwin_ref.at[c:d, :], sem.at[n])
                      for n, ((a, b), (c, d)) in enumerate(moves)]
            for cp in copies:
                cp.start()
            for cp in copies:
                cp.wait()

    def row(w, cb=0):
        return pl.BlockSpec((tm, w), lambda i: (i, cb))

    hl = N_HEADS * HEAD_LANES
    outs = [(D_MODEL, F32)]
    small = [(1, D_MODEL), (1, Q_RANK), (1, KV_RANK), (Q_RANK, hl), (KV_RANK, hl + MLA_WIDTH)]
    return pl.pallas_call(
        body, name="bot", grid=(steps,),
        in_specs=[row(D_MODEL), row(D_MODEL), row(lat_w, P_QL // lat_w), row(D_MODEL), row(Q_RANK), row(KV_RANK),
                  row(hl), row(hl), row(MLA_WIDTH),
                  row(MLA_WIDTH), row(HG_WIDTH), row(HG_WIDTH), row(HG_WIDTH), row(HG_WIDTH),
                  row(LANES), row(LANES), row(LANES),
                  _full((D_IN, D_MODEL)), _full((Q_RANK, hl)), _full((KV_RANK, hl + MLA_WIDTH)),
                  _full((1, D_MODEL)), _full((1, Q_RANK)), _full((1, KV_RANK))],
        out_specs=[row(w) for w, _ in outs] + [_full(s) for s in small] + [pl.BlockSpec(memory_space=pl.ANY)],
        out_shape=[jax.ShapeDtypeStruct((T, w), dt) for w, dt in outs] + [jax.ShapeDtypeStruct(s, F32) for s in small]
        + [jax.ShapeDtypeStruct((D_IN, D_MODEL), F32)],
        scratch_shapes=[pltpu.VMEM((tm, hl), BF16), pltpu.VMEM((tm, hl + MLA_WIDTH), BF16),
                        pltpu.VMEM((tm, D_PERM), BF16), pltpu.VMEM((D_PERM, D_MODEL), F32),
                        pltpu.SemaphoreType.DMA((3,))],
        compiler_params=_params(1),
    )(x, dx2, proj, h, qn, kvn, dq, dk, dv, dgm, dhq, dhf, dhi, dgh, c_t, s1_t, s2_t, w_in_t, w_q_p, w_kv_p, ln_g,
      q_g, kv_g)


RS_ROWS = 256


def _reduce_scatter(slabs, small):
    n = len(slabs)
    units = []
    for a, s in enumerate(slabs):
        rows, cols = s.shape[1:]
        if rows % RS_ROWS == 0 or rows < RS_ROWS:
            units += [(a, (pl.ds(r0, min(rows, RS_ROWS)), slice(None))) for r0 in range(0, rows, RS_ROWS)]
        else:
            units += [(a, (slice(None), pl.ds(c0, RS_ROWS))) for c0 in range(0, cols, RS_ROWS)]
    nu = len(units)

    def body(*refs):
        ins, small_ref = refs[:n], refs[n]
        outs, small_out = refs[n + 1:2 * n + 1], refs[2 * n + 1]
        own, sib_land, ici_out, ici_land = (refs[(2 + g) * n + 2:(3 + g) * n + 2] for g in range(4))
        small_land = refs[6 * n + 2]
        loc_sems, d2d_send, d2d_recv, ici_send, ici_recv, sm_send, sm_recv = refs[6 * n + 3:6 * n + 10]
        x, y, c = lax.axis_index("x"), lax.axis_index("y"), lax.axis_index("c")
        me = 4 * x + 2 * y + c

        def chip(k):
            return (1 - x if k & 2 else x, 1 - y if k & 1 else y)

        def block(k, core):
            px, py = chip(k)
            return 4 * px + 2 * py + core

        def part(u):
            return units[u]

        def local(u, k):
            a, rows = part(u)
            return pltpu.make_async_copy(ins[a].at[(block(k, c),) + rows], own[a].at[(k,) + rows], loc_sems.at[u, k])

        def to_sibling(u, k):
            a, rows = part(u)
            return pltpu.make_async_remote_copy(
                src_ref=ins[a].at[(block(k, 1 - c),) + rows], dst_ref=sib_land[a].at[(k,) + rows],
                send_sem=d2d_send.at[u, k], recv_sem=d2d_recv.at[u, k], device_id=(x, y, 1 - c), device_id_type=MESH)

        def to_chip(u, k):
            a, rows = part(u)
            return pltpu.make_async_remote_copy(
                src_ref=ici_out[a].at[(k - 1,) + rows], dst_ref=ici_land[a].at[(k - 1,) + rows],
                send_sem=ici_send.at[u, k - 1], recv_sem=ici_recv.at[u, k - 1], device_id=(*chip(k), c),
                device_id_type=MESH)

        def small_copy(k, receiving):
            px, py = chip(k >> 1)
            pc = 1 - c if k & 1 else c
            slot = 4 * px + 2 * py + pc if receiving else me
            return pltpu.make_async_remote_copy(
                src_ref=small_ref, dst_ref=small_land.at[slot], send_sem=sm_send.at[k - 1], recv_sem=sm_recv.at[k - 1],
                device_id=(px, py, pc), device_id_type=MESH)

        for u in range(nu):
            for k in range(4):
                local(u, k).start()
        for u in range(nu):
            for k in range(4):
                to_sibling(u, k).start()
        small_land[me] = small_ref[...]
        for k in range(1, N_DEV):
            small_copy(k, False).start()
        for u in range(nu):
            a, rows = part(u)
            for k in range(4):
                local(u, k).wait()
                to_sibling(u, k).wait_recv()
            for k in range(1, 4):
                ici_out[a][(k - 1,) + rows] = (own[a][(k,) + rows] + sib_land[a][(k,) + rows]).astype(BF16)
                to_chip(u, k).start()
        for u in range(nu):
            a, rows = part(u)
            acc = own[a][(0,) + rows] + sib_land[a][(0,) + rows]
            for k in range(1, 4):
                to_chip(u, k).wait_recv()
                acc = acc + ici_land[a][(k - 1,) + rows].astype(F32)
            outs[a][rows] = acc
        for k in range(1, N_DEV):
            small_copy(k, True).wait_recv()
        acc = small_land[0]
        for d in range(1, N_DEV):
            acc = acc + small_land[d]
        small_out[...] = acc
        for u in range(nu):
            for k in range(4):
                to_sibling(u, k).wait_send()
            for k in range(1, 4):
                to_chip(u, k).wait_send()
        for k in range(1, N_DEV):
            small_copy(k, False).wait_send()

    vm = pl.BlockSpec(memory_space=pltpu.VMEM)
    hbm = pl.BlockSpec(memory_space=pl.ANY)
    dma = pltpu.SemaphoreType.DMA
    return pl.pallas_call(
        body, name="reduce_scatter_grads",
        in_specs=[hbm] * n + [vm], out_specs=[vm] * (n + 1),
        out_shape=[jax.ShapeDtypeStruct(s.shape[1:], F32) for s in slabs] + [jax.ShapeDtypeStruct(small.shape, F32)],
        scratch_shapes=[pltpu.VMEM((4,) + s.shape[1:], F32) for s in slabs] * 2
        + [pltpu.VMEM((3,) + s.shape[1:], BF16) for s in slabs] * 2
        + [pltpu.VMEM((N_DEV,) + small.shape, F32)]
        + [dma((nu, 4)), dma((nu, 4)), dma((nu, 4)), dma((nu, 3)), dma((nu, 3)), dma((N_DEV - 1,)), dma((N_DEV - 1,))],
        compiler_params=pltpu.CompilerParams(vmem_limit_bytes=VMEM_LIMIT),
    )(*slabs, small)


def _adamw_math(w, g, m, v):
    m = ADAM_B1 * m + (1.0 - ADAM_B1) * g
    v = ADAM_B2 * v + (1.0 - ADAM_B2) * (g * g)
    m_hat = m / (1.0 - ADAM_B1 ** ADAM_STEP)
    v_hat = v / (1.0 - ADAM_B2 ** ADAM_STEP)
    delta = -ADAM_LR * (m_hat / (jnp.sqrt(v_hat) + ADAM_EPS) + ADAM_WD * w)
    return delta, m, v


SMALL_W = 512


def _adamw(big, small_w, small_g):
    nb, ns = len(big), len(small_w)

    def body(*refs):
        k = 0
        big_in = [refs[4 * i:4 * i + 4] for i in range(nb)]
        k = 4 * nb
        small_in = [refs[k + 3 * i:k + 3 * i + 3] for i in range(ns)]
        k += 3 * ns
        sg_ref = refs[k]
        k += 1
        big_out = [refs[k + 3 * i:k + 3 * i + 3] for i in range(nb)]
        k += 3 * nb
        small_out = [refs[k + 4 * i:k + 4 * i + 4] for i in range(ns)]

        for (w, g, m, v), (od, om, ov) in zip(big_in, big_out):
            od[...], om[...], ov[...] = _adamw_math(w[...], g[...], m[...], v[...])

        sg = sg_ref[...]
        lbp = small_in[2][0][...]
        lb = _lower_bound(lbp)
        t = sg[4:5, :] * lb * (1.0 - lb)
        grads = [jnp.concatenate([sg[0:1, :], sg[1:2, :]], axis=1),
                 jnp.concatenate([sg[2:3, :], sg[3:4, :]], axis=1),
                 jnp.concatenate([t, -t], axis=0),
                 sg[6:7, :], sg[7:8, 0:Q_RANK], sg[7:8, Q_RANK:Q_RANK + KV_RANK]]
        for (w, m, v), g, (og, od, om, ov) in zip(small_in, grads, small_out):
            og[...] = g
            od[...], om[...], ov[...] = _adamw_math(w[...], g, m[...], v[...])

    ins = [a for grp in big for a in grp] + [a for grp in small_w for a in grp] + [small_g]
    out_shape = ([jax.ShapeDtypeStruct(grp[0].shape, F32) for grp in big for _ in range(3)]
                 + [jax.ShapeDtypeStruct(grp[0].shape, F32) for grp in small_w for _ in range(4)])
    vm = pl.BlockSpec(memory_space=pltpu.VMEM)
    res = pl.pallas_call(
        body, name="adamw", in_specs=[vm] * len(ins), out_specs=[vm] * len(out_shape), out_shape=out_shape,
        compiler_params=pltpu.CompilerParams(vmem_limit_bytes=VMEM_LIMIT),
    )(*ins)
    big_res = [res[3 * i:3 * i + 3] for i in range(nb)]
    small_res = [res[3 * nb + 4 * i:3 * nb + 4 * i + 4] for i in range(ns)]
    return big_res, small_res


def _perm_weights(g_in_t, g_q, g_kv, g_out):
    w_in_t = g_in_t.reshape(D_IN, D_MODEL)
    wq = g_q.transpose(1, 0, 2)
    w_q_p = jnp.pad(wq, ((0, 0), (0, 0), (0, HEAD_LANES - NOPE - ROPE))).reshape(Q_RANK, N_HEADS * HEAD_LANES)
    wkv = g_kv.transpose(1, 0, 2)
    wk = jnp.pad(wkv[:, :, :NOPE], ((0, 0), (0, 0), (0, HEAD_LANES - NOPE))).reshape(KV_RANK, N_HEADS * HEAD_LANES)
    wv = wkv[:, :, NOPE:].reshape(KV_RANK, MLA_WIDTH)
    return w_in_t, w_q_p, jnp.concatenate([wk, wv], axis=1), g_out.reshape(D_MODEL, D_MODEL)


def _grad_slabs(dw_in_t, dw_q_p, dw_kv_p, dw_out):
    s_in = dw_in_t.reshape(N_DEV, D_IN // N_DEV, D_MODEL)
    s_q = dw_q_p.reshape(Q_RANK, N_HEADS, HEAD_LANES)[:, :, :NOPE + ROPE].transpose(1, 0, 2)
    hl = N_HEADS * HEAD_LANES
    dk = dw_kv_p[:, :hl].reshape(KV_RANK, N_HEADS, HEAD_LANES)[:, :, :NOPE]
    dv = dw_kv_p[:, hl:].reshape(KV_RANK, N_HEADS, V_DIM)
    s_kv = jnp.concatenate([dk, dv], axis=2).transpose(1, 0, 2)
    return s_in, s_q, s_kv, dw_out.reshape(N_DEV, D_MODEL // N_DEV, D_MODEL)


def _block_sizes(T):
    return min(256, T), min(256, T), min(512, T)


def kernel(x, positions, ln_g, w_in, q_a_norm_g, w_q_b, kv_a_norm_g, w_kv_b, hg_lower_bounds, hg_norm_g, w_out, final_norm_g, loss_target, m_ln_g, m_w_in, m_q_a_norm_g, m_w_q_b, m_kv_a_norm_g, m_w_kv_b, m_hg_lower_bounds, m_hg_norm_g, m_w_out, m_final_norm_g, v_ln_g, v_w_in, v_q_a_norm_g, v_w_q_b, v_kv_a_norm_g, v_w_kv_b, v_hg_lower_bounds, v_hg_norm_g, v_w_out, v_final_norm_g):
    T = x.shape[1]
    tm, tq, bt = _block_sizes(T)
    nq = T // tq
    xs, tgt = x[0], loss_target[0]
    pos_f = positions.astype(F32)
    fng = final_norm_g.reshape(1, D_MODEL)

    w_in_shard_t = w_in[0].T
    gathered, (c_t, s1_t, s2_t) = _all_gather_weights([w_in_shard_t, w_q_b[0], w_kv_b[0], w_out[0]], pos_f)
    w_in_t, w_q_p, w_kv_p, w_out_b = _perm_weights(*gathered)

    proj, h, qn, kvn, q, k, v, v_t, o_hg, states = _fwd_in(
        xs, ln_g, w_in_t, q_a_norm_g, w_q_p, kv_a_norm_g, w_kv_p, c_t, s1_t, s2_t, hg_lower_bounds, bt, tq)
    hl = N_HEADS * HEAD_LANES
    k_t = k.reshape(nq, tq, hl).transpose(0, 2, 1)
    q_t = q.reshape(nq, tq, hl).transpose(0, 2, 1)
    o_mla, o_t, lse = _attn_fwd_flat(k, q_t, v_t, tq)
    dx2, do_t, d_gm, d_oh, d_gh, loss_p, d_fng, d_hgn, dw_out = _top(
        xs, tgt, o_mla, o_hg, proj, w_out_b, hg_norm_g, fng, tm, tq)
    dq_t, dk, dv = _attn_bwd_flat(k, v, q_t, k_t, do_t, o_t, lse, tq)
    dq = dq_t.transpose(0, 2, 1).reshape(T, N_HEADS * HEAD_LANES)
    d_hq, d_hf, d_hi, d_lb = _hgrn_bwd(proj, hg_lower_bounds, d_oh, states)
    dx, d_lng, d_qg, d_kvg, dw_q_p, dw_kv_p, dw_in_t = _bot(
        xs, dx2, proj, h, qn, kvn, dq, dk, dv, d_gm, d_hq, d_hf, d_hi, d_gh, c_t, s1_t, s2_t, w_in_t, w_q_p, w_kv_p,
        ln_g, q_a_norm_g, kv_a_norm_g, tm)

    small = jnp.concatenate([
        d_lng.reshape(2, SMALL_W), d_fng.reshape(2, SMALL_W), d_lb, loss_p, d_hgn,
        jnp.concatenate([d_qg, d_kvg, jnp.zeros((1, SMALL_W - Q_RANK - KV_RANK), F32)], axis=1)], axis=0)
    g_in, g_q, g_kv, g_out, small_sum = _reduce_scatter(list(_grad_slabs(dw_in_t, dw_q_p, dw_kv_p, dw_out)), small)

    big = [(w_in_shard_t, g_in, m_w_in[0].T, v_w_in[0].T), (w_q_b[0], g_q, m_w_q_b[0], v_w_q_b[0]),
           (w_kv_b[0], g_kv, m_w_kv_b[0], v_w_kv_b[0]), (w_out[0], g_out, m_w_out[0], v_w_out[0])]
    small_w = [(ln_g, m_ln_g, v_ln_g),
               (fng, m_final_norm_g.reshape(1, D_MODEL), v_final_norm_g.reshape(1, D_MODEL)),
               (hg_lower_bounds, m_hg_lower_bounds, v_hg_lower_bounds), (hg_norm_g, m_hg_norm_g, v_hg_norm_g),
               (q_a_norm_g, m_q_a_norm_g, v_q_a_norm_g), (kv_a_norm_g, m_kv_a_norm_g, v_kv_a_norm_g)]
    big_res, small_res = _adamw(big, small_w, small_sum)

    loss = small_sum[5, 0]
    (r_in, r_q, r_kv, r_out) = big_res
    (s_ln, s_fn, s_lb, s_hgn, s_qg, s_kvg) = small_res
    flat = lambda t: t.reshape(D_MODEL)
    lead = lambda t: t[None]
    grads = [s_ln[0], lead(g_in.T), s_qg[0], lead(g_q), s_kvg[0], lead(g_kv), s_lb[0], s_hgn[0], lead(g_out), flat(s_fn[0])]

    def pick(i):
        return [s_ln[i + 1], lead(r_in[i].T), s_qg[i + 1], lead(r_q[i]), s_kvg[i + 1], lead(r_kv[i]), s_lb[i + 1],
                s_hgn[i + 1], lead(r_out[i]), flat(s_fn[i + 1])]

    return (loss, dx[None], *grads, *pick(0), *pick(1), *pick(2))
```
